```python
import math
import jax, jax.numpy as jnp
from jax import lax
import numpy as np

D_MODEL = 1024
BATCH = 8
SEQ = 8192
DEPTH = 4

N_MIXERS = 2
N_S5_LAYERS = (DEPTH + 1) // 2
N_ATTN_LAYERS = DEPTH // 2
N_META = 16
GRID_W = 64
HEAD_DIM = 64
N_Q_HEADS = D_MODEL // HEAD_DIM
N_KV_HEADS = N_Q_HEADS // 4
Q_PER_KV = N_Q_HEADS // N_KV_HEADS
QKV_WIDTH = (N_Q_HEADS + 2 * N_KV_HEADS) * HEAD_DIM
QUERY_BLOCK = 128
ROPE_THETA = 10000.0
ROPE_AXIS_DIM = HEAD_DIM // 2
QK_EPS = 1e-6
S5_GROUP_CH = 16
S5_GROUPS = D_MODEL // S5_GROUP_CH
S5_STATE = 64
S5_DT_MIN = 1e-3
S5_DT_MAX = 1e-1
D_FF = -(-8 * D_MODEL // (3 * 256)) * 256
LN_EPS = 1e-5
DEEPNORM_ALPHA = (2.0 * DEPTH) ** 0.25
DEEPNORM_BETA = (8.0 * DEPTH) ** -0.25

kernel_name = "hybrid_s5_gqa_deepnorm_encoder"


def layer_norm(x, gain, bias):
    xf = x.astype(jnp.float32)
    mean = jnp.mean(xf, axis=-1, keepdims=True)
    var = jnp.mean(jnp.square(xf - mean), axis=-1, keepdims=True)
    y = (xf - mean) * lax.rsqrt(var + LN_EPS) * gain.astype(jnp.float32) + bias.astype(jnp.float32)
    return y.astype(x.dtype)


def swiglu_ffn(h, w_gate, w_up, w_down):
    return (jax.nn.silu(h @ w_gate) * (h @ w_up)) @ w_down


def _s5_combine(left, right):
    ar1, ai1, br1, bi1 = left
    ar2, ai2, br2, bi2 = right
    ar = ar1 * ar2 - ai1 * ai2
    ai = ar1 * ai2 + ai1 * ar2
    br = ar2 * br1 - ai2 * bi1 + br2
    bi = ar2 * bi1 + ai2 * br1 + bi2
    return (ar, ai, br, bi)


def s5_direction(u, lam_re, lam_im, log_dt, b_re, b_im, c_re, c_im, reverse):
    L = u.shape[1]
    lr = lam_re.astype(jnp.float32)
    li = lam_im.astype(jnp.float32)
    dt = jnp.exp(log_dt.astype(jnp.float32))[:, None]
    mag = jnp.exp(lr * dt)
    abr = mag * jnp.cos(li * dt)
    abi = mag * jnp.sin(li * dt)
    nr, ni = abr - 1.0, abi
    den = lr * lr + li * li
    cr = (nr * lr + ni * li) / den
    ci = (ni * lr - nr * li) / den
    br = b_re.astype(jnp.float32)
    bi = b_im.astype(jnp.float32)
    bbr = cr[..., None] * br - ci[..., None] * bi
    bbi = cr[..., None] * bi + ci[..., None] * br
    bu_r = jnp.einsum('blgc,gpc->blgp', u, bbr)
    bu_i = jnp.einsum('blgc,gpc->blgp', u, bbi)
    a_r = jnp.broadcast_to(abr[None, None], (1, L) + abr.shape)
    a_i = jnp.broadcast_to(abi[None, None], (1, L) + abi.shape)
    _, _, sr, si = lax.associative_scan(_s5_combine, (a_r, a_i, bu_r, bu_i),
                                        reverse=reverse, axis=1)
    return (jnp.einsum('blgp,gcp->blgc', sr, c_re.astype(jnp.float32))
            - jnp.einsum('blgp,gcp->blgc', si, c_im.astype(jnp.float32)))


def s5_mixer(h, lam_re, lam_im, log_dt, b_re, b_im, c_re, c_im, d_skip, w_glu, w_out):
    bsz, L, _ = h.shape
    hf = h.astype(jnp.float32)
    u = hf.reshape(bsz, L, S5_GROUPS, S5_GROUP_CH)
    y = (s5_direction(u, lam_re[0], lam_im[0], log_dt[0], b_re[0], b_im[0], c_re[0], c_im[0], False)
         + s5_direction(u, lam_re[1], lam_im[1], log_dt[1], b_re[1], b_im[1], c_re[1], c_im[1], True))
    y = y.reshape(bsz, L, D_MODEL) + d_skip.astype(jnp.float32) * hf
    g = jax.nn.gelu(y, approximate=False).astype(h.dtype)
    z = g * jax.nn.sigmoid(g @ w_glu)
    return z @ w_out


def axial_rope_tables(n_real):
    rows = n_real // GRID_W
    row_ids = jnp.repeat(jnp.arange(rows, dtype=jnp.int32), GRID_W)
    col_ids = jnp.tile(jnp.arange(GRID_W, dtype=jnp.int32), rows)
    pad = jnp.zeros((N_META,), jnp.int32)
    row_ids = jnp.concatenate([pad, row_ids]).astype(jnp.float32)
    col_ids = jnp.concatenate([pad, col_ids]).astype(jnp.float32)
    inv_freq = ROPE_THETA ** (-jnp.arange(0, ROPE_AXIS_DIM, 2, dtype=jnp.float32) / ROPE_AXIS_DIM)
    ang_r = row_ids[:, None] * inv_freq[None, :]
    ang_c = col_ids[:, None] * inv_freq[None, :]
    return jnp.cos(ang_r), jnp.sin(ang_r), jnp.cos(ang_c), jnp.sin(ang_c)


def _rotate_half_block(xh, cos, sin):
    half = ROPE_AXIS_DIM // 2
    x1, x2 = xh[..., :half], xh[..., half:]
    c = cos[None, :, None, :]
    s = sin[None, :, None, :]
    return jnp.concatenate([x1 * c - x2 * s, x2 * c + x1 * s], axis=-1)


def rms_axial_rope(t, gain, cos_r, sin_r, cos_c, sin_c):
    tf = t.astype(jnp.float32)
    tf = tf * lax.rsqrt(jnp.mean(jnp.square(tf), axis=-1, keepdims=True) + QK_EPS) * gain.astype(jnp.float32)
    return jnp.concatenate([_rotate_half_block(tf[..., :ROPE_AXIS_DIM], cos_r, sin_r),
                            _rotate_half_block(tf[..., ROPE_AXIS_DIM:], cos_c, sin_c)], axis=-1)


def _attend(q, k, v):
    s = jnp.einsum('bqkgd,bskd->bkgqs', q, k).astype(jnp.float32) * (HEAD_DIM ** -0.5)
    p = jax.nn.softmax(s, axis=-1).astype(v.dtype)
    return jnp.einsum('bkgqs,bskd->bqkgd', p, v)


def gqa_mixer(h, w_qkv, q_gain, k_gain, w_out, cos_r, sin_r, cos_c, sin_c):
    bsz, L, _ = h.shape
    qkv = h @ w_qkv
    nq, nk = N_Q_HEADS * HEAD_DIM, N_KV_HEADS * HEAD_DIM
    q = qkv[..., :nq].reshape(bsz, L, N_Q_HEADS, HEAD_DIM)
    k = qkv[..., nq:nq + nk].reshape(bsz, L, N_KV_HEADS, HEAD_DIM)
    v = qkv[..., nq + nk:].reshape(bsz, L, N_KV_HEADS, HEAD_DIM)
    q = rms_axial_rope(q, q_gain, cos_r, sin_r, cos_c, sin_c).astype(v.dtype)
    k = rms_axial_rope(k, k_gain, cos_r, sin_r, cos_c, sin_c).astype(v.dtype)
    q = q.reshape(bsz, L, N_KV_HEADS, Q_PER_KV, HEAD_DIM)
    out_meta = _attend(q[:, :N_META], k, v)
    n_real = L - N_META
    n_blk = n_real // QUERY_BLOCK
    qb = q[:, N_META:].reshape(bsz, n_blk, QUERY_BLOCK, N_KV_HEADS, Q_PER_KV, HEAD_DIM).swapaxes(0, 1)
    out_real = lax.map(lambda blk: _attend(blk, k, v), qb)
    out_real = out_real.swapaxes(0, 1).reshape(bsz, n_real, N_KV_HEADS, Q_PER_KV, HEAD_DIM)
    out = jnp.concatenate([out_meta, out_real], axis=1).reshape(bsz, L, D_MODEL)
    return out @ w_out


def _fwd_setup_inputs(seed: int = 0) -> dict:
    key = jax.random.key(seed)
    ks = jax.random.split(key, 24)
    f32 = jnp.float32
    D, G, P, C = D_MODEL, S5_GROUPS, S5_STATE, S5_GROUP_CH
    nS, nA = N_S5_LAYERS, N_ATTN_LAYERS
    x = jax.random.normal(ks[0], (BATCH, SEQ, D), f32)
    meta_tokens = jax.random.normal(ks[1], (N_META, D), f32)
    s5_lambda_re = -0.5 * jnp.exp(0.02 * jax.random.normal(ks[2], (nS, 2, G, P), f32))
    s5_lambda_im = (math.pi * jnp.arange(P, dtype=f32))[None, None, None, :] \
        + 0.01 * jax.random.normal(ks[3], (nS, 2, G, P), f32)
    s5_log_dt = jax.random.uniform(ks[4], (nS, 2, G), f32,
                                   minval=math.log(S5_DT_MIN), maxval=math.log(S5_DT_MAX))
    b_scale = (2.0 * C) ** -0.5
    s5_b_re = jax.random.normal(ks[5], (nS, 2, G, P, C), f32) * b_scale
    s5_b_im = jax.random.normal(ks[6], (nS, 2, G, P, C), f32) * b_scale
    c_scale = (2.0 * P) ** -0.5
    s5_c_re = jax.random.normal(ks[7], (nS, 2, G, C, P), f32) * c_scale
    s5_c_im = jax.random.normal(ks[8], (nS, 2, G, C, P), f32) * c_scale
    s5_d = jax.random.normal(ks[9], (nS, D), f32)
    s5_w_glu = jax.random.normal(ks[10], (nS, D, D), f32) * D ** -0.5
    s5_w_out = jax.random.normal(ks[11], (nS, D, D), f32) * (D ** -0.5 * DEEPNORM_BETA)
    w_qk = jax.random.normal(ks[12], (nA, D, (N_Q_HEADS + N_KV_HEADS) * HEAD_DIM), f32) * D ** -0.5
    w_v = jax.random.normal(ks[13], (nA, D, N_KV_HEADS * HEAD_DIM), f32) * (D ** -0.5 * DEEPNORM_BETA)
    attn_w_qkv = jnp.concatenate([w_qk, w_v], axis=-1)
    attn_q_gain = 1.0 + 0.02 * jax.random.normal(ks[14], (nA, HEAD_DIM), f32)
    attn_k_gain = 1.0 + 0.02 * jax.random.normal(ks[15], (nA, HEAD_DIM), f32)
    attn_w_out = jax.random.normal(ks[16], (nA, D, D), f32) * (D ** -0.5 * DEEPNORM_BETA)
    ffn_w_gate = jax.random.normal(ks[17], (DEPTH, D, D_FF), f32) * D ** -0.5
    ffn_w_up = jax.random.normal(ks[18], (DEPTH, D, D_FF), f32) * D ** -0.5
    ffn_w_down = jax.random.normal(ks[19], (DEPTH, D_FF, D), f32) * (D_FF ** -0.5 * DEEPNORM_BETA)
    ln_gain = 1.0 + 0.02 * jax.random.normal(ks[20], (DEPTH, 2, D), f32)
    ln_bias = 0.02 * jax.random.normal(ks[21], (DEPTH, 2, D), f32)
    return {"x": x, "meta_tokens": meta_tokens,
            "s5_lambda_re": s5_lambda_re, "s5_lambda_im": s5_lambda_im, "s5_log_dt": s5_log_dt,
            "s5_b_re": s5_b_re, "s5_b_im": s5_b_im, "s5_c_re": s5_c_re, "s5_c_im": s5_c_im,
            "s5_d": s5_d, "s5_w_glu": s5_w_glu, "s5_w_out": s5_w_out,
            "attn_w_qkv": attn_w_qkv, "attn_q_gain": attn_q_gain, "attn_k_gain": attn_k_gain,
            "attn_w_out": attn_w_out,
            "ffn_w_gate": ffn_w_gate, "ffn_w_up": ffn_w_up, "ffn_w_down": ffn_w_down,
            "ln_gain": ln_gain, "ln_bias": ln_bias}


def _fwd_reference(x, meta_tokens, s5_lambda_re, s5_lambda_im, s5_log_dt, s5_b_re, s5_b_im,
              s5_c_re, s5_c_im, s5_d, s5_w_glu, s5_w_out, attn_w_qkv, attn_q_gain,
              attn_k_gain, attn_w_out, ffn_w_gate, ffn_w_up, ffn_w_down, ln_gain, ln_bias):
    bsz, n_real, d = x.shape
    meta = jnp.broadcast_to(meta_tokens.astype(x.dtype)[None], (bsz, N_META, d))
    h = jnp.concatenate([meta, x], axis=1)
    cos_r, sin_r, cos_c, sin_c = axial_rope_tables(n_real)
    for i in range(DEPTH):
        j = i // N_MIXERS
        if i % N_MIXERS == 0:
            mix = s5_mixer(h, s5_lambda_re[j], s5_lambda_im[j], s5_log_dt[j], s5_b_re[j], s5_b_im[j],
                           s5_c_re[j], s5_c_im[j], s5_d[j], s5_w_glu[j], s5_w_out[j])
        else:
            mix = gqa_mixer(h, attn_w_qkv[j], attn_q_gain[j], attn_k_gain[j], attn_w_out[j],
                            cos_r, sin_r, cos_c, sin_c)
        h = layer_norm(DEEPNORM_ALPHA * h + mix, ln_gain[i, 0], ln_bias[i, 0])
        h = layer_norm(DEEPNORM_ALPHA * h + swiglu_ffn(h, ffn_w_gate[i], ffn_w_up[i], ffn_w_down[i]),
                       ln_gain[i, 1], ln_bias[i, 1])
    return h[:, N_META:]


import jax as _jax
import jax.numpy as _jnp

TWIN_FORMAT = 'train_step'
FWD_PARAMS = ['x', 'meta_tokens', 's5_lambda_re', 's5_lambda_im', 's5_log_dt', 's5_b_re', 's5_b_im', 's5_c_re', 's5_c_im', 's5_d', 's5_w_glu', 's5_w_out', 'attn_w_qkv', 'attn_q_gain', 'attn_k_gain', 'attn_w_out', 'ffn_w_gate', 'ffn_w_up', 'ffn_w_down', 'ln_gain', 'ln_bias']
TWIN_WEIGHTS = ['meta_tokens', 's5_lambda_re', 's5_lambda_im', 's5_log_dt', 's5_b_re', 's5_b_im', 's5_c_re', 's5_c_im', 's5_d', 's5_w_glu', 's5_w_out', 'attn_w_qkv', 'attn_q_gain', 'attn_k_gain', 'attn_w_out', 'ffn_w_gate', 'ffn_w_up', 'ffn_w_down', 'ln_gain', 'ln_bias']
TWIN_DIFF_INPUT = 'x'
TWIN_INPUTS = ['x', 'meta_tokens', 's5_lambda_re', 's5_lambda_im', 's5_log_dt', 's5_b_re', 's5_b_im', 's5_c_re', 's5_c_im', 's5_d', 's5_w_glu', 's5_w_out', 'attn_w_qkv', 'attn_q_gain', 'attn_k_gain', 'attn_w_out', 'ffn_w_gate', 'ffn_w_up', 'ffn_w_down', 'ln_gain', 'ln_bias', 'loss_target', 'm_meta_tokens', 'm_s5_lambda_re', 'm_s5_lambda_im', 'm_s5_log_dt', 'm_s5_b_re', 'm_s5_b_im', 'm_s5_c_re', 'm_s5_c_im', 'm_s5_d', 'm_s5_w_glu', 'm_s5_w_out', 'm_attn_w_qkv', 'm_attn_q_gain', 'm_attn_k_gain', 'm_attn_w_out', 'm_ffn_w_gate', 'm_ffn_w_up', 'm_ffn_w_down', 'm_ln_gain', 'm_ln_bias', 'v_meta_tokens', 'v_s5_lambda_re', 'v_s5_lambda_im', 'v_s5_log_dt', 'v_s5_b_re', 'v_s5_b_im', 'v_s5_c_re', 'v_s5_c_im', 'v_s5_d', 'v_s5_w_glu', 'v_s5_w_out', 'v_attn_w_qkv', 'v_attn_q_gain', 'v_attn_k_gain', 'v_attn_w_out', 'v_ffn_w_gate', 'v_ffn_w_up', 'v_ffn_w_down', 'v_ln_gain', 'v_ln_bias']
TWIN_OUTPUTS = ['loss', 'grad_x', 'grad_meta_tokens', 'grad_s5_lambda_re', 'grad_s5_lambda_im', 'grad_s5_log_dt', 'grad_s5_b_re', 'grad_s5_b_im', 'grad_s5_c_re', 'grad_s5_c_im', 'grad_s5_d', 'grad_s5_w_glu', 'grad_s5_w_out', 'grad_attn_w_qkv', 'grad_attn_q_gain', 'grad_attn_k_gain', 'grad_attn_w_out', 'grad_ffn_w_gate', 'grad_ffn_w_up', 'grad_ffn_w_down', 'grad_ln_gain', 'grad_ln_bias', 'delta_meta_tokens', 'delta_s5_lambda_re', 'delta_s5_lambda_im', 'delta_s5_log_dt', 'delta_s5_b_re', 'delta_s5_b_im', 'delta_s5_c_re', 'delta_s5_c_im', 'delta_s5_d', 'delta_s5_w_glu', 'delta_s5_w_out', 'delta_attn_w_qkv', 'delta_attn_q_gain', 'delta_attn_k_gain', 'delta_attn_w_out', 'delta_ffn_w_gate', 'delta_ffn_w_up', 'delta_ffn_w_down', 'delta_ln_gain', 'delta_ln_bias', 'new_m_meta_tokens', 'new_m_s5_lambda_re', 'new_m_s5_lambda_im', 'new_m_s5_log_dt', 'new_m_s5_b_re', 'new_m_s5_b_im', 'new_m_s5_c_re', 'new_m_s5_c_im', 'new_m_s5_d', 'new_m_s5_w_glu', 'new_m_s5_w_out', 'new_m_attn_w_qkv', 'new_m_attn_q_gain', 'new_m_attn_k_gain', 'new_m_attn_w_out', 'new_m_ffn_w_gate', 'new_m_ffn_w_up', 'new_m_ffn_w_down', 'new_m_ln_gain', 'new_m_ln_bias', 'new_v_meta_tokens', 'new_v_s5_lambda_re', 'new_v_s5_lambda_im', 'new_v_s5_log_dt', 'new_v_s5_b_re', 'new_v_s5_b_im', 'new_v_s5_c_re', 'new_v_s5_c_im', 'new_v_s5_d', 'new_v_s5_w_glu', 'new_v_s5_w_out', 'new_v_attn_w_qkv', 'new_v_attn_q_gain', 'new_v_attn_k_gain', 'new_v_attn_w_out', 'new_v_ffn_w_gate', 'new_v_ffn_w_up', 'new_v_ffn_w_down', 'new_v_ln_gain', 'new_v_ln_bias']
TWIN_LEAF_KINDS = {'loss': 'loss', 'grad_x': 'grad_x', 'grad_meta_tokens': 'grad_w', 'grad_s5_lambda_re': 'grad_w', 'grad_s5_lambda_im': 'grad_w', 'grad_s5_log_dt': 'grad_w', 'grad_s5_b_re': 'grad_w', 'grad_s5_b_im': 'grad_w', 'grad_s5_c_re': 'grad_w', 'grad_s5_c_im': 'grad_w', 'grad_s5_d': 'grad_w', 'grad_s5_w_glu': 'grad_w', 'grad_s5_w_out': 'grad_w', 'grad_attn_w_qkv': 'grad_w', 'grad_attn_q_gain': 'grad_w', 'grad_attn_k_gain': 'grad_w', 'grad_attn_w_out': 'grad_w', 'grad_ffn_w_gate': 'grad_w', 'grad_ffn_w_up': 'grad_w', 'grad_ffn_w_down': 'grad_w', 'grad_ln_gain': 'grad_w', 'grad_ln_bias': 'grad_w', 'delta_meta_tokens': 'delta_w', 'delta_s5_lambda_re': 'delta_w', 'delta_s5_lambda_im': 'delta_w', 'delta_s5_log_dt': 'delta_w', 'delta_s5_b_re': 'delta_w', 'delta_s5_b_im': 'delta_w', 'delta_s5_c_re': 'delta_w', 'delta_s5_c_im': 'delta_w', 'delta_s5_d': 'delta_w', 'delta_s5_w_glu': 'delta_w', 'delta_s5_w_out': 'delta_w', 'delta_attn_w_qkv': 'delta_w', 'delta_attn_q_gain': 'delta_w', 'delta_attn_k_gain': 'delta_w', 'delta_attn_w_out': 'delta_w', 'delta_ffn_w_gate': 'delta_w', 'delta_ffn_w_up': 'delta_w', 'delta_ffn_w_down': 'delta_w', 'delta_ln_gain': 'delta_w', 'delta_ln_bias': 'delta_w', 'new_m_meta_tokens': 'new_m', 'new_m_s5_lambda_re': 'new_m', 'new_m_s5_lambda_im': 'new_m', 'new_m_s5_log_dt': 'new_m', 'new_m_s5_b_re': 'new_m', 'new_m_s5_b_im': 'new_m', 'new_m_s5_c_re': 'new_m', 'new_m_s5_c_im': 'new_m', 'new_m_s5_d': 'new_m', 'new_m_s5_w_glu': 'new_m', 'new_m_s5_w_out': 'new_m', 'new_m_attn_w_qkv': 'new_m', 'new_m_attn_q_gain': 'new_m', 'new_m_attn_k_gain': 'new_m', 'new_m_attn_w_out': 'new_m', 'new_m_ffn_w_gate': 'new_m', 'new_m_ffn_w_up': 'new_m', 'new_m_ffn_w_down': 'new_m', 'new_m_ln_gain': 'new_m', 'new_m_ln_bias': 'new_m', 'new_v_meta_tokens': 'new_v', 'new_v_s5_lambda_re': 'new_v', 'new_v_s5_lambda_im': 'new_v', 'new_v_s5_log_dt': 'new_v', 'new_v_s5_b_re': 'new_v', 'new_v_s5_b_im': 'new_v', 'new_v_s5_c_re': 'new_v', 'new_v_s5_c_im': 'new_v', 'new_v_s5_d': 'new_v', 'new_v_s5_w_glu': 'new_v', 'new_v_s5_w_out': 'new_v', 'new_v_attn_w_qkv': 'new_v', 'new_v_attn_q_gain': 'new_v', 'new_v_attn_k_gain': 'new_v', 'new_v_attn_w_out': 'new_v', 'new_v_ffn_w_gate': 'new_v', 'new_v_ffn_w_up': 'new_v', 'new_v_ffn_w_down': 'new_v', 'new_v_ln_gain': 'new_v', 'new_v_ln_bias': 'new_v'}


def _forward(args):
    return _fwd_reference(*[args[k] for k in FWD_PARAMS])


def _output_shape():
    def fwd():
        inp = _fwd_setup_inputs(0)
        return _fwd_reference(*[inp[k] for k in FWD_PARAMS])
    out = _jax.eval_shape(fwd)
    return out.shape, out.dtype

N_MICROBATCH = 1
ADAM_LR = 0.001
ADAM_B1 = 0.9
ADAM_B2 = 0.999
ADAM_EPS = 1e-08
ADAM_WD = 0.01
ADAM_STEP = 10
PER_EXAMPLE_BATCH_AXIS = {'x': 0, 'loss_target': 0}
SHARED_INPUTS = []
_WEIGHT_DTYPES = {'meta_tokens': _jnp.float32, 's5_lambda_re': _jnp.float32, 's5_lambda_im': _jnp.float32, 's5_log_dt': _jnp.float32, 's5_b_re': _jnp.float32, 's5_b_im': _jnp.float32, 's5_c_re': _jnp.float32, 's5_c_im': _jnp.float32, 's5_d': _jnp.float32, 's5_w_glu': _jnp.float32, 's5_w_out': _jnp.float32, 'attn_w_qkv': _jnp.float32, 'attn_q_gain': _jnp.float32, 'attn_k_gain': _jnp.float32, 'attn_w_out': _jnp.float32, 'ffn_w_gate': _jnp.float32, 'ffn_w_up': _jnp.float32, 'ffn_w_down': _jnp.float32, 'ln_gain': _jnp.float32, 'ln_bias': _jnp.float32}
MOMENT_SCALE = {'meta_tokens': 3.560464e-04, 's5_lambda_re': 1.385607e-03, 's5_lambda_im': 1.294615e-03, 's5_log_dt': 1.065702e+00, 's5_b_re': 8.003871e-04, 's5_b_im': 8.123092e-04, 's5_c_re': 1.616227e-03, 's5_c_im': 1.631288e-03, 's5_d': 4.275601e-02, 's5_w_glu': 8.338959e-03, 's5_w_out': 9.347702e-02, 'attn_w_qkv': 7.849054e-03, 'attn_q_gain': 1.071654e-02, 'attn_k_gain': 1.051232e-02, 'attn_w_out': 1.024453e-02, 'ffn_w_gate': 2.345569e-02, 'ffn_w_up': 2.280547e-02, 'ffn_w_down': 8.986518e-02, 'ln_gain': 2.285091e+01, 'ln_bias': 1.900878e+00}


def _to_microbatches(a, axis):
    t = _jnp.moveaxis(a, axis, 0)
    t = t.reshape((N_MICROBATCH, t.shape[0] // N_MICROBATCH) + t.shape[1:])
    return _jnp.moveaxis(t, 1, axis + 1)


def setup_inputs(seed: int = 0) -> dict:
    inp = _fwd_setup_inputs(seed)
    key = _jax.random.fold_in(_jax.random.key(seed), 7919)
    shape, _ = _output_shape()
    out = dict(inp)
    out["loss_target"] = _jax.random.normal(_jax.random.fold_in(key, 0), shape, _jnp.float32)
    for i, name in enumerate(TWIN_WEIGHTS):
        w = inp[name].astype(_jnp.float32)
        if MOMENT_SCALE is None:
            s = _jnp.sqrt(_jnp.mean(_jnp.square(w)) + 1e-30)
        else:
            s = MOMENT_SCALE[name]
        km, kv = _jax.random.split(_jax.random.fold_in(key, i + 1))
        out[name] = w
        out["m_" + name] = s * _jax.random.normal(km, w.shape, _jnp.float32)
        out["v_" + name] = (s * s) * _jax.random.uniform(kv, w.shape, _jnp.float32, 0.5, 1.5)
    if N_MICROBATCH > 1:
        for name, axis in PER_EXAMPLE_BATCH_AXIS.items():
            out[name] = _to_microbatches(out[name], axis)
    return {'x': out['x'], 'meta_tokens': out['meta_tokens'], 's5_lambda_re': out['s5_lambda_re'], 's5_lambda_im': out['s5_lambda_im'], 's5_log_dt': out['s5_log_dt'], 's5_b_re': out['s5_b_re'], 's5_b_im': out['s5_b_im'], 's5_c_re': out['s5_c_re'], 's5_c_im': out['s5_c_im'], 's5_d': out['s5_d'], 's5_w_glu': out['s5_w_glu'], 's5_w_out': out['s5_w_out'], 'attn_w_qkv': out['attn_w_qkv'], 'attn_q_gain': out['attn_q_gain'], 'attn_k_gain': out['attn_k_gain'], 'attn_w_out': out['attn_w_out'], 'ffn_w_gate': out['ffn_w_gate'], 'ffn_w_up': out['ffn_w_up'], 'ffn_w_down': out['ffn_w_down'], 'ln_gain': out['ln_gain'], 'ln_bias': out['ln_bias'], 'loss_target': out['loss_target'], 'm_meta_tokens': out['m_meta_tokens'], 'm_s5_lambda_re': out['m_s5_lambda_re'], 'm_s5_lambda_im': out['m_s5_lambda_im'], 'm_s5_log_dt': out['m_s5_log_dt'], 'm_s5_b_re': out['m_s5_b_re'], 'm_s5_b_im': out['m_s5_b_im'], 'm_s5_c_re': out['m_s5_c_re'], 'm_s5_c_im': out['m_s5_c_im'], 'm_s5_d': out['m_s5_d'], 'm_s5_w_glu': out['m_s5_w_glu'], 'm_s5_w_out': out['m_s5_w_out'], 'm_attn_w_qkv': out['m_attn_w_qkv'], 'm_attn_q_gain': out['m_attn_q_gain'], 'm_attn_k_gain': out['m_attn_k_gain'], 'm_attn_w_out': out['m_attn_w_out'], 'm_ffn_w_gate': out['m_ffn_w_gate'], 'm_ffn_w_up': out['m_ffn_w_up'], 'm_ffn_w_down': out['m_ffn_w_down'], 'm_ln_gain': out['m_ln_gain'], 'm_ln_bias': out['m_ln_bias'], 'v_meta_tokens': out['v_meta_tokens'], 'v_s5_lambda_re': out['v_s5_lambda_re'], 'v_s5_lambda_im': out['v_s5_lambda_im'], 'v_s5_log_dt': out['v_s5_log_dt'], 'v_s5_b_re': out['v_s5_b_re'], 'v_s5_b_im': out['v_s5_b_im'], 'v_s5_c_re': out['v_s5_c_re'], 'v_s5_c_im': out['v_s5_c_im'], 'v_s5_d': out['v_s5_d'], 'v_s5_w_glu': out['v_s5_w_glu'], 'v_s5_w_out': out['v_s5_w_out'], 'v_attn_w_qkv': out['v_attn_w_qkv'], 'v_attn_q_gain': out['v_attn_q_gain'], 'v_attn_k_gain': out['v_attn_k_gain'], 'v_attn_w_out': out['v_attn_w_out'], 'v_ffn_w_gate': out['v_ffn_w_gate'], 'v_ffn_w_up': out['v_ffn_w_up'], 'v_ffn_w_down': out['v_ffn_w_down'], 'v_ln_gain': out['v_ln_gain'], 'v_ln_bias': out['v_ln_bias']}


def _loss(weights, diff, rest, loss_target):
    with _jax.named_scope("forward"):
        args = {**rest, TWIN_DIFF_INPUT: diff, **{k: w.astype(_WEIGHT_DTYPES[k]) for k, w in weights.items()}}
        y = _forward(args)
    with _jax.named_scope("loss_head"):
        err = _jnp.square(y.astype(_jnp.float32) - loss_target)
        return 0.5 * _jnp.sum(_jnp.mean(err, axis=-1)) if err.ndim else 0.5 * err


def _adamw(w, g, m, v):
    m = ADAM_B1 * m + (1.0 - ADAM_B1) * g
    v = ADAM_B2 * v + (1.0 - ADAM_B2) * _jnp.square(g)
    m_hat = m / (1.0 - ADAM_B1 ** ADAM_STEP)
    v_hat = v / (1.0 - ADAM_B2 ** ADAM_STEP)
    delta = -ADAM_LR * (m_hat / (_jnp.sqrt(v_hat) + ADAM_EPS) + ADAM_WD * w)
    return delta, m, v


def reference(x, meta_tokens, s5_lambda_re, s5_lambda_im, s5_log_dt, s5_b_re, s5_b_im, s5_c_re, s5_c_im, s5_d, s5_w_glu, s5_w_out, attn_w_qkv, attn_q_gain, attn_k_gain, attn_w_out, ffn_w_gate, ffn_w_up, ffn_w_down, ln_gain, ln_bias, loss_target, m_meta_tokens, m_s5_lambda_re, m_s5_lambda_im, m_s5_log_dt, m_s5_b_re, m_s5_b_im, m_s5_c_re, m_s5_c_im, m_s5_d, m_s5_w_glu, m_s5_w_out, m_attn_w_qkv, m_attn_q_gain, m_attn_k_gain, m_attn_w_out, m_ffn_w_gate, m_ffn_w_up, m_ffn_w_down, m_ln_gain, m_ln_bias, v_meta_tokens, v_s5_lambda_re, v_s5_lambda_im, v_s5_log_dt, v_s5_b_re, v_s5_b_im, v_s5_c_re, v_s5_c_im, v_s5_d, v_s5_w_glu, v_s5_w_out, v_attn_w_qkv, v_attn_q_gain, v_attn_k_gain, v_attn_w_out, v_ffn_w_gate, v_ffn_w_up, v_ffn_w_down, v_ln_gain, v_ln_bias):
    given = dict(x=x, meta_tokens=meta_tokens, s5_lambda_re=s5_lambda_re, s5_lambda_im=s5_lambda_im, s5_log_dt=s5_log_dt, s5_b_re=s5_b_re, s5_b_im=s5_b_im, s5_c_re=s5_c_re, s5_c_im=s5_c_im, s5_d=s5_d, s5_w_glu=s5_w_glu, s5_w_out=s5_w_out, attn_w_qkv=attn_w_qkv, attn_q_gain=attn_q_gain, attn_k_gain=attn_k_gain, attn_w_out=attn_w_out, ffn_w_gate=ffn_w_gate, ffn_w_up=ffn_w_up, ffn_w_down=ffn_w_down, ln_gain=ln_gain, ln_bias=ln_bias, loss_target=loss_target, m_meta_tokens=m_meta_tokens, m_s5_lambda_re=m_s5_lambda_re, m_s5_lambda_im=m_s5_lambda_im, m_s5_log_dt=m_s5_log_dt, m_s5_b_re=m_s5_b_re, m_s5_b_im=m_s5_b_im, m_s5_c_re=m_s5_c_re, m_s5_c_im=m_s5_c_im, m_s5_d=m_s5_d, m_s5_w_glu=m_s5_w_glu, m_s5_w_out=m_s5_w_out, m_attn_w_qkv=m_attn_w_qkv, m_attn_q_gain=m_attn_q_gain, m_attn_k_gain=m_attn_k_gain, m_attn_w_out=m_attn_w_out, m_ffn_w_gate=m_ffn_w_gate, m_ffn_w_up=m_ffn_w_up, m_ffn_w_down=m_ffn_w_down, m_ln_gain=m_ln_gain, m_ln_bias=m_ln_bias, v_meta_tokens=v_meta_tokens, v_s5_lambda_re=v_s5_lambda_re, v_s5_lambda_im=v_s5_lambda_im, v_s5_log_dt=v_s5_log_dt, v_s5_b_re=v_s5_b_re, v_s5_b_im=v_s5_b_im, v_s5_c_re=v_s5_c_re, v_s5_c_im=v_s5_c_im, v_s5_d=v_s5_d, v_s5_w_glu=v_s5_w_glu, v_s5_w_out=v_s5_w_out, v_attn_w_qkv=v_attn_w_qkv, v_attn_q_gain=v_attn_q_gain, v_attn_k_gain=v_attn_k_gain, v_attn_w_out=v_attn_w_out, v_ffn_w_gate=v_ffn_w_gate, v_ffn_w_up=v_ffn_w_up, v_ffn_w_down=v_ffn_w_down, v_ln_gain=v_ln_gain, v_ln_bias=v_ln_bias)
    weights = {n: given[n] for n in TWIN_WEIGHTS}
    shared = {n: given[n] for n in SHARED_INPUTS}
    per_example = {n: given[n] for n in ['x']}
    grad_fn = _jax.value_and_grad(_loss, argnums=(0, 1))

    def one_microbatch(ex, loss_target):
        ex = dict(ex)
        diff = ex.pop(TWIN_DIFF_INPUT)
        return grad_fn(weights, diff, {**shared, **ex}, loss_target)

    if N_MICROBATCH == 1:
        loss, (grad_w, grad_x) = one_microbatch(per_example, given["loss_target"])
    else:
        def body(carry, xs):
            loss_sum, grad_sum = carry
            l_k, (gw_k, gx_k) = one_microbatch(xs[0], xs[1])
            with _jax.named_scope("update"):
                return (loss_sum + l_k, _jax.tree.map(_jnp.add, grad_sum, gw_k)), gx_k

        init = (_jnp.zeros((), _jnp.float32), _jax.tree.map(_jnp.zeros_like, weights))
        (loss, grad_w), grad_x = _jax.lax.scan(body, init, (per_example, given["loss_target"]))
    with _jax.named_scope("update"):
        delta_w, new_m, new_v = {}, {}, {}
        for n in TWIN_WEIGHTS:
            delta_w[n], new_m[n], new_v[n] = _adamw(weights[n], grad_w[n], given["m_" + n], given["v_" + n])
    return (loss, grad_x, *[grad_w[n] for n in TWIN_WEIGHTS], *[delta_w[n] for n in TWIN_WEIGHTS],
            *[new_m[n] for n in TWIN_WEIGHTS], *[new_v[n] for n in TWIN_WEIGHTS])
```

```python
import functools
import math

import jax
import jax.numpy as jnp
from jax import lax
from jax.experimental import pallas as pl
from jax.experimental.pallas import tpu as pltpu

f32 = jnp.float32
bf16 = jnp.bfloat16
HI = lax.Precision.HIGHEST
MESH = pl.DeviceIdType.MESH
AXES = ("x", "y", "c")
ANY = pl.BlockSpec(memory_space=pl.ANY)

D = 1024
DEPTH = 4
N_META = 16
PAD = 112
OFF = PAD + N_META
GRID_W = 64
HD = 64
NQ = 16
NKV = 4
QW = NQ * HD
KW = NKV * HD
QKVW = QW + 2 * KW
DFF = 2816
GROUPS = 64
GCH = 16
NSTATE = 64
CHUNK = 16
GB = 8
ROPE_THETA = 10000.0
LN_EPS = 1e-5
QK_EPS = 1e-6
ALPHA = (2.0 * DEPTH) ** 0.25
ADAM_LR, ADAM_B1, ADAM_B2, ADAM_EPS, ADAM_WD, ADAM_STEP = 0.001, 0.9, 0.999, 1e-08, 0.01, 10
NEG = -1e30
VMEM_MB = 56

NT = (((1,), (1,)), ((), ()))
TN = (((0,), (0,)), ((), ()))

WEIGHTS = ['meta_tokens', 's5_lambda_re', 's5_lambda_im', 's5_log_dt', 's5_b_re', 's5_b_im', 's5_c_re', 's5_c_im',
           's5_d', 's5_w_glu', 's5_w_out', 'attn_w_qkv', 'attn_q_gain', 'attn_k_gain', 'attn_w_out', 'ffn_w_gate',
           'ffn_w_up', 'ffn_w_down', 'ln_gain', 'ln_bias']
SHARDED = [('s5_w_glu', 1), ('s5_w_out', 1), ('attn_w_qkv', 2), ('attn_w_out', 1), ('ffn_w_gate', 2),
           ('ffn_w_up', 2), ('ffn_w_down', 1), ('meta_tokens', 1), ('ln_gain', 2), ('ln_bias', 2)]
N_BIG = 7
REPL = ['s5_lambda_re', 's5_lambda_im', 's5_log_dt', 's5_b_re', 's5_b_im', 's5_c_re', 's5_c_im', 's5_d',
        'attn_q_gain', 'attn_k_gain']
BIG_ROWS = 5376
SH_ROWS = 5380
REP_PIECE = 136
REP_ROWS = 8 * REP_PIECE
RS_ROWS = 5632
ADAM_ROWS = 6656


def _params(sem, mb=VMEM_MB):
    return pltpu.CompilerParams(dimension_semantics=sem, vmem_limit_bytes=mb << 20)


def _ew(name, fn, rows, consts, outs, accs=(), tile=640):
    first = rows[0][0] if isinstance(rows[0], tuple) else rows[0]
    n = first.shape[-2]
    tile = min(tile, n)
    assert n % tile == 0, (name, n, tile)
    n_in, n_o, n_a = len(rows) + len(consts), len(outs), len(accs)

    def body(*refs):
        i = pl.program_id(0)
        res_o, res_a = fn(i, *[r[...] for r in refs[:n_in]])
        for r, val in zip(refs[n_in:n_in + n_o], res_o):
            r[...] = val.astype(r.dtype)
        if n_a:
            a_refs = refs[n_in + n_o:]

            @pl.when(i == 0)
            def _():
                for r in a_refs:
                    r[...] = jnp.zeros(r.shape, r.dtype)

            for r, val in zip(a_refs, res_a):
                r[...] += val

    in_specs, args = [], []
    for a in rows:
        if isinstance(a, tuple):
            arr, k = a
            in_specs.append(pl.BlockSpec((None, tile, arr.shape[2]), functools.partial(lambda i, k: (k, i, 0), k=k)))
            args.append(arr)
        else:
            in_specs.append(pl.BlockSpec((tile, a.shape[1]), lambda i: (i, 0)))
            args.append(a)
    for c in consts:
        in_specs.append(pl.BlockSpec(c.shape, lambda i: (0, 0)))
        args.append(c)
    out_specs = [pl.BlockSpec((tile, c), lambda i: (i, 0)) for c, _ in outs]
    out_specs += [pl.BlockSpec(s, lambda i: (0, 0)) for s in accs]
    out_shape = [jax.ShapeDtypeStruct((n, c), dt) for c, dt in outs]
    out_shape += [jax.ShapeDtypeStruct(s, f32) for s in accs]
    res = pl.pallas_call(body, grid=(n // tile,), in_specs=in_specs, out_specs=out_specs, out_shape=out_shape,
                         name=name, compiler_params=_params(("arbitrary",)))(*args)
    return res


def _mm(name, a, b, trans_b=False, out_dtype=f32, tm=640):
    m, k = a.shape
    n = b.shape[0] if trans_b else b.shape[1]
    tm = min(tm, m)
    assert m % tm == 0
    dims = NT if trans_b else (((1,), (0,)), ((), ()))

    def body(a_ref, b_ref, o_ref):
        o_ref[...] = lax.dot_general(a_ref[...], b_ref[...], dims, preferred_element_type=f32).astype(o_ref.dtype)

    return pl.pallas_call(
        body, grid=(m // tm,),
        in_specs=[pl.BlockSpec((tm, k), lambda i: (i, 0)), pl.BlockSpec(b.shape, lambda i: (0, 0))],
        out_specs=pl.BlockSpec((tm, n), lambda i: (i, 0)),
        out_shape=jax.ShapeDtypeStruct((m, n), out_dtype), name=name, compiler_params=_params(("parallel",)))(a, b)


def _mm2(name, a1, b1, a2, b2, out_dtype=f32, tm=640):
    m, k = a1.shape
    n = b1.shape[0]
    tm = min(tm, m)
    assert m % tm == 0

    def body(a1_ref, b1_ref, a2_ref, b2_ref, o_ref):
        acc = lax.dot_general(a1_ref[...], b1_ref[...], NT, preferred_element_type=f32)
        acc += lax.dot_general(a2_ref[...], b2_ref[...], NT, preferred_element_type=f32)
        o_ref[...] = acc.astype(o_ref.dtype)

    row = pl.BlockSpec((tm, k), lambda i: (i, 0))
    whole = pl.BlockSpec(b1.shape, lambda i: (0, 0))
    return pl.pallas_call(
        body, grid=(m // tm,), in_specs=[row, whole, row, whole], out_specs=pl.BlockSpec((tm, n), lambda i: (i, 0)),
        out_shape=jax.ShapeDtypeStruct((m, n), out_dtype), name=name,
        compiler_params=_params(("parallel",)))(a1, b1, a2, b2)


def _mm_tn(name, a, g, tk=512, tl=640):
    rows, k1 = a.shape
    n = g.shape[1]
    tl = min(tl, rows)
    assert rows % tl == 0 and k1 % tk == 0

    def body(a_ref, g_ref, o_ref):
        @pl.when(pl.program_id(1) == 0)
        def _():
            o_ref[...] = jnp.zeros(o_ref.shape, f32)

        o_ref[...] += lax.dot_general(a_ref[...], g_ref[...], TN, preferred_element_type=f32)

    return pl.pallas_call(
        body, grid=(k1 // tk, rows // tl),
        in_specs=[pl.BlockSpec((tl, tk), lambda k, l: (l, k)), pl.BlockSpec((tl, n), lambda k, l: (l, 0))],
        out_specs=pl.BlockSpec((tk, n), lambda k, l: (k, 0)),
        out_shape=jax.ShapeDtypeStruct((k1, n), f32), name=name,
        compiler_params=_params(("parallel", "arbitrary")))(a, g)


def _ln_stats(r):
    mean = jnp.mean(r, axis=-1, keepdims=True)
    c = r - mean
    rstd = lax.rsqrt(jnp.mean(c * c, axis=-1, keepdims=True) + LN_EPS)
    return c * rstd, rstd


def _ln_fwd(h, mix, gain, bias):
    def fn(i, h, mix, g, b):
        r = ALPHA * h + mix
        y = _ln_stats(r)[0] * g + b
        return (r, y, y), ()

    return _ew("ln_fwd", fn, [h, mix], [gain, bias], [(D, f32), (D, f32), (D, bf16)])


def _ln_bwd(d_a, d_b, r, gain):
    def core(dout, r, g):
        xhat, rstd = _ln_stats(r)
        dxh = dout * g
        dr = rstd * (dxh - jnp.mean(dxh, axis=-1, keepdims=True) - xhat * jnp.mean(dxh * xhat, axis=-1, keepdims=True))
        return (dr, dr), (jnp.sum(dout * xhat, axis=0, keepdims=True), jnp.sum(dout, axis=0, keepdims=True))

    outs, accs = [(D, f32), (D, bf16)], [(1, D), (1, D)]
    if d_a is None:
        return _ew("ln_bwd_top", lambda i, d, r, g: core(d, r, g), [d_b, r], [gain], outs, accs)
    return _ew("ln_bwd", lambda i, da, db, r, g: core(ALPHA * da + db, r, g), [d_a, d_b, r], [gain], outs, accs)


def _sigmoid(x):
    return 1.0 / (1.0 + jnp.exp(-x))


def _swiglu_fwd(gate, up):
    def fn(i, g, u):
        g = g.astype(f32)
        return ((g * _sigmoid(g) * u.astype(f32)),), ()

    return _ew("swiglu_fwd", fn, [gate, up], [], [(DFF, bf16)], tile=320)[0]


def _swiglu_bwd(dact, gate, up):
    def fn(i, da, g, u):
        da, g, u = da.astype(f32), g.astype(f32), u.astype(f32)
        s = _sigmoid(g)
        return (da * u * s * (1.0 + g * (1.0 - s)), da * g * s), ()

    return _ew("swiglu_bwd", fn, [dact, gate, up], [], [(DFF, bf16), (DFF, bf16)], tile=320)


def _loss_grad(y, target):
    def fn(i, y, t):
        e = y - t
        return (e * (1.0 / D),), (jnp.sum(e * e, axis=0, keepdims=True),)

    return _ew("loss", fn, [y, target], [], [(D, f32)], [(1, D)], tile=512)


def _adamw(w, g, m, v):
    def fn(i, w, g, m, v):
        m = ADAM_B1 * m + (1.0 - ADAM_B1) * g
        v = ADAM_B2 * v + (1.0 - ADAM_B2) * jnp.square(g)
        m_hat = m / (1.0 - ADAM_B1 ** ADAM_STEP)
        v_hat = v / (1.0 - ADAM_B2 ** ADAM_STEP)
        delta = -ADAM_LR * (m_hat / (jnp.sqrt(v_hat) + ADAM_EPS) + ADAM_WD * w)
        return (delta, m, v), ()

    return _ew("adamw", fn, [w, g, m, v], [], [(D, f32)] * 3, tile=512)


def _s5_mats(lam_re, lam_im, log_dt, b_re, b_im, c_re, c_im):
    steps = jnp.arange(CHUNK + 1, dtype=f32)
    shift = (jnp.arange(CHUNK)[None, None, :] - jnp.arange(CHUNK)[None, :, None]
             == jnp.arange(CHUNK)[:, None, None]).astype(f32)

    def one(lr, li, ldt, br, bi, cr, ci, reverse):
        dt = jnp.exp(ldt)[:, None]
        mag = jnp.exp(lr * dt)
        abr, abi = mag * jnp.cos(li * dt), mag * jnp.sin(li * dt)
        nr, ni = abr - 1.0, abi
        den = lr * lr + li * li
        zr, zi = (nr * lr + ni * li) / den, (ni * lr - nr * li) / den
        bbr = zr[..., None] * br - zi[..., None] * bi
        bbi = zr[..., None] * bi + zi[..., None] * br
        pmag = jnp.exp(steps[:, None, None] * (lr * dt)[None])
        pang = steps[:, None, None] * (li * dt)[None]
        pr, pi = pmag * jnp.cos(pang), pmag * jnp.sin(pang)
        car = cr[None] * pr[:, :, None, :] - ci[None] * pi[:, :, None, :]
        cai = cr[None] * pi[:, :, None, :] + ci[None] * pr[:, :, None, :]
        kern = (jnp.einsum('jgop,gpi->jgoi', car[:CHUNK], bbr, precision=HI)
                - jnp.einsum('jgop,gpi->jgoi', cai[:CHUNK], bbi, precision=HI))
        m = jnp.einsum('jgoi,jst->gsito', kern, shift, precision=HI)
        qr, qi = pr[:CHUNK][::-1], pi[:CHUNK][::-1]
        pin_re = qr[:, :, :, None] * bbr[None] - qi[:, :, :, None] * bbi[None]
        pin_im = qr[:, :, :, None] * bbi[None] + qi[:, :, :, None] * bbr[None]
        pin = jnp.stack([pin_re, pin_im], 0).transpose(2, 1, 4, 0, 3)
        pout = jnp.stack([car[1:], -cai[1:]], 0).transpose(2, 0, 4, 1, 3)
        if reverse:
            m, pin, pout = m[:, ::-1, :, ::-1, :], pin[:, ::-1], pout[:, :, :, ::-1, :]
        n = CHUNK * GCH
        return m.reshape(GROUPS, n, n), pin.reshape(GROUPS, n, 2 * NSTATE), pout.reshape(GROUPS, 2 * NSTATE, n), pr[CHUNK], pi[CHUNK]

    mf, pinf, poutf, arf, aif = one(lam_re[0], lam_im[0], log_dt[0], b_re[0], b_im[0], c_re[0], c_im[0], False)
    mr, pinr, poutr, arr, air = one(lam_re[1], lam_im[1], log_dt[1], b_re[1], b_im[1], c_re[1], c_im[1], True)
    return (mf + mr, jnp.concatenate([pinf, pinr], 2), jnp.concatenate([poutf, poutr], 1),
            jnp.stack([arf, arr]), jnp.stack([aif, air]))


def _s5_coefs(a_re, a_im):
    c1 = jnp.concatenate([a_re, a_re], -1)
    c2 = jnp.concatenate([-a_im, a_im], -1)
    return c1[0], c2[0], c1[1], c2[1]


def _swap(s):
    return pltpu.roll(s, NSTATE, 1)


def _s5_states(nc, u_ref, pin_ref, coef, vf, vr, sf, sr):
    c1f, c2f, c1r, c2r = coef
    for g in range(GB):
        v = jnp.dot(u_ref[g], pin_ref[g], preferred_element_type=f32)
        vf[pl.ds(g * nc, nc), :] = v[:, :2 * NSTATE]
        vr[pl.ds(g * nc, nc), :] = v[:, 2 * NSTATE:]

    def step(i, carry):
        s_f, s_r = carry
        kf, kr = i, nc - 1 - i
        sf[pl.ds(kf, GB, stride=nc), :] = s_f
        sr[pl.ds(kr, GB, stride=nc), :] = s_r
        s_f = c1f * s_f + c2f * _swap(s_f) + vf[pl.ds(kf, GB, stride=nc), :]
        s_r = c1r * s_r + c2r * _swap(s_r) + vr[pl.ds(kr, GB, stride=nc), :]
        return s_f, s_r

    z = jnp.zeros((GB, 2 * NSTATE), f32)
    lax.fori_loop(0, nc, step, (z, z))


def _s5_core_fwd(ug, msum, pin, pout, coefs):
    nc = ug.shape[1]
    n = CHUNK * GCH

    def body(u_ref, m_ref, pin_ref, pout_ref, c1f, c2f, c1r, c2r, y_ref, vf, vr, sf, sr):
        coef = (c1f[...], c2f[...], c1r[...], c2r[...])
        _s5_states(nc, u_ref, pin_ref, coef, vf, vr, sf, sr)
        for g in range(GB):
            s_in = jnp.concatenate([sf[pl.ds(g * nc, nc), :], sr[pl.ds(g * nc, nc), :]], axis=1).astype(bf16)
            y_ref[g] = (jnp.dot(u_ref[g], m_ref[g], preferred_element_type=f32)
                        + jnp.dot(s_in, pout_ref[g], preferred_element_type=f32))

    seq = pl.BlockSpec((GB, nc, n), lambda i: (i, 0, 0))
    mat = pl.BlockSpec((GB, n, n), lambda i: (i, 0, 0))
    cf = pl.BlockSpec((GB, 2 * NSTATE), lambda i: (i, 0))
    scr = pltpu.VMEM((GB * nc, 2 * NSTATE), f32)
    return pl.pallas_call(
        body, grid=(GROUPS // GB,), in_specs=[seq, mat, mat, mat, cf, cf, cf, cf], out_specs=seq,
        out_shape=jax.ShapeDtypeStruct((GROUPS, nc, n), f32), scratch_shapes=[scr, scr, scr, scr],
        name="s5_core_fwd", compiler_params=_params(("parallel",)))(ug, msum, pin, pout, *coefs)


def _s5_core_bwd(ug, dyg, msum, pin, pout, coefs):
    nc = ug.shape[1]
    n = CHUNK * GCH

    def body(u_ref, dy_ref, m_ref, pin_ref, pout_ref, c1f, c2f, c1r, c2r,
             du_ref, dm_ref, dpin_ref, dpout_ref, a1f_ref, a2f_ref, a1r_ref, a2r_ref, vf, vr, sf, sr):
        coef = (c1f[...], c2f[...], c1r[...], c2r[...])
        _s5_states(nc, u_ref, pin_ref, coef, vf, vr, sf, sr)
        for g in range(GB):
            s_in = jnp.concatenate([sf[pl.ds(g * nc, nc), :], sr[pl.ds(g * nc, nc), :]], axis=1).astype(bf16)
            dy = dy_ref[g]
            ds = lax.dot_general(dy, pout_ref[g], NT, preferred_element_type=f32)
            vf[pl.ds(g * nc, nc), :] = ds[:, :2 * NSTATE]
            vr[pl.ds(g * nc, nc), :] = ds[:, 2 * NSTATE:]
            dpout_ref[g] = lax.dot_general(s_in, dy, TN, preferred_element_type=f32)
            dm_ref[g] = lax.dot_general(u_ref[g], dy, TN, preferred_element_type=f32)

        k1f, k2f, k1r, k2r = coef[0], -coef[1], coef[2], -coef[3]

        def step(i, carry):
            g_f, g_r, a1f, a2f, a1r, a2r = carry
            kf, kr = nc - 1 - i, i
            s_f = sf[pl.ds(kf, GB, stride=nc), :]
            s_r = sr[pl.ds(kr, GB, stride=nc), :]
            d_f = vf[pl.ds(kf, GB, stride=nc), :]
            d_r = vr[pl.ds(kr, GB, stride=nc), :]
            sf[pl.ds(kf, GB, stride=nc), :] = g_f
            sr[pl.ds(kr, GB, stride=nc), :] = g_r
            a1f, a2f = a1f + g_f * s_f, a2f + g_f * _swap(s_f)
            a1r, a2r = a1r + g_r * s_r, a2r + g_r * _swap(s_r)
            g_f = d_f + k1f * g_f + k2f * _swap(g_f)
            g_r = d_r + k1r * g_r + k2r * _swap(g_r)
            return g_f, g_r, a1f, a2f, a1r, a2r

        z = jnp.zeros((GB, 2 * NSTATE), f32)
        _, _, a1f, a2f, a1r, a2r = lax.fori_loop(0, nc, step, (z, z, z, z, z, z))
        a1f_ref[...], a2f_ref[...], a1r_ref[...], a2r_ref[...] = a1f, a2f, a1r, a2r
        for g in range(GB):
            dv = jnp.concatenate([sf[pl.ds(g * nc, nc), :], sr[pl.ds(g * nc, nc), :]], axis=1).astype(bf16)
            du_ref[g] = (lax.dot_general(dy_ref[g], m_ref[g], NT, preferred_element_type=f32)
                         + lax.dot_general(dv, pin_ref[g], NT, preferred_element_type=f32))
            dpin_ref[g] = lax.dot_general(u_ref[g], dv, TN, preferred_element_type=f32)

    seq = pl.BlockSpec((GB, nc, n), lambda i: (i, 0, 0))
    mat = pl.BlockSpec((GB, n, n), lambda i: (i, 0, 0))
    cf = pl.BlockSpec((GB, 2 * NSTATE), lambda i: (i, 0))
    scr = pltpu.VMEM((GB * nc, 2 * NSTATE), f32)
    mat_s = jax.ShapeDtypeStruct((GROUPS, n, n), f32)
    cf_s = jax.ShapeDtypeStruct((GROUPS, 2 * NSTATE), f32)
    return pl.pallas_call(
        body, grid=(GROUPS // GB,), in_specs=[seq, seq, mat, mat, mat, cf, cf, cf, cf],
        out_specs=[seq, mat, mat, mat, cf, cf, cf, cf],
        out_shape=[jax.ShapeDtypeStruct((GROUPS, nc, n), f32), mat_s, mat_s, mat_s, cf_s, cf_s, cf_s, cf_s],
        scratch_shapes=[scr, scr, scr, scr], name="s5_core_bwd",
        compiler_params=_params(("parallel",)))(ug, dyg, msum, pin, pout, *coefs)


def _to_groups(a):
    n = a.shape[0]
    return a.reshape(n // CHUNK, CHUNK, GROUPS, GCH).transpose(2, 0, 1, 3).reshape(GROUPS, n // CHUNK, CHUNK * GCH)


def _from_groups(g):
    nc = g.shape[1]
    return g.reshape(GROUPS, nc, CHUNK, GCH).transpose(1, 2, 0, 3).reshape(nc * CHUNK, D)


def _gelu(y):
    return 0.5 * y * (1.0 + lax.erf(y * (2.0 ** -0.5)))


def _gelu_grad(y):
    return 0.5 * (1.0 + lax.erf(y * (2.0 ** -0.5))) + y * jnp.exp(-0.5 * y * y) * (1.0 / math.sqrt(2.0 * math.pi))


def _s5_fwd(h, h_bf, valid, mats, d_skip, w_glu, w_out):
    msum, pin, pout, a_re, a_im = mats
    coefs = _s5_coefs(a_re, a_im)
    ug = _to_groups(jnp.where(valid, h_bf, jnp.zeros_like(h_bf)))
    ys = _from_groups(_s5_core_fwd(ug, msum.astype(bf16), pin.astype(bf16), pout.astype(bf16), coefs))

    def post(i, ys, h, d):
        y = ys + d * h
        return (y, _gelu(y)), ()

    y, g_bf = _ew("s5_gelu", post, [ys, h], [d_skip], [(D, f32), (D, bf16)])
    gw = _mm("s5_glu_mm", g_bf, w_glu)

    def glu(i, y, gw):
        return (_gelu(y) * _sigmoid(gw),), ()

    z_bf = _ew("s5_glu", glu, [y, gw], [], [(D, bf16)])[0]
    mix = _mm("s5_out_mm", z_bf, w_out)
    return mix, (ug, y, g_bf, gw, z_bf)


def _s5_bwd(dmix_bf, h, valid, saved, mats, vjp_mats, d_skip, w_glu, w_out):
    ug, y, g_bf, gw, z_bf = saved
    msum, pin, pout, a_re, a_im = mats
    coefs = _s5_coefs(a_re, a_im)
    dz = _mm("s5_dz_mm", dmix_bf, w_out, trans_b=True)
    d_w_out = _mm_tn("s5_dwout", z_bf, dmix_bf)

    def dglu(i, dz, y, gw):
        g, s = _gelu(y), _sigmoid(gw)
        return (dz * g * s * (1.0 - s), dz * s), ()

    dgw_bf, dg1 = _ew("s5_dglu", dglu, [dz, y, gw], [], [(D, bf16), (D, f32)])
    d_w_glu = _mm_tn("s5_dwglu", g_bf, dgw_bf)
    dg2 = _mm("s5_dg_mm", dgw_bf, w_glu, trans_b=True)

    def dgelu(i, dg1, dg2, y, h, d):
        dy = (dg1 + dg2) * _gelu_grad(y)
        return (dy, dy * d), (jnp.sum(dy * h, axis=0, keepdims=True),)

    dy_bf, dh_skip, dd = _ew("s5_dgelu", dgelu, [dg1, dg2, y, h], [d_skip], [(D, bf16), (D, f32)], [(1, D)])
    dug, dm, dpin, dpout, a1f, a2f, a1r, a2r = _s5_core_bwd(
        ug, _to_groups(dy_bf), msum.astype(bf16), pin.astype(bf16), pout.astype(bf16), coefs)
    du = _from_groups(dug)
    dh = dh_skip + jnp.where(valid, du, jnp.zeros_like(du))
    a1, a2 = jnp.stack([a1f, a1r]), jnp.stack([a2f, a2r])
    da_re = a1[..., :NSTATE] + a1[..., NSTATE:]
    da_im = a2[..., NSTATE:] - a2[..., :NSTATE]
    d_params = vjp_mats((dm, dpin, dpout, da_re, da_im))
    return dh, d_params, dd[0], d_w_glu, d_w_out


def _rope_tables(n):
    row = jnp.arange(n, dtype=jnp.int32) - OFF
    real = row >= 0
    rid = jnp.where(real, row // GRID_W, 0).astype(f32)
    cid = jnp.where(real, row % GRID_W, 0).astype(f32)
    half = HD // 2
    inv = ROPE_THETA ** (-jnp.arange(0, half, 2, dtype=f32) / half)
    ar, ac = rid[:, None] * inv[None, :], cid[:, None] * inv[None, :]
    cos = jnp.concatenate([jnp.cos(ar), jnp.cos(ar), jnp.cos(ac), jnp.cos(ac)], axis=1)
    sin = jnp.concatenate([-jnp.sin(ar), jnp.sin(ar), -jnp.sin(ac), jnp.sin(ac)], axis=1)
    return jnp.tile(cos, (1, 2)), jnp.tile(sin, (1, 2))


def _head_mats():
    head = jnp.arange(QW, dtype=jnp.int32)[:, None] // HD == jnp.arange(128, dtype=jnp.int32)[None, :]
    return head.astype(f32) * (1.0 / HD), head.astype(f32).T


def _rot(v):
    w = v.shape[1]
    lane = lax.broadcasted_iota(jnp.int32, v.shape, 1)
    return jnp.where(lane % 32 < 16, pltpu.roll(v, w - 16, 1), pltpu.roll(v, 16, 1))


def _head_mean(v, e, et):
    w = v.shape[1]
    m = jnp.dot(v, e[:w], preferred_element_type=f32, precision=HI)
    return m, et[:, :w]


def _rms_rope(t, gain, e, et, cos, sin):
    w = t.shape[1]
    ms, spread = _head_mean(t * t, e, et)
    rs = jnp.dot(lax.rsqrt(ms + QK_EPS), spread, preferred_element_type=f32, precision=HI)
    n0 = t * rs
    n = n0 * gain
    reps = w // 128
    return n * jnp.tile(cos, (1, reps)) + _rot(n) * jnp.tile(sin, (1, reps))


def _rms_rope_bwd(dout, t, gain, e, et, cos, sin):
    w = t.shape[1]
    reps = w // 128
    ms, spread = _head_mean(t * t, e, et)
    rs = jnp.dot(lax.rsqrt(ms + QK_EPS), spread, preferred_element_type=f32, precision=HI)
    n0 = t * rs
    dn = dout * jnp.tile(cos, (1, reps)) + _rot(dout * jnp.tile(sin, (1, reps)))
    dn0 = dn * gain
    mm, _ = _head_mean(dn0 * n0, e, et)
    corr = jnp.dot(mm, spread, preferred_element_type=f32, precision=HI)
    return rs * (dn0 - n0 * corr), jnp.sum(dn * n0, axis=0, keepdims=True)


def _qk_fwd(qkv, qg, kg, e, et, cos, sin):
    def fn(i, qkv, cos, sin, qg, kg, e, et):
        q = _rms_rope(qkv[:, :QW], qg, e, et, cos, sin) * (HD ** -0.5)
        k = _rms_rope(qkv[:, QW:QW + KW], kg, e, et, cos, sin)
        return (q, k, qkv[:, QW + KW:]), ()

    return _ew("qk_rope", fn, [qkv, cos, sin], [qg, kg, e, et], [(QW, bf16), (KW, bf16), (KW, bf16)])


def _qk_bwd(qkv, dq, dk, dv, qg, kg, e, et, cos, sin):
    def fn(i, qkv, cos, sin, dq, dk, dv, qg, kg, e, et):
        dtq, dgq = _rms_rope_bwd(dq * (HD ** -0.5), qkv[:, :QW], qg, e, et, cos, sin)
        dtk, dgk = _rms_rope_bwd(dk, qkv[:, QW:QW + KW], kg, e, et, cos, sin)
        return (jnp.concatenate([dtq, dtk, dv], axis=1),), (dgq, dgk)

    return _ew("qk_rope_bwd", fn, [qkv, cos, sin, dq, dk, dv], [qg, kg, e, et], [(QKVW, bf16)], [(1, QW), (1, KW)])


def _to_heads(a, nh):
    return a.reshape(a.shape[0], nh, HD).transpose(1, 0, 2)


def _from_heads(a):
    return a.transpose(1, 0, 2).reshape(a.shape[1], a.shape[0] * HD)


def _flash_fwd(q, k, v, tq=640, tk=640):
    n = q.shape[1]
    tq, tk = min(tq, n), min(tk, n)
    nkb = n // tk

    def body(q_ref, k_ref, v_ref, o_ref, lse_ref):
        qb = q_ref[0]

        def block(kb, carry, first):
            m, l, acc = carry
            start = pl.multiple_of(kb * tk, tk)
            ks = k_ref[0, pl.ds(start, tk), :]
            vs = v_ref[0, pl.ds(start, tk), :]
            s = lax.dot_general(qb, ks, NT, preferred_element_type=f32)
            if first:
                col = lax.broadcasted_iota(jnp.int32, (1, tk), 1)
                s = jnp.where(col >= PAD, s, NEG)
            m_new = jnp.maximum(m, jnp.max(s, axis=1, keepdims=True))
            p = jnp.exp(s - m_new)
            scale = jnp.exp(m - m_new)
            l = scale * l + jnp.sum(p, axis=1, keepdims=True)
            acc = scale * acc + jnp.dot(p.astype(bf16), vs, preferred_element_type=f32)
            return m_new, l, acc

        carry = (jnp.full((tq, 1), NEG, f32), jnp.zeros((tq, 1), f32), jnp.zeros((tq, HD), f32))
        carry = block(0, carry, True)
        m, l, acc = lax.fori_loop(1, nkb, lambda kb, c: block(kb, c, False), carry)
        o_ref[0] = acc / l
        lse_ref[0] = m + jnp.log(l)

    return pl.pallas_call(
        body, grid=(NQ, n // tq),
        in_specs=[pl.BlockSpec((1, tq, HD), lambda h, i: (h, i, 0)),
                  pl.BlockSpec((1, n, HD), lambda h, i: (h // (NQ // NKV), 0, 0)),
                  pl.BlockSpec((1, n, HD), lambda h, i: (h // (NQ // NKV), 0, 0))],
        out_specs=[pl.BlockSpec((1, tq, HD), lambda h, i: (h, i, 0)), pl.BlockSpec((1, tq, 1), lambda h, i: (h, i, 0))],
        out_shape=[jax.ShapeDtypeStruct((NQ, n, HD), f32), jax.ShapeDtypeStruct((NQ, n, 1), f32)],
        name="flash_fwd", compiler_params=_params(("parallel", "parallel")))(q, k, v)


def _flash_dq(q, k, v, do, o, lse, tq=640, tk=640):
    n = q.shape[1]
    tq, tk = min(tq, n), min(tk, n)
    nkb = n // tk

    def body(q_ref, k_ref, v_ref, do_ref, o_ref, lse_ref, dq_ref, delta_ref):
        qb, dob, lse_b = q_ref[0], do_ref[0], lse_ref[0]
        delta = jnp.sum(dob.astype(f32) * o_ref[0], axis=1, keepdims=True)

        def block(kb, dq, first):
            start = pl.multiple_of(kb * tk, tk)
            ks = k_ref[0, pl.ds(start, tk), :]
            vs = v_ref[0, pl.ds(start, tk), :]
            s = lax.dot_general(qb, ks, NT, preferred_element_type=f32)
            if first:
                col = lax.broadcasted_iota(jnp.int32, (1, tk), 1)
                s = jnp.where(col >= PAD, s, NEG)
            p = jnp.exp(s - lse_b)
            dp = lax.dot_general(dob, vs, NT, preferred_element_type=f32)
            ds = p * (dp - delta)
            return dq + jnp.dot(ds.astype(bf16), ks, preferred_element_type=f32)

        dq = block(0, jnp.zeros((tq, HD), f32), True)
        dq_ref[0] = lax.fori_loop(1, nkb, lambda kb, c: block(kb, c, False), dq)
        delta_ref[0] = delta

    qspec = pl.BlockSpec((1, tq, HD), lambda h, i: (h, i, 0))
    kspec = pl.BlockSpec((1, n, HD), lambda h, i: (h // (NQ // NKV), 0, 0))
    cspec = pl.BlockSpec((1, tq, 1), lambda h, i: (h, i, 0))
    return pl.pallas_call(
        body, grid=(NQ, n // tq), in_specs=[qspec, kspec, kspec, qspec, qspec, cspec], out_specs=[qspec, cspec],
        out_shape=[jax.ShapeDtypeStruct((NQ, n, HD), f32), jax.ShapeDtypeStruct((NQ, n, 1), f32)],
        name="flash_dq", compiler_params=_params(("parallel", "parallel")))(q, k, v, do, o, lse)


def _flash_dkv(q, k, v, do, lse_row, delta_row, tq=640, tk=640):
    n = q.shape[1]
    tq, tk = min(tq, n), min(tk, n)
    nqb = n // tq
    grp = NQ // NKV

    def body(q_ref, do_ref, lse_ref, delta_ref, k_ref, v_ref, dk_ref, dv_ref):
        kb, vb = k_ref[0], v_ref[0]
        dk = jnp.zeros((tk, HD), f32)
        dv = jnp.zeros((tk, HD), f32)
        for g in range(grp):
            def block(qi, carry, g=g):
                dk, dv = carry
                start = pl.multiple_of(qi * tq, tq)
                qs = q_ref[g, pl.ds(start, tq), :]
                dos = do_ref[g, pl.ds(start, tq), :]
                lse_r = lse_ref[g, :, pl.ds(start, tq)]
                delta_r = delta_ref[g, :, pl.ds(start, tq)]
                st = lax.dot_general(kb, qs, NT, preferred_element_type=f32)
                pt = jnp.exp(st - lse_r)
                dv = dv + jnp.dot(pt.astype(bf16), dos, preferred_element_type=f32)
                dpt = lax.dot_general(vb, dos, NT, preferred_element_type=f32)
                dst = pt * (dpt - delta_r)
                dk = dk + jnp.dot(dst.astype(bf16), qs, preferred_element_type=f32)
                return dk, dv

            dk, dv = lax.fori_loop(0, nqb, block, (dk, dv))
        row = lax.broadcasted_iota(jnp.int32, (tk, 1), 0) + pl.program_id(1) * tk
        dk_ref[0] = jnp.where(row >= PAD, dk, 0.0)
        dv_ref[0] = jnp.where(row >= PAD, dv, 0.0)

    gspec = pl.BlockSpec((grp, n, HD), lambda h, j: (h, 0, 0))
    rspec = pl.BlockSpec((grp, 1, n), lambda h, j: (h, 0, 0))
    kspec = pl.BlockSpec((1, tk, HD), lambda h, j: (h, j, 0))
    return pl.pallas_call(
        body, grid=(NKV, n // tk), in_specs=[gspec, gspec, rspec, rspec, kspec, kspec], out_specs=[kspec, kspec],
        out_shape=[jax.ShapeDtypeStruct((NKV, n, HD), f32)] * 2,
        name="flash_dkv", compiler_params=_params(("parallel", "parallel")))(q, do, lse_row, delta_row, k, v)


def _attn_fwd(h_bf, w_qkv, qg, kg, w_out, tabs):
    e, et, cos, sin = tabs
    qkv = _mm("attn_qkv_mm", h_bf, w_qkv)
    q_bf, k_bf, v_bf = _qk_fwd(qkv, qg, kg, e, et, cos, sin)
    q16, k4, v4 = _to_heads(q_bf, NQ), _to_heads(k_bf, NKV), _to_heads(v_bf, NKV)
    o16, lse = _flash_fwd(q16, k4, v4)
    o_bf = _from_heads(o16).astype(bf16)
    mix = _mm("attn_out_mm", o_bf, w_out)
    return mix, (qkv, q16, k4, v4, o16, lse, o_bf)


def _attn_bwd(dmix_bf, h_bf, saved, w_qkv, qg, kg, w_out, tabs):
    e, et, cos, sin = tabs
    qkv, q16, k4, v4, o16, lse, o_bf = saved
    n = qkv.shape[0]
    do = _mm("attn_do_mm", dmix_bf, w_out, trans_b=True, out_dtype=bf16)
    d_w_out = _mm_tn("attn_dwout", o_bf, dmix_bf)
    do16 = _to_heads(do, NQ)
    dq16, delta = _flash_dq(q16, k4, v4, do16, o16, lse)
    dk4, dv4 = _flash_dkv(q16, k4, v4, do16, lse.reshape(NQ, 1, n), delta.reshape(NQ, 1, n))
    dqkv_bf, dgq, dgk = _qk_bwd(qkv, _from_heads(dq16), _from_heads(dk4), _from_heads(dv4), qg, kg, e, et, cos, sin)
    d_w_qkv = _mm_tn("attn_dwqkv", h_bf, dqkv_bf)
    dh = _mm("attn_dh_mm", dqkv_bf, w_qkv, trans_b=True)
    return dh, d_w_qkv, dgq.reshape(NQ, HD).sum(0), dgk.reshape(NKV, HD).sum(0), d_w_out


def _all_gather(name, shard):
    def body(x_ref, out_ref, send_sems, recv_sems, local_sem):
        x, y, c = lax.axis_index("x"), lax.axis_index("y"), lax.axis_index("c")
        me, sibling = (x, y, c), (x, y, 1 - c)
        chips = [(1 - x, y), (x, 1 - y), (1 - x, 1 - y)]

        def slot(px, py, pc):
            return out_ref.at[4 * px + 2 * py + pc]

        def copy(k, block, to, src=None):
            return pltpu.make_async_remote_copy(
                src_ref=slot(*block) if src is None else src, dst_ref=slot(*block),
                send_sem=send_sems.at[k], recv_sem=recv_sems.at[k], device_id=to, device_id_type=MESH)

        mine = pltpu.make_async_copy(x_ref, slot(*me), local_sem)
        mine.start()
        first = [copy(0, me, sibling, src=x_ref)]
        first += [copy(1 + j, me, (*chip, c), src=x_ref) for j, chip in enumerate(chips)]
        for cp in first:
            cp.start()
        passed = [copy(4 + j, (*chip, c), sibling) for j, chip in enumerate(chips)]
        for j, chip in enumerate(chips):
            copy(1 + j, (*chip, c), me).wait_recv()
            passed[j].start()
        copy(0, sibling, me).wait_recv()
        for j, chip in enumerate(chips):
            copy(4 + j, (*chip, 1 - c), me).wait_recv()
        for cp in first + passed:
            cp.wait_send()
        mine.wait()

    return pl.pallas_call(
        body, out_shape=jax.ShapeDtypeStruct((8,) + shard.shape, shard.dtype), in_specs=[ANY], out_specs=ANY,
        scratch_shapes=[pltpu.SemaphoreType.DMA((7,)), pltpu.SemaphoreType.DMA((7,)), pltpu.SemaphoreType.DMA],
        name=name)(shard)


def _swap_sibling(name, theirs):
    def body(src_ref, dst_ref, send_sem, recv_sem):
        x, y, c = lax.axis_index("x"), lax.axis_index("y"), lax.axis_index("c")
        cp = pltpu.make_async_remote_copy(src_ref=src_ref, dst_ref=dst_ref, send_sem=send_sem, recv_sem=recv_sem,
                                          device_id=(x, y, 1 - c), device_id_type=MESH)
        cp.start()
        cp.wait()

    return pl.pallas_call(
        body, out_shape=jax.ShapeDtypeStruct(theirs.shape, theirs.dtype), in_specs=[ANY], out_specs=ANY,
        scratch_shapes=[pltpu.SemaphoreType.DMA, pltpu.SemaphoreType.DMA], name=name)(theirs)


def _exchange_chips(name, part):
    def body(p_ref, t_ref, send_sems, recv_sems, local_sem):
        x, y, c = lax.axis_index("x"), lax.axis_index("y"), lax.axis_index("c")
        q = 2 * x + y
        own = pltpu.make_async_copy(p_ref.at[q], t_ref.at[q], local_sem)
        own.start()
        copies = []
        for k in (1, 2, 3):
            tx, ty = x ^ (k >> 1), y ^ (k & 1)
            copies.append(pltpu.make_async_remote_copy(
                src_ref=p_ref.at[2 * tx + ty], dst_ref=t_ref.at[q], send_sem=send_sems.at[k - 1],
                recv_sem=recv_sems.at[k - 1], device_id=(tx, ty, c), device_id_type=MESH))
        for cp in copies:
            cp.start()
        for cp in copies:
            cp.wait()
        own.wait()

    return pl.pallas_call(
        body, out_shape=jax.ShapeDtypeStruct(part.shape, part.dtype), in_specs=[ANY], out_specs=ANY,
        scratch_shapes=[pltpu.SemaphoreType.DMA((3,)), pltpu.SemaphoreType.DMA((3,)), pltpu.SemaphoreType.DMA],
        name=name)(part)


def _reduce_scatter(mine, theirs):
    got = _swap_sibling("rs_sibling", theirs)
    rows = 4 * RS_ROWS
    part = _ew("rs_add2", lambda i, a, b: ((a + b,), ()), [mine.reshape(rows, D), got.reshape(rows, D)], [],
               [(D, f32)], tile=512)[0]
    t = _exchange_chips("rs_chips", part.reshape(4, RS_ROWS, D))
    return _ew("rs_add4", lambda i, a, b, c, d: ((((a + b) + c) + d,), ()), [(t, 0), (t, 1), (t, 2), (t, 3)], [],
               [(D, f32)], tile=512)[0]


def _pack_rows(parts, rows):
    flat = jnp.concatenate([p.reshape(-1) for p in parts])
    return jnp.pad(flat, (0, rows * D - flat.shape[0])).reshape(rows, D)


def _unpack(flat, shapes):
    out, off = [], 0
    for s in shapes:
        n = math.prod(s)
        out.append(flat[off:off + n].reshape(s))
        off += n
    return out


def _unshard(gathered, local_shape, axis):
    a = jnp.moveaxis(gathered.reshape((8,) + tuple(local_shape)), 0, axis)
    shp = list(local_shape)
    shp[axis] *= 8
    return a.reshape(shp)


def _grad_slots(full, repl, cc):
    def halves(a5, axis):
        res = []
        for sel in (cc, 1 - cc):
            s = lax.dynamic_index_in_dim(a5, sel, axis=axis, keepdims=False)
            res.append(jnp.moveaxis(s, axis - 1, 0).reshape(4, -1))
        return res

    mine, theirs = [], []
    for name, ax in SHARDED:
        g = full[name]
        shp = g.shape
        g5 = g.reshape(shp[:ax] + (4, 2, shp[ax] // 8) + shp[ax + 1:])
        a, b = halves(g5, ax + 1)
        mine.append(a)
        theirs.append(b)
    rep = _pack_rows([repl[n] for n in REPL], REP_ROWS).reshape(4, 2, REP_PIECE * D)
    a, b = halves(rep, 1)
    padz = jnp.zeros((4, (RS_ROWS - SH_ROWS - REP_PIECE) * D), f32)
    mine = jnp.concatenate(mine + [a, padz], axis=1).reshape(4, RS_ROWS, D)
    theirs = jnp.concatenate(theirs + [b, padz], axis=1).reshape(4, RS_ROWS, D)
    return mine, theirs


def _local_step(x0, target0, w, fw):
    seq = x0.shape[0]
    n = OFF + seq
    valid = (jnp.arange(n, dtype=jnp.int32) >= PAD)[:, None]
    h = jnp.concatenate([jnp.zeros((PAD, D), f32), fw['meta_tokens'], x0], axis=0)
    h_bf = h.astype(bf16)
    tabs = _head_mats() + _rope_tables(n)
    qg = [jnp.tile(w['attn_q_gain'][j], NQ)[None, :] for j in range(2)]
    kg = [jnp.tile(w['attn_k_gain'][j], NKV)[None, :] for j in range(2)]
    s5_names = ['s5_lambda_re', 's5_lambda_im', 's5_log_dt', 's5_b_re', 's5_b_im', 's5_c_re', 's5_c_im']
    s5_mats, s5_vjp = [], []
    for j in range(2):
        mats, vjp = jax.vjp(_s5_mats, *[w[k][j] for k in s5_names])
        s5_mats.append(mats)
        s5_vjp.append(vjp)
    saved = []
    for i in range(DEPTH):
        j = i // 2
        if i % 2 == 0:
            mix, sv = _s5_fwd(h, h_bf, valid, s5_mats[j], w['s5_d'][j][None, :], fw['s5_w_glu'][j], fw['s5_w_out'][j])
        else:
            mix, sv = _attn_fwd(h_bf, fw['attn_w_qkv'][j], qg[j], kg[j], fw['attn_w_out'][j], tabs)
        r1, h1, h1_bf = _ln_fwd(h, mix, fw['ln_gain'][i, 0][None, :], fw['ln_bias'][i, 0][None, :])
        gate = _mm("ffn_gate_mm", h1_bf, fw['ffn_w_gate'][i], out_dtype=bf16)
        up = _mm("ffn_up_mm", h1_bf, fw['ffn_w_up'][i], out_dtype=bf16)
        act = _swiglu_fwd(gate, up)
        f = _mm("ffn_down_mm", act, fw['ffn_w_down'][i])
        r2, h2, h2_bf = _ln_fwd(h1, f, fw['ln_gain'][i, 1][None, :], fw['ln_bias'][i, 1][None, :])
        saved.append((h, h_bf, sv, r1, h1_bf, gate, up, act, r2))
        h, h_bf = h2, h2_bf

    dy, sq = _loss_grad(h[OFF:], target0)
    loss = 0.5 * jnp.sum(sq) * (1.0 / D)

    grads = {k: [None] * (DEPTH if k.startswith('ffn') else 2) for k in WEIGHTS}
    d_ln_gain = [[None, None] for _ in range(DEPTH)]
    d_ln_bias = [[None, None] for _ in range(DEPTH)]
    d_a, d_b = None, jnp.pad(dy, ((OFF, 0), (0, 0)))
    for i in reversed(range(DEPTH)):
        j = i // 2
        h_in, h_in_bf, sv, r1, h1_bf, gate, up, act, r2 = saved[i]
        dr2, dr2_bf, dg, db = _ln_bwd(d_a, d_b, r2, fw['ln_gain'][i, 1][None, :])
        d_ln_gain[i][1], d_ln_bias[i][1] = dg[0], db[0]
        dact = _mm("ffn_dact_mm", dr2_bf, fw['ffn_w_down'][i], trans_b=True, out_dtype=bf16)
        dgate, dup = _swiglu_bwd(dact, gate, up)
        grads['ffn_w_down'][i] = _mm_tn("ffn_dwdown", act, dr2_bf, tk=DFF // 2)
        grads['ffn_w_gate'][i] = _mm_tn("ffn_dwgate", h1_bf, dgate)
        grads['ffn_w_up'][i] = _mm_tn("ffn_dwup", h1_bf, dup)
        dh1 = _mm2("ffn_dh_mm", dgate, fw['ffn_w_gate'][i], dup, fw['ffn_w_up'][i])
        dr1, dr1_bf, dg, db = _ln_bwd(dr2, dh1, r1, fw['ln_gain'][i, 0][None, :])
        d_ln_gain[i][0], d_ln_bias[i][0] = dg[0], db[0]
        if i % 2 == 0:
            dh, d_par, dd, d_w_glu, d_w_out = _s5_bwd(dr1_bf, h_in, valid, sv, s5_mats[j], s5_vjp[j],
                                                      w['s5_d'][j][None, :], fw['s5_w_glu'][j], fw['s5_w_out'][j])
            for k, g in zip(s5_names, d_par):
                grads[k][j] = g
            grads['s5_d'][j], grads['s5_w_glu'][j], grads['s5_w_out'][j] = dd, d_w_glu, d_w_out
        else:
            dh, d_w_qkv, dgq, dgk, d_w_out = _attn_bwd(dr1_bf, h_in_bf, sv, fw['attn_w_qkv'][j], qg[j], kg[j],
                                                       fw['attn_w_out'][j], tabs)
            grads['attn_w_qkv'][j], grads['attn_w_out'][j] = d_w_qkv, d_w_out
            grads['attn_q_gain'][j], grads['attn_k_gain'][j] = dgq, dgk
        d_a, d_b = dr1, dh
    dh0 = _ew("dh0", lambda i, a, b: ((ALPHA * a + b,), ()), [d_a, d_b], [], [(D, f32)])[0]
    full = {k: jnp.stack(v) for k, v in grads.items() if v[0] is not None}
    full['meta_tokens'] = dh0[PAD:OFF]
    full['ln_gain'] = jnp.stack([jnp.stack(r) for r in d_ln_gain])
    full['ln_bias'] = jnp.stack([jnp.stack(r) for r in d_ln_bias])

    return loss, dh0[OFF:], full


def kernel(x, meta_tokens, s5_lambda_re, s5_lambda_im, s5_log_dt, s5_b_re, s5_b_im, s5_c_re, s5_c_im, s5_d, s5_w_glu, s5_w_out, attn_w_qkv, attn_q_gain, attn_k_gain, attn_w_out, ffn_w_gate, ffn_w_up, ffn_w_down, ln_gain, ln_bias, loss_target, m_meta_tokens, m_s5_lambda_re, m_s5_lambda_im, m_s5_log_dt, m_s5_b_re, m_s5_b_im, m_s5_c_re, m_s5_c_im, m_s5_d, m_s5_w_glu, m_s5_w_out, m_attn_w_qkv, m_attn_q_gain, m_attn_k_gain, m_attn_w_out, m_ffn_w_gate, m_ffn_w_up, m_ffn_w_down, m_ln_gain, m_ln_bias, v_meta_tokens, v_s5_lambda_re, v_s5_lambda_im, v_s5_log_dt, v_s5_b_re, v_s5_b_im, v_s5_c_re, v_s5_c_im, v_s5_d, v_s5_w_glu, v_s5_w_out, v_attn_w_qkv, v_attn_q_gain, v_attn_k_gain, v_attn_w_out, v_ffn_w_gate, v_ffn_w_up, v_ffn_w_down, v_ln_gain, v_ln_bias):
    w = dict(zip(WEIGHTS, (meta_tokens, s5_lambda_re, s5_lambda_im, s5_log_dt, s5_b_re, s5_b_im, s5_c_re, s5_c_im, s5_d, s5_w_glu, s5_w_out, attn_w_qkv, attn_q_gain, attn_k_gain, attn_w_out, ffn_w_gate, ffn_w_up, ffn_w_down, ln_gain, ln_bias)))
    mom = dict(zip(WEIGHTS, (m_meta_tokens, m_s5_lambda_re, m_s5_lambda_im, m_s5_log_dt, m_s5_b_re, m_s5_b_im, m_s5_c_re, m_s5_c_im, m_s5_d, m_s5_w_glu, m_s5_w_out, m_attn_w_qkv, m_attn_q_gain, m_attn_k_gain, m_attn_w_out, m_ffn_w_gate, m_ffn_w_up, m_ffn_w_down, m_ln_gain, m_ln_bias)))
    vel = dict(zip(WEIGHTS, (v_meta_tokens, v_s5_lambda_re, v_s5_lambda_im, v_s5_log_dt, v_s5_b_re, v_s5_b_im, v_s5_c_re, v_s5_c_im, v_s5_d, v_s5_w_glu, v_s5_w_out, v_attn_w_qkv, v_attn_q_gain, v_attn_k_gain, v_attn_w_out, v_ffn_w_gate, v_ffn_w_up, v_ffn_w_down, v_ln_gain, v_ln_bias)))
    cc = lax.axis_index("c")
    sh_names = [n for n, _ in SHARDED]
    sh_axis = dict(SHARDED)

    w_rows = _pack_rows([w[n] for n in sh_names], SH_ROWS)
    g_big = _all_gather("ag_weights", w_rows[:BIG_ROWS].astype(bf16)).reshape(8, BIG_ROWS * D)
    g_small = _all_gather("ag_vectors", jnp.pad(w_rows[BIG_ROWS:], ((0, 8 - (SH_ROWS - BIG_ROWS)), (0, 0))))
    g_small = g_small.reshape(8, 8 * D)
    fw, off_big, off_small = {}, 0, 0
    for i, n in enumerate(sh_names):
        size = w[n].size
        if i < N_BIG:
            fw[n] = _unshard(g_big[:, off_big:off_big + size], w[n].shape, sh_axis[n])
            off_big += size
        else:
            fw[n] = _unshard(g_small[:, off_small:off_small + size], w[n].shape, sh_axis[n])
            off_small += size

    loss, grad_x, full = _local_step(x[0], loss_target[0], w, fw)
    loss = lax.psum(loss, AXES)
    grad_x = grad_x[None]

    mine, theirs = _grad_slots(full, full, cc)
    red = _reduce_scatter(mine, theirs)
    rep_all = _all_gather("ag_repl_grads", red[SH_ROWS:SH_ROWS + REP_PIECE]).reshape(REP_ROWS * D)
    g_sh = dict(zip(sh_names, _unpack(red[:SH_ROWS].reshape(-1), [w[k].shape for k in sh_names])))
    g_rep = dict(zip(REPL, _unpack(rep_all, [w[k].shape for k in REPL])))

    order = sh_names + REPL
    g_all = {**g_sh, **g_rep}
    shapes = [w[k].shape for k in order]
    packed = [_pack_rows([d[k] for k in order], ADAM_ROWS) for d in (w, g_all, mom, vel)]
    delta, new_m, new_v = [dict(zip(order, _unpack(a.reshape(-1), shapes))) for a in _adamw(*packed)]
    return (loss, grad_x, *[g_all[k] for k in WEIGHTS], *[delta[k] for k in WEIGHTS],
            *[new_m[k] for k in WEIGHTS], *[new_v[k] for k in WEIGHTS])
```

```python
import functools
import math

import jax
import jax.numpy as jnp
from jax import lax
from jax.experimental import pallas as pl
from jax.experimental.pallas import tpu as pltpu

f32 = jnp.float32
bf16 = jnp.bfloat16
HI = lax.Precision.HIGHEST
MESH = pl.DeviceIdType.MESH
AXES = ("x", "y", "c")
ANY = pl.BlockSpec(memory_space=pl.ANY)

D = 1024
DEPTH = 4
N_META = 16
PAD = 112
OFF = PAD + N_META
GRID_W = 64
HD = 64
NQ = 16
NKV = 4
QW = NQ * HD
KW = NKV * HD
QKVW = QW + 2 * KW
DFF = 2816
GROUPS = 64
GCH = 16
NSTATE = 64
CHUNK = 16
GB = 8
ROPE_THETA = 10000.0
LN_EPS = 1e-5
QK_EPS = 1e-6
ALPHA = (2.0 * DEPTH) ** 0.25
ADAM_LR, ADAM_B1, ADAM_B2, ADAM_EPS, ADAM_WD, ADAM_STEP = 0.001, 0.9, 0.999, 1e-08, 0.01, 10
NEG = -1e30
VMEM_MB = 56

NT = (((1,), (1,)), ((), ()))
TN = (((0,), (0,)), ((), ()))

WEIGHTS = ['meta_tokens', 's5_lambda_re', 's5_lambda_im', 's5_log_dt', 's5_b_re', 's5_b_im', 's5_c_re', 's5_c_im',
           's5_d', 's5_w_glu', 's5_w_out', 'attn_w_qkv', 'attn_q_gain', 'attn_k_gain', 'attn_w_out', 'ffn_w_gate',
           'ffn_w_up', 'ffn_w_down', 'ln_gain', 'ln_bias']
DFFP = 3072
FF_BLK, FF_BLKP = DFF // 8, DFFP // 8
MATS = [('s5_w_glu', False, 128, 128), ('s5_w_out', False, 128, 128), ('attn_w_qkv', True, 192, 192),
        ('attn_w_out', False, 128, 128), ('ffn_w_gate', True, FF_BLKP, FF_BLK), ('ffn_w_up', True, FF_BLKP, FF_BLK),
        ('ffn_w_down', False, FF_BLKP, FF_BLK)]
VECS = ['meta_tokens', 'ln_gain', 'ln_bias']
REPL = ['s5_lambda_re', 's5_lambda_im', 's5_log_dt', 's5_b_re', 's5_b_im', 's5_c_re', 's5_c_im', 's5_d',
        'attn_q_gain', 'attn_k_gain']
MAT_ROWS = 5760
REP_PIECE = 160
REP_ROWS = 8 * REP_PIECE
RS_ROWS = MAT_ROWS + REP_PIECE
RS_TILE = 160
SMALL_ROWS = 1088


def _params(sem, mb=VMEM_MB):
    return pltpu.CompilerParams(dimension_semantics=sem, vmem_limit_bytes=mb << 20)


def _ew(name, fn, rows, consts, outs, accs=(), tile=640):
    first = rows[0][0] if isinstance(rows[0], tuple) else rows[0]
    n = first.shape[-2]
    tile = min(tile, n)
    assert n % tile == 0, (name, n, tile)
    n_in, n_o, n_a = len(rows) + len(consts), len(outs), len(accs)

    def body(*refs):
        i = pl.program_id(0)
        res_o, res_a = fn(i, *[r[...] for r in refs[:n_in]])
        for r, val in zip(refs[n_in:n_in + n_o], res_o):
            r[...] = val.astype(r.dtype)
        if n_a:
            a_refs = refs[n_in + n_o:]

            @pl.when(i == 0)
            def _():
                for r in a_refs:
                    r[...] = jnp.zeros(r.shape, r.dtype)

            for r, val in zip(a_refs, res_a):
                r[...] += val

    in_specs, args = [], []
    for a in rows:
        if isinstance(a, tuple):
            arr, k = a
            in_specs.append(pl.BlockSpec((None, tile, arr.shape[2]), functools.partial(lambda i, k: (k, i, 0), k=k)))
            args.append(arr)
        else:
            in_specs.append(pl.BlockSpec((tile, a.shape[1]), lambda i: (i, 0)))
            args.append(a)
    for c in consts:
        in_specs.append(pl.BlockSpec(c.shape, lambda i: (0, 0)))
        args.append(c)
    out_specs = [pl.BlockSpec((tile, c), lambda i: (i, 0)) for c, _ in outs]
    out_specs += [pl.BlockSpec(s, lambda i: (0, 0)) for s in accs]
    out_shape = [jax.ShapeDtypeStruct((n, c), dt) for c, dt in outs]
    out_shape += [jax.ShapeDtypeStruct(s, f32) for s in accs]
    res = pl.pallas_call(body, grid=(n // tile,), in_specs=in_specs, out_specs=out_specs, out_shape=out_shape,
                         name=name, compiler_params=_params(("arbitrary",)))(*args)
    return res


def _mm(name, a, b, trans_b=False, out_dtype=f32, tm=640):
    m, k = a.shape
    n = b.shape[0] if trans_b else b.shape[1]
    tm = min(tm, m)
    assert m % tm == 0
    dims = NT if trans_b else (((1,), (0,)), ((), ()))

    def body(a_ref, b_ref, o_ref):
        o_ref[...] = lax.dot_general(a_ref[...], b_ref[...], dims, preferred_element_type=f32).astype(o_ref.dtype)

    return pl.pallas_call(
        body, grid=(m // tm,),
        in_specs=[pl.BlockSpec((tm, k), lambda i: (i, 0)), pl.BlockSpec(b.shape, lambda i: (0, 0))],
        out_specs=pl.BlockSpec((tm, n), lambda i: (i, 0)),
        out_shape=jax.ShapeDtypeStruct((m, n), out_dtype), name=name, compiler_params=_params(("parallel",)))(a, b)


def _mm2(name, a1, b1, a2, b2, out_dtype=f32, tm=640):
    m, k = a1.shape
    n = b1.shape[1]
    tm = min(tm, m)
    assert m % tm == 0

    def body(a1_ref, b1_ref, a2_ref, b2_ref, o_ref):
        acc = jnp.dot(a1_ref[...], b1_ref[...], preferred_element_type=f32)
        acc += jnp.dot(a2_ref[...], b2_ref[...], preferred_element_type=f32)
        o_ref[...] = acc.astype(o_ref.dtype)

    row = pl.BlockSpec((tm, k), lambda i: (i, 0))
    whole = pl.BlockSpec(b1.shape, lambda i: (0, 0))
    return pl.pallas_call(
        body, grid=(m // tm,), in_specs=[row, whole, row, whole], out_specs=pl.BlockSpec((tm, n), lambda i: (i, 0)),
        out_shape=jax.ShapeDtypeStruct((m, n), out_dtype), name=name,
        compiler_params=_params(("parallel",)))(a1, b1, a2, b2)


def _mm_tn(name, a, g, tk=512, tl=640):
    rows, k1 = a.shape
    n = g.shape[1]
    tl = min(tl, rows)
    assert rows % tl == 0 and k1 % tk == 0

    def body(a_ref, g_ref, o_ref):
        @pl.when(pl.program_id(1) == 0)
        def _():
            o_ref[...] = jnp.zeros(o_ref.shape, f32)

        o_ref[...] += lax.dot_general(a_ref[...], g_ref[...], TN, preferred_element_type=f32)

    return pl.pallas_call(
        body, grid=(k1 // tk, rows // tl),
        in_specs=[pl.BlockSpec((tl, tk), lambda k, l: (l, k)), pl.BlockSpec((tl, n), lambda k, l: (l, 0))],
        out_specs=pl.BlockSpec((tk, n), lambda k, l: (k, 0)),
        out_shape=jax.ShapeDtypeStruct((k1, n), f32), name=name,
        compiler_params=_params(("parallel", "arbitrary")))(a, g)


def _ln_stats(r):
    mean = jnp.mean(r, axis=-1, keepdims=True)
    c = r - mean
    rstd = lax.rsqrt(jnp.mean(c * c, axis=-1, keepdims=True) + LN_EPS)
    return c * rstd, rstd


def _ln_fwd(h, mix, gain, bias):
    def fn(i, h, mix, g, b):
        r = ALPHA * h + mix
        y = _ln_stats(r)[0] * g + b
        return (r, y, y), ()

    return _ew("ln_fwd", fn, [h, mix], [gain, bias], [(D, f32), (D, f32), (D, bf16)])


def _ln_bwd(d_a, d_b, r, gain):
    def core(dout, r, g):
        xhat, rstd = _ln_stats(r)
        dxh = dout * g
        dr = rstd * (dxh - jnp.mean(dxh, axis=-1, keepdims=True) - xhat * jnp.mean(dxh * xhat, axis=-1, keepdims=True))
        return (dr, dr), (jnp.sum(dout * xhat, axis=0, keepdims=True), jnp.sum(dout, axis=0, keepdims=True))

    outs, accs = [(D, f32), (D, bf16)], [(1, D), (1, D)]
    if d_a is None:
        return _ew("ln_bwd_top", lambda i, d, r, g: core(d, r, g), [d_b, r], [gain], outs, accs)
    return _ew("ln_bwd", lambda i, da, db, r, g: core(ALPHA * da + db, r, g), [d_a, d_b, r], [gain], outs, accs)


def _sigmoid(x):
    return 1.0 / (1.0 + jnp.exp(-x))


def _swiglu_fwd(gate, up):
    def fn(i, g, u):
        g = g.astype(f32)
        return ((g * _sigmoid(g) * u.astype(f32)),), ()

    return _ew("swiglu_fwd", fn, [gate, up], [], [(DFFP, bf16)], tile=320)[0]


def _swiglu_bwd(dact, gate, up):
    def fn(i, da, g, u):
        da, g, u = da.astype(f32), g.astype(f32), u.astype(f32)
        s = _sigmoid(g)
        return (da * u * s * (1.0 + g * (1.0 - s)), da * g * s), ()

    return _ew("swiglu_bwd", fn, [dact, gate, up], [], [(DFFP, bf16), (DFFP, bf16)], tile=320)


def _loss_grad(y, target):
    def fn(i, y, t):
        e = y - t
        return (e * (1.0 / D),), (jnp.sum(e * e, axis=0, keepdims=True),)

    return _ew("loss", fn, [y, target], [], [(D, f32)], [(1, D)], tile=512)


def _adamw(w, g, m, v):
    def fn(i, w, g, m, v):
        m = ADAM_B1 * m + (1.0 - ADAM_B1) * g
        v = ADAM_B2 * v + (1.0 - ADAM_B2) * jnp.square(g)
        m_hat = m / (1.0 - ADAM_B1 ** ADAM_STEP)
        v_hat = v / (1.0 - ADAM_B2 ** ADAM_STEP)
        delta = -ADAM_LR * (m_hat / (jnp.sqrt(v_hat) + ADAM_EPS) + ADAM_WD * w)
        return (delta, m, v), ()

    rows, cols = w.shape
    tile = max(t for t in range(8, min(rows, 544) + 1, 8) if rows % t == 0)
    return _ew("adamw", fn, [w, g, m, v], [], [(cols, f32)] * 3, tile=tile)


def _s5_mats(lam_re, lam_im, log_dt, b_re, b_im, c_re, c_im):
    steps = jnp.arange(CHUNK + 1, dtype=f32)
    shift = (jnp.arange(CHUNK)[None, None, :] - jnp.arange(CHUNK)[None, :, None]
             == jnp.arange(CHUNK)[:, None, None]).astype(f32)

    def one(lr, li, ldt, br, bi, cr, ci, reverse):
        dt = jnp.exp(ldt)[:, None]
        mag = jnp.exp(lr * dt)
        abr, abi = mag * jnp.cos(li * dt), mag * jnp.sin(li * dt)
        nr, ni = abr - 1.0, abi
        den = lr * lr + li * li
        zr, zi = (nr * lr + ni * li) / den, (ni * lr - nr * li) / den
        bbr = zr[..., None] * br - zi[..., None] * bi
        bbi = zr[..., None] * bi + zi[..., None] * br
        pmag = jnp.exp(steps[:, None, None] * (lr * dt)[None])
        pang = steps[:, None, None] * (li * dt)[None]
        pr, pi = pmag * jnp.cos(pang), pmag * jnp.sin(pang)
        car = cr[None] * pr[:, :, None, :] - ci[None] * pi[:, :, None, :]
        cai = cr[None] * pi[:, :, None, :] + ci[None] * pr[:, :, None, :]
        kern = (jnp.einsum('jgop,gpi->jgoi', car[:CHUNK], bbr, precision=HI)
                - jnp.einsum('jgop,gpi->jgoi', cai[:CHUNK], bbi, precision=HI))
        m = jnp.einsum('jgoi,jst->gsito', kern, shift, precision=HI)
        qr, qi = pr[:CHUNK][::-1], pi[:CHUNK][::-1]
        pin_re = qr[:, :, :, None] * bbr[None] - qi[:, :, :, None] * bbi[None]
        pin_im = qr[:, :, :, None] * bbi[None] + qi[:, :, :, None] * bbr[None]
        pin = jnp.stack([pin_re, pin_im], 0).transpose(2, 1, 4, 0, 3)
        pout = jnp.stack([car[1:], -cai[1:]], 0).transpose(2, 0, 4, 1, 3)
        if reverse:
            m, pin, pout = m[:, ::-1, :, ::-1, :], pin[:, ::-1], pout[:, :, :, ::-1, :]
        n = CHUNK * GCH
        return m.reshape(GROUPS, n, n), pin.reshape(GROUPS, n, 2 * NSTATE), pout.reshape(GROUPS, 2 * NSTATE, n), pr[CHUNK], pi[CHUNK]

    mf, pinf, poutf, arf, aif = one(lam_re[0], lam_im[0], log_dt[0], b_re[0], b_im[0], c_re[0], c_im[0], False)
    mr, pinr, poutr, arr, air = one(lam_re[1], lam_im[1], log_dt[1], b_re[1], b_im[1], c_re[1], c_im[1], True)
    return (mf + mr, jnp.concatenate([pinf, pinr], 2), jnp.concatenate([poutf, poutr], 1),
            jnp.stack([arf, arr]), jnp.stack([aif, air]))


def _s5_coefs(a_re, a_im):
    c1 = jnp.concatenate([a_re, a_re], -1)
    c2 = jnp.concatenate([-a_im, a_im], -1)
    return c1[0], c2[0], c1[1], c2[1]


def _swap(s):
    return pltpu.roll(s, NSTATE, 1)


def _s5_states(nc, u_ref, pin_ref, coef, vf, vr, sf, sr):
    c1f, c2f, c1r, c2r = coef
    for g in range(GB):
        v = jnp.dot(u_ref[g], pin_ref[g], preferred_element_type=f32)
        vf[pl.ds(g * nc, nc), :] = v[:, :2 * NSTATE]
        vr[pl.ds(g * nc, nc), :] = v[:, 2 * NSTATE:]

    def step(i, carry):
        s_f, s_r = carry
        kf, kr = i, nc - 1 - i
        sf[pl.ds(kf, GB, stride=nc), :] = s_f
        sr[pl.ds(kr, GB, stride=nc), :] = s_r
        s_f = c1f * s_f + c2f * _swap(s_f) + vf[pl.ds(kf, GB, stride=nc), :]
        s_r = c1r * s_r + c2r * _swap(s_r) + vr[pl.ds(kr, GB, stride=nc), :]
        return s_f, s_r

    z = jnp.zeros((GB, 2 * NSTATE), f32)
    lax.fori_loop(0, nc, step, (z, z))


def _s5_core_fwd(ug, msum, pin, pout, coefs):
    nc = ug.shape[1]
    n = CHUNK * GCH

    def body(u_ref, m_ref, pin_ref, pout_ref, c1f, c2f, c1r, c2r, y_ref, vf, vr, sf, sr):
        coef = (c1f[...], c2f[...], c1r[...], c2r[...])
        _s5_states(nc, u_ref, pin_ref, coef, vf, vr, sf, sr)
        for g in range(GB):
            s_in = jnp.concatenate([sf[pl.ds(g * nc, nc), :], sr[pl.ds(g * nc, nc), :]], axis=1).astype(bf16)
            y_ref[g] = (jnp.dot(u_ref[g], m_ref[g], preferred_element_type=f32)
                        + jnp.dot(s_in, pout_ref[g], preferred_element_type=f32))

    seq = pl.BlockSpec((GB, nc, n), lambda i: (i, 0, 0))
    mat = pl.BlockSpec((GB, n, n), lambda i: (i, 0, 0))
    cf = pl.BlockSpec((GB, 2 * NSTATE), lambda i: (i, 0))
    scr = pltpu.VMEM((GB * nc, 2 * NSTATE), f32)
    return pl.pallas_call(
        body, grid=(GROUPS // GB,), in_specs=[seq, mat, mat, mat, cf, cf, cf, cf], out_specs=seq,
        out_shape=jax.ShapeDtypeStruct((GROUPS, nc, n), f32), scratch_shapes=[scr, scr, scr, scr],
        name="s5_core_fwd", compiler_params=_params(("parallel",)))(ug, msum, pin, pout, *coefs)


def _s5_core_bwd(ug, dyg, msum, pin, pout, coefs):
    nc = ug.shape[1]
    n = CHUNK * GCH

    def body(u_ref, dy_ref, m_ref, pin_ref, pout_ref, c1f, c2f, c1r, c2r,
             du_ref, dm_ref, dpin_ref, dpout_ref, a1f_ref, a2f_ref, a1r_ref, a2r_ref, vf, vr, sf, sr):
        coef = (c1f[...], c2f[...], c1r[...], c2r[...])
        _s5_states(nc, u_ref, pin_ref, coef, vf, vr, sf, sr)
        for g in range(GB):
            s_in = jnp.concatenate([sf[pl.ds(g * nc, nc), :], sr[pl.ds(g * nc, nc), :]], axis=1).astype(bf16)
            dy = dy_ref[g]
            ds = lax.dot_general(dy, pout_ref[g], NT, preferred_element_type=f32)
            vf[pl.ds(g * nc, nc), :] = ds[:, :2 * NSTATE]
            vr[pl.ds(g * nc, nc), :] = ds[:, 2 * NSTATE:]
            dpout_ref[g] = lax.dot_general(s_in, dy, TN, preferred_element_type=f32)
            dm_ref[g] = lax.dot_general(u_ref[g], dy, TN, preferred_element_type=f32)

        k1f, k2f, k1r, k2r = coef[0], -coef[1], coef[2], -coef[3]

        def step(i, carry):
            g_f, g_r, a1f, a2f, a1r, a2r = carry
            kf, kr = nc - 1 - i, i
            s_f = sf[pl.ds(kf, GB, stride=nc), :]
            s_r = sr[pl.ds(kr, GB, stride=nc), :]
            d_f = vf[pl.ds(kf, GB, stride=nc), :]
            d_r = vr[pl.ds(kr, GB, stride=nc), :]
            sf[pl.ds(kf, GB, stride=nc), :] = g_f
            sr[pl.ds(kr, GB, stride=nc), :] = g_r
            a1f, a2f = a1f + g_f * s_f, a2f + g_f * _swap(s_f)
            a1r, a2r = a1r + g_r * s_r, a2r + g_r * _swap(s_r)
            g_f = d_f + k1f * g_f + k2f * _swap(g_f)
            g_r = d_r + k1r * g_r + k2r * _swap(g_r)
            return g_f, g_r, a1f, a2f, a1r, a2r

        z = jnp.zeros((GB, 2 * NSTATE), f32)
        _, _, a1f, a2f, a1r, a2r = lax.fori_loop(0, nc, step, (z, z, z, z, z, z))
        a1f_ref[...], a2f_ref[...], a1r_ref[...], a2r_ref[...] = a1f, a2f, a1r, a2r
        for g in range(GB):
            dv = jnp.concatenate([sf[pl.ds(g * nc, nc), :], sr[pl.ds(g * nc, nc), :]], axis=1).astype(bf16)
            du_ref[g] = (lax.dot_general(dy_ref[g], m_ref[g], NT, preferred_element_type=f32)
                         + lax.dot_general(dv, pin_ref[g], NT, preferred_element_type=f32))
            dpin_ref[g] = lax.dot_general(u_ref[g], dv, TN, preferred_element_type=f32)

    seq = pl.BlockSpec((GB, nc, n), lambda i: (i, 0, 0))
    mat = pl.BlockSpec((GB, n, n), lambda i: (i, 0, 0))
    cf = pl.BlockSpec((GB, 2 * NSTATE), lambda i: (i, 0))
    scr = pltpu.VMEM((GB * nc, 2 * NSTATE), f32)
    mat_s = jax.ShapeDtypeStruct((GROUPS, n, n), f32)
    cf_s = jax.ShapeDtypeStruct((GROUPS, 2 * NSTATE), f32)
    return pl.pallas_call(
        body, grid=(GROUPS // GB,), in_specs=[seq, seq, mat, mat, mat, cf, cf, cf, cf],
        out_specs=[seq, mat, mat, mat, cf, cf, cf, cf],
        out_shape=[jax.ShapeDtypeStruct((GROUPS, nc, n), f32), mat_s, mat_s, mat_s, cf_s, cf_s, cf_s, cf_s],
        scratch_shapes=[scr, scr, scr, scr], name="s5_core_bwd",
        compiler_params=_params(("parallel",)))(ug, dyg, msum, pin, pout, *coefs)


def _to_groups(a):
    n = a.shape[0]
    return a.reshape(n // CHUNK, CHUNK, GROUPS, GCH).transpose(2, 0, 1, 3).reshape(GROUPS, n // CHUNK, CHUNK * GCH)


def _from_groups(g):
    nc = g.shape[1]
    return g.reshape(GROUPS, nc, CHUNK, GCH).transpose(1, 2, 0, 3).reshape(nc * CHUNK, D)


def _gelu(y):
    return 0.5 * y * (1.0 + lax.erf(y * (2.0 ** -0.5)))


def _gelu_grad(y):
    return 0.5 * (1.0 + lax.erf(y * (2.0 ** -0.5))) + y * jnp.exp(-0.5 * y * y) * (1.0 / math.sqrt(2.0 * math.pi))


def _s5_fwd(h, h_bf, valid, mats, d_skip, w_glu, w_out):
    msum, pin, pout, a_re, a_im = mats
    coefs = _s5_coefs(a_re, a_im)
    ug = _to_groups(jnp.where(valid, h_bf, jnp.zeros_like(h_bf)))
    ys = _from_groups(_s5_core_fwd(ug, msum.astype(bf16), pin.astype(bf16), pout.astype(bf16), coefs))

    def post(i, ys, h, d):
        y = ys + d * h
        return (y, _gelu(y)), ()

    y, g_bf = _ew("s5_gelu", post, [ys, h], [d_skip], [(D, f32), (D, bf16)])
    gw = _mm("s5_glu_mm", g_bf, w_glu)

    def glu(i, y, gw):
        return (_gelu(y) * _sigmoid(gw),), ()

    z_bf = _ew("s5_glu", glu, [y, gw], [], [(D, bf16)])[0]
    mix = _mm("s5_out_mm", z_bf, w_out)
    return mix, (ug, y, g_bf, gw, z_bf)


def _s5_bwd(dmix_bf, h, valid, saved, mats, vjp_mats, d_skip, w_glu, w_out):
    ug, y, g_bf, gw, z_bf = saved
    msum, pin, pout, a_re, a_im = mats
    coefs = _s5_coefs(a_re, a_im)
    dz = _mm("s5_dz_mm", dmix_bf, w_out, trans_b=True)
    d_w_out = _mm_tn("s5_dwout", z_bf, dmix_bf)

    def dglu(i, dz, y, gw):
        g, s = _gelu(y), _sigmoid(gw)
        return (dz * g * s * (1.0 - s), dz * s), ()

    dgw_bf, dg1 = _ew("s5_dglu", dglu, [dz, y, gw], [], [(D, bf16), (D, f32)])
    d_w_glu = _mm_tn("s5_dwglu", g_bf, dgw_bf)
    dg2 = _mm("s5_dg_mm", dgw_bf, w_glu, trans_b=True)

    def dgelu(i, dg1, dg2, y, h, d):
        dy = (dg1 + dg2) * _gelu_grad(y)
        return (dy, dy * d), (jnp.sum(dy * h, axis=0, keepdims=True),)

    dy_bf, dh_skip, dd = _ew("s5_dgelu", dgelu, [dg1, dg2, y, h], [d_skip], [(D, bf16), (D, f32)], [(1, D)])
    dug, dm, dpin, dpout, a1f, a2f, a1r, a2r = _s5_core_bwd(
        ug, _to_groups(dy_bf), msum.astype(bf16), pin.astype(bf16), pout.astype(bf16), coefs)
    du = _from_groups(dug)
    dh = dh_skip + jnp.where(valid, du, jnp.zeros_like(du))
    a1, a2 = jnp.stack([a1f, a1r]), jnp.stack([a2f, a2r])
    da_re = a1[..., :NSTATE] + a1[..., NSTATE:]
    da_im = a2[..., NSTATE:] - a2[..., :NSTATE]
    d_params = vjp_mats((dm, dpin, dpout, da_re, da_im))
    return dh, d_params, dd[0], d_w_glu, d_w_out


def _rope_tables(n):
    row = jnp.arange(n, dtype=jnp.int32) - OFF
    real = row >= 0
    rid = jnp.where(real, row // GRID_W, 0).astype(f32)
    cid = jnp.where(real, row % GRID_W, 0).astype(f32)
    half = HD // 2
    inv = ROPE_THETA ** (-jnp.arange(0, half, 2, dtype=f32) / half)
    ar, ac = rid[:, None] * inv[None, :], cid[:, None] * inv[None, :]
    cos = jnp.concatenate([jnp.cos(ar), jnp.cos(ar), jnp.cos(ac), jnp.cos(ac)], axis=1)
    sin = jnp.concatenate([-jnp.sin(ar), jnp.sin(ar), -jnp.sin(ac), jnp.sin(ac)], axis=1)
    return jnp.tile(cos, (1, 2)), jnp.tile(sin, (1, 2))


def _head_mats():
    head = jnp.arange(QW, dtype=jnp.int32)[:, None] // HD == jnp.arange(128, dtype=jnp.int32)[None, :]
    return head.astype(f32) * (1.0 / HD), head.astype(f32).T


def _rot(v):
    w = v.shape[1]
    lane = lax.broadcasted_iota(jnp.int32, v.shape, 1)
    return jnp.where(lane % 32 < 16, pltpu.roll(v, w - 16, 1), pltpu.roll(v, 16, 1))


def _head_mean(v, e, et):
    w = v.shape[1]
    m = jnp.dot(v, e[:w], preferred_element_type=f32, precision=HI)
    return m, et[:, :w]


def _rms_rope(t, gain, e, et, cos, sin):
    w = t.shape[1]
    ms, spread = _head_mean(t * t, e, et)
    rs = jnp.dot(lax.rsqrt(ms + QK_EPS), spread, preferred_element_type=f32, precision=HI)
    n0 = t * rs
    n = n0 * gain
    reps = w // 128
    return n * jnp.tile(cos, (1, reps)) + _rot(n) * jnp.tile(sin, (1, reps))


def _rms_rope_bwd(dout, t, gain, e, et, cos, sin):
    w = t.shape[1]
    reps = w // 128
    ms, spread = _head_mean(t * t, e, et)
    rs = jnp.dot(lax.rsqrt(ms + QK_EPS), spread, preferred_element_type=f32, precision=HI)
    n0 = t * rs
    dn = dout * jnp.tile(cos, (1, reps)) + _rot(dout * jnp.tile(sin, (1, reps)))
    dn0 = dn * gain
    mm, _ = _head_mean(dn0 * n0, e, et)
    corr = jnp.dot(mm, spread, preferred_element_type=f32, precision=HI)
    return rs * (dn0 - n0 * corr), jnp.sum(dn * n0, axis=0, keepdims=True)


def _qk_fwd(qkv, qg, kg, e, et, cos, sin):
    def fn(i, qkv, cos, sin, qg, kg, e, et):
        q = _rms_rope(qkv[:, :QW], qg, e, et, cos, sin) * (HD ** -0.5)
        k = _rms_rope(qkv[:, QW:QW + KW], kg, e, et, cos, sin)
        return (q, k, qkv[:, QW + KW:]), ()

    return _ew("qk_rope", fn, [qkv, cos, sin], [qg, kg, e, et], [(QW, bf16), (KW, bf16), (KW, bf16)])


def _qk_bwd(qkv, dq, dk, dv, qg, kg, e, et, cos, sin):
    def fn(i, qkv, cos, sin, dq, dk, dv, qg, kg, e, et):
        dtq, dgq = _rms_rope_bwd(dq * (HD ** -0.5), qkv[:, :QW], qg, e, et, cos, sin)
        dtk, dgk = _rms_rope_bwd(dk, qkv[:, QW:QW + KW], kg, e, et, cos, sin)
        return (jnp.concatenate([dtq, dtk, dv], axis=1),), (dgq, dgk)

    return _ew("qk_rope_bwd", fn, [qkv, cos, sin, dq, dk, dv], [qg, kg, e, et], [(QKVW, bf16)], [(1, QW), (1, KW)])


def _to_heads(a, nh):
    return a.reshape(a.shape[0], nh, HD).transpose(1, 0, 2)


def _from_heads(a):
    return a.transpose(1, 0, 2).reshape(a.shape[1], a.shape[0] * HD)


def _flash_fwd(q, k, v, tq=640, tk=640):
    n = q.shape[1]
    tq, tk = min(tq, n), min(tk, n)
    nkb = n // tk

    def body(q_ref, k_ref, v_ref, o_ref, lse_ref):
        qb = q_ref[0]

        def block(kb, carry, first):
            m, l, acc = carry
            start = pl.multiple_of(kb * tk, tk)
            ks = k_ref[0, pl.ds(start, tk), :]
            vs = v_ref[0, pl.ds(start, tk), :]
            s = lax.dot_general(qb, ks, NT, preferred_element_type=f32)
            if first:
                col = lax.broadcasted_iota(jnp.int32, (1, tk), 1)
                s = jnp.where(col >= PAD, s, NEG)
            m_new = jnp.maximum(m, jnp.max(s, axis=1, keepdims=True))
            p = jnp.exp(s - m_new)
            scale = jnp.exp(m - m_new)
            l = scale * l + jnp.sum(p, axis=1, keepdims=True)
            acc = scale * acc + jnp.dot(p.astype(bf16), vs, preferred_element_type=f32)
            return m_new, l, acc

        carry = (jnp.full((tq, 1), NEG, f32), jnp.zeros((tq, 1), f32), jnp.zeros((tq, HD), f32))
        carry = block(0, carry, True)
        m, l, acc = lax.fori_loop(1, nkb, lambda kb, c: block(kb, c, False), carry)
        o_ref[0] = acc / l
        lse_ref[0] = m + jnp.log(l)

    return pl.pallas_call(
        body, grid=(NQ, n // tq),
        in_specs=[pl.BlockSpec((1, tq, HD), lambda h, i: (h, i, 0)),
                  pl.BlockSpec((1, n, HD), lambda h, i: (h // (NQ // NKV), 0, 0)),
                  pl.BlockSpec((1, n, HD), lambda h, i: (h // (NQ // NKV), 0, 0))],
        out_specs=[pl.BlockSpec((1, tq, HD), lambda h, i: (h, i, 0)), pl.BlockSpec((1, tq, 1), lambda h, i: (h, i, 0))],
        out_shape=[jax.ShapeDtypeStruct((NQ, n, HD), f32), jax.ShapeDtypeStruct((NQ, n, 1), f32)],
        name="flash_fwd", compiler_params=_params(("parallel", "parallel")))(q, k, v)


def _flash_dq(q, k, v, do, o, lse, tq=640, tk=640):
    n = q.shape[1]
    tq, tk = min(tq, n), min(tk, n)
    nkb = n // tk

    def body(q_ref, k_ref, v_ref, do_ref, o_ref, lse_ref, dq_ref, delta_ref):
        qb, dob, lse_b = q_ref[0], do_ref[0], lse_ref[0]
        delta = jnp.sum(dob.astype(f32) * o_ref[0], axis=1, keepdims=True)

        def block(kb, dq, first):
            start = pl.multiple_of(kb * tk, tk)
            ks = k_ref[0, pl.ds(start, tk), :]
            vs = v_ref[0, pl.ds(start, tk), :]
            s = lax.dot_general(qb, ks, NT, preferred_element_type=f32)
            if first:
                col = lax.broadcasted_iota(jnp.int32, (1, tk), 1)
                s = jnp.where(col >= PAD, s, NEG)
            p = jnp.exp(s - lse_b)
            dp = lax.dot_general(dob, vs, NT, preferred_element_type=f32)
            ds = p * (dp - delta)
            return dq + jnp.dot(ds.astype(bf16), ks, preferred_element_type=f32)

        dq = block(0, jnp.zeros((tq, HD), f32), True)
        dq_ref[0] = lax.fori_loop(1, nkb, lambda kb, c: block(kb, c, False), dq)
        delta_ref[0] = delta

    qspec = pl.BlockSpec((1, tq, HD), lambda h, i: (h, i, 0))
    kspec = pl.BlockSpec((1, n, HD), lambda h, i: (h // (NQ // NKV), 0, 0))
    cspec = pl.BlockSpec((1, tq, 1), lambda h, i: (h, i, 0))
    return pl.pallas_call(
        body, grid=(NQ, n // tq), in_specs=[qspec, kspec, kspec, qspec, qspec, cspec], out_specs=[qspec, cspec],
        out_shape=[jax.ShapeDtypeStruct((NQ, n, HD), f32), jax.ShapeDtypeStruct((NQ, n, 1), f32)],
        name="flash_dq", compiler_params=_params(("parallel", "parallel")))(q, k, v, do, o, lse)


def _flash_dkv(q, k, v, do, lse_row, delta_row, tq=640, tk=640):
    n = q.shape[1]
    tq, tk = min(tq, n), min(tk, n)
    nqb = n // tq
    grp = NQ // NKV

    def body(q_ref, do_ref, lse_ref, delta_ref, k_ref, v_ref, dk_ref, dv_ref):
        kb, vb = k_ref[0], v_ref[0]
        dk = jnp.zeros((tk, HD), f32)
        dv = jnp.zeros((tk, HD), f32)
        for g in range(grp):
            def block(qi, carry, g=g):
                dk, dv = carry
                start = pl.multiple_of(qi * tq, tq)
                qs = q_ref[g, pl.ds(start, tq), :]
                dos = do_ref[g, pl.ds(start, tq), :]
                lse_r = lse_ref[g, :, pl.ds(start, tq)]
                delta_r = delta_ref[g, :, pl.ds(start, tq)]
                st = lax.dot_general(kb, qs, NT, preferred_element_type=f32)
                pt = jnp.exp(st - lse_r)
                dv = dv + jnp.dot(pt.astype(bf16), dos, preferred_element_type=f32)
                dpt = lax.dot_general(vb, dos, NT, preferred_element_type=f32)
                dst = pt * (dpt - delta_r)
                dk = dk + jnp.dot(dst.astype(bf16), qs, preferred_element_type=f32)
                return dk, dv

            dk, dv = lax.fori_loop(0, nqb, block, (dk, dv))
        row = lax.broadcasted_iota(jnp.int32, (tk, 1), 0) + pl.program_id(1) * tk
        dk_ref[0] = jnp.where(row >= PAD, dk, 0.0)
        dv_ref[0] = jnp.where(row >= PAD, dv, 0.0)

    gspec = pl.BlockSpec((grp, n, HD), lambda h, j: (h, 0, 0))
    rspec = pl.BlockSpec((grp, 1, n), lambda h, j: (h, 0, 0))
    kspec = pl.BlockSpec((1, tk, HD), lambda h, j: (h, j, 0))
    return pl.pallas_call(
        body, grid=(NKV, n // tk), in_specs=[gspec, gspec, rspec, rspec, kspec, kspec], out_specs=[kspec, kspec],
        out_shape=[jax.ShapeDtypeStruct((NKV, n, HD), f32)] * 2,
        name="flash_dkv", compiler_params=_params(("parallel", "parallel")))(q, do, lse_row, delta_row, k, v)


def _attn_fwd(h_bf, w_qkv_t, qg, kg, w_out, tabs):
    e, et, cos, sin = tabs
    qkv = _mm("attn_qkv_mm", h_bf, w_qkv_t, trans_b=True)
    q_bf, k_bf, v_bf = _qk_fwd(qkv, qg, kg, e, et, cos, sin)
    q16, k4, v4 = _to_heads(q_bf, NQ), _to_heads(k_bf, NKV), _to_heads(v_bf, NKV)
    o16, lse = _flash_fwd(q16, k4, v4)
    o_bf = _from_heads(o16).astype(bf16)
    mix = _mm("attn_out_mm", o_bf, w_out)
    return mix, (qkv, q16, k4, v4, o16, lse, o_bf)


def _attn_bwd(dmix_bf, h_bf, saved, w_qkv_t, qg, kg, w_out, tabs):
    e, et, cos, sin = tabs
    qkv, q16, k4, v4, o16, lse, o_bf = saved
    n = qkv.shape[0]
    do = _mm("attn_do_mm", dmix_bf, w_out, trans_b=True, out_dtype=bf16)
    d_w_out = _mm_tn("attn_dwout", o_bf, dmix_bf)
    do16 = _to_heads(do, NQ)
    dq16, delta = _flash_dq(q16, k4, v4, do16, o16, lse)
    dk4, dv4 = _flash_dkv(q16, k4, v4, do16, lse.reshape(NQ, 1, n), delta.reshape(NQ, 1, n))
    dqkv_bf, dgq, dgk = _qk_bwd(qkv, _from_heads(dq16), _from_heads(dk4), _from_heads(dv4), qg, kg, e, et, cos, sin)
    d_w_qkv_t = _mm_tn("attn_dwqkv", dqkv_bf, h_bf)
    dh = _mm("attn_dh_mm", dqkv_bf, w_qkv_t)
    return dh, d_w_qkv_t, dgq.reshape(NQ, HD).sum(0), dgk.reshape(NKV, HD).sum(0), d_w_out


def _all_gather(name, shard):
    def body(x_ref, out_ref, send_sems, recv_sems, local_sem):
        x, y, c = lax.axis_index("x"), lax.axis_index("y"), lax.axis_index("c")
        me, sibling = (x, y, c), (x, y, 1 - c)
        chips = [(1 - x, y), (x, 1 - y), (1 - x, 1 - y)]

        def slot(px, py, pc):
            return out_ref.at[4 * px + 2 * py + pc]

        def copy(k, block, to, src=None):
            return pltpu.make_async_remote_copy(
                src_ref=slot(*block) if src is None else src, dst_ref=slot(*block),
                send_sem=send_sems.at[k], recv_sem=recv_sems.at[k], device_id=to, device_id_type=MESH)

        mine = pltpu.make_async_copy(x_ref, slot(*me), local_sem)
        mine.start()
        first = [copy(0, me, sibling, src=x_ref)]
        first += [copy(1 + j, me, (*chip, c), src=x_ref) for j, chip in enumerate(chips)]
        for cp in first:
            cp.start()
        passed = [copy(4 + j, (*chip, c), sibling) for j, chip in enumerate(chips)]
        for j, chip in enumerate(chips):
            copy(1 + j, (*chip, c), me).wait_recv()
            passed[j].start()
        copy(0, sibling, me).wait_recv()
        for j, chip in enumerate(chips):
            copy(4 + j, (*chip, 1 - c), me).wait_recv()
        for cp in first + passed:
            cp.wait_send()
        mine.wait()

    return pl.pallas_call(
        body, out_shape=jax.ShapeDtypeStruct((8,) + shard.shape, shard.dtype), in_specs=[ANY], out_specs=ANY,
        scratch_shapes=[pltpu.SemaphoreType.DMA((7,)), pltpu.SemaphoreType.DMA((7,)), pltpu.SemaphoreType.DMA],
        name=name)(shard)


def _swap_sibling(name, theirs):
    def body(src_ref, dst_ref, send_sem, recv_sem):
        x, y, c = lax.axis_index("x"), lax.axis_index("y"), lax.axis_index("c")
        cp = pltpu.make_async_remote_copy(src_ref=src_ref, dst_ref=dst_ref, send_sem=send_sem, recv_sem=recv_sem,
                                          device_id=(x, y, 1 - c), device_id_type=MESH)
        cp.start()
        cp.wait()

    return pl.pallas_call(
        body, out_shape=jax.ShapeDtypeStruct(theirs.shape, theirs.dtype), in_specs=[ANY], out_specs=ANY,
        scratch_shapes=[pltpu.SemaphoreType.DMA, pltpu.SemaphoreType.DMA], name=name)(theirs)


def _exchange_chips(name, part):
    def body(p_ref, t_ref, send_sems, recv_sems, local_sem):
        x, y, c = lax.axis_index("x"), lax.axis_index("y"), lax.axis_index("c")
        q = 2 * x + y
        own = pltpu.make_async_copy(p_ref.at[q], t_ref.at[q], local_sem)
        own.start()
        copies = []
        for k in (1, 2, 3):
            tx, ty = x ^ (k >> 1), y ^ (k & 1)
            copies.append(pltpu.make_async_remote_copy(
                src_ref=p_ref.at[2 * tx + ty], dst_ref=t_ref.at[q], send_sem=send_sems.at[k - 1],
                recv_sem=recv_sems.at[k - 1], device_id=(tx, ty, c), device_id_type=MESH))
        for cp in copies:
            cp.start()
        for cp in copies:
            cp.wait()
        own.wait()

    return pl.pallas_call(
        body, out_shape=jax.ShapeDtypeStruct(part.shape, part.dtype), in_specs=[ANY], out_specs=ANY,
        scratch_shapes=[pltpu.SemaphoreType.DMA((3,)), pltpu.SemaphoreType.DMA((3,)), pltpu.SemaphoreType.DMA],
        name=name)(part)


def _reduce_scatter(mine, theirs):
    got = _swap_sibling("rs_sibling", theirs)
    rows = 4 * RS_ROWS
    part = _ew("rs_add2", lambda i, a, b: ((a + b,), ()), [mine.reshape(rows, D), got.reshape(rows, D)], [],
               [(D, f32)], tile=RS_TILE)[0]
    t = _exchange_chips("rs_chips", part.reshape(4, RS_ROWS, D))
    return _ew("rs_add4", lambda i, a, b, c, d: ((((a + b) + c) + d,), ()), [(t, 0), (t, 1), (t, 2), (t, 3)], [],
               [(D, f32)], tile=RS_TILE)[0]


def _pack_rows(parts, rows):
    flat = jnp.concatenate([p.reshape(-1) for p in parts])
    return jnp.pad(flat, (0, rows * D - flat.shape[0])).reshape(rows, D)


def _unpack(flat, shapes):
    out, off = [], 0
    for s in shapes:
        n = math.prod(s)
        out.append(flat[off:off + n].reshape(s))
        off += n
    return out


def _mat_rows(block, transposed, blk):
    a = jnp.swapaxes(block, 1, 2) if transposed else block
    a = jnp.pad(a, ((0, 0), (0, blk - a.shape[1]), (0, 0)))
    return a.reshape(-1, D)


def _mat_block(rows, transposed, blk, real):
    a = rows.reshape(-1, blk, D)[:, :real]
    return jnp.swapaxes(a, 1, 2) if transposed else a


def _mat_full(gathered, blk):
    layers = gathered.shape[1] // blk
    return gathered.reshape(8, layers, blk, D).transpose(1, 0, 2, 3).reshape(layers, 8 * blk, D)


def _vec_full(gathered):
    return gathered.transpose(1, 0, 2).reshape(gathered.shape[1], D)


def _grad_slots(full, small, cc):
    def halves(a, blk):
        layers = a.shape[0]
        a5 = a.reshape(layers, 4, 2, blk, D)
        res = []
        for sel in (cc, 1 - cc):
            s = lax.dynamic_index_in_dim(a5, sel, axis=2, keepdims=False)
            res.append(s.transpose(1, 0, 2, 3).reshape(4, layers * blk, D))
        return res

    mine, theirs = [], []
    for name, _, blk, _ in MATS:
        a, b = halves(full[name], blk)
        mine.append(a)
        theirs.append(b)
    a, b = halves(small[None], REP_PIECE)
    return jnp.concatenate(mine + [a], axis=1), jnp.concatenate(theirs + [b], axis=1)


def _local_step(x0, target0, w, fw):
    seq = x0.shape[0]
    n = OFF + seq
    valid = (jnp.arange(n, dtype=jnp.int32) >= PAD)[:, None]
    h = jnp.concatenate([jnp.zeros((PAD, D), f32), fw['meta_tokens'], x0], axis=0)
    h_bf = h.astype(bf16)
    tabs = _head_mats() + _rope_tables(n)
    qg = [jnp.tile(w['attn_q_gain'][j], NQ)[None, :] for j in range(2)]
    kg = [jnp.tile(w['attn_k_gain'][j], NKV)[None, :] for j in range(2)]
    s5_names = ['s5_lambda_re', 's5_lambda_im', 's5_log_dt', 's5_b_re', 's5_b_im', 's5_c_re', 's5_c_im']
    s5_mats, s5_vjp = [], []
    for j in range(2):
        mats, vjp = jax.vjp(_s5_mats, *[w[k][j] for k in s5_names])
        s5_mats.append(mats)
        s5_vjp.append(vjp)
    saved = []
    for i in range(DEPTH):
        j = i // 2
        if i % 2 == 0:
            mix, sv = _s5_fwd(h, h_bf, valid, s5_mats[j], w['s5_d'][j][None, :], fw['s5_w_glu'][j], fw['s5_w_out'][j])
        else:
            mix, sv = _attn_fwd(h_bf, fw['attn_w_qkv'][j], qg[j], kg[j], fw['attn_w_out'][j], tabs)
        r1, h1, h1_bf = _ln_fwd(h, mix, fw['ln_gain'][i, 0][None, :], fw['ln_bias'][i, 0][None, :])
        gate = _mm("ffn_gate_mm", h1_bf, fw['ffn_w_gate'][i], trans_b=True, out_dtype=bf16)
        up = _mm("ffn_up_mm", h1_bf, fw['ffn_w_up'][i], trans_b=True, out_dtype=bf16)
        act = _swiglu_fwd(gate, up)
        f = _mm("ffn_down_mm", act, fw['ffn_w_down'][i])
        r2, h2, h2_bf = _ln_fwd(h1, f, fw['ln_gain'][i, 1][None, :], fw['ln_bias'][i, 1][None, :])
        saved.append((h, h_bf, sv, r1, h1_bf, gate, up, act, r2))
        h, h_bf = h2, h2_bf

    dy, sq = _loss_grad(h[OFF:], target0)
    loss = 0.5 * jnp.sum(sq) * (1.0 / D)

    grads = {k: [None] * (DEPTH if k.startswith('ffn') else 2) for k in WEIGHTS}
    d_ln_gain = [[None, None] for _ in range(DEPTH)]
    d_ln_bias = [[None, None] for _ in range(DEPTH)]
    d_a, d_b = None, jnp.pad(dy, ((OFF, 0), (0, 0)))
    for i in reversed(range(DEPTH)):
        j = i // 2
        h_in, h_in_bf, sv, r1, h1_bf, gate, up, act, r2 = saved[i]
        dr2, dr2_bf, dg, db = _ln_bwd(d_a, d_b, r2, fw['ln_gain'][i, 1][None, :])
        d_ln_gain[i][1], d_ln_bias[i][1] = dg[0], db[0]
        dact = _mm("ffn_dact_mm", dr2_bf, fw['ffn_w_down'][i], trans_b=True, out_dtype=bf16)
        dgate, dup = _swiglu_bwd(dact, gate, up)
        grads['ffn_w_down'][i] = _mm_tn("ffn_dwdown", act, dr2_bf, tk=DFFP // 2)
        grads['ffn_w_gate'][i] = _mm_tn("ffn_dwgate", dgate, h1_bf, tk=DFFP // 2)
        grads['ffn_w_up'][i] = _mm_tn("ffn_dwup", dup, h1_bf, tk=DFFP // 2)
        dh1 = _mm2("ffn_dh_mm", dgate, fw['ffn_w_gate'][i], dup, fw['ffn_w_up'][i])
        dr1, dr1_bf, dg, db = _ln_bwd(dr2, dh1, r1, fw['ln_gain'][i, 0][None, :])
        d_ln_gain[i][0], d_ln_bias[i][0] = dg[0], db[0]
        if i % 2 == 0:
            dh, d_par, dd, d_w_glu, d_w_out = _s5_bwd(dr1_bf, h_in, valid, sv, s5_mats[j], s5_vjp[j],
                                                      w['s5_d'][j][None, :], fw['s5_w_glu'][j], fw['s5_w_out'][j])
            for k, g in zip(s5_names, d_par):
                grads[k][j] = g
            grads['s5_d'][j], grads['s5_w_glu'][j], grads['s5_w_out'][j] = dd, d_w_glu, d_w_out
        else:
            dh, d_w_qkv, dgq, dgk, d_w_out = _attn_bwd(dr1_bf, h_in_bf, sv, fw['attn_w_qkv'][j], qg[j], kg[j],
                                                       fw['attn_w_out'][j], tabs)
            grads['attn_w_qkv'][j], grads['attn_w_out'][j] = d_w_qkv, d_w_out
            grads['attn_q_gain'][j], grads['attn_k_gain'][j] = dgq, dgk
        d_a, d_b = dr1, dh
    dh0 = _ew("dh0", lambda i, a, b: ((ALPHA * a + b,), ()), [d_a, d_b], [], [(D, f32)])[0]
    full = {k: jnp.stack(v) for k, v in grads.items() if v[0] is not None}
    full['meta_tokens'] = dh0[PAD:OFF]
    full['ln_gain'] = jnp.stack([jnp.stack(r) for r in d_ln_gain])
    full['ln_bias'] = jnp.stack([jnp.stack(r) for r in d_ln_bias])

    return loss, dh0[OFF:], full


def kernel(x, meta_tokens, s5_lambda_re, s5_lambda_im, s5_log_dt, s5_b_re, s5_b_im, s5_c_re, s5_c_im, s5_d, s5_w_glu, s5_w_out, attn_w_qkv, attn_q_gain, attn_k_gain, attn_w_out, ffn_w_gate, ffn_w_up, ffn_w_down, ln_gain, ln_bias, loss_target, m_meta_tokens, m_s5_lambda_re, m_s5_lambda_im, m_s5_log_dt, m_s5_b_re, m_s5_b_im, m_s5_c_re, m_s5_c_im, m_s5_d, m_s5_w_glu, m_s5_w_out, m_attn_w_qkv, m_attn_q_gain, m_attn_k_gain, m_attn_w_out, m_ffn_w_gate, m_ffn_w_up, m_ffn_w_down, m_ln_gain, m_ln_bias, v_meta_tokens, v_s5_lambda_re, v_s5_lambda_im, v_s5_log_dt, v_s5_b_re, v_s5_b_im, v_s5_c_re, v_s5_c_im, v_s5_d, v_s5_w_glu, v_s5_w_out, v_attn_w_qkv, v_attn_q_gain, v_attn_k_gain, v_attn_w_out, v_ffn_w_gate, v_ffn_w_up, v_ffn_w_down, v_ln_gain, v_ln_bias):
    w = dict(zip(WEIGHTS, (meta_tokens, s5_lambda_re, s5_lambda_im, s5_log_dt, s5_b_re, s5_b_im, s5_c_re, s5_c_im, s5_d, s5_w_glu, s5_w_out, attn_w_qkv, attn_q_gain, attn_k_gain, attn_w_out, ffn_w_gate, ffn_w_up, ffn_w_down, ln_gain, ln_bias)))
    mom = dict(zip(WEIGHTS, (m_meta_tokens, m_s5_lambda_re, m_s5_lambda_im, m_s5_log_dt, m_s5_b_re, m_s5_b_im, m_s5_c_re, m_s5_c_im, m_s5_d, m_s5_w_glu, m_s5_w_out, m_attn_w_qkv, m_attn_q_gain, m_attn_k_gain, m_attn_w_out, m_ffn_w_gate, m_ffn_w_up, m_ffn_w_down, m_ln_gain, m_ln_bias)))
    vel = dict(zip(WEIGHTS, (v_meta_tokens, v_s5_lambda_re, v_s5_lambda_im, v_s5_log_dt, v_s5_b_re, v_s5_b_im, v_s5_c_re, v_s5_c_im, v_s5_d, v_s5_w_glu, v_s5_w_out, v_attn_w_qkv, v_attn_q_gain, v_attn_k_gain, v_attn_w_out, v_ffn_w_gate, v_ffn_w_up, v_ffn_w_down, v_ln_gain, v_ln_bias)))
    cc = lax.axis_index("c")
    dev = 4 * lax.axis_index("x") + 2 * lax.axis_index("y") + cc

    mat_rows = jnp.concatenate([_mat_rows(w[n], t, blk) for n, t, blk, _ in MATS]).astype(bf16)
    g_mats = _all_gather("ag_weights", mat_rows)
    g_vecs = _all_gather("ag_vectors", jnp.concatenate([w[n].reshape(-1, 128) for n in VECS]))
    fw, off = {}, 0
    for n, _, blk, _ in MATS:
        rows = w[n].shape[0] * blk
        fw[n] = _mat_full(g_mats[:, off:off + rows], blk)
        off += rows
    off = 0
    for n in VECS:
        rows = w[n].size // 128
        fw[n] = _vec_full(g_vecs[:, off:off + rows]).reshape(w[n].shape[:-1] + (D,))
        off += rows

    loss, grad_x, full = _local_step(x[0], loss_target[0], w, fw)
    loss = lax.psum(loss, AXES)
    grad_x = grad_x[None]

    small_names = REPL + VECS
    mine, theirs = _grad_slots(full, _pack_rows([full[k] for k in small_names], REP_ROWS), cc)
    red = _reduce_scatter(mine, theirs)
    small_all = _all_gather("ag_small_grads", red[MAT_ROWS:]).reshape(REP_ROWS * D)
    g, off = {}, 0
    for n, t, blk, real in MATS:
        rows = w[n].shape[0] * blk
        g[n] = _mat_block(red[off:off + rows], t, blk, real)
        off += rows
    small = dict(zip(small_names, _unpack(small_all, [full[k].shape for k in small_names])))
    for k in REPL:
        g[k] = small[k]
    for k in VECS:
        g[k] = lax.dynamic_slice_in_dim(small[k], dev * 128, 128, axis=small[k].ndim - 1)

    delta, new_m, new_v = {}, {}, {}
    for n, _, _, _ in MATS:
        shp = w[n].shape
        res = _adamw(*[d[n].reshape(-1, shp[-1]) for d in (w, g, mom, vel)])
        delta[n], new_m[n], new_v[n] = [a.reshape(shp) for a in res]
    shapes = [w[k].shape for k in small_names]
    res = _adamw(*[_pack_rows([d[k] for k in small_names], SMALL_ROWS) for d in (w, g, mom, vel)])
    for out, a in zip((delta, new_m, new_v), res):
        out.update(zip(small_names, _unpack(a.reshape(-1), shapes)))
    return (loss, grad_x, *[g[k] for k in WEIGHTS], *[delta[k] for k in WEIGHTS],
            *[new_m[k] for k in WEIGHTS], *[new_v[k] for k in WEIGHTS])
```

```python
import functools
import math

import jax
import jax.numpy as jnp
from jax import lax
from jax.experimental import pallas as pl
from jax.experimental.pallas import tpu as pltpu

f32 = jnp.float32
bf16 = jnp.bfloat16
HI = lax.Precision.HIGHEST
MESH = pl.DeviceIdType.MESH
AXES = ("x", "y", "c")
ANY = pl.BlockSpec(memory_space=pl.ANY)

D = 1024
DEPTH = 4
N_META = 16
PAD = 240
OFF = PAD + N_META
ROW_TILE = 768
KEY_CHUNK = 256
GRID_W = 64
HD = 64
NQ = 16
NKV = 4
QW = NQ * HD
KW = NKV * HD
QKVW = QW + 2 * KW
DFF = 2816
GROUPS = 64
GCH = 16
NSTATE = 64
CHUNK = 16
GB = 8
ROPE_THETA = 10000.0
LN_EPS = 1e-5
QK_EPS = 1e-6
ALPHA = (2.0 * DEPTH) ** 0.25
ADAM_LR, ADAM_B1, ADAM_B2, ADAM_EPS, ADAM_WD, ADAM_STEP = 0.001, 0.9, 0.999, 1e-08, 0.01, 10
NEG = -1e30
Q_SCALE = HD ** -0.5 * math.log2(math.e)
VMEM_MB = 56

NT = (((1,), (1,)), ((), ()))
TN = (((0,), (0,)), ((), ()))

WEIGHTS = ['meta_tokens', 's5_lambda_re', 's5_lambda_im', 's5_log_dt', 's5_b_re', 's5_b_im', 's5_c_re', 's5_c_im',
           's5_d', 's5_w_glu', 's5_w_out', 'attn_w_qkv', 'attn_q_gain', 'attn_k_gain', 'attn_w_out', 'ffn_w_gate',
           'ffn_w_up', 'ffn_w_down', 'ln_gain', 'ln_bias']
DFFP = 3072
FF_BLK, FF_BLKP = DFF // 8, DFFP // 8
MATS = [('s5_w_glu', False, 128, 128), ('s5_w_out', False, 128, 128), ('attn_w_qkv', True, 192, 192),
        ('attn_w_out', False, 128, 128), ('ffn_w_gate', True, FF_BLKP, FF_BLK), ('ffn_w_up', True, FF_BLKP, FF_BLK),
        ('ffn_w_down', False, FF_BLKP, FF_BLK)]
VECS = ['meta_tokens', 'ln_gain', 'ln_bias']
REPL = ['s5_lambda_re', 's5_lambda_im', 's5_log_dt', 's5_b_re', 's5_b_im', 's5_c_re', 's5_c_im', 's5_d',
        'attn_q_gain', 'attn_k_gain']
MAT_ROWS = 5760
REP_PIECE = 160
REP_ROWS = 8 * REP_PIECE
RS_ROWS = MAT_ROWS + REP_PIECE
RS_TILE = 160
SMALL_ROWS = 1088


def _params(sem, mb=VMEM_MB):
    return pltpu.CompilerParams(dimension_semantics=sem, vmem_limit_bytes=mb << 20)


def _ew(name, fn, rows, consts, outs, accs=(), tile=ROW_TILE):
    first = rows[0][0] if isinstance(rows[0], tuple) else rows[0]
    n = first.shape[-2]
    tile = min(tile, n)
    assert n % tile == 0, (name, n, tile)
    n_in, n_o, n_a = len(rows) + len(consts), len(outs), len(accs)

    def body(*refs):
        i = pl.program_id(0)
        res_o, res_a = fn(i, *[r[...] for r in refs[:n_in]])
        for r, val in zip(refs[n_in:n_in + n_o], res_o):
            r[...] = val.astype(r.dtype)
        if n_a:
            a_refs = refs[n_in + n_o:]

            @pl.when(i == 0)
            def _():
                for r in a_refs:
                    r[...] = jnp.zeros(r.shape, r.dtype)

            for r, val in zip(a_refs, res_a):
                r[...] += val

    in_specs, args = [], []
    for a in rows:
        if isinstance(a, tuple):
            arr, k = a
            in_specs.append(pl.BlockSpec((None, tile, arr.shape[2]), functools.partial(lambda i, k: (k, i, 0), k=k)))
            args.append(arr)
        else:
            in_specs.append(pl.BlockSpec((tile, a.shape[1]), lambda i: (i, 0)))
            args.append(a)
    for c in consts:
        in_specs.append(pl.BlockSpec(c.shape, lambda i: (0, 0)))
        args.append(c)
    out_specs = [pl.BlockSpec((tile, c), lambda i: (i, 0)) for c, _ in outs]
    out_specs += [pl.BlockSpec(s, lambda i: (0, 0)) for s in accs]
    out_shape = [jax.ShapeDtypeStruct((n, c), dt) for c, dt in outs]
    out_shape += [jax.ShapeDtypeStruct(s, f32) for s in accs]
    res = pl.pallas_call(body, grid=(n // tile,), in_specs=in_specs, out_specs=out_specs, out_shape=out_shape,
                         name=name, compiler_params=_params(("arbitrary",)))(*args)
    return res


def _mm(name, a, b, trans_b=False, out_dtype=f32, tm=ROW_TILE):
    m, k = a.shape
    n = b.shape[0] if trans_b else b.shape[1]
    tm = min(tm, m)
    assert m % tm == 0
    dims = NT if trans_b else (((1,), (0,)), ((), ()))

    def body(a_ref, b_ref, o_ref):
        o_ref[...] = lax.dot_general(a_ref[...], b_ref[...], dims, preferred_element_type=f32).astype(o_ref.dtype)

    return pl.pallas_call(
        body, grid=(m // tm,),
        in_specs=[pl.BlockSpec((tm, k), lambda i: (i, 0)), pl.BlockSpec(b.shape, lambda i: (0, 0))],
        out_specs=pl.BlockSpec((tm, n), lambda i: (i, 0)),
        out_shape=jax.ShapeDtypeStruct((m, n), out_dtype), name=name, compiler_params=_params(("parallel",)))(a, b)


def _mm2(name, a1, b1, a2, b2, out_dtype=f32, tm=ROW_TILE // 2):
    m, k = a1.shape
    n = b1.shape[1]
    tm = min(tm, m)
    assert m % tm == 0

    def body(a1_ref, b1_ref, a2_ref, b2_ref, o_ref):
        acc = jnp.dot(a1_ref[...], b1_ref[...], preferred_element_type=f32)
        acc += jnp.dot(a2_ref[...], b2_ref[...], preferred_element_type=f32)
        o_ref[...] = acc.astype(o_ref.dtype)

    row = pl.BlockSpec((tm, k), lambda i: (i, 0))
    whole = pl.BlockSpec(b1.shape, lambda i: (0, 0))
    return pl.pallas_call(
        body, grid=(m // tm,), in_specs=[row, whole, row, whole], out_specs=pl.BlockSpec((tm, n), lambda i: (i, 0)),
        out_shape=jax.ShapeDtypeStruct((m, n), out_dtype), name=name,
        compiler_params=_params(("parallel",)))(a1, b1, a2, b2)


def _mm_tn(name, a, g, tk=512, tl=ROW_TILE):
    rows, k1 = a.shape
    n = g.shape[1]
    tl = min(tl, rows)
    assert rows % tl == 0 and k1 % tk == 0

    def body(a_ref, g_ref, o_ref):
        @pl.when(pl.program_id(1) == 0)
        def _():
            o_ref[...] = jnp.zeros(o_ref.shape, f32)

        o_ref[...] += lax.dot_general(a_ref[...], g_ref[...], TN, preferred_element_type=f32)

    return pl.pallas_call(
        body, grid=(k1 // tk, rows // tl),
        in_specs=[pl.BlockSpec((tl, tk), lambda k, l: (l, k)), pl.BlockSpec((tl, n), lambda k, l: (l, 0))],
        out_specs=pl.BlockSpec((tk, n), lambda k, l: (k, 0)),
        out_shape=jax.ShapeDtypeStruct((k1, n), f32), name=name,
        compiler_params=_params(("parallel", "arbitrary")))(a, g)


def _ln_stats(r):
    mean = jnp.mean(r, axis=-1, keepdims=True)
    c = r - mean
    rstd = lax.rsqrt(jnp.mean(c * c, axis=-1, keepdims=True) + LN_EPS)
    return c * rstd, rstd


def _ln_fwd(h, mix, gain, bias):
    def fn(i, h, mix, g, b):
        r = ALPHA * h + mix
        y = _ln_stats(r)[0] * g + b
        return (r, y, y), ()

    return _ew("ln_fwd", fn, [h, mix], [gain, bias], [(D, f32), (D, f32), (D, bf16)])


def _ln_bwd(d_a, d_b, r, gain):
    def core(dout, r, g):
        xhat, rstd = _ln_stats(r)
        dxh = dout * g
        dr = rstd * (dxh - jnp.mean(dxh, axis=-1, keepdims=True) - xhat * jnp.mean(dxh * xhat, axis=-1, keepdims=True))
        return (dr, dr), (jnp.sum(dout * xhat, axis=0, keepdims=True), jnp.sum(dout, axis=0, keepdims=True))

    outs, accs = [(D, f32), (D, bf16)], [(1, D), (1, D)]
    if d_a is None:
        return _ew("ln_bwd_top", lambda i, d, r, g: core(d, r, g), [d_b, r], [gain], outs, accs)
    return _ew("ln_bwd", lambda i, da, db, r, g: core(ALPHA * da + db, r, g), [d_a, d_b, r], [gain], outs, accs)


def _sigmoid(x):
    return 1.0 / (1.0 + jnp.exp(-x))


def _swiglu_fwd(gate, up):
    def fn(i, g, u):
        g = g.astype(f32)
        return ((g * _sigmoid(g) * u.astype(f32)),), ()

    return _ew("swiglu_fwd", fn, [gate, up], [], [(DFFP, bf16)], tile=ROW_TILE // 2)[0]


def _swiglu_bwd(dact, gate, up):
    def fn(i, da, g, u):
        da, g, u = da.astype(f32), g.astype(f32), u.astype(f32)
        s = _sigmoid(g)
        return (da * u * s * (1.0 + g * (1.0 - s)), da * g * s), ()

    return _ew("swiglu_bwd", fn, [dact, gate, up], [], [(DFFP, bf16), (DFFP, bf16)], tile=ROW_TILE // 2)


def _loss_grad(y, target):
    def fn(i, y, t):
        e = y - t
        return (e * (1.0 / D),), (jnp.sum(e * e, axis=0, keepdims=True),)

    return _ew("loss", fn, [y, target], [], [(D, f32)], [(1, D)], tile=512)


def _adamw(w, g, m, v):
    def fn(i, w, g, m, v):
        m = ADAM_B1 * m + (1.0 - ADAM_B1) * g
        v = ADAM_B2 * v + (1.0 - ADAM_B2) * jnp.square(g)
        m_hat = m / (1.0 - ADAM_B1 ** ADAM_STEP)
        v_hat = v / (1.0 - ADAM_B2 ** ADAM_STEP)
        delta = -ADAM_LR * (m_hat / (jnp.sqrt(v_hat) + ADAM_EPS) + ADAM_WD * w)
        return (delta, m, v), ()

    rows, cols = w.shape
    tile = max(t for t in range(8, min(rows, 544) + 1, 8) if rows % t == 0)
    return _ew("adamw", fn, [w, g, m, v], [], [(cols, f32)] * 3, tile=tile)


def _s5_mats(lam_re, lam_im, log_dt, b_re, b_im, c_re, c_im):
    steps = jnp.arange(CHUNK + 1, dtype=f32)
    shift = (jnp.arange(CHUNK)[None, None, :] - jnp.arange(CHUNK)[None, :, None]
             == jnp.arange(CHUNK)[:, None, None]).astype(f32)

    def one(lr, li, ldt, br, bi, cr, ci, reverse):
        dt = jnp.exp(ldt)[:, None]
        mag = jnp.exp(lr * dt)
        abr, abi = mag * jnp.cos(li * dt), mag * jnp.sin(li * dt)
        nr, ni = abr - 1.0, abi
        den = lr * lr + li * li
        zr, zi = (nr * lr + ni * li) / den, (ni * lr - nr * li) / den
        bbr = zr[..., None] * br - zi[..., None] * bi
        bbi = zr[..., None] * bi + zi[..., None] * br
        pmag = jnp.exp(steps[:, None, None] * (lr * dt)[None])
        pang = steps[:, None, None] * (li * dt)[None]
        pr, pi = pmag * jnp.cos(pang), pmag * jnp.sin(pang)
        car = cr[None] * pr[:, :, None, :] - ci[None] * pi[:, :, None, :]
        cai = cr[None] * pi[:, :, None, :] + ci[None] * pr[:, :, None, :]
        kern = (jnp.einsum('jgop,gpi->jgoi', car[:CHUNK], bbr, precision=HI)
                - jnp.einsum('jgop,gpi->jgoi', cai[:CHUNK], bbi, precision=HI))
        m = jnp.einsum('jgoi,jst->gsito', kern, shift, precision=HI)
        qr, qi = pr[:CHUNK][::-1], pi[:CHUNK][::-1]
        pin_re = qr[:, :, :, None] * bbr[None] - qi[:, :, :, None] * bbi[None]
        pin_im = qr[:, :, :, None] * bbi[None] + qi[:, :, :, None] * bbr[None]
        pin = jnp.stack([pin_re, pin_im], 0).transpose(2, 1, 4, 0, 3)
        pout = jnp.stack([car[1:], -cai[1:]], 0).transpose(2, 0, 4, 1, 3)
        if reverse:
            m, pin, pout = m[:, ::-1, :, ::-1, :], pin[:, ::-1], pout[:, :, :, ::-1, :]
        n = CHUNK * GCH
        return m.reshape(GROUPS, n, n), pin.reshape(GROUPS, n, 2 * NSTATE), pout.reshape(GROUPS, 2 * NSTATE, n), pr[CHUNK], pi[CHUNK]

    mf, pinf, poutf, arf, aif = one(lam_re[0], lam_im[0], log_dt[0], b_re[0], b_im[0], c_re[0], c_im[0], False)
    mr, pinr, poutr, arr, air = one(lam_re[1], lam_im[1], log_dt[1], b_re[1], b_im[1], c_re[1], c_im[1], True)
    return (mf + mr, jnp.concatenate([pinf, pinr], 2), jnp.concatenate([poutf, poutr], 1),
            jnp.stack([arf, arr]), jnp.stack([aif, air]))


def _s5_coefs(a_re, a_im):
    c1 = jnp.concatenate([a_re, a_re], -1)
    c2 = jnp.concatenate([-a_im, a_im], -1)
    return c1[0], c2[0], c1[1], c2[1]


def _swap(s):
    return pltpu.roll(s, NSTATE, 1)


def _s5_states(nc, u_ref, pin_ref, coef, vf, vr, sf, sr):
    c1f, c2f, c1r, c2r = coef
    for g in range(GB):
        v = jnp.dot(u_ref[g], pin_ref[g], preferred_element_type=f32)
        vf[pl.ds(g * nc, nc), :] = v[:, :2 * NSTATE]
        vr[pl.ds(g * nc, nc), :] = v[:, 2 * NSTATE:]

    def step(i, carry):
        s_f, s_r = carry
        kf, kr = i, nc - 1 - i
        sf[pl.ds(kf, GB, stride=nc), :] = s_f
        sr[pl.ds(kr, GB, stride=nc), :] = s_r
        s_f = c1f * s_f + c2f * _swap(s_f) + vf[pl.ds(kf, GB, stride=nc), :]
        s_r = c1r * s_r + c2r * _swap(s_r) + vr[pl.ds(kr, GB, stride=nc), :]
        return s_f, s_r

    z = jnp.zeros((GB, 2 * NSTATE), f32)
    lax.fori_loop(0, nc, step, (z, z))


def _s5_core_fwd(ug, msum, pin, pout, coefs):
    nc = ug.shape[1]
    n = CHUNK * GCH

    def body(u_ref, m_ref, pin_ref, pout_ref, c1f, c2f, c1r, c2r, y_ref, vf, vr, sf, sr):
        coef = (c1f[...], c2f[...], c1r[...], c2r[...])
        _s5_states(nc, u_ref, pin_ref, coef, vf, vr, sf, sr)
        for g in range(GB):
            s_in = jnp.concatenate([sf[pl.ds(g * nc, nc), :], sr[pl.ds(g * nc, nc), :]], axis=1).astype(bf16)
            y_ref[g] = (jnp.dot(u_ref[g], m_ref[g], preferred_element_type=f32)
                        + jnp.dot(s_in, pout_ref[g], preferred_element_type=f32))

    seq = pl.BlockSpec((GB, nc, n), lambda i: (i, 0, 0))
    mat = pl.BlockSpec((GB, n, n), lambda i: (i, 0, 0))
    cf = pl.BlockSpec((GB, 2 * NSTATE), lambda i: (i, 0))
    scr = pltpu.VMEM((GB * nc, 2 * NSTATE), f32)
    return pl.pallas_call(
        body, grid=(GROUPS // GB,), in_specs=[seq, mat, mat, mat, cf, cf, cf, cf], out_specs=seq,
        out_shape=jax.ShapeDtypeStruct((GROUPS, nc, n), f32), scratch_shapes=[scr, scr, scr, scr],
        name="s5_core_fwd", compiler_params=_params(("parallel",)))(ug, msum, pin, pout, *coefs)


def _s5_core_bwd(ug, dyg, msum, pin, pout, coefs):
    nc = ug.shape[1]
    n = CHUNK * GCH

    def body(u_ref, dy_ref, m_ref, pin_ref, pout_ref, c1f, c2f, c1r, c2r,
             du_ref, dm_ref, dpin_ref, dpout_ref, a1f_ref, a2f_ref, a1r_ref, a2r_ref, vf, vr, sf, sr):
        coef = (c1f[...], c2f[...], c1r[...], c2r[...])
        _s5_states(nc, u_ref, pin_ref, coef, vf, vr, sf, sr)
        for g in range(GB):
            s_in = jnp.concatenate([sf[pl.ds(g * nc, nc), :], sr[pl.ds(g * nc, nc), :]], axis=1).astype(bf16)
            dy = dy_ref[g]
            ds = lax.dot_general(dy, pout_ref[g], NT, preferred_element_type=f32)
            vf[pl.ds(g * nc, nc), :] = ds[:, :2 * NSTATE]
            vr[pl.ds(g * nc, nc), :] = ds[:, 2 * NSTATE:]
            dpout_ref[g] = lax.dot_general(s_in, dy, TN, preferred_element_type=f32)
            dm_ref[g] = lax.dot_general(u_ref[g], dy, TN, preferred_element_type=f32)

        k1f, k2f, k1r, k2r = coef[0], -coef[1], coef[2], -coef[3]

        def step(i, carry):
            g_f, g_r, a1f, a2f, a1r, a2r = carry
            kf, kr = nc - 1 - i, i
            s_f = sf[pl.ds(kf, GB, stride=nc), :]
            s_r = sr[pl.ds(kr, GB, stride=nc), :]
            d_f = vf[pl.ds(kf, GB, stride=nc), :]
            d_r = vr[pl.ds(kr, GB, stride=nc), :]
            sf[pl.ds(kf, GB, stride=nc), :] = g_f
            sr[pl.ds(kr, GB, stride=nc), :] = g_r
            a1f, a2f = a1f + g_f * s_f, a2f + g_f * _swap(s_f)
            a1r, a2r = a1r + g_r * s_r, a2r + g_r * _swap(s_r)
            g_f = d_f + k1f * g_f + k2f * _swap(g_f)
            g_r = d_r + k1r * g_r + k2r * _swap(g_r)
            return g_f, g_r, a1f, a2f, a1r, a2r

        z = jnp.zeros((GB, 2 * NSTATE), f32)
        _, _, a1f, a2f, a1r, a2r = lax.fori_loop(0, nc, step, (z, z, z, z, z, z))
        a1f_ref[...], a2f_ref[...], a1r_ref[...], a2r_ref[...] = a1f, a2f, a1r, a2r
        for g in range(GB):
            dv = jnp.concatenate([sf[pl.ds(g * nc, nc), :], sr[pl.ds(g * nc, nc), :]], axis=1).astype(bf16)
            du_ref[g] = (lax.dot_general(dy_ref[g], m_ref[g], NT, preferred_element_type=f32)
                         + lax.dot_general(dv, pin_ref[g], NT, preferred_element_type=f32))
            dpin_ref[g] = lax.dot_general(u_ref[g], dv, TN, preferred_element_type=f32)

    seq = pl.BlockSpec((GB, nc, n), lambda i: (i, 0, 0))
    mat = pl.BlockSpec((GB, n, n), lambda i: (i, 0, 0))
    cf = pl.BlockSpec((GB, 2 * NSTATE), lambda i: (i, 0))
    scr = pltpu.VMEM((GB * nc, 2 * NSTATE), f32)
    mat_s = jax.ShapeDtypeStruct((GROUPS, n, n), f32)
    cf_s = jax.ShapeDtypeStruct((GROUPS, 2 * NSTATE), f32)
    return pl.pallas_call(
        body, grid=(GROUPS // GB,), in_specs=[seq, seq, mat, mat, mat, cf, cf, cf, cf],
        out_specs=[seq, mat, mat, mat, cf, cf, cf, cf],
        out_shape=[jax.ShapeDtypeStruct((GROUPS, nc, n), f32), mat_s, mat_s, mat_s, cf_s, cf_s, cf_s, cf_s],
        scratch_shapes=[scr, scr, scr, scr], name="s5_core_bwd",
        compiler_params=_params(("parallel",)))(ug, dyg, msum, pin, pout, *coefs)


def _to_groups(a):
    n = a.shape[0]
    return a.reshape(n // CHUNK, CHUNK, GROUPS, GCH).transpose(2, 0, 1, 3).reshape(GROUPS, n // CHUNK, CHUNK * GCH)


def _from_groups(g):
    nc = g.shape[1]
    return g.reshape(GROUPS, nc, CHUNK, GCH).transpose(1, 2, 0, 3).reshape(nc * CHUNK, D)


def _gelu(y):
    return 0.5 * y * (1.0 + lax.erf(y * (2.0 ** -0.5)))


def _gelu_grad(y):
    return 0.5 * (1.0 + lax.erf(y * (2.0 ** -0.5))) + y * jnp.exp(-0.5 * y * y) * (1.0 / math.sqrt(2.0 * math.pi))


def _s5_fwd(h, h_bf, valid, mats, d_skip, w_glu, w_out):
    msum, pin, pout, a_re, a_im = mats
    coefs = _s5_coefs(a_re, a_im)
    ug = _to_groups(jnp.where(valid, h_bf, jnp.zeros_like(h_bf)))
    ys = _from_groups(_s5_core_fwd(ug, msum.astype(bf16), pin.astype(bf16), pout.astype(bf16), coefs))

    def post(i, ys, h, d):
        y = ys + d * h
        return (y, _gelu(y)), ()

    y, g_bf = _ew("s5_gelu", post, [ys, h], [d_skip], [(D, f32), (D, bf16)])
    gw = _mm("s5_glu_mm", g_bf, w_glu)

    def glu(i, y, gw):
        return (_gelu(y) * _sigmoid(gw),), ()

    z_bf = _ew("s5_glu", glu, [y, gw], [], [(D, bf16)])[0]
    mix = _mm("s5_out_mm", z_bf, w_out)
    return mix, (ug, y, g_bf, gw, z_bf)


def _s5_bwd(dmix_bf, h, valid, saved, mats, vjp_mats, d_skip, w_glu, w_out):
    ug, y, g_bf, gw, z_bf = saved
    msum, pin, pout, a_re, a_im = mats
    coefs = _s5_coefs(a_re, a_im)
    dz = _mm("s5_dz_mm", dmix_bf, w_out, trans_b=True)
    d_w_out = _mm_tn("s5_dwout", z_bf, dmix_bf)

    def dglu(i, dz, y, gw):
        g, s = _gelu(y), _sigmoid(gw)
        return (dz * g * s * (1.0 - s), dz * s), ()

    dgw_bf, dg1 = _ew("s5_dglu", dglu, [dz, y, gw], [], [(D, bf16), (D, f32)])
    d_w_glu = _mm_tn("s5_dwglu", g_bf, dgw_bf)
    dg2 = _mm("s5_dg_mm", dgw_bf, w_glu, trans_b=True)

    def dgelu(i, dg1, dg2, y, h, d):
        dy = (dg1 + dg2) * _gelu_grad(y)
        return (dy, dy * d), (jnp.sum(dy * h, axis=0, keepdims=True),)

    dy_bf, dh_skip, dd = _ew("s5_dgelu", dgelu, [dg1, dg2, y, h], [d_skip], [(D, bf16), (D, f32)], [(1, D)])
    dug, dm, dpin, dpout, a1f, a2f, a1r, a2r = _s5_core_bwd(
        ug, _to_groups(dy_bf), msum.astype(bf16), pin.astype(bf16), pout.astype(bf16), coefs)
    du = _from_groups(dug)
    dh = dh_skip + jnp.where(valid, du, jnp.zeros_like(du))
    a1, a2 = jnp.stack([a1f, a1r]), jnp.stack([a2f, a2r])
    da_re = a1[..., :NSTATE] + a1[..., NSTATE:]
    da_im = a2[..., NSTATE:] - a2[..., :NSTATE]
    d_params = vjp_mats((dm, dpin, dpout, da_re, da_im))
    return dh, d_params, dd[0], d_w_glu, d_w_out


def _rope_tables(n):
    row = jnp.arange(n, dtype=jnp.int32) - OFF
    real = row >= 0
    rid = jnp.where(real, row // GRID_W, 0).astype(f32)
    cid = jnp.where(real, row % GRID_W, 0).astype(f32)
    half = HD // 2
    inv = ROPE_THETA ** (-jnp.arange(0, half, 2, dtype=f32) / half)
    ar, ac = rid[:, None] * inv[None, :], cid[:, None] * inv[None, :]
    cos = jnp.concatenate([jnp.cos(ar), jnp.cos(ar), jnp.cos(ac), jnp.cos(ac)], axis=1)
    sin = jnp.concatenate([-jnp.sin(ar), jnp.sin(ar), -jnp.sin(ac), jnp.sin(ac)], axis=1)
    return jnp.tile(cos, (1, 2)), jnp.tile(sin, (1, 2))


def _head_mats():
    head = jnp.arange(QW, dtype=jnp.int32)[:, None] // HD == jnp.arange(128, dtype=jnp.int32)[None, :]
    return head.astype(f32) * (1.0 / HD), head.astype(f32).T


def _rot(v):
    w = v.shape[1]
    lane = lax.broadcasted_iota(jnp.int32, v.shape, 1)
    return jnp.where(lane % 32 < 16, pltpu.roll(v, w - 16, 1), pltpu.roll(v, 16, 1))


def _head_mean(v, e, et):
    w = v.shape[1]
    m = jnp.dot(v, e[:w], preferred_element_type=f32, precision=HI)
    return m, et[:, :w]


def _rms_rope(t, gain, e, et, cos, sin):
    w = t.shape[1]
    ms, spread = _head_mean(t * t, e, et)
    rs = jnp.dot(lax.rsqrt(ms + QK_EPS), spread, preferred_element_type=f32, precision=HI)
    n0 = t * rs
    n = n0 * gain
    reps = w // 128
    return n * jnp.tile(cos, (1, reps)) + _rot(n) * jnp.tile(sin, (1, reps))


def _rms_rope_bwd(dout, t, gain, e, et, cos, sin):
    w = t.shape[1]
    reps = w // 128
    ms, spread = _head_mean(t * t, e, et)
    rs = jnp.dot(lax.rsqrt(ms + QK_EPS), spread, preferred_element_type=f32, precision=HI)
    n0 = t * rs
    dn = dout * jnp.tile(cos, (1, reps)) + _rot(dout * jnp.tile(sin, (1, reps)))
    dn0 = dn * gain
    mm, _ = _head_mean(dn0 * n0, e, et)
    corr = jnp.dot(mm, spread, preferred_element_type=f32, precision=HI)
    return rs * (dn0 - n0 * corr), jnp.sum(dn * n0, axis=0, keepdims=True)


def _qk_fwd(qkv, qg, kg, e, et, cos, sin):
    def fn(i, qkv, cos, sin, qg, kg, e, et):
        q = _rms_rope(qkv[:, :QW], qg, e, et, cos, sin) * Q_SCALE
        k = _rms_rope(qkv[:, QW:QW + KW], kg, e, et, cos, sin)
        return (q, k, qkv[:, QW + KW:]), ()

    return _ew("qk_rope", fn, [qkv, cos, sin], [qg, kg, e, et], [(QW, bf16), (KW, bf16), (KW, bf16)])


def _qk_bwd(qkv, dq, dk, dv, qg, kg, e, et, cos, sin):
    def fn(i, qkv, cos, sin, dq, dk, dv, qg, kg, e, et):
        dtq, dgq = _rms_rope_bwd(dq * (HD ** -0.5), qkv[:, :QW], qg, e, et, cos, sin)
        dtk, dgk = _rms_rope_bwd(dk * math.log(2.0), qkv[:, QW:QW + KW], kg, e, et, cos, sin)
        return (jnp.concatenate([dtq, dtk, dv], axis=1),), (dgq, dgk)

    return _ew("qk_rope_bwd", fn, [qkv, cos, sin, dq, dk, dv], [qg, kg, e, et], [(QKVW, bf16)], [(1, QW), (1, KW)])


def _to_heads(a, nh):
    return a.reshape(a.shape[0], nh, HD).transpose(1, 0, 2)


def _from_heads(a):
    return a.transpose(1, 0, 2).reshape(a.shape[1], a.shape[0] * HD)


def _masked_first(s, c):
    if c:
        return s
    col = lax.broadcasted_iota(jnp.int32, (1, s.shape[1]), 1)
    return jnp.where(col >= PAD, s, NEG)


def _flash_fwd(q, k, v1, tq=ROW_TILE, tc=KEY_CHUNK):
    n = q.shape[1]
    nc = n // tc

    def body(q_ref, k_ref, v_ref, o_ref, lse_ref):
        qb = q_ref[0]

        def scores(c):
            ks = k_ref[0, pl.ds(c * tc, tc), :]
            return _masked_first(lax.dot_general(qb, ks, NT, preferred_element_type=f32), c)

        m = jnp.full((tq, 1), NEG, f32)
        acc = jnp.zeros((tq, 2 * HD), f32)
        s_next = scores(0)
        for c in range(nc):
            s = s_next
            if c + 1 < nc:
                s_next = scores(c + 1)
            m_new = jnp.maximum(m, jnp.max(s, axis=1, keepdims=True))
            p = jnp.exp2(s - m_new)
            acc = jnp.exp2(m - m_new) * acc + jnp.dot(p.astype(bf16), v_ref[0, pl.ds(c * tc, tc), :],
                                                      preferred_element_type=f32)
            m = m_new
        l = acc[:, HD:HD + 1]
        o_ref[0] = acc[:, :HD] / l
        lse_ref[0] = m + jnp.log2(l)

    return pl.pallas_call(
        body, grid=(NQ, n // tq),
        in_specs=[pl.BlockSpec((1, tq, HD), lambda h, i: (h, i, 0)),
                  pl.BlockSpec((1, n, HD), lambda h, i: (h // (NQ // NKV), 0, 0)),
                  pl.BlockSpec((1, n, 2 * HD), lambda h, i: (h // (NQ // NKV), 0, 0))],
        out_specs=[pl.BlockSpec((1, tq, HD), lambda h, i: (h, i, 0)), pl.BlockSpec((1, tq, 1), lambda h, i: (h, i, 0))],
        out_shape=[jax.ShapeDtypeStruct((NQ, n, HD), f32), jax.ShapeDtypeStruct((NQ, n, 1), f32)],
        name="flash_fwd", compiler_params=_params(("parallel", "parallel")))(q, k, v1)


def _flash_dq(q, k, v, do, o, lse, tq=ROW_TILE, tc=KEY_CHUNK):
    n = q.shape[1]
    nc = n // tc

    def body(q_ref, k_ref, v_ref, do_ref, o_ref, lse_ref, dq_ref, delta_ref):
        qb, dob, lse_b = q_ref[0], do_ref[0], lse_ref[0]
        delta = jnp.sum(dob.astype(f32) * o_ref[0], axis=1, keepdims=True)

        def products(c):
            ks = k_ref[0, pl.ds(c * tc, tc), :]
            vs = v_ref[0, pl.ds(c * tc, tc), :]
            s = _masked_first(lax.dot_general(qb, ks, NT, preferred_element_type=f32), c)
            return s, lax.dot_general(dob, vs, NT, preferred_element_type=f32)

        dq = jnp.zeros((tq, HD), f32)
        nxt = products(0)
        for c in range(nc):
            s, dp = nxt
            if c + 1 < nc:
                nxt = products(c + 1)
            ds = jnp.exp2(s - lse_b) * (dp - delta)
            dq = dq + jnp.dot(ds.astype(bf16), k_ref[0, pl.ds(c * tc, tc), :], preferred_element_type=f32)
        dq_ref[0] = dq
        delta_ref[0] = delta

    qspec = pl.BlockSpec((1, tq, HD), lambda h, i: (h, i, 0))
    kspec = pl.BlockSpec((1, n, HD), lambda h, i: (h // (NQ // NKV), 0, 0))
    cspec = pl.BlockSpec((1, tq, 1), lambda h, i: (h, i, 0))
    return pl.pallas_call(
        body, grid=(NQ, n // tq), in_specs=[qspec, kspec, kspec, qspec, qspec, cspec], out_specs=[qspec, cspec],
        out_shape=[jax.ShapeDtypeStruct((NQ, n, HD), f32), jax.ShapeDtypeStruct((NQ, n, 1), f32)],
        name="flash_dq", compiler_params=_params(("parallel", "parallel")))(q, k, v, do, o, lse)


def _flash_dkv(q, k, v, do, lse_row, delta_row, tk=ROW_TILE, tc=KEY_CHUNK):
    n = q.shape[1]
    nc = n // tc
    grp = NQ // NKV

    def body(q_ref, do_ref, lse_ref, delta_ref, k_ref, v_ref, dk_ref, dv_ref):
        kb, vb = k_ref[0], v_ref[0]

        def head(g, carry):
            dk, dv = carry

            def products(c):
                qs = q_ref[g, pl.ds(c * tc, tc), :]
                dos = do_ref[g, pl.ds(c * tc, tc), :]
                return (lax.dot_general(kb, qs, NT, preferred_element_type=f32),
                        lax.dot_general(vb, dos, NT, preferred_element_type=f32))

            nxt = products(0)
            for c in range(nc):
                st, dpt = nxt
                if c + 1 < nc:
                    nxt = products(c + 1)
                pt = jnp.exp2(st - lse_ref[g, :, pl.ds(c * tc, tc)])
                dv = dv + jnp.dot(pt.astype(bf16), do_ref[g, pl.ds(c * tc, tc), :], preferred_element_type=f32)
                dst = pt * (dpt - delta_ref[g, :, pl.ds(c * tc, tc)])
                dk = dk + jnp.dot(dst.astype(bf16), q_ref[g, pl.ds(c * tc, tc), :], preferred_element_type=f32)
            return dk, dv

        z = jnp.zeros((tk, HD), f32)
        dk, dv = lax.fori_loop(0, grp, head, (z, z))
        row = lax.broadcasted_iota(jnp.int32, (tk, 1), 0) + pl.program_id(1) * tk
        dk_ref[0] = jnp.where(row >= PAD, dk, 0.0)
        dv_ref[0] = jnp.where(row >= PAD, dv, 0.0)

    gspec = pl.BlockSpec((grp, n, HD), lambda h, j: (h, 0, 0))
    rspec = pl.BlockSpec((grp, 1, n), lambda h, j: (h, 0, 0))
    kspec = pl.BlockSpec((1, tk, HD), lambda h, j: (h, j, 0))
    return pl.pallas_call(
        body, grid=(NKV, n // tk), in_specs=[gspec, gspec, rspec, rspec, kspec, kspec], out_specs=[kspec, kspec],
        out_shape=[jax.ShapeDtypeStruct((NKV, n, HD), f32)] * 2,
        name="flash_dkv", compiler_params=_params(("parallel", "parallel")))(q, do, lse_row, delta_row, k, v)


def _attn_fwd(h_bf, w_qkv_t, qg, kg, w_out, tabs):
    e, et, cos, sin = tabs
    qkv = _mm("attn_qkv_mm", h_bf, w_qkv_t, trans_b=True)
    q_bf, k_bf, v_bf = _qk_fwd(qkv, qg, kg, e, et, cos, sin)
    q16, k4, v4 = _to_heads(q_bf, NQ), _to_heads(k_bf, NKV), _to_heads(v_bf, NKV)
    ones = jnp.zeros((NKV, v4.shape[1], HD), bf16).at[:, :, 0].set(1.0)
    o16, lse = _flash_fwd(q16, k4, jnp.concatenate([v4, ones], axis=2))
    o_bf = _from_heads(o16).astype(bf16)
    mix = _mm("attn_out_mm", o_bf, w_out)
    return mix, (qkv, q16, k4, v4, o16, lse, o_bf)


def _attn_bwd(dmix_bf, h_bf, saved, w_qkv_t, qg, kg, w_out, tabs):
    e, et, cos, sin = tabs
    qkv, q16, k4, v4, o16, lse, o_bf = saved
    n = qkv.shape[0]
    do = _mm("attn_do_mm", dmix_bf, w_out, trans_b=True, out_dtype=bf16)
    d_w_out = _mm_tn("attn_dwout", o_bf, dmix_bf)
    do16 = _to_heads(do, NQ)
    dq16, delta = _flash_dq(q16, k4, v4, do16, o16, lse)
    dk4, dv4 = _flash_dkv(q16, k4, v4, do16, lse.reshape(NQ, 1, n), delta.reshape(NQ, 1, n))
    dqkv_bf, dgq, dgk = _qk_bwd(qkv, _from_heads(dq16), _from_heads(dk4), _from_heads(dv4), qg, kg, e, et, cos, sin)
    d_w_qkv_t = _mm_tn("attn_dwqkv", dqkv_bf, h_bf)
    dh = _mm("attn_dh_mm", dqkv_bf, w_qkv_t)
    return dh, d_w_qkv_t, dgq.reshape(NQ, HD).sum(0), dgk.reshape(NKV, HD).sum(0), d_w_out


def _all_gather(name, shard):
    def body(x_ref, out_ref, send_sems, recv_sems, local_sem):
        x, y, c = lax.axis_index("x"), lax.axis_index("y"), lax.axis_index("c")
        me, sibling = (x, y, c), (x, y, 1 - c)
        chips = [(1 - x, y), (x, 1 - y), (1 - x, 1 - y)]

        def slot(px, py, pc):
            return out_ref.at[4 * px + 2 * py + pc]

        def copy(k, block, to, src=None):
            return pltpu.make_async_remote_copy(
                src_ref=slot(*block) if src is None else src, dst_ref=slot(*block),
                send_sem=send_sems.at[k], recv_sem=recv_sems.at[k], device_id=to, device_id_type=MESH)

        mine = pltpu.make_async_copy(x_ref, slot(*me), local_sem)
        mine.start()
        first = [copy(0, me, sibling, src=x_ref)]
        first += [copy(1 + j, me, (*chip, c), src=x_ref) for j, chip in enumerate(chips)]
        for cp in first:
            cp.start()
        passed = [copy(4 + j, (*chip, c), sibling) for j, chip in enumerate(chips)]
        for j, chip in enumerate(chips):
            copy(1 + j, (*chip, c), me).wait_recv()
            passed[j].start()
        copy(0, sibling, me).wait_recv()
        for j, chip in enumerate(chips):
            copy(4 + j, (*chip, 1 - c), me).wait_recv()
        for cp in first + passed:
            cp.wait_send()
        mine.wait()

    return pl.pallas_call(
        body, out_shape=jax.ShapeDtypeStruct((8,) + shard.shape, shard.dtype), in_specs=[ANY], out_specs=ANY,
        scratch_shapes=[pltpu.SemaphoreType.DMA((7,)), pltpu.SemaphoreType.DMA((7,)), pltpu.SemaphoreType.DMA],
        name=name)(shard)


def _swap_sibling(name, theirs):
    def body(src_ref, dst_ref, send_sem, recv_sem):
        x, y, c = lax.axis_index("x"), lax.axis_index("y"), lax.axis_index("c")
        cp = pltpu.make_async_remote_copy(src_ref=src_ref, dst_ref=dst_ref, send_sem=send_sem, recv_sem=recv_sem,
                                          device_id=(x, y, 1 - c), device_id_type=MESH)
        cp.start()
        cp.wait()

    return pl.pallas_call(
        body, out_shape=jax.ShapeDtypeStruct(theirs.shape, theirs.dtype), in_specs=[ANY], out_specs=ANY,
        scratch_shapes=[pltpu.SemaphoreType.DMA, pltpu.SemaphoreType.DMA], name=name)(theirs)


def _exchange_chips(name, part):
    def body(p_ref, t_ref, send_sems, recv_sems, local_sem):
        x, y, c = lax.axis_index("x"), lax.axis_index("y"), lax.axis_index("c")
        q = 2 * x + y
        own = pltpu.make_async_copy(p_ref.at[q], t_ref.at[q], local_sem)
        own.start()
        copies = []
        for k in (1, 2, 3):
            tx, ty = x ^ (k >> 1), y ^ (k & 1)
            copies.append(pltpu.make_async_remote_copy(
                src_ref=p_ref.at[2 * tx + ty], dst_ref=t_ref.at[q], send_sem=send_sems.at[k - 1],
                recv_sem=recv_sems.at[k - 1], device_id=(tx, ty, c), device_id_type=MESH))
        for cp in copies:
            cp.start()
        for cp in copies:
            cp.wait()
        own.wait()

    return pl.pallas_call(
        body, out_shape=jax.ShapeDtypeStruct(part.shape, part.dtype), in_specs=[ANY], out_specs=ANY,
        scratch_shapes=[pltpu.SemaphoreType.DMA((3,)), pltpu.SemaphoreType.DMA((3,)), pltpu.SemaphoreType.DMA],
        name=name)(part)


def _reduce_scatter(mine, theirs):
    got = _swap_sibling("rs_sibling", theirs)
    rows = 4 * RS_ROWS
    part = _ew("rs_add2", lambda i, a, b: ((a + b,), ()), [mine.reshape(rows, D), got.reshape(rows, D)], [],
               [(D, f32)], tile=RS_TILE)[0]
    t = _exchange_chips("rs_chips", part.reshape(4, RS_ROWS, D))
    return _ew("rs_add4", lambda i, a, b, c, d: ((((a + b) + c) + d,), ()), [(t, 0), (t, 1), (t, 2), (t, 3)], [],
               [(D, f32)], tile=RS_TILE)[0]


def _pack_rows(parts, rows):
    flat = jnp.concatenate([p.reshape(-1) for p in parts])
    return jnp.pad(flat, (0, rows * D - flat.shape[0])).reshape(rows, D)


def _unpack(flat, shapes):
    out, off = [], 0
    for s in shapes:
        n = math.prod(s)
        out.append(flat[off:off + n].reshape(s))
        off += n
    return out


def _mat_rows(block, transposed, blk):
    a = jnp.swapaxes(block, 1, 2) if transposed else block
    a = jnp.pad(a, ((0, 0), (0, blk - a.shape[1]), (0, 0)))
    return a.reshape(-1, D)


def _mat_block(rows, transposed, blk, real):
    a = rows.reshape(-1, blk, D)[:, :real]
    return jnp.swapaxes(a, 1, 2) if transposed else a


def _mat_full(gathered, blk):
    layers = gathered.shape[1] // blk
    return gathered.reshape(8, layers, blk, D).transpose(1, 0, 2, 3).reshape(layers, 8 * blk, D)


def _vec_full(gathered):
    return gathered.transpose(1, 0, 2).reshape(gathered.shape[1], D)


def _grad_slots(full, small, cc):
    def halves(a, blk):
        layers = a.shape[0]
        a5 = a.reshape(layers, 4, 2, blk, D)
        res = []
        for sel in (cc, 1 - cc):
            s = lax.dynamic_index_in_dim(a5, sel, axis=2, keepdims=False)
            res.append(s.transpose(1, 0, 2, 3).reshape(4, layers * blk, D))
        return res

    mine, theirs = [], []
    for name, _, blk, _ in MATS:
        a, b = halves(full[name], blk)
        mine.append(a)
        theirs.append(b)
    a, b = halves(small[None], REP_PIECE)
    return jnp.concatenate(mine + [a], axis=1), jnp.concatenate(theirs + [b], axis=1)


def _local_step(x0, target0, w, fw):
    seq = x0.shape[0]
    n = OFF + seq
    valid = (jnp.arange(n, dtype=jnp.int32) >= PAD)[:, None]
    h = jnp.concatenate([jnp.zeros((PAD, D), f32), fw['meta_tokens'], x0], axis=0)
    h_bf = h.astype(bf16)
    tabs = _head_mats() + _rope_tables(n)
    qg = [jnp.tile(w['attn_q_gain'][j], NQ)[None, :] for j in range(2)]
    kg = [jnp.tile(w['attn_k_gain'][j], NKV)[None, :] for j in range(2)]
    s5_names = ['s5_lambda_re', 's5_lambda_im', 's5_log_dt', 's5_b_re', 's5_b_im', 's5_c_re', 's5_c_im']
    s5_mats, s5_vjp = [], []
    for j in range(2):
        mats, vjp = jax.vjp(_s5_mats, *[w[k][j] for k in s5_names])
        s5_mats.append(mats)
        s5_vjp.append(vjp)
    saved = []
    for i in range(DEPTH):
        j = i // 2
        if i % 2 == 0:
            mix, sv = _s5_fwd(h, h_bf, valid, s5_mats[j], w['s5_d'][j][None, :], fw['s5_w_glu'][j], fw['s5_w_out'][j])
        else:
            mix, sv = _attn_fwd(h_bf, fw['attn_w_qkv'][j], qg[j], kg[j], fw['attn_w_out'][j], tabs)
        r1, h1, h1_bf = _ln_fwd(h, mix, fw['ln_gain'][i, 0][None, :], fw['ln_bias'][i, 0][None, :])
        gate = _mm("ffn_gate_mm", h1_bf, fw['ffn_w_gate'][i], trans_b=True, out_dtype=bf16)
        up = _mm("ffn_up_mm", h1_bf, fw['ffn_w_up'][i], trans_b=True, out_dtype=bf16)
        act = _swiglu_fwd(gate, up)
        f = _mm("ffn_down_mm", act, fw['ffn_w_down'][i])
        r2, h2, h2_bf = _ln_fwd(h1, f, fw['ln_gain'][i, 1][None, :], fw['ln_bias'][i, 1][None, :])
        saved.append((h, h_bf, sv, r1, h1_bf, gate, up, act, r2))
        h, h_bf = h2, h2_bf

    dy, sq = _loss_grad(h[OFF:], target0)
    loss = 0.5 * jnp.sum(sq) * (1.0 / D)

    grads = {k: [None] * (DEPTH if k.startswith('ffn') else 2) for k in WEIGHTS}
    d_ln_gain = [[None, None] for _ in range(DEPTH)]
    d_ln_bias = [[None, None] for _ in range(DEPTH)]
    d_a, d_b = None, jnp.pad(dy, ((OFF, 0), (0, 0)))
    for i in reversed(range(DEPTH)):
        j = i // 2
        h_in, h_in_bf, sv, r1, h1_bf, gate, up, act, r2 = saved[i]
        dr2, dr2_bf, dg, db = _ln_bwd(d_a, d_b, r2, fw['ln_gain'][i, 1][None, :])
        d_ln_gain[i][1], d_ln_bias[i][1] = dg[0], db[0]
        dact = _mm("ffn_dact_mm", dr2_bf, fw['ffn_w_down'][i], trans_b=True, out_dtype=bf16)
        dgate, dup = _swiglu_bwd(dact, gate, up)
        grads['ffn_w_down'][i] = _mm_tn("ffn_dwdown", act, dr2_bf, tk=DFFP // 2)
        grads['ffn_w_gate'][i] = _mm_tn("ffn_dwgate", dgate, h1_bf, tk=DFFP // 2)
        grads['ffn_w_up'][i] = _mm_tn("ffn_dwup", dup, h1_bf, tk=DFFP // 2)
        dh1 = _mm2("ffn_dh_mm", dgate, fw['ffn_w_gate'][i], dup, fw['ffn_w_up'][i])
        dr1, dr1_bf, dg, db = _ln_bwd(dr2, dh1, r1, fw['ln_gain'][i, 0][None, :])
        d_ln_gain[i][0], d_ln_bias[i][0] = dg[0], db[0]
        if i % 2 == 0:
            dh, d_par, dd, d_w_glu, d_w_out = _s5_bwd(dr1_bf, h_in, valid, sv, s5_mats[j], s5_vjp[j],
                                                      w['s5_d'][j][None, :], fw['s5_w_glu'][j], fw['s5_w_out'][j])
            for k, g in zip(s5_names, d_par):
                grads[k][j] = g
            grads['s5_d'][j], grads['s5_w_glu'][j], grads['s5_w_out'][j] = dd, d_w_glu, d_w_out
        else:
            dh, d_w_qkv, dgq, dgk, d_w_out = _attn_bwd(dr1_bf, h_in_bf, sv, fw['attn_w_qkv'][j], qg[j], kg[j],
                                                       fw['attn_w_out'][j], tabs)
            grads['attn_w_qkv'][j], grads['attn_w_out'][j] = d_w_qkv, d_w_out
            grads['attn_q_gain'][j], grads['attn_k_gain'][j] = dgq, dgk
        d_a, d_b = dr1, dh
    dh0 = _ew("dh0", lambda i, a, b: ((ALPHA * a + b,), ()), [d_a, d_b], [], [(D, f32)])[0]
    full = {k: jnp.stack(v) for k, v in grads.items() if v[0] is not None}
    full['meta_tokens'] = dh0[PAD:OFF]
    full['ln_gain'] = jnp.stack([jnp.stack(r) for r in d_ln_gain])
    full['ln_bias'] = jnp.stack([jnp.stack(r) for r in d_ln_bias])

    return loss, dh0[OFF:], full


def kernel(x, meta_tokens, s5_lambda_re, s5_lambda_im, s5_log_dt, s5_b_re, s5_b_im, s5_c_re, s5_c_im, s5_d, s5_w_glu, s5_w_out, attn_w_qkv, attn_q_gain, attn_k_gain, attn_w_out, ffn_w_gate, ffn_w_up, ffn_w_down, ln_gain, ln_bias, loss_target, m_meta_tokens, m_s5_lambda_re, m_s5_lambda_im, m_s5_log_dt, m_s5_b_re, m_s5_b_im, m_s5_c_re, m_s5_c_im, m_s5_d, m_s5_w_glu, m_s5_w_out, m_attn_w_qkv, m_attn_q_gain, m_attn_k_gain, m_attn_w_out, m_ffn_w_gate, m_ffn_w_up, m_ffn_w_down, m_ln_gain, m_ln_bias, v_meta_tokens, v_s5_lambda_re, v_s5_lambda_im, v_s5_log_dt, v_s5_b_re, v_s5_b_im, v_s5_c_re, v_s5_c_im, v_s5_d, v_s5_w_glu, v_s5_w_out, v_attn_w_qkv, v_attn_q_gain, v_attn_k_gain, v_attn_w_out, v_ffn_w_gate, v_ffn_w_up, v_ffn_w_down, v_ln_gain, v_ln_bias):
    w = dict(zip(WEIGHTS, (meta_tokens, s5_lambda_re, s5_lambda_im, s5_log_dt, s5_b_re, s5_b_im, s5_c_re, s5_c_im, s5_d, s5_w_glu, s5_w_out, attn_w_qkv, attn_q_gain, attn_k_gain, attn_w_out, ffn_w_gate, ffn_w_up, ffn_w_down, ln_gain, ln_bias)))
    mom = dict(zip(WEIGHTS, (m_meta_tokens, m_s5_lambda_re, m_s5_lambda_im, m_s5_log_dt, m_s5_b_re, m_s5_b_im, m_s5_c_re, m_s5_c_im, m_s5_d, m_s5_w_glu, m_s5_w_out, m_attn_w_qkv, m_attn_q_gain, m_attn_k_gain, m_attn_w_out, m_ffn_w_gate, m_ffn_w_up, m_ffn_w_down, m_ln_gain, m_ln_bias)))
    vel = dict(zip(WEIGHTS, (v_meta_tokens, v_s5_lambda_re, v_s5_lambda_im, v_s5_log_dt, v_s5_b_re, v_s5_b_im, v_s5_c_re, v_s5_c_im, v_s5_d, v_s5_w_glu, v_s5_w_out, v_attn_w_qkv, v_attn_q_gain, v_attn_k_gain, v_attn_w_out, v_ffn_w_gate, v_ffn_w_up, v_ffn_w_down, v_ln_gain, v_ln_bias)))
    cc = lax.axis_index("c")
    dev = 4 * lax.axis_index("x") + 2 * lax.axis_index("y") + cc

    mat_rows = jnp.concatenate([_mat_rows(w[n], t, blk) for n, t, blk, _ in MATS]).astype(bf16)
    g_mats = _all_gather("ag_weights", mat_rows)
    g_vecs = _all_gather("ag_vectors", jnp.concatenate([w[n].reshape(-1, 128) for n in VECS]))
    fw, off = {}, 0
    for n, _, blk, _ in MATS:
        rows = w[n].shape[0] * blk
        fw[n] = _mat_full(g_mats[:, off:off + rows], blk)
        off += rows
    off = 0
    for n in VECS:
        rows = w[n].size // 128
        fw[n] = _vec_full(g_vecs[:, off:off + rows]).reshape(w[n].shape[:-1] + (D,))
        off += rows

    loss, grad_x, full = _local_step(x[0], loss_target[0], w, fw)
    loss = lax.psum(loss, AXES)
    grad_x = grad_x[None]

    small_names = REPL + VECS
    mine, theirs = _grad_slots(full, _pack_rows([full[k] for k in small_names], REP_ROWS), cc)
    red = _reduce_scatter(mine, theirs)
    small_all = _all_gather("ag_small_grads", red[MAT_ROWS:]).reshape(REP_ROWS * D)
    g, off = {}, 0
    for n, t, blk, real in MATS:
        rows = w[n].shape[0] * blk
        g[n] = _mat_block(red[off:off + rows], t, blk, real)
        off += rows
    small = dict(zip(small_names, _unpack(small_all, [full[k].shape for k in small_names])))
    for k in REPL:
        g[k] = small[k]
    for k in VECS:
        g[k] = lax.dynamic_slice_in_dim(small[k], dev * 128, 128, axis=small[k].ndim - 1)

    delta, new_m, new_v = {}, {}, {}
    for n, _, _, _ in MATS:
        shp = w[n].shape
        res = _adamw(*[d[n].reshape(-1, shp[-1]) for d in (w, g, mom, vel)])
        delta[n], new_m[n], new_v[n] = [a.reshape(shp) for a in res]
    shapes = [w[k].shape for k in small_names]
    res = _adamw(*[_pack_rows([d[k] for k in small_names], SMALL_ROWS) for d in (w, g, mom, vel)])
    for out, a in zip((delta, new_m, new_v), res):
        out.update(zip(small_names, _unpack(a.reshape(-1), shapes)))
    return (loss, grad_x, *[g[k] for k in WEIGHTS], *[delta[k] for k in WEIGHTS],
            *[new_m[k] for k in WEIGHTS], *[new_v[k] for k in WEIGHTS])
```

```python
import functools
import math

import jax
import jax.numpy as jnp
from jax import lax
from jax.experimental import pallas as pl
from jax.experimental.pallas import tpu as pltpu

f32 = jnp.float32
bf16 = jnp.bfloat16
HI = lax.Precision.HIGHEST
MESH = pl.DeviceIdType.MESH
AXES = ("x", "y", "c")
ANY = pl.BlockSpec(memory_space=pl.ANY)

D = 1024
DEPTH = 4
N_META = 16
PAD = 240
OFF = PAD + N_META
ROW_TILE = 768
KEY_CHUNK = 256
FFN_TILE = 256
GRID_W = 64
HD = 64
NQ = 16
NKV = 4
QW = NQ * HD
KW = NKV * HD
QKVW = QW + 2 * KW
DFF = 2816
GROUPS = 64
GCH = 16
NSTATE = 64
CHUNK = 16
GB = 8
ROPE_THETA = 10000.0
LN_EPS = 1e-5
QK_EPS = 1e-6
ALPHA = (2.0 * DEPTH) ** 0.25
ADAM_LR, ADAM_B1, ADAM_B2, ADAM_EPS, ADAM_WD, ADAM_STEP = 0.001, 0.9, 0.999, 1e-08, 0.01, 10
NEG = -1e30
Q_SCALE = HD ** -0.5 * math.log2(math.e)
VMEM_MB = 56

NT = (((1,), (1,)), ((), ()))
TN = (((0,), (0,)), ((), ()))

WEIGHTS = ['meta_tokens', 's5_lambda_re', 's5_lambda_im', 's5_log_dt', 's5_b_re', 's5_b_im', 's5_c_re', 's5_c_im',
           's5_d', 's5_w_glu', 's5_w_out', 'attn_w_qkv', 'attn_q_gain', 'attn_k_gain', 'attn_w_out', 'ffn_w_gate',
           'ffn_w_up', 'ffn_w_down', 'ln_gain', 'ln_bias']
DFFP = 3072
FF_BLK, FF_BLKP = DFF // 8, DFFP // 8
MATS = [('s5_w_glu', False, 128, 128), ('s5_w_out', False, 128, 128), ('attn_w_qkv', True, 192, 192),
        ('attn_w_out', False, 128, 128), ('ffn_w_gate', True, FF_BLKP, FF_BLK), ('ffn_w_up', True, FF_BLKP, FF_BLK),
        ('ffn_w_down', False, FF_BLKP, FF_BLK)]
VECS = ['meta_tokens', 'ln_gain', 'ln_bias']
REPL = ['s5_lambda_re', 's5_lambda_im', 's5_log_dt', 's5_b_re', 's5_b_im', 's5_c_re', 's5_c_im', 's5_d',
        'attn_q_gain', 'attn_k_gain']
MAT_ROWS = 5760
REP_PIECE = 160
REP_ROWS = 8 * REP_PIECE
RS_TILE = 640
SMALL_ROWS = 1088


def _params(sem, mb=VMEM_MB):
    return pltpu.CompilerParams(dimension_semantics=sem, vmem_limit_bytes=mb << 20)


def _ew(name, fn, rows, consts, outs, accs=(), tile=ROW_TILE):
    first = rows[0][0] if isinstance(rows[0], tuple) else rows[0]
    n = first.shape[-2]
    tile = min(tile, n)
    assert n % tile == 0, (name, n, tile)
    n_in, n_o, n_a = len(rows) + len(consts), len(outs), len(accs)

    def body(*refs):
        i = pl.program_id(0)
        res_o, res_a = fn(i, *[r[...] for r in refs[:n_in]])
        for r, val in zip(refs[n_in:n_in + n_o], res_o):
            r[...] = val.astype(r.dtype)
        if n_a:
            a_refs = refs[n_in + n_o:]

            @pl.when(i == 0)
            def _():
                for r in a_refs:
                    r[...] = jnp.zeros(r.shape, r.dtype)

            for r, val in zip(a_refs, res_a):
                r[...] += val

    in_specs, args = [], []
    for a in rows:
        if isinstance(a, tuple):
            arr, k = a
            in_specs.append(pl.BlockSpec((None, tile, arr.shape[2]), functools.partial(lambda i, k: (k, i, 0), k=k)))
            args.append(arr)
        else:
            in_specs.append(pl.BlockSpec((tile, a.shape[1]), lambda i: (i, 0)))
            args.append(a)
    for c in consts:
        in_specs.append(pl.BlockSpec(c.shape, lambda i: (0, 0)))
        args.append(c)
    out_specs = [pl.BlockSpec((tile, c), lambda i: (i, 0)) for c, _ in outs]
    out_specs += [pl.BlockSpec(s, lambda i: (0, 0)) for s in accs]
    out_shape = [jax.ShapeDtypeStruct((n, c), dt) for c, dt in outs]
    out_shape += [jax.ShapeDtypeStruct(s, f32) for s in accs]
    res = pl.pallas_call(body, grid=(n // tile,), in_specs=in_specs, out_specs=out_specs, out_shape=out_shape,
                         name=name, compiler_params=_params(("arbitrary",)))(*args)
    return res


def _mm(name, a, b, trans_b=False, out_dtype=f32, tm=ROW_TILE):
    m, k = a.shape
    n = b.shape[0] if trans_b else b.shape[1]
    tm = min(tm, m)
    assert m % tm == 0
    dims = NT if trans_b else (((1,), (0,)), ((), ()))

    def body(a_ref, b_ref, o_ref):
        o_ref[...] = lax.dot_general(a_ref[...], b_ref[...], dims, preferred_element_type=f32).astype(o_ref.dtype)

    return pl.pallas_call(
        body, grid=(m // tm,),
        in_specs=[pl.BlockSpec((tm, k), lambda i: (i, 0)), pl.BlockSpec(b.shape, lambda i: (0, 0))],
        out_specs=pl.BlockSpec((tm, n), lambda i: (i, 0)),
        out_shape=jax.ShapeDtypeStruct((m, n), out_dtype), name=name, compiler_params=_params(("parallel",)))(a, b)


def _mm2(name, a1, b1, a2, b2, out_dtype=f32, tm=ROW_TILE // 2):
    m, k = a1.shape
    n = b1.shape[1]
    tm = min(tm, m)
    assert m % tm == 0

    def body(a1_ref, b1_ref, a2_ref, b2_ref, o_ref):
        acc = jnp.dot(a1_ref[...], b1_ref[...], preferred_element_type=f32)
        acc += jnp.dot(a2_ref[...], b2_ref[...], preferred_element_type=f32)
        o_ref[...] = acc.astype(o_ref.dtype)

    row = pl.BlockSpec((tm, k), lambda i: (i, 0))
    whole = pl.BlockSpec(b1.shape, lambda i: (0, 0))
    return pl.pallas_call(
        body, grid=(m // tm,), in_specs=[row, whole, row, whole], out_specs=pl.BlockSpec((tm, n), lambda i: (i, 0)),
        out_shape=jax.ShapeDtypeStruct((m, n), out_dtype), name=name,
        compiler_params=_params(("parallel",)))(a1, b1, a2, b2)


def _mm_tn(name, a, g, tk=512, tl=ROW_TILE):
    rows, k1 = a.shape
    n = g.shape[1]
    tl = min(tl, rows)
    assert rows % tl == 0 and k1 % tk == 0

    def body(a_ref, g_ref, o_ref):
        @pl.when(pl.program_id(1) == 0)
        def _():
            o_ref[...] = jnp.zeros(o_ref.shape, f32)

        o_ref[...] += lax.dot_general(a_ref[...], g_ref[...], TN, preferred_element_type=f32)

    return pl.pallas_call(
        body, grid=(k1 // tk, rows // tl),
        in_specs=[pl.BlockSpec((tl, tk), lambda k, l: (l, k)), pl.BlockSpec((tl, n), lambda k, l: (l, 0))],
        out_specs=pl.BlockSpec((tk, n), lambda k, l: (k, 0)),
        out_shape=jax.ShapeDtypeStruct((k1, n), f32), name=name,
        compiler_params=_params(("parallel", "arbitrary")))(a, g)


def _ln_stats(r):
    mean = jnp.mean(r, axis=-1, keepdims=True)
    c = r - mean
    rstd = lax.rsqrt(jnp.mean(c * c, axis=-1, keepdims=True) + LN_EPS)
    return c * rstd, rstd


def _ln_fwd(h, mix, gain, bias):
    def fn(i, h, mix, g, b):
        r = ALPHA * h + mix
        y = _ln_stats(r)[0] * g + b
        return (r, y, y), ()

    return _ew("ln_fwd", fn, [h, mix], [gain, bias], [(D, f32), (D, f32), (D, bf16)])


def _ln_bwd(d_a, d_b, r, gain):
    def core(dout, r, g):
        xhat, rstd = _ln_stats(r)
        dxh = dout * g
        dr = rstd * (dxh - jnp.mean(dxh, axis=-1, keepdims=True) - xhat * jnp.mean(dxh * xhat, axis=-1, keepdims=True))
        return (dr, dr), (jnp.sum(dout * xhat, axis=0, keepdims=True), jnp.sum(dout, axis=0, keepdims=True))

    outs, accs = [(D, f32), (D, bf16)], [(1, D), (1, D)]
    if d_a is None:
        return _ew("ln_bwd_top", lambda i, d, r, g: core(d, r, g), [d_b, r], [gain], outs, accs)
    return _ew("ln_bwd", lambda i, da, db, r, g: core(ALPHA * da + db, r, g), [d_a, d_b, r], [gain], outs, accs)


def _sigmoid(x):
    return 1.0 / (1.0 + jnp.exp(-x))


def _ffn_up(h_bf, w_gate_t, w_up_t, tm=FFN_TILE):
    m, k = h_bf.shape
    n = w_gate_t.shape[0]

    def body(h_ref, wg_ref, wu_ref, g_ref, u_ref, a_ref):
        h = h_ref[...]
        g = lax.dot_general(h, wg_ref[...], NT, preferred_element_type=f32).astype(bf16)
        u = lax.dot_general(h, wu_ref[...], NT, preferred_element_type=f32).astype(bf16)
        g_ref[...] = g
        u_ref[...] = u
        g = g.astype(f32)
        a_ref[...] = (g * _sigmoid(g) * u.astype(f32)).astype(bf16)

    row = pl.BlockSpec((tm, n), lambda i: (i, 0))
    whole = pl.BlockSpec((n, k), lambda i: (0, 0))
    return pl.pallas_call(
        body, grid=(m // tm,), in_specs=[pl.BlockSpec((tm, k), lambda i: (i, 0)), whole, whole],
        out_specs=[row, row, row], out_shape=[jax.ShapeDtypeStruct((m, n), bf16)] * 3, name="ffn_up",
        compiler_params=_params(("parallel",)))(h_bf, w_gate_t, w_up_t)


def _ffn_dup(df_bf, w_down, gate, up, tm=FFN_TILE):
    m, k = df_bf.shape
    n = w_down.shape[0]

    def body(d_ref, w_ref, g_ref, u_ref, dg_ref, du_ref):
        da = lax.dot_general(d_ref[...], w_ref[...], NT, preferred_element_type=f32).astype(bf16).astype(f32)
        g, u = g_ref[...].astype(f32), u_ref[...].astype(f32)
        s = _sigmoid(g)
        dg_ref[...] = (da * u * s * (1.0 + g * (1.0 - s))).astype(bf16)
        du_ref[...] = (da * g * s).astype(bf16)

    row = pl.BlockSpec((tm, n), lambda i: (i, 0))
    return pl.pallas_call(
        body, grid=(m // tm,),
        in_specs=[pl.BlockSpec((tm, k), lambda i: (i, 0)), pl.BlockSpec((n, k), lambda i: (0, 0)), row, row],
        out_specs=[row, row], out_shape=[jax.ShapeDtypeStruct((m, n), bf16)] * 2, name="ffn_dup",
        compiler_params=_params(("parallel",)))(df_bf, w_down, gate, up)


def _loss_grad(y, target):
    def fn(i, y, t):
        e = y - t
        return (e * (1.0 / D),), (jnp.sum(e * e, axis=0, keepdims=True),)

    return _ew("loss", fn, [y, target], [], [(D, f32)], [(1, D)], tile=512)


def _adamw(w, g, m, v):
    def fn(i, w, g, m, v):
        m = ADAM_B1 * m + (1.0 - ADAM_B1) * g
        v = ADAM_B2 * v + (1.0 - ADAM_B2) * jnp.square(g)
        m_hat = m / (1.0 - ADAM_B1 ** ADAM_STEP)
        v_hat = v / (1.0 - ADAM_B2 ** ADAM_STEP)
        delta = -ADAM_LR * (m_hat / (jnp.sqrt(v_hat) + ADAM_EPS) + ADAM_WD * w)
        return (delta, m, v), ()

    rows, cols = w.shape
    tile = max(t for t in range(8, min(rows, 544) + 1, 8) if rows % t == 0)
    return _ew("adamw", fn, [w, g, m, v], [], [(cols, f32)] * 3, tile=tile)


def _s5_mats(lam_re, lam_im, log_dt, b_re, b_im, c_re, c_im):
    steps = jnp.arange(CHUNK + 1, dtype=f32)
    shift = (jnp.arange(CHUNK)[None, None, :] - jnp.arange(CHUNK)[None, :, None]
             == jnp.arange(CHUNK)[:, None, None]).astype(f32)

    def one(lr, li, ldt, br, bi, cr, ci, reverse):
        dt = jnp.exp(ldt)[:, None]
        mag = jnp.exp(lr * dt)
        abr, abi = mag * jnp.cos(li * dt), mag * jnp.sin(li * dt)
        nr, ni = abr - 1.0, abi
        den = lr * lr + li * li
        zr, zi = (nr * lr + ni * li) / den, (ni * lr - nr * li) / den
        bbr = zr[..., None] * br - zi[..., None] * bi
        bbi = zr[..., None] * bi + zi[..., None] * br
        pmag = jnp.exp(steps[:, None, None] * (lr * dt)[None])
        pang = steps[:, None, None] * (li * dt)[None]
        pr, pi = pmag * jnp.cos(pang), pmag * jnp.sin(pang)
        car = cr[None] * pr[:, :, None, :] - ci[None] * pi[:, :, None, :]
        cai = cr[None] * pi[:, :, None, :] + ci[None] * pr[:, :, None, :]
        kern = (jnp.einsum('jgop,gpi->jgoi', car[:CHUNK], bbr, precision=HI)
                - jnp.einsum('jgop,gpi->jgoi', cai[:CHUNK], bbi, precision=HI))
        m = jnp.einsum('jgoi,jst->gsito', kern, shift, precision=HI)
        qr, qi = pr[:CHUNK][::-1], pi[:CHUNK][::-1]
        pin_re = qr[:, :, :, None] * bbr[None] - qi[:, :, :, None] * bbi[None]
        pin_im = qr[:, :, :, None] * bbi[None] + qi[:, :, :, None] * bbr[None]
        pin = jnp.stack([pin_re, pin_im], 0).transpose(2, 1, 4, 0, 3)
        pout = jnp.stack([car[1:], -cai[1:]], 0).transpose(2, 0, 4, 1, 3)
        if reverse:
            m, pin, pout = m[:, ::-1, :, ::-1, :], pin[:, ::-1], pout[:, :, :, ::-1, :]
        n = CHUNK * GCH
        return m.reshape(GROUPS, n, n), pin.reshape(GROUPS, n, 2 * NSTATE), pout.reshape(GROUPS, 2 * NSTATE, n), pr[CHUNK], pi[CHUNK]

    mf, pinf, poutf, arf, aif = one(lam_re[0], lam_im[0], log_dt[0], b_re[0], b_im[0], c_re[0], c_im[0], False)
    mr, pinr, poutr, arr, air = one(lam_re[1], lam_im[1], log_dt[1], b_re[1], b_im[1], c_re[1], c_im[1], True)
    return (mf + mr, jnp.concatenate([pinf, pinr], 2), jnp.concatenate([poutf, poutr], 1),
            jnp.stack([arf, arr]), jnp.stack([aif, air]))


def _s5_coefs(a_re, a_im):
    c1 = jnp.concatenate([a_re, a_re], -1)
    c2 = jnp.concatenate([-a_im, a_im], -1)
    return tuple(c.reshape(GROUPS // GB, 1, GB * 2 * NSTATE) for c in (c1[0], c2[0], c1[1], c2[1]))


def _swap(s):
    w = s.shape[1]
    lane = lax.broadcasted_iota(jnp.int32, s.shape, 1)
    return jnp.where(lane % (2 * NSTATE) < NSTATE, pltpu.roll(s, w - NSTATE, 1), pltpu.roll(s, NSTATE, 1))


def _group_lanes(g):
    return slice(g * 2 * NSTATE, (g + 1) * 2 * NSTATE)


def _s5_states(nc, u_ref, pin_ref, coef, vf, vr, wf, wr, sf, sr):
    c1f, c2f, c1r, c2r = coef
    for g in range(GB):
        v = jnp.dot(u_ref[g], pin_ref[g], preferred_element_type=f32)
        vf[:, _group_lanes(g)] = v[:, :2 * NSTATE]
        vr[:, _group_lanes(g)] = v[:, 2 * NSTATE:]
    wf[...] = _swap(vf[...])
    wr[...] = _swap(vr[...])

    def step(i, carry):
        s_f, t_f, s_r, t_r = carry
        kf, kr = pl.ds(i, 1), pl.ds(nc - 1 - i, 1)
        sf[kf, :] = s_f
        sr[kr, :] = s_r
        s_f, t_f = c1f * s_f + c2f * t_f + vf[kf, :], c1f * t_f - c2f * s_f + wf[kf, :]
        s_r, t_r = c1r * s_r + c2r * t_r + vr[kr, :], c1r * t_r - c2r * s_r + wr[kr, :]
        return s_f, t_f, s_r, t_r

    z = jnp.zeros((1, GB * 2 * NSTATE), f32)
    lax.fori_loop(0, nc, step, (z, z, z, z))


def _s5_core_fwd(ug, msum, pin, pout, coefs):
    nc = ug.shape[1]
    n = CHUNK * GCH

    def body(u_ref, m_ref, pin_ref, pout_ref, c1f, c2f, c1r, c2r, y_ref, vf, vr, wf, wr, sf, sr):
        coef = (c1f[...], c2f[...], c1r[...], c2r[...])
        _s5_states(nc, u_ref, pin_ref, coef, vf, vr, wf, wr, sf, sr)
        for g in range(GB):
            s_in = jnp.concatenate([sf[:, _group_lanes(g)], sr[:, _group_lanes(g)]], axis=1).astype(bf16)
            y_ref[g] = (jnp.dot(u_ref[g], m_ref[g], preferred_element_type=f32)
                        + jnp.dot(s_in, pout_ref[g], preferred_element_type=f32))

    seq = pl.BlockSpec((GB, nc, n), lambda i: (i, 0, 0))
    mat = pl.BlockSpec((GB, n, n), lambda i: (i, 0, 0))
    cf = pl.BlockSpec((None, 1, GB * 2 * NSTATE), lambda i: (i, 0, 0))
    scr = pltpu.VMEM((nc, GB * 2 * NSTATE), f32)
    return pl.pallas_call(
        body, grid=(GROUPS // GB,), in_specs=[seq, mat, mat, mat, cf, cf, cf, cf], out_specs=seq,
        out_shape=jax.ShapeDtypeStruct((GROUPS, nc, n), f32), scratch_shapes=[scr] * 6,
        name="s5_core_fwd", compiler_params=_params(("parallel",)))(ug, msum, pin, pout, *coefs)


def _s5_core_bwd(ug, dyg, msum, pin, pout, coefs):
    nc = ug.shape[1]
    n = CHUNK * GCH

    def body(u_ref, dy_ref, m_ref, pin_ref, pout_ref, c1f, c2f, c1r, c2r,
             du_ref, dm_ref, dpin_ref, dpout_ref, a1f_ref, a2f_ref, a1r_ref, a2r_ref, vf, vr, wf, wr, sf, sr):
        coef = (c1f[...], c2f[...], c1r[...], c2r[...])
        _s5_states(nc, u_ref, pin_ref, coef, vf, vr, wf, wr, sf, sr)
        for g in range(GB):
            s_in = jnp.concatenate([sf[:, _group_lanes(g)], sr[:, _group_lanes(g)]], axis=1).astype(bf16)
            dy = dy_ref[g]
            ds = lax.dot_general(dy, pout_ref[g], NT, preferred_element_type=f32)
            vf[:, _group_lanes(g)] = ds[:, :2 * NSTATE]
            vr[:, _group_lanes(g)] = ds[:, 2 * NSTATE:]
            dpout_ref[g] = lax.dot_general(s_in, dy, TN, preferred_element_type=f32)
            dm_ref[g] = lax.dot_general(u_ref[g], dy, TN, preferred_element_type=f32)

        wf[...] = _swap(vf[...])
        wr[...] = _swap(vr[...])
        k1f, k2f, k1r, k2r = coef[0], -coef[1], coef[2], -coef[3]

        def step(i, carry):
            g_f, h_f, g_r, h_r, a1f, b2f, a1r, b2r = carry
            kf, kr = pl.ds(nc - 1 - i, 1), pl.ds(i, 1)
            s_f, s_r = sf[kf, :], sr[kr, :]
            sf[kf, :] = g_f
            sr[kr, :] = g_r
            a1f, b2f = a1f + g_f * s_f, b2f + h_f * s_f
            a1r, b2r = a1r + g_r * s_r, b2r + h_r * s_r
            g_f, h_f = vf[kf, :] + k1f * g_f + k2f * h_f, wf[kf, :] + k1f * h_f - k2f * g_f
            g_r, h_r = vr[kr, :] + k1r * g_r + k2r * h_r, wr[kr, :] + k1r * h_r - k2r * g_r
            return g_f, h_f, g_r, h_r, a1f, b2f, a1r, b2r

        z = jnp.zeros((1, GB * 2 * NSTATE), f32)
        _, _, _, _, a1f, b2f, a1r, b2r = lax.fori_loop(0, nc, step, (z,) * 8)
        a1f_ref[...], a2f_ref[...], a1r_ref[...], a2r_ref[...] = a1f, _swap(b2f), a1r, _swap(b2r)
        for g in range(GB):
            dv = jnp.concatenate([sf[:, _group_lanes(g)], sr[:, _group_lanes(g)]], axis=1).astype(bf16)
            du_ref[g] = (lax.dot_general(dy_ref[g], m_ref[g], NT, preferred_element_type=f32)
                         + lax.dot_general(dv, pin_ref[g], NT, preferred_element_type=f32))
            dpin_ref[g] = lax.dot_general(u_ref[g], dv, TN, preferred_element_type=f32)

    seq = pl.BlockSpec((GB, nc, n), lambda i: (i, 0, 0))
    mat = pl.BlockSpec((GB, n, n), lambda i: (i, 0, 0))
    cf = pl.BlockSpec((None, 1, GB * 2 * NSTATE), lambda i: (i, 0, 0))
    scr = pltpu.VMEM((nc, GB * 2 * NSTATE), f32)
    mat_s = jax.ShapeDtypeStruct((GROUPS, n, n), f32)
    cf_s = jax.ShapeDtypeStruct((GROUPS // GB, 1, GB * 2 * NSTATE), f32)
    return pl.pallas_call(
        body, grid=(GROUPS // GB,), in_specs=[seq, seq, mat, mat, mat, cf, cf, cf, cf],
        out_specs=[seq, mat, mat, mat, cf, cf, cf, cf],
        out_shape=[jax.ShapeDtypeStruct((GROUPS, nc, n), f32), mat_s, mat_s, mat_s, cf_s, cf_s, cf_s, cf_s],
        scratch_shapes=[scr] * 6, name="s5_core_bwd",
        compiler_params=_params(("parallel",)))(ug, dyg, msum, pin, pout, *coefs)


def _to_groups(a):
    n = a.shape[0]
    return a.reshape(n // CHUNK, CHUNK, GROUPS, GCH).transpose(2, 0, 1, 3).reshape(GROUPS, n // CHUNK, CHUNK * GCH)


def _from_groups(g):
    nc = g.shape[1]
    return g.reshape(GROUPS, nc, CHUNK, GCH).transpose(1, 2, 0, 3).reshape(nc * CHUNK, D)


def _gelu(y):
    return 0.5 * y * (1.0 + lax.erf(y * (2.0 ** -0.5)))


def _gelu_grad(y):
    return 0.5 * (1.0 + lax.erf(y * (2.0 ** -0.5))) + y * jnp.exp(-0.5 * y * y) * (1.0 / math.sqrt(2.0 * math.pi))


def _s5_fwd(h, h_bf, valid, mats, d_skip, w_glu, w_out):
    msum, pin, pout, a_re, a_im = mats
    coefs = _s5_coefs(a_re, a_im)
    ug = _to_groups(jnp.where(valid, h_bf, jnp.zeros_like(h_bf)))
    ys = _from_groups(_s5_core_fwd(ug, msum.astype(bf16), pin.astype(bf16), pout.astype(bf16), coefs))

    def post(i, ys, h, d):
        y = ys + d * h
        return (y, _gelu(y)), ()

    y, g_bf = _ew("s5_gelu", post, [ys, h], [d_skip], [(D, f32), (D, bf16)])
    gw = _mm("s5_glu_mm", g_bf, w_glu)

    def glu(i, y, gw):
        return (_gelu(y) * _sigmoid(gw),), ()

    z_bf = _ew("s5_glu", glu, [y, gw], [], [(D, bf16)])[0]
    mix = _mm("s5_out_mm", z_bf, w_out)
    return mix, (ug, y, g_bf, gw, z_bf)


def _s5_bwd(dmix_bf, h, valid, saved, mats, vjp_mats, d_skip, w_glu, w_out):
    ug, y, g_bf, gw, z_bf = saved
    msum, pin, pout, a_re, a_im = mats
    coefs = _s5_coefs(a_re, a_im)
    dz = _mm("s5_dz_mm", dmix_bf, w_out, trans_b=True)
    d_w_out = _mm_tn("s5_dwout", z_bf, dmix_bf)

    def dglu(i, dz, y, gw):
        g, s = _gelu(y), _sigmoid(gw)
        return (dz * g * s * (1.0 - s), dz * s), ()

    dgw_bf, dg1 = _ew("s5_dglu", dglu, [dz, y, gw], [], [(D, bf16), (D, f32)])
    d_w_glu = _mm_tn("s5_dwglu", g_bf, dgw_bf)
    dg2 = _mm("s5_dg_mm", dgw_bf, w_glu, trans_b=True)

    def dgelu(i, dg1, dg2, y, h, d):
        dy = (dg1 + dg2) * _gelu_grad(y)
        return (dy, dy * d), (jnp.sum(dy * h, axis=0, keepdims=True),)

    dy_bf, dh_skip, dd = _ew("s5_dgelu", dgelu, [dg1, dg2, y, h], [d_skip], [(D, bf16), (D, f32)], [(1, D)])
    dug, dm, dpin, dpout, a1f, a2f, a1r, a2r = _s5_core_bwd(
        ug, _to_groups(dy_bf), msum.astype(bf16), pin.astype(bf16), pout.astype(bf16), coefs)
    du = _from_groups(dug)
    dh = dh_skip + jnp.where(valid, du, jnp.zeros_like(du))
    a1 = jnp.stack([a1f, a1r]).reshape(2, GROUPS, 2 * NSTATE)
    a2 = jnp.stack([a2f, a2r]).reshape(2, GROUPS, 2 * NSTATE)
    da_re = a1[..., :NSTATE] + a1[..., NSTATE:]
    da_im = a2[..., NSTATE:] - a2[..., :NSTATE]
    d_params = vjp_mats((dm, dpin, dpout, da_re, da_im))
    return dh, d_params, dd[0], d_w_glu, d_w_out


def _rope_tables(n):
    row = jnp.arange(n, dtype=jnp.int32) - OFF
    real = row >= 0
    rid = jnp.where(real, row // GRID_W, 0).astype(f32)
    cid = jnp.where(real, row % GRID_W, 0).astype(f32)
    half = HD // 2
    inv = ROPE_THETA ** (-jnp.arange(0, half, 2, dtype=f32) / half)
    ar, ac = rid[:, None] * inv[None, :], cid[:, None] * inv[None, :]
    cos = jnp.concatenate([jnp.cos(ar), jnp.cos(ar), jnp.cos(ac), jnp.cos(ac)], axis=1)
    sin = jnp.concatenate([-jnp.sin(ar), jnp.sin(ar), -jnp.sin(ac), jnp.sin(ac)], axis=1)
    return jnp.tile(cos, (1, 2)), jnp.tile(sin, (1, 2))


def _head_mats():
    head = jnp.arange(QW, dtype=jnp.int32)[:, None] // HD == jnp.arange(128, dtype=jnp.int32)[None, :]
    return head.astype(f32) * (1.0 / HD), head.astype(f32).T


def _rot(v):
    w = v.shape[1]
    lane = lax.broadcasted_iota(jnp.int32, v.shape, 1)
    return jnp.where(lane % 32 < 16, pltpu.roll(v, w - 16, 1), pltpu.roll(v, 16, 1))


def _head_mean(v, e, et):
    w = v.shape[1]
    m = jnp.dot(v, e[:w], preferred_element_type=f32, precision=HI)
    return m, et[:, :w]


def _rms_rope(t, gain, e, et, cos, sin):
    w = t.shape[1]
    ms, spread = _head_mean(t * t, e, et)
    rs = jnp.dot(lax.rsqrt(ms + QK_EPS), spread, preferred_element_type=f32, precision=HI)
    n0 = t * rs
    n = n0 * gain
    reps = w // 128
    return n * jnp.tile(cos, (1, reps)) + _rot(n) * jnp.tile(sin, (1, reps))


def _rms_rope_bwd(dout, t, gain, e, et, cos, sin):
    w = t.shape[1]
    reps = w // 128
    ms, spread = _head_mean(t * t, e, et)
    rs = jnp.dot(lax.rsqrt(ms + QK_EPS), spread, preferred_element_type=f32, precision=HI)
    n0 = t * rs
    dn = dout * jnp.tile(cos, (1, reps)) + _rot(dout * jnp.tile(sin, (1, reps)))
    dn0 = dn * gain
    mm, _ = _head_mean(dn0 * n0, e, et)
    corr = jnp.dot(mm, spread, preferred_element_type=f32, precision=HI)
    return rs * (dn0 - n0 * corr), jnp.sum(dn * n0, axis=0, keepdims=True)


def _qk_fwd(qkv, qg, kg, e, et, cos, sin):
    def fn(i, qkv, cos, sin, qg, kg, e, et):
        q = _rms_rope(qkv[:, :QW], qg, e, et, cos, sin) * Q_SCALE
        k = _rms_rope(qkv[:, QW:QW + KW], kg, e, et, cos, sin)
        return (q, k, qkv[:, QW + KW:]), ()

    return _ew("qk_rope", fn, [qkv, cos, sin], [qg, kg, e, et], [(QW, bf16), (KW, bf16), (KW, bf16)])


def _qk_bwd(qkv, dq, dk, dv, qg, kg, e, et, cos, sin):
    def fn(i, qkv, cos, sin, dq, dk, dv, qg, kg, e, et):
        dtq, dgq = _rms_rope_bwd(dq * (HD ** -0.5), qkv[:, :QW], qg, e, et, cos, sin)
        dtk, dgk = _rms_rope_bwd(dk * math.log(2.0), qkv[:, QW:QW + KW], kg, e, et, cos, sin)
        return (jnp.concatenate([dtq, dtk, dv], axis=1),), (dgq, dgk)

    return _ew("qk_rope_bwd", fn, [qkv, cos, sin, dq, dk, dv], [qg, kg, e, et], [(QKVW, bf16)], [(1, QW), (1, KW)])


def _to_heads(a, nh):
    return a.reshape(a.shape[0], nh, HD).transpose(1, 0, 2)


def _from_heads(a):
    return a.transpose(1, 0, 2).reshape(a.shape[1], a.shape[0] * HD)


def _masked_first(s, c):
    if c:
        return s
    col = lax.broadcasted_iota(jnp.int32, (1, s.shape[1]), 1)
    return jnp.where(col >= PAD, s, NEG)


def _flash_fwd(q, k, v1, tq=ROW_TILE, tc=KEY_CHUNK):
    n = q.shape[1]
    nc = n // tc

    def body(q_ref, k_ref, v_ref, o_ref, lse_ref):
        qb = q_ref[0]

        def scores(c):
            ks = k_ref[0, pl.ds(c * tc, tc), :]
            return _masked_first(lax.dot_general(qb, ks, NT, preferred_element_type=f32), c)

        m = jnp.full((tq, 1), NEG, f32)
        acc = jnp.zeros((tq, 2 * HD), f32)
        s_next = scores(0)
        for c in range(nc):
            s = s_next
            if c + 1 < nc:
                s_next = scores(c + 1)
            m_new = jnp.maximum(m, jnp.max(s, axis=1, keepdims=True))
            p = jnp.exp2(s - m_new)
            acc = jnp.exp2(m - m_new) * acc + jnp.dot(p.astype(bf16), v_ref[0, pl.ds(c * tc, tc), :],
                                                      preferred_element_type=f32)
            m = m_new
        l = acc[:, HD:HD + 1]
        o_ref[0] = acc[:, :HD] / l
        lse_ref[0] = m + jnp.log2(l)

    return pl.pallas_call(
        body, grid=(NQ, n // tq),
        in_specs=[pl.BlockSpec((1, tq, HD), lambda h, i: (h, i, 0)),
                  pl.BlockSpec((1, n, HD), lambda h, i: (h // (NQ // NKV), 0, 0)),
                  pl.BlockSpec((1, n, 2 * HD), lambda h, i: (h // (NQ // NKV), 0, 0))],
        out_specs=[pl.BlockSpec((1, tq, HD), lambda h, i: (h, i, 0)), pl.BlockSpec((1, tq, 1), lambda h, i: (h, i, 0))],
        out_shape=[jax.ShapeDtypeStruct((NQ, n, HD), f32), jax.ShapeDtypeStruct((NQ, n, 1), f32)],
        name="flash_fwd", compiler_params=_params(("parallel", "parallel")))(q, k, v1)


def _flash_dq(q, k, v, do, o, lse, tq=ROW_TILE, tc=KEY_CHUNK):
    n = q.shape[1]
    nc = n // tc

    def body(q_ref, k_ref, v_ref, do_ref, o_ref, lse_ref, dq_ref, delta_ref):
        qb, dob, lse_b = q_ref[0], do_ref[0], lse_ref[0]
        delta = jnp.sum(dob.astype(f32) * o_ref[0], axis=1, keepdims=True)

        def products(c):
            ks = k_ref[0, pl.ds(c * tc, tc), :]
            vs = v_ref[0, pl.ds(c * tc, tc), :]
            s = _masked_first(lax.dot_general(qb, ks, NT, preferred_element_type=f32), c)
            return s, lax.dot_general(dob, vs, NT, preferred_element_type=f32)

        dq = jnp.zeros((tq, HD), f32)
        nxt = products(0)
        for c in range(nc):
            s, dp = nxt
            if c + 1 < nc:
                nxt = products(c + 1)
            ds = jnp.exp2(s - lse_b) * (dp - delta)
            dq = dq + jnp.dot(ds.astype(bf16), k_ref[0, pl.ds(c * tc, tc), :], preferred_element_type=f32)
        dq_ref[0] = dq
        delta_ref[0] = delta

    qspec = pl.BlockSpec((1, tq, HD), lambda h, i: (h, i, 0))
    kspec = pl.BlockSpec((1, n, HD), lambda h, i: (h // (NQ // NKV), 0, 0))
    cspec = pl.BlockSpec((1, tq, 1), lambda h, i: (h, i, 0))
    return pl.pallas_call(
        body, grid=(NQ, n // tq), in_specs=[qspec, kspec, kspec, qspec, qspec, cspec], out_specs=[qspec, cspec],
        out_shape=[jax.ShapeDtypeStruct((NQ, n, HD), f32), jax.ShapeDtypeStruct((NQ, n, 1), f32)],
        name="flash_dq", compiler_params=_params(("parallel", "parallel")))(q, k, v, do, o, lse)


def _flash_dkv(q, k, v, do, lse_row, delta_row, tk=ROW_TILE, tc=KEY_CHUNK):
    n = q.shape[1]
    nc = n // tc
    grp = NQ // NKV

    def body(q_ref, do_ref, lse_ref, delta_ref, k_ref, v_ref, dk_ref, dv_ref):
        kb, vb = k_ref[0], v_ref[0]

        def head(g, carry):
            dk, dv = carry

            def products(c):
                qs = q_ref[g, pl.ds(c * tc, tc), :]
                dos = do_ref[g, pl.ds(c * tc, tc), :]
                return (lax.dot_general(kb, qs, NT, preferred_element_type=f32),
                        lax.dot_general(vb, dos, NT, preferred_element_type=f32))

            nxt = products(0)
            for c in range(nc):
                st, dpt = nxt
                if c + 1 < nc:
                    nxt = products(c + 1)
                pt = jnp.exp2(st - lse_ref[g, :, pl.ds(c * tc, tc)])
                dv = dv + jnp.dot(pt.astype(bf16), do_ref[g, pl.ds(c * tc, tc), :], preferred_element_type=f32)
                dst = pt * (dpt - delta_ref[g, :, pl.ds(c * tc, tc)])
                dk = dk + jnp.dot(dst.astype(bf16), q_ref[g, pl.ds(c * tc, tc), :], preferred_element_type=f32)
            return dk, dv

        z = jnp.zeros((tk, HD), f32)
        dk, dv = lax.fori_loop(0, grp, head, (z, z))
        row = lax.broadcasted_iota(jnp.int32, (tk, 1), 0) + pl.program_id(1) * tk
        dk_ref[0] = jnp.where(row >= PAD, dk, 0.0)
        dv_ref[0] = jnp.where(row >= PAD, dv, 0.0)

    gspec = pl.BlockSpec((grp, n, HD), lambda h, j: (h, 0, 0))
    rspec = pl.BlockSpec((grp, 1, n), lambda h, j: (h, 0, 0))
    kspec = pl.BlockSpec((1, tk, HD), lambda h, j: (h, j, 0))
    return pl.pallas_call(
        body, grid=(NKV, n // tk), in_specs=[gspec, gspec, rspec, rspec, kspec, kspec], out_specs=[kspec, kspec],
        out_shape=[jax.ShapeDtypeStruct((NKV, n, HD), f32)] * 2,
        name="flash_dkv", compiler_params=_params(("parallel", "parallel")))(q, do, lse_row, delta_row, k, v)


def _attn_fwd(h_bf, w_qkv_t, qg, kg, w_out, tabs):
    e, et, cos, sin = tabs
    qkv = _mm("attn_qkv_mm", h_bf, w_qkv_t, trans_b=True)
    q_bf, k_bf, v_bf = _qk_fwd(qkv, qg, kg, e, et, cos, sin)
    q16, k4, v4 = _to_heads(q_bf, NQ), _to_heads(k_bf, NKV), _to_heads(v_bf, NKV)
    ones = jnp.zeros((NKV, v4.shape[1], HD), bf16).at[:, :, 0].set(1.0)
    o16, lse = _flash_fwd(q16, k4, jnp.concatenate([v4, ones], axis=2))
    o_bf = _from_heads(o16).astype(bf16)
    mix = _mm("attn_out_mm", o_bf, w_out)
    return mix, (qkv, q16, k4, v4, o16, lse, o_bf)


def _attn_bwd(dmix_bf, h_bf, saved, w_qkv_t, qg, kg, w_out, tabs):
    e, et, cos, sin = tabs
    qkv, q16, k4, v4, o16, lse, o_bf = saved
    n = qkv.shape[0]
    do = _mm("attn_do_mm", dmix_bf, w_out, trans_b=True, out_dtype=bf16)
    d_w_out = _mm_tn("attn_dwout", o_bf, dmix_bf)
    do16 = _to_heads(do, NQ)
    dq16, delta = _flash_dq(q16, k4, v4, do16, o16, lse)
    dk4, dv4 = _flash_dkv(q16, k4, v4, do16, lse.reshape(NQ, 1, n), delta.reshape(NQ, 1, n))
    dqkv_bf, dgq, dgk = _qk_bwd(qkv, _from_heads(dq16), _from_heads(dk4), _from_heads(dv4), qg, kg, e, et, cos, sin)
    d_w_qkv_t = _mm_tn("attn_dwqkv", dqkv_bf, h_bf)
    dh = _mm("attn_dh_mm", dqkv_bf, w_qkv_t)
    return dh, d_w_qkv_t, dgq.reshape(NQ, HD).sum(0), dgk.reshape(NKV, HD).sum(0), d_w_out


def _all_gather(name, shard):
    def body(x_ref, out_ref, send_sems, recv_sems, local_sem):
        x, y, c = lax.axis_index("x"), lax.axis_index("y"), lax.axis_index("c")
        me, sibling = (x, y, c), (x, y, 1 - c)
        chips = [(1 - x, y), (x, 1 - y), (1 - x, 1 - y)]

        def slot(px, py, pc):
            return out_ref.at[4 * px + 2 * py + pc]

        def copy(k, block, to, src=None):
            return pltpu.make_async_remote_copy(
                src_ref=slot(*block) if src is None else src, dst_ref=slot(*block),
                send_sem=send_sems.at[k], recv_sem=recv_sems.at[k], device_id=to, device_id_type=MESH)

        mine = pltpu.make_async_copy(x_ref, slot(*me), local_sem)
        mine.start()
        first = [copy(0, me, sibling, src=x_ref)]
        first += [copy(1 + j, me, (*chip, c), src=x_ref) for j, chip in enumerate(chips)]
        for cp in first:
            cp.start()
        passed = [copy(4 + j, (*chip, c), sibling) for j, chip in enumerate(chips)]
        for j, chip in enumerate(chips):
            copy(1 + j, (*chip, c), me).wait_recv()
            passed[j].start()
        copy(0, sibling, me).wait_recv()
        for j, chip in enumerate(chips):
            copy(4 + j, (*chip, 1 - c), me).wait_recv()
        for cp in first + passed:
            cp.wait_send()
        mine.wait()

    return pl.pallas_call(
        body, out_shape=jax.ShapeDtypeStruct((8,) + shard.shape, shard.dtype), in_specs=[ANY], out_specs=ANY,
        scratch_shapes=[pltpu.SemaphoreType.DMA((7,)), pltpu.SemaphoreType.DMA((7,)), pltpu.SemaphoreType.DMA],
        name=name)(shard)


def _swap_sibling(name, theirs):
    k = len(theirs)

    def body(*refs):
        src, dst, send_sems, recv_sems = refs[:k], refs[k:2 * k], refs[2 * k], refs[2 * k + 1]
        x, y, c = lax.axis_index("x"), lax.axis_index("y"), lax.axis_index("c")
        copies = [pltpu.make_async_remote_copy(src_ref=src[j], dst_ref=dst[j], send_sem=send_sems.at[j],
                                               recv_sem=recv_sems.at[j], device_id=(x, y, 1 - c), device_id_type=MESH)
                  for j in range(k)]
        for cp in copies:
            cp.start()
        for cp in copies:
            cp.wait()

    return pl.pallas_call(
        body, out_shape=[jax.ShapeDtypeStruct(a.shape, a.dtype) for a in theirs], in_specs=[ANY] * k,
        out_specs=[ANY] * k, scratch_shapes=[pltpu.SemaphoreType.DMA((k,)), pltpu.SemaphoreType.DMA((k,))],
        name=name)(*theirs)


def _exchange_chips(name, parts):
    k = len(parts)

    def body(*refs):
        p_refs, t_refs = refs[:k], refs[k:2 * k]
        send_sems, recv_sems, local_sems = refs[2 * k:]
        x, y, c = lax.axis_index("x"), lax.axis_index("y"), lax.axis_index("c")
        q = 2 * x + y
        copies = []
        for j in range(k):
            copies.append(pltpu.make_async_copy(p_refs[j].at[q], t_refs[j].at[q], local_sems.at[j]))
            for hop in (1, 2, 3):
                tx, ty = x ^ (hop >> 1), y ^ (hop & 1)
                copies.append(pltpu.make_async_remote_copy(
                    src_ref=p_refs[j].at[2 * tx + ty], dst_ref=t_refs[j].at[q], send_sem=send_sems.at[3 * j + hop - 1],
                    recv_sem=recv_sems.at[3 * j + hop - 1], device_id=(tx, ty, c), device_id_type=MESH))
        for cp in copies:
            cp.start()
        for cp in copies:
            cp.wait()

    return pl.pallas_call(
        body, out_shape=[jax.ShapeDtypeStruct(a.shape, a.dtype) for a in parts], in_specs=[ANY] * k,
        out_specs=[ANY] * k,
        scratch_shapes=[pltpu.SemaphoreType.DMA((3 * k,)), pltpu.SemaphoreType.DMA((3 * k,)),
                        pltpu.SemaphoreType.DMA((k,))],
        name=name)(*parts)


def _reduce_scatter(mine, theirs):
    got = _swap_sibling("rs_sibling", list(theirs))
    parts = []
    for a, b, dt, nm in zip(mine, got, (bf16, f32), ("rs_add2", "rs_add2_small")):
        rows = 4 * a.shape[1]
        parts.append(_ew(nm, lambda i, a, b: ((a + b,), ()), [a.reshape(rows, D), b.reshape(rows, D)], [],
                         [(D, dt)], tile=RS_TILE)[0].reshape(a.shape))
    ts = _exchange_chips("rs_chips", parts)

    def add4(i, a, b, c, d):
        return ((((a.astype(f32) + b.astype(f32)) + c.astype(f32)) + d.astype(f32),), ())

    return [_ew(nm, add4, [(t, 0), (t, 1), (t, 2), (t, 3)], [], [(D, f32)], tile=RS_TILE)[0]
            for t, nm in zip(ts, ("rs_add4", "rs_add4_small"))]


def _pack_rows(parts, rows):
    flat = jnp.concatenate([p.reshape(-1) for p in parts])
    return jnp.pad(flat, (0, rows * D - flat.shape[0])).reshape(rows, D)


def _unpack(flat, shapes):
    out, off = [], 0
    for s in shapes:
        n = math.prod(s)
        out.append(flat[off:off + n].reshape(s))
        off += n
    return out


def _mat_rows(block, transposed, blk):
    a = jnp.swapaxes(block, 1, 2) if transposed else block
    a = jnp.pad(a, ((0, 0), (0, blk - a.shape[1]), (0, 0)))
    return a.reshape(-1, D)


def _mat_block(rows, transposed, blk, real):
    a = rows.reshape(-1, blk, D)[:, :real]
    return jnp.swapaxes(a, 1, 2) if transposed else a


def _mat_full(gathered, blk):
    layers = gathered.shape[1] // blk
    return gathered.reshape(8, layers, blk, D).transpose(1, 0, 2, 3).reshape(layers, 8 * blk, D)


def _vec_full(gathered):
    return gathered.transpose(1, 0, 2).reshape(gathered.shape[1], D)


def _grad_slots(full, small, cc):
    def halves(a, blk):
        layers = a.shape[0]
        a5 = a.reshape(layers, 4, 2, blk, D)
        res = []
        for sel in (cc, 1 - cc):
            s = lax.dynamic_index_in_dim(a5, sel, axis=2, keepdims=False)
            res.append(s.transpose(1, 0, 2, 3).reshape(4, layers * blk, D))
        return res

    mine, theirs = [], []
    for name, _, blk, _ in MATS:
        a, b = halves(full[name], blk)
        mine.append(a)
        theirs.append(b)
    a, b = halves(small[None], REP_PIECE)
    return (jnp.concatenate(mine, axis=1), a), (jnp.concatenate(theirs, axis=1), b)


def _local_step(x0, target0, w, fw):
    seq = x0.shape[0]
    n = OFF + seq
    valid = (jnp.arange(n, dtype=jnp.int32) >= PAD)[:, None]
    h = jnp.concatenate([jnp.zeros((PAD, D), f32), fw['meta_tokens'], x0], axis=0)
    h_bf = h.astype(bf16)
    tabs = _head_mats() + _rope_tables(n)
    qg = [jnp.tile(w['attn_q_gain'][j], NQ)[None, :] for j in range(2)]
    kg = [jnp.tile(w['attn_k_gain'][j], NKV)[None, :] for j in range(2)]
    s5_names = ['s5_lambda_re', 's5_lambda_im', 's5_log_dt', 's5_b_re', 's5_b_im', 's5_c_re', 's5_c_im']
    s5_mats, s5_vjp = [], []
    for j in range(2):
        mats, vjp = jax.vjp(_s5_mats, *[w[k][j] for k in s5_names])
        s5_mats.append(mats)
        s5_vjp.append(vjp)
    saved = []
    for i in range(DEPTH):
        j = i // 2
        if i % 2 == 0:
            mix, sv = _s5_fwd(h, h_bf, valid, s5_mats[j], w['s5_d'][j][None, :], fw['s5_w_glu'][j], fw['s5_w_out'][j])
        else:
            mix, sv = _attn_fwd(h_bf, fw['attn_w_qkv'][j], qg[j], kg[j], fw['attn_w_out'][j], tabs)
        r1, h1, h1_bf = _ln_fwd(h, mix, fw['ln_gain'][i, 0][None, :], fw['ln_bias'][i, 0][None, :])
        gate, up, act = _ffn_up(h1_bf, fw['ffn_w_gate'][i], fw['ffn_w_up'][i])
        f = _mm("ffn_down_mm", act, fw['ffn_w_down'][i])
        r2, h2, h2_bf = _ln_fwd(h1, f, fw['ln_gain'][i, 1][None, :], fw['ln_bias'][i, 1][None, :])
        saved.append((h, h_bf, sv, r1, h1_bf, gate, up, act, r2))
        h, h_bf = h2, h2_bf

    dy, sq = _loss_grad(h[OFF:], target0)
    loss = 0.5 * jnp.sum(sq) * (1.0 / D)

    grads = {k: [None] * (DEPTH if k.startswith('ffn') else 2) for k in WEIGHTS}
    d_ln_gain = [[None, None] for _ in range(DEPTH)]
    d_ln_bias = [[None, None] for _ in range(DEPTH)]
    d_a, d_b = None, jnp.pad(dy, ((OFF, 0), (0, 0)))
    for i in reversed(range(DEPTH)):
        j = i // 2
        h_in, h_in_bf, sv, r1, h1_bf, gate, up, act, r2 = saved[i]
        dr2, dr2_bf, dg, db = _ln_bwd(d_a, d_b, r2, fw['ln_gain'][i, 1][None, :])
        d_ln_gain[i][1], d_ln_bias[i][1] = dg[0], db[0]
        dgate, dup = _ffn_dup(dr2_bf, fw['ffn_w_down'][i], gate, up)
        grads['ffn_w_down'][i] = _mm_tn("ffn_dwdown", act, dr2_bf, tk=DFFP // 2)
        grads['ffn_w_gate'][i] = _mm_tn("ffn_dwgate", dgate, h1_bf, tk=DFFP // 2)
        grads['ffn_w_up'][i] = _mm_tn("ffn_dwup", dup, h1_bf, tk=DFFP // 2)
        dh1 = _mm2("ffn_dh_mm", dgate, fw['ffn_w_gate'][i], dup, fw['ffn_w_up'][i])
        dr1, dr1_bf, dg, db = _ln_bwd(dr2, dh1, r1, fw['ln_gain'][i, 0][None, :])
        d_ln_gain[i][0], d_ln_bias[i][0] = dg[0], db[0]
        if i % 2 == 0:
            dh, d_par, dd, d_w_glu, d_w_out = _s5_bwd(dr1_bf, h_in, valid, sv, s5_mats[j], s5_vjp[j],
                                                      w['s5_d'][j][None, :], fw['s5_w_glu'][j], fw['s5_w_out'][j])
            for k, g in zip(s5_names, d_par):
                grads[k][j] = g
            grads['s5_d'][j], grads['s5_w_glu'][j], grads['s5_w_out'][j] = dd, d_w_glu, d_w_out
        else:
            dh, d_w_qkv, dgq, dgk, d_w_out = _attn_bwd(dr1_bf, h_in_bf, sv, fw['attn_w_qkv'][j], qg[j], kg[j],
                                                       fw['attn_w_out'][j], tabs)
            grads['attn_w_qkv'][j], grads['attn_w_out'][j] = d_w_qkv, d_w_out
            grads['attn_q_gain'][j], grads['attn_k_gain'][j] = dgq, dgk
        d_a, d_b = dr1, dh
    dh0 = _ew("dh0", lambda i, a, b: ((ALPHA * a + b,), ()), [d_a, d_b], [], [(D, f32)])[0]
    full = {k: jnp.stack(v) for k, v in grads.items() if v[0] is not None}
    full['meta_tokens'] = dh0[PAD:OFF]
    full['ln_gain'] = jnp.stack([jnp.stack(r) for r in d_ln_gain])
    full['ln_bias'] = jnp.stack([jnp.stack(r) for r in d_ln_bias])

    return loss, dh0[OFF:], full


def kernel(x, meta_tokens, s5_lambda_re, s5_lambda_im, s5_log_dt, s5_b_re, s5_b_im, s5_c_re, s5_c_im, s5_d, s5_w_glu, s5_w_out, attn_w_qkv, attn_q_gain, attn_k_gain, attn_w_out, ffn_w_gate, ffn_w_up, ffn_w_down, ln_gain, ln_bias, loss_target, m_meta_tokens, m_s5_lambda_re, m_s5_lambda_im, m_s5_log_dt, m_s5_b_re, m_s5_b_im, m_s5_c_re, m_s5_c_im, m_s5_d, m_s5_w_glu, m_s5_w_out, m_attn_w_qkv, m_attn_q_gain, m_attn_k_gain, m_attn_w_out, m_ffn_w_gate, m_ffn_w_up, m_ffn_w_down, m_ln_gain, m_ln_bias, v_meta_tokens, v_s5_lambda_re, v_s5_lambda_im, v_s5_log_dt, v_s5_b_re, v_s5_b_im, v_s5_c_re, v_s5_c_im, v_s5_d, v_s5_w_glu, v_s5_w_out, v_attn_w_qkv, v_attn_q_gain, v_attn_k_gain, v_attn_w_out, v_ffn_w_gate, v_ffn_w_up, v_ffn_w_down, v_ln_gain, v_ln_bias):
    w = dict(zip(WEIGHTS, (meta_tokens, s5_lambda_re, s5_lambda_im, s5_log_dt, s5_b_re, s5_b_im, s5_c_re, s5_c_im, s5_d, s5_w_glu, s5_w_out, attn_w_qkv, attn_q_gain, attn_k_gain, attn_w_out, ffn_w_gate, ffn_w_up, ffn_w_down, ln_gain, ln_bias)))
    mom = dict(zip(WEIGHTS, (m_meta_tokens, m_s5_lambda_re, m_s5_lambda_im, m_s5_log_dt, m_s5_b_re, m_s5_b_im, m_s5_c_re, m_s5_c_im, m_s5_d, m_s5_w_glu, m_s5_w_out, m_attn_w_qkv, m_attn_q_gain, m_attn_k_gain, m_attn_w_out, m_ffn_w_gate, m_ffn_w_up, m_ffn_w_down, m_ln_gain, m_ln_bias)))
    vel = dict(zip(WEIGHTS, (v_meta_tokens, v_s5_lambda_re, v_s5_lambda_im, v_s5_log_dt, v_s5_b_re, v_s5_b_im, v_s5_c_re, v_s5_c_im, v_s5_d, v_s5_w_glu, v_s5_w_out, v_attn_w_qkv, v_attn_q_gain, v_attn_k_gain, v_attn_w_out, v_ffn_w_gate, v_ffn_w_up, v_ffn_w_down, v_ln_gain, v_ln_bias)))
    cc = lax.axis_index("c")
    dev = 4 * lax.axis_index("x") + 2 * lax.axis_index("y") + cc

    mat_rows = jnp.concatenate([_mat_rows(w[n], t, blk) for n, t, blk, _ in MATS]).astype(bf16)
    g_mats = _all_gather("ag_weights", mat_rows)
    g_vecs = _all_gather("ag_vectors", jnp.concatenate([w[n].reshape(-1, 128) for n in VECS]))
    fw, off = {}, 0
    for n, _, blk, _ in MATS:
        rows = w[n].shape[0] * blk
        fw[n] = _mat_full(g_mats[:, off:off + rows], blk)
        off += rows
    off = 0
    for n in VECS:
        rows = w[n].size // 128
        fw[n] = _vec_full(g_vecs[:, off:off + rows]).reshape(w[n].shape[:-1] + (D,))
        off += rows

    loss, grad_x, full = _local_step(x[0], loss_target[0], w, fw)
    loss = lax.psum(loss, AXES)
    grad_x = grad_x[None]

    small_names = REPL + VECS
    mine, theirs = _grad_slots(full, _pack_rows([full[k] for k in small_names], REP_ROWS), cc)
    red, red_small = _reduce_scatter(mine, theirs)
    small_all = _all_gather("ag_small_grads", red_small).reshape(REP_ROWS * D)
    g, off = {}, 0
    for n, t, blk, real in MATS:
        rows = w[n].shape[0] * blk
        g[n] = _mat_block(red[off:off + rows], t, blk, real)
        off += rows
    small = dict(zip(small_names, _unpack(small_all, [full[k].shape for k in small_names])))
    for k in REPL:
        g[k] = small[k]
    for k in VECS:
        g[k] = lax.dynamic_slice_in_dim(small[k], dev * 128, 128, axis=small[k].ndim - 1)

    delta, new_m, new_v = {}, {}, {}
    for n, _, _, _ in MATS:
        shp = w[n].shape
        res = _adamw(*[d[n].reshape(-1, shp[-1]) for d in (w, g, mom, vel)])
        delta[n], new_m[n], new_v[n] = [a.reshape(shp) for a in res]
    shapes = [w[k].shape for k in small_names]
    res = _adamw(*[_pack_rows([d[k] for k in small_names], SMALL_ROWS) for d in (w, g, mom, vel)])
    for out, a in zip((delta, new_m, new_v), res):
        out.update(zip(small_names, _unpack(a.reshape(-1), shapes)))
    return (loss, grad_x, *[g[k] for k in WEIGHTS], *[delta[k] for k in WEIGHTS],
            *[new_m[k] for k in WEIGHTS], *[new_v[k] for k in WEIGHTS])
```

```python
import functools
import math

import jax
import jax.numpy as jnp
from jax import lax
from jax.experimental import pallas as pl
from jax.experimental.pallas import tpu as pltpu

f32 = jnp.float32
bf16 = jnp.bfloat16
HI = lax.Precision.HIGHEST
MESH = pl.DeviceIdType.MESH
AXES = ("x", "y", "c")
ANY = pl.BlockSpec(memory_space=pl.ANY)

D = 1024
DEPTH = 4
N_META = 16
PAD = 240
OFF = PAD + N_META
ROW_TILE = 768
KEY_CHUNK = 256
FFN_TILE = 256
GRID_W = 64
HD = 64
NQ = 16
NKV = 4
QW = NQ * HD
KW = NKV * HD
QKVW = QW + 2 * KW
DFF = 2816
GROUPS = 64
GCH = 16
NSTATE = 64
CHUNK = 16
GB = 8
ROPE_THETA = 10000.0
LN_EPS = 1e-5
QK_EPS = 1e-6
ALPHA = (2.0 * DEPTH) ** 0.25
ADAM_LR, ADAM_B1, ADAM_B2, ADAM_EPS, ADAM_WD, ADAM_STEP = 0.001, 0.9, 0.999, 1e-08, 0.01, 10
NEG = -1e30
Q_SCALE = HD ** -0.5 * math.log2(math.e)
VMEM_MB = 56

NT = (((1,), (1,)), ((), ()))
TN = (((0,), (0,)), ((), ()))

WEIGHTS = ['meta_tokens', 's5_lambda_re', 's5_lambda_im', 's5_log_dt', 's5_b_re', 's5_b_im', 's5_c_re', 's5_c_im',
           's5_d', 's5_w_glu', 's5_w_out', 'attn_w_qkv', 'attn_q_gain', 'attn_k_gain', 'attn_w_out', 'ffn_w_gate',
           'ffn_w_up', 'ffn_w_down', 'ln_gain', 'ln_bias']
DFFP = 3072
FF_BLK, FF_BLKP = DFF // 8, DFFP // 8
MATS = [('s5_w_glu', False, 128, 128), ('s5_w_out', False, 128, 128), ('attn_w_qkv', True, 192, 192),
        ('attn_w_out', False, 128, 128), ('ffn_w_gate', True, FF_BLKP, FF_BLK), ('ffn_w_up', True, FF_BLKP, FF_BLK),
        ('ffn_w_down', False, FF_BLKP, FF_BLK)]
VECS = ['meta_tokens', 'ln_gain', 'ln_bias']
REPL = ['s5_lambda_re', 's5_lambda_im', 's5_log_dt', 's5_b_re', 's5_b_im', 's5_c_re', 's5_c_im', 's5_d',
        'attn_q_gain', 'attn_k_gain']
MAT_ROWS = 5760
REP_PIECE = 160
REP_ROWS = 8 * REP_PIECE
RS_TILE = 640
SMALL_ROWS = 1088


def _params(sem, mb=VMEM_MB):
    return pltpu.CompilerParams(dimension_semantics=sem, vmem_limit_bytes=mb << 20)


def _ew(name, fn, rows, consts, outs, accs=(), tile=ROW_TILE):
    first = rows[0][0] if isinstance(rows[0], tuple) else rows[0]
    n = first.shape[-2]
    tile = min(tile, n)
    assert n % tile == 0, (name, n, tile)
    n_in, n_o, n_a = len(rows) + len(consts), len(outs), len(accs)

    def body(*refs):
        i = pl.program_id(0)
        res_o, res_a = fn(i, *[r[...] for r in refs[:n_in]])
        for r, val in zip(refs[n_in:n_in + n_o], res_o):
            r[...] = val.astype(r.dtype)
        if n_a:
            a_refs = refs[n_in + n_o:]

            @pl.when(i == 0)
            def _():
                for r in a_refs:
                    r[...] = jnp.zeros(r.shape, r.dtype)

            for r, val in zip(a_refs, res_a):
                r[...] += val

    in_specs, args = [], []
    for a in rows:
        if isinstance(a, tuple):
            arr, k = a
            in_specs.append(pl.BlockSpec((None, tile, arr.shape[2]), functools.partial(lambda i, k: (k, i, 0), k=k)))
            args.append(arr)
        else:
            in_specs.append(pl.BlockSpec((tile, a.shape[1]), lambda i: (i, 0)))
            args.append(a)
    for c in consts:
        in_specs.append(pl.BlockSpec(c.shape, lambda i: (0, 0)))
        args.append(c)
    out_specs = [pl.BlockSpec((tile, c), lambda i: (i, 0)) for c, _ in outs]
    out_specs += [pl.BlockSpec(s, lambda i: (0, 0)) for s in accs]
    out_shape = [jax.ShapeDtypeStruct((n, c), dt) for c, dt in outs]
    out_shape += [jax.ShapeDtypeStruct(s, f32) for s in accs]
    res = pl.pallas_call(body, grid=(n // tile,), in_specs=in_specs, out_specs=out_specs, out_shape=out_shape,
                         name=name, compiler_params=_params(("arbitrary",)))(*args)
    return res


def _mm(name, a, b, trans_b=False, out_dtype=f32, tm=ROW_TILE):
    m, k = a.shape
    n = b.shape[0] if trans_b else b.shape[1]
    tm = min(tm, m)
    assert m % tm == 0
    dims = NT if trans_b else (((1,), (0,)), ((), ()))

    def body(a_ref, b_ref, o_ref):
        o_ref[...] = lax.dot_general(a_ref[...], b_ref[...], dims, preferred_element_type=f32).astype(o_ref.dtype)

    return pl.pallas_call(
        body, grid=(m // tm,),
        in_specs=[pl.BlockSpec((tm, k), lambda i: (i, 0)), pl.BlockSpec(b.shape, lambda i: (0, 0))],
        out_specs=pl.BlockSpec((tm, n), lambda i: (i, 0)),
        out_shape=jax.ShapeDtypeStruct((m, n), out_dtype), name=name, compiler_params=_params(("parallel",)))(a, b)


def _mm2(name, a1, b1, a2, b2, out_dtype=f32, tm=ROW_TILE // 2):
    m, k = a1.shape
    n = b1.shape[1]
    tm = min(tm, m)
    assert m % tm == 0

    def body(a1_ref, b1_ref, a2_ref, b2_ref, o_ref):
        acc = jnp.dot(a1_ref[...], b1_ref[...], preferred_element_type=f32)
        acc += jnp.dot(a2_ref[...], b2_ref[...], preferred_element_type=f32)
        o_ref[...] = acc.astype(o_ref.dtype)

    row = pl.BlockSpec((tm, k), lambda i: (i, 0))
    whole = pl.BlockSpec(b1.shape, lambda i: (0, 0))
    return pl.pallas_call(
        body, grid=(m // tm,), in_specs=[row, whole, row, whole], out_specs=pl.BlockSpec((tm, n), lambda i: (i, 0)),
        out_shape=jax.ShapeDtypeStruct((m, n), out_dtype), name=name,
        compiler_params=_params(("parallel",)))(a1, b1, a2, b2)


def _mm_tn(name, a, g, tk=512, tl=ROW_TILE):
    rows, k1 = a.shape
    n = g.shape[1]
    tl = min(tl, rows)
    assert rows % tl == 0 and k1 % tk == 0

    def body(a_ref, g_ref, o_ref):
        @pl.when(pl.program_id(1) == 0)
        def _():
            o_ref[...] = jnp.zeros(o_ref.shape, f32)

        o_ref[...] += lax.dot_general(a_ref[...], g_ref[...], TN, preferred_element_type=f32)

    return pl.pallas_call(
        body, grid=(k1 // tk, rows // tl),
        in_specs=[pl.BlockSpec((tl, tk), lambda k, l: (l, k)), pl.BlockSpec((tl, n), lambda k, l: (l, 0))],
        out_specs=pl.BlockSpec((tk, n), lambda k, l: (k, 0)),
        out_shape=jax.ShapeDtypeStruct((k1, n), f32), name=name,
        compiler_params=_params(("parallel", "arbitrary")))(a, g)


def _ln_stats(r):
    mean = jnp.mean(r, axis=-1, keepdims=True)
    c = r - mean
    rstd = lax.rsqrt(jnp.mean(c * c, axis=-1, keepdims=True) + LN_EPS)
    return c * rstd, rstd


def _ln_fwd(h, mix, gain, bias):
    def fn(i, h, mix, g, b):
        r = ALPHA * h + mix
        y = _ln_stats(r)[0] * g + b
        return (r, y, y), ()

    return _ew("ln_fwd", fn, [h, mix], [gain, bias], [(D, f32), (D, f32), (D, bf16)])


def _ln_bwd(d_a, d_b, r, gain):
    def core(dout, r, g):
        xhat, rstd = _ln_stats(r)
        dxh = dout * g
        dr = rstd * (dxh - jnp.mean(dxh, axis=-1, keepdims=True) - xhat * jnp.mean(dxh * xhat, axis=-1, keepdims=True))
        return (dr, dr), (jnp.sum(dout * xhat, axis=0, keepdims=True), jnp.sum(dout, axis=0, keepdims=True))

    outs, accs = [(D, f32), (D, bf16)], [(1, D), (1, D)]
    if d_a is None:
        return _ew("ln_bwd_top", lambda i, d, r, g: core(d, r, g), [d_b, r], [gain], outs, accs)
    return _ew("ln_bwd", lambda i, da, db, r, g: core(ALPHA * da + db, r, g), [d_a, d_b, r], [gain], outs, accs)


def _sigmoid(x):
    return 1.0 / (1.0 + jnp.exp(-x))


def _ffn_up(h_bf, w_gate_t, w_up_t, tm=FFN_TILE):
    m, k = h_bf.shape
    n = w_gate_t.shape[0]

    def body(h_ref, wg_ref, wu_ref, g_ref, u_ref, a_ref):
        h = h_ref[...]
        g = lax.dot_general(h, wg_ref[...], NT, preferred_element_type=f32).astype(bf16)
        u = lax.dot_general(h, wu_ref[...], NT, preferred_element_type=f32).astype(bf16)
        g_ref[...] = g
        u_ref[...] = u
        g = g.astype(f32)
        a_ref[...] = (g * _sigmoid(g) * u.astype(f32)).astype(bf16)

    row = pl.BlockSpec((tm, n), lambda i: (i, 0))
    whole = pl.BlockSpec((n, k), lambda i: (0, 0))
    return pl.pallas_call(
        body, grid=(m // tm,), in_specs=[pl.BlockSpec((tm, k), lambda i: (i, 0)), whole, whole],
        out_specs=[row, row, row], out_shape=[jax.ShapeDtypeStruct((m, n), bf16)] * 3, name="ffn_up",
        compiler_params=_params(("parallel",)))(h_bf, w_gate_t, w_up_t)


def _ffn_dup(df_bf, w_down, gate, up, tm=FFN_TILE):
    m, k = df_bf.shape
    n = w_down.shape[0]

    def body(d_ref, w_ref, g_ref, u_ref, dg_ref, du_ref):
        da = lax.dot_general(d_ref[...], w_ref[...], NT, preferred_element_type=f32).astype(bf16).astype(f32)
        g, u = g_ref[...].astype(f32), u_ref[...].astype(f32)
        s = _sigmoid(g)
        dg_ref[...] = (da * u * s * (1.0 + g * (1.0 - s))).astype(bf16)
        du_ref[...] = (da * g * s).astype(bf16)

    row = pl.BlockSpec((tm, n), lambda i: (i, 0))
    return pl.pallas_call(
        body, grid=(m // tm,),
        in_specs=[pl.BlockSpec((tm, k), lambda i: (i, 0)), pl.BlockSpec((n, k), lambda i: (0, 0)), row, row],
        out_specs=[row, row], out_shape=[jax.ShapeDtypeStruct((m, n), bf16)] * 2, name="ffn_dup",
        compiler_params=_params(("parallel",)))(df_bf, w_down, gate, up)


def _loss_grad(y, target):
    def fn(i, y, t):
        e = y - t
        return (e * (1.0 / D),), (jnp.sum(e * e, axis=0, keepdims=True),)

    return _ew("loss", fn, [y, target], [], [(D, f32)], [(1, D)], tile=512)


def _adamw(w, g, m, v):
    def fn(i, w, g, m, v):
        m = ADAM_B1 * m + (1.0 - ADAM_B1) * g
        v = ADAM_B2 * v + (1.0 - ADAM_B2) * jnp.square(g)
        m_hat = m / (1.0 - ADAM_B1 ** ADAM_STEP)
        v_hat = v / (1.0 - ADAM_B2 ** ADAM_STEP)
        delta = -ADAM_LR * (m_hat / (jnp.sqrt(v_hat) + ADAM_EPS) + ADAM_WD * w)
        return (delta, m, v), ()

    rows, cols = w.shape
    tile = max(t for t in range(8, min(rows, 544) + 1, 8) if rows % t == 0)
    return _ew("adamw", fn, [w, g, m, v], [], [(cols, f32)] * 3, tile=tile)


def _s5_mats(lam_re, lam_im, log_dt, b_re, b_im, c_re, c_im):
    steps = jnp.arange(CHUNK + 1, dtype=f32)
    n = CHUNK * GCH
    last = n - GCH

    def one(lr, li, ldt, br, bi, cr, ci, reverse):
        dt = jnp.exp(ldt)[:, None]
        mag = jnp.exp(lr * dt)
        abr, abi = mag * jnp.cos(li * dt), mag * jnp.sin(li * dt)
        nr, ni = abr - 1.0, abi
        den = lr * lr + li * li
        zr, zi = (nr * lr + ni * li) / den, (ni * lr - nr * li) / den
        bbr = zr[..., None] * br - zi[..., None] * bi
        bbi = zr[..., None] * bi + zi[..., None] * br
        pmag = jnp.exp((lr * dt)[..., None] * steps)
        pang = (li * dt)[..., None] * steps
        pr, pi = pmag * jnp.cos(pang), pmag * jnp.sin(pang)
        crt, cit = jnp.swapaxes(cr, 1, 2)[:, :, None, :], jnp.swapaxes(ci, 1, 2)[:, :, None, :]
        car = crt * pr[..., None] - cit * pi[..., None]
        cai = crt * pi[..., None] + cit * pr[..., None]
        if reverse:
            taps = slice(CHUNK - 1, None, -1)
            outs = slice(CHUNK, 0, -1)
            ins = slice(0, CHUNK)
        else:
            taps, outs, ins = slice(0, CHUNK), slice(1, CHUNK + 1), slice(CHUNK - 1, None, -1)
        kern = (jnp.einsum('gpi,gpq->giq', bbr, car[:, :, taps].reshape(GROUPS, NSTATE, n), precision=HI)
                - jnp.einsum('gpi,gpq->giq', bbi, cai[:, :, taps].reshape(GROUPS, NSTATE, n), precision=HI))
        wide = jnp.pad(kern, ((0, 0), (0, 0), (0, last) if reverse else (last, 0)))
        m = jnp.stack([wide[:, :, last - GCH * t:last - GCH * t + n] for t in range(CHUNK)], axis=1)
        qr = jnp.swapaxes(pr[:, :, ins], 1, 2)[:, :, None, :]
        qi = jnp.swapaxes(pi[:, :, ins], 1, 2)[:, :, None, :]
        bbrt, bbit = jnp.swapaxes(bbr, 1, 2)[:, None], jnp.swapaxes(bbi, 1, 2)[:, None]
        pin = jnp.concatenate([qr * bbrt - qi * bbit, qr * bbit + qi * bbrt], axis=-1)
        pout = jnp.concatenate([car[:, :, outs].reshape(GROUPS, NSTATE, n),
                                -cai[:, :, outs].reshape(GROUPS, NSTATE, n)], axis=1)
        return (m.reshape(GROUPS, n, n), pin.reshape(GROUPS, n, 2 * NSTATE), pout, pr[:, :, CHUNK], pi[:, :, CHUNK])

    mf, pinf, poutf, arf, aif = one(lam_re[0], lam_im[0], log_dt[0], b_re[0], b_im[0], c_re[0], c_im[0], False)
    mr, pinr, poutr, arr, air = one(lam_re[1], lam_im[1], log_dt[1], b_re[1], b_im[1], c_re[1], c_im[1], True)
    return (mf + mr, jnp.concatenate([pinf, pinr], 2), jnp.concatenate([poutf, poutr], 1),
            jnp.stack([arf, arr]), jnp.stack([aif, air]))


def _s5_coefs(a_re, a_im):
    c1 = jnp.concatenate([a_re, a_re], -1)
    c2 = jnp.concatenate([-a_im, a_im], -1)
    return tuple(c.reshape(GROUPS // GB, 1, GB * 2 * NSTATE) for c in (c1[0], c2[0], c1[1], c2[1]))


def _swap(s):
    w = s.shape[1]
    lane = lax.broadcasted_iota(jnp.int32, s.shape, 1)
    return jnp.where(lane % (2 * NSTATE) < NSTATE, pltpu.roll(s, w - NSTATE, 1), pltpu.roll(s, NSTATE, 1))


def _group_lanes(g):
    return slice(g * 2 * NSTATE, (g + 1) * 2 * NSTATE)


def _s5_states(nc, u_ref, pin_ref, coef, vf, vr, wf, wr, sf, sr):
    c1f, c2f, c1r, c2r = coef
    for g in range(GB):
        v = jnp.dot(u_ref[g], pin_ref[g], preferred_element_type=f32)
        vf[:, _group_lanes(g)] = v[:, :2 * NSTATE]
        vr[:, _group_lanes(g)] = v[:, 2 * NSTATE:]
    wf[...] = _swap(vf[...])
    wr[...] = _swap(vr[...])

    def step(i, carry):
        s_f, t_f, s_r, t_r = carry
        kf, kr = pl.ds(i, 1), pl.ds(nc - 1 - i, 1)
        sf[kf, :] = s_f
        sr[kr, :] = s_r
        s_f, t_f = c1f * s_f + c2f * t_f + vf[kf, :], c1f * t_f - c2f * s_f + wf[kf, :]
        s_r, t_r = c1r * s_r + c2r * t_r + vr[kr, :], c1r * t_r - c2r * s_r + wr[kr, :]
        return s_f, t_f, s_r, t_r

    z = jnp.zeros((1, GB * 2 * NSTATE), f32)
    lax.fori_loop(0, nc, step, (z, z, z, z))


def _s5_core_fwd(ug, msum, pin, pout, coefs):
    nc = ug.shape[1]
    n = CHUNK * GCH

    def body(u_ref, m_ref, pin_ref, pout_ref, c1f, c2f, c1r, c2r, y_ref, vf, vr, wf, wr, sf, sr):
        coef = (c1f[...], c2f[...], c1r[...], c2r[...])
        _s5_states(nc, u_ref, pin_ref, coef, vf, vr, wf, wr, sf, sr)
        for g in range(GB):
            s_in = jnp.concatenate([sf[:, _group_lanes(g)], sr[:, _group_lanes(g)]], axis=1).astype(bf16)
            y_ref[g] = (jnp.dot(u_ref[g], m_ref[g], preferred_element_type=f32)
                        + jnp.dot(s_in, pout_ref[g], preferred_element_type=f32))

    seq = pl.BlockSpec((GB, nc, n), lambda i: (i, 0, 0))
    mat = pl.BlockSpec((GB, n, n), lambda i: (i, 0, 0))
    cf = pl.BlockSpec((None, 1, GB * 2 * NSTATE), lambda i: (i, 0, 0))
    scr = pltpu.VMEM((nc, GB * 2 * NSTATE), f32)
    return pl.pallas_call(
        body, grid=(GROUPS // GB,), in_specs=[seq, mat, mat, mat, cf, cf, cf, cf], out_specs=seq,
        out_shape=jax.ShapeDtypeStruct((GROUPS, nc, n), f32), scratch_shapes=[scr] * 6,
        name="s5_core_fwd", compiler_params=_params(("parallel",)))(ug, msum, pin, pout, *coefs)


def _s5_core_bwd(ug, dyg, msum, pin, pout, coefs):
    nc = ug.shape[1]
    n = CHUNK * GCH

    def body(u_ref, dy_ref, m_ref, pin_ref, pout_ref, c1f, c2f, c1r, c2r,
             du_ref, dm_ref, dpin_ref, dpout_ref, a1f_ref, a2f_ref, a1r_ref, a2r_ref, vf, vr, wf, wr, sf, sr):
        coef = (c1f[...], c2f[...], c1r[...], c2r[...])
        _s5_states(nc, u_ref, pin_ref, coef, vf, vr, wf, wr, sf, sr)
        for g in range(GB):
            s_in = jnp.concatenate([sf[:, _group_lanes(g)], sr[:, _group_lanes(g)]], axis=1).astype(bf16)
            dy = dy_ref[g]
            ds = lax.dot_general(dy, pout_ref[g], NT, preferred_element_type=f32)
            vf[:, _group_lanes(g)] = ds[:, :2 * NSTATE]
            vr[:, _group_lanes(g)] = ds[:, 2 * NSTATE:]
            dpout_ref[g] = lax.dot_general(s_in, dy, TN, preferred_element_type=f32)
            dm_ref[g] = lax.dot_general(u_ref[g], dy, TN, preferred_element_type=f32)

        wf[...] = _swap(vf[...])
        wr[...] = _swap(vr[...])
        k1f, k2f, k1r, k2r = coef[0], -coef[1], coef[2], -coef[3]

        def step(i, carry):
            g_f, h_f, g_r, h_r, a1f, b2f, a1r, b2r = carry
            kf, kr = pl.ds(nc - 1 - i, 1), pl.ds(i, 1)
            s_f, s_r = sf[kf, :], sr[kr, :]
            sf[kf, :] = g_f
            sr[kr, :] = g_r
            a1f, b2f = a1f + g_f * s_f, b2f + h_f * s_f
            a1r, b2r = a1r + g_r * s_r, b2r + h_r * s_r
            g_f, h_f = vf[kf, :] + k1f * g_f + k2f * h_f, wf[kf, :] + k1f * h_f - k2f * g_f
            g_r, h_r = vr[kr, :] + k1r * g_r + k2r * h_r, wr[kr, :] + k1r * h_r - k2r * g_r
            return g_f, h_f, g_r, h_r, a1f, b2f, a1r, b2r

        z = jnp.zeros((1, GB * 2 * NSTATE), f32)
        _, _, _, _, a1f, b2f, a1r, b2r = lax.fori_loop(0, nc, step, (z,) * 8)
        a1f_ref[...], a2f_ref[...], a1r_ref[...], a2r_ref[...] = a1f, _swap(b2f), a1r, _swap(b2r)
        for g in range(GB):
            dv = jnp.concatenate([sf[:, _group_lanes(g)], sr[:, _group_lanes(g)]], axis=1).astype(bf16)
            du_ref[g] = (lax.dot_general(dy_ref[g], m_ref[g], NT, preferred_element_type=f32)
                         + lax.dot_general(dv, pin_ref[g], NT, preferred_element_type=f32))
            dpin_ref[g] = lax.dot_general(u_ref[g], dv, TN, preferred_element_type=f32)

    seq = pl.BlockSpec((GB, nc, n), lambda i: (i, 0, 0))
    mat = pl.BlockSpec((GB, n, n), lambda i: (i, 0, 0))
    cf = pl.BlockSpec((None, 1, GB * 2 * NSTATE), lambda i: (i, 0, 0))
    scr = pltpu.VMEM((nc, GB * 2 * NSTATE), f32)
    mat_s = jax.ShapeDtypeStruct((GROUPS, n, n), f32)
    cf_s = jax.ShapeDtypeStruct((GROUPS // GB, 1, GB * 2 * NSTATE), f32)
    return pl.pallas_call(
        body, grid=(GROUPS // GB,), in_specs=[seq, seq, mat, mat, mat, cf, cf, cf, cf],
        out_specs=[seq, mat, mat, mat, cf, cf, cf, cf],
        out_shape=[jax.ShapeDtypeStruct((GROUPS, nc, n), f32), mat_s, mat_s, mat_s, cf_s, cf_s, cf_s, cf_s],
        scratch_shapes=[scr] * 6, name="s5_core_bwd",
        compiler_params=_params(("parallel",)))(ug, dyg, msum, pin, pout, *coefs)


def _to_groups(a):
    n = a.shape[0]
    return a.reshape(n // CHUNK, CHUNK, GROUPS, GCH).transpose(2, 0, 1, 3).reshape(GROUPS, n // CHUNK, CHUNK * GCH)


def _from_groups(g):
    nc = g.shape[1]
    return g.reshape(GROUPS, nc, CHUNK, GCH).transpose(1, 2, 0, 3).reshape(nc * CHUNK, D)


def _gelu(y):
    return 0.5 * y * (1.0 + lax.erf(y * (2.0 ** -0.5)))


def _gelu_grad(y):
    return 0.5 * (1.0 + lax.erf(y * (2.0 ** -0.5))) + y * jnp.exp(-0.5 * y * y) * (1.0 / math.sqrt(2.0 * math.pi))


def _s5_fwd(h, h_bf, valid, mats, d_skip, w_glu, w_out):
    msum, pin, pout, a_re, a_im = mats
    coefs = _s5_coefs(a_re, a_im)
    ug = _to_groups(jnp.where(valid, h_bf, jnp.zeros_like(h_bf)))
    ys = _from_groups(_s5_core_fwd(ug, msum.astype(bf16), pin.astype(bf16), pout.astype(bf16), coefs))

    def post(i, ys, h, d):
        y = ys + d * h
        return (y, _gelu(y)), ()

    y, g_bf = _ew("s5_gelu", post, [ys, h], [d_skip], [(D, f32), (D, bf16)])
    gw = _mm("s5_glu_mm", g_bf, w_glu)

    def glu(i, y, gw):
        return (_gelu(y) * _sigmoid(gw),), ()

    z_bf = _ew("s5_glu", glu, [y, gw], [], [(D, bf16)])[0]
    mix = _mm("s5_out_mm", z_bf, w_out)
    return mix, (ug, y, g_bf, gw, z_bf)


def _s5_bwd(dmix_bf, h, valid, saved, mats, vjp_mats, d_skip, w_glu, w_out):
    ug, y, g_bf, gw, z_bf = saved
    msum, pin, pout, a_re, a_im = mats
    coefs = _s5_coefs(a_re, a_im)
    dz = _mm("s5_dz_mm", dmix_bf, w_out, trans_b=True)
    d_w_out = _mm_tn("s5_dwout", z_bf, dmix_bf)

    def dglu(i, dz, y, gw):
        g, s = _gelu(y), _sigmoid(gw)
        return (dz * g * s * (1.0 - s), dz * s), ()

    dgw_bf, dg1 = _ew("s5_dglu", dglu, [dz, y, gw], [], [(D, bf16), (D, f32)])
    d_w_glu = _mm_tn("s5_dwglu", g_bf, dgw_bf)
    dg2 = _mm("s5_dg_mm", dgw_bf, w_glu, trans_b=True)

    def dgelu(i, dg1, dg2, y, h, d):
        dy = (dg1 + dg2) * _gelu_grad(y)
        return (dy, dy * d), (jnp.sum(dy * h, axis=0, keepdims=True),)

    dy_bf, dh_skip, dd = _ew("s5_dgelu", dgelu, [dg1, dg2, y, h], [d_skip], [(D, bf16), (D, f32)], [(1, D)])
    dug, dm, dpin, dpout, a1f, a2f, a1r, a2r = _s5_core_bwd(
        ug, _to_groups(dy_bf), msum.astype(bf16), pin.astype(bf16), pout.astype(bf16), coefs)
    du = _from_groups(dug)
    dh = dh_skip + jnp.where(valid, du, jnp.zeros_like(du))
    a1 = jnp.stack([a1f, a1r]).reshape(2, GROUPS, 2 * NSTATE)
    a2 = jnp.stack([a2f, a2r]).reshape(2, GROUPS, 2 * NSTATE)
    da_re = a1[..., :NSTATE] + a1[..., NSTATE:]
    da_im = a2[..., NSTATE:] - a2[..., :NSTATE]
    d_params = vjp_mats((dm, dpin, dpout, da_re, da_im))
    return dh, d_params, dd[0], d_w_glu, d_w_out


def _rope_tables(n):
    row = jnp.arange(n, dtype=jnp.int32) - OFF
    real = row >= 0
    rid = jnp.where(real, row // GRID_W, 0).astype(f32)
    cid = jnp.where(real, row % GRID_W, 0).astype(f32)
    half = HD // 2
    inv = ROPE_THETA ** (-jnp.arange(0, half, 2, dtype=f32) / half)
    ar, ac = rid[:, None] * inv[None, :], cid[:, None] * inv[None, :]
    cos = jnp.concatenate([jnp.cos(ar), jnp.cos(ar), jnp.cos(ac), jnp.cos(ac)], axis=1)
    sin = jnp.concatenate([-jnp.sin(ar), jnp.sin(ar), -jnp.sin(ac), jnp.sin(ac)], axis=1)
    return jnp.tile(cos, (1, 2)), jnp.tile(sin, (1, 2))


def _head_mats():
    head = jnp.arange(QW, dtype=jnp.int32)[:, None] // HD == jnp.arange(128, dtype=jnp.int32)[None, :]
    return head.astype(f32) * (1.0 / HD), head.astype(f32).T


def _rot(v):
    w = v.shape[1]
    lane = lax.broadcasted_iota(jnp.int32, v.shape, 1)
    return jnp.where(lane % 32 < 16, pltpu.roll(v, w - 16, 1), pltpu.roll(v, 16, 1))


def _head_mean(v, e, et):
    w = v.shape[1]
    m = jnp.dot(v, e[:w], preferred_element_type=f32, precision=HI)
    return m, et[:, :w]


def _rms_rope(t, gain, e, et, cos, sin):
    w = t.shape[1]
    ms, spread = _head_mean(t * t, e, et)
    rs = jnp.dot(lax.rsqrt(ms + QK_EPS), spread, preferred_element_type=f32, precision=HI)
    n0 = t * rs
    n = n0 * gain
    reps = w // 128
    return n * jnp.tile(cos, (1, reps)) + _rot(n) * jnp.tile(sin, (1, reps))


def _rms_rope_bwd(dout, t, gain, e, et, cos, sin):
    w = t.shape[1]
    reps = w // 128
    ms, spread = _head_mean(t * t, e, et)
    rs = jnp.dot(lax.rsqrt(ms + QK_EPS), spread, preferred_element_type=f32, precision=HI)
    n0 = t * rs
    dn = dout * jnp.tile(cos, (1, reps)) + _rot(dout * jnp.tile(sin, (1, reps)))
    dn0 = dn * gain
    mm, _ = _head_mean(dn0 * n0, e, et)
    corr = jnp.dot(mm, spread, preferred_element_type=f32, precision=HI)
    return rs * (dn0 - n0 * corr), jnp.sum(dn * n0, axis=0, keepdims=True)


def _qk_fwd(qkv, qg, kg, e, et, cos, sin):
    def fn(i, qkv, cos, sin, qg, kg, e, et):
        q = _rms_rope(qkv[:, :QW], qg, e, et, cos, sin) * Q_SCALE
        k = _rms_rope(qkv[:, QW:QW + KW], kg, e, et, cos, sin)
        return (q, k, qkv[:, QW + KW:]), ()

    return _ew("qk_rope", fn, [qkv, cos, sin], [qg, kg, e, et], [(QW, bf16), (KW, bf16), (KW, bf16)])


def _qk_bwd(qkv, dq, dk, dv, qg, kg, e, et, cos, sin):
    def fn(i, qkv, cos, sin, dq, dk, dv, qg, kg, e, et):
        dtq, dgq = _rms_rope_bwd(dq * (HD ** -0.5), qkv[:, :QW], qg, e, et, cos, sin)
        dtk, dgk = _rms_rope_bwd(dk * math.log(2.0), qkv[:, QW:QW + KW], kg, e, et, cos, sin)
        return (jnp.concatenate([dtq, dtk, dv], axis=1),), (dgq, dgk)

    return _ew("qk_rope_bwd", fn, [qkv, cos, sin, dq, dk, dv], [qg, kg, e, et], [(QKVW, bf16)], [(1, QW), (1, KW)])


def _to_heads(a, nh):
    return a.reshape(a.shape[0], nh, HD).transpose(1, 0, 2)


def _from_heads(a):
    return a.transpose(1, 0, 2).reshape(a.shape[1], a.shape[0] * HD)


def _masked_first(s, c):
    if c:
        return s
    col = lax.broadcasted_iota(jnp.int32, (1, s.shape[1]), 1)
    return jnp.where(col >= PAD, s, NEG)


def _flash_fwd(q, k, v1, tq=ROW_TILE, tc=KEY_CHUNK):
    n = q.shape[1]
    nc = n // tc

    def body(q_ref, k_ref, v_ref, o_ref, lse_ref):
        qb = q_ref[0]

        def scores(c):
            ks = k_ref[0, pl.ds(c * tc, tc), :]
            return _masked_first(lax.dot_general(qb, ks, NT, preferred_element_type=f32), c)

        m = jnp.full((tq, 1), NEG, f32)
        acc = jnp.zeros((tq, 2 * HD), f32)
        s_next = scores(0)
        for c in range(nc):
            s = s_next
            if c + 1 < nc:
                s_next = scores(c + 1)
            m_new = jnp.maximum(m, jnp.max(s, axis=1, keepdims=True))
            p = jnp.exp2(s - m_new)
            acc = jnp.exp2(m - m_new) * acc + jnp.dot(p.astype(bf16), v_ref[0, pl.ds(c * tc, tc), :],
                                                      preferred_element_type=f32)
            m = m_new
        l = acc[:, HD:HD + 1]
        o_ref[0] = acc[:, :HD] / l
        lse_ref[0] = m + jnp.log2(l)

    return pl.pallas_call(
        body, grid=(NQ, n // tq),
        in_specs=[pl.BlockSpec((1, tq, HD), lambda h, i: (h, i, 0)),
                  pl.BlockSpec((1, n, HD), lambda h, i: (h // (NQ // NKV), 0, 0)),
                  pl.BlockSpec((1, n, 2 * HD), lambda h, i: (h // (NQ // NKV), 0, 0))],
        out_specs=[pl.BlockSpec((1, tq, HD), lambda h, i: (h, i, 0)), pl.BlockSpec((1, tq, 1), lambda h, i: (h, i, 0))],
        out_shape=[jax.ShapeDtypeStruct((NQ, n, HD), f32), jax.ShapeDtypeStruct((NQ, n, 1), f32)],
        name="flash_fwd", compiler_params=_params(("parallel", "parallel")))(q, k, v1)


def _flash_bwd(q, k, kt, v, do, lse_row, delta_row, tk=ROW_TILE, tc=KEY_CHUNK):
    n = q.shape[1]
    nc = n // tc
    grp = NQ // NKV

    def body(q_ref, do_ref, lse_ref, delta_ref, k_ref, kt_ref, v_ref, dqt_ref, dk_ref, dv_ref):
        j, g = pl.program_id(1), pl.program_id(2)
        kb, vb, ktb = k_ref[0], v_ref[0], kt_ref[0]
        valid = lax.broadcasted_iota(jnp.int32, (tk, 1), 0) + j * tk >= PAD

        @pl.when(j == 0)
        def _():
            dqt_ref[g] = jnp.zeros((HD, n), f32)

        def products(c):
            rows = pl.ds(c * tc, tc)
            return (lax.dot_general(kb, q_ref[0, rows, :], NT, preferred_element_type=f32),
                    lax.dot_general(vb, do_ref[0, rows, :], NT, preferred_element_type=f32))

        dk = jnp.zeros((tk, HD), f32)
        dv = jnp.zeros((tk, HD), f32)
        nxt = products(0)
        for c in range(nc):
            st, dpt = nxt
            if c + 1 < nc:
                nxt = products(c + 1)
            rows = pl.ds(c * tc, tc)
            pt = jnp.exp2(jnp.where(valid, st, NEG) - lse_ref[0, :, rows])
            dv = dv + jnp.dot(pt.astype(bf16), do_ref[0, rows, :], preferred_element_type=f32)
            dst = (pt * (dpt - delta_ref[0, :, rows])).astype(bf16)
            dk = dk + jnp.dot(dst, q_ref[0, rows, :], preferred_element_type=f32)
            dqt_ref[g, :, rows] += jnp.dot(ktb, dst, preferred_element_type=f32)

        @pl.when(g == 0)
        def _():
            dk_ref[0] = dk
            dv_ref[0] = dv

        @pl.when(g > 0)
        def _():
            dk_ref[0] += dk
            dv_ref[0] += dv

    hspec = pl.BlockSpec((1, n, HD), lambda h, j, g: (h * grp + g, 0, 0))
    rspec = pl.BlockSpec((1, 1, n), lambda h, j, g: (h * grp + g, 0, 0))
    kspec = pl.BlockSpec((1, tk, HD), lambda h, j, g: (h, j, 0))
    return pl.pallas_call(
        body, grid=(NKV, n // tk, grp),
        in_specs=[hspec, hspec, rspec, rspec, kspec, pl.BlockSpec((1, HD, tk), lambda h, j, g: (h, 0, j)), kspec],
        out_specs=[pl.BlockSpec((grp, HD, n), lambda h, j, g: (h, 0, 0)), kspec, kspec],
        out_shape=[jax.ShapeDtypeStruct((NQ, HD, n), f32)] + [jax.ShapeDtypeStruct((NKV, n, HD), f32)] * 2,
        name="flash_bwd", compiler_params=_params(("parallel", "arbitrary", "arbitrary")))(
            q, do, lse_row, delta_row, k, kt, v)


def _attn_fwd(h_bf, w_qkv_t, qg, kg, w_out, tabs):
    e, et, cos, sin = tabs
    qkv = _mm("attn_qkv_mm", h_bf, w_qkv_t, trans_b=True)
    q_bf, k_bf, v_bf = _qk_fwd(qkv, qg, kg, e, et, cos, sin)
    q16, k4, v4 = _to_heads(q_bf, NQ), _to_heads(k_bf, NKV), _to_heads(v_bf, NKV)
    ones = jnp.zeros((NKV, v4.shape[1], HD), bf16).at[:, :, 0].set(1.0)
    o16, lse = _flash_fwd(q16, k4, jnp.concatenate([v4, ones], axis=2))
    o = _from_heads(o16)
    o_bf = o.astype(bf16)
    mix = _mm("attn_out_mm", o_bf, w_out)
    return mix, (qkv, q16, k4, v4, o, lse, o_bf)


def _attn_bwd(dmix_bf, h_bf, saved, w_qkv_t, qg, kg, w_out, tabs):
    e, et, cos, sin = tabs
    qkv, q16, k4, v4, o, lse, o_bf = saved
    n = qkv.shape[0]
    do = _mm("attn_do_mm", dmix_bf, w_out, trans_b=True, out_dtype=bf16)
    d_w_out = _mm_tn("attn_dwout", o_bf, dmix_bf)

    def head_dots(i, do, o, e):
        return (jnp.dot(do.astype(f32) * o, e, preferred_element_type=f32, precision=HI) * HD,), ()

    delta = _ew("attn_delta", head_dots, [do, o], [e], [(128, f32)])[0]
    dqt, dk4, dv4 = _flash_bwd(q16, k4, k4.transpose(0, 2, 1), v4, _to_heads(do, NQ), lse.reshape(NQ, 1, n),
                               delta[:, :NQ].T.reshape(NQ, 1, n))
    dq = dqt.transpose(2, 0, 1).reshape(n, QW)
    dqkv_bf, dgq, dgk = _qk_bwd(qkv, dq, _from_heads(dk4), _from_heads(dv4), qg, kg, e, et, cos, sin)
    d_w_qkv_t = _mm_tn("attn_dwqkv", dqkv_bf, h_bf)
    dh = _mm("attn_dh_mm", dqkv_bf, w_qkv_t)
    return dh, d_w_qkv_t, dgq.reshape(NQ, HD).sum(0), dgk.reshape(NKV, HD).sum(0), d_w_out


def _all_gather(name, shard):
    def body(x_ref, out_ref, send_sems, recv_sems, local_sem):
        x, y, c = lax.axis_index("x"), lax.axis_index("y"), lax.axis_index("c")
        me, sibling = (x, y, c), (x, y, 1 - c)
        chips = [(1 - x, y), (x, 1 - y), (1 - x, 1 - y)]

        def slot(px, py, pc):
            return out_ref.at[4 * px + 2 * py + pc]

        def copy(k, block, to, src=None):
            return pltpu.make_async_remote_copy(
                src_ref=slot(*block) if src is None else src, dst_ref=slot(*block),
                send_sem=send_sems.at[k], recv_sem=recv_sems.at[k], device_id=to, device_id_type=MESH)

        mine = pltpu.make_async_copy(x_ref, slot(*me), local_sem)
        mine.start()
        first = [copy(0, me, sibling, src=x_ref)]
        first += [copy(1 + j, me, (*chip, c), src=x_ref) for j, chip in enumerate(chips)]
        for cp in first:
            cp.start()
        passed = [copy(4 + j, (*chip, c), sibling) for j, chip in enumerate(chips)]
        for j, chip in enumerate(chips):
            copy(1 + j, (*chip, c), me).wait_recv()
            passed[j].start()
        copy(0, sibling, me).wait_recv()
        for j, chip in enumerate(chips):
            copy(4 + j, (*chip, 1 - c), me).wait_recv()
        for cp in first + passed:
            cp.wait_send()
        mine.wait()

    return pl.pallas_call(
        body, out_shape=jax.ShapeDtypeStruct((8,) + shard.shape, shard.dtype), in_specs=[ANY], out_specs=ANY,
        scratch_shapes=[pltpu.SemaphoreType.DMA((7,)), pltpu.SemaphoreType.DMA((7,)), pltpu.SemaphoreType.DMA],
        name=name)(shard)


def _swap_sibling(name, theirs):
    k = len(theirs)

    def body(*refs):
        src, dst, send_sems, recv_sems = refs[:k], refs[k:2 * k], refs[2 * k], refs[2 * k + 1]
        x, y, c = lax.axis_index("x"), lax.axis_index("y"), lax.axis_index("c")
        copies = [pltpu.make_async_remote_copy(src_ref=src[j], dst_ref=dst[j], send_sem=send_sems.at[j],
                                               recv_sem=recv_sems.at[j], device_id=(x, y, 1 - c), device_id_type=MESH)
                  for j in range(k)]
        for cp in copies:
            cp.start()
        for cp in copies:
            cp.wait()

    return pl.pallas_call(
        body, out_shape=[jax.ShapeDtypeStruct(a.shape, a.dtype) for a in theirs], in_specs=[ANY] * k,
        out_specs=[ANY] * k, scratch_shapes=[pltpu.SemaphoreType.DMA((k,)), pltpu.SemaphoreType.DMA((k,))],
        name=name)(*theirs)


def _exchange_chips(name, parts):
    k = len(parts)

    def body(*refs):
        p_refs, t_refs = refs[:k], refs[k:2 * k]
        send_sems, recv_sems, local_sems = refs[2 * k:]
        x, y, c = lax.axis_index("x"), lax.axis_index("y"), lax.axis_index("c")
        q = 2 * x + y
        copies = []
        for j in range(k):
            copies.append(pltpu.make_async_copy(p_refs[j].at[q], t_refs[j].at[q], local_sems.at[j]))
            for hop in (1, 2, 3):
                tx, ty = x ^ (hop >> 1), y ^ (hop & 1)
                copies.append(pltpu.make_async_remote_copy(
                    src_ref=p_refs[j].at[2 * tx + ty], dst_ref=t_refs[j].at[q], send_sem=send_sems.at[3 * j + hop - 1],
                    recv_sem=recv_sems.at[3 * j + hop - 1], device_id=(tx, ty, c), device_id_type=MESH))
        for cp in copies:
            cp.start()
        for cp in copies:
            cp.wait()

    return pl.pallas_call(
        body, out_shape=[jax.ShapeDtypeStruct(a.shape, a.dtype) for a in parts], in_specs=[ANY] * k,
        out_specs=[ANY] * k,
        scratch_shapes=[pltpu.SemaphoreType.DMA((3 * k,)), pltpu.SemaphoreType.DMA((3 * k,)),
                        pltpu.SemaphoreType.DMA((k,))],
        name=name)(*parts)


def _reduce_scatter(mine, theirs):
    got = _swap_sibling("rs_sibling", list(theirs))
    parts = []
    for a, b, dt, nm in zip(mine, got, (bf16, f32), ("rs_add2", "rs_add2_small")):
        rows = 4 * a.shape[1]
        parts.append(_ew(nm, lambda i, a, b: ((a + b,), ()), [a.reshape(rows, D), b.reshape(rows, D)], [],
                         [(D, dt)], tile=RS_TILE)[0].reshape(a.shape))
    ts = _exchange_chips("rs_chips", parts)

    def add4(i, a, b, c, d):
        return ((((a.astype(f32) + b.astype(f32)) + c.astype(f32)) + d.astype(f32),), ())

    return [_ew(nm, add4, [(t, 0), (t, 1), (t, 2), (t, 3)], [], [(D, f32)], tile=RS_TILE)[0]
            for t, nm in zip(ts, ("rs_add4", "rs_add4_small"))]


def _pack_rows(parts, rows):
    flat = jnp.concatenate([p.reshape(-1) for p in parts])
    return jnp.pad(flat, (0, rows * D - flat.shape[0])).reshape(rows, D)


def _unpack(flat, shapes):
    out, off = [], 0
    for s in shapes:
        n = math.prod(s)
        out.append(flat[off:off + n].reshape(s))
        off += n
    return out


def _mat_rows(block, transposed, blk):
    a = jnp.swapaxes(block, 1, 2) if transposed else block
    a = jnp.pad(a, ((0, 0), (0, blk - a.shape[1]), (0, 0)))
    return a.reshape(-1, D)


def _mat_block(rows, transposed, blk, real):
    a = rows.reshape(-1, blk, D)[:, :real]
    return jnp.swapaxes(a, 1, 2) if transposed else a


def _mat_full(gathered, blk):
    layers = gathered.shape[1] // blk
    return gathered.reshape(8, layers, blk, D).transpose(1, 0, 2, 3).reshape(layers, 8 * blk, D)


def _vec_full(gathered):
    return gathered.transpose(1, 0, 2).reshape(gathered.shape[1], D)


def _grad_slots(full, small, cc):
    def halves(a, blk):
        layers = a.shape[0]
        a5 = a.reshape(layers, 4, 2, blk, D)
        res = []
        for sel in (cc, 1 - cc):
            s = lax.dynamic_index_in_dim(a5, sel, axis=2, keepdims=False)
            res.append(s.transpose(1, 0, 2, 3).reshape(4, layers * blk, D))
        return res

    mine, theirs = [], []
    for name, _, blk, _ in MATS:
        a, b = halves(full[name], blk)
        mine.append(a)
        theirs.append(b)
    a, b = halves(small[None], REP_PIECE)
    return (jnp.concatenate(mine, axis=1), a), (jnp.concatenate(theirs, axis=1), b)


def _local_step(x0, target0, w, fw):
    seq = x0.shape[0]
    n = OFF + seq
    valid = (jnp.arange(n, dtype=jnp.int32) >= PAD)[:, None]
    h = jnp.concatenate([jnp.zeros((PAD, D), f32), fw['meta_tokens'], x0], axis=0)
    h_bf = h.astype(bf16)
    tabs = _head_mats() + _rope_tables(n)
    qg = [jnp.tile(w['attn_q_gain'][j], NQ)[None, :] for j in range(2)]
    kg = [jnp.tile(w['attn_k_gain'][j], NKV)[None, :] for j in range(2)]
    s5_names = ['s5_lambda_re', 's5_lambda_im', 's5_log_dt', 's5_b_re', 's5_b_im', 's5_c_re', 's5_c_im']
    s5_mats, s5_vjp = [], []
    for j in range(2):
        mats, vjp = jax.vjp(_s5_mats, *[w[k][j] for k in s5_names])
        s5_mats.append(mats)
        s5_vjp.append(vjp)
    saved = []
    for i in range(DEPTH):
        j = i // 2
        if i % 2 == 0:
            mix, sv = _s5_fwd(h, h_bf, valid, s5_mats[j], w['s5_d'][j][None, :], fw['s5_w_glu'][j], fw['s5_w_out'][j])
        else:
            mix, sv = _attn_fwd(h_bf, fw['attn_w_qkv'][j], qg[j], kg[j], fw['attn_w_out'][j], tabs)
        r1, h1, h1_bf = _ln_fwd(h, mix, fw['ln_gain'][i, 0][None, :], fw['ln_bias'][i, 0][None, :])
        gate, up, act = _ffn_up(h1_bf, fw['ffn_w_gate'][i], fw['ffn_w_up'][i])
        f = _mm("ffn_down_mm", act, fw['ffn_w_down'][i])
        r2, h2, h2_bf = _ln_fwd(h1, f, fw['ln_gain'][i, 1][None, :], fw['ln_bias'][i, 1][None, :])
        saved.append((h, h_bf, sv, r1, h1_bf, gate, up, act, r2))
        h, h_bf = h2, h2_bf

    dy, sq = _loss_grad(h[OFF:], target0)
    loss = 0.5 * jnp.sum(sq) * (1.0 / D)

    grads = {k: [None] * (DEPTH if k.startswith('ffn') else 2) for k in WEIGHTS}
    d_ln_gain = [[None, None] for _ in range(DEPTH)]
    d_ln_bias = [[None, None] for _ in range(DEPTH)]
    d_a, d_b = None, jnp.pad(dy, ((OFF, 0), (0, 0)))
    for i in reversed(range(DEPTH)):
        j = i // 2
        h_in, h_in_bf, sv, r1, h1_bf, gate, up, act, r2 = saved[i]
        dr2, dr2_bf, dg, db = _ln_bwd(d_a, d_b, r2, fw['ln_gain'][i, 1][None, :])
        d_ln_gain[i][1], d_ln_bias[i][1] = dg[0], db[0]
        dgate, dup = _ffn_dup(dr2_bf, fw['ffn_w_down'][i], gate, up)
        grads['ffn_w_down'][i] = _mm_tn("ffn_dwdown", act, dr2_bf, tk=DFFP // 2)
        grads['ffn_w_gate'][i] = _mm_tn("ffn_dwgate", dgate, h1_bf, tk=DFFP // 2)
        grads['ffn_w_up'][i] = _mm_tn("ffn_dwup", dup, h1_bf, tk=DFFP // 2)
        dh1 = _mm2("ffn_dh_mm", dgate, fw['ffn_w_gate'][i], dup, fw['ffn_w_up'][i])
        dr1, dr1_bf, dg, db = _ln_bwd(dr2, dh1, r1, fw['ln_gain'][i, 0][None, :])
        d_ln_gain[i][0], d_ln_bias[i][0] = dg[0], db[0]
        if i % 2 == 0:
            dh, d_par, dd, d_w_glu, d_w_out = _s5_bwd(dr1_bf, h_in, valid, sv, s5_mats[j], s5_vjp[j],
                                                      w['s5_d'][j][None, :], fw['s5_w_glu'][j], fw['s5_w_out'][j])
            for k, g in zip(s5_names, d_par):
                grads[k][j] = g
            grads['s5_d'][j], grads['s5_w_glu'][j], grads['s5_w_out'][j] = dd, d_w_glu, d_w_out
        else:
            dh, d_w_qkv, dgq, dgk, d_w_out = _attn_bwd(dr1_bf, h_in_bf, sv, fw['attn_w_qkv'][j], qg[j], kg[j],
                                                       fw['attn_w_out'][j], tabs)
            grads['attn_w_qkv'][j], grads['attn_w_out'][j] = d_w_qkv, d_w_out
            grads['attn_q_gain'][j], grads['attn_k_gain'][j] = dgq, dgk
        d_a, d_b = dr1, dh
    dh0 = _ew("dh0", lambda i, a, b: ((ALPHA * a + b,), ()), [d_a, d_b], [], [(D, f32)])[0]
    full = {k: jnp.stack(v) for k, v in grads.items() if v[0] is not None}
    full['meta_tokens'] = dh0[PAD:OFF]
    full['ln_gain'] = jnp.stack([jnp.stack(r) for r in d_ln_gain])
    full['ln_bias'] = jnp.stack([jnp.stack(r) for r in d_ln_bias])

    return loss, dh0[OFF:], full


def kernel(x, meta_tokens, s5_lambda_re, s5_lambda_im, s5_log_dt, s5_b_re, s5_b_im, s5_c_re, s5_c_im, s5_d, s5_w_glu, s5_w_out, attn_w_qkv, attn_q_gain, attn_k_gain, attn_w_out, ffn_w_gate, ffn_w_up, ffn_w_down, ln_gain, ln_bias, loss_target, m_meta_tokens, m_s5_lambda_re, m_s5_lambda_im, m_s5_log_dt, m_s5_b_re, m_s5_b_im, m_s5_c_re, m_s5_c_im, m_s5_d, m_s5_w_glu, m_s5_w_out, m_attn_w_qkv, m_attn_q_gain, m_attn_k_gain, m_attn_w_out, m_ffn_w_gate, m_ffn_w_up, m_ffn_w_down, m_ln_gain, m_ln_bias, v_meta_tokens, v_s5_lambda_re, v_s5_lambda_im, v_s5_log_dt, v_s5_b_re, v_s5_b_im, v_s5_c_re, v_s5_c_im, v_s5_d, v_s5_w_glu, v_s5_w_out, v_attn_w_qkv, v_attn_q_gain, v_attn_k_gain, v_attn_w_out, v_ffn_w_gate, v_ffn_w_up, v_ffn_w_down, v_ln_gain, v_ln_bias):
    w = dict(zip(WEIGHTS, (meta_tokens, s5_lambda_re, s5_lambda_im, s5_log_dt, s5_b_re, s5_b_im, s5_c_re, s5_c_im, s5_d, s5_w_glu, s5_w_out, attn_w_qkv, attn_q_gain, attn_k_gain, attn_w_out, ffn_w_gate, ffn_w_up, ffn_w_down, ln_gain, ln_bias)))
    mom = dict(zip(WEIGHTS, (m_meta_tokens, m_s5_lambda_re, m_s5_lambda_im, m_s5_log_dt, m_s5_b_re, m_s5_b_im, m_s5_c_re, m_s5_c_im, m_s5_d, m_s5_w_glu, m_s5_w_out, m_attn_w_qkv, m_attn_q_gain, m_attn_k_gain, m_attn_w_out, m_ffn_w_gate, m_ffn_w_up, m_ffn_w_down, m_ln_gain, m_ln_bias)))
    vel = dict(zip(WEIGHTS, (v_meta_tokens, v_s5_lambda_re, v_s5_lambda_im, v_s5_log_dt, v_s5_b_re, v_s5_b_im, v_s5_c_re, v_s5_c_im, v_s5_d, v_s5_w_glu, v_s5_w_out, v_attn_w_qkv, v_attn_q_gain, v_attn_k_gain, v_attn_w_out, v_ffn_w_gate, v_ffn_w_up, v_ffn_w_down, v_ln_gain, v_ln_bias)))
    cc = lax.axis_index("c")
    dev = 4 * lax.axis_index("x") + 2 * lax.axis_index("y") + cc

    mat_rows = jnp.concatenate([_mat_rows(w[n], t, blk) for n, t, blk, _ in MATS]).astype(bf16)
    g_mats = _all_gather("ag_weights", mat_rows)
    g_vecs = _all_gather("ag_vectors", jnp.concatenate([w[n].reshape(-1, 128) for n in VECS]))
    fw, off = {}, 0
    for n, _, blk, _ in MATS:
        rows = w[n].shape[0] * blk
        fw[n] = _mat_full(g_mats[:, off:off + rows], blk)
        off += rows
    off = 0
    for n in VECS:
        rows = w[n].size // 128
        fw[n] = _vec_full(g_vecs[:, off:off + rows]).reshape(w[n].shape[:-1] + (D,))
        off += rows

    loss, grad_x, full = _local_step(x[0], loss_target[0], w, fw)
    loss = lax.psum(loss, AXES)
    grad_x = grad_x[None]

    small_names = REPL + VECS
    mine, theirs = _grad_slots(full, _pack_rows([full[k] for k in small_names], REP_ROWS), cc)
    red, red_small = _reduce_scatter(mine, theirs)
    small_all = _all_gather("ag_small_grads", red_small).reshape(REP_ROWS * D)
    g, off = {}, 0
    for n, t, blk, real in MATS:
        rows = w[n].shape[0] * blk
        g[n] = _mat_block(red[off:off + rows], t, blk, real)
        off += rows
    small = dict(zip(small_names, _unpack(small_all, [full[k].shape for k in small_names])))
    for k in REPL:
        g[k] = small[k]
    for k in VECS:
        g[k] = lax.dynamic_slice_in_dim(small[k], dev * 128, 128, axis=small[k].ndim - 1)

    delta, new_m, new_v = {}, {}, {}
    for n, _, _, _ in MATS:
        shp = w[n].shape
        res = _adamw(*[d[n].reshape(-1, shp[-1]) for d in (w, g, mom, vel)])
        delta[n], new_m[n], new_v[n] = [a.reshape(shp) for a in res]
    shapes = [w[k].shape for k in small_names]
    res = _adamw(*[_pack_rows([d[k] for k in small_names], SMALL_ROWS) for d in (w, g, mom, vel)])
    for out, a in zip((delta, new_m, new_v), res):
        out.update(zip(small_names, _unpack(a.reshape(-1), shapes)))
    return (loss, grad_x, *[g[k] for k in WEIGHTS], *[delta[k] for k in WEIGHTS],
            *[new_m[k] for k in WEIGHTS], *[new_v[k] for k in WEIGHTS])
```

```python
import functools
import math

import jax
import jax.numpy as jnp
from jax import lax
from jax.experimental import pallas as pl
from jax.experimental.pallas import tpu as pltpu

f32 = jnp.float32
bf16 = jnp.bfloat16
HI = lax.Precision.HIGHEST
MESH = pl.DeviceIdType.MESH
AXES = ("x", "y", "c")
ANY = pl.BlockSpec(memory_space=pl.ANY)

D = 1024
DEPTH = 4
N_META = 16
PAD = 240
OFF = PAD + N_META
ROW_TILE = 768
KEY_CHUNK = 256
FFN_TILE = 256
GRID_W = 64
HD = 64
NQ = 16
NKV = 4
QW = NQ * HD
KW = NKV * HD
QKVW = QW + 2 * KW
DFF = 2816
GROUPS = 64
GCH = 16
NSTATE = 64
CHUNK = 16
GB = 8
ROPE_THETA = 10000.0
LN_EPS = 1e-5
QK_EPS = 1e-6
ALPHA = (2.0 * DEPTH) ** 0.25
ADAM_LR, ADAM_B1, ADAM_B2, ADAM_EPS, ADAM_WD, ADAM_STEP = 0.001, 0.9, 0.999, 1e-08, 0.01, 10
NEG = -1e30
Q_SCALE = HD ** -0.5 * math.log2(math.e)
VMEM_MB = 56

NT = (((1,), (1,)), ((), ()))
TN = (((0,), (0,)), ((), ()))

WEIGHTS = ['meta_tokens', 's5_lambda_re', 's5_lambda_im', 's5_log_dt', 's5_b_re', 's5_b_im', 's5_c_re', 's5_c_im',
           's5_d', 's5_w_glu', 's5_w_out', 'attn_w_qkv', 'attn_q_gain', 'attn_k_gain', 'attn_w_out', 'ffn_w_gate',
           'ffn_w_up', 'ffn_w_down', 'ln_gain', 'ln_bias']
DFFP = 3072
FF_BLK, FF_BLKP = DFF // 8, DFFP // 8
MATS = [('s5_w_glu', False, 128, 128), ('s5_w_out', False, 128, 128), ('attn_w_qkv', True, 192, 192),
        ('attn_w_out', False, 128, 128), ('ffn_w_gate', True, FF_BLKP, FF_BLK), ('ffn_w_up', True, FF_BLKP, FF_BLK),
        ('ffn_w_down', False, FF_BLKP, FF_BLK)]
VECS = ['meta_tokens', 'ln_gain', 'ln_bias']
REPL = ['s5_lambda_re', 's5_lambda_im', 's5_log_dt', 's5_b_re', 's5_b_im', 's5_c_re', 's5_c_im', 's5_d',
        'attn_q_gain', 'attn_k_gain']
MAT_ROWS = 5760
REP_PIECE = 160
REP_ROWS = 8 * REP_PIECE
RS_TILE = 640
SMALL_ROWS = 1088


def _params(sem, mb=VMEM_MB):
    return pltpu.CompilerParams(dimension_semantics=sem, vmem_limit_bytes=mb << 20)


def _ew(name, fn, rows, consts, outs, accs=(), tile=ROW_TILE):
    first = rows[0][0] if isinstance(rows[0], tuple) else rows[0]
    n = first.shape[-2]
    tile = min(tile, n)
    assert n % tile == 0, (name, n, tile)
    n_in, n_o, n_a = len(rows) + len(consts), len(outs), len(accs)

    def body(*refs):
        i = pl.program_id(0)
        res_o, res_a = fn(i, *[r[...] for r in refs[:n_in]])
        for r, val in zip(refs[n_in:n_in + n_o], res_o):
            r[...] = val.astype(r.dtype)
        if n_a:
            a_refs = refs[n_in + n_o:]

            @pl.when(i == 0)
            def _():
                for r in a_refs:
                    r[...] = jnp.zeros(r.shape, r.dtype)

            for r, val in zip(a_refs, res_a):
                r[...] += val

    in_specs, args = [], []
    for a in rows:
        if isinstance(a, tuple):
            arr, k = a
            in_specs.append(pl.BlockSpec((None, tile, arr.shape[2]), functools.partial(lambda i, k: (k, i, 0), k=k)))
            args.append(arr)
        else:
            in_specs.append(pl.BlockSpec((tile, a.shape[1]), lambda i: (i, 0)))
            args.append(a)
    for c in consts:
        in_specs.append(pl.BlockSpec(c.shape, lambda i: (0, 0)))
        args.append(c)
    out_specs = [pl.BlockSpec((tile, c), lambda i: (i, 0)) for c, _ in outs]
    out_specs += [pl.BlockSpec(s, lambda i: (0, 0)) for s in accs]
    out_shape = [jax.ShapeDtypeStruct((n, c), dt) for c, dt in outs]
    out_shape += [jax.ShapeDtypeStruct(s, f32) for s in accs]
    res = pl.pallas_call(body, grid=(n // tile,), in_specs=in_specs, out_specs=out_specs, out_shape=out_shape,
                         name=name, compiler_params=_params(("arbitrary",)))(*args)
    return res


def _mm(name, a, b, trans_b=False, out_dtype=f32, tm=ROW_TILE):
    m, k = a.shape
    n = b.shape[0] if trans_b else b.shape[1]
    tm = min(tm, m)
    assert m % tm == 0
    dims = NT if trans_b else (((1,), (0,)), ((), ()))

    def body(a_ref, b_ref, o_ref):
        o_ref[...] = lax.dot_general(a_ref[...], b_ref[...], dims, preferred_element_type=f32).astype(o_ref.dtype)

    return pl.pallas_call(
        body, grid=(m // tm,),
        in_specs=[pl.BlockSpec((tm, k), lambda i: (i, 0)), pl.BlockSpec(b.shape, lambda i: (0, 0))],
        out_specs=pl.BlockSpec((tm, n), lambda i: (i, 0)),
        out_shape=jax.ShapeDtypeStruct((m, n), out_dtype), name=name, compiler_params=_params(("parallel",)))(a, b)


def _mm2(name, a1, b1, a2, b2, out_dtype=f32, tm=ROW_TILE // 2):
    m, k = a1.shape
    n = b1.shape[1]
    tm = min(tm, m)
    assert m % tm == 0

    def body(a1_ref, b1_ref, a2_ref, b2_ref, o_ref):
        acc = jnp.dot(a1_ref[...], b1_ref[...], preferred_element_type=f32)
        acc += jnp.dot(a2_ref[...], b2_ref[...], preferred_element_type=f32)
        o_ref[...] = acc.astype(o_ref.dtype)

    row = pl.BlockSpec((tm, k), lambda i: (i, 0))
    whole = pl.BlockSpec(b1.shape, lambda i: (0, 0))
    return pl.pallas_call(
        body, grid=(m // tm,), in_specs=[row, whole, row, whole], out_specs=pl.BlockSpec((tm, n), lambda i: (i, 0)),
        out_shape=jax.ShapeDtypeStruct((m, n), out_dtype), name=name,
        compiler_params=_params(("parallel",)))(a1, b1, a2, b2)


def _mm_tn(name, a, g, tk=512, tl=ROW_TILE):
    rows, k1 = a.shape
    n = g.shape[1]
    tl = min(tl, rows)
    assert rows % tl == 0 and k1 % tk == 0

    def body(a_ref, g_ref, o_ref):
        @pl.when(pl.program_id(1) == 0)
        def _():
            o_ref[...] = jnp.zeros(o_ref.shape, f32)

        o_ref[...] += lax.dot_general(a_ref[...], g_ref[...], TN, preferred_element_type=f32)

    return pl.pallas_call(
        body, grid=(k1 // tk, rows // tl),
        in_specs=[pl.BlockSpec((tl, tk), lambda k, l: (l, k)), pl.BlockSpec((tl, n), lambda k, l: (l, 0))],
        out_specs=pl.BlockSpec((tk, n), lambda k, l: (k, 0)),
        out_shape=jax.ShapeDtypeStruct((k1, n), f32), name=name,
        compiler_params=_params(("parallel", "arbitrary")))(a, g)


def _ln_stats(r):
    mean = jnp.mean(r, axis=-1, keepdims=True)
    c = r - mean
    rstd = lax.rsqrt(jnp.mean(c * c, axis=-1, keepdims=True) + LN_EPS)
    return c * rstd, rstd


def _ln_fwd(h, mix, gain, bias):
    def fn(i, h, mix, g, b):
        r = ALPHA * h + mix
        y = _ln_stats(r)[0] * g + b
        return (r, y, y), ()

    return _ew("ln_fwd", fn, [h, mix], [gain, bias], [(D, f32), (D, f32), (D, bf16)])


def _ln_bwd(d_a, d_b, r, gain):
    def core(dout, r, g):
        xhat, rstd = _ln_stats(r)
        dxh = dout * g
        dr = rstd * (dxh - jnp.mean(dxh, axis=-1, keepdims=True) - xhat * jnp.mean(dxh * xhat, axis=-1, keepdims=True))
        return (dr, dr), (jnp.sum(dout * xhat, axis=0, keepdims=True), jnp.sum(dout, axis=0, keepdims=True))

    outs, accs = [(D, f32), (D, bf16)], [(1, D), (1, D)]
    if d_a is None:
        return _ew("ln_bwd_top", lambda i, d, r, g: core(d, r, g), [d_b, r], [gain], outs, accs)
    return _ew("ln_bwd", lambda i, da, db, r, g: core(ALPHA * da + db, r, g), [d_a, d_b, r], [gain], outs, accs)


def _sigmoid(x):
    return 1.0 / (1.0 + jnp.exp(-x))


def _ffn_up(h_bf, w_gate_t, w_up_t, tm=FFN_TILE):
    m, k = h_bf.shape
    n = w_gate_t.shape[0]

    def body(h_ref, wg_ref, wu_ref, g_ref, u_ref, a_ref):
        h = h_ref[...]
        g = lax.dot_general(h, wg_ref[...], NT, preferred_element_type=f32).astype(bf16)
        u = lax.dot_general(h, wu_ref[...], NT, preferred_element_type=f32).astype(bf16)
        g_ref[...] = g
        u_ref[...] = u
        g = g.astype(f32)
        a_ref[...] = (g * _sigmoid(g) * u.astype(f32)).astype(bf16)

    row = pl.BlockSpec((tm, n), lambda i: (i, 0))
    whole = pl.BlockSpec((n, k), lambda i: (0, 0))
    return pl.pallas_call(
        body, grid=(m // tm,), in_specs=[pl.BlockSpec((tm, k), lambda i: (i, 0)), whole, whole],
        out_specs=[row, row, row], out_shape=[jax.ShapeDtypeStruct((m, n), bf16)] * 3, name="ffn_up",
        compiler_params=_params(("parallel",)))(h_bf, w_gate_t, w_up_t)


def _ffn_dup(df_bf, w_down, gate, up, tm=FFN_TILE):
    m, k = df_bf.shape
    n = w_down.shape[0]

    def body(d_ref, w_ref, g_ref, u_ref, dg_ref, du_ref):
        da = lax.dot_general(d_ref[...], w_ref[...], NT, preferred_element_type=f32).astype(bf16).astype(f32)
        g, u = g_ref[...].astype(f32), u_ref[...].astype(f32)
        s = _sigmoid(g)
        dg_ref[...] = (da * u * s * (1.0 + g * (1.0 - s))).astype(bf16)
        du_ref[...] = (da * g * s).astype(bf16)

    row = pl.BlockSpec((tm, n), lambda i: (i, 0))
    return pl.pallas_call(
        body, grid=(m // tm,),
        in_specs=[pl.BlockSpec((tm, k), lambda i: (i, 0)), pl.BlockSpec((n, k), lambda i: (0, 0)), row, row],
        out_specs=[row, row], out_shape=[jax.ShapeDtypeStruct((m, n), bf16)] * 2, name="ffn_dup",
        compiler_params=_params(("parallel",)))(df_bf, w_down, gate, up)


def _loss_grad(h, target):
    n = h.shape[0]

    def body(h_ref, t_ref, d_ref, sq_ref):
        i = pl.program_id(0)

        @pl.when(i == 0)
        def _():
            d_ref[...] = jnp.zeros(d_ref.shape, f32)
            sq_ref[...] = jnp.zeros(sq_ref.shape, f32)

        @pl.when(i > 0)
        def _():
            e = h_ref[...] - t_ref[...]
            d_ref[...] = e * (1.0 / D)
            sq_ref[...] += jnp.sum(e * e, axis=0, keepdims=True)

    return pl.pallas_call(
        body, grid=(n // OFF,),
        in_specs=[pl.BlockSpec((OFF, D), lambda i: (i, 0)), pl.BlockSpec((OFF, D), lambda i: (jnp.maximum(i - 1, 0), 0))],
        out_specs=[pl.BlockSpec((OFF, D), lambda i: (i, 0)), pl.BlockSpec((1, D), lambda i: (0, 0))],
        out_shape=[jax.ShapeDtypeStruct((n, D), f32), jax.ShapeDtypeStruct((1, D), f32)], name="loss",
        compiler_params=_params(("arbitrary",)))(h, target)


def _adamw(w, g, m, v):
    def fn(i, w, g, m, v):
        m = ADAM_B1 * m + (1.0 - ADAM_B1) * g
        v = ADAM_B2 * v + (1.0 - ADAM_B2) * jnp.square(g)
        m_hat = m / (1.0 - ADAM_B1 ** ADAM_STEP)
        v_hat = v / (1.0 - ADAM_B2 ** ADAM_STEP)
        delta = -ADAM_LR * (m_hat / (jnp.sqrt(v_hat) + ADAM_EPS) + ADAM_WD * w)
        return (delta, m, v), ()

    rows, cols = w.shape
    tile = max(t for t in range(8, min(rows, 544) + 1, 8) if rows % t == 0)
    return _ew("adamw", fn, [w, g, m, v], [], [(cols, f32)] * 3, tile=tile)


def _s5_mats(lam_re, lam_im, log_dt, b_re, b_im, c_re, c_im):
    steps = jnp.arange(CHUNK + 1, dtype=f32)
    n = CHUNK * GCH
    last = n - GCH

    def one(lr, li, ldt, br, bi, cr, ci, reverse):
        dt = jnp.exp(ldt)[:, None]
        mag = jnp.exp(lr * dt)
        abr, abi = mag * jnp.cos(li * dt), mag * jnp.sin(li * dt)
        nr, ni = abr - 1.0, abi
        den = lr * lr + li * li
        zr, zi = (nr * lr + ni * li) / den, (ni * lr - nr * li) / den
        bbr = zr[..., None] * br - zi[..., None] * bi
        bbi = zr[..., None] * bi + zi[..., None] * br
        pmag = jnp.exp((lr * dt)[..., None] * steps)
        pang = (li * dt)[..., None] * steps
        pr, pi = pmag * jnp.cos(pang), pmag * jnp.sin(pang)
        crt, cit = jnp.swapaxes(cr, 1, 2)[:, :, None, :], jnp.swapaxes(ci, 1, 2)[:, :, None, :]
        car = crt * pr[..., None] - cit * pi[..., None]
        cai = crt * pi[..., None] + cit * pr[..., None]
        if reverse:
            taps = slice(CHUNK - 1, None, -1)
            outs = slice(CHUNK, 0, -1)
            ins = slice(0, CHUNK)
        else:
            taps, outs, ins = slice(0, CHUNK), slice(1, CHUNK + 1), slice(CHUNK - 1, None, -1)
        kern = (jnp.einsum('gpi,gpq->giq', bbr, car[:, :, taps].reshape(GROUPS, NSTATE, n), precision=HI)
                - jnp.einsum('gpi,gpq->giq', bbi, cai[:, :, taps].reshape(GROUPS, NSTATE, n), precision=HI))
        wide = jnp.pad(kern, ((0, 0), (0, 0), (0, last) if reverse else (last, 0)))
        m = jnp.stack([wide[:, :, last - GCH * t:last - GCH * t + n] for t in range(CHUNK)], axis=1)
        qr = jnp.swapaxes(pr[:, :, ins], 1, 2)[:, :, None, :]
        qi = jnp.swapaxes(pi[:, :, ins], 1, 2)[:, :, None, :]
        bbrt, bbit = jnp.swapaxes(bbr, 1, 2)[:, None], jnp.swapaxes(bbi, 1, 2)[:, None]
        pin = jnp.concatenate([qr * bbrt - qi * bbit, qr * bbit + qi * bbrt], axis=-1)
        pout = jnp.concatenate([car[:, :, outs].reshape(GROUPS, NSTATE, n),
                                -cai[:, :, outs].reshape(GROUPS, NSTATE, n)], axis=1)
        return (m.reshape(GROUPS, n, n), pin.reshape(GROUPS, n, 2 * NSTATE), pout, pr[:, :, CHUNK], pi[:, :, CHUNK])

    mf, pinf, poutf, arf, aif = one(lam_re[0], lam_im[0], log_dt[0], b_re[0], b_im[0], c_re[0], c_im[0], False)
    mr, pinr, poutr, arr, air = one(lam_re[1], lam_im[1], log_dt[1], b_re[1], b_im[1], c_re[1], c_im[1], True)
    return (mf + mr, jnp.concatenate([pinf, pinr], 2), jnp.concatenate([poutf, poutr], 1),
            jnp.stack([arf, arr]), jnp.stack([aif, air]))


def _s5_coefs(a_re, a_im):
    c1 = jnp.concatenate([a_re, a_re], -1)
    c2 = jnp.concatenate([-a_im, a_im], -1)
    return tuple(c.reshape(GROUPS // GB, 1, GB * 2 * NSTATE) for c in (c1[0], c2[0], c1[1], c2[1]))


def _swap(s):
    w = s.shape[1]
    lane = lax.broadcasted_iota(jnp.int32, s.shape, 1)
    return jnp.where(lane % (2 * NSTATE) < NSTATE, pltpu.roll(s, w - NSTATE, 1), pltpu.roll(s, NSTATE, 1))


def _group_lanes(g):
    return slice(g * 2 * NSTATE, (g + 1) * 2 * NSTATE)


def _s5_states(nc, u_ref, pin_ref, coef, vf, vr, wf, wr, sf, sr):
    c1f, c2f, c1r, c2r = coef
    for g in range(GB):
        v = jnp.dot(u_ref[g], pin_ref[g], preferred_element_type=f32)
        vf[:, _group_lanes(g)] = v[:, :2 * NSTATE]
        vr[:, _group_lanes(g)] = v[:, 2 * NSTATE:]
    wf[...] = _swap(vf[...])
    wr[...] = _swap(vr[...])

    def step(i, carry):
        s_f, t_f, s_r, t_r = carry
        kf, kr = pl.ds(i, 1), pl.ds(nc - 1 - i, 1)
        sf[kf, :] = s_f
        sr[kr, :] = s_r
        s_f, t_f = c1f * s_f + c2f * t_f + vf[kf, :], c1f * t_f - c2f * s_f + wf[kf, :]
        s_r, t_r = c1r * s_r + c2r * t_r + vr[kr, :], c1r * t_r - c2r * s_r + wr[kr, :]
        return s_f, t_f, s_r, t_r

    z = jnp.zeros((1, GB * 2 * NSTATE), f32)
    lax.fori_loop(0, nc, step, (z, z, z, z))


def _s5_core_fwd(ug, msum, pin, pout, coefs):
    nc = ug.shape[1]
    n = CHUNK * GCH

    def body(u_ref, m_ref, pin_ref, pout_ref, c1f, c2f, c1r, c2r, y_ref, vf, vr, wf, wr, sf, sr):
        coef = (c1f[...], c2f[...], c1r[...], c2r[...])
        _s5_states(nc, u_ref, pin_ref, coef, vf, vr, wf, wr, sf, sr)
        for g in range(GB):
            s_in = jnp.concatenate([sf[:, _group_lanes(g)], sr[:, _group_lanes(g)]], axis=1).astype(bf16)
            y_ref[g] = (jnp.dot(u_ref[g], m_ref[g], preferred_element_type=f32)
                        + jnp.dot(s_in, pout_ref[g], preferred_element_type=f32)).astype(bf16)

    seq = pl.BlockSpec((GB, nc, n), lambda i: (i, 0, 0))
    mat = pl.BlockSpec((GB, n, n), lambda i: (i, 0, 0))
    cf = pl.BlockSpec((None, 1, GB * 2 * NSTATE), lambda i: (i, 0, 0))
    scr = pltpu.VMEM((nc, GB * 2 * NSTATE), f32)
    return pl.pallas_call(
        body, grid=(GROUPS // GB,), in_specs=[seq, mat, mat, mat, cf, cf, cf, cf], out_specs=seq,
        out_shape=jax.ShapeDtypeStruct((GROUPS, nc, n), bf16), scratch_shapes=[scr] * 6,
        name="s5_core_fwd", compiler_params=_params(("parallel",)))(ug, msum, pin, pout, *coefs)


def _s5_core_bwd(ug, dyg, msum, pin, pout, coefs):
    nc = ug.shape[1]
    n = CHUNK * GCH

    def body(u_ref, dy_ref, m_ref, pin_ref, pout_ref, c1f, c2f, c1r, c2r,
             du_ref, dm_ref, dpin_ref, dpout_ref, a1f_ref, a2f_ref, a1r_ref, a2r_ref, vf, vr, wf, wr, sf, sr):
        coef = (c1f[...], c2f[...], c1r[...], c2r[...])
        _s5_states(nc, u_ref, pin_ref, coef, vf, vr, wf, wr, sf, sr)
        for g in range(GB):
            s_in = jnp.concatenate([sf[:, _group_lanes(g)], sr[:, _group_lanes(g)]], axis=1).astype(bf16)
            dy = dy_ref[g]
            ds = lax.dot_general(dy, pout_ref[g], NT, preferred_element_type=f32)
            vf[:, _group_lanes(g)] = ds[:, :2 * NSTATE]
            vr[:, _group_lanes(g)] = ds[:, 2 * NSTATE:]
            dpout_ref[g] = lax.dot_general(s_in, dy, TN, preferred_element_type=f32)
            dm_ref[g] = lax.dot_general(u_ref[g], dy, TN, preferred_element_type=f32)

        wf[...] = _swap(vf[...])
        wr[...] = _swap(vr[...])
        k1f, k2f, k1r, k2r = coef[0], -coef[1], coef[2], -coef[3]

        def step(i, carry):
            g_f, h_f, g_r, h_r, a1f, b2f, a1r, b2r = carry
            kf, kr = pl.ds(nc - 1 - i, 1), pl.ds(i, 1)
            s_f, s_r = sf[kf, :], sr[kr, :]
            sf[kf, :] = g_f
            sr[kr, :] = g_r
            a1f, b2f = a1f + g_f * s_f, b2f + h_f * s_f
            a1r, b2r = a1r + g_r * s_r, b2r + h_r * s_r
            g_f, h_f = vf[kf, :] + k1f * g_f + k2f * h_f, wf[kf, :] + k1f * h_f - k2f * g_f
            g_r, h_r = vr[kr, :] + k1r * g_r + k2r * h_r, wr[kr, :] + k1r * h_r - k2r * g_r
            return g_f, h_f, g_r, h_r, a1f, b2f, a1r, b2r

        z = jnp.zeros((1, GB * 2 * NSTATE), f32)
        _, _, _, _, a1f, b2f, a1r, b2r = lax.fori_loop(0, nc, step, (z,) * 8)
        a1f_ref[...], a2f_ref[...], a1r_ref[...], a2r_ref[...] = a1f, _swap(b2f), a1r, _swap(b2r)
        for g in range(GB):
            dv = jnp.concatenate([sf[:, _group_lanes(g)], sr[:, _group_lanes(g)]], axis=1).astype(bf16)
            du_ref[g] = (lax.dot_general(dy_ref[g], m_ref[g], NT, preferred_element_type=f32)
                         + lax.dot_general(dv, pin_ref[g], NT, preferred_element_type=f32)).astype(bf16)
            dpin_ref[g] = lax.dot_general(u_ref[g], dv, TN, preferred_element_type=f32)

    seq = pl.BlockSpec((GB, nc, n), lambda i: (i, 0, 0))
    mat = pl.BlockSpec((GB, n, n), lambda i: (i, 0, 0))
    cf = pl.BlockSpec((None, 1, GB * 2 * NSTATE), lambda i: (i, 0, 0))
    scr = pltpu.VMEM((nc, GB * 2 * NSTATE), f32)
    mat_s = jax.ShapeDtypeStruct((GROUPS, n, n), f32)
    cf_s = jax.ShapeDtypeStruct((GROUPS // GB, 1, GB * 2 * NSTATE), f32)
    return pl.pallas_call(
        body, grid=(GROUPS // GB,), in_specs=[seq, seq, mat, mat, mat, cf, cf, cf, cf],
        out_specs=[seq, mat, mat, mat, cf, cf, cf, cf],
        out_shape=[jax.ShapeDtypeStruct((GROUPS, nc, n), bf16), mat_s, mat_s, mat_s, cf_s, cf_s, cf_s, cf_s],
        scratch_shapes=[scr] * 6, name="s5_core_bwd",
        compiler_params=_params(("parallel",)))(ug, dyg, msum, pin, pout, *coefs)


def _to_groups(a):
    n = a.shape[0]
    return a.reshape(n // CHUNK, CHUNK, GROUPS, GCH).transpose(2, 0, 1, 3).reshape(GROUPS, n // CHUNK, CHUNK * GCH)


def _from_groups(g):
    nc = g.shape[1]
    return g.reshape(GROUPS, nc, CHUNK, GCH).transpose(1, 2, 0, 3).reshape(nc * CHUNK, D)


def _gelu(y):
    return 0.5 * y * (1.0 + lax.erf(y * (2.0 ** -0.5)))


def _gelu_grad(y):
    return 0.5 * (1.0 + lax.erf(y * (2.0 ** -0.5))) + y * jnp.exp(-0.5 * y * y) * (1.0 / math.sqrt(2.0 * math.pi))


def _s5_fwd(h, h_bf, valid, mats, d_skip, w_glu, w_out):
    msum, pin, pout, a_re, a_im = mats
    coefs = _s5_coefs(a_re, a_im)
    ug = _to_groups(jnp.where(valid, h_bf, jnp.zeros_like(h_bf)))
    ys = _from_groups(_s5_core_fwd(ug, msum.astype(bf16), pin.astype(bf16), pout.astype(bf16), coefs))

    def post(i, ys, h, d):
        y = ys.astype(f32) + d * h
        return (y, _gelu(y)), ()

    y, g_bf = _ew("s5_gelu", post, [ys, h], [d_skip], [(D, f32), (D, bf16)])
    gw = _mm("s5_glu_mm", g_bf, w_glu)

    def glu(i, y, gw):
        return (_gelu(y) * _sigmoid(gw),), ()

    z_bf = _ew("s5_glu", glu, [y, gw], [], [(D, bf16)])[0]
    mix = _mm("s5_out_mm", z_bf, w_out)
    return mix, (ug, y, g_bf, gw, z_bf)


def _s5_bwd(dmix_bf, h, valid, saved, mats, vjp_mats, d_skip, w_glu, w_out):
    ug, y, g_bf, gw, z_bf = saved
    msum, pin, pout, a_re, a_im = mats
    coefs = _s5_coefs(a_re, a_im)
    dz = _mm("s5_dz_mm", dmix_bf, w_out, trans_b=True)
    d_w_out = _mm_tn("s5_dwout", z_bf, dmix_bf)

    def dglu(i, dz, y, gw):
        g, s = _gelu(y), _sigmoid(gw)
        return (dz * g * s * (1.0 - s), dz * s), ()

    dgw_bf, dg1 = _ew("s5_dglu", dglu, [dz, y, gw], [], [(D, bf16), (D, f32)])
    d_w_glu = _mm_tn("s5_dwglu", g_bf, dgw_bf)
    dg2 = _mm("s5_dg_mm", dgw_bf, w_glu, trans_b=True)

    def dgelu(i, dg1, dg2, y, h, d):
        dy = (dg1 + dg2) * _gelu_grad(y)
        return (dy, dy * d), (jnp.sum(dy * h, axis=0, keepdims=True),)

    dy_bf, dh_skip, dd = _ew("s5_dgelu", dgelu, [dg1, dg2, y, h], [d_skip], [(D, bf16), (D, f32)], [(1, D)])
    dug, dm, dpin, dpout, a1f, a2f, a1r, a2r = _s5_core_bwd(
        ug, _to_groups(dy_bf), msum.astype(bf16), pin.astype(bf16), pout.astype(bf16), coefs)
    du = _from_groups(dug)
    dh = dh_skip + jnp.where(valid, du, jnp.zeros_like(du)).astype(f32)
    a1 = jnp.stack([a1f, a1r]).reshape(2, GROUPS, 2 * NSTATE)
    a2 = jnp.stack([a2f, a2r]).reshape(2, GROUPS, 2 * NSTATE)
    da_re = a1[..., :NSTATE] + a1[..., NSTATE:]
    da_im = a2[..., NSTATE:] - a2[..., :NSTATE]
    d_params = vjp_mats((dm, dpin, dpout, da_re, da_im))
    return dh, d_params, dd[0], d_w_glu, d_w_out


def _rope_tables(n):
    row = jnp.arange(n, dtype=jnp.int32) - OFF
    real = row >= 0
    rid = jnp.where(real, row // GRID_W, 0).astype(f32)
    cid = jnp.where(real, row % GRID_W, 0).astype(f32)
    half = HD // 2
    inv = ROPE_THETA ** (-jnp.arange(0, half, 2, dtype=f32) / half)
    ar, ac = rid[:, None] * inv[None, :], cid[:, None] * inv[None, :]
    cos = jnp.concatenate([jnp.cos(ar), jnp.cos(ar), jnp.cos(ac), jnp.cos(ac)], axis=1)
    sin = jnp.concatenate([-jnp.sin(ar), jnp.sin(ar), -jnp.sin(ac), jnp.sin(ac)], axis=1)
    return jnp.tile(cos, (1, 2)), jnp.tile(sin, (1, 2))


def _head_mats():
    head = jnp.arange(QW, dtype=jnp.int32)[:, None] // HD == jnp.arange(128, dtype=jnp.int32)[None, :]
    return head.astype(f32) * (1.0 / HD), head.astype(f32).T


def _rot(v):
    w = v.shape[1]
    lane = lax.broadcasted_iota(jnp.int32, v.shape, 1)
    return jnp.where(lane % 32 < 16, pltpu.roll(v, w - 16, 1), pltpu.roll(v, 16, 1))


def _head_mean(v, e, et):
    w = v.shape[1]
    m = jnp.dot(v, e[:w], preferred_element_type=f32, precision=HI)
    return m, et[:, :w]


def _rms_rope(t, gain, e, et, cos, sin):
    w = t.shape[1]
    ms, spread = _head_mean(t * t, e, et)
    rs = jnp.dot(lax.rsqrt(ms + QK_EPS), spread, preferred_element_type=f32, precision=HI)
    n0 = t * rs
    n = n0 * gain
    reps = w // 128
    return n * jnp.tile(cos, (1, reps)) + _rot(n) * jnp.tile(sin, (1, reps))


def _rms_rope_bwd(dout, t, gain, e, et, cos, sin):
    w = t.shape[1]
    reps = w // 128
    ms, spread = _head_mean(t * t, e, et)
    rs = jnp.dot(lax.rsqrt(ms + QK_EPS), spread, preferred_element_type=f32, precision=HI)
    n0 = t * rs
    dn = dout * jnp.tile(cos, (1, reps)) + _rot(dout * jnp.tile(sin, (1, reps)))
    dn0 = dn * gain
    mm, _ = _head_mean(dn0 * n0, e, et)
    corr = jnp.dot(mm, spread, preferred_element_type=f32, precision=HI)
    return rs * (dn0 - n0 * corr), jnp.sum(dn * n0, axis=0, keepdims=True)


def _qk_fwd(qkv, qg, kg, e, et, cos, sin):
    def fn(i, qkv, cos, sin, qg, kg, e, et):
        q = _rms_rope(qkv[:, :QW], qg, e, et, cos, sin) * Q_SCALE
        k = _rms_rope(qkv[:, QW:QW + KW], kg, e, et, cos, sin)
        return (q, k, qkv[:, QW + KW:]), ()

    return _ew("qk_rope", fn, [qkv, cos, sin], [qg, kg, e, et], [(QW, bf16), (KW, bf16), (KW, bf16)])


def _qk_bwd(qkv, dq, dk, dv, qg, kg, e, et, cos, sin):
    def fn(i, qkv, cos, sin, dq, dk, dv, qg, kg, e, et):
        dtq, dgq = _rms_rope_bwd(dq * (HD ** -0.5), qkv[:, :QW], qg, e, et, cos, sin)
        dtk, dgk = _rms_rope_bwd(dk * math.log(2.0), qkv[:, QW:QW + KW], kg, e, et, cos, sin)
        return (jnp.concatenate([dtq, dtk, dv], axis=1),), (dgq, dgk)

    return _ew("qk_rope_bwd", fn, [qkv, cos, sin, dq, dk, dv], [qg, kg, e, et], [(QKVW, bf16)], [(1, QW), (1, KW)])


def _to_heads(a, nh):
    return a.reshape(a.shape[0], nh, HD).transpose(1, 0, 2)


def _from_heads(a):
    return a.transpose(1, 0, 2).reshape(a.shape[1], a.shape[0] * HD)


def _masked_first(s, c):
    if c:
        return s
    col = lax.broadcasted_iota(jnp.int32, (1, s.shape[1]), 1)
    return jnp.where(col >= PAD, s, NEG)


def _flash_fwd(q, k, v1, tq=ROW_TILE, tc=KEY_CHUNK):
    n = q.shape[1]
    nc = n // tc

    def body(q_ref, k_ref, v_ref, o_ref, lse_ref):
        qb = q_ref[0]

        def scores(c):
            ks = k_ref[0, pl.ds(c * tc, tc), :]
            return _masked_first(lax.dot_general(qb, ks, NT, preferred_element_type=f32), c)

        m = jnp.full((tq, 1), NEG, f32)
        acc = jnp.zeros((tq, 2 * HD), f32)
        s_next = scores(0)
        for c in range(nc):
            s = s_next
            if c + 1 < nc:
                s_next = scores(c + 1)
            m_new = jnp.maximum(m, jnp.max(s, axis=1, keepdims=True))
            p = jnp.exp2(s - m_new)
            acc = jnp.exp2(m - m_new) * acc + jnp.dot(p.astype(bf16), v_ref[0, pl.ds(c * tc, tc), :],
                                                      preferred_element_type=f32)
            m = m_new
        l = acc[:, HD:HD + 1]
        o_ref[0] = acc[:, :HD] / l
        lse_ref[0] = m + jnp.log2(l)

    return pl.pallas_call(
        body, grid=(NQ, n // tq),
        in_specs=[pl.BlockSpec((1, tq, HD), lambda h, i: (h, i, 0)),
                  pl.BlockSpec((1, n, HD), lambda h, i: (h // (NQ // NKV), 0, 0)),
                  pl.BlockSpec((1, n, 2 * HD), lambda h, i: (h // (NQ // NKV), 0, 0))],
        out_specs=[pl.BlockSpec((1, tq, HD), lambda h, i: (h, i, 0)), pl.BlockSpec((1, tq, 1), lambda h, i: (h, i, 0))],
        out_shape=[jax.ShapeDtypeStruct((NQ, n, HD), f32), jax.ShapeDtypeStruct((NQ, n, 1), f32)],
        name="flash_fwd", compiler_params=_params(("parallel", "parallel")))(q, k, v1)


def _flash_bwd(q, k, kt, v, do, lse_row, delta_row, tk=ROW_TILE, tc=KEY_CHUNK):
    n = q.shape[1]
    nc = n // tc
    grp = NQ // NKV

    def body(q_ref, do_ref, lse_ref, delta_ref, k_ref, kt_ref, v_ref, dqt_ref, dk_ref, dv_ref):
        j, g = pl.program_id(1), pl.program_id(2)
        kb, vb, ktb = k_ref[0], v_ref[0], kt_ref[0]
        valid = lax.broadcasted_iota(jnp.int32, (tk, 1), 0) + j * tk >= PAD

        @pl.when(j == 0)
        def _():
            dqt_ref[g] = jnp.zeros((HD, n), f32)

        def products(c):
            rows = pl.ds(c * tc, tc)
            return (lax.dot_general(kb, q_ref[0, rows, :], NT, preferred_element_type=f32),
                    lax.dot_general(vb, do_ref[0, rows, :], NT, preferred_element_type=f32))

        dk = jnp.zeros((tk, HD), f32)
        dv = jnp.zeros((tk, HD), f32)
        nxt = products(0)
        for c in range(nc):
            st, dpt = nxt
            if c + 1 < nc:
                nxt = products(c + 1)
            rows = pl.ds(c * tc, tc)
            pt = jnp.exp2(jnp.where(valid, st, NEG) - lse_ref[0, :, rows])
            dv = dv + jnp.dot(pt.astype(bf16), do_ref[0, rows, :], preferred_element_type=f32)
            dst = (pt * (dpt - delta_ref[0, :, rows])).astype(bf16)
            dk = dk + jnp.dot(dst, q_ref[0, rows, :], preferred_element_type=f32)
            dqt_ref[g, :, rows] += jnp.dot(ktb, dst, preferred_element_type=f32)

        @pl.when(g == 0)
        def _():
            dk_ref[0] = dk
            dv_ref[0] = dv

        @pl.when(g > 0)
        def _():
            dk_ref[0] += dk
            dv_ref[0] += dv

    hspec = pl.BlockSpec((1, n, HD), lambda h, j, g: (h * grp + g, 0, 0))
    rspec = pl.BlockSpec((1, 1, n), lambda h, j, g: (h * grp + g, 0, 0))
    kspec = pl.BlockSpec((1, tk, HD), lambda h, j, g: (h, j, 0))
    return pl.pallas_call(
        body, grid=(NKV, n // tk, grp),
        in_specs=[hspec, hspec, rspec, rspec, kspec, pl.BlockSpec((1, HD, tk), lambda h, j, g: (h, 0, j)), kspec],
        out_specs=[pl.BlockSpec((grp, HD, n), lambda h, j, g: (h, 0, 0)), kspec, kspec],
        out_shape=[jax.ShapeDtypeStruct((NQ, HD, n), f32)] + [jax.ShapeDtypeStruct((NKV, n, HD), f32)] * 2,
        name="flash_bwd", compiler_params=_params(("parallel", "arbitrary", "arbitrary")))(
            q, do, lse_row, delta_row, k, kt, v)


def _attn_fwd(h_bf, w_qkv_t, qg, kg, w_out, tabs):
    e, et, cos, sin = tabs
    qkv = _mm("attn_qkv_mm", h_bf, w_qkv_t, trans_b=True)
    q_bf, k_bf, v_bf = _qk_fwd(qkv, qg, kg, e, et, cos, sin)
    q16, k4, v4 = _to_heads(q_bf, NQ), _to_heads(k_bf, NKV), _to_heads(v_bf, NKV)
    ones = jnp.zeros((NKV, v4.shape[1], HD), bf16).at[:, :, 0].set(1.0)
    o16, lse = _flash_fwd(q16, k4, jnp.concatenate([v4, ones], axis=2))
    o = _from_heads(o16)
    o_bf = o.astype(bf16)
    mix = _mm("attn_out_mm", o_bf, w_out)
    return mix, (qkv, q16, k4, v4, o, lse, o_bf)


def _attn_bwd(dmix_bf, h_bf, saved, w_qkv_t, qg, kg, w_out, tabs):
    e, et, cos, sin = tabs
    qkv, q16, k4, v4, o, lse, o_bf = saved
    n = qkv.shape[0]
    do = _mm("attn_do_mm", dmix_bf, w_out, trans_b=True, out_dtype=bf16)
    d_w_out = _mm_tn("attn_dwout", o_bf, dmix_bf)

    def head_dots(i, do, o, e):
        return (jnp.dot(do.astype(f32) * o, e, preferred_element_type=f32, precision=HI) * HD,), ()

    delta = _ew("attn_delta", head_dots, [do, o], [e], [(128, f32)])[0]
    dqt, dk4, dv4 = _flash_bwd(q16, k4, k4.transpose(0, 2, 1), v4, _to_heads(do, NQ), lse.reshape(NQ, 1, n),
                               delta[:, :NQ].T.reshape(NQ, 1, n))
    dq = dqt.transpose(2, 0, 1).reshape(n, QW)
    dqkv_bf, dgq, dgk = _qk_bwd(qkv, dq, _from_heads(dk4), _from_heads(dv4), qg, kg, e, et, cos, sin)
    d_w_qkv_t = _mm_tn("attn_dwqkv", dqkv_bf, h_bf)
    dh = _mm("attn_dh_mm", dqkv_bf, w_qkv_t)
    return dh, d_w_qkv_t, dgq.reshape(NQ, HD).sum(0), dgk.reshape(NKV, HD).sum(0), d_w_out


def _all_gather(name, shard):
    def body(x_ref, out_ref, send_sems, recv_sems, local_sem):
        x, y, c = lax.axis_index("x"), lax.axis_index("y"), lax.axis_index("c")
        me, sibling = (x, y, c), (x, y, 1 - c)
        chips = [(1 - x, y), (x, 1 - y), (1 - x, 1 - y)]

        def slot(px, py, pc):
            return out_ref.at[4 * px + 2 * py + pc]

        def copy(k, block, to, src=None):
            return pltpu.make_async_remote_copy(
                src_ref=slot(*block) if src is None else src, dst_ref=slot(*block),
                send_sem=send_sems.at[k], recv_sem=recv_sems.at[k], device_id=to, device_id_type=MESH)

        mine = pltpu.make_async_copy(x_ref, slot(*me), local_sem)
        mine.start()
        first = [copy(0, me, sibling, src=x_ref)]
        first += [copy(1 + j, me, (*chip, c), src=x_ref) for j, chip in enumerate(chips)]
        for cp in first:
            cp.start()
        passed = [copy(4 + j, (*chip, c), sibling) for j, chip in enumerate(chips)]
        for j, chip in enumerate(chips):
            copy(1 + j, (*chip, c), me).wait_recv()
            passed[j].start()
        copy(0, sibling, me).wait_recv()
        for j, chip in enumerate(chips):
            copy(4 + j, (*chip, 1 - c), me).wait_recv()
        for cp in first + passed:
            cp.wait_send()
        mine.wait()

    return pl.pallas_call(
        body, out_shape=jax.ShapeDtypeStruct((8,) + shard.shape, shard.dtype), in_specs=[ANY], out_specs=ANY,
        scratch_shapes=[pltpu.SemaphoreType.DMA((7,)), pltpu.SemaphoreType.DMA((7,)), pltpu.SemaphoreType.DMA],
        name=name)(shard)


def _swap_sibling(name, theirs):
    k = len(theirs)

    def body(*refs):
        src, dst, send_sems, recv_sems = refs[:k], refs[k:2 * k], refs[2 * k], refs[2 * k + 1]
        x, y, c = lax.axis_index("x"), lax.axis_index("y"), lax.axis_index("c")
        copies = [pltpu.make_async_remote_copy(src_ref=src[j], dst_ref=dst[j], send_sem=send_sems.at[j],
                                               recv_sem=recv_sems.at[j], device_id=(x, y, 1 - c), device_id_type=MESH)
                  for j in range(k)]
        for cp in copies:
            cp.start()
        for cp in copies:
            cp.wait()

    return pl.pallas_call(
        body, out_shape=[jax.ShapeDtypeStruct(a.shape, a.dtype) for a in theirs], in_specs=[ANY] * k,
        out_specs=[ANY] * k, scratch_shapes=[pltpu.SemaphoreType.DMA((k,)), pltpu.SemaphoreType.DMA((k,))],
        name=name)(*theirs)


def _exchange_chips(name, parts):
    k = len(parts)

    def body(*refs):
        p_refs, t_refs = refs[:k], refs[k:2 * k]
        send_sems, recv_sems, local_sems = refs[2 * k:]
        x, y, c = lax.axis_index("x"), lax.axis_index("y"), lax.axis_index("c")
        q = 2 * x + y
        copies = []
        for j in range(k):
            copies.append(pltpu.make_async_copy(p_refs[j].at[q], t_refs[j].at[q], local_sems.at[j]))
            for hop in (1, 2, 3):
                tx, ty = x ^ (hop >> 1), y ^ (hop & 1)
                copies.append(pltpu.make_async_remote_copy(
                    src_ref=p_refs[j].at[2 * tx + ty], dst_ref=t_refs[j].at[q], send_sem=send_sems.at[3 * j + hop - 1],
                    recv_sem=recv_sems.at[3 * j + hop - 1], device_id=(tx, ty, c), device_id_type=MESH))
        for cp in copies:
            cp.start()
        for cp in copies:
            cp.wait()

    return pl.pallas_call(
        body, out_shape=[jax.ShapeDtypeStruct(a.shape, a.dtype) for a in parts], in_specs=[ANY] * k,
        out_specs=[ANY] * k,
        scratch_shapes=[pltpu.SemaphoreType.DMA((3 * k,)), pltpu.SemaphoreType.DMA((3 * k,)),
                        pltpu.SemaphoreType.DMA((k,))],
        name=name)(*parts)


def _reduce_scatter(mine, theirs):
    got = _swap_sibling("rs_sibling", list(theirs))
    parts = []
    for a, b, dt, nm in zip(mine, got, (bf16, f32), ("rs_add2", "rs_add2_small")):
        rows = 4 * a.shape[1]
        parts.append(_ew(nm, lambda i, a, b: ((a + b,), ()), [a.reshape(rows, D), b.reshape(rows, D)], [],
                         [(D, dt)], tile=RS_TILE)[0].reshape(a.shape))
    ts = _exchange_chips("rs_chips", parts)

    def add4(i, a, b, c, d):
        return ((((a.astype(f32) + b.astype(f32)) + c.astype(f32)) + d.astype(f32),), ())

    return [_ew(nm, add4, [(t, 0), (t, 1), (t, 2), (t, 3)], [], [(D, f32)], tile=RS_TILE)[0]
            for t, nm in zip(ts, ("rs_add4", "rs_add4_small"))]


def _pack_rows(parts, rows):
    flat = jnp.concatenate([p.reshape(-1) for p in parts])
    return jnp.pad(flat, (0, rows * D - flat.shape[0])).reshape(rows, D)


def _unpack(flat, shapes):
    out, off = [], 0
    for s in shapes:
        n = math.prod(s)
        out.append(flat[off:off + n].reshape(s))
        off += n
    return out


def _mat_rows(block, transposed, blk):
    a = jnp.swapaxes(block, 1, 2) if transposed else block
    a = jnp.pad(a, ((0, 0), (0, blk - a.shape[1]), (0, 0)))
    return a.reshape(-1, D)


def _mat_block(rows, transposed, blk, real):
    a = rows.reshape(-1, blk, D)[:, :real]
    return jnp.swapaxes(a, 1, 2) if transposed else a


def _mat_full(gathered, blk):
    layers = gathered.shape[1] // blk
    return gathered.reshape(8, layers, blk, D).transpose(1, 0, 2, 3).reshape(layers, 8 * blk, D)


def _vec_full(gathered):
    return gathered.transpose(1, 0, 2).reshape(gathered.shape[1], D)


def _grad_slots(full, small, cc):
    def halves(a, blk):
        a4 = a.reshape(4, 2, blk, D)
        return [lax.dynamic_index_in_dim(a4, sel, axis=1, keepdims=False) for sel in (cc, 1 - cc)]

    mine, theirs = [], []
    for name, _, blk, _ in MATS:
        for layer in full[name]:
            a, b = halves(layer, blk)
            mine.append(a)
            theirs.append(b)
    a, b = halves(small, REP_PIECE)
    return (jnp.concatenate(mine, axis=1), a), (jnp.concatenate(theirs, axis=1), b)


def _local_step(x0, target0, w, fw):
    seq = x0.shape[0]
    n = OFF + seq
    valid = (jnp.arange(n, dtype=jnp.int32) >= PAD)[:, None]
    h = jnp.concatenate([jnp.zeros((PAD, D), f32), fw['meta_tokens'], x0], axis=0)
    h_bf = h.astype(bf16)
    tabs = _head_mats() + _rope_tables(n)
    qg = [jnp.tile(w['attn_q_gain'][j], NQ)[None, :] for j in range(2)]
    kg = [jnp.tile(w['attn_k_gain'][j], NKV)[None, :] for j in range(2)]
    s5_names = ['s5_lambda_re', 's5_lambda_im', 's5_log_dt', 's5_b_re', 's5_b_im', 's5_c_re', 's5_c_im']
    s5_mats, s5_vjp = [], []
    for j in range(2):
        mats, vjp = jax.vjp(_s5_mats, *[w[k][j] for k in s5_names])
        s5_mats.append(mats)
        s5_vjp.append(vjp)
    saved = []
    for i in range(DEPTH):
        j = i // 2
        if i % 2 == 0:
            mix, sv = _s5_fwd(h, h_bf, valid, s5_mats[j], w['s5_d'][j][None, :], fw['s5_w_glu'][j], fw['s5_w_out'][j])
        else:
            mix, sv = _attn_fwd(h_bf, fw['attn_w_qkv'][j], qg[j], kg[j], fw['attn_w_out'][j], tabs)
        r1, h1, h1_bf = _ln_fwd(h, mix, fw['ln_gain'][i, 0][None, :], fw['ln_bias'][i, 0][None, :])
        gate, up, act = _ffn_up(h1_bf, fw['ffn_w_gate'][i], fw['ffn_w_up'][i])
        f = _mm("ffn_down_mm", act, fw['ffn_w_down'][i])
        r2, h2, h2_bf = _ln_fwd(h1, f, fw['ln_gain'][i, 1][None, :], fw['ln_bias'][i, 1][None, :])
        saved.append((h, h_bf, sv, r1, h1_bf, gate, up, act, r2))
        h, h_bf = h2, h2_bf

    d_b, sq = _loss_grad(h, target0)
    loss = 0.5 * jnp.sum(sq) * (1.0 / D)

    grads = {k: [None] * (DEPTH if k.startswith('ffn') else 2) for k in WEIGHTS}
    d_ln_gain = [[None, None] for _ in range(DEPTH)]
    d_ln_bias = [[None, None] for _ in range(DEPTH)]
    d_a = None
    for i in reversed(range(DEPTH)):
        j = i // 2
        h_in, h_in_bf, sv, r1, h1_bf, gate, up, act, r2 = saved[i]
        dr2, dr2_bf, dg, db = _ln_bwd(d_a, d_b, r2, fw['ln_gain'][i, 1][None, :])
        d_ln_gain[i][1], d_ln_bias[i][1] = dg[0], db[0]
        dgate, dup = _ffn_dup(dr2_bf, fw['ffn_w_down'][i], gate, up)
        grads['ffn_w_down'][i] = _mm_tn("ffn_dwdown", act, dr2_bf, tk=DFFP // 2)
        grads['ffn_w_gate'][i] = _mm_tn("ffn_dwgate", dgate, h1_bf, tk=DFFP // 2)
        grads['ffn_w_up'][i] = _mm_tn("ffn_dwup", dup, h1_bf, tk=DFFP // 2)
        dh1 = _mm2("ffn_dh_mm", dgate, fw['ffn_w_gate'][i], dup, fw['ffn_w_up'][i])
        dr1, dr1_bf, dg, db = _ln_bwd(dr2, dh1, r1, fw['ln_gain'][i, 0][None, :])
        d_ln_gain[i][0], d_ln_bias[i][0] = dg[0], db[0]
        if i % 2 == 0:
            dh, d_par, dd, d_w_glu, d_w_out = _s5_bwd(dr1_bf, h_in, valid, sv, s5_mats[j], s5_vjp[j],
                                                      w['s5_d'][j][None, :], fw['s5_w_glu'][j], fw['s5_w_out'][j])
            for k, g in zip(s5_names, d_par):
                grads[k][j] = g
            grads['s5_d'][j], grads['s5_w_glu'][j], grads['s5_w_out'][j] = dd, d_w_glu, d_w_out
        else:
            dh, d_w_qkv, dgq, dgk, d_w_out = _attn_bwd(dr1_bf, h_in_bf, sv, fw['attn_w_qkv'][j], qg[j], kg[j],
                                                       fw['attn_w_out'][j], tabs)
            grads['attn_w_qkv'][j], grads['attn_w_out'][j] = d_w_qkv, d_w_out
            grads['attn_q_gain'][j], grads['attn_k_gain'][j] = dgq, dgk
        d_a, d_b = dr1, dh
    dh0 = _ew("dh0", lambda i, a, b: ((ALPHA * a + b,), ()), [d_a, d_b], [], [(D, f32)])[0]
    mats = {m[0] for m in MATS}
    full = {k: (v if k in mats else jnp.stack(v)) for k, v in grads.items() if v[0] is not None}
    full['meta_tokens'] = dh0[PAD:OFF]
    full['ln_gain'] = jnp.stack([jnp.stack(r) for r in d_ln_gain])
    full['ln_bias'] = jnp.stack([jnp.stack(r) for r in d_ln_bias])

    return loss, dh0[OFF:], full


def kernel(x, meta_tokens, s5_lambda_re, s5_lambda_im, s5_log_dt, s5_b_re, s5_b_im, s5_c_re, s5_c_im, s5_d, s5_w_glu, s5_w_out, attn_w_qkv, attn_q_gain, attn_k_gain, attn_w_out, ffn_w_gate, ffn_w_up, ffn_w_down, ln_gain, ln_bias, loss_target, m_meta_tokens, m_s5_lambda_re, m_s5_lambda_im, m_s5_log_dt, m_s5_b_re, m_s5_b_im, m_s5_c_re, m_s5_c_im, m_s5_d, m_s5_w_glu, m_s5_w_out, m_attn_w_qkv, m_attn_q_gain, m_attn_k_gain, m_attn_w_out, m_ffn_w_gate, m_ffn_w_up, m_ffn_w_down, m_ln_gain, m_ln_bias, v_meta_tokens, v_s5_lambda_re, v_s5_lambda_im, v_s5_log_dt, v_s5_b_re, v_s5_b_im, v_s5_c_re, v_s5_c_im, v_s5_d, v_s5_w_glu, v_s5_w_out, v_attn_w_qkv, v_attn_q_gain, v_attn_k_gain, v_attn_w_out, v_ffn_w_gate, v_ffn_w_up, v_ffn_w_down, v_ln_gain, v_ln_bias):
    w = dict(zip(WEIGHTS, (meta_tokens, s5_lambda_re, s5_lambda_im, s5_log_dt, s5_b_re, s5_b_im, s5_c_re, s5_c_im, s5_d, s5_w_glu, s5_w_out, attn_w_qkv, attn_q_gain, attn_k_gain, attn_w_out, ffn_w_gate, ffn_w_up, ffn_w_down, ln_gain, ln_bias)))
    mom = dict(zip(WEIGHTS, (m_meta_tokens, m_s5_lambda_re, m_s5_lambda_im, m_s5_log_dt, m_s5_b_re, m_s5_b_im, m_s5_c_re, m_s5_c_im, m_s5_d, m_s5_w_glu, m_s5_w_out, m_attn_w_qkv, m_attn_q_gain, m_attn_k_gain, m_attn_w_out, m_ffn_w_gate, m_ffn_w_up, m_ffn_w_down, m_ln_gain, m_ln_bias)))
    vel = dict(zip(WEIGHTS, (v_meta_tokens, v_s5_lambda_re, v_s5_lambda_im, v_s5_log_dt, v_s5_b_re, v_s5_b_im, v_s5_c_re, v_s5_c_im, v_s5_d, v_s5_w_glu, v_s5_w_out, v_attn_w_qkv, v_attn_q_gain, v_attn_k_gain, v_attn_w_out, v_ffn_w_gate, v_ffn_w_up, v_ffn_w_down, v_ln_gain, v_ln_bias)))
    cc = lax.axis_index("c")
    dev = 4 * lax.axis_index("x") + 2 * lax.axis_index("y") + cc

    mat_rows = jnp.concatenate([_mat_rows(w[n], t, blk) for n, t, blk, _ in MATS]).astype(bf16)
    g_mats = _all_gather("ag_weights", mat_rows)
    g_vecs = _all_gather("ag_vectors", jnp.concatenate([w[n].reshape(-1, 128) for n in VECS]))
    fw, off = {}, 0
    for n, _, blk, _ in MATS:
        rows = w[n].shape[0] * blk
        fw[n] = _mat_full(g_mats[:, off:off + rows], blk)
        off += rows
    off = 0
    for n in VECS:
        rows = w[n].size // 128
        fw[n] = _vec_full(g_vecs[:, off:off + rows]).reshape(w[n].shape[:-1] + (D,))
        off += rows

    loss, grad_x, full = _local_step(x[0], loss_target[0], w, fw)
    loss = lax.psum(loss, AXES)
    grad_x = grad_x[None]

    small_names = REPL + VECS
    mine, theirs = _grad_slots(full, _pack_rows([full[k] for k in small_names], REP_ROWS), cc)
    red, red_small = _reduce_scatter(mine, theirs)
    small_all = _all_gather("ag_small_grads", red_small).reshape(REP_ROWS * D)
    g, off = {}, 0
    for n, t, blk, real in MATS:
        rows = w[n].shape[0] * blk
        g[n] = _mat_block(red[off:off + rows], t, blk, real)
        off += rows
    small = dict(zip(small_names, _unpack(small_all, [full[k].shape for k in small_names])))
    for k in REPL:
        g[k] = small[k]
    for k in VECS:
        g[k] = lax.dynamic_slice_in_dim(small[k], dev * 128, 128, axis=small[k].ndim - 1)

    delta, new_m, new_v = {}, {}, {}
    for n, _, _, _ in MATS:
        shp = w[n].shape
        res = _adamw(*[d[n].reshape(-1, shp[-1]) for d in (w, g, mom, vel)])
        delta[n], new_m[n], new_v[n] = [a.reshape(shp) for a in res]
    shapes = [w[k].shape for k in small_names]
    res = _adamw(*[_pack_rows([d[k] for k in small_names], SMALL_ROWS) for d in (w, g, mom, vel)])
    for out, a in zip((delta, new_m, new_v), res):
        out.update(zip(small_names, _unpack(a.reshape(-1), shapes)))
    return (loss, grad_x, *[g[k] for k in WEIGHTS], *[delta[k] for k in WEIGHTS],
            *[new_m[k] for k in WEIGHTS], *[new_v[k] for k in WEIGHTS])
```

```python
import functools
import math

import jax
import jax.numpy as jnp
from jax import lax
from jax.experimental import pallas as pl
from jax.experimental.pallas import tpu as pltpu

f32 = jnp.float32
bf16 = jnp.bfloat16
HI = lax.Precision.HIGHEST
MESH = pl.DeviceIdType.MESH
AXES = ("x", "y", "c")
ANY = pl.BlockSpec(memory_space=pl.ANY)

D = 1024
DEPTH = 4
N_META = 16
PAD = 240
OFF = PAD + N_META
ROW_TILE = 768
KEY_CHUNK = 256
FFN_TILE = 256
GRID_W = 64
HD = 64
NQ = 16
NKV = 4
QW = NQ * HD
KW = NKV * HD
QKVW = QW + 2 * KW
DFF = 2816
GROUPS = 64
GCH = 16
NSTATE = 64
CHUNK = 16
GB = 8
ROPE_THETA = 10000.0
LN_EPS = 1e-5
QK_EPS = 1e-6
ALPHA = (2.0 * DEPTH) ** 0.25
ADAM_LR, ADAM_B1, ADAM_B2, ADAM_EPS, ADAM_WD, ADAM_STEP = 0.001, 0.9, 0.999, 1e-08, 0.01, 10
NEG = -1e30
Q_SCALE = HD ** -0.5 * math.log2(math.e)
VMEM_MB = 56

NT = (((1,), (1,)), ((), ()))
TN = (((0,), (0,)), ((), ()))

WEIGHTS = ['meta_tokens', 's5_lambda_re', 's5_lambda_im', 's5_log_dt', 's5_b_re', 's5_b_im', 's5_c_re', 's5_c_im',
           's5_d', 's5_w_glu', 's5_w_out', 'attn_w_qkv', 'attn_q_gain', 'attn_k_gain', 'attn_w_out', 'ffn_w_gate',
           'ffn_w_up', 'ffn_w_down', 'ln_gain', 'ln_bias']
DFFP = 3072
FF_BLK, FF_BLKP = DFF // 8, DFFP // 8
MATS = [('s5_w_glu', False, 128, 128), ('s5_w_out', False, 128, 128), ('attn_w_qkv', True, 192, 192),
        ('attn_w_out', False, 128, 128), ('ffn_w_gate', True, FF_BLKP, FF_BLK), ('ffn_w_up', True, FF_BLKP, FF_BLK),
        ('ffn_w_down', False, FF_BLKP, FF_BLK)]
VECS = ['meta_tokens', 'ln_gain', 'ln_bias']
REPL = ['s5_lambda_re', 's5_lambda_im', 's5_log_dt', 's5_b_re', 's5_b_im', 's5_c_re', 's5_c_im', 's5_d',
        'attn_q_gain', 'attn_k_gain']
MAT_ROWS = 5760
REP_PIECE = 160
REP_ROWS = 8 * REP_PIECE
RS_TILE = 640
SMALL_ROWS = 1088


def _params(sem, mb=VMEM_MB):
    return pltpu.CompilerParams(dimension_semantics=sem, vmem_limit_bytes=mb << 20)


def _ew(name, fn, rows, consts, outs, accs=(), tile=ROW_TILE):
    first = rows[0][0] if isinstance(rows[0], tuple) else rows[0]
    n = first.shape[-2]
    tile = min(tile, n)
    assert n % tile == 0, (name, n, tile)
    n_in, n_o, n_a = len(rows) + len(consts), len(outs), len(accs)

    def body(*refs):
        i = pl.program_id(0)
        res_o, res_a = fn(i, *[r[...] for r in refs[:n_in]])
        for r, val in zip(refs[n_in:n_in + n_o], res_o):
            r[...] = val.astype(r.dtype)
        if n_a:
            a_refs = refs[n_in + n_o:]

            @pl.when(i == 0)
            def _():
                for r in a_refs:
                    r[...] = jnp.zeros(r.shape, r.dtype)

            for r, val in zip(a_refs, res_a):
                r[...] += val

    in_specs, args = [], []
    for a in rows:
        if isinstance(a, tuple):
            arr, k = a
            in_specs.append(pl.BlockSpec((None, tile, arr.shape[2]), functools.partial(lambda i, k: (k, i, 0), k=k)))
            args.append(arr)
        else:
            in_specs.append(pl.BlockSpec((tile, a.shape[1]), lambda i: (i, 0)))
            args.append(a)
    for c in consts:
        in_specs.append(pl.BlockSpec(c.shape, lambda i: (0, 0)))
        args.append(c)
    out_specs = [pl.BlockSpec((tile, c), lambda i: (i, 0)) for c, _ in outs]
    out_specs += [pl.BlockSpec(s, lambda i: (0, 0)) for s in accs]
    out_shape = [jax.ShapeDtypeStruct((n, c), dt) for c, dt in outs]
    out_shape += [jax.ShapeDtypeStruct(s, f32) for s in accs]
    res = pl.pallas_call(body, grid=(n // tile,), in_specs=in_specs, out_specs=out_specs, out_shape=out_shape,
                         name=name, compiler_params=_params(("arbitrary",)))(*args)
    return res


def _mm(name, a, b, trans_b=False, out_dtype=f32, tm=ROW_TILE):
    m, k = a.shape
    n = b.shape[0] if trans_b else b.shape[1]
    tm = min(tm, m)
    assert m % tm == 0
    dims = NT if trans_b else (((1,), (0,)), ((), ()))

    def body(a_ref, b_ref, o_ref):
        o_ref[...] = lax.dot_general(a_ref[...], b_ref[...], dims, preferred_element_type=f32).astype(o_ref.dtype)

    return pl.pallas_call(
        body, grid=(m // tm,),
        in_specs=[pl.BlockSpec((tm, k), lambda i: (i, 0)), pl.BlockSpec(b.shape, lambda i: (0, 0))],
        out_specs=pl.BlockSpec((tm, n), lambda i: (i, 0)),
        out_shape=jax.ShapeDtypeStruct((m, n), out_dtype), name=name, compiler_params=_params(("parallel",)))(a, b)


def _mm2(name, a1, b1, a2, b2, out_dtype=f32, tm=ROW_TILE // 2):
    m, k = a1.shape
    n = b1.shape[1]
    tm = min(tm, m)
    assert m % tm == 0

    def body(a1_ref, b1_ref, a2_ref, b2_ref, o_ref):
        acc = jnp.dot(a1_ref[...], b1_ref[...], preferred_element_type=f32)
        acc += jnp.dot(a2_ref[...], b2_ref[...], preferred_element_type=f32)
        o_ref[...] = acc.astype(o_ref.dtype)

    row = pl.BlockSpec((tm, k), lambda i: (i, 0))
    whole = pl.BlockSpec(b1.shape, lambda i: (0, 0))
    return pl.pallas_call(
        body, grid=(m // tm,), in_specs=[row, whole, row, whole], out_specs=pl.BlockSpec((tm, n), lambda i: (i, 0)),
        out_shape=jax.ShapeDtypeStruct((m, n), out_dtype), name=name,
        compiler_params=_params(("parallel",)))(a1, b1, a2, b2)


def _mm_tn(name, a, g, tk=512, tl=ROW_TILE):
    rows, k1 = a.shape
    n = g.shape[1]
    tl = min(tl, rows)
    assert rows % tl == 0 and k1 % tk == 0

    def body(a_ref, g_ref, o_ref):
        @pl.when(pl.program_id(1) == 0)
        def _():
            o_ref[...] = jnp.zeros(o_ref.shape, f32)

        o_ref[...] += lax.dot_general(a_ref[...], g_ref[...], TN, preferred_element_type=f32)

    return pl.pallas_call(
        body, grid=(k1 // tk, rows // tl),
        in_specs=[pl.BlockSpec((tl, tk), lambda k, l: (l, k)), pl.BlockSpec((tl, n), lambda k, l: (l, 0))],
        out_specs=pl.BlockSpec((tk, n), lambda k, l: (k, 0)),
        out_shape=jax.ShapeDtypeStruct((k1, n), f32), name=name,
        compiler_params=_params(("parallel", "arbitrary")))(a, g)


def _ln_stats(r):
    mean = jnp.mean(r, axis=-1, keepdims=True)
    c = r - mean
    rstd = lax.rsqrt(jnp.mean(c * c, axis=-1, keepdims=True) + LN_EPS)
    return c * rstd, rstd


def _ln_fwd(h, mix, gain, bias):
    def fn(i, h, mix, g, b):
        r = ALPHA * h + mix
        y = _ln_stats(r)[0] * g + b
        return (r, y, y), ()

    return _ew("ln_fwd", fn, [h, mix], [gain, bias], [(D, f32), (D, f32), (D, bf16)])


def _ln_bwd(d_a, d_b, r, gain):
    def core(dout, r, g):
        xhat, rstd = _ln_stats(r)
        dxh = dout * g
        dr = rstd * (dxh - jnp.mean(dxh, axis=-1, keepdims=True) - xhat * jnp.mean(dxh * xhat, axis=-1, keepdims=True))
        return (dr, dr), (jnp.sum(dout * xhat, axis=0, keepdims=True), jnp.sum(dout, axis=0, keepdims=True))

    outs, accs = [(D, f32), (D, bf16)], [(1, D), (1, D)]
    if d_a is None:
        return _ew("ln_bwd_top", lambda i, d, r, g: core(d, r, g), [d_b, r], [gain], outs, accs)
    return _ew("ln_bwd", lambda i, da, db, r, g: core(ALPHA * da + db, r, g), [d_a, d_b, r], [gain], outs, accs)


def _sigmoid(x):
    return 1.0 / (1.0 + jnp.exp(-x))


def _ffn_up(h_bf, w_gate_t, w_up_t, tm=FFN_TILE):
    m, k = h_bf.shape
    n = w_gate_t.shape[0]

    def body(h_ref, wg_ref, wu_ref, g_ref, u_ref, a_ref):
        h = h_ref[...]
        g = lax.dot_general(h, wg_ref[...], NT, preferred_element_type=f32).astype(bf16)
        u = lax.dot_general(h, wu_ref[...], NT, preferred_element_type=f32).astype(bf16)
        g_ref[...] = g
        u_ref[...] = u
        g = g.astype(f32)
        a_ref[...] = (g * _sigmoid(g) * u.astype(f32)).astype(bf16)

    row = pl.BlockSpec((tm, n), lambda i: (i, 0))
    whole = pl.BlockSpec((n, k), lambda i: (0, 0))
    return pl.pallas_call(
        body, grid=(m // tm,), in_specs=[pl.BlockSpec((tm, k), lambda i: (i, 0)), whole, whole],
        out_specs=[row, row, row], out_shape=[jax.ShapeDtypeStruct((m, n), bf16)] * 3, name="ffn_up",
        compiler_params=_params(("parallel",)))(h_bf, w_gate_t, w_up_t)


def _ffn_dup(df_bf, w_down, gate, up, tm=FFN_TILE):
    m, k = df_bf.shape
    n = w_down.shape[0]

    def body(d_ref, w_ref, g_ref, u_ref, dg_ref, du_ref):
        da = lax.dot_general(d_ref[...], w_ref[...], NT, preferred_element_type=f32).astype(bf16).astype(f32)
        g, u = g_ref[...].astype(f32), u_ref[...].astype(f32)
        s = _sigmoid(g)
        dg_ref[...] = (da * u * s * (1.0 + g * (1.0 - s))).astype(bf16)
        du_ref[...] = (da * g * s).astype(bf16)

    row = pl.BlockSpec((tm, n), lambda i: (i, 0))
    return pl.pallas_call(
        body, grid=(m // tm,),
        in_specs=[pl.BlockSpec((tm, k), lambda i: (i, 0)), pl.BlockSpec((n, k), lambda i: (0, 0)), row, row],
        out_specs=[row, row], out_shape=[jax.ShapeDtypeStruct((m, n), bf16)] * 2, name="ffn_dup",
        compiler_params=_params(("parallel",)))(df_bf, w_down, gate, up)


def _loss_grad(h, target):
    n = h.shape[0]

    def body(h_ref, t_ref, d_ref, sq_ref):
        i = pl.program_id(0)

        @pl.when(i == 0)
        def _():
            d_ref[...] = jnp.zeros(d_ref.shape, f32)
            sq_ref[...] = jnp.zeros(sq_ref.shape, f32)

        @pl.when(i > 0)
        def _():
            e = h_ref[...] - t_ref[...]
            d_ref[...] = e * (1.0 / D)
            sq_ref[...] += jnp.sum(e * e, axis=0, keepdims=True)

    return pl.pallas_call(
        body, grid=(n // OFF,),
        in_specs=[pl.BlockSpec((OFF, D), lambda i: (i, 0)), pl.BlockSpec((OFF, D), lambda i: (jnp.maximum(i - 1, 0), 0))],
        out_specs=[pl.BlockSpec((OFF, D), lambda i: (i, 0)), pl.BlockSpec((1, D), lambda i: (0, 0))],
        out_shape=[jax.ShapeDtypeStruct((n, D), f32), jax.ShapeDtypeStruct((1, D), f32)], name="loss",
        compiler_params=_params(("arbitrary",)))(h, target)


def _adamw(w, g, m, v):
    def fn(i, w, g, m, v):
        m = ADAM_B1 * m + (1.0 - ADAM_B1) * g
        v = ADAM_B2 * v + (1.0 - ADAM_B2) * jnp.square(g)
        m_hat = m / (1.0 - ADAM_B1 ** ADAM_STEP)
        v_hat = v / (1.0 - ADAM_B2 ** ADAM_STEP)
        delta = -ADAM_LR * (m_hat / (jnp.sqrt(v_hat) + ADAM_EPS) + ADAM_WD * w)
        return (delta, m, v), ()

    rows, cols = w.shape
    tile = max(t for t in range(8, min(rows, 544) + 1, 8) if rows % t == 0)
    return _ew("adamw", fn, [w, g, m, v], [], [(cols, f32)] * 3, tile=tile)


def _s5_mats(lam_re, lam_im, log_dt, b_re, b_im, c_re, c_im):
    steps = jnp.arange(CHUNK + 1, dtype=f32)
    n = CHUNK * GCH
    last = n - GCH

    def one(lr, li, ldt, br, bi, cr, ci, reverse):
        dt = jnp.exp(ldt)[:, None]
        mag = jnp.exp(lr * dt)
        abr, abi = mag * jnp.cos(li * dt), mag * jnp.sin(li * dt)
        nr, ni = abr - 1.0, abi
        den = lr * lr + li * li
        zr, zi = (nr * lr + ni * li) / den, (ni * lr - nr * li) / den
        bbr = zr[..., None] * br - zi[..., None] * bi
        bbi = zr[..., None] * bi + zi[..., None] * br
        pmag = jnp.exp((lr * dt)[..., None] * steps)
        pang = (li * dt)[..., None] * steps
        pr, pi = pmag * jnp.cos(pang), pmag * jnp.sin(pang)
        crt, cit = jnp.swapaxes(cr, 1, 2)[:, :, None, :], jnp.swapaxes(ci, 1, 2)[:, :, None, :]
        car = crt * pr[..., None] - cit * pi[..., None]
        cai = crt * pi[..., None] + cit * pr[..., None]
        if reverse:
            taps = slice(CHUNK - 1, None, -1)
            outs = slice(CHUNK, 0, -1)
            ins = slice(0, CHUNK)
        else:
            taps, outs, ins = slice(0, CHUNK), slice(1, CHUNK + 1), slice(CHUNK - 1, None, -1)
        kern = (jnp.einsum('gpi,gpq->giq', bbr, car[:, :, taps].reshape(GROUPS, NSTATE, n), precision=HI)
                - jnp.einsum('gpi,gpq->giq', bbi, cai[:, :, taps].reshape(GROUPS, NSTATE, n), precision=HI))
        wide = jnp.pad(kern, ((0, 0), (0, 0), (0, last) if reverse else (last, 0)))
        m = jnp.stack([wide[:, :, last - GCH * t:last - GCH * t + n] for t in range(CHUNK)], axis=1)
        qr = jnp.swapaxes(pr[:, :, ins], 1, 2)[:, :, None, :]
        qi = jnp.swapaxes(pi[:, :, ins], 1, 2)[:, :, None, :]
        bbrt, bbit = jnp.swapaxes(bbr, 1, 2)[:, None], jnp.swapaxes(bbi, 1, 2)[:, None]
        pin = jnp.concatenate([qr * bbrt - qi * bbit, qr * bbit + qi * bbrt], axis=-1)
        pout = jnp.concatenate([car[:, :, outs].reshape(GROUPS, NSTATE, n),
                                -cai[:, :, outs].reshape(GROUPS, NSTATE, n)], axis=1)
        return (m.reshape(GROUPS, n, n), pin.reshape(GROUPS, n, 2 * NSTATE), pout, pr[:, :, CHUNK], pi[:, :, CHUNK])

    mf, pinf, poutf, arf, aif = one(lam_re[0], lam_im[0], log_dt[0], b_re[0], b_im[0], c_re[0], c_im[0], False)
    mr, pinr, poutr, arr, air = one(lam_re[1], lam_im[1], log_dt[1], b_re[1], b_im[1], c_re[1], c_im[1], True)
    return (mf + mr, jnp.concatenate([pinf, pinr], 2), jnp.concatenate([poutf, poutr], 1),
            jnp.stack([arf, arr]), jnp.stack([aif, air]))


def _s5_coefs(a_re, a_im):
    c1 = jnp.concatenate([a_re, a_re], -1)
    c2 = jnp.concatenate([-a_im, a_im], -1)
    return tuple(c.reshape(GROUPS // GB, 1, GB * 2 * NSTATE) for c in (c1[0], c2[0], c1[1], c2[1]))


def _swap(s):
    w = s.shape[1]
    lane = lax.broadcasted_iota(jnp.int32, s.shape, 1)
    return jnp.where(lane % (2 * NSTATE) < NSTATE, pltpu.roll(s, w - NSTATE, 1), pltpu.roll(s, NSTATE, 1))


def _group_lanes(g):
    return slice(g * 2 * NSTATE, (g + 1) * 2 * NSTATE)


def _s5_states(nc, u_ref, pin_ref, coef, vf, vr, wf, wr, sf, sr):
    c1f, c2f, c1r, c2r = coef
    for g in range(GB):
        v = jnp.dot(u_ref[g], pin_ref[g], preferred_element_type=f32)
        vf[:, _group_lanes(g)] = v[:, :2 * NSTATE]
        vr[:, _group_lanes(g)] = v[:, 2 * NSTATE:]
    wf[...] = _swap(vf[...])
    wr[...] = _swap(vr[...])

    def step(i, carry):
        s_f, t_f, s_r, t_r = carry
        kf, kr = pl.ds(i, 1), pl.ds(nc - 1 - i, 1)
        sf[kf, :] = s_f
        sr[kr, :] = s_r
        s_f, t_f = c1f * s_f + c2f * t_f + vf[kf, :], c1f * t_f - c2f * s_f + wf[kf, :]
        s_r, t_r = c1r * s_r + c2r * t_r + vr[kr, :], c1r * t_r - c2r * s_r + wr[kr, :]
        return s_f, t_f, s_r, t_r

    z = jnp.zeros((1, GB * 2 * NSTATE), f32)
    lax.fori_loop(0, nc, step, (z, z, z, z))


def _s5_core_fwd(ug, msum, pin, pout, coefs):
    nc = ug.shape[1]
    n = CHUNK * GCH

    def body(u_ref, m_ref, pin_ref, pout_ref, c1f, c2f, c1r, c2r, y_ref, vf, vr, wf, wr, sf, sr):
        coef = (c1f[...], c2f[...], c1r[...], c2r[...])
        _s5_states(nc, u_ref, pin_ref, coef, vf, vr, wf, wr, sf, sr)
        for g in range(GB):
            s_in = jnp.concatenate([sf[:, _group_lanes(g)], sr[:, _group_lanes(g)]], axis=1).astype(bf16)
            y_ref[g] = (jnp.dot(u_ref[g], m_ref[g], preferred_element_type=f32)
                        + jnp.dot(s_in, pout_ref[g], preferred_element_type=f32)).astype(bf16)

    seq = pl.BlockSpec((GB, nc, n), lambda i: (i, 0, 0))
    mat = pl.BlockSpec((GB, n, n), lambda i: (i, 0, 0))
    cf = pl.BlockSpec((None, 1, GB * 2 * NSTATE), lambda i: (i, 0, 0))
    scr = pltpu.VMEM((nc, GB * 2 * NSTATE), f32)
    return pl.pallas_call(
        body, grid=(GROUPS // GB,), in_specs=[seq, mat, mat, mat, cf, cf, cf, cf], out_specs=seq,
        out_shape=jax.ShapeDtypeStruct((GROUPS, nc, n), bf16), scratch_shapes=[scr] * 6,
        name="s5_core_fwd", compiler_params=_params(("parallel",)))(ug, msum, pin, pout, *coefs)


def _s5_core_bwd(ug, dyg, msum, pin, pout, coefs):
    nc = ug.shape[1]
    n = CHUNK * GCH

    def body(u_ref, dy_ref, m_ref, pin_ref, pout_ref, c1f, c2f, c1r, c2r,
             du_ref, dm_ref, dpin_ref, dpout_ref, a1f_ref, a2f_ref, a1r_ref, a2r_ref, vf, vr, wf, wr, sf, sr):
        coef = (c1f[...], c2f[...], c1r[...], c2r[...])
        _s5_states(nc, u_ref, pin_ref, coef, vf, vr, wf, wr, sf, sr)
        for g in range(GB):
            s_in = jnp.concatenate([sf[:, _group_lanes(g)], sr[:, _group_lanes(g)]], axis=1).astype(bf16)
            dy = dy_ref[g]
            ds = lax.dot_general(dy, pout_ref[g], NT, preferred_element_type=f32)
            vf[:, _group_lanes(g)] = ds[:, :2 * NSTATE]
            vr[:, _group_lanes(g)] = ds[:, 2 * NSTATE:]
            dpout_ref[g] = lax.dot_general(s_in, dy, TN, preferred_element_type=f32)
            dm_ref[g] = lax.dot_general(u_ref[g], dy, TN, preferred_element_type=f32)

        wf[...] = _swap(vf[...])
        wr[...] = _swap(vr[...])
        k1f, k2f, k1r, k2r = coef[0], -coef[1], coef[2], -coef[3]

        def step(i, carry):
            g_f, h_f, g_r, h_r, a1f, b2f, a1r, b2r = carry
            kf, kr = pl.ds(nc - 1 - i, 1), pl.ds(i, 1)
            s_f, s_r = sf[kf, :], sr[kr, :]
            sf[kf, :] = g_f
            sr[kr, :] = g_r
            a1f, b2f = a1f + g_f * s_f, b2f + h_f * s_f
            a1r, b2r = a1r + g_r * s_r, b2r + h_r * s_r
            g_f, h_f = vf[kf, :] + k1f * g_f + k2f * h_f, wf[kf, :] + k1f * h_f - k2f * g_f
            g_r, h_r = vr[kr, :] + k1r * g_r + k2r * h_r, wr[kr, :] + k1r * h_r - k2r * g_r
            return g_f, h_f, g_r, h_r, a1f, b2f, a1r, b2r

        z = jnp.zeros((1, GB * 2 * NSTATE), f32)
        _, _, _, _, a1f, b2f, a1r, b2r = lax.fori_loop(0, nc, step, (z,) * 8)
        a1f_ref[...], a2f_ref[...], a1r_ref[...], a2r_ref[...] = a1f, _swap(b2f), a1r, _swap(b2r)
        for g in range(GB):
            dv = jnp.concatenate([sf[:, _group_lanes(g)], sr[:, _group_lanes(g)]], axis=1).astype(bf16)
            du_ref[g] = (lax.dot_general(dy_ref[g], m_ref[g], NT, preferred_element_type=f32)
                         + lax.dot_general(dv, pin_ref[g], NT, preferred_element_type=f32)).astype(bf16)
            dpin_ref[g] = lax.dot_general(u_ref[g], dv, TN, preferred_element_type=f32)

    seq = pl.BlockSpec((GB, nc, n), lambda i: (i, 0, 0))
    mat = pl.BlockSpec((GB, n, n), lambda i: (i, 0, 0))
    cf = pl.BlockSpec((None, 1, GB * 2 * NSTATE), lambda i: (i, 0, 0))
    scr = pltpu.VMEM((nc, GB * 2 * NSTATE), f32)
    mat_s = jax.ShapeDtypeStruct((GROUPS, n, n), f32)
    cf_s = jax.ShapeDtypeStruct((GROUPS // GB, 1, GB * 2 * NSTATE), f32)
    return pl.pallas_call(
        body, grid=(GROUPS // GB,), in_specs=[seq, seq, mat, mat, mat, cf, cf, cf, cf],
        out_specs=[seq, mat, mat, mat, cf, cf, cf, cf],
        out_shape=[jax.ShapeDtypeStruct((GROUPS, nc, n), bf16), mat_s, mat_s, mat_s, cf_s, cf_s, cf_s, cf_s],
        scratch_shapes=[scr] * 6, name="s5_core_bwd",
        compiler_params=_params(("parallel",)))(ug, dyg, msum, pin, pout, *coefs)


def _block_movers():
    a_in, l_in = jnp.divmod(jnp.arange(GB * 128, dtype=jnp.int32), 128)
    a_out, c_out = jnp.divmod(jnp.arange(128, dtype=jnp.int32), GCH)
    j = jnp.arange(GB, dtype=jnp.int32)[:, None, None]
    hit = (a_in[None, :, None] == a_out[None, None, :]) & (l_in[None, :, None] == GCH * j + c_out[None, None, :])
    return hit.astype(bf16)


def _to_groups(x, movers, mask):
    n = x.shape[0]
    nc = n // CHUNK
    half = CHUNK // 2

    def body(x_ref, mv_ref, o_ref):
        keep = lax.broadcasted_iota(jnp.int32, (nc, 1), 0) >= PAD // CHUNK
        steps = [x_ref[pl.ds(t, nc, stride=CHUNK), :] for t in range(CHUNK)]
        if mask:
            steps = [jnp.where(keep, s, 0.0) for s in steps]
        lo = jnp.concatenate(steps[:half], axis=1).astype(bf16)
        hi = jnp.concatenate(steps[half:], axis=1).astype(bf16)
        for g in range(GB):
            o_ref[g] = jnp.concatenate([jnp.dot(lo, mv_ref[g], preferred_element_type=f32),
                                        jnp.dot(hi, mv_ref[g], preferred_element_type=f32)], axis=1).astype(bf16)

    return pl.pallas_call(
        body, grid=(GROUPS // GB,),
        in_specs=[pl.BlockSpec((n, 128), lambda i: (0, i)), pl.BlockSpec(movers.shape, lambda i: (0, 0, 0))],
        out_specs=pl.BlockSpec((GB, nc, CHUNK * GCH), lambda i: (i, 0, 0)),
        out_shape=jax.ShapeDtypeStruct((GROUPS, nc, CHUNK * GCH), bf16), name="s5_to_groups",
        compiler_params=_params(("parallel",)))(x, movers)


def _from_groups(y, movers, base=None):
    nc = y.shape[1]
    n = nc * CHUNK
    half = CHUNK // 2

    def body(*refs):
        y_ref, mv_ref = refs[:2]
        o_ref = refs[-1]
        keep = lax.broadcasted_iota(jnp.int32, (nc, 1), 0) >= PAD // CHUNK
        lo = jnp.concatenate([y_ref[g][:, :128] for g in range(GB)], axis=1)
        hi = jnp.concatenate([y_ref[g][:, 128:] for g in range(GB)], axis=1)
        for t in range(CHUNK):
            rows = pl.ds(t, nc, stride=CHUNK)
            v = jnp.dot(lo if t < half else hi, mv_ref[t % half], preferred_element_type=f32)
            if base is not None:
                v = refs[2][rows, :] + jnp.where(keep, v, 0.0)
            o_ref[rows, :] = v

    tok = pl.BlockSpec((n, 128), lambda i: (0, i))
    args = (y, movers) if base is None else (y, movers, base)
    return pl.pallas_call(
        body, grid=(GROUPS // GB,),
        in_specs=[pl.BlockSpec((GB, nc, CHUNK * GCH), lambda i: (i, 0, 0)),
                  pl.BlockSpec(movers.shape, lambda i: (0, 0, 0))] + ([] if base is None else [tok]),
        out_specs=tok, out_shape=jax.ShapeDtypeStruct((n, D), f32), name="s5_from_groups",
        compiler_params=_params(("parallel",)))(*args)


def _gelu(y):
    return 0.5 * y * (1.0 + lax.erf(y * (2.0 ** -0.5)))


def _gelu_grad(y):
    return 0.5 * (1.0 + lax.erf(y * (2.0 ** -0.5))) + y * jnp.exp(-0.5 * y * y) * (1.0 / math.sqrt(2.0 * math.pi))


def _s5_fwd(h, movers, mats, d_skip, w_glu, w_out):
    msum, pin, pout, a_re, a_im = mats
    coefs = _s5_coefs(a_re, a_im)
    ug = _to_groups(h, movers, mask=True)
    ys = _from_groups(_s5_core_fwd(ug, msum.astype(bf16), pin.astype(bf16), pout.astype(bf16), coefs), movers)

    def post(i, ys, h, d):
        y = ys + d * h
        return (y, _gelu(y)), ()

    y, g_bf = _ew("s5_gelu", post, [ys, h], [d_skip], [(D, f32), (D, bf16)])
    gw = _mm("s5_glu_mm", g_bf, w_glu)

    def glu(i, y, gw):
        return (_gelu(y) * _sigmoid(gw),), ()

    z_bf = _ew("s5_glu", glu, [y, gw], [], [(D, bf16)])[0]
    mix = _mm("s5_out_mm", z_bf, w_out)
    return mix, (ug, y, g_bf, gw, z_bf)


def _s5_bwd(dmix_bf, h, movers, saved, mats, vjp_mats, d_skip, w_glu, w_out):
    ug, y, g_bf, gw, z_bf = saved
    msum, pin, pout, a_re, a_im = mats
    coefs = _s5_coefs(a_re, a_im)
    dz = _mm("s5_dz_mm", dmix_bf, w_out, trans_b=True)
    d_w_out = _mm_tn("s5_dwout", z_bf, dmix_bf)

    def dglu(i, dz, y, gw):
        g, s = _gelu(y), _sigmoid(gw)
        return (dz * g * s * (1.0 - s), dz * s), ()

    dgw_bf, dg1 = _ew("s5_dglu", dglu, [dz, y, gw], [], [(D, bf16), (D, f32)])
    d_w_glu = _mm_tn("s5_dwglu", g_bf, dgw_bf)
    dg2 = _mm("s5_dg_mm", dgw_bf, w_glu, trans_b=True)

    def dgelu(i, dg1, dg2, y, h, d):
        dy = (dg1 + dg2) * _gelu_grad(y)
        return (dy, dy * d), (jnp.sum(dy * h, axis=0, keepdims=True),)

    dy, dh_skip, dd = _ew("s5_dgelu", dgelu, [dg1, dg2, y, h], [d_skip], [(D, f32), (D, f32)], [(1, D)])
    dug, dm, dpin, dpout, a1f, a2f, a1r, a2r = _s5_core_bwd(
        ug, _to_groups(dy, movers, mask=False), msum.astype(bf16), pin.astype(bf16), pout.astype(bf16), coefs)
    dh = _from_groups(dug, movers, base=dh_skip)
    a1 = jnp.stack([a1f, a1r]).reshape(2, GROUPS, 2 * NSTATE)
    a2 = jnp.stack([a2f, a2r]).reshape(2, GROUPS, 2 * NSTATE)
    da_re = a1[..., :NSTATE] + a1[..., NSTATE:]
    da_im = a2[..., NSTATE:] - a2[..., :NSTATE]
    d_params = vjp_mats((dm, dpin, dpout, da_re, da_im))
    return dh, d_params, dd[0], d_w_glu, d_w_out


def _rope_tables(n):
    row = jnp.arange(n, dtype=jnp.int32) - OFF
    real = row >= 0
    rid = jnp.where(real, row // GRID_W, 0).astype(f32)
    cid = jnp.where(real, row % GRID_W, 0).astype(f32)
    half = HD // 2
    inv = ROPE_THETA ** (-jnp.arange(0, half, 2, dtype=f32) / half)
    ar, ac = rid[:, None] * inv[None, :], cid[:, None] * inv[None, :]
    cos = jnp.concatenate([jnp.cos(ar), jnp.cos(ar), jnp.cos(ac), jnp.cos(ac)], axis=1)
    sin = jnp.concatenate([-jnp.sin(ar), jnp.sin(ar), -jnp.sin(ac), jnp.sin(ac)], axis=1)
    return jnp.tile(cos, (1, 2)), jnp.tile(sin, (1, 2))


def _head_mats():
    head = jnp.arange(QW, dtype=jnp.int32)[:, None] // HD == jnp.arange(128, dtype=jnp.int32)[None, :]
    return head.astype(f32) * (1.0 / HD), head.astype(f32).T


def _rot(v):
    w = v.shape[1]
    lane = lax.broadcasted_iota(jnp.int32, v.shape, 1)
    return jnp.where(lane % 32 < 16, pltpu.roll(v, w - 16, 1), pltpu.roll(v, 16, 1))


def _head_mean(v, e, et):
    w = v.shape[1]
    m = jnp.dot(v, e[:w], preferred_element_type=f32, precision=HI)
    return m, et[:, :w]


def _rms_rope(t, gain, e, et, cos, sin):
    w = t.shape[1]
    ms, spread = _head_mean(t * t, e, et)
    rs = jnp.dot(lax.rsqrt(ms + QK_EPS), spread, preferred_element_type=f32, precision=HI)
    n0 = t * rs
    n = n0 * gain
    reps = w // 128
    return n * jnp.tile(cos, (1, reps)) + _rot(n) * jnp.tile(sin, (1, reps))


def _rms_rope_bwd(dout, t, gain, e, et, cos, sin):
    w = t.shape[1]
    reps = w // 128
    ms, spread = _head_mean(t * t, e, et)
    rs = jnp.dot(lax.rsqrt(ms + QK_EPS), spread, preferred_element_type=f32, precision=HI)
    n0 = t * rs
    dn = dout * jnp.tile(cos, (1, reps)) + _rot(dout * jnp.tile(sin, (1, reps)))
    dn0 = dn * gain
    mm, _ = _head_mean(dn0 * n0, e, et)
    corr = jnp.dot(mm, spread, preferred_element_type=f32, precision=HI)
    return rs * (dn0 - n0 * corr), jnp.sum(dn * n0, axis=0, keepdims=True)


def _qk_fwd(qkv, qg, kg, e, et, cos, sin):
    def fn(i, qkv, cos, sin, qg, kg, e, et):
        q = _rms_rope(qkv[:, :QW], qg, e, et, cos, sin) * Q_SCALE
        k = _rms_rope(qkv[:, QW:QW + KW], kg, e, et, cos, sin)
        return (q, k, qkv[:, QW + KW:]), ()

    return _ew("qk_rope", fn, [qkv, cos, sin], [qg, kg, e, et], [(QW, bf16), (KW, bf16), (KW, bf16)])


def _qk_bwd(qkv, dq, dk, dv, qg, kg, e, et, cos, sin):
    def fn(i, qkv, cos, sin, dq, dk, dv, qg, kg, e, et):
        dtq, dgq = _rms_rope_bwd(dq * (HD ** -0.5), qkv[:, :QW], qg, e, et, cos, sin)
        dtk, dgk = _rms_rope_bwd(dk * math.log(2.0), qkv[:, QW:QW + KW], kg, e, et, cos, sin)
        return (jnp.concatenate([dtq, dtk, dv], axis=1),), (dgq, dgk)

    return _ew("qk_rope_bwd", fn, [qkv, cos, sin, dq, dk, dv], [qg, kg, e, et], [(QKVW, bf16)], [(1, QW), (1, KW)])


def _to_heads(a, nh):
    return a.reshape(a.shape[0], nh, HD).transpose(1, 0, 2)


def _from_heads(a):
    return a.transpose(1, 0, 2).reshape(a.shape[1], a.shape[0] * HD)


def _masked_first(s, c):
    if c:
        return s
    col = lax.broadcasted_iota(jnp.int32, (1, s.shape[1]), 1)
    return jnp.where(col >= PAD, s, NEG)


def _flash_fwd(q, k, v1, tq=ROW_TILE, tc=KEY_CHUNK):
    n = q.shape[1]
    nc = n // tc

    def body(q_ref, k_ref, v_ref, o_ref, lse_ref):
        qb = q_ref[0]

        def scores(c):
            ks = k_ref[0, pl.ds(c * tc, tc), :]
            return _masked_first(lax.dot_general(qb, ks, NT, preferred_element_type=f32), c)

        m = jnp.full((tq, 1), NEG, f32)
        acc = jnp.zeros((tq, 2 * HD), f32)
        s_next = scores(0)
        for c in range(nc):
            s = s_next
            if c + 1 < nc:
                s_next = scores(c + 1)
            m_new = jnp.maximum(m, jnp.max(s, axis=1, keepdims=True))
            p = jnp.exp2(s - m_new)
            acc = jnp.exp2(m - m_new) * acc + jnp.dot(p.astype(bf16), v_ref[0, pl.ds(c * tc, tc), :],
                                                      preferred_element_type=f32)
            m = m_new
        l = acc[:, HD:HD + 1]
        o_ref[0] = acc[:, :HD] / l
        lse_ref[0] = m + jnp.log2(l)

    return pl.pallas_call(
        body, grid=(NQ, n // tq),
        in_specs=[pl.BlockSpec((1, tq, HD), lambda h, i: (h, i, 0)),
                  pl.BlockSpec((1, n, HD), lambda h, i: (h // (NQ // NKV), 0, 0)),
                  pl.BlockSpec((1, n, 2 * HD), lambda h, i: (h // (NQ // NKV), 0, 0))],
        out_specs=[pl.BlockSpec((1, tq, HD), lambda h, i: (h, i, 0)), pl.BlockSpec((1, tq, 1), lambda h, i: (h, i, 0))],
        out_shape=[jax.ShapeDtypeStruct((NQ, n, HD), f32), jax.ShapeDtypeStruct((NQ, n, 1), f32)],
        name="flash_fwd", compiler_params=_params(("parallel", "parallel")))(q, k, v1)


def _flash_bwd(q, k, kt, v, do, lse_row, delta_row, tk=ROW_TILE, tc=KEY_CHUNK):
    n = q.shape[1]
    nc = n // tc
    grp = NQ // NKV

    def body(q_ref, do_ref, lse_ref, delta_ref, k_ref, kt_ref, v_ref, dqt_ref, dk_ref, dv_ref):
        j, g = pl.program_id(1), pl.program_id(2)
        kb, vb, ktb = k_ref[0], v_ref[0], kt_ref[0]
        valid = lax.broadcasted_iota(jnp.int32, (tk, 1), 0) + j * tk >= PAD

        @pl.when(j == 0)
        def _():
            dqt_ref[g] = jnp.zeros((HD, n), f32)

        def products(c):
            rows = pl.ds(c * tc, tc)
            return (lax.dot_general(kb, q_ref[0, rows, :], NT, preferred_element_type=f32),
                    lax.dot_general(vb, do_ref[0, rows, :], NT, preferred_element_type=f32))

        dk = jnp.zeros((tk, HD), f32)
        dv = jnp.zeros((tk, HD), f32)
        nxt = products(0)
        for c in range(nc):
            st, dpt = nxt
            if c + 1 < nc:
                nxt = products(c + 1)
            rows = pl.ds(c * tc, tc)
            pt = jnp.exp2(jnp.where(valid, st, NEG) - lse_ref[0, :, rows])
            dv = dv + jnp.dot(pt.astype(bf16), do_ref[0, rows, :], preferred_element_type=f32)
            dst = (pt * (dpt - delta_ref[0, :, rows])).astype(bf16)
            dk = dk + jnp.dot(dst, q_ref[0, rows, :], preferred_element_type=f32)
            dqt_ref[g, :, rows] += jnp.dot(ktb, dst, preferred_element_type=f32)

        @pl.when(g == 0)
        def _():
            dk_ref[0] = dk
            dv_ref[0] = dv

        @pl.when(g > 0)
        def _():
            dk_ref[0] += dk
            dv_ref[0] += dv

    hspec = pl.BlockSpec((1, n, HD), lambda h, j, g: (h * grp + g, 0, 0))
    rspec = pl.BlockSpec((1, 1, n), lambda h, j, g: (h * grp + g, 0, 0))
    kspec = pl.BlockSpec((1, tk, HD), lambda h, j, g: (h, j, 0))
    return pl.pallas_call(
        body, grid=(NKV, n // tk, grp),
        in_specs=[hspec, hspec, rspec, rspec, kspec, pl.BlockSpec((1, HD, tk), lambda h, j, g: (h, 0, j)), kspec],
        out_specs=[pl.BlockSpec((grp, HD, n), lambda h, j, g: (h, 0, 0)), kspec, kspec],
        out_shape=[jax.ShapeDtypeStruct((NQ, HD, n), f32)] + [jax.ShapeDtypeStruct((NKV, n, HD), f32)] * 2,
        name="flash_bwd", compiler_params=_params(("parallel", "arbitrary", "arbitrary")))(
            q, do, lse_row, delta_row, k, kt, v)


def _attn_fwd(h_bf, w_qkv_t, qg, kg, w_out, tabs):
    e, et, cos, sin = tabs
    qkv = _mm("attn_qkv_mm", h_bf, w_qkv_t, trans_b=True)
    q_bf, k_bf, v_bf = _qk_fwd(qkv, qg, kg, e, et, cos, sin)
    q16, k4, v4 = _to_heads(q_bf, NQ), _to_heads(k_bf, NKV), _to_heads(v_bf, NKV)
    ones = jnp.zeros((NKV, v4.shape[1], HD), bf16).at[:, :, 0].set(1.0)
    o16, lse = _flash_fwd(q16, k4, jnp.concatenate([v4, ones], axis=2))
    o = _from_heads(o16)
    o_bf = o.astype(bf16)
    mix = _mm("attn_out_mm", o_bf, w_out)
    return mix, (qkv, q16, k4, v4, o, lse, o_bf)


def _attn_bwd(dmix_bf, h_bf, saved, w_qkv_t, qg, kg, w_out, tabs):
    e, et, cos, sin = tabs
    qkv, q16, k4, v4, o, lse, o_bf = saved
    n = qkv.shape[0]
    do = _mm("attn_do_mm", dmix_bf, w_out, trans_b=True, out_dtype=bf16)
    d_w_out = _mm_tn("attn_dwout", o_bf, dmix_bf)

    def head_dots(i, do, o, e):
        return (jnp.dot(do.astype(f32) * o, e, preferred_element_type=f32, precision=HI) * HD,), ()

    delta = _ew("attn_delta", head_dots, [do, o], [e], [(128, f32)])[0]
    dqt, dk4, dv4 = _flash_bwd(q16, k4, k4.transpose(0, 2, 1), v4, _to_heads(do, NQ), lse.reshape(NQ, 1, n),
                               delta[:, :NQ].T.reshape(NQ, 1, n))
    dq = dqt.transpose(2, 0, 1).reshape(n, QW)
    dqkv_bf, dgq, dgk = _qk_bwd(qkv, dq, _from_heads(dk4), _from_heads(dv4), qg, kg, e, et, cos, sin)
    d_w_qkv_t = _mm_tn("attn_dwqkv", dqkv_bf, h_bf)
    dh = _mm("attn_dh_mm", dqkv_bf, w_qkv_t)
    return dh, d_w_qkv_t, dgq.reshape(NQ, HD).sum(0), dgk.reshape(NKV, HD).sum(0), d_w_out


def _all_gather(name, shard):
    def body(x_ref, out_ref, send_sems, recv_sems, local_sem):
        x, y, c = lax.axis_index("x"), lax.axis_index("y"), lax.axis_index("c")
        me, sibling = (x, y, c), (x, y, 1 - c)
        chips = [(1 - x, y), (x, 1 - y), (1 - x, 1 - y)]

        def slot(px, py, pc):
            return out_ref.at[4 * px + 2 * py + pc]

        def copy(k, block, to, src=None):
            return pltpu.make_async_remote_copy(
                src_ref=slot(*block) if src is None else src, dst_ref=slot(*block),
                send_sem=send_sems.at[k], recv_sem=recv_sems.at[k], device_id=to, device_id_type=MESH)

        mine = pltpu.make_async_copy(x_ref, slot(*me), local_sem)
        mine.start()
        first = [copy(0, me, sibling, src=x_ref)]
        first += [copy(1 + j, me, (*chip, c), src=x_ref) for j, chip in enumerate(chips)]
        for cp in first:
            cp.start()
        passed = [copy(4 + j, (*chip, c), sibling) for j, chip in enumerate(chips)]
        for j, chip in enumerate(chips):
            copy(1 + j, (*chip, c), me).wait_recv()
            passed[j].start()
        copy(0, sibling, me).wait_recv()
        for j, chip in enumerate(chips):
            copy(4 + j, (*chip, 1 - c), me).wait_recv()
        for cp in first + passed:
            cp.wait_send()
        mine.wait()

    return pl.pallas_call(
        body, out_shape=jax.ShapeDtypeStruct((8,) + shard.shape, shard.dtype), in_specs=[ANY], out_specs=ANY,
        scratch_shapes=[pltpu.SemaphoreType.DMA((7,)), pltpu.SemaphoreType.DMA((7,)), pltpu.SemaphoreType.DMA],
        name=name)(shard)


def _swap_sibling(name, theirs):
    k = len(theirs)

    def body(*refs):
        src, dst, send_sems, recv_sems = refs[:k], refs[k:2 * k], refs[2 * k], refs[2 * k + 1]
        x, y, c = lax.axis_index("x"), lax.axis_index("y"), lax.axis_index("c")
        copies = [pltpu.make_async_remote_copy(src_ref=src[j], dst_ref=dst[j], send_sem=send_sems.at[j],
                                               recv_sem=recv_sems.at[j], device_id=(x, y, 1 - c), device_id_type=MESH)
                  for j in range(k)]
        for cp in copies:
            cp.start()
        for cp in copies:
            cp.wait()

    return pl.pallas_call(
        body, out_shape=[jax.ShapeDtypeStruct(a.shape, a.dtype) for a in theirs], in_specs=[ANY] * k,
        out_specs=[ANY] * k, scratch_shapes=[pltpu.SemaphoreType.DMA((k,)), pltpu.SemaphoreType.DMA((k,))],
        name=name)(*theirs)


def _exchange_chips(name, parts):
    k = len(parts)

    def body(*refs):
        p_refs, t_refs = refs[:k], refs[k:2 * k]
        send_sems, recv_sems, local_sems = refs[2 * k:]
        x, y, c = lax.axis_index("x"), lax.axis_index("y"), lax.axis_index("c")
        q = 2 * x + y
        copies = []
        for j in range(k):
            copies.append(pltpu.make_async_copy(p_refs[j].at[q], t_refs[j].at[q], local_sems.at[j]))
            for hop in (1, 2, 3):
                tx, ty = x ^ (hop >> 1), y ^ (hop & 1)
                copies.append(pltpu.make_async_remote_copy(
                    src_ref=p_refs[j].at[2 * tx + ty], dst_ref=t_refs[j].at[q], send_sem=send_sems.at[3 * j + hop - 1],
                    recv_sem=recv_sems.at[3 * j + hop - 1], device_id=(tx, ty, c), device_id_type=MESH))
        for cp in copies:
            cp.start()
        for cp in copies:
            cp.wait()

    return pl.pallas_call(
        body, out_shape=[jax.ShapeDtypeStruct(a.shape, a.dtype) for a in parts], in_specs=[ANY] * k,
        out_specs=[ANY] * k,
        scratch_shapes=[pltpu.SemaphoreType.DMA((3 * k,)), pltpu.SemaphoreType.DMA((3 * k,)),
                        pltpu.SemaphoreType.DMA((k,))],
        name=name)(*parts)


def _reduce_scatter(mine, theirs):
    got = _swap_sibling("rs_sibling", list(theirs))
    parts = []
    for a, b, dt, nm in zip(mine, got, (bf16, f32), ("rs_add2", "rs_add2_small")):
        rows = 4 * a.shape[1]
        parts.append(_ew(nm, lambda i, a, b: ((a + b,), ()), [a.reshape(rows, D), b.reshape(rows, D)], [],
                         [(D, dt)], tile=RS_TILE)[0].reshape(a.shape))
    ts = _exchange_chips("rs_chips", parts)

    def add4(i, a, b, c, d):
        return ((((a.astype(f32) + b.astype(f32)) + c.astype(f32)) + d.astype(f32),), ())

    return [_ew(nm, add4, [(t, 0), (t, 1), (t, 2), (t, 3)], [], [(D, f32)], tile=RS_TILE)[0]
            for t, nm in zip(ts, ("rs_add4", "rs_add4_small"))]


def _pack_rows(parts, rows):
    flat = jnp.concatenate([p.reshape(-1) for p in parts])
    return jnp.pad(flat, (0, rows * D - flat.shape[0])).reshape(rows, D)


def _unpack(flat, shapes):
    out, off = [], 0
    for s in shapes:
        n = math.prod(s)
        out.append(flat[off:off + n].reshape(s))
        off += n
    return out


def _mat_rows(block, transposed, blk):
    a = jnp.swapaxes(block, 1, 2) if transposed else block
    a = jnp.pad(a, ((0, 0), (0, blk - a.shape[1]), (0, 0)))
    return a.reshape(-1, D)


def _mat_block(rows, transposed, blk, real):
    a = rows.reshape(-1, blk, D)[:, :real]
    return jnp.swapaxes(a, 1, 2) if transposed else a


def _mat_full(gathered, blk):
    layers = gathered.shape[1] // blk
    return gathered.reshape(8, layers, blk, D).transpose(1, 0, 2, 3).reshape(layers, 8 * blk, D)


def _vec_full(gathered):
    return gathered.transpose(1, 0, 2).reshape(gathered.shape[1], D)


def _grad_slots(full, small, cc):
    def halves(a, blk):
        a4 = a.reshape(4, 2, blk, D)
        return [lax.dynamic_index_in_dim(a4, sel, axis=1, keepdims=False) for sel in (cc, 1 - cc)]

    mine, theirs = [], []
    for name, _, blk, _ in MATS:
        for layer in full[name]:
            a, b = halves(layer, blk)
            mine.append(a)
            theirs.append(b)
    a, b = halves(small, REP_PIECE)
    return (jnp.concatenate(mine, axis=1), a), (jnp.concatenate(theirs, axis=1), b)


def _local_step(x0, target0, w, fw):
    seq = x0.shape[0]
    n = OFF + seq
    movers = _block_movers()
    h = jnp.concatenate([jnp.zeros((PAD, D), f32), fw['meta_tokens'], x0], axis=0)
    h_bf = h.astype(bf16)
    tabs = _head_mats() + _rope_tables(n)
    qg = [jnp.tile(w['attn_q_gain'][j], NQ)[None, :] for j in range(2)]
    kg = [jnp.tile(w['attn_k_gain'][j], NKV)[None, :] for j in range(2)]
    s5_names = ['s5_lambda_re', 's5_lambda_im', 's5_log_dt', 's5_b_re', 's5_b_im', 's5_c_re', 's5_c_im']
    s5_mats, s5_vjp = [], []
    for j in range(2):
        mats, vjp = jax.vjp(_s5_mats, *[w[k][j] for k in s5_names])
        s5_mats.append(mats)
        s5_vjp.append(vjp)
    saved = []
    for i in range(DEPTH):
        j = i // 2
        if i % 2 == 0:
            mix, sv = _s5_fwd(h, movers, s5_mats[j], w['s5_d'][j][None, :], fw['s5_w_glu'][j], fw['s5_w_out'][j])
        else:
            mix, sv = _attn_fwd(h_bf, fw['attn_w_qkv'][j], qg[j], kg[j], fw['attn_w_out'][j], tabs)
        r1, h1, h1_bf = _ln_fwd(h, mix, fw['ln_gain'][i, 0][None, :], fw['ln_bias'][i, 0][None, :])
        gate, up, act = _ffn_up(h1_bf, fw['ffn_w_gate'][i], fw['ffn_w_up'][i])
        f = _mm("ffn_down_mm", act, fw['ffn_w_down'][i])
        r2, h2, h2_bf = _ln_fwd(h1, f, fw['ln_gain'][i, 1][None, :], fw['ln_bias'][i, 1][None, :])
        saved.append((h, h_bf, sv, r1, h1_bf, gate, up, act, r2))
        h, h_bf = h2, h2_bf

    d_b, sq = _loss_grad(h, target0)
    loss = 0.5 * jnp.sum(sq) * (1.0 / D)

    grads = {k: [None] * (DEPTH if k.startswith('ffn') else 2) for k in WEIGHTS}
    d_ln_gain = [[None, None] for _ in range(DEPTH)]
    d_ln_bias = [[None, None] for _ in range(DEPTH)]
    d_a = None
    for i in reversed(range(DEPTH)):
        j = i // 2
        h_in, h_in_bf, sv, r1, h1_bf, gate, up, act, r2 = saved[i]
        dr2, dr2_bf, dg, db = _ln_bwd(d_a, d_b, r2, fw['ln_gain'][i, 1][None, :])
        d_ln_gain[i][1], d_ln_bias[i][1] = dg[0], db[0]
        dgate, dup = _ffn_dup(dr2_bf, fw['ffn_w_down'][i], gate, up)
        grads['ffn_w_down'][i] = _mm_tn("ffn_dwdown", act, dr2_bf, tk=DFFP // 2)
        grads['ffn_w_gate'][i] = _mm_tn("ffn_dwgate", dgate, h1_bf, tk=DFFP // 2)
        grads['ffn_w_up'][i] = _mm_tn("ffn_dwup", dup, h1_bf, tk=DFFP // 2)
        dh1 = _mm2("ffn_dh_mm", dgate, fw['ffn_w_gate'][i], dup, fw['ffn_w_up'][i])
        dr1, dr1_bf, dg, db = _ln_bwd(dr2, dh1, r1, fw['ln_gain'][i, 0][None, :])
        d_ln_gain[i][0], d_ln_bias[i][0] = dg[0], db[0]
        if i % 2 == 0:
            dh, d_par, dd, d_w_glu, d_w_out = _s5_bwd(dr1_bf, h_in, movers, sv, s5_mats[j], s5_vjp[j],
                                                      w['s5_d'][j][None, :], fw['s5_w_glu'][j], fw['s5_w_out'][j])
            for k, g in zip(s5_names, d_par):
                grads[k][j] = g
            grads['s5_d'][j], grads['s5_w_glu'][j], grads['s5_w_out'][j] = dd, d_w_glu, d_w_out
        else:
            dh, d_w_qkv, dgq, dgk, d_w_out = _attn_bwd(dr1_bf, h_in_bf, sv, fw['attn_w_qkv'][j], qg[j], kg[j],
                                                       fw['attn_w_out'][j], tabs)
            grads['attn_w_qkv'][j], grads['attn_w_out'][j] = d_w_qkv, d_w_out
            grads['attn_q_gain'][j], grads['attn_k_gain'][j] = dgq, dgk
        d_a, d_b = dr1, dh
    dh0 = _ew("dh0", lambda i, a, b: ((ALPHA * a + b,), ()), [d_a, d_b], [], [(D, f32)])[0]
    mats = {m[0] for m in MATS}
    full = {k: (v if k in mats else jnp.stack(v)) for k, v in grads.items() if v[0] is not None}
    full['meta_tokens'] = dh0[PAD:OFF]
    full['ln_gain'] = jnp.stack([jnp.stack(r) for r in d_ln_gain])
    full['ln_bias'] = jnp.stack([jnp.stack(r) for r in d_ln_bias])

    return loss, dh0[OFF:], full


def kernel(x, meta_tokens, s5_lambda_re, s5_lambda_im, s5_log_dt, s5_b_re, s5_b_im, s5_c_re, s5_c_im, s5_d, s5_w_glu, s5_w_out, attn_w_qkv, attn_q_gain, attn_k_gain, attn_w_out, ffn_w_gate, ffn_w_up, ffn_w_down, ln_gain, ln_bias, loss_target, m_meta_tokens, m_s5_lambda_re, m_s5_lambda_im, m_s5_log_dt, m_s5_b_re, m_s5_b_im, m_s5_c_re, m_s5_c_im, m_s5_d, m_s5_w_glu, m_s5_w_out, m_attn_w_qkv, m_attn_q_gain, m_attn_k_gain, m_attn_w_out, m_ffn_w_gate, m_ffn_w_up, m_ffn_w_down, m_ln_gain, m_ln_bias, v_meta_tokens, v_s5_lambda_re, v_s5_lambda_im, v_s5_log_dt, v_s5_b_re, v_s5_b_im, v_s5_c_re, v_s5_c_im, v_s5_d, v_s5_w_glu, v_s5_w_out, v_attn_w_qkv, v_attn_q_gain, v_attn_k_gain, v_attn_w_out, v_ffn_w_gate, v_ffn_w_up, v_ffn_w_down, v_ln_gain, v_ln_bias):
    w = dict(zip(WEIGHTS, (meta_tokens, s5_lambda_re, s5_lambda_im, s5_log_dt, s5_b_re, s5_b_im, s5_c_re, s5_c_im, s5_d, s5_w_glu, s5_w_out, attn_w_qkv, attn_q_gain, attn_k_gain, attn_w_out, ffn_w_gate, ffn_w_up, ffn_w_down, ln_gain, ln_bias)))
    mom = dict(zip(WEIGHTS, (m_meta_tokens, m_s5_lambda_re, m_s5_lambda_im, m_s5_log_dt, m_s5_b_re, m_s5_b_im, m_s5_c_re, m_s5_c_im, m_s5_d, m_s5_w_glu, m_s5_w_out, m_attn_w_qkv, m_attn_q_gain, m_attn_k_gain, m_attn_w_out, m_ffn_w_gate, m_ffn_w_up, m_ffn_w_down, m_ln_gain, m_ln_bias)))
    vel = dict(zip(WEIGHTS, (v_meta_tokens, v_s5_lambda_re, v_s5_lambda_im, v_s5_log_dt, v_s5_b_re, v_s5_b_im, v_s5_c_re, v_s5_c_im, v_s5_d, v_s5_w_glu, v_s5_w_out, v_attn_w_qkv, v_attn_q_gain, v_attn_k_gain, v_attn_w_out, v_ffn_w_gate, v_ffn_w_up, v_ffn_w_down, v_ln_gain, v_ln_bias)))
    cc = lax.axis_index("c")
    dev = 4 * lax.axis_index("x") + 2 * lax.axis_index("y") + cc

    mat_rows = jnp.concatenate([_mat_rows(w[n], t, blk) for n, t, blk, _ in MATS]).astype(bf16)
    g_mats = _all_gather("ag_weights", mat_rows)
    g_vecs = _all_gather("ag_vectors", jnp.concatenate([w[n].reshape(-1, 128) for n in VECS]))
    fw, off = {}, 0
    for n, _, blk, _ in MATS:
        rows = w[n].shape[0] * blk
        fw[n] = _mat_full(g_mats[:, off:off + rows], blk)
        off += rows
    off = 0
    for n in VECS:
        rows = w[n].size // 128
        fw[n] = _vec_full(g_vecs[:, off:off + rows]).reshape(w[n].shape[:-1] + (D,))
        off += rows

    loss, grad_x, full = _local_step(x[0], loss_target[0], w, fw)
    loss = lax.psum(loss, AXES)
    grad_x = grad_x[None]

    small_names = REPL + VECS
    mine, theirs = _grad_slots(full, _pack_rows([full[k] for k in small_names], REP_ROWS), cc)
    red, red_small = _reduce_scatter(mine, theirs)
    small_all = _all_gather("ag_small_grads", red_small).reshape(REP_ROWS * D)
    g, off = {}, 0
    for n, t, blk, real in MATS:
        rows = w[n].shape[0] * blk
        g[n] = _mat_block(red[off:off + rows], t, blk, real)
        off += rows
    small = dict(zip(small_names, _unpack(small_all, [full[k].shape for k in small_names])))
    for k in REPL:
        g[k] = small[k]
    for k in VECS:
        g[k] = lax.dynamic_slice_in_dim(small[k], dev * 128, 128, axis=small[k].ndim - 1)

    delta, new_m, new_v = {}, {}, {}
    for n, _, _, _ in MATS:
        shp = w[n].shape
        res = _adamw(*[d[n].reshape(-1, shp[-1]) for d in (w, g, mom, vel)])
        delta[n], new_m[n], new_v[n] = [a.reshape(shp) for a in res]
    shapes = [w[k].shape for k in small_names]
    res = _adamw(*[_pack_rows([d[k] for k in small_names], SMALL_ROWS) for d in (w, g, mom, vel)])
    for out, a in zip((delta, new_m, new_v), res):
        out.update(zip(small_names, _unpack(a.reshape(-1), shapes)))
    return (loss, grad_x, *[g[k] for k in WEIGHTS], *[delta[k] for k in WEIGHTS],
            *[new_m[k] for k in WEIGHTS], *[new_v[k] for k in WEIGHTS])
```

```python
import functools
import math

import jax
import jax.numpy as jnp
from jax import lax
from jax.experimental import pallas as pl
from jax.experimental.pallas import tpu as pltpu

f32 = jnp.float32
bf16 = jnp.bfloat16
HI = lax.Precision.HIGHEST
HIGH = lax.Precision.HIGH
MESH = pl.DeviceIdType.MESH
AXES = ("x", "y", "c")
ANY = pl.BlockSpec(memory_space=pl.ANY)

D = 1024
DEPTH = 4
N_META = 16
PAD = 240
OFF = PAD + N_META
ROW_TILE = 768
KEY_CHUNK = 256
FFN_TILE = 256
GRID_W = 64
HD = 64
NQ = 16
NKV = 4
QW = NQ * HD
KW = NKV * HD
QKVW = QW + 2 * KW
DFF = 2816
GROUPS = 64
GCH = 16
NSTATE = 64
CHUNK = 16
GB = 8
ROPE_THETA = 10000.0
LN_EPS = 1e-5
QK_EPS = 1e-6
ALPHA = (2.0 * DEPTH) ** 0.25
ADAM_LR, ADAM_B1, ADAM_B2, ADAM_EPS, ADAM_WD, ADAM_STEP = 0.001, 0.9, 0.999, 1e-08, 0.01, 10
NEG = -1e30
Q_SCALE = HD ** -0.5 * math.log2(math.e)
VMEM_MB = 56

NT = (((1,), (1,)), ((), ()))
TN = (((0,), (0,)), ((), ()))

WEIGHTS = ['meta_tokens', 's5_lambda_re', 's5_lambda_im', 's5_log_dt', 's5_b_re', 's5_b_im', 's5_c_re', 's5_c_im',
           's5_d', 's5_w_glu', 's5_w_out', 'attn_w_qkv', 'attn_q_gain', 'attn_k_gain', 'attn_w_out', 'ffn_w_gate',
           'ffn_w_up', 'ffn_w_down', 'ln_gain', 'ln_bias']
DFFP = 3072
FF_BLK, FF_BLKP = DFF // 8, DFFP // 8
MATS = [('s5_w_glu', False, 128, 128), ('s5_w_out', False, 128, 128), ('attn_w_qkv', True, 192, 192),
        ('attn_w_out', False, 128, 128), ('ffn_w_gate', True, FF_BLKP, FF_BLK), ('ffn_w_up', True, FF_BLKP, FF_BLK),
        ('ffn_w_down', False, FF_BLKP, FF_BLK)]
VECS = ['meta_tokens', 'ln_gain', 'ln_bias']
REPL = ['s5_lambda_re', 's5_lambda_im', 's5_log_dt', 's5_b_re', 's5_b_im', 's5_c_re', 's5_c_im', 's5_d',
        'attn_q_gain', 'attn_k_gain']
MAT_ROWS = 5760
REP_PIECE = 160
REP_ROWS = 8 * REP_PIECE
RS_TILE = 640
SMALL_ROWS = 1088


def _params(sem, mb=VMEM_MB):
    return pltpu.CompilerParams(dimension_semantics=sem, vmem_limit_bytes=mb << 20)


def _ew(name, fn, rows, consts, outs, accs=(), tile=ROW_TILE):
    first = rows[0][0] if isinstance(rows[0], tuple) else rows[0]
    n = first.shape[-2]
    tile = min(tile, n)
    assert n % tile == 0, (name, n, tile)
    n_in, n_o, n_a = len(rows) + len(consts), len(outs), len(accs)

    def body(*refs):
        i = pl.program_id(0)
        res_o, res_a = fn(i, *[r[...] for r in refs[:n_in]])
        for r, val in zip(refs[n_in:n_in + n_o], res_o):
            r[...] = val.astype(r.dtype)
        if n_a:
            a_refs = refs[n_in + n_o:]

            @pl.when(i == 0)
            def _():
                for r in a_refs:
                    r[...] = jnp.zeros(r.shape, r.dtype)

            for r, val in zip(a_refs, res_a):
                r[...] += val

    in_specs, args = [], []
    for a in rows:
        if isinstance(a, tuple):
            arr, k = a
            in_specs.append(pl.BlockSpec((None, tile, arr.shape[2]), functools.partial(lambda i, k: (k, i, 0), k=k)))
            args.append(arr)
        else:
            in_specs.append(pl.BlockSpec((tile, a.shape[1]), lambda i: (i, 0)))
            args.append(a)
    for c in consts:
        in_specs.append(pl.BlockSpec(c.shape, lambda i: (0, 0)))
        args.append(c)
    out_specs = [pl.BlockSpec((tile, c), lambda i: (i, 0)) for c, _ in outs]
    out_specs += [pl.BlockSpec(s, lambda i: (0, 0)) for s in accs]
    out_shape = [jax.ShapeDtypeStruct((n, c), dt) for c, dt in outs]
    out_shape += [jax.ShapeDtypeStruct(s, f32) for s in accs]
    res = pl.pallas_call(body, grid=(n // tile,), in_specs=in_specs, out_specs=out_specs, out_shape=out_shape,
                         name=name, compiler_params=_params(("arbitrary",)))(*args)
    return res


def _mm(name, a, b, trans_b=False, out_dtype=f32, tm=ROW_TILE):
    m, k = a.shape
    spec, b, shape = _whole(b)
    n = shape[0] if trans_b else shape[1]
    tm = min(tm, m)
    assert m % tm == 0
    dims = NT if trans_b else (((1,), (0,)), ((), ()))

    def body(a_ref, b_ref, o_ref):
        o_ref[...] = lax.dot_general(a_ref[...], b_ref[...], dims, preferred_element_type=f32).astype(o_ref.dtype)

    return pl.pallas_call(
        body, grid=(m // tm,), in_specs=[pl.BlockSpec((tm, k), lambda i: (i, 0)), spec],
        out_specs=pl.BlockSpec((tm, n), lambda i: (i, 0)),
        out_shape=jax.ShapeDtypeStruct((m, n), out_dtype), name=name, compiler_params=_params(("parallel",)))(a, b)


def _whole(b):
    if isinstance(b, tuple):
        arr, layer = b
        return pl.BlockSpec((None,) + arr.shape[1:], lambda i: (layer, 0, 0)), arr, arr.shape[1:]
    return pl.BlockSpec(b.shape, lambda i: (0, 0)), b, b.shape


def _mm2(name, a1, b1, a2, b2, out_dtype=f32, tm=ROW_TILE // 2):
    m, k = a1.shape
    spec1, b1, shape = _whole(b1)
    spec2, b2, _ = _whole(b2)
    n = shape[1]
    tm = min(tm, m)
    assert m % tm == 0

    def body(a1_ref, b1_ref, a2_ref, b2_ref, o_ref):
        acc = jnp.dot(a1_ref[...], b1_ref[...], preferred_element_type=f32)
        acc += jnp.dot(a2_ref[...], b2_ref[...], preferred_element_type=f32)
        o_ref[...] = acc.astype(o_ref.dtype)

    row = pl.BlockSpec((tm, k), lambda i: (i, 0))
    return pl.pallas_call(
        body, grid=(m // tm,), in_specs=[row, spec1, row, spec2], out_specs=pl.BlockSpec((tm, n), lambda i: (i, 0)),
        out_shape=jax.ShapeDtypeStruct((m, n), out_dtype), name=name,
        compiler_params=_params(("parallel",)))(a1, b1, a2, b2)


def _mm_tn(name, a, g, tk=512, tl=ROW_TILE):
    rows, k1 = a.shape
    n = g.shape[1]
    tl = min(tl, rows)
    assert rows % tl == 0 and k1 % tk == 0

    def body(a_ref, g_ref, o_ref):
        @pl.when(pl.program_id(1) == 0)
        def _():
            o_ref[...] = jnp.zeros(o_ref.shape, f32)

        o_ref[...] += lax.dot_general(a_ref[...], g_ref[...], TN, preferred_element_type=f32)

    return pl.pallas_call(
        body, grid=(k1 // tk, rows // tl),
        in_specs=[pl.BlockSpec((tl, tk), lambda k, l: (l, k)), pl.BlockSpec((tl, n), lambda k, l: (l, 0))],
        out_specs=pl.BlockSpec((tk, n), lambda k, l: (k, 0)),
        out_shape=jax.ShapeDtypeStruct((k1, n), f32), name=name,
        compiler_params=_params(("parallel", "arbitrary")))(a, g)


def _ln_stats(r):
    mean = jnp.mean(r, axis=-1, keepdims=True)
    c = r - mean
    rstd = lax.rsqrt(jnp.mean(c * c, axis=-1, keepdims=True) + LN_EPS)
    return c * rstd, rstd


def _ln_fwd(h, mix, gain, bias):
    def fn(i, h, mix, g, b):
        r = ALPHA * h + mix
        y = _ln_stats(r)[0] * g + b
        return (r, y, y), ()

    return _ew("ln_fwd", fn, [h, mix], [gain, bias], [(D, f32), (D, f32), (D, bf16)])


def _ln_bwd(d_a, d_b, r, gain):
    def core(dout, r, g):
        xhat, rstd = _ln_stats(r)
        dxh = dout * g
        dr = rstd * (dxh - jnp.mean(dxh, axis=-1, keepdims=True) - xhat * jnp.mean(dxh * xhat, axis=-1, keepdims=True))
        return (dr, dr), (jnp.sum(dout * xhat, axis=0, keepdims=True), jnp.sum(dout, axis=0, keepdims=True))

    outs, accs = [(D, f32), (D, bf16)], [(1, D), (1, D)]
    if d_a is None:
        return _ew("ln_bwd_top", lambda i, d, r, g: core(d, r, g), [d_b, r], [gain], outs, accs)
    return _ew("ln_bwd", lambda i, da, db, r, g: core(ALPHA * da + db, r, g), [d_a, d_b, r], [gain], outs, accs)


def _sigmoid(x):
    return 1.0 / (1.0 + jnp.exp(-x))


def _ffn_up(h_bf, w_gate_t, w_up_t, tm=FFN_TILE):
    m, k = h_bf.shape
    gspec, w_gate_t, (n, _) = _whole(w_gate_t)
    uspec, w_up_t, _ = _whole(w_up_t)

    def body(h_ref, wg_ref, wu_ref, g_ref, u_ref, a_ref):
        h = h_ref[...]
        g = lax.dot_general(h, wg_ref[...], NT, preferred_element_type=f32).astype(bf16)
        u = lax.dot_general(h, wu_ref[...], NT, preferred_element_type=f32).astype(bf16)
        g_ref[...] = g
        u_ref[...] = u
        g = g.astype(f32)
        a_ref[...] = (g * _sigmoid(g) * u.astype(f32)).astype(bf16)

    row = pl.BlockSpec((tm, n), lambda i: (i, 0))
    return pl.pallas_call(
        body, grid=(m // tm,), in_specs=[pl.BlockSpec((tm, k), lambda i: (i, 0)), gspec, uspec],
        out_specs=[row, row, row], out_shape=[jax.ShapeDtypeStruct((m, n), bf16)] * 3, name="ffn_up",
        compiler_params=_params(("parallel",)))(h_bf, w_gate_t, w_up_t)


def _ffn_dup(df_bf, w_down, gate, up, tm=FFN_TILE):
    m, k = df_bf.shape
    wspec, w_down, (n, _) = _whole(w_down)

    def body(d_ref, w_ref, g_ref, u_ref, dg_ref, du_ref):
        da = lax.dot_general(d_ref[...], w_ref[...], NT, preferred_element_type=f32).astype(bf16).astype(f32)
        g, u = g_ref[...].astype(f32), u_ref[...].astype(f32)
        s = _sigmoid(g)
        dg_ref[...] = (da * u * s * (1.0 + g * (1.0 - s))).astype(bf16)
        du_ref[...] = (da * g * s).astype(bf16)

    row = pl.BlockSpec((tm, n), lambda i: (i, 0))
    return pl.pallas_call(
        body, grid=(m // tm,),
        in_specs=[pl.BlockSpec((tm, k), lambda i: (i, 0)), wspec, row, row],
        out_specs=[row, row], out_shape=[jax.ShapeDtypeStruct((m, n), bf16)] * 2, name="ffn_dup",
        compiler_params=_params(("parallel",)))(df_bf, w_down, gate, up)


def _loss_grad(h, target):
    n = h.shape[0]

    def body(h_ref, t_ref, d_ref, sq_ref):
        i = pl.program_id(0)

        @pl.when(i == 0)
        def _():
            d_ref[...] = jnp.zeros(d_ref.shape, f32)
            sq_ref[...] = jnp.zeros(sq_ref.shape, f32)

        @pl.when(i > 0)
        def _():
            e = h_ref[...] - t_ref[...]
            d_ref[...] = e * (1.0 / D)
            sq_ref[...] += jnp.sum(e * e, axis=0, keepdims=True)

    return pl.pallas_call(
        body, grid=(n // OFF,),
        in_specs=[pl.BlockSpec((OFF, D), lambda i: (i, 0)), pl.BlockSpec((OFF, D), lambda i: (jnp.maximum(i - 1, 0), 0))],
        out_specs=[pl.BlockSpec((OFF, D), lambda i: (i, 0)), pl.BlockSpec((1, D), lambda i: (0, 0))],
        out_shape=[jax.ShapeDtypeStruct((n, D), f32), jax.ShapeDtypeStruct((1, D), f32)], name="loss",
        compiler_params=_params(("arbitrary",)))(h, target)


def _adamw(w, g, m, v):
    def fn(i, w, g, m, v):
        m = ADAM_B1 * m + (1.0 - ADAM_B1) * g
        v = ADAM_B2 * v + (1.0 - ADAM_B2) * jnp.square(g)
        m_hat = m / (1.0 - ADAM_B1 ** ADAM_STEP)
        v_hat = v / (1.0 - ADAM_B2 ** ADAM_STEP)
        delta = -ADAM_LR * (m_hat / (jnp.sqrt(v_hat) + ADAM_EPS) + ADAM_WD * w)
        return (delta, m, v), ()

    rows, cols = w.shape
    tile = max(t for t in range(8, min(rows, 544) + 1, 8) if rows % t == 0)
    return _ew("adamw", fn, [w, g, m, v], [], [(cols, f32)] * 3, tile=tile)


def _s5_mats(lam_re, lam_im, log_dt, b_re, b_im, c_re, c_im):
    steps = jnp.arange(CHUNK + 1, dtype=f32)
    n = CHUNK * GCH
    last = n - GCH

    def one(lr, li, ldt, br, bi, cr, ci, reverse):
        dt = jnp.exp(ldt)[:, None]
        mag = jnp.exp(lr * dt)
        abr, abi = mag * jnp.cos(li * dt), mag * jnp.sin(li * dt)
        nr, ni = abr - 1.0, abi
        den = lr * lr + li * li
        zr, zi = (nr * lr + ni * li) / den, (ni * lr - nr * li) / den
        bbr = zr[..., None] * br - zi[..., None] * bi
        bbi = zr[..., None] * bi + zi[..., None] * br
        pmag = jnp.exp((lr * dt)[..., None] * steps)
        pang = (li * dt)[..., None] * steps
        pr, pi = pmag * jnp.cos(pang), pmag * jnp.sin(pang)
        crt, cit = jnp.swapaxes(cr, 1, 2)[:, :, None, :], jnp.swapaxes(ci, 1, 2)[:, :, None, :]
        car = crt * pr[..., None] - cit * pi[..., None]
        cai = crt * pi[..., None] + cit * pr[..., None]
        if reverse:
            taps = slice(CHUNK - 1, None, -1)
            outs = slice(CHUNK, 0, -1)
            ins = slice(0, CHUNK)
        else:
            taps, outs, ins = slice(0, CHUNK), slice(1, CHUNK + 1), slice(CHUNK - 1, None, -1)
        kern = (jnp.einsum('gpi,gpq->giq', bbr, car[:, :, taps].reshape(GROUPS, NSTATE, n), precision=HI)
                - jnp.einsum('gpi,gpq->giq', bbi, cai[:, :, taps].reshape(GROUPS, NSTATE, n), precision=HI))
        wide = jnp.pad(kern, ((0, 0), (0, 0), (0, last) if reverse else (last, 0)))
        m = jnp.stack([wide[:, :, last - GCH * t:last - GCH * t + n] for t in range(CHUNK)], axis=1)
        qr = jnp.swapaxes(pr[:, :, ins], 1, 2)[:, :, None, :]
        qi = jnp.swapaxes(pi[:, :, ins], 1, 2)[:, :, None, :]
        bbrt, bbit = jnp.swapaxes(bbr, 1, 2)[:, None], jnp.swapaxes(bbi, 1, 2)[:, None]
        pin = jnp.concatenate([qr * bbrt - qi * bbit, qr * bbit + qi * bbrt], axis=-1)
        pout = jnp.concatenate([car[:, :, outs].reshape(GROUPS, NSTATE, n),
                                -cai[:, :, outs].reshape(GROUPS, NSTATE, n)], axis=1)
        return (m.reshape(GROUPS, n, n), pin.reshape(GROUPS, n, 2 * NSTATE), pout, pr[:, :, CHUNK], pi[:, :, CHUNK])

    mf, pinf, poutf, arf, aif = one(lam_re[0], lam_im[0], log_dt[0], b_re[0], b_im[0], c_re[0], c_im[0], False)
    mr, pinr, poutr, arr, air = one(lam_re[1], lam_im[1], log_dt[1], b_re[1], b_im[1], c_re[1], c_im[1], True)
    return (mf + mr, jnp.concatenate([pinf, pinr], 2), jnp.concatenate([poutf, poutr], 1),
            jnp.stack([arf, arr]), jnp.stack([aif, air]))


def _s5_coefs(a_re, a_im):
    c1 = jnp.concatenate([a_re, a_re], -1)
    c2 = jnp.concatenate([-a_im, a_im], -1)
    return tuple(c.reshape(GROUPS // GB, 1, GB * 2 * NSTATE) for c in (c1[0], c2[0], c1[1], c2[1]))


def _swap(s):
    w = s.shape[1]
    lane = lax.broadcasted_iota(jnp.int32, s.shape, 1)
    return jnp.where(lane % (2 * NSTATE) < NSTATE, pltpu.roll(s, w - NSTATE, 1), pltpu.roll(s, NSTATE, 1))


def _group_lanes(g):
    return slice(g * 2 * NSTATE, (g + 1) * 2 * NSTATE)


def _s5_states(nc, u_ref, pin_ref, coef, vf, vr, wf, wr, sf, sr):
    c1f, c2f, c1r, c2r = coef
    for g in range(GB):
        v = jnp.dot(u_ref[g], pin_ref[g], preferred_element_type=f32)
        vf[:, _group_lanes(g)] = v[:, :2 * NSTATE]
        vr[:, _group_lanes(g)] = v[:, 2 * NSTATE:]
    wf[...] = _swap(vf[...])
    wr[...] = _swap(vr[...])

    def step(i, carry):
        s_f, t_f, s_r, t_r = carry
        kf, kr = pl.ds(i, 1), pl.ds(nc - 1 - i, 1)
        sf[kf, :] = s_f
        sr[kr, :] = s_r
        s_f, t_f = c1f * s_f + c2f * t_f + vf[kf, :], c1f * t_f - c2f * s_f + wf[kf, :]
        s_r, t_r = c1r * s_r + c2r * t_r + vr[kr, :], c1r * t_r - c2r * s_r + wr[kr, :]
        return s_f, t_f, s_r, t_r

    z = jnp.zeros((1, GB * 2 * NSTATE), f32)
    lax.fori_loop(0, nc, step, (z, z, z, z))


def _s5_core_fwd(ug, msum, pin, pout, coefs):
    nc = ug.shape[1]
    n = CHUNK * GCH

    def body(u_ref, m_ref, pin_ref, pout_ref, c1f, c2f, c1r, c2r, y_ref, vf, vr, wf, wr, sf, sr):
        coef = (c1f[...], c2f[...], c1r[...], c2r[...])
        _s5_states(nc, u_ref, pin_ref, coef, vf, vr, wf, wr, sf, sr)
        for g in range(GB):
            s_in = jnp.concatenate([sf[:, _group_lanes(g)], sr[:, _group_lanes(g)]], axis=1).astype(bf16)
            y_ref[g] = (jnp.dot(u_ref[g], m_ref[g], preferred_element_type=f32)
                        + jnp.dot(s_in, pout_ref[g], preferred_element_type=f32)).astype(bf16)

    seq = pl.BlockSpec((GB, nc, n), lambda i: (i, 0, 0))
    mat = pl.BlockSpec((GB, n, n), lambda i: (i, 0, 0))
    cf = pl.BlockSpec((None, 1, GB * 2 * NSTATE), lambda i: (i, 0, 0))
    scr = pltpu.VMEM((nc, GB * 2 * NSTATE), f32)
    return pl.pallas_call(
        body, grid=(GROUPS // GB,), in_specs=[seq, mat, mat, mat, cf, cf, cf, cf], out_specs=seq,
        out_shape=jax.ShapeDtypeStruct((GROUPS, nc, n), bf16), scratch_shapes=[scr] * 6,
        name="s5_core_fwd", compiler_params=_params(("parallel",)))(ug, msum, pin, pout, *coefs)


def _s5_core_bwd(ug, dyg, msum, pin, pout, coefs):
    nc = ug.shape[1]
    n = CHUNK * GCH

    def body(u_ref, dy_ref, m_ref, pin_ref, pout_ref, c1f, c2f, c1r, c2r,
             du_ref, dm_ref, dpin_ref, dpout_ref, a1f_ref, a2f_ref, a1r_ref, a2r_ref, vf, vr, wf, wr, sf, sr):
        coef = (c1f[...], c2f[...], c1r[...], c2r[...])
        _s5_states(nc, u_ref, pin_ref, coef, vf, vr, wf, wr, sf, sr)
        for g in range(GB):
            s_in = jnp.concatenate([sf[:, _group_lanes(g)], sr[:, _group_lanes(g)]], axis=1).astype(bf16)
            dy = dy_ref[g]
            ds = lax.dot_general(dy, pout_ref[g], NT, preferred_element_type=f32)
            vf[:, _group_lanes(g)] = ds[:, :2 * NSTATE]
            vr[:, _group_lanes(g)] = ds[:, 2 * NSTATE:]
            dpout_ref[g] = lax.dot_general(s_in, dy, TN, preferred_element_type=f32)
            dm_ref[g] = lax.dot_general(u_ref[g], dy, TN, preferred_element_type=f32)

        wf[...] = _swap(vf[...])
        wr[...] = _swap(vr[...])
        k1f, k2f, k1r, k2r = coef[0], -coef[1], coef[2], -coef[3]

        def step(i, carry):
            g_f, h_f, g_r, h_r, a1f, b2f, a1r, b2r = carry
            kf, kr = pl.ds(nc - 1 - i, 1), pl.ds(i, 1)
            s_f, s_r = sf[kf, :], sr[kr, :]
            sf[kf, :] = g_f
            sr[kr, :] = g_r
            a1f, b2f = a1f + g_f * s_f, b2f + h_f * s_f
            a1r, b2r = a1r + g_r * s_r, b2r + h_r * s_r
            g_f, h_f = vf[kf, :] + k1f * g_f + k2f * h_f, wf[kf, :] + k1f * h_f - k2f * g_f
            g_r, h_r = vr[kr, :] + k1r * g_r + k2r * h_r, wr[kr, :] + k1r * h_r - k2r * g_r
            return g_f, h_f, g_r, h_r, a1f, b2f, a1r, b2r

        z = jnp.zeros((1, GB * 2 * NSTATE), f32)
        _, _, _, _, a1f, b2f, a1r, b2r = lax.fori_loop(0, nc, step, (z,) * 8)
        a1f_ref[...], a2f_ref[...], a1r_ref[...], a2r_ref[...] = a1f, _swap(b2f), a1r, _swap(b2r)
        for g in range(GB):
            dv = jnp.concatenate([sf[:, _group_lanes(g)], sr[:, _group_lanes(g)]], axis=1).astype(bf16)
            du_ref[g] = (lax.dot_general(dy_ref[g], m_ref[g], NT, preferred_element_type=f32)
                         + lax.dot_general(dv, pin_ref[g], NT, preferred_element_type=f32)).astype(bf16)
            dpin_ref[g] = lax.dot_general(u_ref[g], dv, TN, preferred_element_type=f32)

    seq = pl.BlockSpec((GB, nc, n), lambda i: (i, 0, 0))
    mat = pl.BlockSpec((GB, n, n), lambda i: (i, 0, 0))
    cf = pl.BlockSpec((None, 1, GB * 2 * NSTATE), lambda i: (i, 0, 0))
    scr = pltpu.VMEM((nc, GB * 2 * NSTATE), f32)
    mat_s = jax.ShapeDtypeStruct((GROUPS, n, n), f32)
    cf_s = jax.ShapeDtypeStruct((GROUPS // GB, 1, GB * 2 * NSTATE), f32)
    return pl.pallas_call(
        body, grid=(GROUPS // GB,), in_specs=[seq, seq, mat, mat, mat, cf, cf, cf, cf],
        out_specs=[seq, mat, mat, mat, cf, cf, cf, cf],
        out_shape=[jax.ShapeDtypeStruct((GROUPS, nc, n), bf16), mat_s, mat_s, mat_s, cf_s, cf_s, cf_s, cf_s],
        scratch_shapes=[scr] * 6, name="s5_core_bwd",
        compiler_params=_params(("parallel",)))(ug, dyg, msum, pin, pout, *coefs)


def _block_movers():
    a_in, l_in = jnp.divmod(jnp.arange(GB * 128, dtype=jnp.int32), 128)
    a_out, c_out = jnp.divmod(jnp.arange(128, dtype=jnp.int32), GCH)
    j = jnp.arange(GB, dtype=jnp.int32)[:, None, None]
    hit = (a_in[None, :, None] == a_out[None, None, :]) & (l_in[None, :, None] == GCH * j + c_out[None, None, :])
    return hit.astype(bf16)


def _to_groups(x, movers, mask):
    n = x.shape[0]
    nc = n // CHUNK
    half = CHUNK // 2

    def body(x_ref, mv_ref, o_ref):
        keep = lax.broadcasted_iota(jnp.int32, (nc, 1), 0) >= PAD // CHUNK
        steps = [x_ref[pl.ds(t, nc, stride=CHUNK), :] for t in range(CHUNK)]
        if mask:
            steps = [jnp.where(keep, s, 0.0) for s in steps]
        lo = jnp.concatenate(steps[:half], axis=1).astype(bf16)
        hi = jnp.concatenate(steps[half:], axis=1).astype(bf16)
        for g in range(GB):
            o_ref[g] = jnp.concatenate([jnp.dot(lo, mv_ref[g], preferred_element_type=f32),
                                        jnp.dot(hi, mv_ref[g], preferred_element_type=f32)], axis=1).astype(bf16)

    return pl.pallas_call(
        body, grid=(GROUPS // GB,),
        in_specs=[pl.BlockSpec((n, 128), lambda i: (0, i)), pl.BlockSpec(movers.shape, lambda i: (0, 0, 0))],
        out_specs=pl.BlockSpec((GB, nc, CHUNK * GCH), lambda i: (i, 0, 0)),
        out_shape=jax.ShapeDtypeStruct((GROUPS, nc, CHUNK * GCH), bf16), name="s5_to_groups",
        compiler_params=_params(("parallel",)))(x, movers)


def _from_groups(y, movers, base=None):
    nc = y.shape[1]
    n = nc * CHUNK
    half = CHUNK // 2

    def body(*refs):
        y_ref, mv_ref = refs[:2]
        o_ref = refs[-1]
        keep = lax.broadcasted_iota(jnp.int32, (nc, 1), 0) >= PAD // CHUNK
        lo = jnp.concatenate([y_ref[g][:, :128] for g in range(GB)], axis=1)
        hi = jnp.concatenate([y_ref[g][:, 128:] for g in range(GB)], axis=1)
        for t in range(CHUNK):
            rows = pl.ds(t, nc, stride=CHUNK)
            v = jnp.dot(lo if t < half else hi, mv_ref[t % half], preferred_element_type=f32)
            if base is not None:
                v = refs[2][rows, :] + jnp.where(keep, v, 0.0)
            o_ref[rows, :] = v

    tok = pl.BlockSpec((n, 128), lambda i: (0, i))
    args = (y, movers) if base is None else (y, movers, base)
    return pl.pallas_call(
        body, grid=(GROUPS // GB,),
        in_specs=[pl.BlockSpec((GB, nc, CHUNK * GCH), lambda i: (i, 0, 0)),
                  pl.BlockSpec(movers.shape, lambda i: (0, 0, 0))] + ([] if base is None else [tok]),
        out_specs=tok, out_shape=jax.ShapeDtypeStruct((n, D), f32), name="s5_from_groups",
        compiler_params=_params(("parallel",)))(*args)


def _gelu(y):
    return 0.5 * y * (1.0 + lax.erf(y * (2.0 ** -0.5)))


def _gelu_grad(y):
    return 0.5 * (1.0 + lax.erf(y * (2.0 ** -0.5))) + y * jnp.exp(-0.5 * y * y) * (1.0 / math.sqrt(2.0 * math.pi))


def _s5_fwd(h, movers, mats, d_skip, w_glu, w_out):
    msum, pin, pout, a_re, a_im = mats
    coefs = _s5_coefs(a_re, a_im)
    ug = _to_groups(h, movers, mask=True)
    ys = _from_groups(_s5_core_fwd(ug, msum.astype(bf16), pin.astype(bf16), pout.astype(bf16), coefs), movers)

    def post(i, ys, h, d):
        y = ys + d * h
        return (y, _gelu(y)), ()

    y, g_bf = _ew("s5_gelu", post, [ys, h], [d_skip], [(D, f32), (D, bf16)])
    gw = _mm("s5_glu_mm", g_bf, w_glu)

    def glu(i, y, gw):
        return (_gelu(y) * _sigmoid(gw),), ()

    z_bf = _ew("s5_glu", glu, [y, gw], [], [(D, bf16)])[0]
    mix = _mm("s5_out_mm", z_bf, w_out)
    return mix, (ug, y, g_bf, gw, z_bf)


def _s5_bwd(dmix_bf, h, movers, saved, mats, vjp_mats, d_skip, w_glu, w_out):
    ug, y, g_bf, gw, z_bf = saved
    msum, pin, pout, a_re, a_im = mats
    coefs = _s5_coefs(a_re, a_im)
    dz = _mm("s5_dz_mm", dmix_bf, w_out, trans_b=True)
    d_w_out = _mm_tn("s5_dwout", z_bf, dmix_bf)

    def dglu(i, dz, y, gw):
        g, s = _gelu(y), _sigmoid(gw)
        return (dz * g * s * (1.0 - s), dz * s), ()

    dgw_bf, dg1 = _ew("s5_dglu", dglu, [dz, y, gw], [], [(D, bf16), (D, f32)])
    d_w_glu = _mm_tn("s5_dwglu", g_bf, dgw_bf)
    dg2 = _mm("s5_dg_mm", dgw_bf, w_glu, trans_b=True)

    def dgelu(i, dg1, dg2, y, h, d):
        dy = (dg1 + dg2) * _gelu_grad(y)
        return (dy, dy * d), (jnp.sum(dy * h, axis=0, keepdims=True),)

    dy, dh_skip, dd = _ew("s5_dgelu", dgelu, [dg1, dg2, y, h], [d_skip], [(D, f32), (D, f32)], [(1, D)])
    dug, dm, dpin, dpout, a1f, a2f, a1r, a2r = _s5_core_bwd(
        ug, _to_groups(dy, movers, mask=False), msum.astype(bf16), pin.astype(bf16), pout.astype(bf16), coefs)
    dh = _from_groups(dug, movers, base=dh_skip)
    a1 = jnp.stack([a1f, a1r]).reshape(2, GROUPS, 2 * NSTATE)
    a2 = jnp.stack([a2f, a2r]).reshape(2, GROUPS, 2 * NSTATE)
    da_re = a1[..., :NSTATE] + a1[..., NSTATE:]
    da_im = a2[..., NSTATE:] - a2[..., :NSTATE]
    d_params = vjp_mats((dm, dpin, dpout, da_re, da_im))
    return dh, d_params, dd[0], d_w_glu, d_w_out


def _rope_tables(n):
    row = jnp.arange(n, dtype=jnp.int32) - OFF
    real = row >= 0
    rid = jnp.where(real, row // GRID_W, 0).astype(f32)
    cid = jnp.where(real, row % GRID_W, 0).astype(f32)
    half = HD // 2
    inv = ROPE_THETA ** (-jnp.arange(0, half, 2, dtype=f32) / half)
    ar, ac = rid[:, None] * inv[None, :], cid[:, None] * inv[None, :]
    cos = jnp.concatenate([jnp.cos(ar), jnp.cos(ar), jnp.cos(ac), jnp.cos(ac)], axis=1)
    sin = jnp.concatenate([-jnp.sin(ar), jnp.sin(ar), -jnp.sin(ac), jnp.sin(ac)], axis=1)
    return jnp.tile(cos, (1, 2)), jnp.tile(sin, (1, 2))


def _head_mats():
    head = jnp.arange(QW, dtype=jnp.int32)[:, None] // HD == jnp.arange(128, dtype=jnp.int32)[None, :]
    return head.astype(f32) * (1.0 / HD), head.astype(f32).T


def _rot(v):
    w = v.shape[1]
    lane = lax.broadcasted_iota(jnp.int32, v.shape, 1)
    return jnp.where(lane % 32 < 16, pltpu.roll(v, w - 16, 1), pltpu.roll(v, 16, 1))


def _head_mean(v, e, et):
    w = v.shape[1]
    m = jnp.dot(v, e[:w], preferred_element_type=f32, precision=HIGH)
    return m, et[:, :w]


def _rms_rope(t, gain, e, et, cos, sin):
    w = t.shape[1]
    ms, spread = _head_mean(t * t, e, et)
    rs = jnp.dot(lax.rsqrt(ms + QK_EPS), spread, preferred_element_type=f32, precision=HIGH)
    n0 = t * rs
    n = n0 * gain
    reps = w // 128
    return n * jnp.tile(cos, (1, reps)) + _rot(n) * jnp.tile(sin, (1, reps))


def _rms_rope_bwd(dout, t, gain, e, et, cos, sin):
    w = t.shape[1]
    reps = w // 128
    ms, spread = _head_mean(t * t, e, et)
    rs = jnp.dot(lax.rsqrt(ms + QK_EPS), spread, preferred_element_type=f32, precision=HIGH)
    n0 = t * rs
    dn = dout * jnp.tile(cos, (1, reps)) + _rot(dout * jnp.tile(sin, (1, reps)))
    dn0 = dn * gain
    mm, _ = _head_mean(dn0 * n0, e, et)
    corr = jnp.dot(mm, spread, preferred_element_type=f32, precision=HIGH)
    return rs * (dn0 - n0 * corr), jnp.sum(dn * n0, axis=0, keepdims=True)


def _qk_fwd(qkv, qg, kg, e, et, cos, sin):
    def fn(i, qkv, cos, sin, qg, kg, e, et):
        q = _rms_rope(qkv[:, :QW], qg, e, et, cos, sin) * Q_SCALE
        k = _rms_rope(qkv[:, QW:QW + KW], kg, e, et, cos, sin)
        return (q, k, qkv[:, QW + KW:]), ()

    return _ew("qk_rope", fn, [qkv, cos, sin], [qg, kg, e, et], [(QW, bf16), (KW, bf16), (KW, bf16)])


def _qk_bwd(qkv, dq, dk, dv, qg, kg, e, et, cos, sin):
    def fn(i, qkv, cos, sin, dq, dk, dv, qg, kg, e, et):
        dtq, dgq = _rms_rope_bwd(dq * (HD ** -0.5), qkv[:, :QW], qg, e, et, cos, sin)
        dtk, dgk = _rms_rope_bwd(dk * math.log(2.0), qkv[:, QW:QW + KW], kg, e, et, cos, sin)
        return (jnp.concatenate([dtq, dtk, dv], axis=1),), (dgq, dgk)

    return _ew("qk_rope_bwd", fn, [qkv, cos, sin, dq, dk, dv], [qg, kg, e, et], [(QKVW, bf16)], [(1, QW), (1, KW)])


def _to_heads(a, nh):
    return a.reshape(a.shape[0], nh, HD).transpose(1, 0, 2)


def _from_heads(a):
    return a.transpose(1, 0, 2).reshape(a.shape[1], a.shape[0] * HD)


def _masked_first(s, c):
    if c:
        return s
    col = lax.broadcasted_iota(jnp.int32, (1, s.shape[1]), 1)
    return jnp.where(col >= PAD, s, NEG)


def _flash_fwd(q, k, v1, tq=ROW_TILE, tc=KEY_CHUNK):
    n = q.shape[1]
    nc = n // tc

    def body(q_ref, k_ref, v_ref, o_ref, lse_ref):
        qb = q_ref[0]

        def scores(c):
            ks = k_ref[0, pl.ds(c * tc, tc), :]
            return _masked_first(lax.dot_general(qb, ks, NT, preferred_element_type=f32), c)

        m = jnp.full((tq, 1), NEG, f32)
        acc = jnp.zeros((tq, 2 * HD), f32)
        s_next = scores(0)
        for c in range(nc):
            s = s_next
            if c + 1 < nc:
                s_next = scores(c + 1)
            m_new = jnp.maximum(m, jnp.max(s, axis=1, keepdims=True))
            p = jnp.exp2(s - m_new)
            acc = jnp.exp2(m - m_new) * acc + jnp.dot(p.astype(bf16), v_ref[0, pl.ds(c * tc, tc), :],
                                                      preferred_element_type=f32)
            m = m_new
        l = acc[:, HD:HD + 1]
        o_ref[0] = acc[:, :HD] / l
        lse_ref[0] = m + jnp.log2(l)

    return pl.pallas_call(
        body, grid=(NQ, n // tq),
        in_specs=[pl.BlockSpec((1, tq, HD), lambda h, i: (h, i, 0)),
                  pl.BlockSpec((1, n, HD), lambda h, i: (h // (NQ // NKV), 0, 0)),
                  pl.BlockSpec((1, n, 2 * HD), lambda h, i: (h // (NQ // NKV), 0, 0))],
        out_specs=[pl.BlockSpec((1, tq, HD), lambda h, i: (h, i, 0)), pl.BlockSpec((1, tq, 1), lambda h, i: (h, i, 0))],
        out_shape=[jax.ShapeDtypeStruct((NQ, n, HD), f32), jax.ShapeDtypeStruct((NQ, n, 1), f32)],
        name="flash_fwd", compiler_params=_params(("parallel", "parallel")))(q, k, v1)


def _flash_bwd(q, k, kt, v, do, lse_row, delta_row, tk=ROW_TILE, tc=KEY_CHUNK):
    n = q.shape[1]
    nc = n // tc
    grp = NQ // NKV

    def body(q_ref, do_ref, lse_ref, delta_ref, k_ref, kt_ref, v_ref, dqt_ref, dk_ref, dv_ref):
        j, g = pl.program_id(1), pl.program_id(2)
        kb, vb, ktb = k_ref[0], v_ref[0], kt_ref[0]
        valid = lax.broadcasted_iota(jnp.int32, (tk, 1), 0) + j * tk >= PAD

        @pl.when(j == 0)
        def _():
            dqt_ref[g] = jnp.zeros((HD, n), f32)

        def products(c):
            rows = pl.ds(c * tc, tc)
            return (lax.dot_general(kb, q_ref[0, rows, :], NT, preferred_element_type=f32),
                    lax.dot_general(vb, do_ref[0, rows, :], NT, preferred_element_type=f32))

        dk = jnp.zeros((tk, HD), f32)
        dv = jnp.zeros((tk, HD), f32)
        nxt = products(0)
        for c in range(nc):
            st, dpt = nxt
            if c + 1 < nc:
                nxt = products(c + 1)
            rows = pl.ds(c * tc, tc)
            pt = jnp.exp2(jnp.where(valid, st, NEG) - lse_ref[0, :, rows])
            dv = dv + jnp.dot(pt.astype(bf16), do_ref[0, rows, :], preferred_element_type=f32)
            dst = (pt * (dpt - delta_ref[0, :, rows])).astype(bf16)
            dk = dk + jnp.dot(dst, q_ref[0, rows, :], preferred_element_type=f32)
            dqt_ref[g, :, rows] += jnp.dot(ktb, dst, preferred_element_type=f32)

        @pl.when(g == 0)
        def _():
            dk_ref[0] = dk
            dv_ref[0] = dv

        @pl.when(g > 0)
        def _():
            dk_ref[0] += dk
            dv_ref[0] += dv

    hspec = pl.BlockSpec((1, n, HD), lambda h, j, g: (h * grp + g, 0, 0))
    rspec = pl.BlockSpec((1, 1, n), lambda h, j, g: (h * grp + g, 0, 0))
    kspec = pl.BlockSpec((1, tk, HD), lambda h, j, g: (h, j, 0))
    return pl.pallas_call(
        body, grid=(NKV, n // tk, grp),
        in_specs=[hspec, hspec, rspec, rspec, kspec, pl.BlockSpec((1, HD, tk), lambda h, j, g: (h, 0, j)), kspec],
        out_specs=[pl.BlockSpec((grp, HD, n), lambda h, j, g: (h, 0, 0)), kspec, kspec],
        out_shape=[jax.ShapeDtypeStruct((NQ, HD, n), f32)] + [jax.ShapeDtypeStruct((NKV, n, HD), f32)] * 2,
        name="flash_bwd", compiler_params=_params(("parallel", "arbitrary", "arbitrary")))(
            q, do, lse_row, delta_row, k, kt, v)


def _attn_fwd(h_bf, w_qkv_t, qg, kg, w_out, tabs):
    e, et, cos, sin = tabs
    qkv = _mm("attn_qkv_mm", h_bf, w_qkv_t, trans_b=True)
    q_bf, k_bf, v_bf = _qk_fwd(qkv, qg, kg, e, et, cos, sin)
    q16, k4, v4 = _to_heads(q_bf, NQ), _to_heads(k_bf, NKV), _to_heads(v_bf, NKV)
    ones = jnp.zeros((NKV, v4.shape[1], HD), bf16).at[:, :, 0].set(1.0)
    o16, lse = _flash_fwd(q16, k4, jnp.concatenate([v4, ones], axis=2))
    o = _from_heads(o16)
    o_bf = o.astype(bf16)
    mix = _mm("attn_out_mm", o_bf, w_out)
    return mix, (qkv, q16, k4, v4, o, lse, o_bf)


def _attn_bwd(dmix_bf, h_bf, saved, w_qkv_t, qg, kg, w_out, tabs):
    e, et, cos, sin = tabs
    qkv, q16, k4, v4, o, lse, o_bf = saved
    n = qkv.shape[0]
    do = _mm("attn_do_mm", dmix_bf, w_out, trans_b=True, out_dtype=bf16)
    d_w_out = _mm_tn("attn_dwout", o_bf, dmix_bf)

    def head_dots(i, do, o, e):
        return (jnp.dot(do.astype(f32) * o, e, preferred_element_type=f32, precision=HIGH) * HD,), ()

    delta = _ew("attn_delta", head_dots, [do, o], [e], [(128, f32)])[0]
    dqt, dk4, dv4 = _flash_bwd(q16, k4, k4.transpose(0, 2, 1), v4, _to_heads(do, NQ), lse.reshape(NQ, 1, n),
                               delta[:, :NQ].T.reshape(NQ, 1, n))
    dq = dqt.transpose(2, 0, 1).reshape(n, QW)
    dqkv_bf, dgq, dgk = _qk_bwd(qkv, dq, _from_heads(dk4), _from_heads(dv4), qg, kg, e, et, cos, sin)
    d_w_qkv_t = _mm_tn("attn_dwqkv", dqkv_bf, h_bf)
    dh = _mm("attn_dh_mm", dqkv_bf, w_qkv_t)
    return dh, d_w_qkv_t, dgq.reshape(NQ, HD).sum(0), dgk.reshape(NKV, HD).sum(0), d_w_out


def _all_gather(name, shard):
    def body(x_ref, out_ref, send_sems, recv_sems, local_sem):
        x, y, c = lax.axis_index("x"), lax.axis_index("y"), lax.axis_index("c")
        me, sibling = (x, y, c), (x, y, 1 - c)
        chips = [(1 - x, y), (x, 1 - y), (1 - x, 1 - y)]

        def slot(px, py, pc):
            return out_ref.at[4 * px + 2 * py + pc]

        def copy(k, block, to, src=None):
            return pltpu.make_async_remote_copy(
                src_ref=slot(*block) if src is None else src, dst_ref=slot(*block),
                send_sem=send_sems.at[k], recv_sem=recv_sems.at[k], device_id=to, device_id_type=MESH)

        mine = pltpu.make_async_copy(x_ref, slot(*me), local_sem)
        mine.start()
        first = [copy(0, me, sibling, src=x_ref)]
        first += [copy(1 + j, me, (*chip, c), src=x_ref) for j, chip in enumerate(chips)]
        for cp in first:
            cp.start()
        passed = [copy(4 + j, (*chip, c), sibling) for j, chip in enumerate(chips)]
        for j, chip in enumerate(chips):
            copy(1 + j, (*chip, c), me).wait_recv()
            passed[j].start()
        copy(0, sibling, me).wait_recv()
        for j, chip in enumerate(chips):
            copy(4 + j, (*chip, 1 - c), me).wait_recv()
        for cp in first + passed:
            cp.wait_send()
        mine.wait()

    return pl.pallas_call(
        body, out_shape=jax.ShapeDtypeStruct((8,) + shard.shape, shard.dtype), in_specs=[ANY], out_specs=ANY,
        scratch_shapes=[pltpu.SemaphoreType.DMA((7,)), pltpu.SemaphoreType.DMA((7,)), pltpu.SemaphoreType.DMA],
        name=name)(shard)


def _swap_sibling(name, theirs):
    k = len(theirs)

    def body(*refs):
        src, dst, send_sems, recv_sems = refs[:k], refs[k:2 * k], refs[2 * k], refs[2 * k + 1]
        x, y, c = lax.axis_index("x"), lax.axis_index("y"), lax.axis_index("c")
        copies = [pltpu.make_async_remote_copy(src_ref=src[j], dst_ref=dst[j], send_sem=send_sems.at[j],
                                               recv_sem=recv_sems.at[j], device_id=(x, y, 1 - c), device_id_type=MESH)
                  for j in range(k)]
        for cp in copies:
            cp.start()
        for cp in copies:
            cp.wait()

    return pl.pallas_call(
        body, out_shape=[jax.ShapeDtypeStruct(a.shape, a.dtype) for a in theirs], in_specs=[ANY] * k,
        out_specs=[ANY] * k, scratch_shapes=[pltpu.SemaphoreType.DMA((k,)), pltpu.SemaphoreType.DMA((k,))],
        name=name)(*theirs)


def _exchange_chips(name, parts):
    k = len(parts)

    def body(*refs):
        p_refs, t_refs = refs[:k], refs[k:2 * k]
        send_sems, recv_sems, local_sems = refs[2 * k:]
        x, y, c = lax.axis_index("x"), lax.axis_index("y"), lax.axis_index("c")
        q = 2 * x + y
        copies = []
        for j in range(k):
            copies.append(pltpu.make_async_copy(p_refs[j].at[q], t_refs[j].at[q], local_sems.at[j]))
            for hop in (1, 2, 3):
                tx, ty = x ^ (hop >> 1), y ^ (hop & 1)
                copies.append(pltpu.make_async_remote_copy(
                    src_ref=p_refs[j].at[2 * tx + ty], dst_ref=t_refs[j].at[q], send_sem=send_sems.at[3 * j + hop - 1],
                    recv_sem=recv_sems.at[3 * j + hop - 1], device_id=(tx, ty, c), device_id_type=MESH))
        for cp in copies:
            cp.start()
        for cp in copies:
            cp.wait()

    return pl.pallas_call(
        body, out_shape=[jax.ShapeDtypeStruct(a.shape, a.dtype) for a in parts], in_specs=[ANY] * k,
        out_specs=[ANY] * k,
        scratch_shapes=[pltpu.SemaphoreType.DMA((3 * k,)), pltpu.SemaphoreType.DMA((3 * k,)),
                        pltpu.SemaphoreType.DMA((k,))],
        name=name)(*parts)


def _reduce_scatter(mine, theirs):
    got = _swap_sibling("rs_sibling", list(theirs))
    parts = []
    for a, b, dt, nm in zip(mine, got, (bf16, f32), ("rs_add2", "rs_add2_small")):
        rows = 4 * a.shape[1]
        parts.append(_ew(nm, lambda i, a, b: ((a + b,), ()), [a.reshape(rows, D), b.reshape(rows, D)], [],
                         [(D, dt)], tile=RS_TILE)[0].reshape(a.shape))
    ts = _exchange_chips("rs_chips", parts)

    def add4(i, a, b, c, d):
        return ((((a.astype(f32) + b.astype(f32)) + c.astype(f32)) + d.astype(f32),), ())

    return [_ew(nm, add4, [(t, 0), (t, 1), (t, 2), (t, 3)], [], [(D, f32)], tile=RS_TILE)[0]
            for t, nm in zip(ts, ("rs_add4", "rs_add4_small"))]


def _pack_rows(parts, rows):
    flat = jnp.concatenate([p.reshape(-1) for p in parts])
    return jnp.pad(flat, (0, rows * D - flat.shape[0])).reshape(rows, D)


def _unpack(flat, shapes):
    out, off = [], 0
    for s in shapes:
        n = math.prod(s)
        out.append(flat[off:off + n].reshape(s))
        off += n
    return out


def _mat_rows(block, transposed, blk):
    a = jnp.swapaxes(block, 1, 2) if transposed else block
    a = jnp.pad(a, ((0, 0), (0, blk - a.shape[1]), (0, 0)))
    return a.reshape(-1, D)


def _mat_block(rows, transposed, blk, real):
    a = rows.reshape(-1, blk, D)[:, :real]
    return jnp.swapaxes(a, 1, 2) if transposed else a


def _mat_full(gathered, blk):
    layers = gathered.shape[1] // blk
    return gathered.reshape(8, layers, blk, D).transpose(1, 0, 2, 3).reshape(layers, 8 * blk, D)


def _vec_full(gathered):
    return gathered.transpose(1, 0, 2).reshape(gathered.shape[1], D)


def _grad_slots(full, small, cc):
    def halves(a, blk):
        a4 = a.reshape(4, 2, blk, D)
        return [lax.dynamic_index_in_dim(a4, sel, axis=1, keepdims=False) for sel in (cc, 1 - cc)]

    mine, theirs = [], []
    for name, _, blk, _ in MATS:
        for layer in full[name]:
            a, b = halves(layer, blk)
            mine.append(a)
            theirs.append(b)
    a, b = halves(small, REP_PIECE)
    return (jnp.concatenate(mine, axis=1), a), (jnp.concatenate(theirs, axis=1), b)


def _local_step(x0, target0, w, fw):
    seq = x0.shape[0]
    n = OFF + seq
    movers = _block_movers()
    h = jnp.concatenate([jnp.zeros((PAD, D), f32), fw['meta_tokens'], x0], axis=0)
    h_bf = h.astype(bf16)
    tabs = _head_mats() + _rope_tables(n)
    qg = [jnp.tile(w['attn_q_gain'][j], NQ)[None, :] for j in range(2)]
    kg = [jnp.tile(w['attn_k_gain'][j], NKV)[None, :] for j in range(2)]
    s5_names = ['s5_lambda_re', 's5_lambda_im', 's5_log_dt', 's5_b_re', 's5_b_im', 's5_c_re', 's5_c_im']
    s5_mats, s5_vjp = [], []
    for j in range(2):
        mats, vjp = jax.vjp(_s5_mats, *[w[k][j] for k in s5_names])
        s5_mats.append(mats)
        s5_vjp.append(vjp)
    saved = []
    for i in range(DEPTH):
        j = i // 2
        if i % 2 == 0:
            mix, sv = _s5_fwd(h, movers, s5_mats[j], w['s5_d'][j][None, :], (fw['s5_w_glu'], j), (fw['s5_w_out'], j))
        else:
            mix, sv = _attn_fwd(h_bf, (fw['attn_w_qkv'], j), qg[j], kg[j], (fw['attn_w_out'], j), tabs)
        r1, h1, h1_bf = _ln_fwd(h, mix, fw['ln_gain'][i, 0][None, :], fw['ln_bias'][i, 0][None, :])
        gate, up, act = _ffn_up(h1_bf, (fw['ffn_w_gate'], i), (fw['ffn_w_up'], i))
        f = _mm("ffn_down_mm", act, (fw['ffn_w_down'], i))
        r2, h2, h2_bf = _ln_fwd(h1, f, fw['ln_gain'][i, 1][None, :], fw['ln_bias'][i, 1][None, :])
        saved.append((h, h_bf, sv, r1, h1_bf, gate, up, act, r2))
        h, h_bf = h2, h2_bf

    d_b, sq = _loss_grad(h, target0)
    loss = 0.5 * jnp.sum(sq) * (1.0 / D)

    grads = {k: [None] * (DEPTH if k.startswith('ffn') else 2) for k in WEIGHTS}
    d_ln_gain = [[None, None] for _ in range(DEPTH)]
    d_ln_bias = [[None, None] for _ in range(DEPTH)]
    d_a = None
    for i in reversed(range(DEPTH)):
        j = i // 2
        h_in, h_in_bf, sv, r1, h1_bf, gate, up, act, r2 = saved[i]
        dr2, dr2_bf, dg, db = _ln_bwd(d_a, d_b, r2, fw['ln_gain'][i, 1][None, :])
        d_ln_gain[i][1], d_ln_bias[i][1] = dg[0], db[0]
        dgate, dup = _ffn_dup(dr2_bf, (fw['ffn_w_down'], i), gate, up)
        grads['ffn_w_down'][i] = _mm_tn("ffn_dwdown", act, dr2_bf, tk=DFFP // 2)
        grads['ffn_w_gate'][i] = _mm_tn("ffn_dwgate", dgate, h1_bf, tk=DFFP // 2)
        grads['ffn_w_up'][i] = _mm_tn("ffn_dwup", dup, h1_bf, tk=DFFP // 2)
        dh1 = _mm2("ffn_dh_mm", dgate, (fw['ffn_w_gate'], i), dup, (fw['ffn_w_up'], i))
        dr1, dr1_bf, dg, db = _ln_bwd(dr2, dh1, r1, fw['ln_gain'][i, 0][None, :])
        d_ln_gain[i][0], d_ln_bias[i][0] = dg[0], db[0]
        if i % 2 == 0:
            dh, d_par, dd, d_w_glu, d_w_out = _s5_bwd(dr1_bf, h_in, movers, sv, s5_mats[j], s5_vjp[j],
                                                      w['s5_d'][j][None, :], (fw['s5_w_glu'], j), (fw['s5_w_out'], j))
            for k, g in zip(s5_names, d_par):
                grads[k][j] = g
            grads['s5_d'][j], grads['s5_w_glu'][j], grads['s5_w_out'][j] = dd, d_w_glu, d_w_out
        else:
            dh, d_w_qkv, dgq, dgk, d_w_out = _attn_bwd(dr1_bf, h_in_bf, sv, (fw['attn_w_qkv'], j), qg[j], kg[j],
                                                       (fw['attn_w_out'], j), tabs)
            grads['attn_w_qkv'][j], grads['attn_w_out'][j] = d_w_qkv, d_w_out
            grads['attn_q_gain'][j], grads['attn_k_gain'][j] = dgq, dgk
        d_a, d_b = dr1, dh
    dh0 = _ew("dh0", lambda i, a, b: ((ALPHA * a + b,), ()), [d_a, d_b], [], [(D, f32)])[0]
    mats = {m[0] for m in MATS}
    full = {k: (v if k in mats else jnp.stack(v)) for k, v in grads.items() if v[0] is not None}
    full['meta_tokens'] = dh0[PAD:OFF]
    full['ln_gain'] = jnp.stack([jnp.stack(r) for r in d_ln_gain])
    full['ln_bias'] = jnp.stack([jnp.stack(r) for r in d_ln_bias])

    return loss, dh0[OFF:], full


def kernel(x, meta_tokens, s5_lambda_re, s5_lambda_im, s5_log_dt, s5_b_re, s5_b_im, s5_c_re, s5_c_im, s5_d, s5_w_glu, s5_w_out, attn_w_qkv, attn_q_gain, attn_k_gain, attn_w_out, ffn_w_gate, ffn_w_up, ffn_w_down, ln_gain, ln_bias, loss_target, m_meta_tokens, m_s5_lambda_re, m_s5_lambda_im, m_s5_log_dt, m_s5_b_re, m_s5_b_im, m_s5_c_re, m_s5_c_im, m_s5_d, m_s5_w_glu, m_s5_w_out, m_attn_w_qkv, m_attn_q_gain, m_attn_k_gain, m_attn_w_out, m_ffn_w_gate, m_ffn_w_up, m_ffn_w_down, m_ln_gain, m_ln_bias, v_meta_tokens, v_s5_lambda_re, v_s5_lambda_im, v_s5_log_dt, v_s5_b_re, v_s5_b_im, v_s5_c_re, v_s5_c_im, v_s5_d, v_s5_w_glu, v_s5_w_out, v_attn_w_qkv, v_attn_q_gain, v_attn_k_gain, v_attn_w_out, v_ffn_w_gate, v_ffn_w_up, v_ffn_w_down, v_ln_gain, v_ln_bias):
    w = dict(zip(WEIGHTS, (meta_tokens, s5_lambda_re, s5_lambda_im, s5_log_dt, s5_b_re, s5_b_im, s5_c_re, s5_c_im, s5_d, s5_w_glu, s5_w_out, attn_w_qkv, attn_q_gain, attn_k_gain, attn_w_out, ffn_w_gate, ffn_w_up, ffn_w_down, ln_gain, ln_bias)))
    mom = dict(zip(WEIGHTS, (m_meta_tokens, m_s5_lambda_re, m_s5_lambda_im, m_s5_log_dt, m_s5_b_re, m_s5_b_im, m_s5_c_re, m_s5_c_im, m_s5_d, m_s5_w_glu, m_s5_w_out, m_attn_w_qkv, m_attn_q_gain, m_attn_k_gain, m_attn_w_out, m_ffn_w_gate, m_ffn_w_up, m_ffn_w_down, m_ln_gain, m_ln_bias)))
    vel = dict(zip(WEIGHTS, (v_meta_tokens, v_s5_lambda_re, v_s5_lambda_im, v_s5_log_dt, v_s5_b_re, v_s5_b_im, v_s5_c_re, v_s5_c_im, v_s5_d, v_s5_w_glu, v_s5_w_out, v_attn_w_qkv, v_attn_q_gain, v_attn_k_gain, v_attn_w_out, v_ffn_w_gate, v_ffn_w_up, v_ffn_w_down, v_ln_gain, v_ln_bias)))
    cc = lax.axis_index("c")
    dev = 4 * lax.axis_index("x") + 2 * lax.axis_index("y") + cc

    mat_rows = jnp.concatenate([_mat_rows(w[n], t, blk) for n, t, blk, _ in MATS]).astype(bf16)
    g_mats = _all_gather("ag_weights", mat_rows)
    g_vecs = _all_gather("ag_vectors", jnp.concatenate([w[n].reshape(-1, 128) for n in VECS]))
    fw, off = {}, 0
    for n, _, blk, _ in MATS:
        rows = w[n].shape[0] * blk
        fw[n] = _mat_full(g_mats[:, off:off + rows], blk)
        off += rows
    off = 0
    for n in VECS:
        rows = w[n].size // 128
        fw[n] = _vec_full(g_vecs[:, off:off + rows]).reshape(w[n].shape[:-1] + (D,))
        off += rows

    loss, grad_x, full = _local_step(x[0], loss_target[0], w, fw)
    loss = lax.psum(loss, AXES)
    grad_x = grad_x[None]

    small_names = REPL + VECS
    mine, theirs = _grad_slots(full, _pack_rows([full[k] for k in small_names], REP_ROWS), cc)
    red, red_small = _reduce_scatter(mine, theirs)
    small_all = _all_gather("ag_small_grads", red_small).reshape(REP_ROWS * D)
    g, off = {}, 0
    for n, t, blk, real in MATS:
        rows = w[n].shape[0] * blk
        g[n] = _mat_block(red[off:off + rows], t, blk, real)
        off += rows
    small = dict(zip(small_names, _unpack(small_all, [full[k].shape for k in small_names])))
    for k in REPL:
        g[k] = small[k]
    for k in VECS:
        g[k] = lax.dynamic_slice_in_dim(small[k], dev * 128, 128, axis=small[k].ndim - 1)

    delta, new_m, new_v = {}, {}, {}
    for n, _, _, _ in MATS:
        shp = w[n].shape
        res = _adamw(*[d[n].reshape(-1, shp[-1]) for d in (w, g, mom, vel)])
        delta[n], new_m[n], new_v[n] = [a.reshape(shp) for a in res]
    shapes = [w[k].shape for k in small_names]
    res = _adamw(*[_pack_rows([d[k] for k in small_names], SMALL_ROWS) for d in (w, g, mom, vel)])
    for out, a in zip((delta, new_m, new_v), res):
        out.update(zip(small_names, _unpack(a.reshape(-1), shapes)))
    return (loss, grad_x, *[g[k] for k in WEIGHTS], *[delta[k] for k in WEIGHTS],
            *[new_m[k] for k in WEIGHTS], *[new_v[k] for k in WEIGHTS])
```

```python
import functools
import math

import jax
import jax.numpy as jnp
from jax import lax
from jax.experimental import pallas as pl
from jax.experimental.pallas import tpu as pltpu

f32 = jnp.float32
bf16 = jnp.bfloat16
HI = lax.Precision.HIGHEST
HIGH = lax.Precision.HIGH
MESH = pl.DeviceIdType.MESH
AXES = ("x", "y", "c")
ANY = pl.BlockSpec(memory_space=pl.ANY)

D = 1024
DEPTH = 4
N_META = 16
PAD = 240
OFF = PAD + N_META
ROW_TILE = 768
KEY_CHUNK = 256
FFN_TILE = 256
ADAM_TILE = 544
GRID_W = 64
HD = 64
NQ = 16
NKV = 4
QW = NQ * HD
KW = NKV * HD
QKVW = QW + 2 * KW
DFF = 2816
GROUPS = 64
GCH = 16
NSTATE = 64
CHUNK = 16
GB = 8
ROPE_THETA = 10000.0
LN_EPS = 1e-5
QK_EPS = 1e-6
ALPHA = (2.0 * DEPTH) ** 0.25
ADAM_LR, ADAM_B1, ADAM_B2, ADAM_EPS, ADAM_WD, ADAM_STEP = 0.001, 0.9, 0.999, 1e-08, 0.01, 10
NEG = -1e30
Q_SCALE = HD ** -0.5 * math.log2(math.e)
VMEM_MB = 56

NT = (((1,), (1,)), ((), ()))
TN = (((0,), (0,)), ((), ()))

WEIGHTS = ['meta_tokens', 's5_lambda_re', 's5_lambda_im', 's5_log_dt', 's5_b_re', 's5_b_im', 's5_c_re', 's5_c_im',
           's5_d', 's5_w_glu', 's5_w_out', 'attn_w_qkv', 'attn_q_gain', 'attn_k_gain', 'attn_w_out', 'ffn_w_gate',
           'ffn_w_up', 'ffn_w_down', 'ln_gain', 'ln_bias']
DFFP = 3072
FF_BLK, FF_BLKP = DFF // 8, DFFP // 8
MATS = [('s5_w_glu', False, 128, 128), ('s5_w_out', False, 128, 128), ('attn_w_qkv', True, 192, 192),
        ('attn_w_out', False, 128, 128), ('ffn_w_gate', True, FF_BLKP, FF_BLK), ('ffn_w_up', True, FF_BLKP, FF_BLK),
        ('ffn_w_down', False, FF_BLKP, FF_BLK)]
VECS = ['meta_tokens', 'ln_gain', 'ln_bias']
REPL = ['s5_lambda_re', 's5_lambda_im', 's5_log_dt', 's5_b_re', 's5_b_im', 's5_c_re', 's5_c_im', 's5_d',
        'attn_q_gain', 'attn_k_gain']
MAT_ROWS = 5760
REP_PIECE = 160
REP_ROWS = 8 * REP_PIECE
RS_TILE = 640


def _params(sem, mb=VMEM_MB):
    return pltpu.CompilerParams(dimension_semantics=sem, vmem_limit_bytes=mb << 20)


def _ew(name, fn, rows, consts, outs, accs=(), tile=ROW_TILE):
    first = rows[0][0] if isinstance(rows[0], tuple) else rows[0]
    n = first.shape[-2]
    tile = min(tile, n)
    assert n % tile == 0, (name, n, tile)
    n_in, n_o, n_a = len(rows) + len(consts), len(outs), len(accs)

    def body(*refs):
        i = pl.program_id(0)
        res_o, res_a = fn(i, *[r[...] for r in refs[:n_in]])
        for r, val in zip(refs[n_in:n_in + n_o], res_o):
            r[...] = val.astype(r.dtype)
        if n_a:
            a_refs = refs[n_in + n_o:]

            @pl.when(i == 0)
            def _():
                for r in a_refs:
                    r[...] = jnp.zeros(r.shape, r.dtype)

            for r, val in zip(a_refs, res_a):
                r[...] += val

    in_specs, args = [], []
    for a in rows:
        if isinstance(a, tuple):
            arr, k = a
            in_specs.append(pl.BlockSpec((None, tile, arr.shape[2]), functools.partial(lambda i, k: (k, i, 0), k=k)))
            args.append(arr)
        else:
            in_specs.append(pl.BlockSpec((tile, a.shape[1]), lambda i: (i, 0)))
            args.append(a)
    for c in consts:
        in_specs.append(pl.BlockSpec(c.shape, lambda i: (0, 0)))
        args.append(c)
    out_specs = [pl.BlockSpec((tile, c), lambda i: (i, 0)) for c, _ in outs]
    out_specs += [pl.BlockSpec(s, lambda i: (0, 0)) for s in accs]
    out_shape = [jax.ShapeDtypeStruct((n, c), dt) for c, dt in outs]
    out_shape += [jax.ShapeDtypeStruct(s, f32) for s in accs]
    res = pl.pallas_call(body, grid=(n // tile,), in_specs=in_specs, out_specs=out_specs, out_shape=out_shape,
                         name=name, compiler_params=_params(("arbitrary",)))(*args)
    return res


def _mm(name, a, b, trans_b=False, out_dtype=f32, tm=ROW_TILE):
    m, k = a.shape
    spec, b, shape = _whole(b)
    n = shape[0] if trans_b else shape[1]
    tm = min(tm, m)
    assert m % tm == 0
    dims = NT if trans_b else (((1,), (0,)), ((), ()))

    def body(a_ref, b_ref, o_ref):
        o_ref[...] = lax.dot_general(a_ref[...], b_ref[...], dims, preferred_element_type=f32).astype(o_ref.dtype)

    return pl.pallas_call(
        body, grid=(m // tm,), in_specs=[pl.BlockSpec((tm, k), lambda i: (i, 0)), spec],
        out_specs=pl.BlockSpec((tm, n), lambda i: (i, 0)),
        out_shape=jax.ShapeDtypeStruct((m, n), out_dtype), name=name, compiler_params=_params(("parallel",)))(a, b)


def _whole(b):
    if isinstance(b, tuple):
        arr, layer = b
        return pl.BlockSpec((None,) + arr.shape[1:], lambda i: (layer, 0, 0)), arr, arr.shape[1:]
    return pl.BlockSpec(b.shape, lambda i: (0, 0)), b, b.shape


def _mm2(name, a1, b1, a2, b2, out_dtype=f32, tm=ROW_TILE // 2):
    m, k = a1.shape
    spec1, b1, shape = _whole(b1)
    spec2, b2, _ = _whole(b2)
    n = shape[1]
    tm = min(tm, m)
    assert m % tm == 0

    def body(a1_ref, b1_ref, a2_ref, b2_ref, o_ref):
        acc = jnp.dot(a1_ref[...], b1_ref[...], preferred_element_type=f32)
        acc += jnp.dot(a2_ref[...], b2_ref[...], preferred_element_type=f32)
        o_ref[...] = acc.astype(o_ref.dtype)

    row = pl.BlockSpec((tm, k), lambda i: (i, 0))
    return pl.pallas_call(
        body, grid=(m // tm,), in_specs=[row, spec1, row, spec2], out_specs=pl.BlockSpec((tm, n), lambda i: (i, 0)),
        out_shape=jax.ShapeDtypeStruct((m, n), out_dtype), name=name,
        compiler_params=_params(("parallel",)))(a1, b1, a2, b2)


def _mm_tn(name, a, g, tk=512, tl=ROW_TILE):
    rows, k1 = a.shape
    n = g.shape[1]
    tl = min(tl, rows)
    assert rows % tl == 0 and k1 % tk == 0

    def body(a_ref, g_ref, o_ref):
        @pl.when(pl.program_id(1) == 0)
        def _():
            o_ref[...] = jnp.zeros(o_ref.shape, f32)

        o_ref[...] += lax.dot_general(a_ref[...], g_ref[...], TN, preferred_element_type=f32)

    return pl.pallas_call(
        body, grid=(k1 // tk, rows // tl),
        in_specs=[pl.BlockSpec((tl, tk), lambda k, l: (l, k)), pl.BlockSpec((tl, n), lambda k, l: (l, 0))],
        out_specs=pl.BlockSpec((tk, n), lambda k, l: (k, 0)),
        out_shape=jax.ShapeDtypeStruct((k1, n), f32), name=name,
        compiler_params=_params(("parallel", "arbitrary")))(a, g)


def _ln_stats(r):
    mean = jnp.mean(r, axis=-1, keepdims=True)
    c = r - mean
    rstd = lax.rsqrt(jnp.mean(c * c, axis=-1, keepdims=True) + LN_EPS)
    return c * rstd, rstd


def _ln_fwd(h, mix, gain, bias):
    def fn(i, h, mix, g, b):
        r = ALPHA * h + mix
        y = _ln_stats(r)[0] * g + b
        return (r, y, y), ()

    return _ew("ln_fwd", fn, [h, mix], [gain, bias], [(D, f32), (D, f32), (D, bf16)])


def _ln_bwd(d_a, d_b, r, gain):
    def core(dout, r, g):
        xhat, rstd = _ln_stats(r)
        dxh = dout * g
        dr = rstd * (dxh - jnp.mean(dxh, axis=-1, keepdims=True) - xhat * jnp.mean(dxh * xhat, axis=-1, keepdims=True))
        return (dr, dr), (jnp.sum(dout * xhat, axis=0, keepdims=True), jnp.sum(dout, axis=0, keepdims=True))

    outs, accs = [(D, f32), (D, bf16)], [(1, D), (1, D)]
    if d_a is None:
        return _ew("ln_bwd_top", lambda i, d, r, g: core(d, r, g), [d_b, r], [gain], outs, accs)
    return _ew("ln_bwd", lambda i, da, db, r, g: core(ALPHA * da + db, r, g), [d_a, d_b, r], [gain], outs, accs)


def _sigmoid(x):
    return 1.0 / (1.0 + jnp.exp(-x))


def _ffn_up(h_bf, w_gate_t, w_up_t, tm=FFN_TILE):
    m, k = h_bf.shape
    gspec, w_gate_t, (n, _) = _whole(w_gate_t)
    uspec, w_up_t, _ = _whole(w_up_t)

    def body(h_ref, wg_ref, wu_ref, g_ref, u_ref, a_ref):
        h = h_ref[...]
        g = lax.dot_general(h, wg_ref[...], NT, preferred_element_type=f32).astype(bf16)
        u = lax.dot_general(h, wu_ref[...], NT, preferred_element_type=f32).astype(bf16)
        g_ref[...] = g
        u_ref[...] = u
        g = g.astype(f32)
        a_ref[...] = (g * _sigmoid(g) * u.astype(f32)).astype(bf16)

    row = pl.BlockSpec((tm, n), lambda i: (i, 0))
    return pl.pallas_call(
        body, grid=(m // tm,), in_specs=[pl.BlockSpec((tm, k), lambda i: (i, 0)), gspec, uspec],
        out_specs=[row, row, row], out_shape=[jax.ShapeDtypeStruct((m, n), bf16)] * 3, name="ffn_up",
        compiler_params=_params(("parallel",)))(h_bf, w_gate_t, w_up_t)


def _ffn_dup(df_bf, w_down, gate, up, tm=FFN_TILE):
    m, k = df_bf.shape
    wspec, w_down, (n, _) = _whole(w_down)

    def body(d_ref, w_ref, g_ref, u_ref, dg_ref, du_ref):
        da = lax.dot_general(d_ref[...], w_ref[...], NT, preferred_element_type=f32).astype(bf16).astype(f32)
        g, u = g_ref[...].astype(f32), u_ref[...].astype(f32)
        s = _sigmoid(g)
        dg_ref[...] = (da * u * s * (1.0 + g * (1.0 - s))).astype(bf16)
        du_ref[...] = (da * g * s).astype(bf16)

    row = pl.BlockSpec((tm, n), lambda i: (i, 0))
    return pl.pallas_call(
        body, grid=(m // tm,),
        in_specs=[pl.BlockSpec((tm, k), lambda i: (i, 0)), wspec, row, row],
        out_specs=[row, row], out_shape=[jax.ShapeDtypeStruct((m, n), bf16)] * 2, name="ffn_dup",
        compiler_params=_params(("parallel",)))(df_bf, w_down, gate, up)


def _loss_grad(h, target):
    n = h.shape[0]

    def body(h_ref, t_ref, d_ref, sq_ref):
        i = pl.program_id(0)

        @pl.when(i == 0)
        def _():
            d_ref[...] = jnp.zeros(d_ref.shape, f32)
            sq_ref[...] = jnp.zeros(sq_ref.shape, f32)

        @pl.when(i > 0)
        def _():
            e = h_ref[...] - t_ref[...]
            d_ref[...] = e * (1.0 / D)
            sq_ref[...] += jnp.sum(e * e, axis=0, keepdims=True)

    return pl.pallas_call(
        body, grid=(n // OFF,),
        in_specs=[pl.BlockSpec((OFF, D), lambda i: (i, 0)), pl.BlockSpec((OFF, D), lambda i: (jnp.maximum(i - 1, 0), 0))],
        out_specs=[pl.BlockSpec((OFF, D), lambda i: (i, 0)), pl.BlockSpec((1, D), lambda i: (0, 0))],
        out_shape=[jax.ShapeDtypeStruct((n, D), f32), jax.ShapeDtypeStruct((1, D), f32)], name="loss",
        compiler_params=_params(("arbitrary",)))(h, target)


def _adamw(w, g, m, v):
    def fn(i, w, g, m, v):
        m = ADAM_B1 * m + (1.0 - ADAM_B1) * g
        v = ADAM_B2 * v + (1.0 - ADAM_B2) * jnp.square(g)
        m_hat = m / (1.0 - ADAM_B1 ** ADAM_STEP)
        v_hat = v / (1.0 - ADAM_B2 ** ADAM_STEP)
        delta = -ADAM_LR * (m_hat / (jnp.sqrt(v_hat) + ADAM_EPS) + ADAM_WD * w)
        return (delta, m, v), ()

    rows, cols = w.shape
    cap = ADAM_TILE if cols > 128 else 4 * ADAM_TILE
    fits = [t for t in range(8, min(rows, cap) + 1, 8) if rows % t == 0]
    return _ew("adamw", fn, [w, g, m, v], [], [(cols, f32)] * 3, tile=max(fits) if fits else rows)


def _s5_mats(lam_re, lam_im, log_dt, b_re, b_im, c_re, c_im):
    steps = jnp.arange(CHUNK + 1, dtype=f32)
    n = CHUNK * GCH
    last = n - GCH

    def one(lr, li, ldt, br, bi, cr, ci, reverse):
        dt = jnp.exp(ldt)[:, None]
        mag = jnp.exp(lr * dt)
        abr, abi = mag * jnp.cos(li * dt), mag * jnp.sin(li * dt)
        nr, ni = abr - 1.0, abi
        den = lr * lr + li * li
        zr, zi = (nr * lr + ni * li) / den, (ni * lr - nr * li) / den
        bbr = zr[..., None] * br - zi[..., None] * bi
        bbi = zr[..., None] * bi + zi[..., None] * br
        pmag = jnp.exp((lr * dt)[..., None] * steps)
        pang = (li * dt)[..., None] * steps
        pr, pi = pmag * jnp.cos(pang), pmag * jnp.sin(pang)
        crt, cit = jnp.swapaxes(cr, 1, 2)[:, :, None, :], jnp.swapaxes(ci, 1, 2)[:, :, None, :]
        car = crt * pr[..., None] - cit * pi[..., None]
        cai = crt * pi[..., None] + cit * pr[..., None]
        if reverse:
            taps = slice(CHUNK - 1, None, -1)
            outs = slice(CHUNK, 0, -1)
            ins = slice(0, CHUNK)
        else:
            taps, outs, ins = slice(0, CHUNK), slice(1, CHUNK + 1), slice(CHUNK - 1, None, -1)
        kern = (jnp.einsum('gpi,gpq->giq', bbr, car[:, :, taps].reshape(GROUPS, NSTATE, n), precision=HI)
                - jnp.einsum('gpi,gpq->giq', bbi, cai[:, :, taps].reshape(GROUPS, NSTATE, n), precision=HI))
        wide = jnp.pad(kern, ((0, 0), (0, 0), (0, last) if reverse else (last, 0)))
        m = jnp.stack([wide[:, :, last - GCH * t:last - GCH * t + n] for t in range(CHUNK)], axis=1)
        qr = jnp.swapaxes(pr[:, :, ins], 1, 2)[:, :, None, :]
        qi = jnp.swapaxes(pi[:, :, ins], 1, 2)[:, :, None, :]
        bbrt, bbit = jnp.swapaxes(bbr, 1, 2)[:, None], jnp.swapaxes(bbi, 1, 2)[:, None]
        pin = jnp.concatenate([qr * bbrt - qi * bbit, qr * bbit + qi * bbrt], axis=-1)
        pout = jnp.concatenate([car[:, :, outs].reshape(GROUPS, NSTATE, n),
                                -cai[:, :, outs].reshape(GROUPS, NSTATE, n)], axis=1)
        return (m.reshape(GROUPS, n, n), pin.reshape(GROUPS, n, 2 * NSTATE), pout, pr[:, :, CHUNK], pi[:, :, CHUNK])

    mf, pinf, poutf, arf, aif = one(lam_re[0], lam_im[0], log_dt[0], b_re[0], b_im[0], c_re[0], c_im[0], False)
    mr, pinr, poutr, arr, air = one(lam_re[1], lam_im[1], log_dt[1], b_re[1], b_im[1], c_re[1], c_im[1], True)
    return (mf + mr, jnp.concatenate([pinf, pinr], 2), jnp.concatenate([poutf, poutr], 1),
            jnp.stack([arf, arr]), jnp.stack([aif, air]))


def _s5_coefs(a_re, a_im):
    c1 = jnp.concatenate([a_re, a_re], -1)
    c2 = jnp.concatenate([-a_im, a_im], -1)
    return tuple(c.reshape(GROUPS // GB, 1, GB * 2 * NSTATE) for c in (c1[0], c2[0], c1[1], c2[1]))


def _swap(s):
    w = s.shape[1]
    lane = lax.broadcasted_iota(jnp.int32, s.shape, 1)
    return jnp.where(lane % (2 * NSTATE) < NSTATE, pltpu.roll(s, w - NSTATE, 1), pltpu.roll(s, NSTATE, 1))


def _group_lanes(g):
    return slice(g * 2 * NSTATE, (g + 1) * 2 * NSTATE)


def _s5_states(nc, u_ref, pin_ref, coef, vf, vr, wf, wr, sf, sr):
    c1f, c2f, c1r, c2r = coef
    for g in range(GB):
        v = jnp.dot(u_ref[g], pin_ref[g], preferred_element_type=f32)
        vf[:, _group_lanes(g)] = v[:, :2 * NSTATE]
        vr[:, _group_lanes(g)] = v[:, 2 * NSTATE:]
    wf[...] = _swap(vf[...])
    wr[...] = _swap(vr[...])

    def step(i, carry):
        s_f, t_f, s_r, t_r = carry
        kf, kr = pl.ds(i, 1), pl.ds(nc - 1 - i, 1)
        sf[kf, :] = s_f
        sr[kr, :] = s_r
        s_f, t_f = c1f * s_f + c2f * t_f + vf[kf, :], c1f * t_f - c2f * s_f + wf[kf, :]
        s_r, t_r = c1r * s_r + c2r * t_r + vr[kr, :], c1r * t_r - c2r * s_r + wr[kr, :]
        return s_f, t_f, s_r, t_r

    z = jnp.zeros((1, GB * 2 * NSTATE), f32)
    lax.fori_loop(0, nc, step, (z, z, z, z))


def _s5_core_fwd(ug, msum, pin, pout, coefs):
    nc = ug.shape[1]
    n = CHUNK * GCH

    def body(u_ref, m_ref, pin_ref, pout_ref, c1f, c2f, c1r, c2r, y_ref, vf, vr, wf, wr, sf, sr):
        coef = (c1f[...], c2f[...], c1r[...], c2r[...])
        _s5_states(nc, u_ref, pin_ref, coef, vf, vr, wf, wr, sf, sr)
        for g in range(GB):
            s_in = jnp.concatenate([sf[:, _group_lanes(g)], sr[:, _group_lanes(g)]], axis=1).astype(bf16)
            y_ref[g] = (jnp.dot(u_ref[g], m_ref[g], preferred_element_type=f32)
                        + jnp.dot(s_in, pout_ref[g], preferred_element_type=f32)).astype(bf16)

    seq = pl.BlockSpec((GB, nc, n), lambda i: (i, 0, 0))
    mat = pl.BlockSpec((GB, n, n), lambda i: (i, 0, 0))
    cf = pl.BlockSpec((None, 1, GB * 2 * NSTATE), lambda i: (i, 0, 0))
    scr = pltpu.VMEM((nc, GB * 2 * NSTATE), f32)
    return pl.pallas_call(
        body, grid=(GROUPS // GB,), in_specs=[seq, mat, mat, mat, cf, cf, cf, cf], out_specs=seq,
        out_shape=jax.ShapeDtypeStruct((GROUPS, nc, n), bf16), scratch_shapes=[scr] * 6,
        name="s5_core_fwd", compiler_params=_params(("parallel",)))(ug, msum, pin, pout, *coefs)


def _s5_core_bwd(ug, dyg, msum, pin, pout, coefs):
    nc = ug.shape[1]
    n = CHUNK * GCH

    def body(u_ref, dy_ref, m_ref, pin_ref, pout_ref, c1f, c2f, c1r, c2r,
             du_ref, dm_ref, dpin_ref, dpout_ref, a1f_ref, a2f_ref, a1r_ref, a2r_ref, vf, vr, wf, wr, sf, sr):
        coef = (c1f[...], c2f[...], c1r[...], c2r[...])
        _s5_states(nc, u_ref, pin_ref, coef, vf, vr, wf, wr, sf, sr)
        for g in range(GB):
            s_in = jnp.concatenate([sf[:, _group_lanes(g)], sr[:, _group_lanes(g)]], axis=1).astype(bf16)
            dy = dy_ref[g]
            ds = lax.dot_general(dy, pout_ref[g], NT, preferred_element_type=f32)
            vf[:, _group_lanes(g)] = ds[:, :2 * NSTATE]
            vr[:, _group_lanes(g)] = ds[:, 2 * NSTATE:]
            dpout_ref[g] = lax.dot_general(s_in, dy, TN, preferred_element_type=f32)
            dm_ref[g] = lax.dot_general(u_ref[g], dy, TN, preferred_element_type=f32)

        wf[...] = _swap(vf[...])
        wr[...] = _swap(vr[...])
        k1f, k2f, k1r, k2r = coef[0], -coef[1], coef[2], -coef[3]

        def step(i, carry):
            g_f, h_f, g_r, h_r, a1f, b2f, a1r, b2r = carry
            kf, kr = pl.ds(nc - 1 - i, 1), pl.ds(i, 1)
            s_f, s_r = sf[kf, :], sr[kr, :]
            sf[kf, :] = g_f
            sr[kr, :] = g_r
            a1f, b2f = a1f + g_f * s_f, b2f + h_f * s_f
            a1r, b2r = a1r + g_r * s_r, b2r + h_r * s_r
            g_f, h_f = vf[kf, :] + k1f * g_f + k2f * h_f, wf[kf, :] + k1f * h_f - k2f * g_f
            g_r, h_r = vr[kr, :] + k1r * g_r + k2r * h_r, wr[kr, :] + k1r * h_r - k2r * g_r
            return g_f, h_f, g_r, h_r, a1f, b2f, a1r, b2r

        z = jnp.zeros((1, GB * 2 * NSTATE), f32)
        _, _, _, _, a1f, b2f, a1r, b2r = lax.fori_loop(0, nc, step, (z,) * 8)
        a1f_ref[...], a2f_ref[...], a1r_ref[...], a2r_ref[...] = a1f, _swap(b2f), a1r, _swap(b2r)
        for g in range(GB):
            dv = jnp.concatenate([sf[:, _group_lanes(g)], sr[:, _group_lanes(g)]], axis=1).astype(bf16)
            du_ref[g] = (lax.dot_general(dy_ref[g], m_ref[g], NT, preferred_element_type=f32)
                         + lax.dot_general(dv, pin_ref[g], NT, preferred_element_type=f32)).astype(bf16)
            dpin_ref[g] = lax.dot_general(u_ref[g], dv, TN, preferred_element_type=f32)

    seq = pl.BlockSpec((GB, nc, n), lambda i: (i, 0, 0))
    mat = pl.BlockSpec((GB, n, n), lambda i: (i, 0, 0))
    cf = pl.BlockSpec((None, 1, GB * 2 * NSTATE), lambda i: (i, 0, 0))
    scr = pltpu.VMEM((nc, GB * 2 * NSTATE), f32)
    mat_s = jax.ShapeDtypeStruct((GROUPS, n, n), f32)
    cf_s = jax.ShapeDtypeStruct((GROUPS // GB, 1, GB * 2 * NSTATE), f32)
    return pl.pallas_call(
        body, grid=(GROUPS // GB,), in_specs=[seq, seq, mat, mat, mat, cf, cf, cf, cf],
        out_specs=[seq, mat, mat, mat, cf, cf, cf, cf],
        out_shape=[jax.ShapeDtypeStruct((GROUPS, nc, n), bf16), mat_s, mat_s, mat_s, cf_s, cf_s, cf_s, cf_s],
        scratch_shapes=[scr] * 6, name="s5_core_bwd",
        compiler_params=_params(("parallel",)))(ug, dyg, msum, pin, pout, *coefs)


def _block_movers():
    a_in, l_in = jnp.divmod(jnp.arange(GB * 128, dtype=jnp.int32), 128)
    a_out, c_out = jnp.divmod(jnp.arange(128, dtype=jnp.int32), GCH)
    j = jnp.arange(GB, dtype=jnp.int32)[:, None, None]
    hit = (a_in[None, :, None] == a_out[None, None, :]) & (l_in[None, :, None] == GCH * j + c_out[None, None, :])
    return hit.astype(bf16)


def _to_groups(x, movers, mask):
    n = x.shape[0]
    nc = n // CHUNK
    half = CHUNK // 2

    def body(x_ref, mv_ref, o_ref):
        keep = lax.broadcasted_iota(jnp.int32, (nc, 1), 0) >= PAD // CHUNK
        steps = [x_ref[pl.ds(t, nc, stride=CHUNK), :] for t in range(CHUNK)]
        if mask:
            steps = [jnp.where(keep, s, 0.0) for s in steps]
        lo = jnp.concatenate(steps[:half], axis=1).astype(bf16)
        hi = jnp.concatenate(steps[half:], axis=1).astype(bf16)
        for g in range(GB):
            o_ref[g] = jnp.concatenate([jnp.dot(lo, mv_ref[g], preferred_element_type=f32),
                                        jnp.dot(hi, mv_ref[g], preferred_element_type=f32)], axis=1).astype(bf16)

    return pl.pallas_call(
        body, grid=(GROUPS // GB,),
        in_specs=[pl.BlockSpec((n, 128), lambda i: (0, i)), pl.BlockSpec(movers.shape, lambda i: (0, 0, 0))],
        out_specs=pl.BlockSpec((GB, nc, CHUNK * GCH), lambda i: (i, 0, 0)),
        out_shape=jax.ShapeDtypeStruct((GROUPS, nc, CHUNK * GCH), bf16), name="s5_to_groups",
        compiler_params=_params(("parallel",)))(x, movers)


def _from_groups(y, movers, base=None):
    nc = y.shape[1]
    n = nc * CHUNK
    half = CHUNK // 2

    def body(*refs):
        y_ref, mv_ref = refs[:2]
        o_ref = refs[-1]
        keep = lax.broadcasted_iota(jnp.int32, (nc, 1), 0) >= PAD // CHUNK
        lo = jnp.concatenate([y_ref[g][:, :128] for g in range(GB)], axis=1)
        hi = jnp.concatenate([y_ref[g][:, 128:] for g in range(GB)], axis=1)
        for t in range(CHUNK):
            rows = pl.ds(t, nc, stride=CHUNK)
            v = jnp.dot(lo if t < half else hi, mv_ref[t % half], preferred_element_type=f32)
            if base is not None:
                v = refs[2][rows, :] + jnp.where(keep, v, 0.0)
            o_ref[rows, :] = v

    tok = pl.BlockSpec((n, 128), lambda i: (0, i))
    args = (y, movers) if base is None else (y, movers, base)
    return pl.pallas_call(
        body, grid=(GROUPS // GB,),
        in_specs=[pl.BlockSpec((GB, nc, CHUNK * GCH), lambda i: (i, 0, 0)),
                  pl.BlockSpec(movers.shape, lambda i: (0, 0, 0))] + ([] if base is None else [tok]),
        out_specs=tok, out_shape=jax.ShapeDtypeStruct((n, D), f32), name="s5_from_groups",
        compiler_params=_params(("parallel",)))(*args)


def _gelu(y):
    return 0.5 * y * (1.0 + lax.erf(y * (2.0 ** -0.5)))


def _gelu_grad(y):
    return 0.5 * (1.0 + lax.erf(y * (2.0 ** -0.5))) + y * jnp.exp(-0.5 * y * y) * (1.0 / math.sqrt(2.0 * math.pi))


def _s5_fwd(h, movers, mats, d_skip, w_glu, w_out):
    msum, pin, pout, a_re, a_im = mats
    coefs = _s5_coefs(a_re, a_im)
    ug = _to_groups(h, movers, mask=True)
    ys = _from_groups(_s5_core_fwd(ug, msum.astype(bf16), pin.astype(bf16), pout.astype(bf16), coefs), movers)

    def post(i, ys, h, d):
        y = ys + d * h
        return (y, _gelu(y)), ()

    y, g_bf = _ew("s5_gelu", post, [ys, h], [d_skip], [(D, f32), (D, bf16)])
    gw = _mm("s5_glu_mm", g_bf, w_glu)

    def glu(i, y, gw):
        return (_gelu(y) * _sigmoid(gw),), ()

    z_bf = _ew("s5_glu", glu, [y, gw], [], [(D, bf16)])[0]
    mix = _mm("s5_out_mm", z_bf, w_out)
    return mix, (ug, y, g_bf, gw, z_bf)


def _s5_bwd(dmix_bf, h, movers, saved, mats, vjp_mats, d_skip, w_glu, w_out):
    ug, y, g_bf, gw, z_bf = saved
    msum, pin, pout, a_re, a_im = mats
    coefs = _s5_coefs(a_re, a_im)
    dz = _mm("s5_dz_mm", dmix_bf, w_out, trans_b=True)
    d_w_out = _mm_tn("s5_dwout", z_bf, dmix_bf)

    def dglu(i, dz, y, gw):
        g, s = _gelu(y), _sigmoid(gw)
        return (dz * g * s * (1.0 - s), dz * s), ()

    dgw_bf, dg1 = _ew("s5_dglu", dglu, [dz, y, gw], [], [(D, bf16), (D, f32)])
    d_w_glu = _mm_tn("s5_dwglu", g_bf, dgw_bf)
    dg2 = _mm("s5_dg_mm", dgw_bf, w_glu, trans_b=True)

    def dgelu(i, dg1, dg2, y, h, d):
        dy = (dg1 + dg2) * _gelu_grad(y)
        return (dy, dy * d), (jnp.sum(dy * h, axis=0, keepdims=True),)

    dy, dh_skip, dd = _ew("s5_dgelu", dgelu, [dg1, dg2, y, h], [d_skip], [(D, f32), (D, f32)], [(1, D)])
    dug, dm, dpin, dpout, a1f, a2f, a1r, a2r = _s5_core_bwd(
        ug, _to_groups(dy, movers, mask=False), msum.astype(bf16), pin.astype(bf16), pout.astype(bf16), coefs)
    dh = _from_groups(dug, movers, base=dh_skip)
    a1 = jnp.stack([a1f, a1r]).reshape(2, GROUPS, 2 * NSTATE)
    a2 = jnp.stack([a2f, a2r]).reshape(2, GROUPS, 2 * NSTATE)
    da_re = a1[..., :NSTATE] + a1[..., NSTATE:]
    da_im = a2[..., NSTATE:] - a2[..., :NSTATE]
    d_params = vjp_mats((dm, dpin, dpout, da_re, da_im))
    return dh, d_params, dd[0], d_w_glu, d_w_out


def _rope_tables(n):
    row = jnp.arange(n, dtype=jnp.int32) - OFF
    real = row >= 0
    rid = jnp.where(real, row // GRID_W, 0).astype(f32)
    cid = jnp.where(real, row % GRID_W, 0).astype(f32)
    half = HD // 2
    inv = ROPE_THETA ** (-jnp.arange(0, half, 2, dtype=f32) / half)
    ar, ac = rid[:, None] * inv[None, :], cid[:, None] * inv[None, :]
    cos = jnp.concatenate([jnp.cos(ar), jnp.cos(ar), jnp.cos(ac), jnp.cos(ac)], axis=1)
    sin = jnp.concatenate([-jnp.sin(ar), jnp.sin(ar), -jnp.sin(ac), jnp.sin(ac)], axis=1)
    return jnp.tile(cos, (1, 2)), jnp.tile(sin, (1, 2))


def _head_mats():
    head = jnp.arange(QW, dtype=jnp.int32)[:, None] // HD == jnp.arange(128, dtype=jnp.int32)[None, :]
    return head.astype(f32) * (1.0 / HD), head.astype(f32).T


def _rot(v):
    w = v.shape[1]
    lane = lax.broadcasted_iota(jnp.int32, v.shape, 1)
    return jnp.where(lane % 32 < 16, pltpu.roll(v, w - 16, 1), pltpu.roll(v, 16, 1))


def _head_mean(v, e, et):
    w = v.shape[1]
    m = jnp.dot(v, e[:w], preferred_element_type=f32, precision=HIGH)
    return m, et[:, :w]


def _rms_rope(t, gain, e, et, cos, sin):
    w = t.shape[1]
    ms, spread = _head_mean(t * t, e, et)
    rs = jnp.dot(lax.rsqrt(ms + QK_EPS), spread, preferred_element_type=f32, precision=HIGH)
    n0 = t * rs
    n = n0 * gain
    reps = w // 128
    return n * jnp.tile(cos, (1, reps)) + _rot(n) * jnp.tile(sin, (1, reps))


def _rms_rope_bwd(dout, t, gain, e, et, cos, sin):
    w = t.shape[1]
    reps = w // 128
    ms, spread = _head_mean(t * t, e, et)
    rs = jnp.dot(lax.rsqrt(ms + QK_EPS), spread, preferred_element_type=f32, precision=HIGH)
    n0 = t * rs
    dn = dout * jnp.tile(cos, (1, reps)) + _rot(dout * jnp.tile(sin, (1, reps)))
    dn0 = dn * gain
    mm, _ = _head_mean(dn0 * n0, e, et)
    corr = jnp.dot(mm, spread, preferred_element_type=f32, precision=HIGH)
    return rs * (dn0 - n0 * corr), jnp.sum(dn * n0, axis=0, keepdims=True)


def _qk_fwd(qkv, qg, kg, e, et, cos, sin):
    def fn(i, qkv, cos, sin, qg, kg, e, et):
        q = _rms_rope(qkv[:, :QW], qg, e, et, cos, sin) * Q_SCALE
        k = _rms_rope(qkv[:, QW:QW + KW], kg, e, et, cos, sin)
        return (q, k, qkv[:, QW + KW:]), ()

    return _ew("qk_rope", fn, [qkv, cos, sin], [qg, kg, e, et], [(QW, bf16), (KW, bf16), (KW, bf16)])


def _qk_bwd(qkv, dq, dk, dv, qg, kg, e, et, cos, sin):
    def fn(i, qkv, cos, sin, dq, dk, dv, qg, kg, e, et):
        dtq, dgq = _rms_rope_bwd(dq * (HD ** -0.5), qkv[:, :QW], qg, e, et, cos, sin)
        dtk, dgk = _rms_rope_bwd(dk * math.log(2.0), qkv[:, QW:QW + KW], kg, e, et, cos, sin)
        return (jnp.concatenate([dtq, dtk, dv], axis=1),), (dgq, dgk)

    return _ew("qk_rope_bwd", fn, [qkv, cos, sin, dq, dk, dv], [qg, kg, e, et], [(QKVW, bf16)], [(1, QW), (1, KW)])


def _to_heads(a, nh):
    return a.reshape(a.shape[0], nh, HD).transpose(1, 0, 2)


def _from_heads(a):
    return a.transpose(1, 0, 2).reshape(a.shape[1], a.shape[0] * HD)


def _masked_first(s, c):
    if c:
        return s
    col = lax.broadcasted_iota(jnp.int32, (1, s.shape[1]), 1)
    return jnp.where(col >= PAD, s, NEG)


def _flash_fwd(q, k, v1, tq=ROW_TILE, tc=KEY_CHUNK):
    n = q.shape[1]
    nc = n // tc

    def body(q_ref, k_ref, v_ref, o_ref, lse_ref):
        qb = q_ref[0]

        def scores(c):
            ks = k_ref[0, pl.ds(c * tc, tc), :]
            return _masked_first(lax.dot_general(qb, ks, NT, preferred_element_type=f32), c)

        m = jnp.full((tq, 1), NEG, f32)
        acc = jnp.zeros((tq, 2 * HD), f32)
        s_next = scores(0)
        for c in range(nc):
            s = s_next
            if c + 1 < nc:
                s_next = scores(c + 1)
            m_new = jnp.maximum(m, jnp.max(s, axis=1, keepdims=True))
            p = jnp.exp2(s - m_new)
            acc = jnp.exp2(m - m_new) * acc + jnp.dot(p.astype(bf16), v_ref[0, pl.ds(c * tc, tc), :],
                                                      preferred_element_type=f32)
            m = m_new
        l = acc[:, HD:HD + 1]
        o_ref[0] = acc[:, :HD] / l
        lse_ref[0] = m + jnp.log2(l)

    return pl.pallas_call(
        body, grid=(NQ, n // tq),
        in_specs=[pl.BlockSpec((1, tq, HD), lambda h, i: (h, i, 0)),
                  pl.BlockSpec((1, n, HD), lambda h, i: (h // (NQ // NKV), 0, 0)),
                  pl.BlockSpec((1, n, 2 * HD), lambda h, i: (h // (NQ // NKV), 0, 0))],
        out_specs=[pl.BlockSpec((1, tq, HD), lambda h, i: (h, i, 0)), pl.BlockSpec((1, tq, 1), lambda h, i: (h, i, 0))],
        out_shape=[jax.ShapeDtypeStruct((NQ, n, HD), f32), jax.ShapeDtypeStruct((NQ, n, 1), f32)],
        name="flash_fwd", compiler_params=_params(("parallel", "parallel")))(q, k, v1)


def _flash_bwd(q, k, kt, v, do, lse_row, delta_row, tk=ROW_TILE, tc=KEY_CHUNK):
    n = q.shape[1]
    nc = n // tc
    grp = NQ // NKV

    def body(q_ref, do_ref, lse_ref, delta_ref, k_ref, kt_ref, v_ref, dqt_ref, dk_ref, dv_ref):
        j, g = pl.program_id(1), pl.program_id(2)
        kb, vb, ktb = k_ref[0], v_ref[0], kt_ref[0]
        valid = lax.broadcasted_iota(jnp.int32, (tk, 1), 0) + j * tk >= PAD

        @pl.when(j == 0)
        def _():
            dqt_ref[g] = jnp.zeros((HD, n), f32)

        def products(c):
            rows = pl.ds(c * tc, tc)
            return (lax.dot_general(kb, q_ref[0, rows, :], NT, preferred_element_type=f32),
                    lax.dot_general(vb, do_ref[0, rows, :], NT, preferred_element_type=f32))

        dk = jnp.zeros((tk, HD), f32)
        dv = jnp.zeros((tk, HD), f32)
        nxt = products(0)
        for c in range(nc):
            st, dpt = nxt
            if c + 1 < nc:
                nxt = products(c + 1)
            rows = pl.ds(c * tc, tc)
            pt = jnp.exp2(jnp.where(valid, st, NEG) - lse_ref[0, :, rows])
            dv = dv + jnp.dot(pt.astype(bf16), do_ref[0, rows, :], preferred_element_type=f32)
            dst = (pt * (dpt - delta_ref[0, :, rows])).astype(bf16)
            dk = dk + jnp.dot(dst, q_ref[0, rows, :], preferred_element_type=f32)
            dqt_ref[g, :, rows] += jnp.dot(ktb, dst, preferred_element_type=f32)

        @pl.when(g == 0)
        def _():
            dk_ref[0] = dk
            dv_ref[0] = dv

        @pl.when(g > 0)
        def _():
            dk_ref[0] += dk
            dv_ref[0] += dv

    hspec = pl.BlockSpec((1, n, HD), lambda h, j, g: (h * grp + g, 0, 0))
    rspec = pl.BlockSpec((1, 1, n), lambda h, j, g: (h * grp + g, 0, 0))
    kspec = pl.BlockSpec((1, tk, HD), lambda h, j, g: (h, j, 0))
    return pl.pallas_call(
        body, grid=(NKV, n // tk, grp),
        in_specs=[hspec, hspec, rspec, rspec, kspec, pl.BlockSpec((1, HD, tk), lambda h, j, g: (h, 0, j)), kspec],
        out_specs=[pl.BlockSpec((grp, HD, n), lambda h, j, g: (h, 0, 0)), kspec, kspec],
        out_shape=[jax.ShapeDtypeStruct((NQ, HD, n), f32)] + [jax.ShapeDtypeStruct((NKV, n, HD), f32)] * 2,
        name="flash_bwd", compiler_params=_params(("parallel", "arbitrary", "arbitrary")))(
            q, do, lse_row, delta_row, k, kt, v)


def _attn_fwd(h_bf, w_qkv_t, qg, kg, w_out, tabs):
    e, et, cos, sin = tabs
    qkv = _mm("attn_qkv_mm", h_bf, w_qkv_t, trans_b=True)
    q_bf, k_bf, v_bf = _qk_fwd(qkv, qg, kg, e, et, cos, sin)
    q16, k4, v4 = _to_heads(q_bf, NQ), _to_heads(k_bf, NKV), _to_heads(v_bf, NKV)
    ones = jnp.zeros((NKV, v4.shape[1], HD), bf16).at[:, :, 0].set(1.0)
    o16, lse = _flash_fwd(q16, k4, jnp.concatenate([v4, ones], axis=2))
    o = _from_heads(o16)
    o_bf = o.astype(bf16)
    mix = _mm("attn_out_mm", o_bf, w_out)
    return mix, (qkv, q16, k4, v4, o, lse, o_bf)


def _attn_bwd(dmix_bf, h_bf, saved, w_qkv_t, qg, kg, w_out, tabs):
    e, et, cos, sin = tabs
    qkv, q16, k4, v4, o, lse, o_bf = saved
    n = qkv.shape[0]
    do = _mm("attn_do_mm", dmix_bf, w_out, trans_b=True, out_dtype=bf16)
    d_w_out = _mm_tn("attn_dwout", o_bf, dmix_bf)

    def head_dots(i, do, o, e):
        return (jnp.dot(do.astype(f32) * o, e, preferred_element_type=f32, precision=HIGH) * HD,), ()

    delta = _ew("attn_delta", head_dots, [do, o], [e], [(128, f32)])[0]
    dqt, dk4, dv4 = _flash_bwd(q16, k4, k4.transpose(0, 2, 1), v4, _to_heads(do, NQ), lse.reshape(NQ, 1, n),
                               delta[:, :NQ].T.reshape(NQ, 1, n))
    dq = dqt.transpose(2, 0, 1).reshape(n, QW)
    dqkv_bf, dgq, dgk = _qk_bwd(qkv, dq, _from_heads(dk4), _from_heads(dv4), qg, kg, e, et, cos, sin)
    d_w_qkv_t = _mm_tn("attn_dwqkv", dqkv_bf, h_bf)
    dh = _mm("attn_dh_mm", dqkv_bf, w_qkv_t)
    return dh, d_w_qkv_t, dgq.reshape(NQ, HD).sum(0), dgk.reshape(NKV, HD).sum(0), d_w_out


def _all_gather(name, shard):
    def body(x_ref, out_ref, send_sems, recv_sems, local_sem):
        x, y, c = lax.axis_index("x"), lax.axis_index("y"), lax.axis_index("c")
        me, sibling = (x, y, c), (x, y, 1 - c)
        chips = [(1 - x, y), (x, 1 - y), (1 - x, 1 - y)]

        def slot(px, py, pc):
            return out_ref.at[4 * px + 2 * py + pc]

        def copy(k, block, to, src=None):
            return pltpu.make_async_remote_copy(
                src_ref=slot(*block) if src is None else src, dst_ref=slot(*block),
                send_sem=send_sems.at[k], recv_sem=recv_sems.at[k], device_id=to, device_id_type=MESH)

        mine = pltpu.make_async_copy(x_ref, slot(*me), local_sem)
        mine.start()
        first = [copy(0, me, sibling, src=x_ref)]
        first += [copy(1 + j, me, (*chip, c), src=x_ref) for j, chip in enumerate(chips)]
        for cp in first:
            cp.start()
        passed = [copy(4 + j, (*chip, c), sibling) for j, chip in enumerate(chips)]
        for j, chip in enumerate(chips):
            copy(1 + j, (*chip, c), me).wait_recv()
            passed[j].start()
        copy(0, sibling, me).wait_recv()
        for j, chip in enumerate(chips):
            copy(4 + j, (*chip, 1 - c), me).wait_recv()
        for cp in first + passed:
            cp.wait_send()
        mine.wait()

    return pl.pallas_call(
        body, out_shape=jax.ShapeDtypeStruct((8,) + shard.shape, shard.dtype), in_specs=[ANY], out_specs=ANY,
        scratch_shapes=[pltpu.SemaphoreType.DMA((7,)), pltpu.SemaphoreType.DMA((7,)), pltpu.SemaphoreType.DMA],
        name=name)(shard)


def _swap_sibling(name, theirs):
    k = len(theirs)

    def body(*refs):
        src, dst, send_sems, recv_sems = refs[:k], refs[k:2 * k], refs[2 * k], refs[2 * k + 1]
        x, y, c = lax.axis_index("x"), lax.axis_index("y"), lax.axis_index("c")
        copies = [pltpu.make_async_remote_copy(src_ref=src[j], dst_ref=dst[j], send_sem=send_sems.at[j],
                                               recv_sem=recv_sems.at[j], device_id=(x, y, 1 - c), device_id_type=MESH)
                  for j in range(k)]
        for cp in copies:
            cp.start()
        for cp in copies:
            cp.wait()

    return pl.pallas_call(
        body, out_shape=[jax.ShapeDtypeStruct(a.shape, a.dtype) for a in theirs], in_specs=[ANY] * k,
        out_specs=[ANY] * k, scratch_shapes=[pltpu.SemaphoreType.DMA((k,)), pltpu.SemaphoreType.DMA((k,))],
        name=name)(*theirs)


def _exchange_chips(name, parts):
    k = len(parts)

    def body(*refs):
        p_refs, t_refs = refs[:k], refs[k:2 * k]
        send_sems, recv_sems, local_sems = refs[2 * k:]
        x, y, c = lax.axis_index("x"), lax.axis_index("y"), lax.axis_index("c")
        q = 2 * x + y
        copies = []
        for j in range(k):
            copies.append(pltpu.make_async_copy(p_refs[j].at[q], t_refs[j].at[q], local_sems.at[j]))
            for hop in (1, 2, 3):
                tx, ty = x ^ (hop >> 1), y ^ (hop & 1)
                copies.append(pltpu.make_async_remote_copy(
                    src_ref=p_refs[j].at[2 * tx + ty], dst_ref=t_refs[j].at[q], send_sem=send_sems.at[3 * j + hop - 1],
                    recv_sem=recv_sems.at[3 * j + hop - 1], device_id=(tx, ty, c), device_id_type=MESH))
        for cp in copies:
            cp.start()
        for cp in copies:
            cp.wait()

    return pl.pallas_call(
        body, out_shape=[jax.ShapeDtypeStruct(a.shape, a.dtype) for a in parts], in_specs=[ANY] * k,
        out_specs=[ANY] * k,
        scratch_shapes=[pltpu.SemaphoreType.DMA((3 * k,)), pltpu.SemaphoreType.DMA((3 * k,)),
                        pltpu.SemaphoreType.DMA((k,))],
        name=name)(*parts)


def _reduce_scatter(mine, theirs):
    got = _swap_sibling("rs_sibling", list(theirs))
    parts = []
    for a, b, dt, nm in zip(mine, got, (bf16, f32), ("rs_add2", "rs_add2_small")):
        rows = 4 * a.shape[1]
        parts.append(_ew(nm, lambda i, a, b: ((a + b,), ()), [a.reshape(rows, D), b.reshape(rows, D)], [],
                         [(D, dt)], tile=RS_TILE)[0].reshape(a.shape))
    ts = _exchange_chips("rs_chips", parts)

    def add4(i, a, b, c, d):
        return ((((a.astype(f32) + b.astype(f32)) + c.astype(f32)) + d.astype(f32),), ())

    return [_ew(nm, add4, [(t, 0), (t, 1), (t, 2), (t, 3)], [], [(D, f32)], tile=RS_TILE)[0]
            for t, nm in zip(ts, ("rs_add4", "rs_add4_small"))]


def _pack_rows(parts, rows):
    flat = jnp.concatenate([p.reshape(-1) for p in parts])
    return jnp.pad(flat, (0, rows * D - flat.shape[0])).reshape(rows, D)


def _unpack(flat, shapes):
    out, off = [], 0
    for s in shapes:
        n = math.prod(s)
        out.append(flat[off:off + n].reshape(s))
        off += n
    return out


def _mat_rows(block, transposed, blk):
    a = jnp.swapaxes(block, 1, 2) if transposed else block
    a = jnp.pad(a, ((0, 0), (0, blk - a.shape[1]), (0, 0)))
    return a.reshape(-1, D)


def _mat_block(rows, transposed, blk, real):
    a = rows.reshape(-1, blk, D)[:, :real]
    return jnp.swapaxes(a, 1, 2) if transposed else a


def _mat_full(gathered, blk):
    layers = gathered.shape[1] // blk
    return gathered.reshape(8, layers, blk, D).transpose(1, 0, 2, 3).reshape(layers, 8 * blk, D)


def _vec_full(gathered):
    return gathered.transpose(1, 0, 2).reshape(gathered.shape[1], D)


def _grad_slots(full, small, cc):
    def halves(a, blk):
        a4 = a.reshape(4, 2, blk, D)
        return [lax.dynamic_index_in_dim(a4, sel, axis=1, keepdims=False) for sel in (cc, 1 - cc)]

    mine, theirs = [], []
    for name, _, blk, _ in MATS:
        for layer in full[name]:
            a, b = halves(layer, blk)
            mine.append(a)
            theirs.append(b)
    a, b = halves(small, REP_PIECE)
    return (jnp.concatenate(mine, axis=1), a), (jnp.concatenate(theirs, axis=1), b)


def _local_step(x0, target0, w, fw):
    seq = x0.shape[0]
    n = OFF + seq
    movers = _block_movers()
    h = jnp.concatenate([jnp.zeros((PAD, D), f32), fw['meta_tokens'], x0], axis=0)
    h_bf = h.astype(bf16)
    tabs = _head_mats() + _rope_tables(n)
    qg = [jnp.tile(w['attn_q_gain'][j], NQ)[None, :] for j in range(2)]
    kg = [jnp.tile(w['attn_k_gain'][j], NKV)[None, :] for j in range(2)]
    s5_names = ['s5_lambda_re', 's5_lambda_im', 's5_log_dt', 's5_b_re', 's5_b_im', 's5_c_re', 's5_c_im']
    s5_mats, s5_vjp = [], []
    for j in range(2):
        mats, vjp = jax.vjp(_s5_mats, *[w[k][j] for k in s5_names])
        s5_mats.append(mats)
        s5_vjp.append(vjp)
    saved = []
    for i in range(DEPTH):
        j = i // 2
        if i % 2 == 0:
            mix, sv = _s5_fwd(h, movers, s5_mats[j], w['s5_d'][j][None, :], (fw['s5_w_glu'], j), (fw['s5_w_out'], j))
        else:
            mix, sv = _attn_fwd(h_bf, (fw['attn_w_qkv'], j), qg[j], kg[j], (fw['attn_w_out'], j), tabs)
        r1, h1, h1_bf = _ln_fwd(h, mix, fw['ln_gain'][i, 0][None, :], fw['ln_bias'][i, 0][None, :])
        gate, up, act = _ffn_up(h1_bf, (fw['ffn_w_gate'], i), (fw['ffn_w_up'], i))
        f = _mm("ffn_down_mm", act, (fw['ffn_w_down'], i))
        r2, h2, h2_bf = _ln_fwd(h1, f, fw['ln_gain'][i, 1][None, :], fw['ln_bias'][i, 1][None, :])
        saved.append((h, h_bf, sv, r1, h1_bf, gate, up, act, r2))
        h, h_bf = h2, h2_bf

    d_b, sq = _loss_grad(h, target0)
    loss = 0.5 * jnp.sum(sq) * (1.0 / D)

    grads = {k: [None] * (DEPTH if k.startswith('ffn') else 2) for k in WEIGHTS}
    d_ln_gain = [[None, None] for _ in range(DEPTH)]
    d_ln_bias = [[None, None] for _ in range(DEPTH)]
    d_a = None
    for i in reversed(range(DEPTH)):
        j = i // 2
        h_in, h_in_bf, sv, r1, h1_bf, gate, up, act, r2 = saved[i]
        dr2, dr2_bf, dg, db = _ln_bwd(d_a, d_b, r2, fw['ln_gain'][i, 1][None, :])
        d_ln_gain[i][1], d_ln_bias[i][1] = dg[0], db[0]
        dgate, dup = _ffn_dup(dr2_bf, (fw['ffn_w_down'], i), gate, up)
        grads['ffn_w_down'][i] = _mm_tn("ffn_dwdown", act, dr2_bf, tk=DFFP // 2)
        grads['ffn_w_gate'][i] = _mm_tn("ffn_dwgate", dgate, h1_bf, tk=DFFP // 2)
        grads['ffn_w_up'][i] = _mm_tn("ffn_dwup", dup, h1_bf, tk=DFFP // 2)
        dh1 = _mm2("ffn_dh_mm", dgate, (fw['ffn_w_gate'], i), dup, (fw['ffn_w_up'], i))
        dr1, dr1_bf, dg, db = _ln_bwd(dr2, dh1, r1, fw['ln_gain'][i, 0][None, :])
        d_ln_gain[i][0], d_ln_bias[i][0] = dg[0], db[0]
        if i % 2 == 0:
            dh, d_par, dd, d_w_glu, d_w_out = _s5_bwd(dr1_bf, h_in, movers, sv, s5_mats[j], s5_vjp[j],
                                                      w['s5_d'][j][None, :], (fw['s5_w_glu'], j), (fw['s5_w_out'], j))
            for k, g in zip(s5_names, d_par):
                grads[k][j] = g
            grads['s5_d'][j], grads['s5_w_glu'][j], grads['s5_w_out'][j] = dd, d_w_glu, d_w_out
        else:
            dh, d_w_qkv, dgq, dgk, d_w_out = _attn_bwd(dr1_bf, h_in_bf, sv, (fw['attn_w_qkv'], j), qg[j], kg[j],
                                                       (fw['attn_w_out'], j), tabs)
            grads['attn_w_qkv'][j], grads['attn_w_out'][j] = d_w_qkv, d_w_out
            grads['attn_q_gain'][j], grads['attn_k_gain'][j] = dgq, dgk
        d_a, d_b = dr1, dh
    dh0 = _ew("dh0", lambda i, a, b: ((ALPHA * a + b,), ()), [d_a, d_b], [], [(D, f32)])[0]
    mats = {m[0] for m in MATS}
    full = {k: (v if k in mats else jnp.stack(v)) for k, v in grads.items() if v[0] is not None}
    full['meta_tokens'] = dh0[PAD:OFF]
    full['ln_gain'] = jnp.stack([jnp.stack(r) for r in d_ln_gain])
    full['ln_bias'] = jnp.stack([jnp.stack(r) for r in d_ln_bias])

    return loss, dh0[OFF:], full


def kernel(x, meta_tokens, s5_lambda_re, s5_lambda_im, s5_log_dt, s5_b_re, s5_b_im, s5_c_re, s5_c_im, s5_d, s5_w_glu, s5_w_out, attn_w_qkv, attn_q_gain, attn_k_gain, attn_w_out, ffn_w_gate, ffn_w_up, ffn_w_down, ln_gain, ln_bias, loss_target, m_meta_tokens, m_s5_lambda_re, m_s5_lambda_im, m_s5_log_dt, m_s5_b_re, m_s5_b_im, m_s5_c_re, m_s5_c_im, m_s5_d, m_s5_w_glu, m_s5_w_out, m_attn_w_qkv, m_attn_q_gain, m_attn_k_gain, m_attn_w_out, m_ffn_w_gate, m_ffn_w_up, m_ffn_w_down, m_ln_gain, m_ln_bias, v_meta_tokens, v_s5_lambda_re, v_s5_lambda_im, v_s5_log_dt, v_s5_b_re, v_s5_b_im, v_s5_c_re, v_s5_c_im, v_s5_d, v_s5_w_glu, v_s5_w_out, v_attn_w_qkv, v_attn_q_gain, v_attn_k_gain, v_attn_w_out, v_ffn_w_gate, v_ffn_w_up, v_ffn_w_down, v_ln_gain, v_ln_bias):
    w = dict(zip(WEIGHTS, (meta_tokens, s5_lambda_re, s5_lambda_im, s5_log_dt, s5_b_re, s5_b_im, s5_c_re, s5_c_im, s5_d, s5_w_glu, s5_w_out, attn_w_qkv, attn_q_gain, attn_k_gain, attn_w_out, ffn_w_gate, ffn_w_up, ffn_w_down, ln_gain, ln_bias)))
    mom = dict(zip(WEIGHTS, (m_meta_tokens, m_s5_lambda_re, m_s5_lambda_im, m_s5_log_dt, m_s5_b_re, m_s5_b_im, m_s5_c_re, m_s5_c_im, m_s5_d, m_s5_w_glu, m_s5_w_out, m_attn_w_qkv, m_attn_q_gain, m_attn_k_gain, m_attn_w_out, m_ffn_w_gate, m_ffn_w_up, m_ffn_w_down, m_ln_gain, m_ln_bias)))
    vel = dict(zip(WEIGHTS, (v_meta_tokens, v_s5_lambda_re, v_s5_lambda_im, v_s5_log_dt, v_s5_b_re, v_s5_b_im, v_s5_c_re, v_s5_c_im, v_s5_d, v_s5_w_glu, v_s5_w_out, v_attn_w_qkv, v_attn_q_gain, v_attn_k_gain, v_attn_w_out, v_ffn_w_gate, v_ffn_w_up, v_ffn_w_down, v_ln_gain, v_ln_bias)))
    cc = lax.axis_index("c")
    dev = 4 * lax.axis_index("x") + 2 * lax.axis_index("y") + cc

    mat_rows = jnp.concatenate([_mat_rows(w[n], t, blk) for n, t, blk, _ in MATS]).astype(bf16)
    g_mats = _all_gather("ag_weights", mat_rows)
    g_vecs = _all_gather("ag_vectors", jnp.concatenate([w[n].reshape(-1, 128) for n in VECS]))
    fw, off = {}, 0
    for n, _, blk, _ in MATS:
        rows = w[n].shape[0] * blk
        fw[n] = _mat_full(g_mats[:, off:off + rows], blk)
        off += rows
    off = 0
    for n in VECS:
        rows = w[n].size // 128
        fw[n] = _vec_full(g_vecs[:, off:off + rows]).reshape(w[n].shape[:-1] + (D,))
        off += rows

    loss, grad_x, full = _local_step(x[0], loss_target[0], w, fw)
    loss = lax.psum(loss, AXES)
    grad_x = grad_x[None]

    small_names = REPL + VECS
    mine, theirs = _grad_slots(full, _pack_rows([full[k] for k in small_names], REP_ROWS), cc)
    red, red_small = _reduce_scatter(mine, theirs)
    small_all = _all_gather("ag_small_grads", red_small).reshape(REP_ROWS * D)
    g, off = {}, 0
    for n, t, blk, real in MATS:
        rows = w[n].shape[0] * blk
        g[n] = _mat_block(red[off:off + rows], t, blk, real)
        off += rows
    small = dict(zip(small_names, _unpack(small_all, [full[k].shape for k in small_names])))
    for k in REPL:
        g[k] = small[k]
    for k in VECS:
        g[k] = lax.dynamic_slice_in_dim(small[k], dev * 128, 128, axis=small[k].ndim - 1)

    delta, new_m, new_v = {}, {}, {}
    for n in WEIGHTS:
        shp = w[n].shape
        res = _adamw(*[d[n].reshape(-1, shp[-1]) for d in (w, g, mom, vel)])
        delta[n], new_m[n], new_v[n] = [a.reshape(shp) for a in res]
    return (loss, grad_x, *[g[k] for k in WEIGHTS], *[delta[k] for k in WEIGHTS],
            *[new_m[k] for k in WEIGHTS], *[new_v[k] for k in WEIGHTS])
```

```python
import functools
import math

import jax
import jax.numpy as jnp
from jax import lax
from jax.experimental import pallas as pl
from jax.experimental.pallas import tpu as pltpu

f32 = jnp.float32
bf16 = jnp.bfloat16
HIGH = lax.Precision.HIGH
MESH = pl.DeviceIdType.MESH
AXES = ("x", "y", "c")
ANY = pl.BlockSpec(memory_space=pl.ANY)

D = 1024
DEPTH = 4
N_META = 16
PAD = 240
OFF = PAD + N_META
ROW_TILE = 768
KEY_CHUNK = 256
FFN_TILE = 256
ADAM_TILE = 544
GRID_W = 64
HD = 64
NQ = 16
NKV = 4
QW = NQ * HD
KW = NKV * HD
QKVW = QW + 2 * KW
DFF = 2816
GROUPS = 64
GCH = 16
NSTATE = 64
CHUNK = 16
GB = 8
ROPE_THETA = 10000.0
LN_EPS = 1e-5
QK_EPS = 1e-6
ALPHA = (2.0 * DEPTH) ** 0.25
ADAM_LR, ADAM_B1, ADAM_B2, ADAM_EPS, ADAM_WD, ADAM_STEP = 0.001, 0.9, 0.999, 1e-08, 0.01, 10
NEG = -1e30
Q_SCALE = HD ** -0.5 * math.log2(math.e)
VMEM_MB = 56

NT = (((1,), (1,)), ((), ()))
TN = (((0,), (0,)), ((), ()))

WEIGHTS = ['meta_tokens', 's5_lambda_re', 's5_lambda_im', 's5_log_dt', 's5_b_re', 's5_b_im', 's5_c_re', 's5_c_im',
           's5_d', 's5_w_glu', 's5_w_out', 'attn_w_qkv', 'attn_q_gain', 'attn_k_gain', 'attn_w_out', 'ffn_w_gate',
           'ffn_w_up', 'ffn_w_down', 'ln_gain', 'ln_bias']
DFFP = 3072
FF_BLK, FF_BLKP = DFF // 8, DFFP // 8
MATS = [('s5_w_glu', False, 128, 128), ('s5_w_out', False, 128, 128), ('attn_w_qkv', True, 192, 192),
        ('attn_w_out', False, 128, 128), ('ffn_w_gate', True, FF_BLKP, FF_BLK), ('ffn_w_up', True, FF_BLKP, FF_BLK),
        ('ffn_w_down', False, FF_BLKP, FF_BLK)]
VECS = ['meta_tokens', 'ln_gain', 'ln_bias']
REPL = ['s5_lambda_re', 's5_lambda_im', 's5_log_dt', 's5_b_re', 's5_b_im', 's5_c_re', 's5_c_im', 's5_d',
        'attn_q_gain', 'attn_k_gain']
MAT_ROWS = 5760
REP_PIECE = 160
REP_ROWS = 8 * REP_PIECE
RS_TILE = 640


def _params(sem, mb=VMEM_MB):
    return pltpu.CompilerParams(dimension_semantics=sem, vmem_limit_bytes=mb << 20)


def _ew(name, fn, rows, consts, outs, accs=(), tile=ROW_TILE):
    first = rows[0][0] if isinstance(rows[0], tuple) else rows[0]
    n = first.shape[-2]
    tile = min(tile, n)
    assert n % tile == 0, (name, n, tile)
    n_in, n_o, n_a = len(rows) + len(consts), len(outs), len(accs)

    def body(*refs):
        i = pl.program_id(0)
        res_o, res_a = fn(i, *[r[...] for r in refs[:n_in]])
        for r, val in zip(refs[n_in:n_in + n_o], res_o):
            r[...] = val.astype(r.dtype)
        if n_a:
            a_refs = refs[n_in + n_o:]

            @pl.when(i == 0)
            def _():
                for r in a_refs:
                    r[...] = jnp.zeros(r.shape, r.dtype)

            for r, val in zip(a_refs, res_a):
                r[...] += val

    in_specs, args = [], []
    for a in rows:
        if isinstance(a, tuple):
            arr, k = a
            in_specs.append(pl.BlockSpec((None, tile, arr.shape[2]), functools.partial(lambda i, k: (k, i, 0), k=k)))
            args.append(arr)
        else:
            in_specs.append(pl.BlockSpec((tile, a.shape[1]), lambda i: (i, 0)))
            args.append(a)
    for c in consts:
        in_specs.append(pl.BlockSpec(c.shape, lambda i: (0, 0)))
        args.append(c)
    out_specs = [pl.BlockSpec((tile, c), lambda i: (i, 0)) for c, _ in outs]
    out_specs += [pl.BlockSpec(s, lambda i: (0, 0)) for s in accs]
    out_shape = [jax.ShapeDtypeStruct((n, c), dt) for c, dt in outs]
    out_shape += [jax.ShapeDtypeStruct(s, f32) for s in accs]
    res = pl.pallas_call(body, grid=(n // tile,), in_specs=in_specs, out_specs=out_specs, out_shape=out_shape,
                         name=name, compiler_params=_params(("arbitrary",)))(*args)
    return res


def _mm(name, a, b, trans_b=False, out_dtype=f32, tm=ROW_TILE):
    m, k = a.shape
    spec, b, shape = _whole(b)
    n = shape[0] if trans_b else shape[1]
    tm = min(tm, m)
    assert m % tm == 0
    dims = NT if trans_b else (((1,), (0,)), ((), ()))

    def body(a_ref, b_ref, o_ref):
        o_ref[...] = lax.dot_general(a_ref[...], b_ref[...], dims, preferred_element_type=f32).astype(o_ref.dtype)

    return pl.pallas_call(
        body, grid=(m // tm,), in_specs=[pl.BlockSpec((tm, k), lambda i: (i, 0)), spec],
        out_specs=pl.BlockSpec((tm, n), lambda i: (i, 0)),
        out_shape=jax.ShapeDtypeStruct((m, n), out_dtype), name=name, compiler_params=_params(("parallel",)))(a, b)


def _whole(b):
    if isinstance(b, tuple):
        arr, layer = b
        return pl.BlockSpec((None,) + arr.shape[1:], lambda i: (layer, 0, 0)), arr, arr.shape[1:]
    return pl.BlockSpec(b.shape, lambda i: (0, 0)), b, b.shape


def _mm2(name, a1, b1, a2, b2, out_dtype=f32, tm=ROW_TILE // 2):
    m, k = a1.shape
    spec1, b1, shape = _whole(b1)
    spec2, b2, _ = _whole(b2)
    n = shape[1]
    tm = min(tm, m)
    assert m % tm == 0

    def body(a1_ref, b1_ref, a2_ref, b2_ref, o_ref):
        acc = jnp.dot(a1_ref[...], b1_ref[...], preferred_element_type=f32)
        acc += jnp.dot(a2_ref[...], b2_ref[...], preferred_element_type=f32)
        o_ref[...] = acc.astype(o_ref.dtype)

    row = pl.BlockSpec((tm, k), lambda i: (i, 0))
    return pl.pallas_call(
        body, grid=(m // tm,), in_specs=[row, spec1, row, spec2], out_specs=pl.BlockSpec((tm, n), lambda i: (i, 0)),
        out_shape=jax.ShapeDtypeStruct((m, n), out_dtype), name=name,
        compiler_params=_params(("parallel",)))(a1, b1, a2, b2)


def _mm_tn(name, a, g, tk=512, tl=ROW_TILE):
    rows, k1 = a.shape
    n = g.shape[1]
    tl = min(tl, rows)
    assert rows % tl == 0 and k1 % tk == 0

    def body(a_ref, g_ref, o_ref):
        @pl.when(pl.program_id(1) == 0)
        def _():
            o_ref[...] = jnp.zeros(o_ref.shape, f32)

        o_ref[...] += lax.dot_general(a_ref[...], g_ref[...], TN, preferred_element_type=f32)

    return pl.pallas_call(
        body, grid=(k1 // tk, rows // tl),
        in_specs=[pl.BlockSpec((tl, tk), lambda k, l: (l, k)), pl.BlockSpec((tl, n), lambda k, l: (l, 0))],
        out_specs=pl.BlockSpec((tk, n), lambda k, l: (k, 0)),
        out_shape=jax.ShapeDtypeStruct((k1, n), f32), name=name,
        compiler_params=_params(("parallel", "arbitrary")))(a, g)


def _ln_stats(r):
    mean = jnp.mean(r, axis=-1, keepdims=True)
    c = r - mean
    rstd = lax.rsqrt(jnp.mean(c * c, axis=-1, keepdims=True) + LN_EPS)
    return c * rstd, rstd


def _ln_fwd(h, mix, gain, bias):
    def fn(i, h, mix, g, b):
        r = ALPHA * h + mix
        y = _ln_stats(r)[0] * g + b
        return (r, y, y), ()

    return _ew("ln_fwd", fn, [h, mix], [gain, bias], [(D, f32), (D, f32), (D, bf16)])


def _ln_bwd(d_a, d_b, r, gain):
    def core(dout, r, g):
        xhat, rstd = _ln_stats(r)
        dxh = dout * g
        dr = rstd * (dxh - jnp.mean(dxh, axis=-1, keepdims=True) - xhat * jnp.mean(dxh * xhat, axis=-1, keepdims=True))
        return (dr, dr), (jnp.sum(dout * xhat, axis=0, keepdims=True), jnp.sum(dout, axis=0, keepdims=True))

    outs, accs = [(D, f32), (D, bf16)], [(1, D), (1, D)]
    if d_a is None:
        return _ew("ln_bwd_top", lambda i, d, r, g: core(d, r, g), [d_b, r], [gain], outs, accs)
    return _ew("ln_bwd", lambda i, da, db, r, g: core(ALPHA * da + db, r, g), [d_a, d_b, r], [gain], outs, accs)


def _sigmoid(x):
    return 1.0 / (1.0 + jnp.exp(-x))


def _ffn_up(h_bf, w_gate_t, w_up_t, tm=FFN_TILE):
    m, k = h_bf.shape
    gspec, w_gate_t, (n, _) = _whole(w_gate_t)
    uspec, w_up_t, _ = _whole(w_up_t)

    def body(h_ref, wg_ref, wu_ref, g_ref, u_ref, a_ref):
        h = h_ref[...]
        g = lax.dot_general(h, wg_ref[...], NT, preferred_element_type=f32).astype(bf16)
        u = lax.dot_general(h, wu_ref[...], NT, preferred_element_type=f32).astype(bf16)
        g_ref[...] = g
        u_ref[...] = u
        g = g.astype(f32)
        a_ref[...] = (g * _sigmoid(g) * u.astype(f32)).astype(bf16)

    row = pl.BlockSpec((tm, n), lambda i: (i, 0))
    return pl.pallas_call(
        body, grid=(m // tm,), in_specs=[pl.BlockSpec((tm, k), lambda i: (i, 0)), gspec, uspec],
        out_specs=[row, row, row], out_shape=[jax.ShapeDtypeStruct((m, n), bf16)] * 3, name="ffn_up",
        compiler_params=_params(("parallel",)))(h_bf, w_gate_t, w_up_t)


def _ffn_dup(df_bf, w_down, gate, up, tm=FFN_TILE):
    m, k = df_bf.shape
    wspec, w_down, (n, _) = _whole(w_down)

    def body(d_ref, w_ref, g_ref, u_ref, dg_ref, du_ref):
        da = lax.dot_general(d_ref[...], w_ref[...], NT, preferred_element_type=f32).astype(bf16).astype(f32)
        g, u = g_ref[...].astype(f32), u_ref[...].astype(f32)
        s = _sigmoid(g)
        dg_ref[...] = (da * u * s * (1.0 + g * (1.0 - s))).astype(bf16)
        du_ref[...] = (da * g * s).astype(bf16)

    row = pl.BlockSpec((tm, n), lambda i: (i, 0))
    return pl.pallas_call(
        body, grid=(m // tm,),
        in_specs=[pl.BlockSpec((tm, k), lambda i: (i, 0)), wspec, row, row],
        out_specs=[row, row], out_shape=[jax.ShapeDtypeStruct((m, n), bf16)] * 2, name="ffn_dup",
        compiler_params=_params(("parallel",)))(df_bf, w_down, gate, up)


def _loss_grad(h, target):
    n = h.shape[0]

    def body(h_ref, t_ref, d_ref, sq_ref):
        i = pl.program_id(0)

        @pl.when(i == 0)
        def _():
            d_ref[...] = jnp.zeros(d_ref.shape, f32)
            sq_ref[...] = jnp.zeros(sq_ref.shape, f32)

        @pl.when(i > 0)
        def _():
            e = h_ref[...] - t_ref[...]
            d_ref[...] = e * (1.0 / D)
            sq_ref[...] += jnp.sum(e * e, axis=0, keepdims=True)

    return pl.pallas_call(
        body, grid=(n // OFF,),
        in_specs=[pl.BlockSpec((OFF, D), lambda i: (i, 0)), pl.BlockSpec((OFF, D), lambda i: (jnp.maximum(i - 1, 0), 0))],
        out_specs=[pl.BlockSpec((OFF, D), lambda i: (i, 0)), pl.BlockSpec((1, D), lambda i: (0, 0))],
        out_shape=[jax.ShapeDtypeStruct((n, D), f32), jax.ShapeDtypeStruct((1, D), f32)], name="loss",
        compiler_params=_params(("arbitrary",)))(h, target)


def _adamw(w, g, m, v):
    def fn(i, w, g, m, v):
        m = ADAM_B1 * m + (1.0 - ADAM_B1) * g
        v = ADAM_B2 * v + (1.0 - ADAM_B2) * jnp.square(g)
        m_hat = m / (1.0 - ADAM_B1 ** ADAM_STEP)
        v_hat = v / (1.0 - ADAM_B2 ** ADAM_STEP)
        delta = -ADAM_LR * (m_hat / (jnp.sqrt(v_hat) + ADAM_EPS) + ADAM_WD * w)
        return (delta, m, v), ()

    rows, cols = w.shape
    cap = ADAM_TILE if cols > 128 else 4 * ADAM_TILE
    fits = [t for t in range(8, min(rows, cap) + 1, 8) if rows % t == 0]
    return _ew("adamw", fn, [w, g, m, v], [], [(cols, f32)] * 3, tile=max(fits) if fits else rows)


def _s5_mats(lam_re, lam_im, log_dt, b_re, b_im, c_re, c_im):
    steps = jnp.arange(CHUNK + 1, dtype=f32)
    n = CHUNK * GCH
    last = n - GCH

    def one(lr, li, ldt, br, bi, cr, ci, reverse):
        dt = jnp.exp(ldt)[:, None]
        mag = jnp.exp(lr * dt)
        abr, abi = mag * jnp.cos(li * dt), mag * jnp.sin(li * dt)
        nr, ni = abr - 1.0, abi
        den = lr * lr + li * li
        zr, zi = (nr * lr + ni * li) / den, (ni * lr - nr * li) / den
        bbr = zr[..., None] * br - zi[..., None] * bi
        bbi = zr[..., None] * bi + zi[..., None] * br
        pmag = jnp.exp((lr * dt)[..., None] * steps)
        pang = (li * dt)[..., None] * steps
        pr, pi = pmag * jnp.cos(pang), pmag * jnp.sin(pang)
        crt, cit = jnp.swapaxes(cr, 1, 2)[:, :, None, :], jnp.swapaxes(ci, 1, 2)[:, :, None, :]
        car = crt * pr[..., None] - cit * pi[..., None]
        cai = crt * pi[..., None] + cit * pr[..., None]
        if reverse:
            taps = slice(CHUNK - 1, None, -1)
            outs = slice(CHUNK, 0, -1)
            ins = slice(0, CHUNK)
        else:
            taps, outs, ins = slice(0, CHUNK), slice(1, CHUNK + 1), slice(CHUNK - 1, None, -1)
        kern = (jnp.einsum('gpi,gpq->giq', bbr, car[:, :, taps].reshape(GROUPS, NSTATE, n), precision=HIGH)
                - jnp.einsum('gpi,gpq->giq', bbi, cai[:, :, taps].reshape(GROUPS, NSTATE, n), precision=HIGH))
        wide = jnp.pad(kern, ((0, 0), (0, 0), (0, last) if reverse else (last, 0)))
        m = jnp.stack([wide[:, :, last - GCH * t:last - GCH * t + n] for t in range(CHUNK)], axis=1)
        qr = jnp.swapaxes(pr[:, :, ins], 1, 2)[:, :, None, :]
        qi = jnp.swapaxes(pi[:, :, ins], 1, 2)[:, :, None, :]
        bbrt, bbit = jnp.swapaxes(bbr, 1, 2)[:, None], jnp.swapaxes(bbi, 1, 2)[:, None]
        pin = jnp.concatenate([qr * bbrt - qi * bbit, qr * bbit + qi * bbrt], axis=-1)
        pout = jnp.concatenate([car[:, :, outs].reshape(GROUPS, NSTATE, n),
                                -cai[:, :, outs].reshape(GROUPS, NSTATE, n)], axis=1)
        return (m.reshape(GROUPS, n, n), pin.reshape(GROUPS, n, 2 * NSTATE), pout, pr[:, :, CHUNK], pi[:, :, CHUNK])

    mf, pinf, poutf, arf, aif = one(lam_re[0], lam_im[0], log_dt[0], b_re[0], b_im[0], c_re[0], c_im[0], False)
    mr, pinr, poutr, arr, air = one(lam_re[1], lam_im[1], log_dt[1], b_re[1], b_im[1], c_re[1], c_im[1], True)
    return (mf + mr, jnp.concatenate([pinf, pinr], 2), jnp.concatenate([poutf, poutr], 1),
            jnp.stack([arf, arr]), jnp.stack([aif, air]))


def _s5_coefs(a_re, a_im):
    c1 = jnp.concatenate([a_re, a_re], -1)
    c2 = jnp.concatenate([-a_im, a_im], -1)
    return tuple(c.reshape(GROUPS // GB, 1, GB * 2 * NSTATE) for c in (c1[0], c2[0], c1[1], c2[1]))


def _swap(s):
    w = s.shape[1]
    lane = lax.broadcasted_iota(jnp.int32, s.shape, 1)
    return jnp.where(lane % (2 * NSTATE) < NSTATE, pltpu.roll(s, w - NSTATE, 1), pltpu.roll(s, NSTATE, 1))


def _group_lanes(g):
    return slice(g * 2 * NSTATE, (g + 1) * 2 * NSTATE)


def _s5_states(nc, u_ref, pin_ref, coef, vf, vr, wf, wr, sf, sr):
    c1f, c2f, c1r, c2r = coef
    for g in range(GB):
        v = jnp.dot(u_ref[g], pin_ref[g], preferred_element_type=f32)
        vf[:, _group_lanes(g)] = v[:, :2 * NSTATE]
        vr[:, _group_lanes(g)] = v[:, 2 * NSTATE:]
    wf[...] = _swap(vf[...])
    wr[...] = _swap(vr[...])

    def step(i, carry):
        s_f, t_f, s_r, t_r = carry
        kf, kr = pl.ds(i, 1), pl.ds(nc - 1 - i, 1)
        sf[kf, :] = s_f
        sr[kr, :] = s_r
        s_f, t_f = c1f * s_f + c2f * t_f + vf[kf, :], c1f * t_f - c2f * s_f + wf[kf, :]
        s_r, t_r = c1r * s_r + c2r * t_r + vr[kr, :], c1r * t_r - c2r * s_r + wr[kr, :]
        return s_f, t_f, s_r, t_r

    z = jnp.zeros((1, GB * 2 * NSTATE), f32)
    lax.fori_loop(0, nc, step, (z, z, z, z))


def _s5_core_fwd(ug, msum, pin, pout, coefs):
    nc = ug.shape[1]
    n = CHUNK * GCH

    def body(u_ref, m_ref, pin_ref, pout_ref, c1f, c2f, c1r, c2r, y_ref, vf, vr, wf, wr, sf, sr):
        coef = (c1f[...], c2f[...], c1r[...], c2r[...])
        _s5_states(nc, u_ref, pin_ref, coef, vf, vr, wf, wr, sf, sr)
        for g in range(GB):
            s_in = jnp.concatenate([sf[:, _group_lanes(g)], sr[:, _group_lanes(g)]], axis=1).astype(bf16)
            y_ref[g] = (jnp.dot(u_ref[g], m_ref[g], preferred_element_type=f32)
                        + jnp.dot(s_in, pout_ref[g], preferred_element_type=f32)).astype(bf16)

    seq = pl.BlockSpec((GB, nc, n), lambda i: (i, 0, 0))
    mat = pl.BlockSpec((GB, n, n), lambda i: (i, 0, 0))
    cf = pl.BlockSpec((None, 1, GB * 2 * NSTATE), lambda i: (i, 0, 0))
    scr = pltpu.VMEM((nc, GB * 2 * NSTATE), f32)
    return pl.pallas_call(
        body, grid=(GROUPS // GB,), in_specs=[seq, mat, mat, mat, cf, cf, cf, cf], out_specs=seq,
        out_shape=jax.ShapeDtypeStruct((GROUPS, nc, n), bf16), scratch_shapes=[scr] * 6,
        name="s5_core_fwd", compiler_params=_params(("parallel",)))(ug, msum, pin, pout, *coefs)


def _s5_core_bwd(ug, dyg, msum, pin, pout, coefs):
    nc = ug.shape[1]
    n = CHUNK * GCH

    def body(u_ref, dy_ref, m_ref, pin_ref, pout_ref, c1f, c2f, c1r, c2r,
             du_ref, dm_ref, dpin_ref, dpout_ref, a1f_ref, a2f_ref, a1r_ref, a2r_ref, vf, vr, wf, wr, sf, sr):
        coef = (c1f[...], c2f[...], c1r[...], c2r[...])
        _s5_states(nc, u_ref, pin_ref, coef, vf, vr, wf, wr, sf, sr)
        for g in range(GB):
            s_in = jnp.concatenate([sf[:, _group_lanes(g)], sr[:, _group_lanes(g)]], axis=1).astype(bf16)
            dy = dy_ref[g]
            ds = lax.dot_general(dy, pout_ref[g], NT, preferred_element_type=f32)
            vf[:, _group_lanes(g)] = ds[:, :2 * NSTATE]
            vr[:, _group_lanes(g)] = ds[:, 2 * NSTATE:]
            dpout_ref[g] = lax.dot_general(s_in, dy, TN, preferred_element_type=f32)
            dm_ref[g] = lax.dot_general(u_ref[g], dy, TN, preferred_element_type=f32)

        wf[...] = _swap(vf[...])
        wr[...] = _swap(vr[...])
        k1f, k2f, k1r, k2r = coef[0], -coef[1], coef[2], -coef[3]

        def step(i, carry):
            g_f, h_f, g_r, h_r, a1f, b2f, a1r, b2r = carry
            kf, kr = pl.ds(nc - 1 - i, 1), pl.ds(i, 1)
            s_f, s_r = sf[kf, :], sr[kr, :]
            sf[kf, :] = g_f
            sr[kr, :] = g_r
            a1f, b2f = a1f + g_f * s_f, b2f + h_f * s_f
            a1r, b2r = a1r + g_r * s_r, b2r + h_r * s_r
            g_f, h_f = vf[kf, :] + k1f * g_f + k2f * h_f, wf[kf, :] + k1f * h_f - k2f * g_f
            g_r, h_r = vr[kr, :] + k1r * g_r + k2r * h_r, wr[kr, :] + k1r * h_r - k2r * g_r
            return g_f, h_f, g_r, h_r, a1f, b2f, a1r, b2r

        z = jnp.zeros((1, GB * 2 * NSTATE), f32)
        _, _, _, _, a1f, b2f, a1r, b2r = lax.fori_loop(0, nc, step, (z,) * 8)
        a1f_ref[...], a2f_ref[...], a1r_ref[...], a2r_ref[...] = a1f, _swap(b2f), a1r, _swap(b2r)
        for g in range(GB):
            dv = jnp.concatenate([sf[:, _group_lanes(g)], sr[:, _group_lanes(g)]], axis=1).astype(bf16)
            du_ref[g] = (lax.dot_general(dy_ref[g], m_ref[g], NT, preferred_element_type=f32)
                         + lax.dot_general(dv, pin_ref[g], NT, preferred_element_type=f32)).astype(bf16)
            dpin_ref[g] = lax.dot_general(u_ref[g], dv, TN, preferred_element_type=f32)

    seq = pl.BlockSpec((GB, nc, n), lambda i: (i, 0, 0))
    mat = pl.BlockSpec((GB, n, n), lambda i: (i, 0, 0))
    cf = pl.BlockSpec((None, 1, GB * 2 * NSTATE), lambda i: (i, 0, 0))
    scr = pltpu.VMEM((nc, GB * 2 * NSTATE), f32)
    mat_s = jax.ShapeDtypeStruct((GROUPS, n, n), f32)
    cf_s = jax.ShapeDtypeStruct((GROUPS // GB, 1, GB * 2 * NSTATE), f32)
    return pl.pallas_call(
        body, grid=(GROUPS // GB,), in_specs=[seq, seq, mat, mat, mat, cf, cf, cf, cf],
        out_specs=[seq, mat, mat, mat, cf, cf, cf, cf],
        out_shape=[jax.ShapeDtypeStruct((GROUPS, nc, n), bf16), mat_s, mat_s, mat_s, cf_s, cf_s, cf_s, cf_s],
        scratch_shapes=[scr] * 6, name="s5_core_bwd",
        compiler_params=_params(("parallel",)))(ug, dyg, msum, pin, pout, *coefs)


def _block_movers():
    a_in, l_in = jnp.divmod(jnp.arange(GB * 128, dtype=jnp.int32), 128)
    a_out, c_out = jnp.divmod(jnp.arange(128, dtype=jnp.int32), GCH)
    j = jnp.arange(GB, dtype=jnp.int32)[:, None, None]
    hit = (a_in[None, :, None] == a_out[None, None, :]) & (l_in[None, :, None] == GCH * j + c_out[None, None, :])
    return hit.astype(bf16)


def _to_groups(x, movers, mask):
    n = x.shape[0]
    nc = n // CHUNK
    half = CHUNK // 2

    def body(x_ref, mv_ref, o_ref):
        keep = lax.broadcasted_iota(jnp.int32, (nc, 1), 0) >= PAD // CHUNK
        steps = [x_ref[pl.ds(t, nc, stride=CHUNK), :] for t in range(CHUNK)]
        if mask:
            steps = [jnp.where(keep, s, 0.0) for s in steps]
        lo = jnp.concatenate(steps[:half], axis=1).astype(bf16)
        hi = jnp.concatenate(steps[half:], axis=1).astype(bf16)
        for g in range(GB):
            o_ref[g] = jnp.concatenate([jnp.dot(lo, mv_ref[g], preferred_element_type=f32),
                                        jnp.dot(hi, mv_ref[g], preferred_element_type=f32)], axis=1).astype(bf16)

    return pl.pallas_call(
        body, grid=(GROUPS // GB,),
        in_specs=[pl.BlockSpec((n, 128), lambda i: (0, i)), pl.BlockSpec(movers.shape, lambda i: (0, 0, 0))],
        out_specs=pl.BlockSpec((GB, nc, CHUNK * GCH), lambda i: (i, 0, 0)),
        out_shape=jax.ShapeDtypeStruct((GROUPS, nc, CHUNK * GCH), bf16), name="s5_to_groups",
        compiler_params=_params(("parallel",)))(x, movers)


def _from_groups(y, movers, base=None):
    nc = y.shape[1]
    n = nc * CHUNK
    half = CHUNK // 2

    def body(*refs):
        y_ref, mv_ref = refs[:2]
        o_ref = refs[-1]
        keep = lax.broadcasted_iota(jnp.int32, (nc, 1), 0) >= PAD // CHUNK
        lo = jnp.concatenate([y_ref[g][:, :128] for g in range(GB)], axis=1)
        hi = jnp.concatenate([y_ref[g][:, 128:] for g in range(GB)], axis=1)
        for t in range(CHUNK):
            rows = pl.ds(t, nc, stride=CHUNK)
            v = jnp.dot(lo if t < half else hi, mv_ref[t % half], preferred_element_type=f32)
            if base is not None:
                v = refs[2][rows, :] + jnp.where(keep, v, 0.0)
            o_ref[rows, :] = v

    tok = pl.BlockSpec((n, 128), lambda i: (0, i))
    args = (y, movers) if base is None else (y, movers, base)
    return pl.pallas_call(
        body, grid=(GROUPS // GB,),
        in_specs=[pl.BlockSpec((GB, nc, CHUNK * GCH), lambda i: (i, 0, 0)),
                  pl.BlockSpec(movers.shape, lambda i: (0, 0, 0))] + ([] if base is None else [tok]),
        out_specs=tok, out_shape=jax.ShapeDtypeStruct((n, D), f32), name="s5_from_groups",
        compiler_params=_params(("parallel",)))(*args)


def _gelu(y):
    return 0.5 * y * (1.0 + lax.erf(y * (2.0 ** -0.5)))


def _gelu_grad(y):
    return 0.5 * (1.0 + lax.erf(y * (2.0 ** -0.5))) + y * jnp.exp(-0.5 * y * y) * (1.0 / math.sqrt(2.0 * math.pi))


def _s5_fwd(h, movers, mats, d_skip, w_glu, w_out):
    msum, pin, pout, a_re, a_im = mats
    coefs = _s5_coefs(a_re, a_im)
    ug = _to_groups(h, movers, mask=True)
    ys = _from_groups(_s5_core_fwd(ug, msum.astype(bf16), pin.astype(bf16), pout.astype(bf16), coefs), movers)

    def post(i, ys, h, d):
        y = ys + d * h
        return (y, _gelu(y)), ()

    y, g_bf = _ew("s5_gelu", post, [ys, h], [d_skip], [(D, f32), (D, bf16)])
    gw = _mm("s5_glu_mm", g_bf, w_glu)

    def glu(i, y, gw):
        return (_gelu(y) * _sigmoid(gw),), ()

    z_bf = _ew("s5_glu", glu, [y, gw], [], [(D, bf16)])[0]
    mix = _mm("s5_out_mm", z_bf, w_out)
    return mix, (ug, y, g_bf, gw, z_bf)


def _s5_bwd(dmix_bf, h, movers, saved, mats, vjp_mats, d_skip, w_glu, w_out):
    ug, y, g_bf, gw, z_bf = saved
    msum, pin, pout, a_re, a_im = mats
    coefs = _s5_coefs(a_re, a_im)
    dz = _mm("s5_dz_mm", dmix_bf, w_out, trans_b=True)
    d_w_out = _mm_tn("s5_dwout", z_bf, dmix_bf)

    def dglu(i, dz, y, gw):
        g, s = _gelu(y), _sigmoid(gw)
        return (dz * g * s * (1.0 - s), dz * s), ()

    dgw_bf, dg1 = _ew("s5_dglu", dglu, [dz, y, gw], [], [(D, bf16), (D, f32)])
    d_w_glu = _mm_tn("s5_dwglu", g_bf, dgw_bf)
    dg2 = _mm("s5_dg_mm", dgw_bf, w_glu, trans_b=True)

    def dgelu(i, dg1, dg2, y, h, d):
        dy = (dg1 + dg2) * _gelu_grad(y)
        return (dy, dy * d), (jnp.sum(dy * h, axis=0, keepdims=True),)

    dy, dh_skip, dd = _ew("s5_dgelu", dgelu, [dg1, dg2, y, h], [d_skip], [(D, f32), (D, f32)], [(1, D)])
    dug, dm, dpin, dpout, a1f, a2f, a1r, a2r = _s5_core_bwd(
        ug, _to_groups(dy, movers, mask=False), msum.astype(bf16), pin.astype(bf16), pout.astype(bf16), coefs)
    dh = _from_groups(dug, movers, base=dh_skip)
    a1 = jnp.stack([a1f, a1r]).reshape(2, GROUPS, 2 * NSTATE)
    a2 = jnp.stack([a2f, a2r]).reshape(2, GROUPS, 2 * NSTATE)
    da_re = a1[..., :NSTATE] + a1[..., NSTATE:]
    da_im = a2[..., NSTATE:] - a2[..., :NSTATE]
    d_params = vjp_mats((dm, dpin, dpout, da_re, da_im))
    return dh, d_params, dd[0], d_w_glu, d_w_out


def _rope_tables(n):
    row = jnp.arange(n, dtype=jnp.int32) - OFF
    real = row >= 0
    rid = jnp.where(real, row // GRID_W, 0).astype(f32)
    cid = jnp.where(real, row % GRID_W, 0).astype(f32)
    half = HD // 2
    inv = ROPE_THETA ** (-jnp.arange(0, half, 2, dtype=f32) / half)
    ar, ac = rid[:, None] * inv[None, :], cid[:, None] * inv[None, :]
    cos = jnp.concatenate([jnp.cos(ar), jnp.cos(ar), jnp.cos(ac), jnp.cos(ac)], axis=1)
    sin = jnp.concatenate([-jnp.sin(ar), jnp.sin(ar), -jnp.sin(ac), jnp.sin(ac)], axis=1)
    return jnp.tile(cos, (1, 2)), jnp.tile(sin, (1, 2))


def _head_mats():
    head = jnp.arange(QW, dtype=jnp.int32)[:, None] // HD == jnp.arange(128, dtype=jnp.int32)[None, :]
    return head.astype(f32) * (1.0 / HD), head.astype(f32).T


def _rot(v):
    w = v.shape[1]
    lane = lax.broadcasted_iota(jnp.int32, v.shape, 1)
    return jnp.where(lane % 32 < 16, pltpu.roll(v, w - 16, 1), pltpu.roll(v, 16, 1))


def _head_mean(v, e, et):
    w = v.shape[1]
    m = jnp.dot(v, e[:w], preferred_element_type=f32, precision=HIGH)
    return m, et[:, :w]


def _rms_rope(t, gain, e, et, cos, sin):
    w = t.shape[1]
    ms, spread = _head_mean(t * t, e, et)
    rs = jnp.dot(lax.rsqrt(ms + QK_EPS), spread, preferred_element_type=f32, precision=HIGH)
    n0 = t * rs
    n = n0 * gain
    reps = w // 128
    return n * jnp.tile(cos, (1, reps)) + _rot(n) * jnp.tile(sin, (1, reps))


def _rms_rope_bwd(dout, t, gain, e, et, cos, sin):
    w = t.shape[1]
    reps = w // 128
    ms, spread = _head_mean(t * t, e, et)
    rs = jnp.dot(lax.rsqrt(ms + QK_EPS), spread, preferred_element_type=f32, precision=HIGH)
    n0 = t * rs
    dn = dout * jnp.tile(cos, (1, reps)) + _rot(dout * jnp.tile(sin, (1, reps)))
    dn0 = dn * gain
    mm, _ = _head_mean(dn0 * n0, e, et)
    corr = jnp.dot(mm, spread, preferred_element_type=f32, precision=HIGH)
    return rs * (dn0 - n0 * corr), jnp.sum(dn * n0, axis=0, keepdims=True)


def _qk_fwd(qkv, qg, kg, e, et, cos, sin):
    def fn(i, qkv, cos, sin, qg, kg, e, et):
        q = _rms_rope(qkv[:, :QW], qg, e, et, cos, sin) * Q_SCALE
        k = _rms_rope(qkv[:, QW:QW + KW], kg, e, et, cos, sin)
        return (q, k, qkv[:, QW + KW:]), ()

    return _ew("qk_rope", fn, [qkv, cos, sin], [qg, kg, e, et], [(QW, bf16), (KW, bf16), (KW, bf16)])


def _qk_bwd(qkv, dq, dk, dv, qg, kg, e, et, cos, sin):
    def fn(i, qkv, cos, sin, dq, dk, dv, qg, kg, e, et):
        dtq, dgq = _rms_rope_bwd(dq * (HD ** -0.5), qkv[:, :QW], qg, e, et, cos, sin)
        dtk, dgk = _rms_rope_bwd(dk * math.log(2.0), qkv[:, QW:QW + KW], kg, e, et, cos, sin)
        return (jnp.concatenate([dtq, dtk, dv], axis=1),), (dgq, dgk)

    return _ew("qk_rope_bwd", fn, [qkv, cos, sin, dq, dk, dv], [qg, kg, e, et], [(QKVW, bf16)], [(1, QW), (1, KW)])


def _to_heads(a, nh):
    return a.reshape(a.shape[0], nh, HD).transpose(1, 0, 2)


def _from_heads(a):
    return a.transpose(1, 0, 2).reshape(a.shape[1], a.shape[0] * HD)


def _masked_first(s, c):
    if c:
        return s
    col = lax.broadcasted_iota(jnp.int32, (1, s.shape[1]), 1)
    return jnp.where(col >= PAD, s, NEG)


def _flash_fwd(q, k, v1, tq=ROW_TILE, tc=KEY_CHUNK):
    n = q.shape[1]
    nc = n // tc

    def body(q_ref, k_ref, v_ref, o_ref, lse_ref):
        qb = q_ref[0]

        def scores(c):
            ks = k_ref[0, pl.ds(c * tc, tc), :]
            return _masked_first(lax.dot_general(qb, ks, NT, preferred_element_type=f32), c)

        m = jnp.full((tq, 1), NEG, f32)
        acc = jnp.zeros((tq, 2 * HD), f32)
        s_next = scores(0)
        for c in range(nc):
            s = s_next
            if c + 1 < nc:
                s_next = scores(c + 1)
            m_new = jnp.maximum(m, jnp.max(s, axis=1, keepdims=True))
            p = jnp.exp2(s - m_new)
            acc = jnp.exp2(m - m_new) * acc + jnp.dot(p.astype(bf16), v_ref[0, pl.ds(c * tc, tc), :],
                                                      preferred_element_type=f32)
            m = m_new
        l = acc[:, HD:HD + 1]
        o_ref[0] = acc[:, :HD] / l
        lse_ref[0] = m + jnp.log2(l)

    return pl.pallas_call(
        body, grid=(NQ, n // tq),
        in_specs=[pl.BlockSpec((1, tq, HD), lambda h, i: (h, i, 0)),
                  pl.BlockSpec((1, n, HD), lambda h, i: (h // (NQ // NKV), 0, 0)),
                  pl.BlockSpec((1, n, 2 * HD), lambda h, i: (h // (NQ // NKV), 0, 0))],
        out_specs=[pl.BlockSpec((1, tq, HD), lambda h, i: (h, i, 0)), pl.BlockSpec((1, tq, 1), lambda h, i: (h, i, 0))],
        out_shape=[jax.ShapeDtypeStruct((NQ, n, HD), f32), jax.ShapeDtypeStruct((NQ, n, 1), f32)],
        name="flash_fwd", compiler_params=_params(("parallel", "parallel")))(q, k, v1)


def _flash_bwd(q, k, kt, v, do, lse_row, delta_row, tk=ROW_TILE, tc=KEY_CHUNK):
    n = q.shape[1]
    nc = n // tc
    grp = NQ // NKV

    def body(q_ref, do_ref, lse_ref, delta_ref, k_ref, kt_ref, v_ref, dqt_ref, dk_ref, dv_ref):
        j, g = pl.program_id(1), pl.program_id(2)
        kb, vb, ktb = k_ref[0], v_ref[0], kt_ref[0]
        valid = lax.broadcasted_iota(jnp.int32, (tk, 1), 0) + j * tk >= PAD

        @pl.when(j == 0)
        def _():
            dqt_ref[g] = jnp.zeros((HD, n), f32)

        def products(c):
            rows = pl.ds(c * tc, tc)
            return (lax.dot_general(kb, q_ref[0, rows, :], NT, preferred_element_type=f32),
                    lax.dot_general(vb, do_ref[0, rows, :], NT, preferred_element_type=f32))

        dk = jnp.zeros((tk, HD), f32)
        dv = jnp.zeros((tk, HD), f32)
        nxt = products(0)
        for c in range(nc):
            st, dpt = nxt
            if c + 1 < nc:
                nxt = products(c + 1)
            rows = pl.ds(c * tc, tc)
            pt = jnp.exp2(jnp.where(valid, st, NEG) - lse_ref[0, :, rows])
            dv = dv + jnp.dot(pt.astype(bf16), do_ref[0, rows, :], preferred_element_type=f32)
            dst = (pt * (dpt - delta_ref[0, :, rows])).astype(bf16)
            dk = dk + jnp.dot(dst, q_ref[0, rows, :], preferred_element_type=f32)
            dqt_ref[g, :, rows] += jnp.dot(ktb, dst, preferred_element_type=f32)

        @pl.when(g == 0)
        def _():
            dk_ref[0] = dk
            dv_ref[0] = dv

        @pl.when(g > 0)
        def _():
            dk_ref[0] += dk
            dv_ref[0] += dv

    hspec = pl.BlockSpec((1, n, HD), lambda h, j, g: (h * grp + g, 0, 0))
    rspec = pl.BlockSpec((1, 1, n), lambda h, j, g: (h * grp + g, 0, 0))
    kspec = pl.BlockSpec((1, tk, HD), lambda h, j, g: (h, j, 0))
    return pl.pallas_call(
        body, grid=(NKV, n // tk, grp),
        in_specs=[hspec, hspec, rspec, rspec, kspec, pl.BlockSpec((1, HD, tk), lambda h, j, g: (h, 0, j)), kspec],
        out_specs=[pl.BlockSpec((grp, HD, n), lambda h, j, g: (h, 0, 0)), kspec, kspec],
        out_shape=[jax.ShapeDtypeStruct((NQ, HD, n), f32)] + [jax.ShapeDtypeStruct((NKV, n, HD), f32)] * 2,
        name="flash_bwd", compiler_params=_params(("parallel", "arbitrary", "arbitrary")))(
            q, do, lse_row, delta_row, k, kt, v)


def _attn_fwd(h_bf, w_qkv_t, qg, kg, w_out, tabs):
    e, et, cos, sin = tabs
    qkv = _mm("attn_qkv_mm", h_bf, w_qkv_t, trans_b=True)
    q_bf, k_bf, v_bf = _qk_fwd(qkv, qg, kg, e, et, cos, sin)
    q16, k4, v4 = _to_heads(q_bf, NQ), _to_heads(k_bf, NKV), _to_heads(v_bf, NKV)
    ones = jnp.zeros((NKV, v4.shape[1], HD), bf16).at[:, :, 0].set(1.0)
    o16, lse = _flash_fwd(q16, k4, jnp.concatenate([v4, ones], axis=2))
    o = _from_heads(o16)
    o_bf = o.astype(bf16)
    mix = _mm("attn_out_mm", o_bf, w_out)
    return mix, (qkv, q16, k4, v4, o, lse, o_bf)


def _attn_bwd(dmix_bf, h_bf, saved, w_qkv_t, qg, kg, w_out, tabs):
    e, et, cos, sin = tabs
    qkv, q16, k4, v4, o, lse, o_bf = saved
    n = qkv.shape[0]
    do = _mm("attn_do_mm", dmix_bf, w_out, trans_b=True, out_dtype=bf16)
    d_w_out = _mm_tn("attn_dwout", o_bf, dmix_bf)

    def head_dots(i, do, o, e):
        return (jnp.dot(do.astype(f32) * o, e, preferred_element_type=f32, precision=HIGH) * HD,), ()

    delta = _ew("attn_delta", head_dots, [do, o], [e], [(128, f32)])[0]
    dqt, dk4, dv4 = _flash_bwd(q16, k4, k4.transpose(0, 2, 1), v4, _to_heads(do, NQ), lse.reshape(NQ, 1, n),
                               delta[:, :NQ].T.reshape(NQ, 1, n))
    dq = dqt.transpose(2, 0, 1).reshape(n, QW)
    dqkv_bf, dgq, dgk = _qk_bwd(qkv, dq, _from_heads(dk4), _from_heads(dv4), qg, kg, e, et, cos, sin)
    d_w_qkv_t = _mm_tn("attn_dwqkv", dqkv_bf, h_bf)
    dh = _mm("attn_dh_mm", dqkv_bf, w_qkv_t)
    return dh, d_w_qkv_t, dgq.reshape(NQ, HD).sum(0), dgk.reshape(NKV, HD).sum(0), d_w_out


def _all_gather(name, shard):
    def body(x_ref, out_ref, send_sems, recv_sems, local_sem):
        x, y, c = lax.axis_index("x"), lax.axis_index("y"), lax.axis_index("c")
        me, sibling = (x, y, c), (x, y, 1 - c)
        chips = [(1 - x, y), (x, 1 - y), (1 - x, 1 - y)]

        def slot(px, py, pc):
            return out_ref.at[4 * px + 2 * py + pc]

        def copy(k, block, to, src=None):
            return pltpu.make_async_remote_copy(
                src_ref=slot(*block) if src is None else src, dst_ref=slot(*block),
                send_sem=send_sems.at[k], recv_sem=recv_sems.at[k], device_id=to, device_id_type=MESH)

        mine = pltpu.make_async_copy(x_ref, slot(*me), local_sem)
        mine.start()
        first = [copy(0, me, sibling, src=x_ref)]
        first += [copy(1 + j, me, (*chip, c), src=x_ref) for j, chip in enumerate(chips)]
        for cp in first:
            cp.start()
        passed = [copy(4 + j, (*chip, c), sibling) for j, chip in enumerate(chips)]
        for j, chip in enumerate(chips):
            copy(1 + j, (*chip, c), me).wait_recv()
            passed[j].start()
        copy(0, sibling, me).wait_recv()
        for j, chip in enumerate(chips):
            copy(4 + j, (*chip, 1 - c), me).wait_recv()
        for cp in first + passed:
            cp.wait_send()
        mine.wait()

    return pl.pallas_call(
        body, out_shape=jax.ShapeDtypeStruct((8,) + shard.shape, shard.dtype), in_specs=[ANY], out_specs=ANY,
        scratch_shapes=[pltpu.SemaphoreType.DMA((7,)), pltpu.SemaphoreType.DMA((7,)), pltpu.SemaphoreType.DMA],
        name=name)(shard)


def _swap_sibling(name, theirs):
    k = len(theirs)

    def body(*refs):
        src, dst, send_sems, recv_sems = refs[:k], refs[k:2 * k], refs[2 * k], refs[2 * k + 1]
        x, y, c = lax.axis_index("x"), lax.axis_index("y"), lax.axis_index("c")
        copies = [pltpu.make_async_remote_copy(src_ref=src[j], dst_ref=dst[j], send_sem=send_sems.at[j],
                                               recv_sem=recv_sems.at[j], device_id=(x, y, 1 - c), device_id_type=MESH)
                  for j in range(k)]
        for cp in copies:
            cp.start()
        for cp in copies:
            cp.wait()

    return pl.pallas_call(
        body, out_shape=[jax.ShapeDtypeStruct(a.shape, a.dtype) for a in theirs], in_specs=[ANY] * k,
        out_specs=[ANY] * k, scratch_shapes=[pltpu.SemaphoreType.DMA((k,)), pltpu.SemaphoreType.DMA((k,))],
        name=name)(*theirs)


def _exchange_chips(name, parts):
    k = len(parts)

    def body(*refs):
        p_refs, t_refs = refs[:k], refs[k:2 * k]
        send_sems, recv_sems, local_sems = refs[2 * k:]
        x, y, c = lax.axis_index("x"), lax.axis_index("y"), lax.axis_index("c")
        q = 2 * x + y
        copies = []
        for j in range(k):
            copies.append(pltpu.make_async_copy(p_refs[j].at[q], t_refs[j].at[q], local_sems.at[j]))
            for hop in (1, 2, 3):
                tx, ty = x ^ (hop >> 1), y ^ (hop & 1)
                copies.append(pltpu.make_async_remote_copy(
                    src_ref=p_refs[j].at[2 * tx + ty], dst_ref=t_refs[j].at[q], send_sem=send_sems.at[3 * j + hop - 1],
                    recv_sem=recv_sems.at[3 * j + hop - 1], device_id=(tx, ty, c), device_id_type=MESH))
        for cp in copies:
            cp.start()
        for cp in copies:
            cp.wait()

    return pl.pallas_call(
        body, out_shape=[jax.ShapeDtypeStruct(a.shape, a.dtype) for a in parts], in_specs=[ANY] * k,
        out_specs=[ANY] * k,
        scratch_shapes=[pltpu.SemaphoreType.DMA((3 * k,)), pltpu.SemaphoreType.DMA((3 * k,)),
                        pltpu.SemaphoreType.DMA((k,))],
        name=name)(*parts)


def _reduce_scatter(mine, theirs):
    got = _swap_sibling("rs_sibling", list(theirs))
    parts = []
    for a, b, dt, nm in zip(mine, got, (bf16, f32), ("rs_add2", "rs_add2_small")):
        rows = 4 * a.shape[1]
        parts.append(_ew(nm, lambda i, a, b: ((a + b,), ()), [a.reshape(rows, D), b.reshape(rows, D)], [],
                         [(D, dt)], tile=RS_TILE)[0].reshape(a.shape))
    ts = _exchange_chips("rs_chips", parts)

    def add4(i, a, b, c, d):
        return ((((a.astype(f32) + b.astype(f32)) + c.astype(f32)) + d.astype(f32),), ())

    return [_ew(nm, add4, [(t, 0), (t, 1), (t, 2), (t, 3)], [], [(D, f32)], tile=RS_TILE)[0]
            for t, nm in zip(ts, ("rs_add4", "rs_add4_small"))]


def _pack_rows(parts, rows):
    flat = jnp.concatenate([p.reshape(-1) for p in parts])
    return jnp.pad(flat, (0, rows * D - flat.shape[0])).reshape(rows, D)


def _unpack(flat, shapes):
    out, off = [], 0
    for s in shapes:
        n = math.prod(s)
        out.append(flat[off:off + n].reshape(s))
        off += n
    return out


def _mat_rows(block, transposed, blk):
    a = jnp.swapaxes(block, 1, 2) if transposed else block
    a = jnp.pad(a, ((0, 0), (0, blk - a.shape[1]), (0, 0)))
    return a.reshape(-1, D)


def _mat_block(rows, transposed, blk, real):
    a = rows.reshape(-1, blk, D)[:, :real]
    return jnp.swapaxes(a, 1, 2) if transposed else a


def _mat_full(gathered, blk):
    layers = gathered.shape[1] // blk
    return gathered.reshape(8, layers, blk, D).transpose(1, 0, 2, 3).reshape(layers, 8 * blk, D)


def _vec_full(gathered):
    return gathered.transpose(1, 0, 2).reshape(gathered.shape[1], D)


def _grad_slots(full, small, cc):
    def halves(a, blk):
        a4 = a.reshape(4, 2, blk, D)
        return [lax.dynamic_index_in_dim(a4, sel, axis=1, keepdims=False) for sel in (cc, 1 - cc)]

    mine, theirs = [], []
    for name, _, blk, _ in MATS:
        for layer in full[name]:
            a, b = halves(layer, blk)
            mine.append(a)
            theirs.append(b)
    a, b = halves(small, REP_PIECE)
    return (jnp.concatenate(mine, axis=1), a), (jnp.concatenate(theirs, axis=1), b)


def _local_step(x0, target0, w, fw):
    seq = x0.shape[0]
    n = OFF + seq
    movers = _block_movers()
    h = jnp.concatenate([jnp.zeros((PAD, D), f32), fw['meta_tokens'], x0], axis=0)
    h_bf = h.astype(bf16)
    tabs = _head_mats() + _rope_tables(n)
    qg = [jnp.tile(w['attn_q_gain'][j], NQ)[None, :] for j in range(2)]
    kg = [jnp.tile(w['attn_k_gain'][j], NKV)[None, :] for j in range(2)]
    s5_names = ['s5_lambda_re', 's5_lambda_im', 's5_log_dt', 's5_b_re', 's5_b_im', 's5_c_re', 's5_c_im']
    s5_mats, s5_vjp = [], []
    for j in range(2):
        mats, vjp = jax.vjp(_s5_mats, *[w[k][j] for k in s5_names])
        s5_mats.append(mats)
        s5_vjp.append(vjp)
    saved = []
    for i in range(DEPTH):
        j = i // 2
        if i % 2 == 0:
            mix, sv = _s5_fwd(h, movers, s5_mats[j], w['s5_d'][j][None, :], (fw['s5_w_glu'], j), (fw['s5_w_out'], j))
        else:
            mix, sv = _attn_fwd(h_bf, (fw['attn_w_qkv'], j), qg[j], kg[j], (fw['attn_w_out'], j), tabs)
        r1, h1, h1_bf = _ln_fwd(h, mix, fw['ln_gain'][i, 0][None, :], fw['ln_bias'][i, 0][None, :])
        gate, up, act = _ffn_up(h1_bf, (fw['ffn_w_gate'], i), (fw['ffn_w_up'], i))
        f = _mm("ffn_down_mm", act, (fw['ffn_w_down'], i))
        r2, h2, h2_bf = _ln_fwd(h1, f, fw['ln_gain'][i, 1][None, :], fw['ln_bias'][i, 1][None, :])
        saved.append((h, h_bf, sv, r1, h1_bf, gate, up, act, r2))
        h, h_bf = h2, h2_bf

    d_b, sq = _loss_grad(h, target0)
    loss = 0.5 * jnp.sum(sq) * (1.0 / D)

    grads = {k: [None] * (DEPTH if k.startswith('ffn') else 2) for k in WEIGHTS}
    d_ln_gain = [[None, None] for _ in range(DEPTH)]
    d_ln_bias = [[None, None] for _ in range(DEPTH)]
    d_a = None
    for i in reversed(range(DEPTH)):
        j = i // 2
        h_in, h_in_bf, sv, r1, h1_bf, gate, up, act, r2 = saved[i]
        dr2, dr2_bf, dg, db = _ln_bwd(d_a, d_b, r2, fw['ln_gain'][i, 1][None, :])
        d_ln_gain[i][1], d_ln_bias[i][1] = dg[0], db[0]
        dgate, dup = _ffn_dup(dr2_bf, (fw['ffn_w_down'], i), gate, up)
        grads['ffn_w_down'][i] = _mm_tn("ffn_dwdown", act, dr2_bf, tk=DFFP // 2)
        grads['ffn_w_gate'][i] = _mm_tn("ffn_dwgate", dgate, h1_bf, tk=DFFP // 2)
        grads['ffn_w_up'][i] = _mm_tn("ffn_dwup", dup, h1_bf, tk=DFFP // 2)
        dh1 = _mm2("ffn_dh_mm", dgate, (fw['ffn_w_gate'], i), dup, (fw['ffn_w_up'], i))
        dr1, dr1_bf, dg, db = _ln_bwd(dr2, dh1, r1, fw['ln_gain'][i, 0][None, :])
        d_ln_gain[i][0], d_ln_bias[i][0] = dg[0], db[0]
        if i % 2 == 0:
            dh, d_par, dd, d_w_glu, d_w_out = _s5_bwd(dr1_bf, h_in, movers, sv, s5_mats[j], s5_vjp[j],
                                                      w['s5_d'][j][None, :], (fw['s5_w_glu'], j), (fw['s5_w_out'], j))
            for k, g in zip(s5_names, d_par):
                grads[k][j] = g
            grads['s5_d'][j], grads['s5_w_glu'][j], grads['s5_w_out'][j] = dd, d_w_glu, d_w_out
        else:
            dh, d_w_qkv, dgq, dgk, d_w_out = _attn_bwd(dr1_bf, h_in_bf, sv, (fw['attn_w_qkv'], j), qg[j], kg[j],
                                                       (fw['attn_w_out'], j), tabs)
            grads['attn_w_qkv'][j], grads['attn_w_out'][j] = d_w_qkv, d_w_out
            grads['attn_q_gain'][j], grads['attn_k_gain'][j] = dgq, dgk
        d_a, d_b = dr1, dh
    dh0 = _ew("dh0", lambda i, a, b: ((ALPHA * a + b,), ()), [d_a, d_b], [], [(D, f32)])[0]
    mats = {m[0] for m in MATS}
    full = {k: (v if k in mats else jnp.stack(v)) for k, v in grads.items() if v[0] is not None}
    full['meta_tokens'] = dh0[PAD:OFF]
    full['ln_gain'] = jnp.stack([jnp.stack(r) for r in d_ln_gain])
    full['ln_bias'] = jnp.stack([jnp.stack(r) for r in d_ln_bias])

    return loss, dh0[OFF:], full


def kernel(x, meta_tokens, s5_lambda_re, s5_lambda_im, s5_log_dt, s5_b_re, s5_b_im, s5_c_re, s5_c_im, s5_d, s5_w_glu, s5_w_out, attn_w_qkv, attn_q_gain, attn_k_gain, attn_w_out, ffn_w_gate, ffn_w_up, ffn_w_down, ln_gain, ln_bias, loss_target, m_meta_tokens, m_s5_lambda_re, m_s5_lambda_im, m_s5_log_dt, m_s5_b_re, m_s5_b_im, m_s5_c_re, m_s5_c_im, m_s5_d, m_s5_w_glu, m_s5_w_out, m_attn_w_qkv, m_attn_q_gain, m_attn_k_gain, m_attn_w_out, m_ffn_w_gate, m_ffn_w_up, m_ffn_w_down, m_ln_gain, m_ln_bias, v_meta_tokens, v_s5_lambda_re, v_s5_lambda_im, v_s5_log_dt, v_s5_b_re, v_s5_b_im, v_s5_c_re, v_s5_c_im, v_s5_d, v_s5_w_glu, v_s5_w_out, v_attn_w_qkv, v_attn_q_gain, v_attn_k_gain, v_attn_w_out, v_ffn_w_gate, v_ffn_w_up, v_ffn_w_down, v_ln_gain, v_ln_bias):
    w = dict(zip(WEIGHTS, (meta_tokens, s5_lambda_re, s5_lambda_im, s5_log_dt, s5_b_re, s5_b_im, s5_c_re, s5_c_im, s5_d, s5_w_glu, s5_w_out, attn_w_qkv, attn_q_gain, attn_k_gain, attn_w_out, ffn_w_gate, ffn_w_up, ffn_w_down, ln_gain, ln_bias)))
    mom = dict(zip(WEIGHTS, (m_meta_tokens, m_s5_lambda_re, m_s5_lambda_im, m_s5_log_dt, m_s5_b_re, m_s5_b_im, m_s5_c_re, m_s5_c_im, m_s5_d, m_s5_w_glu, m_s5_w_out, m_attn_w_qkv, m_attn_q_gain, m_attn_k_gain, m_attn_w_out, m_ffn_w_gate, m_ffn_w_up, m_ffn_w_down, m_ln_gain, m_ln_bias)))
    vel = dict(zip(WEIGHTS, (v_meta_tokens, v_s5_lambda_re, v_s5_lambda_im, v_s5_log_dt, v_s5_b_re, v_s5_b_im, v_s5_c_re, v_s5_c_im, v_s5_d, v_s5_w_glu, v_s5_w_out, v_attn_w_qkv, v_attn_q_gain, v_attn_k_gain, v_attn_w_out, v_ffn_w_gate, v_ffn_w_up, v_ffn_w_down, v_ln_gain, v_ln_bias)))
    cc = lax.axis_index("c")
    dev = 4 * lax.axis_index("x") + 2 * lax.axis_index("y") + cc

    mat_rows = jnp.concatenate([_mat_rows(w[n], t, blk) for n, t, blk, _ in MATS]).astype(bf16)
    g_mats = _all_gather("ag_weights", mat_rows)
    g_vecs = _all_gather("ag_vectors", jnp.concatenate([w[n].reshape(-1, 128) for n in VECS]))
    fw, off = {}, 0
    for n, _, blk, _ in MATS:
        rows = w[n].shape[0] * blk
        fw[n] = _mat_full(g_mats[:, off:off + rows], blk)
        off += rows
    off = 0
    for n in VECS:
        rows = w[n].size // 128
        fw[n] = _vec_full(g_vecs[:, off:off + rows]).reshape(w[n].shape[:-1] + (D,))
        off += rows

    loss, grad_x, full = _local_step(x[0], loss_target[0], w, fw)
    loss = lax.psum(loss, AXES)
    grad_x = grad_x[None]

    small_names = REPL + VECS
    mine, theirs = _grad_slots(full, _pack_rows([full[k] for k in small_names], REP_ROWS), cc)
    red, red_small = _reduce_scatter(mine, theirs)
    small_all = _all_gather("ag_small_grads", red_small).reshape(REP_ROWS * D)
    g, off = {}, 0
    for n, t, blk, real in MATS:
        rows = w[n].shape[0] * blk
        g[n] = _mat_block(red[off:off + rows], t, blk, real)
        off += rows
    small = dict(zip(small_names, _unpack(small_all, [full[k].shape for k in small_names])))
    for k in REPL:
        g[k] = small[k]
    for k in VECS:
        g[k] = lax.dynamic_slice_in_dim(small[k], dev * 128, 128, axis=small[k].ndim - 1)

    delta, new_m, new_v = {}, {}, {}
    for n in WEIGHTS:
        shp = w[n].shape
        res = _adamw(*[d[n].reshape(-1, shp[-1]) for d in (w, g, mom, vel)])
        delta[n], new_m[n], new_v[n] = [a.reshape(shp) for a in res]
    return (loss, grad_x, *[g[k] for k in WEIGHTS], *[delta[k] for k in WEIGHTS],
            *[new_m[k] for k in WEIGHTS], *[new_v[k] for k in WEIGHTS])
```

```python
import functools
import math

import jax
import jax.numpy as jnp
from jax import lax
from jax.experimental import pallas as pl
from jax.experimental.pallas import tpu as pltpu

f32 = jnp.float32
bf16 = jnp.bfloat16
HIGH = lax.Precision.HIGH
MESH = pl.DeviceIdType.MESH
AXES = ("x", "y", "c")
ANY = pl.BlockSpec(memory_space=pl.ANY)

D = 1024
DEPTH = 4
N_META = 16
PAD = 240
OFF = PAD + N_META
ROW_TILE = 768
KEY_CHUNK = 256
FFN_TILE = 256
ADAM_TILE = 544
GRID_W = 64
HD = 64
NQ = 16
NKV = 4
QW = NQ * HD
KW = NKV * HD
QKVW = QW + 2 * KW
DFF = 2816
GROUPS = 64
GCH = 16
NSTATE = 64
CHUNK = 16
GB = 8
ROPE_THETA = 10000.0
LN_EPS = 1e-5
QK_EPS = 1e-6
ALPHA = (2.0 * DEPTH) ** 0.25
ADAM_LR, ADAM_B1, ADAM_B2, ADAM_EPS, ADAM_WD, ADAM_STEP = 0.001, 0.9, 0.999, 1e-08, 0.01, 10
NEG = -1e30
Q_SCALE = HD ** -0.5 * math.log2(math.e)
VMEM_MB = 56

NT = (((1,), (1,)), ((), ()))
TN = (((0,), (0,)), ((), ()))

WEIGHTS = ['meta_tokens', 's5_lambda_re', 's5_lambda_im', 's5_log_dt', 's5_b_re', 's5_b_im', 's5_c_re', 's5_c_im',
           's5_d', 's5_w_glu', 's5_w_out', 'attn_w_qkv', 'attn_q_gain', 'attn_k_gain', 'attn_w_out', 'ffn_w_gate',
           'ffn_w_up', 'ffn_w_down', 'ln_gain', 'ln_bias']
DFFP = 3072
FF_BLK, FF_BLKP = DFF // 8, DFFP // 8
MATS = [('s5_w_glu', False, 128, 128), ('s5_w_out', False, 128, 128), ('attn_w_qkv', True, 192, 192),
        ('attn_w_out', False, 128, 128), ('ffn_w_gate', True, FF_BLKP, FF_BLK), ('ffn_w_up', True, FF_BLKP, FF_BLK),
        ('ffn_w_down', False, FF_BLKP, FF_BLK)]
VECS = ['meta_tokens', 'ln_gain', 'ln_bias']
REPL = ['s5_lambda_re', 's5_lambda_im', 's5_log_dt', 's5_b_re', 's5_b_im', 's5_c_re', 's5_c_im', 's5_d',
        'attn_q_gain', 'attn_k_gain']
MAT_ROWS = 5760
REP_PIECE = 160
REP_ROWS = 8 * REP_PIECE
RS_TILE = 640


def _params(sem, mb=VMEM_MB):
    return pltpu.CompilerParams(dimension_semantics=sem, vmem_limit_bytes=mb << 20)


def _ew(name, fn, rows, consts, outs, accs=(), tile=ROW_TILE):
    first = rows[0][0] if isinstance(rows[0], tuple) else rows[0]
    n = first.shape[-2]
    tile = min(tile, n)
    assert n % tile == 0, (name, n, tile)
    n_in, n_o, n_a = len(rows) + len(consts), len(outs), len(accs)

    def body(*refs):
        i = pl.program_id(0)
        res_o, res_a = fn(i, *[r[...] for r in refs[:n_in]])
        for r, val in zip(refs[n_in:n_in + n_o], res_o):
            r[...] = val.astype(r.dtype)
        if n_a:
            a_refs = refs[n_in + n_o:]

            @pl.when(i == 0)
            def _():
                for r in a_refs:
                    r[...] = jnp.zeros(r.shape, r.dtype)

            for r, val in zip(a_refs, res_a):
                r[...] += val

    in_specs, args = [], []
    for a in rows:
        if isinstance(a, tuple):
            arr, k = a
            in_specs.append(pl.BlockSpec((None, tile, arr.shape[2]), functools.partial(lambda i, k: (k, i, 0), k=k)))
            args.append(arr)
        else:
            in_specs.append(pl.BlockSpec((tile, a.shape[1]), lambda i: (i, 0)))
            args.append(a)
    for c in consts:
        in_specs.append(pl.BlockSpec(c.shape, lambda i: (0, 0)))
        args.append(c)
    out_specs = [pl.BlockSpec((tile, c), lambda i: (i, 0)) for c, _ in outs]
    out_specs += [pl.BlockSpec(s, lambda i: (0, 0)) for s in accs]
    out_shape = [jax.ShapeDtypeStruct((n, c), dt) for c, dt in outs]
    out_shape += [jax.ShapeDtypeStruct(s, f32) for s in accs]
    res = pl.pallas_call(body, grid=(n // tile,), in_specs=in_specs, out_specs=out_specs, out_shape=out_shape,
                         name=name, compiler_params=_params(("arbitrary",)))(*args)
    return res


def _mm(name, a, b, trans_b=False, out_dtype=f32, tm=ROW_TILE):
    m, k = a.shape
    spec, b, shape = _whole(b)
    n = shape[0] if trans_b else shape[1]
    tm = min(tm, m)
    assert m % tm == 0
    dims = NT if trans_b else (((1,), (0,)), ((), ()))

    def body(a_ref, b_ref, o_ref):
        o_ref[...] = lax.dot_general(a_ref[...], b_ref[...], dims, preferred_element_type=f32).astype(o_ref.dtype)

    return pl.pallas_call(
        body, grid=(m // tm,), in_specs=[pl.BlockSpec((tm, k), lambda i: (i, 0)), spec],
        out_specs=pl.BlockSpec((tm, n), lambda i: (i, 0)),
        out_shape=jax.ShapeDtypeStruct((m, n), out_dtype), name=name, compiler_params=_params(("parallel",)))(a, b)


def _whole(b):
    if isinstance(b, tuple):
        arr, layer = b
        return pl.BlockSpec((None,) + arr.shape[1:], lambda i: (layer, 0, 0)), arr, arr.shape[1:]
    return pl.BlockSpec(b.shape, lambda i: (0, 0)), b, b.shape


def _mm2(name, a1, b1, a2, b2, out_dtype=f32, tm=ROW_TILE // 2):
    m, k = a1.shape
    spec1, b1, shape = _whole(b1)
    spec2, b2, _ = _whole(b2)
    n = shape[1]
    tm = min(tm, m)
    assert m % tm == 0

    def body(a1_ref, b1_ref, a2_ref, b2_ref, o_ref):
        acc = jnp.dot(a1_ref[...], b1_ref[...], preferred_element_type=f32)
        acc += jnp.dot(a2_ref[...], b2_ref[...], preferred_element_type=f32)
        o_ref[...] = acc.astype(o_ref.dtype)

    row = pl.BlockSpec((tm, k), lambda i: (i, 0))
    return pl.pallas_call(
        body, grid=(m // tm,), in_specs=[row, spec1, row, spec2], out_specs=pl.BlockSpec((tm, n), lambda i: (i, 0)),
        out_shape=jax.ShapeDtypeStruct((m, n), out_dtype), name=name,
        compiler_params=_params(("parallel",)))(a1, b1, a2, b2)


def _mm_tn(name, a, g, tk=512, tl=ROW_TILE):
    rows, k1 = a.shape
    n = g.shape[1]
    tl = min(tl, rows)
    assert rows % tl == 0 and k1 % tk == 0

    def body(a_ref, g_ref, o_ref):
        @pl.when(pl.program_id(1) == 0)
        def _():
            o_ref[...] = jnp.zeros(o_ref.shape, f32)

        o_ref[...] += lax.dot_general(a_ref[...], g_ref[...], TN, preferred_element_type=f32)

    return pl.pallas_call(
        body, grid=(k1 // tk, rows // tl),
        in_specs=[pl.BlockSpec((tl, tk), lambda k, l: (l, k)), pl.BlockSpec((tl, n), lambda k, l: (l, 0))],
        out_specs=pl.BlockSpec((tk, n), lambda k, l: (k, 0)),
        out_shape=jax.ShapeDtypeStruct((k1, n), f32), name=name,
        compiler_params=_params(("parallel", "arbitrary")))(a, g)


def _ln_stats(r):
    mean = jnp.mean(r, axis=-1, keepdims=True)
    c = r - mean
    rstd = lax.rsqrt(jnp.mean(c * c, axis=-1, keepdims=True) + LN_EPS)
    return c * rstd, rstd


def _mm_ln(name, a, b, h, gain, bias, tm=ROW_TILE):
    m, k = a.shape
    spec, b, _ = _whole(b)

    def body(a_ref, b_ref, h_ref, g_ref, bias_ref, r_ref, y_ref, yb_ref):
        r = ALPHA * h_ref[...] + jnp.dot(a_ref[...], b_ref[...], preferred_element_type=f32)
        y = _ln_stats(r)[0] * g_ref[...] + bias_ref[...]
        r_ref[...] = r
        y_ref[...] = y
        yb_ref[...] = y.astype(bf16)

    row = pl.BlockSpec((tm, D), lambda i: (i, 0))
    vec = pl.BlockSpec((1, D), lambda i: (0, 0))
    return pl.pallas_call(
        body, grid=(m // tm,), in_specs=[pl.BlockSpec((tm, k), lambda i: (i, 0)), spec, row, vec, vec],
        out_specs=[row, row, row],
        out_shape=[jax.ShapeDtypeStruct((m, D), f32), jax.ShapeDtypeStruct((m, D), f32), jax.ShapeDtypeStruct((m, D), bf16)],
        name=name, compiler_params=_params(("parallel",)))(a, b, h, gain, bias)


def _ln_bwd(d_a, d_b, r, gain):
    def core(dout, r, g):
        xhat, rstd = _ln_stats(r)
        dxh = dout * g
        dr = rstd * (dxh - jnp.mean(dxh, axis=-1, keepdims=True) - xhat * jnp.mean(dxh * xhat, axis=-1, keepdims=True))
        return (dr, dr), (jnp.sum(dout * xhat, axis=0, keepdims=True), jnp.sum(dout, axis=0, keepdims=True))

    outs, accs = [(D, f32), (D, bf16)], [(1, D), (1, D)]
    if d_a is None:
        return _ew("ln_bwd_top", lambda i, d, r, g: core(d, r, g), [d_b, r], [gain], outs, accs)
    return _ew("ln_bwd", lambda i, da, db, r, g: core(ALPHA * da + db, r, g), [d_a, d_b, r], [gain], outs, accs)


def _sigmoid(x):
    return 1.0 / (1.0 + jnp.exp(-x))


def _ffn_up(h_bf, w_gate_t, w_up_t, tm=FFN_TILE):
    m, k = h_bf.shape
    gspec, w_gate_t, (n, _) = _whole(w_gate_t)
    uspec, w_up_t, _ = _whole(w_up_t)

    def body(h_ref, wg_ref, wu_ref, g_ref, u_ref, a_ref):
        h = h_ref[...]
        g = lax.dot_general(h, wg_ref[...], NT, preferred_element_type=f32).astype(bf16)
        u = lax.dot_general(h, wu_ref[...], NT, preferred_element_type=f32).astype(bf16)
        g_ref[...] = g
        u_ref[...] = u
        g = g.astype(f32)
        a_ref[...] = (g * _sigmoid(g) * u.astype(f32)).astype(bf16)

    row = pl.BlockSpec((tm, n), lambda i: (i, 0))
    return pl.pallas_call(
        body, grid=(m // tm,), in_specs=[pl.BlockSpec((tm, k), lambda i: (i, 0)), gspec, uspec],
        out_specs=[row, row, row], out_shape=[jax.ShapeDtypeStruct((m, n), bf16)] * 3, name="ffn_up",
        compiler_params=_params(("parallel",)))(h_bf, w_gate_t, w_up_t)


def _ffn_dup(df_bf, w_down, gate, up, tm=FFN_TILE):
    m, k = df_bf.shape
    wspec, w_down, (n, _) = _whole(w_down)

    def body(d_ref, w_ref, g_ref, u_ref, dg_ref, du_ref):
        da = lax.dot_general(d_ref[...], w_ref[...], NT, preferred_element_type=f32).astype(bf16).astype(f32)
        g, u = g_ref[...].astype(f32), u_ref[...].astype(f32)
        s = _sigmoid(g)
        dg_ref[...] = (da * u * s * (1.0 + g * (1.0 - s))).astype(bf16)
        du_ref[...] = (da * g * s).astype(bf16)

    row = pl.BlockSpec((tm, n), lambda i: (i, 0))
    return pl.pallas_call(
        body, grid=(m // tm,),
        in_specs=[pl.BlockSpec((tm, k), lambda i: (i, 0)), wspec, row, row],
        out_specs=[row, row], out_shape=[jax.ShapeDtypeStruct((m, n), bf16)] * 2, name="ffn_dup",
        compiler_params=_params(("parallel",)))(df_bf, w_down, gate, up)


def _loss_grad(h, target):
    n = h.shape[0]

    def body(h_ref, t_ref, d_ref, sq_ref):
        i = pl.program_id(0)

        @pl.when(i == 0)
        def _():
            d_ref[...] = jnp.zeros(d_ref.shape, f32)
            sq_ref[...] = jnp.zeros(sq_ref.shape, f32)

        @pl.when(i > 0)
        def _():
            e = h_ref[...] - t_ref[...]
            d_ref[...] = e * (1.0 / D)
            sq_ref[...] += jnp.sum(e * e, axis=0, keepdims=True)

    return pl.pallas_call(
        body, grid=(n // OFF,),
        in_specs=[pl.BlockSpec((OFF, D), lambda i: (i, 0)), pl.BlockSpec((OFF, D), lambda i: (jnp.maximum(i - 1, 0), 0))],
        out_specs=[pl.BlockSpec((OFF, D), lambda i: (i, 0)), pl.BlockSpec((1, D), lambda i: (0, 0))],
        out_shape=[jax.ShapeDtypeStruct((n, D), f32), jax.ShapeDtypeStruct((1, D), f32)], name="loss",
        compiler_params=_params(("arbitrary",)))(h, target)


def _adamw(w, g, m, v):
    def fn(i, w, g, m, v):
        m = ADAM_B1 * m + (1.0 - ADAM_B1) * g
        v = ADAM_B2 * v + (1.0 - ADAM_B2) * jnp.square(g)
        m_hat = m / (1.0 - ADAM_B1 ** ADAM_STEP)
        v_hat = v / (1.0 - ADAM_B2 ** ADAM_STEP)
        delta = -ADAM_LR * (m_hat / (jnp.sqrt(v_hat) + ADAM_EPS) + ADAM_WD * w)
        return (delta, m, v), ()

    rows, cols = w.shape
    cap = ADAM_TILE if cols > 128 else 4 * ADAM_TILE
    fits = [t for t in range(8, min(rows, cap) + 1, 8) if rows % t == 0]
    return _ew("adamw", fn, [w, g, m, v], [], [(cols, f32)] * 3, tile=max(fits) if fits else rows)


def _s5_mats(lam_re, lam_im, log_dt, b_re, b_im, c_re, c_im):
    steps = jnp.arange(CHUNK + 1, dtype=f32)
    n = CHUNK * GCH
    last = n - GCH

    def one(lr, li, ldt, br, bi, cr, ci, reverse):
        dt = jnp.exp(ldt)[:, None]
        mag = jnp.exp(lr * dt)
        abr, abi = mag * jnp.cos(li * dt), mag * jnp.sin(li * dt)
        nr, ni = abr - 1.0, abi
        den = lr * lr + li * li
        zr, zi = (nr * lr + ni * li) / den, (ni * lr - nr * li) / den
        bbr = zr[..., None] * br - zi[..., None] * bi
        bbi = zr[..., None] * bi + zi[..., None] * br
        pmag = jnp.exp((lr * dt)[..., None] * steps)
        pang = (li * dt)[..., None] * steps
        pr, pi = pmag * jnp.cos(pang), pmag * jnp.sin(pang)
        crt, cit = jnp.swapaxes(cr, 1, 2)[:, :, None, :], jnp.swapaxes(ci, 1, 2)[:, :, None, :]
        car = crt * pr[..., None] - cit * pi[..., None]
        cai = crt * pi[..., None] + cit * pr[..., None]
        if reverse:
            taps = slice(CHUNK - 1, None, -1)
            outs = slice(CHUNK, 0, -1)
            ins = slice(0, CHUNK)
        else:
            taps, outs, ins = slice(0, CHUNK), slice(1, CHUNK + 1), slice(CHUNK - 1, None, -1)
        kern = (jnp.einsum('gpi,gpq->giq', bbr, car[:, :, taps].reshape(GROUPS, NSTATE, n), precision=HIGH)
                - jnp.einsum('gpi,gpq->giq', bbi, cai[:, :, taps].reshape(GROUPS, NSTATE, n), precision=HIGH))
        wide = jnp.pad(kern, ((0, 0), (0, 0), (0, last) if reverse else (last, 0)))
        m = jnp.stack([wide[:, :, last - GCH * t:last - GCH * t + n] for t in range(CHUNK)], axis=1)
        qr = jnp.swapaxes(pr[:, :, ins], 1, 2)[:, :, None, :]
        qi = jnp.swapaxes(pi[:, :, ins], 1, 2)[:, :, None, :]
        bbrt, bbit = jnp.swapaxes(bbr, 1, 2)[:, None], jnp.swapaxes(bbi, 1, 2)[:, None]
        pin = jnp.concatenate([qr * bbrt - qi * bbit, qr * bbit + qi * bbrt], axis=-1)
        pout = jnp.concatenate([car[:, :, outs].reshape(GROUPS, NSTATE, n),
                                -cai[:, :, outs].reshape(GROUPS, NSTATE, n)], axis=1)
        return (m.reshape(GROUPS, n, n), pin.reshape(GROUPS, n, 2 * NSTATE), pout, pr[:, :, CHUNK], pi[:, :, CHUNK])

    mf, pinf, poutf, arf, aif = one(lam_re[0], lam_im[0], log_dt[0], b_re[0], b_im[0], c_re[0], c_im[0], False)
    mr, pinr, poutr, arr, air = one(lam_re[1], lam_im[1], log_dt[1], b_re[1], b_im[1], c_re[1], c_im[1], True)
    return (mf + mr, jnp.concatenate([pinf, pinr], 2), jnp.concatenate([poutf, poutr], 1),
            jnp.stack([arf, arr]), jnp.stack([aif, air]))


def _s5_coefs(a_re, a_im):
    c1 = jnp.concatenate([a_re, a_re], -1)
    c2 = jnp.concatenate([-a_im, a_im], -1)
    return tuple(c.reshape(GROUPS // GB, 1, GB * 2 * NSTATE) for c in (c1[0], c2[0], c1[1], c2[1]))


def _swap(s):
    w = s.shape[1]
    lane = lax.broadcasted_iota(jnp.int32, s.shape, 1)
    return jnp.where(lane % (2 * NSTATE) < NSTATE, pltpu.roll(s, w - NSTATE, 1), pltpu.roll(s, NSTATE, 1))


def _group_lanes(g):
    return slice(g * 2 * NSTATE, (g + 1) * 2 * NSTATE)


def _s5_states(nc, u_ref, pin_ref, coef, vf, vr, wf, wr, sf, sr):
    c1f, c2f, c1r, c2r = coef
    for g in range(GB):
        v = jnp.dot(u_ref[g], pin_ref[g], preferred_element_type=f32)
        vf[:, _group_lanes(g)] = v[:, :2 * NSTATE]
        vr[:, _group_lanes(g)] = v[:, 2 * NSTATE:]
    wf[...] = _swap(vf[...])
    wr[...] = _swap(vr[...])

    def step(i, carry):
        s_f, t_f, s_r, t_r = carry
        kf, kr = pl.ds(i, 1), pl.ds(nc - 1 - i, 1)
        sf[kf, :] = s_f
        sr[kr, :] = s_r
        s_f, t_f = c1f * s_f + c2f * t_f + vf[kf, :], c1f * t_f - c2f * s_f + wf[kf, :]
        s_r, t_r = c1r * s_r + c2r * t_r + vr[kr, :], c1r * t_r - c2r * s_r + wr[kr, :]
        return s_f, t_f, s_r, t_r

    z = jnp.zeros((1, GB * 2 * NSTATE), f32)
    lax.fori_loop(0, nc, step, (z, z, z, z))


def _s5_core_fwd(ug, msum, pin, pout, coefs):
    nc = ug.shape[1]
    n = CHUNK * GCH

    def body(u_ref, m_ref, pin_ref, pout_ref, c1f, c2f, c1r, c2r, y_ref, vf, vr, wf, wr, sf, sr):
        coef = (c1f[...], c2f[...], c1r[...], c2r[...])
        _s5_states(nc, u_ref, pin_ref, coef, vf, vr, wf, wr, sf, sr)
        for g in range(GB):
            s_in = jnp.concatenate([sf[:, _group_lanes(g)], sr[:, _group_lanes(g)]], axis=1).astype(bf16)
            y_ref[g] = (jnp.dot(u_ref[g], m_ref[g], preferred_element_type=f32)
                        + jnp.dot(s_in, pout_ref[g], preferred_element_type=f32)).astype(bf16)

    seq = pl.BlockSpec((GB, nc, n), lambda i: (i, 0, 0))
    mat = pl.BlockSpec((GB, n, n), lambda i: (i, 0, 0))
    cf = pl.BlockSpec((None, 1, GB * 2 * NSTATE), lambda i: (i, 0, 0))
    scr = pltpu.VMEM((nc, GB * 2 * NSTATE), f32)
    return pl.pallas_call(
        body, grid=(GROUPS // GB,), in_specs=[seq, mat, mat, mat, cf, cf, cf, cf], out_specs=seq,
        out_shape=jax.ShapeDtypeStruct((GROUPS, nc, n), bf16), scratch_shapes=[scr] * 6,
        name="s5_core_fwd", compiler_params=_params(("parallel",)))(ug, msum, pin, pout, *coefs)


def _s5_core_bwd(ug, dyg, msum, pin, pout, coefs):
    nc = ug.shape[1]
    n = CHUNK * GCH

    def body(u_ref, dy_ref, m_ref, pin_ref, pout_ref, c1f, c2f, c1r, c2r,
             du_ref, dm_ref, dpin_ref, dpout_ref, a1f_ref, a2f_ref, a1r_ref, a2r_ref, vf, vr, wf, wr, sf, sr):
        coef = (c1f[...], c2f[...], c1r[...], c2r[...])
        _s5_states(nc, u_ref, pin_ref, coef, vf, vr, wf, wr, sf, sr)
        for g in range(GB):
            s_in = jnp.concatenate([sf[:, _group_lanes(g)], sr[:, _group_lanes(g)]], axis=1).astype(bf16)
            dy = dy_ref[g]
            ds = lax.dot_general(dy, pout_ref[g], NT, preferred_element_type=f32)
            vf[:, _group_lanes(g)] = ds[:, :2 * NSTATE]
            vr[:, _group_lanes(g)] = ds[:, 2 * NSTATE:]
            dpout_ref[g] = lax.dot_general(s_in, dy, TN, preferred_element_type=f32)
            dm_ref[g] = lax.dot_general(u_ref[g], dy, TN, preferred_element_type=f32)

        wf[...] = _swap(vf[...])
        wr[...] = _swap(vr[...])
        k1f, k2f, k1r, k2r = coef[0], -coef[1], coef[2], -coef[3]

        def step(i, carry):
            g_f, h_f, g_r, h_r, a1f, b2f, a1r, b2r = carry
            kf, kr = pl.ds(nc - 1 - i, 1), pl.ds(i, 1)
            s_f, s_r = sf[kf, :], sr[kr, :]
            sf[kf, :] = g_f
            sr[kr, :] = g_r
            a1f, b2f = a1f + g_f * s_f, b2f + h_f * s_f
            a1r, b2r = a1r + g_r * s_r, b2r + h_r * s_r
            g_f, h_f = vf[kf, :] + k1f * g_f + k2f * h_f, wf[kf, :] + k1f * h_f - k2f * g_f
            g_r, h_r = vr[kr, :] + k1r * g_r + k2r * h_r, wr[kr, :] + k1r * h_r - k2r * g_r
            return g_f, h_f, g_r, h_r, a1f, b2f, a1r, b2r

        z = jnp.zeros((1, GB * 2 * NSTATE), f32)
        _, _, _, _, a1f, b2f, a1r, b2r = lax.fori_loop(0, nc, step, (z,) * 8)
        a1f_ref[...], a2f_ref[...], a1r_ref[...], a2r_ref[...] = a1f, _swap(b2f), a1r, _swap(b2r)
        for g in range(GB):
            dv = jnp.concatenate([sf[:, _group_lanes(g)], sr[:, _group_lanes(g)]], axis=1).astype(bf16)
            du_ref[g] = (lax.dot_general(dy_ref[g], m_ref[g], NT, preferred_element_type=f32)
                         + lax.dot_general(dv, pin_ref[g], NT, preferred_element_type=f32)).astype(bf16)
            dpin_ref[g] = lax.dot_general(u_ref[g], dv, TN, preferred_element_type=f32)

    seq = pl.BlockSpec((GB, nc, n), lambda i: (i, 0, 0))
    mat = pl.BlockSpec((GB, n, n), lambda i: (i, 0, 0))
    cf = pl.BlockSpec((None, 1, GB * 2 * NSTATE), lambda i: (i, 0, 0))
    scr = pltpu.VMEM((nc, GB * 2 * NSTATE), f32)
    mat_s = jax.ShapeDtypeStruct((GROUPS, n, n), f32)
    cf_s = jax.ShapeDtypeStruct((GROUPS // GB, 1, GB * 2 * NSTATE), f32)
    return pl.pallas_call(
        body, grid=(GROUPS // GB,), in_specs=[seq, seq, mat, mat, mat, cf, cf, cf, cf],
        out_specs=[seq, mat, mat, mat, cf, cf, cf, cf],
        out_shape=[jax.ShapeDtypeStruct((GROUPS, nc, n), bf16), mat_s, mat_s, mat_s, cf_s, cf_s, cf_s, cf_s],
        scratch_shapes=[scr] * 6, name="s5_core_bwd",
        compiler_params=_params(("parallel",)))(ug, dyg, msum, pin, pout, *coefs)


def _block_movers():
    a_in, l_in = jnp.divmod(jnp.arange(GB * 128, dtype=jnp.int32), 128)
    a_out, c_out = jnp.divmod(jnp.arange(128, dtype=jnp.int32), GCH)
    j = jnp.arange(GB, dtype=jnp.int32)[:, None, None]
    hit = (a_in[None, :, None] == a_out[None, None, :]) & (l_in[None, :, None] == GCH * j + c_out[None, None, :])
    return hit.astype(bf16)


def _to_groups(x, movers, mask):
    n = x.shape[0]
    nc = n // CHUNK
    half = CHUNK // 2

    def body(x_ref, mv_ref, o_ref):
        keep = lax.broadcasted_iota(jnp.int32, (nc, 1), 0) >= PAD // CHUNK
        steps = [x_ref[pl.ds(t, nc, stride=CHUNK), :] for t in range(CHUNK)]
        if mask:
            steps = [jnp.where(keep, s, 0.0) for s in steps]
        lo = jnp.concatenate(steps[:half], axis=1).astype(bf16)
        hi = jnp.concatenate(steps[half:], axis=1).astype(bf16)
        for g in range(GB):
            o_ref[g] = jnp.concatenate([jnp.dot(lo, mv_ref[g], preferred_element_type=f32),
                                        jnp.dot(hi, mv_ref[g], preferred_element_type=f32)], axis=1).astype(bf16)

    return pl.pallas_call(
        body, grid=(GROUPS // GB,),
        in_specs=[pl.BlockSpec((n, 128), lambda i: (0, i)), pl.BlockSpec(movers.shape, lambda i: (0, 0, 0))],
        out_specs=pl.BlockSpec((GB, nc, CHUNK * GCH), lambda i: (i, 0, 0)),
        out_shape=jax.ShapeDtypeStruct((GROUPS, nc, CHUNK * GCH), bf16), name="s5_to_groups",
        compiler_params=_params(("parallel",)))(x, movers)


def _from_groups(y, movers, base=None):
    nc = y.shape[1]
    n = nc * CHUNK
    half = CHUNK // 2

    def body(*refs):
        y_ref, mv_ref = refs[:2]
        o_ref = refs[-1]
        keep = lax.broadcasted_iota(jnp.int32, (nc, 1), 0) >= PAD // CHUNK
        lo = jnp.concatenate([y_ref[g][:, :128] for g in range(GB)], axis=1)
        hi = jnp.concatenate([y_ref[g][:, 128:] for g in range(GB)], axis=1)
        for t in range(CHUNK):
            rows = pl.ds(t, nc, stride=CHUNK)
            v = jnp.dot(lo if t < half else hi, mv_ref[t % half], preferred_element_type=f32)
            if base is not None:
                v = refs[2][rows, :] + jnp.where(keep, v, 0.0)
            o_ref[rows, :] = v

    tok = pl.BlockSpec((n, 128), lambda i: (0, i))
    args = (y, movers) if base is None else (y, movers, base)
    return pl.pallas_call(
        body, grid=(GROUPS // GB,),
        in_specs=[pl.BlockSpec((GB, nc, CHUNK * GCH), lambda i: (i, 0, 0)),
                  pl.BlockSpec(movers.shape, lambda i: (0, 0, 0))] + ([] if base is None else [tok]),
        out_specs=tok, out_shape=jax.ShapeDtypeStruct((n, D), f32), name="s5_from_groups",
        compiler_params=_params(("parallel",)))(*args)


def _gelu(y):
    return 0.5 * y * (1.0 + lax.erf(y * (2.0 ** -0.5)))


def _gelu_grad(y):
    return 0.5 * (1.0 + lax.erf(y * (2.0 ** -0.5))) + y * jnp.exp(-0.5 * y * y) * (1.0 / math.sqrt(2.0 * math.pi))


def _s5_fwd(h, movers, mats, d_skip, w_glu):
    msum, pin, pout, a_re, a_im = mats
    coefs = _s5_coefs(a_re, a_im)
    ug = _to_groups(h, movers, mask=True)
    ys = _from_groups(_s5_core_fwd(ug, msum.astype(bf16), pin.astype(bf16), pout.astype(bf16), coefs), movers)

    def post(i, ys, h, d):
        y = ys + d * h
        return (y, _gelu(y)), ()

    y, g_bf = _ew("s5_gelu", post, [ys, h], [d_skip], [(D, f32), (D, bf16)])
    gw = _mm("s5_glu_mm", g_bf, w_glu)

    def glu(i, y, gw):
        return (_gelu(y) * _sigmoid(gw),), ()

    z_bf = _ew("s5_glu", glu, [y, gw], [], [(D, bf16)])[0]
    return z_bf, (ug, y, g_bf, gw, z_bf)


def _s5_bwd(dmix_bf, h, movers, saved, mats, vjp_mats, d_skip, w_glu, w_out):
    ug, y, g_bf, gw, z_bf = saved
    msum, pin, pout, a_re, a_im = mats
    coefs = _s5_coefs(a_re, a_im)
    dz = _mm("s5_dz_mm", dmix_bf, w_out, trans_b=True)
    d_w_out = _mm_tn("s5_dwout", z_bf, dmix_bf)

    def dglu(i, dz, y, gw):
        g, s = _gelu(y), _sigmoid(gw)
        return (dz * g * s * (1.0 - s), dz * s), ()

    dgw_bf, dg1 = _ew("s5_dglu", dglu, [dz, y, gw], [], [(D, bf16), (D, f32)])
    d_w_glu = _mm_tn("s5_dwglu", g_bf, dgw_bf)
    dg2 = _mm("s5_dg_mm", dgw_bf, w_glu, trans_b=True)

    def dgelu(i, dg1, dg2, y, h, d):
        dy = (dg1 + dg2) * _gelu_grad(y)
        return (dy, dy * d), (jnp.sum(dy * h, axis=0, keepdims=True),)

    dy, dh_skip, dd = _ew("s5_dgelu", dgelu, [dg1, dg2, y, h], [d_skip], [(D, f32), (D, f32)], [(1, D)])
    dug, dm, dpin, dpout, a1f, a2f, a1r, a2r = _s5_core_bwd(
        ug, _to_groups(dy, movers, mask=False), msum.astype(bf16), pin.astype(bf16), pout.astype(bf16), coefs)
    dh = _from_groups(dug, movers, base=dh_skip)
    a1 = jnp.stack([a1f, a1r]).reshape(2, GROUPS, 2 * NSTATE)
    a2 = jnp.stack([a2f, a2r]).reshape(2, GROUPS, 2 * NSTATE)
    da_re = a1[..., :NSTATE] + a1[..., NSTATE:]
    da_im = a2[..., NSTATE:] - a2[..., :NSTATE]
    d_params = vjp_mats((dm, dpin, dpout, da_re, da_im))
    return dh, d_params, dd[0], d_w_glu, d_w_out


def _rope_tables(n):
    row = jnp.arange(n, dtype=jnp.int32) - OFF
    real = row >= 0
    rid = jnp.where(real, row // GRID_W, 0).astype(f32)
    cid = jnp.where(real, row % GRID_W, 0).astype(f32)
    half = HD // 2
    inv = ROPE_THETA ** (-jnp.arange(0, half, 2, dtype=f32) / half)
    ar, ac = rid[:, None] * inv[None, :], cid[:, None] * inv[None, :]
    cos = jnp.concatenate([jnp.cos(ar), jnp.cos(ar), jnp.cos(ac), jnp.cos(ac)], axis=1)
    sin = jnp.concatenate([-jnp.sin(ar), jnp.sin(ar), -jnp.sin(ac), jnp.sin(ac)], axis=1)
    return jnp.tile(cos, (1, 2)), jnp.tile(sin, (1, 2))


def _head_mats():
    head = jnp.arange(QW, dtype=jnp.int32)[:, None] // HD == jnp.arange(128, dtype=jnp.int32)[None, :]
    return head.astype(f32) * (1.0 / HD), head.astype(f32).T


def _rot(v):
    w = v.shape[1]
    lane = lax.broadcasted_iota(jnp.int32, v.shape, 1)
    return jnp.where(lane % 32 < 16, pltpu.roll(v, w - 16, 1), pltpu.roll(v, 16, 1))


def _head_mean(v, e, et):
    w = v.shape[1]
    m = jnp.dot(v, e[:w], preferred_element_type=f32, precision=HIGH)
    return m, et[:, :w]


def _rms_rope(t, gain, e, et, cos, sin):
    w = t.shape[1]
    ms, spread = _head_mean(t * t, e, et)
    rs = jnp.dot(lax.rsqrt(ms + QK_EPS), spread, preferred_element_type=f32, precision=HIGH)
    n0 = t * rs
    n = n0 * gain
    reps = w // 128
    return n * jnp.tile(cos, (1, reps)) + _rot(n) * jnp.tile(sin, (1, reps))


def _rms_rope_bwd(dout, t, gain, e, et, cos, sin):
    w = t.shape[1]
    reps = w // 128
    ms, spread = _head_mean(t * t, e, et)
    rs = jnp.dot(lax.rsqrt(ms + QK_EPS), spread, preferred_element_type=f32, precision=HIGH)
    n0 = t * rs
    dn = dout * jnp.tile(cos, (1, reps)) + _rot(dout * jnp.tile(sin, (1, reps)))
    dn0 = dn * gain
    mm, _ = _head_mean(dn0 * n0, e, et)
    corr = jnp.dot(mm, spread, preferred_element_type=f32, precision=HIGH)
    return rs * (dn0 - n0 * corr), jnp.sum(dn * n0, axis=0, keepdims=True)


def _qk_fwd(qkv, qg, kg, e, et, cos, sin):
    def fn(i, qkv, cos, sin, qg, kg, e, et):
        q = _rms_rope(qkv[:, :QW], qg, e, et, cos, sin) * Q_SCALE
        k = _rms_rope(qkv[:, QW:QW + KW], kg, e, et, cos, sin)
        return (q, k, qkv[:, QW + KW:]), ()

    return _ew("qk_rope", fn, [qkv, cos, sin], [qg, kg, e, et], [(QW, bf16), (KW, bf16), (KW, bf16)])


def _qk_bwd(qkv, dq, dk, dv, qg, kg, e, et, cos, sin):
    def fn(i, qkv, cos, sin, dq, dk, dv, qg, kg, e, et):
        dtq, dgq = _rms_rope_bwd(dq * (HD ** -0.5), qkv[:, :QW], qg, e, et, cos, sin)
        dtk, dgk = _rms_rope_bwd(dk * math.log(2.0), qkv[:, QW:QW + KW], kg, e, et, cos, sin)
        return (jnp.concatenate([dtq, dtk, dv], axis=1),), (dgq, dgk)

    return _ew("qk_rope_bwd", fn, [qkv, cos, sin, dq, dk, dv], [qg, kg, e, et], [(QKVW, bf16)], [(1, QW), (1, KW)])


def _to_heads(a, nh):
    return a.reshape(a.shape[0], nh, HD).transpose(1, 0, 2)


def _from_heads(a):
    return a.transpose(1, 0, 2).reshape(a.shape[1], a.shape[0] * HD)


def _masked_first(s, c):
    if c:
        return s
    col = lax.broadcasted_iota(jnp.int32, (1, s.shape[1]), 1)
    return jnp.where(col >= PAD, s, NEG)


def _flash_fwd(q, k, v1, tq=ROW_TILE, tc=KEY_CHUNK):
    n = q.shape[1]
    nc = n // tc

    def body(q_ref, k_ref, v_ref, o_ref, lse_ref):
        qb = q_ref[0]

        def scores(c):
            ks = k_ref[0, pl.ds(c * tc, tc), :]
            return _masked_first(lax.dot_general(qb, ks, NT, preferred_element_type=f32), c)

        m = jnp.full((tq, 1), NEG, f32)
        acc = jnp.zeros((tq, 2 * HD), f32)
        s_next = scores(0)
        for c in range(nc):
            s = s_next
            if c + 1 < nc:
                s_next = scores(c + 1)
            m_new = jnp.maximum(m, jnp.max(s, axis=1, keepdims=True))
            p = jnp.exp2(s - m_new)
            acc = jnp.exp2(m - m_new) * acc + jnp.dot(p.astype(bf16), v_ref[0, pl.ds(c * tc, tc), :],
                                                      preferred_element_type=f32)
            m = m_new
        l = acc[:, HD:HD + 1]
        o_ref[0] = acc[:, :HD] / l
        lse_ref[0] = m + jnp.log2(l)

    return pl.pallas_call(
        body, grid=(NQ, n // tq),
        in_specs=[pl.BlockSpec((1, tq, HD), lambda h, i: (h, i, 0)),
                  pl.BlockSpec((1, n, HD), lambda h, i: (h // (NQ // NKV), 0, 0)),
                  pl.BlockSpec((1, n, 2 * HD), lambda h, i: (h // (NQ // NKV), 0, 0))],
        out_specs=[pl.BlockSpec((1, tq, HD), lambda h, i: (h, i, 0)), pl.BlockSpec((1, tq, 1), lambda h, i: (h, i, 0))],
        out_shape=[jax.ShapeDtypeStruct((NQ, n, HD), f32), jax.ShapeDtypeStruct((NQ, n, 1), f32)],
        name="flash_fwd", compiler_params=_params(("parallel", "parallel")))(q, k, v1)


def _flash_bwd(q, k, kt, v, do, lse_row, delta_row, tk=ROW_TILE, tc=KEY_CHUNK):
    n = q.shape[1]
    nc = n // tc
    grp = NQ // NKV

    def body(q_ref, do_ref, lse_ref, delta_ref, k_ref, kt_ref, v_ref, dqt_ref, dk_ref, dv_ref):
        j, g = pl.program_id(1), pl.program_id(2)
        kb, vb, ktb = k_ref[0], v_ref[0], kt_ref[0]
        valid = lax.broadcasted_iota(jnp.int32, (tk, 1), 0) + j * tk >= PAD

        @pl.when(j == 0)
        def _():
            dqt_ref[g] = jnp.zeros((HD, n), f32)

        def products(c):
            rows = pl.ds(c * tc, tc)
            return (lax.dot_general(kb, q_ref[0, rows, :], NT, preferred_element_type=f32),
                    lax.dot_general(vb, do_ref[0, rows, :], NT, preferred_element_type=f32))

        dk = jnp.zeros((tk, HD), f32)
        dv = jnp.zeros((tk, HD), f32)
        nxt = products(0)
        for c in range(nc):
            st, dpt = nxt
            if c + 1 < nc:
                nxt = products(c + 1)
            rows = pl.ds(c * tc, tc)
            pt = jnp.exp2(jnp.where(valid, st, NEG) - lse_ref[0, :, rows])
            dv = dv + jnp.dot(pt.astype(bf16), do_ref[0, rows, :], preferred_element_type=f32)
            dst = (pt * (dpt - delta_ref[0, :, rows])).astype(bf16)
            dk = dk + jnp.dot(dst, q_ref[0, rows, :], preferred_element_type=f32)
            dqt_ref[g, :, rows] += jnp.dot(ktb, dst, preferred_element_type=f32)

        @pl.when(g == 0)
        def _():
            dk_ref[0] = dk
            dv_ref[0] = dv

        @pl.when(g > 0)
        def _():
            dk_ref[0] += dk
            dv_ref[0] += dv

    hspec = pl.BlockSpec((1, n, HD), lambda h, j, g: (h * grp + g, 0, 0))
    rspec = pl.BlockSpec((1, 1, n), lambda h, j, g: (h * grp + g, 0, 0))
    kspec = pl.BlockSpec((1, tk, HD), lambda h, j, g: (h, j, 0))
    return pl.pallas_call(
        body, grid=(NKV, n // tk, grp),
        in_specs=[hspec, hspec, rspec, rspec, kspec, pl.BlockSpec((1, HD, tk), lambda h, j, g: (h, 0, j)), kspec],
        out_specs=[pl.BlockSpec((grp, HD, n), lambda h, j, g: (h, 0, 0)), kspec, kspec],
        out_shape=[jax.ShapeDtypeStruct((NQ, HD, n), f32)] + [jax.ShapeDtypeStruct((NKV, n, HD), f32)] * 2,
        name="flash_bwd", compiler_params=_params(("parallel", "arbitrary", "arbitrary")))(
            q, do, lse_row, delta_row, k, kt, v)


def _attn_fwd(h_bf, w_qkv_t, qg, kg, tabs):
    e, et, cos, sin = tabs
    qkv = _mm("attn_qkv_mm", h_bf, w_qkv_t, trans_b=True)
    q_bf, k_bf, v_bf = _qk_fwd(qkv, qg, kg, e, et, cos, sin)
    q16, k4, v4 = _to_heads(q_bf, NQ), _to_heads(k_bf, NKV), _to_heads(v_bf, NKV)
    ones = jnp.zeros((NKV, v4.shape[1], HD), bf16).at[:, :, 0].set(1.0)
    o16, lse = _flash_fwd(q16, k4, jnp.concatenate([v4, ones], axis=2))
    o = _from_heads(o16)
    o_bf = o.astype(bf16)
    return o_bf, (qkv, q16, k4, v4, o, lse, o_bf)


def _attn_bwd(dmix_bf, h_bf, saved, w_qkv_t, qg, kg, w_out, tabs):
    e, et, cos, sin = tabs
    qkv, q16, k4, v4, o, lse, o_bf = saved
    n = qkv.shape[0]
    do = _mm("attn_do_mm", dmix_bf, w_out, trans_b=True, out_dtype=bf16)
    d_w_out = _mm_tn("attn_dwout", o_bf, dmix_bf)

    def head_dots(i, do, o, e):
        return (jnp.dot(do.astype(f32) * o, e, preferred_element_type=f32, precision=HIGH) * HD,), ()

    delta = _ew("attn_delta", head_dots, [do, o], [e], [(128, f32)])[0]
    dqt, dk4, dv4 = _flash_bwd(q16, k4, k4.transpose(0, 2, 1), v4, _to_heads(do, NQ), lse.reshape(NQ, 1, n),
                               delta[:, :NQ].T.reshape(NQ, 1, n))
    dq = dqt.transpose(2, 0, 1).reshape(n, QW)
    dqkv_bf, dgq, dgk = _qk_bwd(qkv, dq, _from_heads(dk4), _from_heads(dv4), qg, kg, e, et, cos, sin)
    d_w_qkv_t = _mm_tn("attn_dwqkv", dqkv_bf, h_bf)
    dh = _mm("attn_dh_mm", dqkv_bf, w_qkv_t)
    return dh, d_w_qkv_t, dgq.reshape(NQ, HD).sum(0), dgk.reshape(NKV, HD).sum(0), d_w_out


def _all_gather(name, shard):
    def body(x_ref, out_ref, send_sems, recv_sems, local_sem):
        x, y, c = lax.axis_index("x"), lax.axis_index("y"), lax.axis_index("c")
        me, sibling = (x, y, c), (x, y, 1 - c)
        chips = [(1 - x, y), (x, 1 - y), (1 - x, 1 - y)]

        def slot(px, py, pc):
            return out_ref.at[4 * px + 2 * py + pc]

        def copy(k, block, to, src=None):
            return pltpu.make_async_remote_copy(
                src_ref=slot(*block) if src is None else src, dst_ref=slot(*block),
                send_sem=send_sems.at[k], recv_sem=recv_sems.at[k], device_id=to, device_id_type=MESH)

        mine = pltpu.make_async_copy(x_ref, slot(*me), local_sem)
        mine.start()
        first = [copy(0, me, sibling, src=x_ref)]
        first += [copy(1 + j, me, (*chip, c), src=x_ref) for j, chip in enumerate(chips)]
        for cp in first:
            cp.start()
        passed = [copy(4 + j, (*chip, c), sibling) for j, chip in enumerate(chips)]
        for j, chip in enumerate(chips):
            copy(1 + j, (*chip, c), me).wait_recv()
            passed[j].start()
        copy(0, sibling, me).wait_recv()
        for j, chip in enumerate(chips):
            copy(4 + j, (*chip, 1 - c), me).wait_recv()
        for cp in first + passed:
            cp.wait_send()
        mine.wait()

    return pl.pallas_call(
        body, out_shape=jax.ShapeDtypeStruct((8,) + shard.shape, shard.dtype), in_specs=[ANY], out_specs=ANY,
        scratch_shapes=[pltpu.SemaphoreType.DMA((7,)), pltpu.SemaphoreType.DMA((7,)), pltpu.SemaphoreType.DMA],
        name=name)(shard)


def _swap_sibling(name, theirs):
    k = len(theirs)

    def body(*refs):
        src, dst, send_sems, recv_sems = refs[:k], refs[k:2 * k], refs[2 * k], refs[2 * k + 1]
        x, y, c = lax.axis_index("x"), lax.axis_index("y"), lax.axis_index("c")
        copies = [pltpu.make_async_remote_copy(src_ref=src[j], dst_ref=dst[j], send_sem=send_sems.at[j],
                                               recv_sem=recv_sems.at[j], device_id=(x, y, 1 - c), device_id_type=MESH)
                  for j in range(k)]
        for cp in copies:
            cp.start()
        for cp in copies:
            cp.wait()

    return pl.pallas_call(
        body, out_shape=[jax.ShapeDtypeStruct(a.shape, a.dtype) for a in theirs], in_specs=[ANY] * k,
        out_specs=[ANY] * k, scratch_shapes=[pltpu.SemaphoreType.DMA((k,)), pltpu.SemaphoreType.DMA((k,))],
        name=name)(*theirs)


def _exchange_chips(name, parts):
    k = len(parts)

    def body(*refs):
        p_refs, t_refs = refs[:k], refs[k:2 * k]
        send_sems, recv_sems, local_sems = refs[2 * k:]
        x, y, c = lax.axis_index("x"), lax.axis_index("y"), lax.axis_index("c")
        q = 2 * x + y
        copies = []
        for j in range(k):
            copies.append(pltpu.make_async_copy(p_refs[j].at[q], t_refs[j].at[q], local_sems.at[j]))
            for hop in (1, 2, 3):
                tx, ty = x ^ (hop >> 1), y ^ (hop & 1)
                copies.append(pltpu.make_async_remote_copy(
                    src_ref=p_refs[j].at[2 * tx + ty], dst_ref=t_refs[j].at[q], send_sem=send_sems.at[3 * j + hop - 1],
                    recv_sem=recv_sems.at[3 * j + hop - 1], device_id=(tx, ty, c), device_id_type=MESH))
        for cp in copies:
            cp.start()
        for cp in copies:
            cp.wait()

    return pl.pallas_call(
        body, out_shape=[jax.ShapeDtypeStruct(a.shape, a.dtype) for a in parts], in_specs=[ANY] * k,
        out_specs=[ANY] * k,
        scratch_shapes=[pltpu.SemaphoreType.DMA((3 * k,)), pltpu.SemaphoreType.DMA((3 * k,)),
                        pltpu.SemaphoreType.DMA((k,))],
        name=name)(*parts)


def _reduce_scatter(mine, theirs):
    got = _swap_sibling("rs_sibling", list(theirs))
    parts = []
    for a, b, dt, nm in zip(mine, got, (bf16, f32), ("rs_add2", "rs_add2_small")):
        rows = 4 * a.shape[1]
        parts.append(_ew(nm, lambda i, a, b: ((a + b,), ()), [a.reshape(rows, D), b.reshape(rows, D)], [],
                         [(D, dt)], tile=RS_TILE)[0].reshape(a.shape))
    ts = _exchange_chips("rs_chips", parts)

    def add4(i, a, b, c, d):
        return ((((a.astype(f32) + b.astype(f32)) + c.astype(f32)) + d.astype(f32),), ())

    return [_ew(nm, add4, [(t, 0), (t, 1), (t, 2), (t, 3)], [], [(D, f32)], tile=RS_TILE)[0]
            for t, nm in zip(ts, ("rs_add4", "rs_add4_small"))]


def _pack_rows(parts, rows):
    flat = jnp.concatenate([p.reshape(-1) for p in parts])
    return jnp.pad(flat, (0, rows * D - flat.shape[0])).reshape(rows, D)


def _unpack(flat, shapes):
    out, off = [], 0
    for s in shapes:
        n = math.prod(s)
        out.append(flat[off:off + n].reshape(s))
        off += n
    return out


def _mat_rows(block, transposed, blk):
    a = jnp.swapaxes(block, 1, 2) if transposed else block
    a = jnp.pad(a, ((0, 0), (0, blk - a.shape[1]), (0, 0)))
    return a.reshape(-1, D)


def _mat_block(rows, transposed, blk, real):
    a = rows.reshape(-1, blk, D)[:, :real]
    return jnp.swapaxes(a, 1, 2) if transposed else a


def _mat_full(gathered, blk):
    layers = gathered.shape[1] // blk
    return gathered.reshape(8, layers, blk, D).transpose(1, 0, 2, 3).reshape(layers, 8 * blk, D)


def _vec_full(gathered):
    return gathered.transpose(1, 0, 2).reshape(gathered.shape[1], D)


def _grad_slots(full, small, cc):
    def halves(a, blk):
        a4 = a.reshape(4, 2, blk, D)
        return [lax.dynamic_index_in_dim(a4, sel, axis=1, keepdims=False) for sel in (cc, 1 - cc)]

    mine, theirs = [], []
    for name, _, blk, _ in MATS:
        for layer in full[name]:
            a, b = halves(layer, blk)
            mine.append(a)
            theirs.append(b)
    a, b = halves(small, REP_PIECE)
    return (jnp.concatenate(mine, axis=1), a), (jnp.concatenate(theirs, axis=1), b)


def _local_step(x0, target0, w, fw):
    seq = x0.shape[0]
    n = OFF + seq
    movers = _block_movers()
    h = jnp.concatenate([jnp.zeros((PAD, D), f32), fw['meta_tokens'], x0], axis=0)
    h_bf = h.astype(bf16)
    tabs = _head_mats() + _rope_tables(n)
    qg = [jnp.tile(w['attn_q_gain'][j], NQ)[None, :] for j in range(2)]
    kg = [jnp.tile(w['attn_k_gain'][j], NKV)[None, :] for j in range(2)]
    s5_names = ['s5_lambda_re', 's5_lambda_im', 's5_log_dt', 's5_b_re', 's5_b_im', 's5_c_re', 's5_c_im']
    s5_mats, s5_vjp = [], []
    for j in range(2):
        mats, vjp = jax.vjp(_s5_mats, *[w[k][j] for k in s5_names])
        s5_mats.append(mats)
        s5_vjp.append(vjp)
    saved = []
    for i in range(DEPTH):
        j = i // 2
        if i % 2 == 0:
            mixed, sv = _s5_fwd(h, movers, s5_mats[j], w['s5_d'][j][None, :], (fw['s5_w_glu'], j))
            w_out = (fw['s5_w_out'], j)
        else:
            mixed, sv = _attn_fwd(h_bf, (fw['attn_w_qkv'], j), qg[j], kg[j], tabs)
            w_out = (fw['attn_w_out'], j)
        r1, h1, h1_bf = _mm_ln("mixer_out_ln", mixed, w_out, h, fw['ln_gain'][i, 0][None, :], fw['ln_bias'][i, 0][None, :])
        gate, up, act = _ffn_up(h1_bf, (fw['ffn_w_gate'], i), (fw['ffn_w_up'], i))
        r2, h2, h2_bf = _mm_ln("ffn_down_ln", act, (fw['ffn_w_down'], i), h1, fw['ln_gain'][i, 1][None, :],
                               fw['ln_bias'][i, 1][None, :])
        saved.append((h, h_bf, sv, r1, h1_bf, gate, up, act, r2))
        h, h_bf = h2, h2_bf

    d_b, sq = _loss_grad(h, target0)
    loss = 0.5 * jnp.sum(sq) * (1.0 / D)

    grads = {k: [None] * (DEPTH if k.startswith('ffn') else 2) for k in WEIGHTS}
    d_ln_gain = [[None, None] for _ in range(DEPTH)]
    d_ln_bias = [[None, None] for _ in range(DEPTH)]
    d_a = None
    for i in reversed(range(DEPTH)):
        j = i // 2
        h_in, h_in_bf, sv, r1, h1_bf, gate, up, act, r2 = saved[i]
        dr2, dr2_bf, dg, db = _ln_bwd(d_a, d_b, r2, fw['ln_gain'][i, 1][None, :])
        d_ln_gain[i][1], d_ln_bias[i][1] = dg[0], db[0]
        dgate, dup = _ffn_dup(dr2_bf, (fw['ffn_w_down'], i), gate, up)
        grads['ffn_w_down'][i] = _mm_tn("ffn_dwdown", act, dr2_bf, tk=DFFP // 2)
        grads['ffn_w_gate'][i] = _mm_tn("ffn_dwgate", dgate, h1_bf, tk=DFFP // 2)
        grads['ffn_w_up'][i] = _mm_tn("ffn_dwup", dup, h1_bf, tk=DFFP // 2)
        dh1 = _mm2("ffn_dh_mm", dgate, (fw['ffn_w_gate'], i), dup, (fw['ffn_w_up'], i))
        dr1, dr1_bf, dg, db = _ln_bwd(dr2, dh1, r1, fw['ln_gain'][i, 0][None, :])
        d_ln_gain[i][0], d_ln_bias[i][0] = dg[0], db[0]
        if i % 2 == 0:
            dh, d_par, dd, d_w_glu, d_w_out = _s5_bwd(dr1_bf, h_in, movers, sv, s5_mats[j], s5_vjp[j],
                                                      w['s5_d'][j][None, :], (fw['s5_w_glu'], j), (fw['s5_w_out'], j))
            for k, g in zip(s5_names, d_par):
                grads[k][j] = g
            grads['s5_d'][j], grads['s5_w_glu'][j], grads['s5_w_out'][j] = dd, d_w_glu, d_w_out
        else:
            dh, d_w_qkv, dgq, dgk, d_w_out = _attn_bwd(dr1_bf, h_in_bf, sv, (fw['attn_w_qkv'], j), qg[j], kg[j],
                                                       (fw['attn_w_out'], j), tabs)
            grads['attn_w_qkv'][j], grads['attn_w_out'][j] = d_w_qkv, d_w_out
            grads['attn_q_gain'][j], grads['attn_k_gain'][j] = dgq, dgk
        d_a, d_b = dr1, dh
    dh0 = _ew("dh0", lambda i, a, b: ((ALPHA * a + b,), ()), [d_a, d_b], [], [(D, f32)])[0]
    mats = {m[0] for m in MATS}
    full = {k: (v if k in mats else jnp.stack(v)) for k, v in grads.items() if v[0] is not None}
    full['meta_tokens'] = dh0[PAD:OFF]
    full['ln_gain'] = jnp.stack([jnp.stack(r) for r in d_ln_gain])
    full['ln_bias'] = jnp.stack([jnp.stack(r) for r in d_ln_bias])

    return loss, dh0[OFF:], full


def kernel(x, meta_tokens, s5_lambda_re, s5_lambda_im, s5_log_dt, s5_b_re, s5_b_im, s5_c_re, s5_c_im, s5_d, s5_w_glu, s5_w_out, attn_w_qkv, attn_q_gain, attn_k_gain, attn_w_out, ffn_w_gate, ffn_w_up, ffn_w_down, ln_gain, ln_bias, loss_target, m_meta_tokens, m_s5_lambda_re, m_s5_lambda_im, m_s5_log_dt, m_s5_b_re, m_s5_b_im, m_s5_c_re, m_s5_c_im, m_s5_d, m_s5_w_glu, m_s5_w_out, m_attn_w_qkv, m_attn_q_gain, m_attn_k_gain, m_attn_w_out, m_ffn_w_gate, m_ffn_w_up, m_ffn_w_down, m_ln_gain, m_ln_bias, v_meta_tokens, v_s5_lambda_re, v_s5_lambda_im, v_s5_log_dt, v_s5_b_re, v_s5_b_im, v_s5_c_re, v_s5_c_im, v_s5_d, v_s5_w_glu, v_s5_w_out, v_attn_w_qkv, v_attn_q_gain, v_attn_k_gain, v_attn_w_out, v_ffn_w_gate, v_ffn_w_up, v_ffn_w_down, v_ln_gain, v_ln_bias):
    w = dict(zip(WEIGHTS, (meta_tokens, s5_lambda_re, s5_lambda_im, s5_log_dt, s5_b_re, s5_b_im, s5_c_re, s5_c_im, s5_d, s5_w_glu, s5_w_out, attn_w_qkv, attn_q_gain, attn_k_gain, attn_w_out, ffn_w_gate, ffn_w_up, ffn_w_down, ln_gain, ln_bias)))
    mom = dict(zip(WEIGHTS, (m_meta_tokens, m_s5_lambda_re, m_s5_lambda_im, m_s5_log_dt, m_s5_b_re, m_s5_b_im, m_s5_c_re, m_s5_c_im, m_s5_d, m_s5_w_glu, m_s5_w_out, m_attn_w_qkv, m_attn_q_gain, m_attn_k_gain, m_attn_w_out, m_ffn_w_gate, m_ffn_w_up, m_ffn_w_down, m_ln_gain, m_ln_bias)))
    vel = dict(zip(WEIGHTS, (v_meta_tokens, v_s5_lambda_re, v_s5_lambda_im, v_s5_log_dt, v_s5_b_re, v_s5_b_im, v_s5_c_re, v_s5_c_im, v_s5_d, v_s5_w_glu, v_s5_w_out, v_attn_w_qkv, v_attn_q_gain, v_attn_k_gain, v_attn_w_out, v_ffn_w_gate, v_ffn_w_up, v_ffn_w_down, v_ln_gain, v_ln_bias)))
    cc = lax.axis_index("c")
    dev = 4 * lax.axis_index("x") + 2 * lax.axis_index("y") + cc

    mat_rows = jnp.concatenate([_mat_rows(w[n], t, blk) for n, t, blk, _ in MATS]).astype(bf16)
    g_mats = _all_gather("ag_weights", mat_rows)
    g_vecs = _all_gather("ag_vectors", jnp.concatenate([w[n].reshape(-1, 128) for n in VECS]))
    fw, off = {}, 0
    for n, _, blk, _ in MATS:
        rows = w[n].shape[0] * blk
        fw[n] = _mat_full(g_mats[:, off:off + rows], blk)
        off += rows
    off = 0
    for n in VECS:
        rows = w[n].size // 128
        fw[n] = _vec_full(g_vecs[:, off:off + rows]).reshape(w[n].shape[:-1] + (D,))
        off += rows

    loss, grad_x, full = _local_step(x[0], loss_target[0], w, fw)
    loss = lax.psum(loss, AXES)
    grad_x = grad_x[None]

    small_names = REPL + VECS
    mine, theirs = _grad_slots(full, _pack_rows([full[k] for k in small_names], REP_ROWS), cc)
    red, red_small = _reduce_scatter(mine, theirs)
    small_all = _all_gather("ag_small_grads", red_small).reshape(REP_ROWS * D)
    g, off = {}, 0
    for n, t, blk, real in MATS:
        rows = w[n].shape[0] * blk
        g[n] = _mat_block(red[off:off + rows], t, blk, real)
        off += rows
    small = dict(zip(small_names, _unpack(small_all, [full[k].shape for k in small_names])))
    for k in REPL:
        g[k] = small[k]
    for k in VECS:
        g[k] = lax.dynamic_slice_in_dim(small[k], dev * 128, 128, axis=small[k].ndim - 1)

    delta, new_m, new_v = {}, {}, {}
    for n in WEIGHTS:
        shp = w[n].shape
        res = _adamw(*[d[n].reshape(-1, shp[-1]) for d in (w, g, mom, vel)])
        delta[n], new_m[n], new_v[n] = [a.reshape(shp) for a in res]
    return (loss, grad_x, *[g[k] for k in WEIGHTS], *[delta[k] for k in WEIGHTS],
            *[new_m[k] for k in WEIGHTS], *[new_v[k] for k in WEIGHTS])
```

```python
import functools
import math

import jax
import jax.numpy as jnp
from jax import lax
from jax.experimental import pallas as pl
from jax.experimental.pallas import tpu as pltpu

f32 = jnp.float32
bf16 = jnp.bfloat16
HIGH = lax.Precision.HIGH
MESH = pl.DeviceIdType.MESH
AXES = ("x", "y", "c")
ANY = pl.BlockSpec(memory_space=pl.ANY)

D = 1024
DEPTH = 4
N_META = 16
PAD = 240
OFF = PAD + N_META
ROW_TILE = 768
KEY_CHUNK = 256
FFN_TILE = 256
ADAM_TILE = 544
GRID_W = 64
HD = 64
NQ = 16
NKV = 4
QW = NQ * HD
KW = NKV * HD
QKVW = QW + 2 * KW
DFF = 2816
GROUPS = 64
GCH = 16
NSTATE = 64
CHUNK = 16
GB = 8
ROPE_THETA = 10000.0
LN_EPS = 1e-5
QK_EPS = 1e-6
ALPHA = (2.0 * DEPTH) ** 0.25
ADAM_LR, ADAM_B1, ADAM_B2, ADAM_EPS, ADAM_WD, ADAM_STEP = 0.001, 0.9, 0.999, 1e-08, 0.01, 10
NEG = -1e30
Q_SCALE = HD ** -0.5 * math.log2(math.e)
VMEM_MB = 56

NT = (((1,), (1,)), ((), ()))
TN = (((0,), (0,)), ((), ()))

WEIGHTS = ['meta_tokens', 's5_lambda_re', 's5_lambda_im', 's5_log_dt', 's5_b_re', 's5_b_im', 's5_c_re', 's5_c_im',
           's5_d', 's5_w_glu', 's5_w_out', 'attn_w_qkv', 'attn_q_gain', 'attn_k_gain', 'attn_w_out', 'ffn_w_gate',
           'ffn_w_up', 'ffn_w_down', 'ln_gain', 'ln_bias']
DFFP = 3072
FF_BLK, FF_BLKP = DFF // 8, DFFP // 8
MATS = [('s5_w_glu', False, 128, 128), ('s5_w_out', False, 128, 128), ('attn_w_qkv', True, 192, 192),
        ('attn_w_out', False, 128, 128), ('ffn_w_gate', True, FF_BLKP, FF_BLK), ('ffn_w_up', True, FF_BLKP, FF_BLK),
        ('ffn_w_down', False, FF_BLKP, FF_BLK)]
VECS = ['meta_tokens', 'ln_gain', 'ln_bias']
REPL = ['s5_lambda_re', 's5_lambda_im', 's5_log_dt', 's5_b_re', 's5_b_im', 's5_c_re', 's5_c_im', 's5_d',
        'attn_q_gain', 'attn_k_gain']
MAT_ROWS = 5760
REP_PIECE = 160
REP_ROWS = 8 * REP_PIECE
RS_TILE = 640


def _params(sem, mb=VMEM_MB):
    return pltpu.CompilerParams(dimension_semantics=sem, vmem_limit_bytes=mb << 20)


def _ew(name, fn, rows, consts, outs, accs=(), tile=ROW_TILE):
    first = rows[0][0] if isinstance(rows[0], tuple) else rows[0]
    n = first.shape[-2]
    tile = min(tile, n)
    assert n % tile == 0, (name, n, tile)
    n_in, n_o, n_a = len(rows) + len(consts), len(outs), len(accs)

    def body(*refs):
        i = pl.program_id(0)
        res_o, res_a = fn(i, *[r[...] for r in refs[:n_in]])
        for r, val in zip(refs[n_in:n_in + n_o], res_o):
            r[...] = val.astype(r.dtype)
        if n_a:
            a_refs = refs[n_in + n_o:]

            @pl.when(i == 0)
            def _():
                for r in a_refs:
                    r[...] = jnp.zeros(r.shape, r.dtype)

            for r, val in zip(a_refs, res_a):
                r[...] += val

    in_specs, args = [], []
    for a in rows:
        if isinstance(a, tuple):
            arr, k = a
            in_specs.append(pl.BlockSpec((None, tile, arr.shape[2]), functools.partial(lambda i, k: (k, i, 0), k=k)))
            args.append(arr)
        else:
            in_specs.append(pl.BlockSpec((tile, a.shape[1]), lambda i: (i, 0)))
            args.append(a)
    for c in consts:
        in_specs.append(pl.BlockSpec(c.shape, lambda i: (0, 0)))
        args.append(c)
    out_specs = [pl.BlockSpec((tile, c), lambda i: (i, 0)) for c, _ in outs]
    out_specs += [pl.BlockSpec(s, lambda i: (0, 0)) for s in accs]
    out_shape = [jax.ShapeDtypeStruct((n, c), dt) for c, dt in outs]
    out_shape += [jax.ShapeDtypeStruct(s, f32) for s in accs]
    res = pl.pallas_call(body, grid=(n // tile,), in_specs=in_specs, out_specs=out_specs, out_shape=out_shape,
                         name=name, compiler_params=_params(("arbitrary",)))(*args)
    return res


def _mm(name, a, b, trans_b=False, out_dtype=f32, tm=ROW_TILE):
    m, k = a.shape
    spec, b, shape = _whole(b)
    n = shape[0] if trans_b else shape[1]
    tm = min(tm, m)
    assert m % tm == 0
    dims = NT if trans_b else (((1,), (0,)), ((), ()))

    def body(a_ref, b_ref, o_ref):
        o_ref[...] = lax.dot_general(a_ref[...], b_ref[...], dims, preferred_element_type=f32).astype(o_ref.dtype)

    return pl.pallas_call(
        body, grid=(m // tm,), in_specs=[pl.BlockSpec((tm, k), lambda i: (i, 0)), spec],
        out_specs=pl.BlockSpec((tm, n), lambda i: (i, 0)),
        out_shape=jax.ShapeDtypeStruct((m, n), out_dtype), name=name, compiler_params=_params(("parallel",)))(a, b)


def _whole(b):
    if isinstance(b, tuple):
        arr, layer = b
        return pl.BlockSpec((None,) + arr.shape[1:], lambda i: (layer, 0, 0)), arr, arr.shape[1:]
    return pl.BlockSpec(b.shape, lambda i: (0, 0)), b, b.shape


def _mm2(name, a1, b1, a2, b2, out_dtype=f32, tm=ROW_TILE // 2):
    m, k = a1.shape
    spec1, b1, shape = _whole(b1)
    spec2, b2, _ = _whole(b2)
    n = shape[1]
    tm = min(tm, m)
    assert m % tm == 0

    def body(a1_ref, b1_ref, a2_ref, b2_ref, o_ref):
        acc = jnp.dot(a1_ref[...], b1_ref[...], preferred_element_type=f32)
        acc += jnp.dot(a2_ref[...], b2_ref[...], preferred_element_type=f32)
        o_ref[...] = acc.astype(o_ref.dtype)

    row = pl.BlockSpec((tm, k), lambda i: (i, 0))
    return pl.pallas_call(
        body, grid=(m // tm,), in_specs=[row, spec1, row, spec2], out_specs=pl.BlockSpec((tm, n), lambda i: (i, 0)),
        out_shape=jax.ShapeDtypeStruct((m, n), out_dtype), name=name,
        compiler_params=_params(("parallel",)))(a1, b1, a2, b2)


def _mm_tn(name, a, g, tk=512, tl=ROW_TILE):
    rows, k1 = a.shape
    n = g.shape[1]
    tl = min(tl, rows)
    assert rows % tl == 0 and k1 % tk == 0

    def body(a_ref, g_ref, o_ref):
        @pl.when(pl.program_id(1) == 0)
        def _():
            o_ref[...] = jnp.zeros(o_ref.shape, f32)

        o_ref[...] += lax.dot_general(a_ref[...], g_ref[...], TN, preferred_element_type=f32)

    return pl.pallas_call(
        body, grid=(k1 // tk, rows // tl),
        in_specs=[pl.BlockSpec((tl, tk), lambda k, l: (l, k)), pl.BlockSpec((tl, n), lambda k, l: (l, 0))],
        out_specs=pl.BlockSpec((tk, n), lambda k, l: (k, 0)),
        out_shape=jax.ShapeDtypeStruct((k1, n), f32), name=name,
        compiler_params=_params(("parallel", "arbitrary")))(a, g)


def _ln_stats(r):
    mean = jnp.mean(r, axis=-1, keepdims=True)
    c = r - mean
    rstd = lax.rsqrt(jnp.mean(c * c, axis=-1, keepdims=True) + LN_EPS)
    return c * rstd, rstd


def _mm_ln(name, a, b, h, gain, bias, tm=ROW_TILE):
    m, k = a.shape
    spec, b, _ = _whole(b)

    def body(a_ref, b_ref, h_ref, g_ref, bias_ref, r_ref, y_ref, yb_ref):
        r = ALPHA * h_ref[...] + jnp.dot(a_ref[...], b_ref[...], preferred_element_type=f32)
        y = _ln_stats(r)[0] * g_ref[...] + bias_ref[...]
        r_ref[...] = r
        y_ref[...] = y
        yb_ref[...] = y.astype(bf16)

    row = pl.BlockSpec((tm, D), lambda i: (i, 0))
    vec = pl.BlockSpec((1, D), lambda i: (0, 0))
    return pl.pallas_call(
        body, grid=(m // tm,), in_specs=[pl.BlockSpec((tm, k), lambda i: (i, 0)), spec, row, vec, vec],
        out_specs=[row, row, row],
        out_shape=[jax.ShapeDtypeStruct((m, D), f32), jax.ShapeDtypeStruct((m, D), f32), jax.ShapeDtypeStruct((m, D), bf16)],
        name=name, compiler_params=_params(("parallel",)))(a, b, h, gain, bias)


def _ln_bwd(d_a, d_b, r, gain):
    def core(dout, r, g):
        xhat, rstd = _ln_stats(r)
        dxh = dout * g
        dr = rstd * (dxh - jnp.mean(dxh, axis=-1, keepdims=True) - xhat * jnp.mean(dxh * xhat, axis=-1, keepdims=True))
        return (dr, dr), (jnp.sum(dout * xhat, axis=0, keepdims=True), jnp.sum(dout, axis=0, keepdims=True))

    outs, accs = [(D, f32), (D, bf16)], [(1, D), (1, D)]
    if d_a is None:
        return _ew("ln_bwd_top", lambda i, d, r, g: core(d, r, g), [d_b, r], [gain], outs, accs)
    return _ew("ln_bwd", lambda i, da, db, r, g: core(ALPHA * da + db, r, g), [d_a, d_b, r], [gain], outs, accs)


def _sigmoid(x):
    return 1.0 / (1.0 + jnp.exp(-x))


def _ffn_up(h_bf, w_gate_t, w_up_t, tm=FFN_TILE):
    m, k = h_bf.shape
    gspec, w_gate_t, (n, _) = _whole(w_gate_t)
    uspec, w_up_t, _ = _whole(w_up_t)

    def body(h_ref, wg_ref, wu_ref, g_ref, u_ref, a_ref):
        h = h_ref[...]
        g = lax.dot_general(h, wg_ref[...], NT, preferred_element_type=f32).astype(bf16)
        u = lax.dot_general(h, wu_ref[...], NT, preferred_element_type=f32).astype(bf16)
        g_ref[...] = g
        u_ref[...] = u
        g = g.astype(f32)
        a_ref[...] = (g * _sigmoid(g) * u.astype(f32)).astype(bf16)

    row = pl.BlockSpec((tm, n), lambda i: (i, 0))
    return pl.pallas_call(
        body, grid=(m // tm,), in_specs=[pl.BlockSpec((tm, k), lambda i: (i, 0)), gspec, uspec],
        out_specs=[row, row, row], out_shape=[jax.ShapeDtypeStruct((m, n), bf16)] * 3, name="ffn_up",
        compiler_params=_params(("parallel",)))(h_bf, w_gate_t, w_up_t)


def _ffn_dup(df_bf, w_down, gate, up, tm=FFN_TILE):
    m, k = df_bf.shape
    wspec, w_down, (n, _) = _whole(w_down)

    def body(d_ref, w_ref, g_ref, u_ref, dg_ref, du_ref):
        da = lax.dot_general(d_ref[...], w_ref[...], NT, preferred_element_type=f32).astype(bf16).astype(f32)
        g, u = g_ref[...].astype(f32), u_ref[...].astype(f32)
        s = _sigmoid(g)
        dg_ref[...] = (da * u * s * (1.0 + g * (1.0 - s))).astype(bf16)
        du_ref[...] = (da * g * s).astype(bf16)

    row = pl.BlockSpec((tm, n), lambda i: (i, 0))
    return pl.pallas_call(
        body, grid=(m // tm,),
        in_specs=[pl.BlockSpec((tm, k), lambda i: (i, 0)), wspec, row, row],
        out_specs=[row, row], out_shape=[jax.ShapeDtypeStruct((m, n), bf16)] * 2, name="ffn_dup",
        compiler_params=_params(("parallel",)))(df_bf, w_down, gate, up)


def _loss_grad(h, target):
    n = h.shape[0]

    def body(h_ref, t_ref, d_ref, sq_ref):
        i = pl.program_id(0)

        @pl.when(i == 0)
        def _():
            d_ref[...] = jnp.zeros(d_ref.shape, f32)
            sq_ref[...] = jnp.zeros(sq_ref.shape, f32)

        @pl.when(i > 0)
        def _():
            e = h_ref[...] - t_ref[...]
            d_ref[...] = e * (1.0 / D)
            sq_ref[...] += jnp.sum(e * e, axis=0, keepdims=True)

    return pl.pallas_call(
        body, grid=(n // OFF,),
        in_specs=[pl.BlockSpec((OFF, D), lambda i: (i, 0)), pl.BlockSpec((OFF, D), lambda i: (jnp.maximum(i - 1, 0), 0))],
        out_specs=[pl.BlockSpec((OFF, D), lambda i: (i, 0)), pl.BlockSpec((1, D), lambda i: (0, 0))],
        out_shape=[jax.ShapeDtypeStruct((n, D), f32), jax.ShapeDtypeStruct((1, D), f32)], name="loss",
        compiler_params=_params(("arbitrary",)))(h, target)


def _adamw(w, g, m, v):
    def fn(i, w, g, m, v):
        m = ADAM_B1 * m + (1.0 - ADAM_B1) * g
        v = ADAM_B2 * v + (1.0 - ADAM_B2) * jnp.square(g)
        m_hat = m / (1.0 - ADAM_B1 ** ADAM_STEP)
        v_hat = v / (1.0 - ADAM_B2 ** ADAM_STEP)
        delta = -ADAM_LR * (m_hat / (jnp.sqrt(v_hat) + ADAM_EPS) + ADAM_WD * w)
        return (delta, m, v), ()

    rows, cols = w.shape
    cap = ADAM_TILE if cols > 128 else 4 * ADAM_TILE
    fits = [t for t in range(8, min(rows, cap) + 1, 8) if rows % t == 0]
    return _ew("adamw", fn, [w, g, m, v], [], [(cols, f32)] * 3, tile=max(fits) if fits else rows)


def _s5_mats(lam_re, lam_im, log_dt, b_re, b_im, c_re, c_im):
    steps = jnp.arange(CHUNK + 1, dtype=f32)
    n = CHUNK * GCH
    last = n - GCH

    def one(lr, li, ldt, br, bi, cr, ci, reverse):
        dt = jnp.exp(ldt)[:, None]
        mag = jnp.exp(lr * dt)
        abr, abi = mag * jnp.cos(li * dt), mag * jnp.sin(li * dt)
        nr, ni = abr - 1.0, abi
        den = lr * lr + li * li
        zr, zi = (nr * lr + ni * li) / den, (ni * lr - nr * li) / den
        bbr = zr[..., None] * br - zi[..., None] * bi
        bbi = zr[..., None] * bi + zi[..., None] * br
        pmag = jnp.exp((lr * dt)[..., None] * steps)
        pang = (li * dt)[..., None] * steps
        pr, pi = pmag * jnp.cos(pang), pmag * jnp.sin(pang)
        crt, cit = jnp.swapaxes(cr, 1, 2)[:, :, None, :], jnp.swapaxes(ci, 1, 2)[:, :, None, :]
        car = crt * pr[..., None] - cit * pi[..., None]
        cai = crt * pi[..., None] + cit * pr[..., None]
        if reverse:
            taps = slice(CHUNK - 1, None, -1)
            outs = slice(CHUNK, 0, -1)
            ins = slice(0, CHUNK)
        else:
            taps, outs, ins = slice(0, CHUNK), slice(1, CHUNK + 1), slice(CHUNK - 1, None, -1)
        kern = (jnp.einsum('gpi,gpq->giq', bbr, car[:, :, taps].reshape(GROUPS, NSTATE, n), precision=HIGH)
                - jnp.einsum('gpi,gpq->giq', bbi, cai[:, :, taps].reshape(GROUPS, NSTATE, n), precision=HIGH))
        wide = jnp.pad(kern, ((0, 0), (0, 0), (0, last) if reverse else (last, 0)))
        m = jnp.stack([wide[:, :, last - GCH * t:last - GCH * t + n] for t in range(CHUNK)], axis=1)
        qr = jnp.swapaxes(pr[:, :, ins], 1, 2)[:, :, None, :]
        qi = jnp.swapaxes(pi[:, :, ins], 1, 2)[:, :, None, :]
        bbrt, bbit = jnp.swapaxes(bbr, 1, 2)[:, None], jnp.swapaxes(bbi, 1, 2)[:, None]
        pin = jnp.concatenate([qr * bbrt - qi * bbit, qr * bbit + qi * bbrt], axis=-1)
        pout = jnp.concatenate([car[:, :, outs].reshape(GROUPS, NSTATE, n),
                                -cai[:, :, outs].reshape(GROUPS, NSTATE, n)], axis=1)
        return (m.reshape(GROUPS, n, n), pin.reshape(GROUPS, n, 2 * NSTATE), pout, pr[:, :, CHUNK], pi[:, :, CHUNK])

    mf, pinf, poutf, arf, aif = one(lam_re[0], lam_im[0], log_dt[0], b_re[0], b_im[0], c_re[0], c_im[0], False)
    mr, pinr, poutr, arr, air = one(lam_re[1], lam_im[1], log_dt[1], b_re[1], b_im[1], c_re[1], c_im[1], True)
    return (mf + mr, jnp.concatenate([pinf, pinr], 2), jnp.concatenate([poutf, poutr], 1),
            jnp.stack([arf, arr]), jnp.stack([aif, air]))


def _s5_coefs(a_re, a_im):
    c1 = jnp.concatenate([a_re, a_re], -1)
    c2 = jnp.concatenate([-a_im, a_im], -1)
    return tuple(c.reshape(GROUPS // GB, 1, GB * 2 * NSTATE) for c in (c1[0], c2[0], c1[1], c2[1]))


def _swap(s):
    w = s.shape[1]
    lane = lax.broadcasted_iota(jnp.int32, s.shape, 1)
    return jnp.where(lane % (2 * NSTATE) < NSTATE, pltpu.roll(s, w - NSTATE, 1), pltpu.roll(s, NSTATE, 1))


def _group_lanes(g):
    return slice(g * 2 * NSTATE, (g + 1) * 2 * NSTATE)


def _s5_states(nc, u_ref, pin_ref, coef, vf, vr, wf, wr, sf, sr):
    c1f, c2f, c1r, c2r = coef
    for g in range(GB):
        v = jnp.dot(u_ref[g], pin_ref[g], preferred_element_type=f32)
        vf[:, _group_lanes(g)] = v[:, :2 * NSTATE]
        vr[:, _group_lanes(g)] = v[:, 2 * NSTATE:]
    wf[...] = _swap(vf[...])
    wr[...] = _swap(vr[...])

    def step(i, carry):
        s_f, t_f, s_r, t_r = carry
        kf, kr = pl.ds(i, 1), pl.ds(nc - 1 - i, 1)
        sf[kf, :] = s_f
        sr[kr, :] = s_r
        s_f, t_f = c1f * s_f + c2f * t_f + vf[kf, :], c1f * t_f - c2f * s_f + wf[kf, :]
        s_r, t_r = c1r * s_r + c2r * t_r + vr[kr, :], c1r * t_r - c2r * s_r + wr[kr, :]
        return s_f, t_f, s_r, t_r

    z = jnp.zeros((1, GB * 2 * NSTATE), f32)
    lax.fori_loop(0, nc, step, (z, z, z, z))


def _s5_core_fwd(ug, msum, pin, pout, coefs):
    nc = ug.shape[1]
    n = CHUNK * GCH

    def body(u_ref, m_ref, pin_ref, pout_ref, c1f, c2f, c1r, c2r, y_ref, vf, vr, wf, wr, sf, sr):
        coef = (c1f[...], c2f[...], c1r[...], c2r[...])
        _s5_states(nc, u_ref, pin_ref, coef, vf, vr, wf, wr, sf, sr)
        for g in range(GB):
            s_in = jnp.concatenate([sf[:, _group_lanes(g)], sr[:, _group_lanes(g)]], axis=1).astype(bf16)
            y_ref[g] = (jnp.dot(u_ref[g], m_ref[g], preferred_element_type=f32)
                        + jnp.dot(s_in, pout_ref[g], preferred_element_type=f32)).astype(bf16)

    seq = pl.BlockSpec((GB, nc, n), lambda i: (i, 0, 0))
    mat = pl.BlockSpec((GB, n, n), lambda i: (i, 0, 0))
    cf = pl.BlockSpec((None, 1, GB * 2 * NSTATE), lambda i: (i, 0, 0))
    scr = pltpu.VMEM((nc, GB * 2 * NSTATE), f32)
    return pl.pallas_call(
        body, grid=(GROUPS // GB,), in_specs=[seq, mat, mat, mat, cf, cf, cf, cf], out_specs=seq,
        out_shape=jax.ShapeDtypeStruct((GROUPS, nc, n), bf16), scratch_shapes=[scr] * 6,
        name="s5_core_fwd", compiler_params=_params(("parallel",)))(ug, msum, pin, pout, *coefs)


def _s5_core_bwd(ug, dyg, msum, pin, pout, coefs):
    nc = ug.shape[1]
    n = CHUNK * GCH

    def body(u_ref, dy_ref, m_ref, pin_ref, pout_ref, c1f, c2f, c1r, c2r,
             du_ref, dm_ref, dpin_ref, dpout_ref, a1f_ref, a2f_ref, a1r_ref, a2r_ref, vf, vr, wf, wr, sf, sr):
        coef = (c1f[...], c2f[...], c1r[...], c2r[...])
        _s5_states(nc, u_ref, pin_ref, coef, vf, vr, wf, wr, sf, sr)
        for g in range(GB):
            s_in = jnp.concatenate([sf[:, _group_lanes(g)], sr[:, _group_lanes(g)]], axis=1).astype(bf16)
            dy = dy_ref[g]
            ds = lax.dot_general(dy, pout_ref[g], NT, preferred_element_type=f32)
            vf[:, _group_lanes(g)] = ds[:, :2 * NSTATE]
            vr[:, _group_lanes(g)] = ds[:, 2 * NSTATE:]
            dpout_ref[g] = lax.dot_general(s_in, dy, TN, preferred_element_type=f32)
            dm_ref[g] = lax.dot_general(u_ref[g], dy, TN, preferred_element_type=f32)

        wf[...] = _swap(vf[...])
        wr[...] = _swap(vr[...])
        k1f, k2f, k1r, k2r = coef[0], -coef[1], coef[2], -coef[3]

        def step(i, carry):
            g_f, h_f, g_r, h_r, a1f, b2f, a1r, b2r = carry
            kf, kr = pl.ds(nc - 1 - i, 1), pl.ds(i, 1)
            s_f, s_r = sf[kf, :], sr[kr, :]
            sf[kf, :] = g_f
            sr[kr, :] = g_r
            a1f, b2f = a1f + g_f * s_f, b2f + h_f * s_f
            a1r, b2r = a1r + g_r * s_r, b2r + h_r * s_r
            g_f, h_f = vf[kf, :] + k1f * g_f + k2f * h_f, wf[kf, :] + k1f * h_f - k2f * g_f
            g_r, h_r = vr[kr, :] + k1r * g_r + k2r * h_r, wr[kr, :] + k1r * h_r - k2r * g_r
            return g_f, h_f, g_r, h_r, a1f, b2f, a1r, b2r

        z = jnp.zeros((1, GB * 2 * NSTATE), f32)
        _, _, _, _, a1f, b2f, a1r, b2r = lax.fori_loop(0, nc, step, (z,) * 8)
        a1f_ref[...], a2f_ref[...], a1r_ref[...], a2r_ref[...] = a1f, _swap(b2f), a1r, _swap(b2r)
        for g in range(GB):
            dv = jnp.concatenate([sf[:, _group_lanes(g)], sr[:, _group_lanes(g)]], axis=1).astype(bf16)
            du_ref[g] = (lax.dot_general(dy_ref[g], m_ref[g], NT, preferred_element_type=f32)
                         + lax.dot_general(dv, pin_ref[g], NT, preferred_element_type=f32)).astype(bf16)
            dpin_ref[g] = lax.dot_general(u_ref[g], dv, TN, preferred_element_type=f32)

    seq = pl.BlockSpec((GB, nc, n), lambda i: (i, 0, 0))
    mat = pl.BlockSpec((GB, n, n), lambda i: (i, 0, 0))
    cf = pl.BlockSpec((None, 1, GB * 2 * NSTATE), lambda i: (i, 0, 0))
    scr = pltpu.VMEM((nc, GB * 2 * NSTATE), f32)
    mat_s = jax.ShapeDtypeStruct((GROUPS, n, n), f32)
    cf_s = jax.ShapeDtypeStruct((GROUPS // GB, 1, GB * 2 * NSTATE), f32)
    return pl.pallas_call(
        body, grid=(GROUPS // GB,), in_specs=[seq, seq, mat, mat, mat, cf, cf, cf, cf],
        out_specs=[seq, mat, mat, mat, cf, cf, cf, cf],
        out_shape=[jax.ShapeDtypeStruct((GROUPS, nc, n), bf16), mat_s, mat_s, mat_s, cf_s, cf_s, cf_s, cf_s],
        scratch_shapes=[scr] * 6, name="s5_core_bwd",
        compiler_params=_params(("parallel",)))(ug, dyg, msum, pin, pout, *coefs)


def _block_movers():
    a_in, l_in = jnp.divmod(jnp.arange(GB * 128, dtype=jnp.int32), 128)
    a_out, c_out = jnp.divmod(jnp.arange(128, dtype=jnp.int32), GCH)
    j = jnp.arange(GB, dtype=jnp.int32)[:, None, None]
    hit = (a_in[None, :, None] == a_out[None, None, :]) & (l_in[None, :, None] == GCH * j + c_out[None, None, :])
    return hit.astype(bf16)


def _to_groups(x, movers, mask):
    n = x.shape[0]
    nc = n // CHUNK
    half = CHUNK // 2

    def body(x_ref, mv_ref, o_ref):
        keep = lax.broadcasted_iota(jnp.int32, (nc, 1), 0) >= PAD // CHUNK
        steps = [x_ref[pl.ds(t, nc, stride=CHUNK), :] for t in range(CHUNK)]
        if mask:
            steps = [jnp.where(keep, s, 0.0) for s in steps]
        lo = jnp.concatenate(steps[:half], axis=1).astype(bf16)
        hi = jnp.concatenate(steps[half:], axis=1).astype(bf16)
        for g in range(GB):
            o_ref[g] = jnp.concatenate([jnp.dot(lo, mv_ref[g], preferred_element_type=f32),
                                        jnp.dot(hi, mv_ref[g], preferred_element_type=f32)], axis=1).astype(bf16)

    return pl.pallas_call(
        body, grid=(GROUPS // GB,),
        in_specs=[pl.BlockSpec((n, 128), lambda i: (0, i)), pl.BlockSpec(movers.shape, lambda i: (0, 0, 0))],
        out_specs=pl.BlockSpec((GB, nc, CHUNK * GCH), lambda i: (i, 0, 0)),
        out_shape=jax.ShapeDtypeStruct((GROUPS, nc, CHUNK * GCH), bf16), name="s5_to_groups",
        compiler_params=_params(("parallel",)))(x, movers)


def _from_groups(y, movers, base=None):
    nc = y.shape[1]
    n = nc * CHUNK
    half = CHUNK // 2

    def body(*refs):
        y_ref, mv_ref = refs[:2]
        o_ref = refs[-1]
        keep = lax.broadcasted_iota(jnp.int32, (nc, 1), 0) >= PAD // CHUNK
        lo = jnp.concatenate([y_ref[g][:, :128] for g in range(GB)], axis=1)
        hi = jnp.concatenate([y_ref[g][:, 128:] for g in range(GB)], axis=1)
        for t in range(CHUNK):
            rows = pl.ds(t, nc, stride=CHUNK)
            v = jnp.dot(lo if t < half else hi, mv_ref[t % half], preferred_element_type=f32)
            if base is not None:
                v = refs[2][rows, :] + jnp.where(keep, v, 0.0)
            o_ref[rows, :] = v

    tok = pl.BlockSpec((n, 128), lambda i: (0, i))
    args = (y, movers) if base is None else (y, movers, base)
    return pl.pallas_call(
        body, grid=(GROUPS // GB,),
        in_specs=[pl.BlockSpec((GB, nc, CHUNK * GCH), lambda i: (i, 0, 0)),
                  pl.BlockSpec(movers.shape, lambda i: (0, 0, 0))] + ([] if base is None else [tok]),
        out_specs=tok, out_shape=jax.ShapeDtypeStruct((n, D), f32), name="s5_from_groups",
        compiler_params=_params(("parallel",)))(*args)


def _gelu(y):
    return 0.5 * y * (1.0 + lax.erf(y * (2.0 ** -0.5)))


def _gelu_grad(y):
    return 0.5 * (1.0 + lax.erf(y * (2.0 ** -0.5))) + y * jnp.exp(-0.5 * y * y) * (1.0 / math.sqrt(2.0 * math.pi))


def _s5_fwd(h, movers, mats, d_skip, w_glu):
    msum, pin, pout, a_re, a_im = mats
    coefs = _s5_coefs(a_re, a_im)
    ug = _to_groups(h, movers, mask=True)
    ys = _from_groups(_s5_core_fwd(ug, msum.astype(bf16), pin.astype(bf16), pout.astype(bf16), coefs), movers)

    def post(i, ys, h, d):
        y = ys + d * h
        return (y, _gelu(y)), ()

    y, g_bf = _ew("s5_gelu", post, [ys, h], [d_skip], [(D, f32), (D, bf16)])
    gw = _mm("s5_glu_mm", g_bf, w_glu)

    def glu(i, y, gw):
        return (_gelu(y) * _sigmoid(gw),), ()

    z_bf = _ew("s5_glu", glu, [y, gw], [], [(D, bf16)])[0]
    return z_bf, (ug, y, g_bf, gw, z_bf)


def _s5_bwd(dmix_bf, h, movers, saved, mats, vjp_mats, d_skip, w_glu, w_out):
    ug, y, g_bf, gw, z_bf = saved
    msum, pin, pout, a_re, a_im = mats
    coefs = _s5_coefs(a_re, a_im)
    dz = _mm("s5_dz_mm", dmix_bf, w_out, trans_b=True)
    d_w_out = _mm_tn("s5_dwout", z_bf, dmix_bf)

    def dglu(i, dz, y, gw):
        g, s = _gelu(y), _sigmoid(gw)
        return (dz * g * s * (1.0 - s), dz * s), ()

    dgw_bf, dg1 = _ew("s5_dglu", dglu, [dz, y, gw], [], [(D, bf16), (D, f32)])
    d_w_glu = _mm_tn("s5_dwglu", g_bf, dgw_bf)
    dg2 = _mm("s5_dg_mm", dgw_bf, w_glu, trans_b=True)

    def dgelu(i, dg1, dg2, y, h, d):
        dy = (dg1 + dg2) * _gelu_grad(y)
        return (dy, dy * d), (jnp.sum(dy * h, axis=0, keepdims=True),)

    dy, dh_skip, dd = _ew("s5_dgelu", dgelu, [dg1, dg2, y, h], [d_skip], [(D, f32), (D, f32)], [(1, D)])
    dug, dm, dpin, dpout, a1f, a2f, a1r, a2r = _s5_core_bwd(
        ug, _to_groups(dy, movers, mask=False), msum.astype(bf16), pin.astype(bf16), pout.astype(bf16), coefs)
    dh = _from_groups(dug, movers, base=dh_skip)
    a1 = jnp.stack([a1f, a1r]).reshape(2, GROUPS, 2 * NSTATE)
    a2 = jnp.stack([a2f, a2r]).reshape(2, GROUPS, 2 * NSTATE)
    da_re = a1[..., :NSTATE] + a1[..., NSTATE:]
    da_im = a2[..., NSTATE:] - a2[..., :NSTATE]
    d_params = vjp_mats((dm, dpin, dpout, da_re, da_im))
    return dh, d_params, dd[0], d_w_glu, d_w_out


def _rope_tables(n):
    row = jnp.arange(n, dtype=jnp.int32) - OFF
    real = row >= 0
    rid = jnp.where(real, row // GRID_W, 0).astype(f32)
    cid = jnp.where(real, row % GRID_W, 0).astype(f32)
    half = HD // 2
    inv = ROPE_THETA ** (-jnp.arange(0, half, 2, dtype=f32) / half)
    ar, ac = rid[:, None] * inv[None, :], cid[:, None] * inv[None, :]
    cos = jnp.concatenate([jnp.cos(ar), jnp.cos(ar), jnp.cos(ac), jnp.cos(ac)], axis=1)
    sin = jnp.concatenate([-jnp.sin(ar), jnp.sin(ar), -jnp.sin(ac), jnp.sin(ac)], axis=1)
    return jnp.tile(cos, (1, 2)), jnp.tile(sin, (1, 2))


def _head_mats():
    head = jnp.arange(QW, dtype=jnp.int32)[:, None] // HD == jnp.arange(128, dtype=jnp.int32)[None, :]
    return head.astype(f32) * (1.0 / HD), head.astype(f32).T


def _rot(v):
    w = v.shape[1]
    lane = lax.broadcasted_iota(jnp.int32, v.shape, 1)
    return jnp.where(lane % 32 < 16, pltpu.roll(v, w - 16, 1), pltpu.roll(v, 16, 1))


def _head_mean(v, e, et):
    w = v.shape[1]
    m = jnp.dot(v, e[:w], preferred_element_type=f32, precision=HIGH)
    return m, et[:, :w]


def _rms_rope(t, gain, e, et, cos, sin):
    w = t.shape[1]
    ms, spread = _head_mean(t * t, e, et)
    rs = jnp.dot(lax.rsqrt(ms + QK_EPS), spread, preferred_element_type=f32, precision=HIGH)
    n0 = t * rs
    n = n0 * gain
    reps = w // 128
    return n * jnp.tile(cos, (1, reps)) + _rot(n) * jnp.tile(sin, (1, reps))


def _rms_rope_bwd(dout, t, gain, e, et, cos, sin):
    w = t.shape[1]
    reps = w // 128
    ms, spread = _head_mean(t * t, e, et)
    rs = jnp.dot(lax.rsqrt(ms + QK_EPS), spread, preferred_element_type=f32, precision=HIGH)
    n0 = t * rs
    dn = dout * jnp.tile(cos, (1, reps)) + _rot(dout * jnp.tile(sin, (1, reps)))
    dn0 = dn * gain
    mm, _ = _head_mean(dn0 * n0, e, et)
    corr = jnp.dot(mm, spread, preferred_element_type=f32, precision=HIGH)
    return rs * (dn0 - n0 * corr), jnp.sum(dn * n0, axis=0, keepdims=True)


def _qk_fwd(qkv, qg, kg, e, et, cos, sin):
    def fn(i, qkv, cos, sin, qg, kg, e, et):
        q = _rms_rope(qkv[:, :QW], qg, e, et, cos, sin) * Q_SCALE
        k = _rms_rope(qkv[:, QW:QW + KW], kg, e, et, cos, sin)
        return (q, k, qkv[:, QW + KW:]), ()

    return _ew("qk_rope", fn, [qkv, cos, sin], [qg, kg, e, et], [(QW, bf16), (KW, bf16), (KW, bf16)])


def _qk_bwd(qkv, dq, dk, dv, qg, kg, e, et, cos, sin):
    def fn(i, qkv, cos, sin, dq, dk, dv, qg, kg, e, et):
        dtq, dgq = _rms_rope_bwd(dq * (HD ** -0.5), qkv[:, :QW], qg, e, et, cos, sin)
        dtk, dgk = _rms_rope_bwd(dk * math.log(2.0), qkv[:, QW:QW + KW], kg, e, et, cos, sin)
        return (jnp.concatenate([dtq, dtk, dv], axis=1),), (dgq, dgk)

    return _ew("qk_rope_bwd", fn, [qkv, cos, sin, dq, dk, dv], [qg, kg, e, et], [(QKVW, bf16)], [(1, QW), (1, KW)])


def _to_heads(a, nh):
    return a.reshape(a.shape[0], nh, HD).transpose(1, 0, 2)


def _from_heads(a):
    return a.transpose(1, 0, 2).reshape(a.shape[1], a.shape[0] * HD)


def _masked_first(s, c):
    if c:
        return s
    col = lax.broadcasted_iota(jnp.int32, (1, s.shape[1]), 1)
    return jnp.where(col >= PAD, s, NEG)


def _flash_fwd(q, k, v1, tq=ROW_TILE, tc=KEY_CHUNK):
    n = q.shape[1]
    nc = n // tc
    pair = 2

    def body(q_ref, k_ref, v_ref, o_ref, ob_ref, lse_ref):
        def scores(h, c):
            ks = k_ref[0, pl.ds(c * tc, tc), :]
            return _masked_first(lax.dot_general(q_ref[h], ks, NT, preferred_element_type=f32), c)

        m = [jnp.full((tq, 1), NEG, f32) for _ in range(pair)]
        acc = [jnp.zeros((tq, 2 * HD), f32) for _ in range(pair)]
        nxt = [scores(h, 0) for h in range(pair)]
        for c in range(nc):
            for h in range(pair):
                s = nxt[h]
                if c + 1 < nc:
                    nxt[h] = scores(h, c + 1)
                m_new = jnp.maximum(m[h], jnp.max(s, axis=1, keepdims=True))
                p = jnp.exp2(s - m_new)
                acc[h] = jnp.exp2(m[h] - m_new) * acc[h] + jnp.dot(p.astype(bf16), v_ref[0, pl.ds(c * tc, tc), :],
                                                                     preferred_element_type=f32)
                m[h] = m_new
        ls = [a[:, HD:HD + 1] for a in acc]
        o = jnp.concatenate([a[:, :HD] / l for a, l in zip(acc, ls)], axis=1)
        o_ref[...] = o
        ob_ref[...] = o.astype(bf16)
        lse_ref[0] = jnp.concatenate([mh + jnp.log2(l) for mh, l in zip(m, ls)], axis=1)

    kv_of = NQ // NKV // pair
    tok = pl.BlockSpec((tq, pair * HD), lambda hp, i: (i, hp))
    return pl.pallas_call(
        body, grid=(NQ // pair, n // tq),
        in_specs=[pl.BlockSpec((pair, tq, HD), lambda hp, i: (hp, i, 0)),
                  pl.BlockSpec((1, n, HD), lambda hp, i: (hp // kv_of, 0, 0)),
                  pl.BlockSpec((1, n, 2 * HD), lambda hp, i: (hp // kv_of, 0, 0))],
        out_specs=[tok, tok, pl.BlockSpec((1, tq, pair), lambda hp, i: (hp, i, 0))],
        out_shape=[jax.ShapeDtypeStruct((n, NQ * HD), f32), jax.ShapeDtypeStruct((n, NQ * HD), bf16),
                   jax.ShapeDtypeStruct((NQ // pair, n, pair), f32)],
        name="flash_fwd", compiler_params=_params(("parallel", "parallel")))(q, k, v1)


def _flash_bwd(q, k, kt, v, do, lse_row, delta_row, tk=ROW_TILE, tc=KEY_CHUNK):
    n = q.shape[1]
    nc = n // tc
    grp = NQ // NKV

    def body(q_ref, do_ref, lse_ref, delta_ref, k_ref, kt_ref, v_ref, dqt_ref, dk_ref, dv_ref):
        j, g = pl.program_id(1), pl.program_id(2)
        kb, vb, ktb = k_ref[0], v_ref[0], kt_ref[0]
        valid = lax.broadcasted_iota(jnp.int32, (tk, 1), 0) + j * tk >= PAD

        @pl.when(j == 0)
        def _():
            dqt_ref[g] = jnp.zeros((HD, n), f32)

        def products(c):
            rows = pl.ds(c * tc, tc)
            return (lax.dot_general(kb, q_ref[0, rows, :], NT, preferred_element_type=f32),
                    lax.dot_general(vb, do_ref[0, rows, :], NT, preferred_element_type=f32))

        dk = jnp.zeros((tk, HD), f32)
        dv = jnp.zeros((tk, HD), f32)
        nxt = products(0)
        for c in range(nc):
            st, dpt = nxt
            if c + 1 < nc:
                nxt = products(c + 1)
            rows = pl.ds(c * tc, tc)
            pt = jnp.exp2(jnp.where(valid, st, NEG) - lse_ref[0, :, rows])
            dv = dv + jnp.dot(pt.astype(bf16), do_ref[0, rows, :], preferred_element_type=f32)
            dst = (pt * (dpt - delta_ref[0, :, rows])).astype(bf16)
            dk = dk + jnp.dot(dst, q_ref[0, rows, :], preferred_element_type=f32)
            dqt_ref[g, :, rows] += jnp.dot(ktb, dst, preferred_element_type=f32)

        @pl.when(g == 0)
        def _():
            dk_ref[0] = dk
            dv_ref[0] = dv

        @pl.when(g > 0)
        def _():
            dk_ref[0] += dk
            dv_ref[0] += dv

    hspec = pl.BlockSpec((1, n, HD), lambda h, j, g: (h * grp + g, 0, 0))
    rspec = pl.BlockSpec((1, 1, n), lambda h, j, g: (h * grp + g, 0, 0))
    kspec = pl.BlockSpec((1, tk, HD), lambda h, j, g: (h, j, 0))
    return pl.pallas_call(
        body, grid=(NKV, n // tk, grp),
        in_specs=[hspec, hspec, rspec, rspec, kspec, pl.BlockSpec((1, HD, tk), lambda h, j, g: (h, 0, j)), kspec],
        out_specs=[pl.BlockSpec((grp, HD, n), lambda h, j, g: (h, 0, 0)), kspec, kspec],
        out_shape=[jax.ShapeDtypeStruct((NQ, HD, n), f32)] + [jax.ShapeDtypeStruct((NKV, n, HD), f32)] * 2,
        name="flash_bwd", compiler_params=_params(("parallel", "arbitrary", "arbitrary")))(
            q, do, lse_row, delta_row, k, kt, v)


def _attn_fwd(h_bf, w_qkv_t, qg, kg, tabs):
    e, et, cos, sin = tabs
    qkv = _mm("attn_qkv_mm", h_bf, w_qkv_t, trans_b=True)
    q_bf, k_bf, v_bf = _qk_fwd(qkv, qg, kg, e, et, cos, sin)
    q16, k4, v4 = _to_heads(q_bf, NQ), _to_heads(k_bf, NKV), _to_heads(v_bf, NKV)
    ones = jnp.zeros((NKV, v4.shape[1], HD), bf16).at[:, :, 0].set(1.0)
    o, o_bf, lse = _flash_fwd(q16, k4, jnp.concatenate([v4, ones], axis=2))
    lse_row = lse.transpose(0, 2, 1).reshape(NQ, 1, qkv.shape[0])
    return o_bf, (qkv, q16, k4, v4, o, lse_row, o_bf)


def _attn_bwd(dmix_bf, h_bf, saved, w_qkv_t, qg, kg, w_out, tabs):
    e, et, cos, sin = tabs
    qkv, q16, k4, v4, o, lse_row, o_bf = saved
    n = qkv.shape[0]
    do = _mm("attn_do_mm", dmix_bf, w_out, trans_b=True, out_dtype=bf16)
    d_w_out = _mm_tn("attn_dwout", o_bf, dmix_bf)

    def head_dots(i, do, o, e):
        return (jnp.dot(do.astype(f32) * o, e, preferred_element_type=f32, precision=HIGH) * HD,), ()

    delta = _ew("attn_delta", head_dots, [do, o], [e], [(128, f32)])[0]
    dqt, dk4, dv4 = _flash_bwd(q16, k4, k4.transpose(0, 2, 1), v4, _to_heads(do, NQ), lse_row,
                               delta[:, :NQ].T.reshape(NQ, 1, n))
    dq = dqt.transpose(2, 0, 1).reshape(n, QW)
    dqkv_bf, dgq, dgk = _qk_bwd(qkv, dq, _from_heads(dk4), _from_heads(dv4), qg, kg, e, et, cos, sin)
    d_w_qkv_t = _mm_tn("attn_dwqkv", dqkv_bf, h_bf)
    dh = _mm("attn_dh_mm", dqkv_bf, w_qkv_t)
    return dh, d_w_qkv_t, dgq.reshape(NQ, HD).sum(0), dgk.reshape(NKV, HD).sum(0), d_w_out


def _all_gather(name, shard):
    def body(x_ref, out_ref, send_sems, recv_sems, local_sem):
        x, y, c = lax.axis_index("x"), lax.axis_index("y"), lax.axis_index("c")
        me, sibling = (x, y, c), (x, y, 1 - c)
        chips = [(1 - x, y), (x, 1 - y), (1 - x, 1 - y)]

        def slot(px, py, pc):
            return out_ref.at[4 * px + 2 * py + pc]

        def copy(k, block, to, src=None):
            return pltpu.make_async_remote_copy(
                src_ref=slot(*block) if src is None else src, dst_ref=slot(*block),
                send_sem=send_sems.at[k], recv_sem=recv_sems.at[k], device_id=to, device_id_type=MESH)

        mine = pltpu.make_async_copy(x_ref, slot(*me), local_sem)
        mine.start()
        first = [copy(0, me, sibling, src=x_ref)]
        first += [copy(1 + j, me, (*chip, c), src=x_ref) for j, chip in enumerate(chips)]
        for cp in first:
            cp.start()
        passed = [copy(4 + j, (*chip, c), sibling) for j, chip in enumerate(chips)]
        for j, chip in enumerate(chips):
            copy(1 + j, (*chip, c), me).wait_recv()
            passed[j].start()
        copy(0, sibling, me).wait_recv()
        for j, chip in enumerate(chips):
            copy(4 + j, (*chip, 1 - c), me).wait_recv()
        for cp in first + passed:
            cp.wait_send()
        mine.wait()

    return pl.pallas_call(
        body, out_shape=jax.ShapeDtypeStruct((8,) + shard.shape, shard.dtype), in_specs=[ANY], out_specs=ANY,
        scratch_shapes=[pltpu.SemaphoreType.DMA((7,)), pltpu.SemaphoreType.DMA((7,)), pltpu.SemaphoreType.DMA],
        name=name)(shard)


def _swap_sibling(name, theirs):
    k = len(theirs)

    def body(*refs):
        src, dst, send_sems, recv_sems = refs[:k], refs[k:2 * k], refs[2 * k], refs[2 * k + 1]
        x, y, c = lax.axis_index("x"), lax.axis_index("y"), lax.axis_index("c")
        copies = [pltpu.make_async_remote_copy(src_ref=src[j], dst_ref=dst[j], send_sem=send_sems.at[j],
                                               recv_sem=recv_sems.at[j], device_id=(x, y, 1 - c), device_id_type=MESH)
                  for j in range(k)]
        for cp in copies:
            cp.start()
        for cp in copies:
            cp.wait()

    return pl.pallas_call(
        body, out_shape=[jax.ShapeDtypeStruct(a.shape, a.dtype) for a in theirs], in_specs=[ANY] * k,
        out_specs=[ANY] * k, scratch_shapes=[pltpu.SemaphoreType.DMA((k,)), pltpu.SemaphoreType.DMA((k,))],
        name=name)(*theirs)


def _exchange_chips(name, parts):
    k = len(parts)

    def body(*refs):
        p_refs, t_refs = refs[:k], refs[k:2 * k]
        send_sems, recv_sems, local_sems = refs[2 * k:]
        x, y, c = lax.axis_index("x"), lax.axis_index("y"), lax.axis_index("c")
        q = 2 * x + y
        copies = []
        for j in range(k):
            copies.append(pltpu.make_async_copy(p_refs[j].at[q], t_refs[j].at[q], local_sems.at[j]))
            for hop in (1, 2, 3):
                tx, ty = x ^ (hop >> 1), y ^ (hop & 1)
                copies.append(pltpu.make_async_remote_copy(
                    src_ref=p_refs[j].at[2 * tx + ty], dst_ref=t_refs[j].at[q], send_sem=send_sems.at[3 * j + hop - 1],
                    recv_sem=recv_sems.at[3 * j + hop - 1], device_id=(tx, ty, c), device_id_type=MESH))
        for cp in copies:
            cp.start()
        for cp in copies:
            cp.wait()

    return pl.pallas_call(
        body, out_shape=[jax.ShapeDtypeStruct(a.shape, a.dtype) for a in parts], in_specs=[ANY] * k,
        out_specs=[ANY] * k,
        scratch_shapes=[pltpu.SemaphoreType.DMA((3 * k,)), pltpu.SemaphoreType.DMA((3 * k,)),
                        pltpu.SemaphoreType.DMA((k,))],
        name=name)(*parts)


def _reduce_scatter(mine, theirs):
    got = _swap_sibling("rs_sibling", list(theirs))
    parts = []
    for a, b, dt, nm in zip(mine, got, (bf16, f32), ("rs_add2", "rs_add2_small")):
        rows = 4 * a.shape[1]
        parts.append(_ew(nm, lambda i, a, b: ((a + b,), ()), [a.reshape(rows, D), b.reshape(rows, D)], [],
                         [(D, dt)], tile=RS_TILE)[0].reshape(a.shape))
    ts = _exchange_chips("rs_chips", parts)

    def add4(i, a, b, c, d):
        return ((((a.astype(f32) + b.astype(f32)) + c.astype(f32)) + d.astype(f32),), ())

    return [_ew(nm, add4, [(t, 0), (t, 1), (t, 2), (t, 3)], [], [(D, f32)], tile=RS_TILE)[0]
            for t, nm in zip(ts, ("rs_add4", "rs_add4_small"))]


def _pack_rows(parts, rows):
    flat = jnp.concatenate([p.reshape(-1) for p in parts])
    return jnp.pad(flat, (0, rows * D - flat.shape[0])).reshape(rows, D)


def _unpack(flat, shapes):
    out, off = [], 0
    for s in shapes:
        n = math.prod(s)
        out.append(flat[off:off + n].reshape(s))
        off += n
    return out


def _mat_rows(block, transposed, blk):
    a = jnp.swapaxes(block, 1, 2) if transposed else block
    a = jnp.pad(a, ((0, 0), (0, blk - a.shape[1]), (0, 0)))
    return a.reshape(-1, D)


def _mat_block(rows, transposed, blk, real):
    a = rows.reshape(-1, blk, D)[:, :real]
    return jnp.swapaxes(a, 1, 2) if transposed else a


def _mat_full(gathered, blk):
    layers = gathered.shape[1] // blk
    return gathered.reshape(8, layers, blk, D).transpose(1, 0, 2, 3).reshape(layers, 8 * blk, D)


def _vec_full(gathered):
    return gathered.transpose(1, 0, 2).reshape(gathered.shape[1], D)


def _grad_slots(full, small, cc):
    def halves(a, blk):
        a4 = a.reshape(4, 2, blk, D)
        return [lax.dynamic_index_in_dim(a4, sel, axis=1, keepdims=False) for sel in (cc, 1 - cc)]

    mine, theirs = [], []
    for name, _, blk, _ in MATS:
        for layer in full[name]:
            a, b = halves(layer, blk)
            mine.append(a)
            theirs.append(b)
    a, b = halves(small, REP_PIECE)
    return (jnp.concatenate(mine, axis=1), a), (jnp.concatenate(theirs, axis=1), b)


def _local_step(x0, target0, w, fw):
    seq = x0.shape[0]
    n = OFF + seq
    movers = _block_movers()
    h = jnp.concatenate([jnp.zeros((PAD, D), f32), fw['meta_tokens'], x0], axis=0)
    h_bf = h.astype(bf16)
    tabs = _head_mats() + _rope_tables(n)
    qg = [jnp.tile(w['attn_q_gain'][j], NQ)[None, :] for j in range(2)]
    kg = [jnp.tile(w['attn_k_gain'][j], NKV)[None, :] for j in range(2)]
    s5_names = ['s5_lambda_re', 's5_lambda_im', 's5_log_dt', 's5_b_re', 's5_b_im', 's5_c_re', 's5_c_im']
    s5_mats, s5_vjp = [], []
    for j in range(2):
        mats, vjp = jax.vjp(_s5_mats, *[w[k][j] for k in s5_names])
        s5_mats.append(mats)
        s5_vjp.append(vjp)
    saved = []
    for i in range(DEPTH):
        j = i // 2
        if i % 2 == 0:
            mixed, sv = _s5_fwd(h, movers, s5_mats[j], w['s5_d'][j][None, :], (fw['s5_w_glu'], j))
            w_out = (fw['s5_w_out'], j)
        else:
            mixed, sv = _attn_fwd(h_bf, (fw['attn_w_qkv'], j), qg[j], kg[j], tabs)
            w_out = (fw['attn_w_out'], j)
        r1, h1, h1_bf = _mm_ln("mixer_out_ln", mixed, w_out, h, fw['ln_gain'][i, 0][None, :], fw['ln_bias'][i, 0][None, :])
        gate, up, act = _ffn_up(h1_bf, (fw['ffn_w_gate'], i), (fw['ffn_w_up'], i))
        r2, h2, h2_bf = _mm_ln("ffn_down_ln", act, (fw['ffn_w_down'], i), h1, fw['ln_gain'][i, 1][None, :],
                               fw['ln_bias'][i, 1][None, :])
        saved.append((h, h_bf, sv, r1, h1_bf, gate, up, act, r2))
        h, h_bf = h2, h2_bf

    d_b, sq = _loss_grad(h, target0)
    loss = 0.5 * jnp.sum(sq) * (1.0 / D)

    grads = {k: [None] * (DEPTH if k.startswith('ffn') else 2) for k in WEIGHTS}
    d_ln_gain = [[None, None] for _ in range(DEPTH)]
    d_ln_bias = [[None, None] for _ in range(DEPTH)]
    d_a = None
    for i in reversed(range(DEPTH)):
        j = i // 2
        h_in, h_in_bf, sv, r1, h1_bf, gate, up, act, r2 = saved[i]
        dr2, dr2_bf, dg, db = _ln_bwd(d_a, d_b, r2, fw['ln_gain'][i, 1][None, :])
        d_ln_gain[i][1], d_ln_bias[i][1] = dg[0], db[0]
        dgate, dup = _ffn_dup(dr2_bf, (fw['ffn_w_down'], i), gate, up)
        grads['ffn_w_down'][i] = _mm_tn("ffn_dwdown", act, dr2_bf, tk=DFFP // 2)
        grads['ffn_w_gate'][i] = _mm_tn("ffn_dwgate", dgate, h1_bf, tk=DFFP // 2)
        grads['ffn_w_up'][i] = _mm_tn("ffn_dwup", dup, h1_bf, tk=DFFP // 2)
        dh1 = _mm2("ffn_dh_mm", dgate, (fw['ffn_w_gate'], i), dup, (fw['ffn_w_up'], i))
        dr1, dr1_bf, dg, db = _ln_bwd(dr2, dh1, r1, fw['ln_gain'][i, 0][None, :])
        d_ln_gain[i][0], d_ln_bias[i][0] = dg[0], db[0]
        if i % 2 == 0:
            dh, d_par, dd, d_w_glu, d_w_out = _s5_bwd(dr1_bf, h_in, movers, sv, s5_mats[j], s5_vjp[j],
                                                      w['s5_d'][j][None, :], (fw['s5_w_glu'], j), (fw['s5_w_out'], j))
            for k, g in zip(s5_names, d_par):
                grads[k][j] = g
            grads['s5_d'][j], grads['s5_w_glu'][j], grads['s5_w_out'][j] = dd, d_w_glu, d_w_out
        else:
            dh, d_w_qkv, dgq, dgk, d_w_out = _attn_bwd(dr1_bf, h_in_bf, sv, (fw['attn_w_qkv'], j), qg[j], kg[j],
                                                       (fw['attn_w_out'], j), tabs)
            grads['attn_w_qkv'][j], grads['attn_w_out'][j] = d_w_qkv, d_w_out
            grads['attn_q_gain'][j], grads['attn_k_gain'][j] = dgq, dgk
        d_a, d_b = dr1, dh
    dh0 = _ew("dh0", lambda i, a, b: ((ALPHA * a + b,), ()), [d_a, d_b], [], [(D, f32)])[0]
    mats = {m[0] for m in MATS}
    full = {k: (v if k in mats else jnp.stack(v)) for k, v in grads.items() if v[0] is not None}
    full['meta_tokens'] = dh0[PAD:OFF]
    full['ln_gain'] = jnp.stack([jnp.stack(r) for r in d_ln_gain])
    full['ln_bias'] = jnp.stack([jnp.stack(r) for r in d_ln_bias])

    return loss, dh0[OFF:], full


def kernel(x, meta_tokens, s5_lambda_re, s5_lambda_im, s5_log_dt, s5_b_re, s5_b_im, s5_c_re, s5_c_im, s5_d, s5_w_glu, s5_w_out, attn_w_qkv, attn_q_gain, attn_k_gain, attn_w_out, ffn_w_gate, ffn_w_up, ffn_w_down, ln_gain, ln_bias, loss_target, m_meta_tokens, m_s5_lambda_re, m_s5_lambda_im, m_s5_log_dt, m_s5_b_re, m_s5_b_im, m_s5_c_re, m_s5_c_im, m_s5_d, m_s5_w_glu, m_s5_w_out, m_attn_w_qkv, m_attn_q_gain, m_attn_k_gain, m_attn_w_out, m_ffn_w_gate, m_ffn_w_up, m_ffn_w_down, m_ln_gain, m_ln_bias, v_meta_tokens, v_s5_lambda_re, v_s5_lambda_im, v_s5_log_dt, v_s5_b_re, v_s5_b_im, v_s5_c_re, v_s5_c_im, v_s5_d, v_s5_w_glu, v_s5_w_out, v_attn_w_qkv, v_attn_q_gain, v_attn_k_gain, v_attn_w_out, v_ffn_w_gate, v_ffn_w_up, v_ffn_w_down, v_ln_gain, v_ln_bias):
    w = dict(zip(WEIGHTS, (meta_tokens, s5_lambda_re, s5_lambda_im, s5_log_dt, s5_b_re, s5_b_im, s5_c_re, s5_c_im, s5_d, s5_w_glu, s5_w_out, attn_w_qkv, attn_q_gain, attn_k_gain, attn_w_out, ffn_w_gate, ffn_w_up, ffn_w_down, ln_gain, ln_bias)))
    mom = dict(zip(WEIGHTS, (m_meta_tokens, m_s5_lambda_re, m_s5_lambda_im, m_s5_log_dt, m_s5_b_re, m_s5_b_im, m_s5_c_re, m_s5_c_im, m_s5_d, m_s5_w_glu, m_s5_w_out, m_attn_w_qkv, m_attn_q_gain, m_attn_k_gain, m_attn_w_out, m_ffn_w_gate, m_ffn_w_up, m_ffn_w_down, m_ln_gain, m_ln_bias)))
    vel = dict(zip(WEIGHTS, (v_meta_tokens, v_s5_lambda_re, v_s5_lambda_im, v_s5_log_dt, v_s5_b_re, v_s5_b_im, v_s5_c_re, v_s5_c_im, v_s5_d, v_s5_w_glu, v_s5_w_out, v_attn_w_qkv, v_attn_q_gain, v_attn_k_gain, v_attn_w_out, v_ffn_w_gate, v_ffn_w_up, v_ffn_w_down, v_ln_gain, v_ln_bias)))
    cc = lax.axis_index("c")
    dev = 4 * lax.axis_index("x") + 2 * lax.axis_index("y") + cc

    mat_rows = jnp.concatenate([_mat_rows(w[n], t, blk) for n, t, blk, _ in MATS]).astype(bf16)
    g_mats = _all_gather("ag_weights", mat_rows)
    g_vecs = _all_gather("ag_vectors", jnp.concatenate([w[n].reshape(-1, 128) for n in VECS]))
    fw, off = {}, 0
    for n, _, blk, _ in MATS:
        rows = w[n].shape[0] * blk
        fw[n] = _mat_full(g_mats[:, off:off + rows], blk)
        off += rows
    off = 0
    for n in VECS:
        rows = w[n].size // 128
        fw[n] = _vec_full(g_vecs[:, off:off + rows]).reshape(w[n].shape[:-1] + (D,))
        off += rows

    loss, grad_x, full = _local_step(x[0], loss_target[0], w, fw)
    loss = lax.psum(loss, AXES)
    grad_x = grad_x[None]

    small_names = REPL + VECS
    mine, theirs = _grad_slots(full, _pack_rows([full[k] for k in small_names], REP_ROWS), cc)
    red, red_small = _reduce_scatter(mine, theirs)
    small_all = _all_gather("ag_small_grads", red_small).reshape(REP_ROWS * D)
    g, off = {}, 0
    for n, t, blk, real in MATS:
        rows = w[n].shape[0] * blk
        g[n] = _mat_block(red[off:off + rows], t, blk, real)
        off += rows
    small = dict(zip(small_names, _unpack(small_all, [full[k].shape for k in small_names])))
    for k in REPL:
        g[k] = small[k]
    for k in VECS:
        g[k] = lax.dynamic_slice_in_dim(small[k], dev * 128, 128, axis=small[k].ndim - 1)

    delta, new_m, new_v = {}, {}, {}
    for n in WEIGHTS:
        shp = w[n].shape
        res = _adamw(*[d[n].reshape(-1, shp[-1]) for d in (w, g, mom, vel)])
        delta[n], new_m[n], new_v[n] = [a.reshape(shp) for a in res]
    return (loss, grad_x, *[g[k] for k in WEIGHTS], *[delta[k] for k in WEIGHTS],
            *[new_m[k] for k in WEIGHTS], *[new_v[k] for k in WEIGHTS])
```

```python
import functools
import math

import jax
import jax.numpy as jnp
from jax import lax
from jax.experimental import pallas as pl
from jax.experimental.pallas import tpu as pltpu

f32 = jnp.float32
bf16 = jnp.bfloat16
HIGH = lax.Precision.HIGH
MESH = pl.DeviceIdType.MESH
AXES = ("x", "y", "c")
ANY = pl.BlockSpec(memory_space=pl.ANY)

D = 1024
DEPTH = 4
N_META = 16
PAD = 240
OFF = PAD + N_META
ROW_TILE = 768
KEY_CHUNK = 256
FFN_TILE = 256
ADAM_TILE = 544
GRID_W = 64
HD = 64
NQ = 16
NKV = 4
QW = NQ * HD
KW = NKV * HD
QKVW = QW + 2 * KW
DFF = 2816
GROUPS = 64
GCH = 16
NSTATE = 64
CHUNK = 16
GB = 8
ROPE_THETA = 10000.0
LN_EPS = 1e-5
QK_EPS = 1e-6
ALPHA = (2.0 * DEPTH) ** 0.25
ADAM_LR, ADAM_B1, ADAM_B2, ADAM_EPS, ADAM_WD, ADAM_STEP = 0.001, 0.9, 0.999, 1e-08, 0.01, 10
NEG = -1e30
Q_SCALE = HD ** -0.5 * math.log2(math.e)
VMEM_MB = 56

NT = (((1,), (1,)), ((), ()))
TN = (((0,), (0,)), ((), ()))

WEIGHTS = ['meta_tokens', 's5_lambda_re', 's5_lambda_im', 's5_log_dt', 's5_b_re', 's5_b_im', 's5_c_re', 's5_c_im',
           's5_d', 's5_w_glu', 's5_w_out', 'attn_w_qkv', 'attn_q_gain', 'attn_k_gain', 'attn_w_out', 'ffn_w_gate',
           'ffn_w_up', 'ffn_w_down', 'ln_gain', 'ln_bias']
DFFP = 3072
FF_BLK, FF_BLKP = DFF // 8, DFFP // 8
MATS = [('s5_w_glu', False, 128, 128), ('s5_w_out', False, 128, 128), ('attn_w_qkv', True, 192, 192),
        ('attn_w_out', False, 128, 128), ('ffn_w_gate', True, FF_BLKP, FF_BLK), ('ffn_w_up', True, FF_BLKP, FF_BLK),
        ('ffn_w_down', False, FF_BLKP, FF_BLK)]
VECS = ['meta_tokens', 'ln_gain', 'ln_bias']
REPL = ['s5_lambda_re', 's5_lambda_im', 's5_log_dt', 's5_b_re', 's5_b_im', 's5_c_re', 's5_c_im', 's5_d',
        'attn_q_gain', 'attn_k_gain']
MAT_ROWS = 5760
REP_PIECE = 160
REP_ROWS = 8 * REP_PIECE
RS_TILE = 640


def _params(sem, mb=VMEM_MB):
    return pltpu.CompilerParams(dimension_semantics=sem, vmem_limit_bytes=mb << 20)


def _ew(name, fn, rows, consts, outs, accs=(), tile=ROW_TILE):
    first = rows[0][0] if isinstance(rows[0], tuple) else rows[0]
    n = first.shape[-2]
    tile = min(tile, n)
    assert n % tile == 0, (name, n, tile)
    n_in, n_o, n_a = len(rows) + len(consts), len(outs), len(accs)

    def body(*refs):
        i = pl.program_id(0)
        res_o, res_a = fn(i, *[r[...] for r in refs[:n_in]])
        for r, val in zip(refs[n_in:n_in + n_o], res_o):
            r[...] = val.astype(r.dtype)
        if n_a:
            a_refs = refs[n_in + n_o:]

            @pl.when(i == 0)
            def _():
                for r in a_refs:
                    r[...] = jnp.zeros(r.shape, r.dtype)

            for r, val in zip(a_refs, res_a):
                r[...] += val

    in_specs, args = [], []
    for a in rows:
        if isinstance(a, tuple):
            arr, k = a
            in_specs.append(pl.BlockSpec((None, tile, arr.shape[2]), functools.partial(lambda i, k: (k, i, 0), k=k)))
            args.append(arr)
        else:
            in_specs.append(pl.BlockSpec((tile, a.shape[1]), lambda i: (i, 0)))
            args.append(a)
    for c in consts:
        in_specs.append(pl.BlockSpec(c.shape, lambda i: (0, 0)))
        args.append(c)
    out_specs = [pl.BlockSpec((tile, c), lambda i: (i, 0)) for c, _ in outs]
    out_specs += [pl.BlockSpec(s, lambda i: (0, 0)) for s in accs]
    out_shape = [jax.ShapeDtypeStruct((n, c), dt) for c, dt in outs]
    out_shape += [jax.ShapeDtypeStruct(s, f32) for s in accs]
    res = pl.pallas_call(body, grid=(n // tile,), in_specs=in_specs, out_specs=out_specs, out_shape=out_shape,
                         name=name, compiler_params=_params(("arbitrary",)))(*args)
    return res


def _mm(name, a, b, trans_b=False, out_dtype=f32, tm=ROW_TILE):
    m, k = a.shape
    spec, b, shape = _whole(b)
    n = shape[0] if trans_b else shape[1]
    tm = min(tm, m)
    assert m % tm == 0
    dims = NT if trans_b else (((1,), (0,)), ((), ()))

    def body(a_ref, b_ref, o_ref):
        o_ref[...] = lax.dot_general(a_ref[...], b_ref[...], dims, preferred_element_type=f32).astype(o_ref.dtype)

    return pl.pallas_call(
        body, grid=(m // tm,), in_specs=[pl.BlockSpec((tm, k), lambda i: (i, 0)), spec],
        out_specs=pl.BlockSpec((tm, n), lambda i: (i, 0)),
        out_shape=jax.ShapeDtypeStruct((m, n), out_dtype), name=name, compiler_params=_params(("parallel",)))(a, b)


def _whole(b):
    if isinstance(b, tuple):
        arr, layer = b
        return pl.BlockSpec((None,) + arr.shape[1:], lambda i: (layer, 0, 0)), arr, arr.shape[1:]
    return pl.BlockSpec(b.shape, lambda i: (0, 0)), b, b.shape


def _mm2(name, a1, b1, a2, b2, out_dtype=f32, tm=ROW_TILE // 2):
    m, k = a1.shape
    spec1, b1, shape = _whole(b1)
    spec2, b2, _ = _whole(b2)
    n = shape[1]
    tm = min(tm, m)
    assert m % tm == 0

    def body(a1_ref, b1_ref, a2_ref, b2_ref, o_ref):
        acc = jnp.dot(a1_ref[...], b1_ref[...], preferred_element_type=f32)
        acc += jnp.dot(a2_ref[...], b2_ref[...], preferred_element_type=f32)
        o_ref[...] = acc.astype(o_ref.dtype)

    row = pl.BlockSpec((tm, k), lambda i: (i, 0))
    return pl.pallas_call(
        body, grid=(m // tm,), in_specs=[row, spec1, row, spec2], out_specs=pl.BlockSpec((tm, n), lambda i: (i, 0)),
        out_shape=jax.ShapeDtypeStruct((m, n), out_dtype), name=name,
        compiler_params=_params(("parallel",)))(a1, b1, a2, b2)


def _mm_tn(name, a, g, tk=512, tl=ROW_TILE):
    rows, k1 = a.shape
    n = g.shape[1]
    tl = min(tl, rows)
    assert rows % tl == 0 and k1 % tk == 0

    def body(a_ref, g_ref, o_ref):
        @pl.when(pl.program_id(1) == 0)
        def _():
            o_ref[...] = jnp.zeros(o_ref.shape, f32)

        o_ref[...] += lax.dot_general(a_ref[...], g_ref[...], TN, preferred_element_type=f32)

    return pl.pallas_call(
        body, grid=(k1 // tk, rows // tl),
        in_specs=[pl.BlockSpec((tl, tk), lambda k, l: (l, k)), pl.BlockSpec((tl, n), lambda k, l: (l, 0))],
        out_specs=pl.BlockSpec((tk, n), lambda k, l: (k, 0)),
        out_shape=jax.ShapeDtypeStruct((k1, n), f32), name=name,
        compiler_params=_params(("parallel", "arbitrary")))(a, g)


def _ln_stats(r):
    mean = jnp.mean(r, axis=-1, keepdims=True)
    c = r - mean
    rstd = lax.rsqrt(jnp.mean(c * c, axis=-1, keepdims=True) + LN_EPS)
    return c * rstd, rstd


def _mm_ln(name, a, b, h, gain, bias, tm=ROW_TILE):
    m, k = a.shape
    spec, b, _ = _whole(b)

    def body(a_ref, b_ref, h_ref, g_ref, bias_ref, r_ref, y_ref, yb_ref):
        r = ALPHA * h_ref[...] + jnp.dot(a_ref[...], b_ref[...], preferred_element_type=f32)
        y = _ln_stats(r)[0] * g_ref[...] + bias_ref[...]
        r_ref[...] = r
        y_ref[...] = y
        yb_ref[...] = y.astype(bf16)

    row = pl.BlockSpec((tm, D), lambda i: (i, 0))
    vec = pl.BlockSpec((1, D), lambda i: (0, 0))
    return pl.pallas_call(
        body, grid=(m // tm,), in_specs=[pl.BlockSpec((tm, k), lambda i: (i, 0)), spec, row, vec, vec],
        out_specs=[row, row, row],
        out_shape=[jax.ShapeDtypeStruct((m, D), f32), jax.ShapeDtypeStruct((m, D), f32), jax.ShapeDtypeStruct((m, D), bf16)],
        name=name, compiler_params=_params(("parallel",)))(a, b, h, gain, bias)


def _ln_bwd(d_a, d_b, r, gain):
    def core(dout, r, g):
        xhat, rstd = _ln_stats(r)
        dxh = dout * g
        dr = rstd * (dxh - jnp.mean(dxh, axis=-1, keepdims=True) - xhat * jnp.mean(dxh * xhat, axis=-1, keepdims=True))
        return (dr, dr), (jnp.sum(dout * xhat, axis=0, keepdims=True), jnp.sum(dout, axis=0, keepdims=True))

    outs, accs = [(D, f32), (D, bf16)], [(1, D), (1, D)]
    if d_a is None:
        return _ew("ln_bwd_top", lambda i, d, r, g: core(d, r, g), [d_b, r], [gain], outs, accs)
    return _ew("ln_bwd", lambda i, da, db, r, g: core(ALPHA * da + db, r, g), [d_a, d_b, r], [gain], outs, accs)


def _sigmoid(x):
    return 1.0 / (1.0 + jnp.exp(-x))


def _ffn_up(h_bf, w_gate_t, w_up_t, tm=FFN_TILE):
    m, k = h_bf.shape
    gspec, w_gate_t, (n, _) = _whole(w_gate_t)
    uspec, w_up_t, _ = _whole(w_up_t)

    def body(h_ref, wg_ref, wu_ref, g_ref, u_ref, a_ref):
        h = h_ref[...]
        g = lax.dot_general(h, wg_ref[...], NT, preferred_element_type=f32).astype(bf16)
        u = lax.dot_general(h, wu_ref[...], NT, preferred_element_type=f32).astype(bf16)
        g_ref[...] = g
        u_ref[...] = u
        g = g.astype(f32)
        a_ref[...] = (g * _sigmoid(g) * u.astype(f32)).astype(bf16)

    row = pl.BlockSpec((tm, n), lambda i: (i, 0))
    return pl.pallas_call(
        body, grid=(m // tm,), in_specs=[pl.BlockSpec((tm, k), lambda i: (i, 0)), gspec, uspec],
        out_specs=[row, row, row], out_shape=[jax.ShapeDtypeStruct((m, n), bf16)] * 3, name="ffn_up",
        compiler_params=_params(("parallel",)))(h_bf, w_gate_t, w_up_t)


def _ffn_dup(df_bf, w_down, gate, up, tm=FFN_TILE):
    m, k = df_bf.shape
    wspec, w_down, (n, _) = _whole(w_down)

    def body(d_ref, w_ref, g_ref, u_ref, dg_ref, du_ref):
        da = lax.dot_general(d_ref[...], w_ref[...], NT, preferred_element_type=f32).astype(bf16).astype(f32)
        g, u = g_ref[...].astype(f32), u_ref[...].astype(f32)
        s = _sigmoid(g)
        dg_ref[...] = (da * u * s * (1.0 + g * (1.0 - s))).astype(bf16)
        du_ref[...] = (da * g * s).astype(bf16)

    row = pl.BlockSpec((tm, n), lambda i: (i, 0))
    return pl.pallas_call(
        body, grid=(m // tm,),
        in_specs=[pl.BlockSpec((tm, k), lambda i: (i, 0)), wspec, row, row],
        out_specs=[row, row], out_shape=[jax.ShapeDtypeStruct((m, n), bf16)] * 2, name="ffn_dup",
        compiler_params=_params(("parallel",)))(df_bf, w_down, gate, up)


def _loss_grad(h, target):
    n = h.shape[0]

    def body(h_ref, t_ref, d_ref, sq_ref):
        i = pl.program_id(0)

        @pl.when(i == 0)
        def _():
            d_ref[...] = jnp.zeros(d_ref.shape, f32)
            sq_ref[...] = jnp.zeros(sq_ref.shape, f32)

        @pl.when(i > 0)
        def _():
            e = h_ref[...] - t_ref[...]
            d_ref[...] = e * (1.0 / D)
            sq_ref[...] += jnp.sum(e * e, axis=0, keepdims=True)

    return pl.pallas_call(
        body, grid=(n // OFF,),
        in_specs=[pl.BlockSpec((OFF, D), lambda i: (i, 0)), pl.BlockSpec((OFF, D), lambda i: (jnp.maximum(i - 1, 0), 0))],
        out_specs=[pl.BlockSpec((OFF, D), lambda i: (i, 0)), pl.BlockSpec((1, D), lambda i: (0, 0))],
        out_shape=[jax.ShapeDtypeStruct((n, D), f32), jax.ShapeDtypeStruct((1, D), f32)], name="loss",
        compiler_params=_params(("arbitrary",)))(h, target)


def _adamw(w, g, m, v):
    def fn(i, w, g, m, v):
        m = ADAM_B1 * m + (1.0 - ADAM_B1) * g
        v = ADAM_B2 * v + (1.0 - ADAM_B2) * jnp.square(g)
        m_hat = m / (1.0 - ADAM_B1 ** ADAM_STEP)
        v_hat = v / (1.0 - ADAM_B2 ** ADAM_STEP)
        delta = -ADAM_LR * (m_hat / (jnp.sqrt(v_hat) + ADAM_EPS) + ADAM_WD * w)
        return (delta, m, v), ()

    rows, cols = w.shape
    cap = ADAM_TILE if cols > 128 else 4 * ADAM_TILE
    fits = [t for t in range(8, min(rows, cap) + 1, 8) if rows % t == 0]
    return _ew("adamw", fn, [w, g, m, v], [], [(cols, f32)] * 3, tile=max(fits) if fits else rows)


def _s5_mats(lam_re, lam_im, log_dt, b_re, b_im, c_re, c_im):
    steps = jnp.arange(CHUNK + 1, dtype=f32)
    n = CHUNK * GCH
    last = n - GCH

    def one(lr, li, ldt, br, bi, cr, ci, reverse):
        dt = jnp.exp(ldt)[:, None]
        mag = jnp.exp(lr * dt)
        abr, abi = mag * jnp.cos(li * dt), mag * jnp.sin(li * dt)
        nr, ni = abr - 1.0, abi
        den = lr * lr + li * li
        zr, zi = (nr * lr + ni * li) / den, (ni * lr - nr * li) / den
        bbr = zr[..., None] * br - zi[..., None] * bi
        bbi = zr[..., None] * bi + zi[..., None] * br
        pmag = jnp.exp((lr * dt)[..., None] * steps)
        pang = (li * dt)[..., None] * steps
        pr, pi = pmag * jnp.cos(pang), pmag * jnp.sin(pang)
        crt, cit = jnp.swapaxes(cr, 1, 2)[:, :, None, :], jnp.swapaxes(ci, 1, 2)[:, :, None, :]
        car = crt * pr[..., None] - cit * pi[..., None]
        cai = crt * pi[..., None] + cit * pr[..., None]
        if reverse:
            taps = slice(CHUNK - 1, None, -1)
            outs = slice(CHUNK, 0, -1)
            ins = slice(0, CHUNK)
        else:
            taps, outs, ins = slice(0, CHUNK), slice(1, CHUNK + 1), slice(CHUNK - 1, None, -1)
        kern = (jnp.einsum('gpi,gpq->giq', bbr, car[:, :, taps].reshape(GROUPS, NSTATE, n), precision=HIGH)
                - jnp.einsum('gpi,gpq->giq', bbi, cai[:, :, taps].reshape(GROUPS, NSTATE, n), precision=HIGH))
        wide = jnp.pad(kern, ((0, 0), (0, 0), (0, last) if reverse else (last, 0)))
        m = jnp.stack([wide[:, :, last - GCH * t:last - GCH * t + n] for t in range(CHUNK)], axis=1)
        qr = jnp.swapaxes(pr[:, :, ins], 1, 2)[:, :, None, :]
        qi = jnp.swapaxes(pi[:, :, ins], 1, 2)[:, :, None, :]
        bbrt, bbit = jnp.swapaxes(bbr, 1, 2)[:, None], jnp.swapaxes(bbi, 1, 2)[:, None]
        pin = jnp.concatenate([qr * bbrt - qi * bbit, qr * bbit + qi * bbrt], axis=-1)
        pout = jnp.concatenate([car[:, :, outs].reshape(GROUPS, NSTATE, n),
                                -cai[:, :, outs].reshape(GROUPS, NSTATE, n)], axis=1)
        return (m.reshape(GROUPS, n, n), pin.reshape(GROUPS, n, 2 * NSTATE), pout, pr[:, :, CHUNK], pi[:, :, CHUNK])

    mf, pinf, poutf, arf, aif = one(lam_re[0], lam_im[0], log_dt[0], b_re[0], b_im[0], c_re[0], c_im[0], False)
    mr, pinr, poutr, arr, air = one(lam_re[1], lam_im[1], log_dt[1], b_re[1], b_im[1], c_re[1], c_im[1], True)
    return (mf + mr, jnp.concatenate([pinf, pinr], 2), jnp.concatenate([poutf, poutr], 1),
            jnp.stack([arf, arr]), jnp.stack([aif, air]))


def _s5_coefs(a_re, a_im):
    c1 = jnp.concatenate([a_re, a_re], -1)
    c2 = jnp.concatenate([-a_im, a_im], -1)
    return tuple(c.reshape(GROUPS // GB, 1, GB * 2 * NSTATE) for c in (c1[0], c2[0], c1[1], c2[1]))


def _swap(s):
    w = s.shape[1]
    lane = lax.broadcasted_iota(jnp.int32, s.shape, 1)
    return jnp.where(lane % (2 * NSTATE) < NSTATE, pltpu.roll(s, w - NSTATE, 1), pltpu.roll(s, NSTATE, 1))


def _group_lanes(g):
    return slice(g * 2 * NSTATE, (g + 1) * 2 * NSTATE)


def _s5_states(nc, u_ref, pin_ref, coef, vf, vr, wf, wr, sf, sr):
    c1f, c2f, c1r, c2r = coef
    for g in range(GB):
        v = jnp.dot(u_ref[g], pin_ref[g], preferred_element_type=f32)
        vf[:, _group_lanes(g)] = v[:, :2 * NSTATE]
        vr[:, _group_lanes(g)] = v[:, 2 * NSTATE:]
    wf[...] = _swap(vf[...])
    wr[...] = _swap(vr[...])

    def step(i, carry):
        s_f, t_f, s_r, t_r = carry
        kf, kr = pl.ds(i, 1), pl.ds(nc - 1 - i, 1)
        sf[kf, :] = s_f
        sr[kr, :] = s_r
        s_f, t_f = c1f * s_f + c2f * t_f + vf[kf, :], c1f * t_f - c2f * s_f + wf[kf, :]
        s_r, t_r = c1r * s_r + c2r * t_r + vr[kr, :], c1r * t_r - c2r * s_r + wr[kr, :]
        return s_f, t_f, s_r, t_r

    z = jnp.zeros((1, GB * 2 * NSTATE), f32)
    lax.fori_loop(0, nc, step, (z, z, z, z))


def _s5_core_fwd(ug, msum, pin, pout, coefs):
    nc = ug.shape[1]
    n = CHUNK * GCH

    def body(u_ref, m_ref, pin_ref, pout_ref, c1f, c2f, c1r, c2r, y_ref, vf, vr, wf, wr, sf, sr):
        coef = (c1f[...], c2f[...], c1r[...], c2r[...])
        _s5_states(nc, u_ref, pin_ref, coef, vf, vr, wf, wr, sf, sr)
        for g in range(GB):
            s_in = jnp.concatenate([sf[:, _group_lanes(g)], sr[:, _group_lanes(g)]], axis=1).astype(bf16)
            y_ref[g] = (jnp.dot(u_ref[g], m_ref[g], preferred_element_type=f32)
                        + jnp.dot(s_in, pout_ref[g], preferred_element_type=f32)).astype(bf16)

    seq = pl.BlockSpec((GB, nc, n), lambda i: (i, 0, 0))
    mat = pl.BlockSpec((GB, n, n), lambda i: (i, 0, 0))
    cf = pl.BlockSpec((None, 1, GB * 2 * NSTATE), lambda i: (i, 0, 0))
    scr = pltpu.VMEM((nc, GB * 2 * NSTATE), f32)
    return pl.pallas_call(
        body, grid=(GROUPS // GB,), in_specs=[seq, mat, mat, mat, cf, cf, cf, cf], out_specs=seq,
        out_shape=jax.ShapeDtypeStruct((GROUPS, nc, n), bf16), scratch_shapes=[scr] * 6,
        name="s5_core_fwd", compiler_params=_params(("parallel",)))(ug, msum, pin, pout, *coefs)


def _s5_core_bwd(ug, dyg, msum, pin, pout, coefs):
    nc = ug.shape[1]
    n = CHUNK * GCH

    def body(u_ref, dy_ref, m_ref, pin_ref, pout_ref, c1f, c2f, c1r, c2r,
             du_ref, dm_ref, dpin_ref, dpout_ref, a1f_ref, a2f_ref, a1r_ref, a2r_ref, vf, vr, wf, wr, sf, sr):
        coef = (c1f[...], c2f[...], c1r[...], c2r[...])
        _s5_states(nc, u_ref, pin_ref, coef, vf, vr, wf, wr, sf, sr)
        for g in range(GB):
            s_in = jnp.concatenate([sf[:, _group_lanes(g)], sr[:, _group_lanes(g)]], axis=1).astype(bf16)
            dy = dy_ref[g]
            ds = lax.dot_general(dy, pout_ref[g], NT, preferred_element_type=f32)
            vf[:, _group_lanes(g)] = ds[:, :2 * NSTATE]
            vr[:, _group_lanes(g)] = ds[:, 2 * NSTATE:]
            dpout_ref[g] = lax.dot_general(s_in, dy, TN, preferred_element_type=f32)
            dm_ref[g] = lax.dot_general(u_ref[g], dy, TN, preferred_element_type=f32)

        wf[...] = _swap(vf[...])
        wr[...] = _swap(vr[...])
        k1f, k2f, k1r, k2r = coef[0], -coef[1], coef[2], -coef[3]

        def step(i, carry):
            g_f, h_f, g_r, h_r, a1f, b2f, a1r, b2r = carry
            kf, kr = pl.ds(nc - 1 - i, 1), pl.ds(i, 1)
            s_f, s_r = sf[kf, :], sr[kr, :]
            sf[kf, :] = g_f
            sr[kr, :] = g_r
            a1f, b2f = a1f + g_f * s_f, b2f + h_f * s_f
            a1r, b2r = a1r + g_r * s_r, b2r + h_r * s_r
            g_f, h_f = vf[kf, :] + k1f * g_f + k2f * h_f, wf[kf, :] + k1f * h_f - k2f * g_f
            g_r, h_r = vr[kr, :] + k1r * g_r + k2r * h_r, wr[kr, :] + k1r * h_r - k2r * g_r
            return g_f, h_f, g_r, h_r, a1f, b2f, a1r, b2r

        z = jnp.zeros((1, GB * 2 * NSTATE), f32)
        _, _, _, _, a1f, b2f, a1r, b2r = lax.fori_loop(0, nc, step, (z,) * 8)
        a1f_ref[...], a2f_ref[...], a1r_ref[...], a2r_ref[...] = a1f, _swap(b2f), a1r, _swap(b2r)
        for g in range(GB):
            dv = jnp.concatenate([sf[:, _group_lanes(g)], sr[:, _group_lanes(g)]], axis=1).astype(bf16)
            du_ref[g] = (lax.dot_general(dy_ref[g], m_ref[g], NT, preferred_element_type=f32)
                         + lax.dot_general(dv, pin_ref[g], NT, preferred_element_type=f32)).astype(bf16)
            dpin_ref[g] = lax.dot_general(u_ref[g], dv, TN, preferred_element_type=f32)

    seq = pl.BlockSpec((GB, nc, n), lambda i: (i, 0, 0))
    mat = pl.BlockSpec((GB, n, n), lambda i: (i, 0, 0))
    cf = pl.BlockSpec((None, 1, GB * 2 * NSTATE), lambda i: (i, 0, 0))
    scr = pltpu.VMEM((nc, GB * 2 * NSTATE), f32)
    mat_s = jax.ShapeDtypeStruct((GROUPS, n, n), f32)
    cf_s = jax.ShapeDtypeStruct((GROUPS // GB, 1, GB * 2 * NSTATE), f32)
    return pl.pallas_call(
        body, grid=(GROUPS // GB,), in_specs=[seq, seq, mat, mat, mat, cf, cf, cf, cf],
        out_specs=[seq, mat, mat, mat, cf, cf, cf, cf],
        out_shape=[jax.ShapeDtypeStruct((GROUPS, nc, n), bf16), mat_s, mat_s, mat_s, cf_s, cf_s, cf_s, cf_s],
        scratch_shapes=[scr] * 6, name="s5_core_bwd",
        compiler_params=_params(("parallel",)))(ug, dyg, msum, pin, pout, *coefs)


def _block_movers():
    a_in, l_in = jnp.divmod(jnp.arange(GB * 128, dtype=jnp.int32), 128)
    a_out, c_out = jnp.divmod(jnp.arange(128, dtype=jnp.int32), GCH)
    j = jnp.arange(GB, dtype=jnp.int32)[:, None, None]
    hit = (a_in[None, :, None] == a_out[None, None, :]) & (l_in[None, :, None] == GCH * j + c_out[None, None, :])
    return hit.astype(bf16)


def _to_groups(x, movers, mask):
    n = x.shape[0]
    nc = n // CHUNK
    half = CHUNK // 2

    def body(x_ref, mv_ref, o_ref):
        keep = lax.broadcasted_iota(jnp.int32, (nc, 1), 0) >= PAD // CHUNK
        steps = [x_ref[pl.ds(t, nc, stride=CHUNK), :] for t in range(CHUNK)]
        if mask:
            steps = [jnp.where(keep, s, 0.0) for s in steps]
        lo = jnp.concatenate(steps[:half], axis=1).astype(bf16)
        hi = jnp.concatenate(steps[half:], axis=1).astype(bf16)
        for g in range(GB):
            o_ref[g] = jnp.concatenate([jnp.dot(lo, mv_ref[g], preferred_element_type=f32),
                                        jnp.dot(hi, mv_ref[g], preferred_element_type=f32)], axis=1).astype(bf16)

    return pl.pallas_call(
        body, grid=(GROUPS // GB,),
        in_specs=[pl.BlockSpec((n, 128), lambda i: (0, i)), pl.BlockSpec(movers.shape, lambda i: (0, 0, 0))],
        out_specs=pl.BlockSpec((GB, nc, CHUNK * GCH), lambda i: (i, 0, 0)),
        out_shape=jax.ShapeDtypeStruct((GROUPS, nc, CHUNK * GCH), bf16), name="s5_to_groups",
        compiler_params=_params(("parallel",)))(x, movers)


def _from_groups(y, movers, base=None):
    nc = y.shape[1]
    n = nc * CHUNK
    half = CHUNK // 2

    def body(*refs):
        y_ref, mv_ref = refs[:2]
        o_ref = refs[-1]
        keep = lax.broadcasted_iota(jnp.int32, (nc, 1), 0) >= PAD // CHUNK
        lo = jnp.concatenate([y_ref[g][:, :128] for g in range(GB)], axis=1)
        hi = jnp.concatenate([y_ref[g][:, 128:] for g in range(GB)], axis=1)
        for t in range(CHUNK):
            rows = pl.ds(t, nc, stride=CHUNK)
            v = jnp.dot(lo if t < half else hi, mv_ref[t % half], preferred_element_type=f32)
            if base is not None:
                v = refs[2][rows, :] + jnp.where(keep, v, 0.0)
            o_ref[rows, :] = v

    tok = pl.BlockSpec((n, 128), lambda i: (0, i))
    args = (y, movers) if base is None else (y, movers, base)
    return pl.pallas_call(
        body, grid=(GROUPS // GB,),
        in_specs=[pl.BlockSpec((GB, nc, CHUNK * GCH), lambda i: (i, 0, 0)),
                  pl.BlockSpec(movers.shape, lambda i: (0, 0, 0))] + ([] if base is None else [tok]),
        out_specs=tok, out_shape=jax.ShapeDtypeStruct((n, D), f32), name="s5_from_groups",
        compiler_params=_params(("parallel",)))(*args)


def _gelu(y):
    return 0.5 * y * (1.0 + lax.erf(y * (2.0 ** -0.5)))


def _gelu_grad(y):
    return 0.5 * (1.0 + lax.erf(y * (2.0 ** -0.5))) + y * jnp.exp(-0.5 * y * y) * (1.0 / math.sqrt(2.0 * math.pi))


def _s5_fwd(h, movers, mats, d_skip, w_glu):
    msum, pin, pout, a_re, a_im = mats
    coefs = _s5_coefs(a_re, a_im)
    ug = _to_groups(h, movers, mask=True)
    ys = _from_groups(_s5_core_fwd(ug, msum.astype(bf16), pin.astype(bf16), pout.astype(bf16), coefs), movers)

    def post(i, ys, h, d):
        y = ys + d * h
        return (y, _gelu(y)), ()

    y, g_bf = _ew("s5_gelu", post, [ys, h], [d_skip], [(D, f32), (D, bf16)])
    gw = _mm("s5_glu_mm", g_bf, w_glu)

    def glu(i, y, gw):
        return (_gelu(y) * _sigmoid(gw),), ()

    z_bf = _ew("s5_glu", glu, [y, gw], [], [(D, bf16)])[0]
    return z_bf, (ug, y, g_bf, gw, z_bf)


def _s5_bwd(dmix_bf, h, movers, saved, mats, vjp_mats, d_skip, w_glu, w_out):
    ug, y, g_bf, gw, z_bf = saved
    msum, pin, pout, a_re, a_im = mats
    coefs = _s5_coefs(a_re, a_im)
    dz = _mm("s5_dz_mm", dmix_bf, w_out, trans_b=True)
    d_w_out = _mm_tn("s5_dwout", z_bf, dmix_bf)

    def dglu(i, dz, y, gw):
        g, s = _gelu(y), _sigmoid(gw)
        return (dz * g * s * (1.0 - s), dz * s), ()

    dgw_bf, dg1 = _ew("s5_dglu", dglu, [dz, y, gw], [], [(D, bf16), (D, f32)])
    d_w_glu = _mm_tn("s5_dwglu", g_bf, dgw_bf)
    dg2 = _mm("s5_dg_mm", dgw_bf, w_glu, trans_b=True)

    def dgelu(i, dg1, dg2, y, h, d):
        dy = (dg1 + dg2) * _gelu_grad(y)
        return (dy, dy * d), (jnp.sum(dy * h, axis=0, keepdims=True),)

    dy, dh_skip, dd = _ew("s5_dgelu", dgelu, [dg1, dg2, y, h], [d_skip], [(D, f32), (D, f32)], [(1, D)])
    dug, dm, dpin, dpout, a1f, a2f, a1r, a2r = _s5_core_bwd(
        ug, _to_groups(dy, movers, mask=False), msum.astype(bf16), pin.astype(bf16), pout.astype(bf16), coefs)
    dh = _from_groups(dug, movers, base=dh_skip)
    a1 = jnp.stack([a1f, a1r]).reshape(2, GROUPS, 2 * NSTATE)
    a2 = jnp.stack([a2f, a2r]).reshape(2, GROUPS, 2 * NSTATE)
    da_re = a1[..., :NSTATE] + a1[..., NSTATE:]
    da_im = a2[..., NSTATE:] - a2[..., :NSTATE]
    d_params = vjp_mats((dm, dpin, dpout, da_re, da_im))
    return dh, d_params, dd[0], d_w_glu, d_w_out


def _rope_tables(n):
    row = jnp.arange(n, dtype=jnp.int32) - OFF
    real = row >= 0
    rid = jnp.where(real, row // GRID_W, 0).astype(f32)
    cid = jnp.where(real, row % GRID_W, 0).astype(f32)
    half = HD // 2
    inv = ROPE_THETA ** (-jnp.arange(0, half, 2, dtype=f32) / half)
    ar, ac = rid[:, None] * inv[None, :], cid[:, None] * inv[None, :]
    cos = jnp.concatenate([jnp.cos(ar), jnp.cos(ar), jnp.cos(ac), jnp.cos(ac)], axis=1)
    sin = jnp.concatenate([-jnp.sin(ar), jnp.sin(ar), -jnp.sin(ac), jnp.sin(ac)], axis=1)
    return jnp.tile(cos, (1, 2)), jnp.tile(sin, (1, 2))


def _head_mats():
    head = jnp.arange(QW, dtype=jnp.int32)[:, None] // HD == jnp.arange(128, dtype=jnp.int32)[None, :]
    return head.astype(f32) * (1.0 / HD), head.astype(f32).T


def _rot(v):
    w = v.shape[1]
    lane = lax.broadcasted_iota(jnp.int32, v.shape, 1)
    return jnp.where(lane % 32 < 16, pltpu.roll(v, w - 16, 1), pltpu.roll(v, 16, 1))


def _head_mean(v, e, et):
    w = v.shape[1]
    m = jnp.dot(v, e[:w], preferred_element_type=f32, precision=HIGH)
    return m, et[:, :w]


def _rms_rope(t, gain, e, et, cos, sin):
    w = t.shape[1]
    ms, spread = _head_mean(t * t, e, et)
    rs = jnp.dot(lax.rsqrt(ms + QK_EPS), spread, preferred_element_type=f32, precision=HIGH)
    n0 = t * rs
    n = n0 * gain
    reps = w // 128
    return n * jnp.tile(cos, (1, reps)) + _rot(n) * jnp.tile(sin, (1, reps))


def _rms_rope_bwd(dout, t, gain, e, et, cos, sin):
    w = t.shape[1]
    reps = w // 128
    ms, spread = _head_mean(t * t, e, et)
    rs = jnp.dot(lax.rsqrt(ms + QK_EPS), spread, preferred_element_type=f32, precision=HIGH)
    n0 = t * rs
    dn = dout * jnp.tile(cos, (1, reps)) + _rot(dout * jnp.tile(sin, (1, reps)))
    dn0 = dn * gain
    mm, _ = _head_mean(dn0 * n0, e, et)
    corr = jnp.dot(mm, spread, preferred_element_type=f32, precision=HIGH)
    return rs * (dn0 - n0 * corr), jnp.sum(dn * n0, axis=0, keepdims=True)


def _qk_fwd(qkv, qg, kg, e, et, cos, sin):
    def fn(i, qkv, cos, sin, qg, kg, e, et):
        q = _rms_rope(qkv[:, :QW], qg, e, et, cos, sin) * Q_SCALE
        k = _rms_rope(qkv[:, QW:QW + KW], kg, e, et, cos, sin)
        return (q, k, qkv[:, QW + KW:]), ()

    return _ew("qk_rope", fn, [qkv, cos, sin], [qg, kg, e, et], [(QW, bf16), (KW, bf16), (KW, bf16)])


def _qk_bwd(qkv, dq, dk, dv, qg, kg, e, et, cos, sin):
    def fn(i, qkv, cos, sin, dq, dk, dv, qg, kg, e, et):
        dtq, dgq = _rms_rope_bwd(dq * (HD ** -0.5), qkv[:, :QW], qg, e, et, cos, sin)
        dtk, dgk = _rms_rope_bwd(dk * math.log(2.0), qkv[:, QW:QW + KW], kg, e, et, cos, sin)
        return (jnp.concatenate([dtq, dtk, dv], axis=1),), (dgq, dgk)

    return _ew("qk_rope_bwd", fn, [qkv, cos, sin, dq, dk, dv], [qg, kg, e, et], [(QKVW, bf16)], [(1, QW), (1, KW)])


def _to_heads(a, nh):
    return a.reshape(a.shape[0], nh, HD).transpose(1, 0, 2)


def _from_heads(a):
    return a.transpose(1, 0, 2).reshape(a.shape[1], a.shape[0] * HD)


def _masked_first(s, c):
    if c:
        return s
    col = lax.broadcasted_iota(jnp.int32, (1, s.shape[1]), 1)
    return jnp.where(col >= PAD, s, NEG)


def _flash_fwd(q, k, v1, tq=ROW_TILE, tc=KEY_CHUNK):
    n = q.shape[1]
    nc = n // tc
    pair = 2

    def body(q_ref, k_ref, v_ref, o_ref, ob_ref, lse_ref):
        def scores(h, c):
            ks = k_ref[0, pl.ds(c * tc, tc), :]
            return _masked_first(lax.dot_general(q_ref[h], ks, NT, preferred_element_type=f32), c)

        m = [jnp.full((tq, 1), NEG, f32) for _ in range(pair)]
        acc = [jnp.zeros((tq, 2 * HD), f32) for _ in range(pair)]
        nxt = [scores(h, 0) for h in range(pair)]
        for c in range(nc):
            for h in range(pair):
                s = nxt[h]
                if c + 1 < nc:
                    nxt[h] = scores(h, c + 1)
                m_new = jnp.maximum(m[h], jnp.max(s, axis=1, keepdims=True))
                p = jnp.exp2(s - m_new)
                acc[h] = jnp.exp2(m[h] - m_new) * acc[h] + jnp.dot(p.astype(bf16), v_ref[0, pl.ds(c * tc, tc), :],
                                                                     preferred_element_type=f32)
                m[h] = m_new
        ls = [a[:, HD:HD + 1] for a in acc]
        o = jnp.concatenate([a[:, :HD] / l for a, l in zip(acc, ls)], axis=1)
        o_ref[...] = o
        ob_ref[...] = o.astype(bf16)
        lse_ref[0] = jnp.concatenate([mh + jnp.log2(l) for mh, l in zip(m, ls)], axis=1)

    kv_of = NQ // NKV // pair
    tok = pl.BlockSpec((tq, pair * HD), lambda hp, i: (i, hp))
    return pl.pallas_call(
        body, grid=(NQ // pair, n // tq),
        in_specs=[pl.BlockSpec((pair, tq, HD), lambda hp, i: (hp, i, 0)),
                  pl.BlockSpec((1, n, HD), lambda hp, i: (hp // kv_of, 0, 0)),
                  pl.BlockSpec((1, n, 2 * HD), lambda hp, i: (hp // kv_of, 0, 0))],
        out_specs=[tok, tok, pl.BlockSpec((1, tq, pair), lambda hp, i: (hp, i, 0))],
        out_shape=[jax.ShapeDtypeStruct((n, NQ * HD), f32), jax.ShapeDtypeStruct((n, NQ * HD), bf16),
                   jax.ShapeDtypeStruct((NQ // pair, n, pair), f32)],
        name="flash_fwd", compiler_params=_params(("parallel", "parallel")))(q, k, v1)


def _flash_bwd(q, k, kt, v, do, lse_row, delta_row, tk=ROW_TILE, tc=KEY_CHUNK):
    n = q.shape[1]
    nc = n // tc
    grp = NQ // NKV
    pair = 2

    def body(q_ref, do_ref, lse_ref, delta_ref, k_ref, kt_ref, v_ref, dqt_ref, dk_ref, dv_ref):
        j, g = pl.program_id(1), pl.program_id(2)
        kb, vb, ktb = k_ref[0], v_ref[0], kt_ref[0]
        valid = lax.broadcasted_iota(jnp.int32, (tk, 1), 0) + j * tk >= PAD

        @pl.when(j == 0)
        def _():
            for h in range(pair):
                dqt_ref[pair * g + h] = jnp.zeros((HD, n), f32)

        def products(h, c):
            rows = pl.ds(c * tc, tc)
            return (lax.dot_general(kb, q_ref[h, rows, :], NT, preferred_element_type=f32),
                    lax.dot_general(vb, do_ref[h, rows, :], NT, preferred_element_type=f32))

        dk = jnp.zeros((tk, HD), f32)
        dv = jnp.zeros((tk, HD), f32)
        nxt = [products(h, 0) for h in range(pair)]
        for c in range(nc):
            rows = pl.ds(c * tc, tc)
            for h in range(pair):
                st, dpt = nxt[h]
                if c + 1 < nc:
                    nxt[h] = products(h, c + 1)
                pt = jnp.exp2(jnp.where(valid, st, NEG) - lse_ref[h, :, rows])
                dv = dv + jnp.dot(pt.astype(bf16), do_ref[h, rows, :], preferred_element_type=f32)
                dst = (pt * (dpt - delta_ref[h, :, rows])).astype(bf16)
                dk = dk + jnp.dot(dst, q_ref[h, rows, :], preferred_element_type=f32)
                dqt_ref[pair * g + h, :, rows] += jnp.dot(ktb, dst, preferred_element_type=f32)

        @pl.when(g == 0)
        def _():
            dk_ref[0] = dk
            dv_ref[0] = dv

        @pl.when(g > 0)
        def _():
            dk_ref[0] += dk
            dv_ref[0] += dv

    steps = grp // pair
    hspec = pl.BlockSpec((pair, n, HD), lambda h, j, g: (h * steps + g, 0, 0))
    rspec = pl.BlockSpec((pair, 1, n), lambda h, j, g: (h * steps + g, 0, 0))
    kspec = pl.BlockSpec((1, tk, HD), lambda h, j, g: (h, j, 0))
    return pl.pallas_call(
        body, grid=(NKV, n // tk, steps),
        in_specs=[hspec, hspec, rspec, rspec, kspec, pl.BlockSpec((1, HD, tk), lambda h, j, g: (h, 0, j)), kspec],
        out_specs=[pl.BlockSpec((grp, HD, n), lambda h, j, g: (h, 0, 0)), kspec, kspec],
        out_shape=[jax.ShapeDtypeStruct((NQ, HD, n), f32)] + [jax.ShapeDtypeStruct((NKV, n, HD), f32)] * 2,
        name="flash_bwd", compiler_params=_params(("parallel", "arbitrary", "arbitrary")))(
            q, do, lse_row, delta_row, k, kt, v)


def _attn_fwd(h_bf, w_qkv_t, qg, kg, tabs):
    e, et, cos, sin = tabs
    qkv = _mm("attn_qkv_mm", h_bf, w_qkv_t, trans_b=True)
    q_bf, k_bf, v_bf = _qk_fwd(qkv, qg, kg, e, et, cos, sin)
    q16, k4, v4 = _to_heads(q_bf, NQ), _to_heads(k_bf, NKV), _to_heads(v_bf, NKV)
    ones = jnp.zeros((NKV, v4.shape[1], HD), bf16).at[:, :, 0].set(1.0)
    o, o_bf, lse = _flash_fwd(q16, k4, jnp.concatenate([v4, ones], axis=2))
    lse_row = lse.transpose(0, 2, 1).reshape(NQ, 1, qkv.shape[0])
    return o_bf, (qkv, q16, k4, v4, o, lse_row, o_bf)


def _attn_bwd(dmix_bf, h_bf, saved, w_qkv_t, qg, kg, w_out, tabs):
    e, et, cos, sin = tabs
    qkv, q16, k4, v4, o, lse_row, o_bf = saved
    n = qkv.shape[0]
    do = _mm("attn_do_mm", dmix_bf, w_out, trans_b=True, out_dtype=bf16)
    d_w_out = _mm_tn("attn_dwout", o_bf, dmix_bf)

    def head_dots(i, do, o, e):
        return (jnp.dot(do.astype(f32) * o, e, preferred_element_type=f32, precision=HIGH) * HD,), ()

    delta = _ew("attn_delta", head_dots, [do, o], [e], [(128, f32)])[0]
    dqt, dk4, dv4 = _flash_bwd(q16, k4, k4.transpose(0, 2, 1), v4, _to_heads(do, NQ), lse_row,
                               delta[:, :NQ].T.reshape(NQ, 1, n))
    dq = dqt.transpose(2, 0, 1).reshape(n, QW)
    dqkv_bf, dgq, dgk = _qk_bwd(qkv, dq, _from_heads(dk4), _from_heads(dv4), qg, kg, e, et, cos, sin)
    d_w_qkv_t = _mm_tn("attn_dwqkv", dqkv_bf, h_bf)
    dh = _mm("attn_dh_mm", dqkv_bf, w_qkv_t)
    return dh, d_w_qkv_t, dgq.reshape(NQ, HD).sum(0), dgk.reshape(NKV, HD).sum(0), d_w_out


def _all_gather(name, shard):
    def body(x_ref, out_ref, send_sems, recv_sems, local_sem):
        x, y, c = lax.axis_index("x"), lax.axis_index("y"), lax.axis_index("c")
        me, sibling = (x, y, c), (x, y, 1 - c)
        chips = [(1 - x, y), (x, 1 - y), (1 - x, 1 - y)]

        def slot(px, py, pc):
            return out_ref.at[4 * px + 2 * py + pc]

        def copy(k, block, to, src=None):
            return pltpu.make_async_remote_copy(
                src_ref=slot(*block) if src is None else src, dst_ref=slot(*block),
                send_sem=send_sems.at[k], recv_sem=recv_sems.at[k], device_id=to, device_id_type=MESH)

        mine = pltpu.make_async_copy(x_ref, slot(*me), local_sem)
        mine.start()
        first = [copy(0, me, sibling, src=x_ref)]
        first += [copy(1 + j, me, (*chip, c), src=x_ref) for j, chip in enumerate(chips)]
        for cp in first:
            cp.start()
        passed = [copy(4 + j, (*chip, c), sibling) for j, chip in enumerate(chips)]
        for j, chip in enumerate(chips):
            copy(1 + j, (*chip, c), me).wait_recv()
            passed[j].start()
        copy(0, sibling, me).wait_recv()
        for j, chip in enumerate(chips):
            copy(4 + j, (*chip, 1 - c), me).wait_recv()
        for cp in first + passed:
            cp.wait_send()
        mine.wait()

    return pl.pallas_call(
        body, out_shape=jax.ShapeDtypeStruct((8,) + shard.shape, shard.dtype), in_specs=[ANY], out_specs=ANY,
        scratch_shapes=[pltpu.SemaphoreType.DMA((7,)), pltpu.SemaphoreType.DMA((7,)), pltpu.SemaphoreType.DMA],
        name=name)(shard)


def _swap_sibling(name, theirs):
    k = len(theirs)

    def body(*refs):
        src, dst, send_sems, recv_sems = refs[:k], refs[k:2 * k], refs[2 * k], refs[2 * k + 1]
        x, y, c = lax.axis_index("x"), lax.axis_index("y"), lax.axis_index("c")
        copies = [pltpu.make_async_remote_copy(src_ref=src[j], dst_ref=dst[j], send_sem=send_sems.at[j],
                                               recv_sem=recv_sems.at[j], device_id=(x, y, 1 - c), device_id_type=MESH)
                  for j in range(k)]
        for cp in copies:
            cp.start()
        for cp in copies:
            cp.wait()

    return pl.pallas_call(
        body, out_shape=[jax.ShapeDtypeStruct(a.shape, a.dtype) for a in theirs], in_specs=[ANY] * k,
        out_specs=[ANY] * k, scratch_shapes=[pltpu.SemaphoreType.DMA((k,)), pltpu.SemaphoreType.DMA((k,))],
        name=name)(*theirs)


def _exchange_chips(name, parts):
    k = len(parts)

    def body(*refs):
        p_refs, t_refs = refs[:k], refs[k:2 * k]
        send_sems, recv_sems, local_sems = refs[2 * k:]
        x, y, c = lax.axis_index("x"), lax.axis_index("y"), lax.axis_index("c")
        q = 2 * x + y
        copies = []
        for j in range(k):
            copies.append(pltpu.make_async_copy(p_refs[j].at[q], t_refs[j].at[q], local_sems.at[j]))
            for hop in (1, 2, 3):
                tx, ty = x ^ (hop >> 1), y ^ (hop & 1)
                copies.append(pltpu.make_async_remote_copy(
                    src_ref=p_refs[j].at[2 * tx + ty], dst_ref=t_refs[j].at[q], send_sem=send_sems.at[3 * j + hop - 1],
                    recv_sem=recv_sems.at[3 * j + hop - 1], device_id=(tx, ty, c), device_id_type=MESH))
        for cp in copies:
            cp.start()
        for cp in copies:
            cp.wait()

    return pl.pallas_call(
        body, out_shape=[jax.ShapeDtypeStruct(a.shape, a.dtype) for a in parts], in_specs=[ANY] * k,
        out_specs=[ANY] * k,
        scratch_shapes=[pltpu.SemaphoreType.DMA((3 * k,)), pltpu.SemaphoreType.DMA((3 * k,)),
                        pltpu.SemaphoreType.DMA((k,))],
        name=name)(*parts)


def _reduce_scatter(mine, theirs):
    got = _swap_sibling("rs_sibling", list(theirs))
    parts = []
    for a, b, dt, nm in zip(mine, got, (bf16, f32), ("rs_add2", "rs_add2_small")):
        rows = 4 * a.shape[1]
        parts.append(_ew(nm, lambda i, a, b: ((a + b,), ()), [a.reshape(rows, D), b.reshape(rows, D)], [],
                         [(D, dt)], tile=RS_TILE)[0].reshape(a.shape))
    ts = _exchange_chips("rs_chips", parts)

    def add4(i, a, b, c, d):
        return ((((a.astype(f32) + b.astype(f32)) + c.astype(f32)) + d.astype(f32),), ())

    return [_ew(nm, add4, [(t, 0), (t, 1), (t, 2), (t, 3)], [], [(D, f32)], tile=RS_TILE)[0]
            for t, nm in zip(ts, ("rs_add4", "rs_add4_small"))]


def _pack_rows(parts, rows):
    flat = jnp.concatenate([p.reshape(-1) for p in parts])
    return jnp.pad(flat, (0, rows * D - flat.shape[0])).reshape(rows, D)


def _unpack(flat, shapes):
    out, off = [], 0
    for s in shapes:
        n = math.prod(s)
        out.append(flat[off:off + n].reshape(s))
        off += n
    return out


def _mat_rows(block, transposed, blk):
    a = jnp.swapaxes(block, 1, 2) if transposed else block
    a = jnp.pad(a, ((0, 0), (0, blk - a.shape[1]), (0, 0)))
    return a.reshape(-1, D)


def _mat_block(rows, transposed, blk, real):
    a = rows.reshape(-1, blk, D)[:, :real]
    return jnp.swapaxes(a, 1, 2) if transposed else a


def _mat_full(gathered, blk):
    layers = gathered.shape[1] // blk
    return gathered.reshape(8, layers, blk, D).transpose(1, 0, 2, 3).reshape(layers, 8 * blk, D)


def _vec_full(gathered):
    return gathered.transpose(1, 0, 2).reshape(gathered.shape[1], D)


def _grad_slots(full, small, cc):
    def halves(a, blk):
        a4 = a.reshape(4, 2, blk, D)
        return [lax.dynamic_index_in_dim(a4, sel, axis=1, keepdims=False) for sel in (cc, 1 - cc)]

    mine, theirs = [], []
    for name, _, blk, _ in MATS:
        for layer in full[name]:
            a, b = halves(layer, blk)
            mine.append(a)
            theirs.append(b)
    a, b = halves(small, REP_PIECE)
    return (jnp.concatenate(mine, axis=1), a), (jnp.concatenate(theirs, axis=1), b)


def _local_step(x0, target0, w, fw):
    seq = x0.shape[0]
    n = OFF + seq
    movers = _block_movers()
    h = jnp.concatenate([jnp.zeros((PAD, D), f32), fw['meta_tokens'], x0], axis=0)
    h_bf = h.astype(bf16)
    tabs = _head_mats() + _rope_tables(n)
    qg = [jnp.tile(w['attn_q_gain'][j], NQ)[None, :] for j in range(2)]
    kg = [jnp.tile(w['attn_k_gain'][j], NKV)[None, :] for j in range(2)]
    s5_names = ['s5_lambda_re', 's5_lambda_im', 's5_log_dt', 's5_b_re', 's5_b_im', 's5_c_re', 's5_c_im']
    s5_mats, s5_vjp = [], []
    for j in range(2):
        mats, vjp = jax.vjp(_s5_mats, *[w[k][j] for k in s5_names])
        s5_mats.append(mats)
        s5_vjp.append(vjp)
    saved = []
    for i in range(DEPTH):
        j = i // 2
        if i % 2 == 0:
            mixed, sv = _s5_fwd(h, movers, s5_mats[j], w['s5_d'][j][None, :], (fw['s5_w_glu'], j))
            w_out = (fw['s5_w_out'], j)
        else:
            mixed, sv = _attn_fwd(h_bf, (fw['attn_w_qkv'], j), qg[j], kg[j], tabs)
            w_out = (fw['attn_w_out'], j)
        r1, h1, h1_bf = _mm_ln("mixer_out_ln", mixed, w_out, h, fw['ln_gain'][i, 0][None, :], fw['ln_bias'][i, 0][None, :])
        gate, up, act = _ffn_up(h1_bf, (fw['ffn_w_gate'], i), (fw['ffn_w_up'], i))
        r2, h2, h2_bf = _mm_ln("ffn_down_ln", act, (fw['ffn_w_down'], i), h1, fw['ln_gain'][i, 1][None, :],
                               fw['ln_bias'][i, 1][None, :])
        saved.append((h, h_bf, sv, r1, h1_bf, gate, up, act, r2))
        h, h_bf = h2, h2_bf

    d_b, sq = _loss_grad(h, target0)
    loss = 0.5 * jnp.sum(sq) * (1.0 / D)

    grads = {k: [None] * (DEPTH if k.startswith('ffn') else 2) for k in WEIGHTS}
    d_ln_gain = [[None, None] for _ in range(DEPTH)]
    d_ln_bias = [[None, None] for _ in range(DEPTH)]
    d_a = None
    for i in reversed(range(DEPTH)):
        j = i // 2
        h_in, h_in_bf, sv, r1, h1_bf, gate, up, act, r2 = saved[i]
        dr2, dr2_bf, dg, db = _ln_bwd(d_a, d_b, r2, fw['ln_gain'][i, 1][None, :])
        d_ln_gain[i][1], d_ln_bias[i][1] = dg[0], db[0]
        dgate, dup = _ffn_dup(dr2_bf, (fw['ffn_w_down'], i), gate, up)
        grads['ffn_w_down'][i] = _mm_tn("ffn_dwdown", act, dr2_bf, tk=DFFP // 2)
        grads['ffn_w_gate'][i] = _mm_tn("ffn_dwgate", dgate, h1_bf, tk=DFFP // 2)
        grads['ffn_w_up'][i] = _mm_tn("ffn_dwup", dup, h1_bf, tk=DFFP // 2)
        dh1 = _mm2("ffn_dh_mm", dgate, (fw['ffn_w_gate'], i), dup, (fw['ffn_w_up'], i))
        dr1, dr1_bf, dg, db = _ln_bwd(dr2, dh1, r1, fw['ln_gain'][i, 0][None, :])
        d_ln_gain[i][0], d_ln_bias[i][0] = dg[0], db[0]
        if i % 2 == 0:
            dh, d_par, dd, d_w_glu, d_w_out = _s5_bwd(dr1_bf, h_in, movers, sv, s5_mats[j], s5_vjp[j],
                                                      w['s5_d'][j][None, :], (fw['s5_w_glu'], j), (fw['s5_w_out'], j))
            for k, g in zip(s5_names, d_par):
                grads[k][j] = g
            grads['s5_d'][j], grads['s5_w_glu'][j], grads['s5_w_out'][j] = dd, d_w_glu, d_w_out
        else:
            dh, d_w_qkv, dgq, dgk, d_w_out = _attn_bwd(dr1_bf, h_in_bf, sv, (fw['attn_w_qkv'], j), qg[j], kg[j],
                                                       (fw['attn_w_out'], j), tabs)
            grads['attn_w_qkv'][j], grads['attn_w_out'][j] = d_w_qkv, d_w_out
            grads['attn_q_gain'][j], grads['attn_k_gain'][j] = dgq, dgk
        d_a, d_b = dr1, dh
    dh0 = _ew("dh0", lambda i, a, b: ((ALPHA * a + b,), ()), [d_a, d_b], [], [(D, f32)])[0]
    mats = {m[0] for m in MATS}
    full = {k: (v if k in mats else jnp.stack(v)) for k, v in grads.items() if v[0] is not None}
    full['meta_tokens'] = dh0[PAD:OFF]
    full['ln_gain'] = jnp.stack([jnp.stack(r) for r in d_ln_gain])
    full['ln_bias'] = jnp.stack([jnp.stack(r) for r in d_ln_bias])

    return loss, dh0[OFF:], full


def kernel(x, meta_tokens, s5_lambda_re, s5_lambda_im, s5_log_dt, s5_b_re, s5_b_im, s5_c_re, s5_c_im, s5_d, s5_w_glu, s5_w_out, attn_w_qkv, attn_q_gain, attn_k_gain, attn_w_out, ffn_w_gate, ffn_w_up, ffn_w_down, ln_gain, ln_bias, loss_target, m_meta_tokens, m_s5_lambda_re, m_s5_lambda_im, m_s5_log_dt, m_s5_b_re, m_s5_b_im, m_s5_c_re, m_s5_c_im, m_s5_d, m_s5_w_glu, m_s5_w_out, m_attn_w_qkv, m_attn_q_gain, m_attn_k_gain, m_attn_w_out, m_ffn_w_gate, m_ffn_w_up, m_ffn_w_down, m_ln_gain, m_ln_bias, v_meta_tokens, v_s5_lambda_re, v_s5_lambda_im, v_s5_log_dt, v_s5_b_re, v_s5_b_im, v_s5_c_re, v_s5_c_im, v_s5_d, v_s5_w_glu, v_s5_w_out, v_attn_w_qkv, v_attn_q_gain, v_attn_k_gain, v_attn_w_out, v_ffn_w_gate, v_ffn_w_up, v_ffn_w_down, v_ln_gain, v_ln_bias):
    w = dict(zip(WEIGHTS, (meta_tokens, s5_lambda_re, s5_lambda_im, s5_log_dt, s5_b_re, s5_b_im, s5_c_re, s5_c_im, s5_d, s5_w_glu, s5_w_out, attn_w_qkv, attn_q_gain, attn_k_gain, attn_w_out, ffn_w_gate, ffn_w_up, ffn_w_down, ln_gain, ln_bias)))
    mom = dict(zip(WEIGHTS, (m_meta_tokens, m_s5_lambda_re, m_s5_lambda_im, m_s5_log_dt, m_s5_b_re, m_s5_b_im, m_s5_c_re, m_s5_c_im, m_s5_d, m_s5_w_glu, m_s5_w_out, m_attn_w_qkv, m_attn_q_gain, m_attn_k_gain, m_attn_w_out, m_ffn_w_gate, m_ffn_w_up, m_ffn_w_down, m_ln_gain, m_ln_bias)))
    vel = dict(zip(WEIGHTS, (v_meta_tokens, v_s5_lambda_re, v_s5_lambda_im, v_s5_log_dt, v_s5_b_re, v_s5_b_im, v_s5_c_re, v_s5_c_im, v_s5_d, v_s5_w_glu, v_s5_w_out, v_attn_w_qkv, v_attn_q_gain, v_attn_k_gain, v_attn_w_out, v_ffn_w_gate, v_ffn_w_up, v_ffn_w_down, v_ln_gain, v_ln_bias)))
    cc = lax.axis_index("c")
    dev = 4 * lax.axis_index("x") + 2 * lax.axis_index("y") + cc

    mat_rows = jnp.concatenate([_mat_rows(w[n], t, blk) for n, t, blk, _ in MATS]).astype(bf16)
    g_mats = _all_gather("ag_weights", mat_rows)
    g_vecs = _all_gather("ag_vectors", jnp.concatenate([w[n].reshape(-1, 128) for n in VECS]))
    fw, off = {}, 0
    for n, _, blk, _ in MATS:
        rows = w[n].shape[0] * blk
        fw[n] = _mat_full(g_mats[:, off:off + rows], blk)
        off += rows
    off = 0
    for n in VECS:
        rows = w[n].size // 128
        fw[n] = _vec_full(g_vecs[:, off:off + rows]).reshape(w[n].shape[:-1] + (D,))
        off += rows

    loss, grad_x, full = _local_step(x[0], loss_target[0], w, fw)
    loss = lax.psum(loss, AXES)
    grad_x = grad_x[None]

    small_names = REPL + VECS
    mine, theirs = _grad_slots(full, _pack_rows([full[k] for k in small_names], REP_ROWS), cc)
    red, red_small = _reduce_scatter(mine, theirs)
    small_all = _all_gather("ag_small_grads", red_small).reshape(REP_ROWS * D)
    g, off = {}, 0
    for n, t, blk, real in MATS:
        rows = w[n].shape[0] * blk
        g[n] = _mat_block(red[off:off + rows], t, blk, real)
        off += rows
    small = dict(zip(small_names, _unpack(small_all, [full[k].shape for k in small_names])))
    for k in REPL:
        g[k] = small[k]
    for k in VECS:
        g[k] = lax.dynamic_slice_in_dim(small[k], dev * 128, 128, axis=small[k].ndim - 1)

    delta, new_m, new_v = {}, {}, {}
    for n in WEIGHTS:
        shp = w[n].shape
        res = _adamw(*[d[n].reshape(-1, shp[-1]) for d in (w, g, mom, vel)])
        delta[n], new_m[n], new_v[n] = [a.reshape(shp) for a in res]
    return (loss, grad_x, *[g[k] for k in WEIGHTS], *[delta[k] for k in WEIGHTS],
            *[new_m[k] for k in WEIGHTS], *[new_v[k] for k in WEIGHTS])
```

```python
import functools
import math

import jax
import jax.numpy as jnp
from jax import lax
from jax.experimental import pallas as pl
from jax.experimental.pallas import tpu as pltpu

f32 = jnp.float32
bf16 = jnp.bfloat16
HIGH = lax.Precision.HIGH
MESH = pl.DeviceIdType.MESH
AXES = ("x", "y", "c")
ANY = pl.BlockSpec(memory_space=pl.ANY)

D = 1024
DEPTH = 4
N_META = 16
PAD = 240
OFF = PAD + N_META
ROW_TILE = 768
KEY_CHUNK = 256
FFN_TILE = 256
ADAM_TILE = 544
GRID_W = 64
HD = 64
NQ = 16
NKV = 4
QW = NQ * HD
KW = NKV * HD
QKVW = QW + 2 * KW
DFF = 2816
GROUPS = 64
GCH = 16
NSTATE = 64
CHUNK = 16
GB = 8
ROPE_THETA = 10000.0
LN_EPS = 1e-5
QK_EPS = 1e-6
ALPHA = (2.0 * DEPTH) ** 0.25
ADAM_LR, ADAM_B1, ADAM_B2, ADAM_EPS, ADAM_WD, ADAM_STEP = 0.001, 0.9, 0.999, 1e-08, 0.01, 10
NEG = -1e30
Q_SCALE = HD ** -0.5 * math.log2(math.e)
VMEM_MB = 56

NT = (((1,), (1,)), ((), ()))
TN = (((0,), (0,)), ((), ()))

WEIGHTS = ['meta_tokens', 's5_lambda_re', 's5_lambda_im', 's5_log_dt', 's5_b_re', 's5_b_im', 's5_c_re', 's5_c_im',
           's5_d', 's5_w_glu', 's5_w_out', 'attn_w_qkv', 'attn_q_gain', 'attn_k_gain', 'attn_w_out', 'ffn_w_gate',
           'ffn_w_up', 'ffn_w_down', 'ln_gain', 'ln_bias']
DFFP = DFF
FF_BLK, FF_BLKP = DFF // 8, DFFP // 8
MATS = [('s5_w_glu', False, 128, 128), ('s5_w_out', False, 128, 128), ('attn_w_qkv', True, 192, 192),
        ('attn_w_out', False, 128, 128), ('ffn_w_gate', True, FF_BLKP, FF_BLK), ('ffn_w_up', True, FF_BLKP, FF_BLK),
        ('ffn_w_down', False, FF_BLKP, FF_BLK)]
VECS = ['meta_tokens', 'ln_gain', 'ln_bias']
REPL = ['s5_lambda_re', 's5_lambda_im', 's5_log_dt', 's5_b_re', 's5_b_im', 's5_c_re', 's5_c_im', 's5_d',
        'attn_q_gain', 'attn_k_gain']
MAT_ROWS = 3 * 2 * 128 + 2 * 192 + 3 * DEPTH * FF_BLKP
REP_PIECE = 160
REP_ROWS = 8 * REP_PIECE
RS_TILE = 768


def _params(sem, mb=VMEM_MB):
    return pltpu.CompilerParams(dimension_semantics=sem, vmem_limit_bytes=mb << 20)


def _ew(name, fn, rows, consts, outs, accs=(), tile=ROW_TILE):
    first = rows[0][0] if isinstance(rows[0], tuple) else rows[0]
    n = first.shape[-2]
    tile = min(tile, n)
    assert n % tile == 0, (name, n, tile)
    n_in, n_o, n_a = len(rows) + len(consts), len(outs), len(accs)

    def body(*refs):
        i = pl.program_id(0)
        res_o, res_a = fn(i, *[r[...] for r in refs[:n_in]])
        for r, val in zip(refs[n_in:n_in + n_o], res_o):
            r[...] = val.astype(r.dtype)
        if n_a:
            a_refs = refs[n_in + n_o:]

            @pl.when(i == 0)
            def _():
                for r in a_refs:
                    r[...] = jnp.zeros(r.shape, r.dtype)

            for r, val in zip(a_refs, res_a):
                r[...] += val

    in_specs, args = [], []
    for a in rows:
        if isinstance(a, tuple):
            arr, k = a
            in_specs.append(pl.BlockSpec((None, tile, arr.shape[2]), functools.partial(lambda i, k: (k, i, 0), k=k)))
            args.append(arr)
        else:
            in_specs.append(pl.BlockSpec((tile, a.shape[1]), lambda i: (i, 0)))
            args.append(a)
    for c in consts:
        in_specs.append(pl.BlockSpec(c.shape, lambda i: (0, 0)))
        args.append(c)
    out_specs = [pl.BlockSpec((tile, c), lambda i: (i, 0)) for c, _ in outs]
    out_specs += [pl.BlockSpec(s, lambda i: (0, 0)) for s in accs]
    out_shape = [jax.ShapeDtypeStruct((n, c), dt) for c, dt in outs]
    out_shape += [jax.ShapeDtypeStruct(s, f32) for s in accs]
    res = pl.pallas_call(body, grid=(n // tile,), in_specs=in_specs, out_specs=out_specs, out_shape=out_shape,
                         name=name, compiler_params=_params(("arbitrary",)))(*args)
    return res


def _mm(name, a, b, trans_b=False, out_dtype=f32, tm=ROW_TILE):
    m, k = a.shape
    spec, b, shape = _whole(b)
    n = shape[0] if trans_b else shape[1]
    tm = min(tm, m)
    assert m % tm == 0
    dims = NT if trans_b else (((1,), (0,)), ((), ()))

    def body(a_ref, b_ref, o_ref):
        o_ref[...] = lax.dot_general(a_ref[...], b_ref[...], dims, preferred_element_type=f32).astype(o_ref.dtype)

    return pl.pallas_call(
        body, grid=(m // tm,), in_specs=[pl.BlockSpec((tm, k), lambda i: (i, 0)), spec],
        out_specs=pl.BlockSpec((tm, n), lambda i: (i, 0)),
        out_shape=jax.ShapeDtypeStruct((m, n), out_dtype), name=name, compiler_params=_params(("parallel",)))(a, b)


def _whole(b):
    if isinstance(b, tuple):
        arr, layer = b
        return pl.BlockSpec((None,) + arr.shape[1:], lambda i: (layer, 0, 0)), arr, arr.shape[1:]
    return pl.BlockSpec(b.shape, lambda i: (0, 0)), b, b.shape


def _mm2(name, a1, b1, a2, b2, out_dtype=f32, tm=ROW_TILE // 2):
    m, k = a1.shape
    spec1, b1, shape = _whole(b1)
    spec2, b2, _ = _whole(b2)
    n = shape[1]
    tm = min(tm, m)
    assert m % tm == 0

    def body(a1_ref, b1_ref, a2_ref, b2_ref, o_ref):
        acc = jnp.dot(a1_ref[...], b1_ref[...], preferred_element_type=f32)
        acc += jnp.dot(a2_ref[...], b2_ref[...], preferred_element_type=f32)
        o_ref[...] = acc.astype(o_ref.dtype)

    row = pl.BlockSpec((tm, k), lambda i: (i, 0))
    return pl.pallas_call(
        body, grid=(m // tm,), in_specs=[row, spec1, row, spec2], out_specs=pl.BlockSpec((tm, n), lambda i: (i, 0)),
        out_shape=jax.ShapeDtypeStruct((m, n), out_dtype), name=name,
        compiler_params=_params(("parallel",)))(a1, b1, a2, b2)


def _mm_tn(name, a, g, tk=512, tl=ROW_TILE):
    rows, k1 = a.shape
    n = g.shape[1]
    tl = min(tl, rows)
    assert rows % tl == 0 and k1 % tk == 0

    def body(a_ref, g_ref, o_ref):
        @pl.when(pl.program_id(1) == 0)
        def _():
            o_ref[...] = jnp.zeros(o_ref.shape, f32)

        o_ref[...] += lax.dot_general(a_ref[...], g_ref[...], TN, preferred_element_type=f32)

    return pl.pallas_call(
        body, grid=(k1 // tk, rows // tl),
        in_specs=[pl.BlockSpec((tl, tk), lambda k, l: (l, k)), pl.BlockSpec((tl, n), lambda k, l: (l, 0))],
        out_specs=pl.BlockSpec((tk, n), lambda k, l: (k, 0)),
        out_shape=jax.ShapeDtypeStruct((k1, n), f32), name=name,
        compiler_params=_params(("parallel", "arbitrary")))(a, g)


def _ln_stats(r):
    mean = jnp.mean(r, axis=-1, keepdims=True)
    c = r - mean
    rstd = lax.rsqrt(jnp.mean(c * c, axis=-1, keepdims=True) + LN_EPS)
    return c * rstd, rstd


def _mm_ln(name, a, b, h, gain, bias, tm=ROW_TILE):
    m, k = a.shape
    spec, b, _ = _whole(b)

    def body(a_ref, b_ref, h_ref, g_ref, bias_ref, r_ref, y_ref, yb_ref):
        r = ALPHA * h_ref[...] + jnp.dot(a_ref[...], b_ref[...], preferred_element_type=f32)
        y = _ln_stats(r)[0] * g_ref[...] + bias_ref[...]
        r_ref[...] = r
        y_ref[...] = y
        yb_ref[...] = y.astype(bf16)

    row = pl.BlockSpec((tm, D), lambda i: (i, 0))
    vec = pl.BlockSpec((1, D), lambda i: (0, 0))
    return pl.pallas_call(
        body, grid=(m // tm,), in_specs=[pl.BlockSpec((tm, k), lambda i: (i, 0)), spec, row, vec, vec],
        out_specs=[row, row, row],
        out_shape=[jax.ShapeDtypeStruct((m, D), f32), jax.ShapeDtypeStruct((m, D), f32), jax.ShapeDtypeStruct((m, D), bf16)],
        name=name, compiler_params=_params(("parallel",)))(a, b, h, gain, bias)


def _ln_bwd(d_a, d_b, r, gain):
    def core(dout, r, g):
        xhat, rstd = _ln_stats(r)
        dxh = dout * g
        dr = rstd * (dxh - jnp.mean(dxh, axis=-1, keepdims=True) - xhat * jnp.mean(dxh * xhat, axis=-1, keepdims=True))
        return (dr, dr), (jnp.sum(dout * xhat, axis=0, keepdims=True), jnp.sum(dout, axis=0, keepdims=True))

    outs, accs = [(D, f32), (D, bf16)], [(1, D), (1, D)]
    if d_a is None:
        return _ew("ln_bwd_top", lambda i, d, r, g: core(d, r, g), [d_b, r], [gain], outs, accs)
    return _ew("ln_bwd", lambda i, da, db, r, g: core(ALPHA * da + db, r, g), [d_a, d_b, r], [gain], outs, accs)


def _sigmoid(x):
    return 1.0 / (1.0 + jnp.exp(-x))


def _ffn_up(h_bf, w_gate_t, w_up_t, tm=FFN_TILE):
    m, k = h_bf.shape
    gspec, w_gate_t, (n, _) = _whole(w_gate_t)
    uspec, w_up_t, _ = _whole(w_up_t)

    def body(h_ref, wg_ref, wu_ref, g_ref, u_ref, a_ref):
        h = h_ref[...]
        g = lax.dot_general(h, wg_ref[...], NT, preferred_element_type=f32).astype(bf16)
        u = lax.dot_general(h, wu_ref[...], NT, preferred_element_type=f32).astype(bf16)
        g_ref[...] = g
        u_ref[...] = u
        g = g.astype(f32)
        a_ref[...] = (g * _sigmoid(g) * u.astype(f32)).astype(bf16)

    row = pl.BlockSpec((tm, n), lambda i: (i, 0))
    return pl.pallas_call(
        body, grid=(m // tm,), in_specs=[pl.BlockSpec((tm, k), lambda i: (i, 0)), gspec, uspec],
        out_specs=[row, row, row], out_shape=[jax.ShapeDtypeStruct((m, n), bf16)] * 3, name="ffn_up",
        compiler_params=_params(("parallel",)))(h_bf, w_gate_t, w_up_t)


def _ffn_dup(df_bf, w_down, gate, up, tm=FFN_TILE):
    m, k = df_bf.shape
    wspec, w_down, (n, _) = _whole(w_down)

    def body(d_ref, w_ref, g_ref, u_ref, dg_ref, du_ref):
        da = lax.dot_general(d_ref[...], w_ref[...], NT, preferred_element_type=f32).astype(bf16).astype(f32)
        g, u = g_ref[...].astype(f32), u_ref[...].astype(f32)
        s = _sigmoid(g)
        dg_ref[...] = (da * u * s * (1.0 + g * (1.0 - s))).astype(bf16)
        du_ref[...] = (da * g * s).astype(bf16)

    row = pl.BlockSpec((tm, n), lambda i: (i, 0))
    return pl.pallas_call(
        body, grid=(m // tm,),
        in_specs=[pl.BlockSpec((tm, k), lambda i: (i, 0)), wspec, row, row],
        out_specs=[row, row], out_shape=[jax.ShapeDtypeStruct((m, n), bf16)] * 2, name="ffn_dup",
        compiler_params=_params(("parallel",)))(df_bf, w_down, gate, up)


def _loss_grad(h, target):
    n = h.shape[0]

    def body(h_ref, t_ref, d_ref, sq_ref):
        i = pl.program_id(0)

        @pl.when(i == 0)
        def _():
            d_ref[...] = jnp.zeros(d_ref.shape, f32)
            sq_ref[...] = jnp.zeros(sq_ref.shape, f32)

        @pl.when(i > 0)
        def _():
            e = h_ref[...] - t_ref[...]
            d_ref[...] = e * (1.0 / D)
            sq_ref[...] += jnp.sum(e * e, axis=0, keepdims=True)

    return pl.pallas_call(
        body, grid=(n // OFF,),
        in_specs=[pl.BlockSpec((OFF, D), lambda i: (i, 0)), pl.BlockSpec((OFF, D), lambda i: (jnp.maximum(i - 1, 0), 0))],
        out_specs=[pl.BlockSpec((OFF, D), lambda i: (i, 0)), pl.BlockSpec((1, D), lambda i: (0, 0))],
        out_shape=[jax.ShapeDtypeStruct((n, D), f32), jax.ShapeDtypeStruct((1, D), f32)], name="loss",
        compiler_params=_params(("arbitrary",)))(h, target)


def _adamw(w, g, m, v):
    def fn(i, w, g, m, v):
        m = ADAM_B1 * m + (1.0 - ADAM_B1) * g
        v = ADAM_B2 * v + (1.0 - ADAM_B2) * jnp.square(g)
        m_hat = m / (1.0 - ADAM_B1 ** ADAM_STEP)
        v_hat = v / (1.0 - ADAM_B2 ** ADAM_STEP)
        delta = -ADAM_LR * (m_hat / (jnp.sqrt(v_hat) + ADAM_EPS) + ADAM_WD * w)
        return (delta, m, v), ()

    rows, cols = w.shape
    cap = ADAM_TILE if cols > 128 else 4 * ADAM_TILE
    fits = [t for t in range(8, min(rows, cap) + 1, 8) if rows % t == 0]
    return _ew("adamw", fn, [w, g, m, v], [], [(cols, f32)] * 3, tile=max(fits) if fits else rows)


def _s5_mats(lam_re, lam_im, log_dt, b_re, b_im, c_re, c_im):
    steps = jnp.arange(CHUNK + 1, dtype=f32)
    n = CHUNK * GCH
    last = n - GCH

    def one(lr, li, ldt, br, bi, cr, ci, reverse):
        dt = jnp.exp(ldt)[:, None]
        mag = jnp.exp(lr * dt)
        abr, abi = mag * jnp.cos(li * dt), mag * jnp.sin(li * dt)
        nr, ni = abr - 1.0, abi
        den = lr * lr + li * li
        zr, zi = (nr * lr + ni * li) / den, (ni * lr - nr * li) / den
        bbr = zr[..., None] * br - zi[..., None] * bi
        bbi = zr[..., None] * bi + zi[..., None] * br
        pmag = jnp.exp((lr * dt)[..., None] * steps)
        pang = (li * dt)[..., None] * steps
        pr, pi = pmag * jnp.cos(pang), pmag * jnp.sin(pang)
        crt, cit = jnp.swapaxes(cr, 1, 2)[:, :, None, :], jnp.swapaxes(ci, 1, 2)[:, :, None, :]
        car = crt * pr[..., None] - cit * pi[..., None]
        cai = crt * pi[..., None] + cit * pr[..., None]
        if reverse:
            taps = slice(CHUNK - 1, None, -1)
            outs = slice(CHUNK, 0, -1)
            ins = slice(0, CHUNK)
        else:
            taps, outs, ins = slice(0, CHUNK), slice(1, CHUNK + 1), slice(CHUNK - 1, None, -1)
        kern = (jnp.einsum('gpi,gpq->giq', bbr, car[:, :, taps].reshape(GROUPS, NSTATE, n), precision=HIGH)
                - jnp.einsum('gpi,gpq->giq', bbi, cai[:, :, taps].reshape(GROUPS, NSTATE, n), precision=HIGH))
        wide = jnp.pad(kern, ((0, 0), (0, 0), (0, last) if reverse else (last, 0)))
        m = jnp.stack([wide[:, :, last - GCH * t:last - GCH * t + n] for t in range(CHUNK)], axis=1)
        qr = jnp.swapaxes(pr[:, :, ins], 1, 2)[:, :, None, :]
        qi = jnp.swapaxes(pi[:, :, ins], 1, 2)[:, :, None, :]
        bbrt, bbit = jnp.swapaxes(bbr, 1, 2)[:, None], jnp.swapaxes(bbi, 1, 2)[:, None]
        pin = jnp.concatenate([qr * bbrt - qi * bbit, qr * bbit + qi * bbrt], axis=-1)
        pout = jnp.concatenate([car[:, :, outs].reshape(GROUPS, NSTATE, n),
                                -cai[:, :, outs].reshape(GROUPS, NSTATE, n)], axis=1)
        return (m.reshape(GROUPS, n, n), pin.reshape(GROUPS, n, 2 * NSTATE), pout, pr[:, :, CHUNK], pi[:, :, CHUNK])

    mf, pinf, poutf, arf, aif = one(lam_re[0], lam_im[0], log_dt[0], b_re[0], b_im[0], c_re[0], c_im[0], False)
    mr, pinr, poutr, arr, air = one(lam_re[1], lam_im[1], log_dt[1], b_re[1], b_im[1], c_re[1], c_im[1], True)
    return (mf + mr, jnp.concatenate([pinf, pinr], 2), jnp.concatenate([poutf, poutr], 1),
            jnp.stack([arf, arr]), jnp.stack([aif, air]))


def _s5_coefs(a_re, a_im):
    c1 = jnp.concatenate([a_re, a_re], -1)
    c2 = jnp.concatenate([-a_im, a_im], -1)
    return tuple(c.reshape(GROUPS // GB, 1, GB * 2 * NSTATE) for c in (c1[0], c2[0], c1[1], c2[1]))


def _swap(s):
    w = s.shape[1]
    lane = lax.broadcasted_iota(jnp.int32, s.shape, 1)
    return jnp.where(lane % (2 * NSTATE) < NSTATE, pltpu.roll(s, w - NSTATE, 1), pltpu.roll(s, NSTATE, 1))


def _group_lanes(g):
    return slice(g * 2 * NSTATE, (g + 1) * 2 * NSTATE)


def _s5_states(nc, u_ref, pin_ref, coef, vf, vr, wf, wr, sf, sr):
    c1f, c2f, c1r, c2r = coef
    for g in range(GB):
        v = jnp.dot(u_ref[g], pin_ref[g], preferred_element_type=f32)
        vf[:, _group_lanes(g)] = v[:, :2 * NSTATE]
        vr[:, _group_lanes(g)] = v[:, 2 * NSTATE:]
    wf[...] = _swap(vf[...])
    wr[...] = _swap(vr[...])

    def step(i, carry):
        s_f, t_f, s_r, t_r = carry
        kf, kr = pl.ds(i, 1), pl.ds(nc - 1 - i, 1)
        sf[kf, :] = s_f
        sr[kr, :] = s_r
        s_f, t_f = c1f * s_f + c2f * t_f + vf[kf, :], c1f * t_f - c2f * s_f + wf[kf, :]
        s_r, t_r = c1r * s_r + c2r * t_r + vr[kr, :], c1r * t_r - c2r * s_r + wr[kr, :]
        return s_f, t_f, s_r, t_r

    z = jnp.zeros((1, GB * 2 * NSTATE), f32)
    lax.fori_loop(0, nc, step, (z, z, z, z))


def _s5_core_fwd(ug, msum, pin, pout, coefs):
    nc = ug.shape[1]
    n = CHUNK * GCH

    def body(u_ref, m_ref, pin_ref, pout_ref, c1f, c2f, c1r, c2r, y_ref, vf, vr, wf, wr, sf, sr):
        coef = (c1f[...], c2f[...], c1r[...], c2r[...])
        _s5_states(nc, u_ref, pin_ref, coef, vf, vr, wf, wr, sf, sr)
        for g in range(GB):
            s_in = jnp.concatenate([sf[:, _group_lanes(g)], sr[:, _group_lanes(g)]], axis=1).astype(bf16)
            y_ref[g] = (jnp.dot(u_ref[g], m_ref[g], preferred_element_type=f32)
                        + jnp.dot(s_in, pout_ref[g], preferred_element_type=f32)).astype(bf16)

    seq = pl.BlockSpec((GB, nc, n), lambda i: (i, 0, 0))
    mat = pl.BlockSpec((GB, n, n), lambda i: (i, 0, 0))
    cf = pl.BlockSpec((None, 1, GB * 2 * NSTATE), lambda i: (i, 0, 0))
    scr = pltpu.VMEM((nc, GB * 2 * NSTATE), f32)
    return pl.pallas_call(
        body, grid=(GROUPS // GB,), in_specs=[seq, mat, mat, mat, cf, cf, cf, cf], out_specs=seq,
        out_shape=jax.ShapeDtypeStruct((GROUPS, nc, n), bf16), scratch_shapes=[scr] * 6,
        name="s5_core_fwd", compiler_params=_params(("parallel",)))(ug, msum, pin, pout, *coefs)


def _s5_core_bwd(ug, dyg, msum, pin, pout, coefs):
    nc = ug.shape[1]
    n = CHUNK * GCH

    def body(u_ref, dy_ref, m_ref, pin_ref, pout_ref, c1f, c2f, c1r, c2r,
             du_ref, dm_ref, dpin_ref, dpout_ref, a1f_ref, a2f_ref, a1r_ref, a2r_ref, vf, vr, wf, wr, sf, sr):
        coef = (c1f[...], c2f[...], c1r[...], c2r[...])
        _s5_states(nc, u_ref, pin_ref, coef, vf, vr, wf, wr, sf, sr)
        for g in range(GB):
            s_in = jnp.concatenate([sf[:, _group_lanes(g)], sr[:, _group_lanes(g)]], axis=1).astype(bf16)
            dy = dy_ref[g]
            ds = lax.dot_general(dy, pout_ref[g], NT, preferred_element_type=f32)
            vf[:, _group_lanes(g)] = ds[:, :2 * NSTATE]
            vr[:, _group_lanes(g)] = ds[:, 2 * NSTATE:]
            dpout_ref[g] = lax.dot_general(s_in, dy, TN, preferred_element_type=f32)
            dm_ref[g] = lax.dot_general(u_ref[g], dy, TN, preferred_element_type=f32)

        wf[...] = _swap(vf[...])
        wr[...] = _swap(vr[...])
        k1f, k2f, k1r, k2r = coef[0], -coef[1], coef[2], -coef[3]

        def step(i, carry):
            g_f, h_f, g_r, h_r, a1f, b2f, a1r, b2r = carry
            kf, kr = pl.ds(nc - 1 - i, 1), pl.ds(i, 1)
            s_f, s_r = sf[kf, :], sr[kr, :]
            sf[kf, :] = g_f
            sr[kr, :] = g_r
            a1f, b2f = a1f + g_f * s_f, b2f + h_f * s_f
            a1r, b2r = a1r + g_r * s_r, b2r + h_r * s_r
            g_f, h_f = vf[kf, :] + k1f * g_f + k2f * h_f, wf[kf, :] + k1f * h_f - k2f * g_f
            g_r, h_r = vr[kr, :] + k1r * g_r + k2r * h_r, wr[kr, :] + k1r * h_r - k2r * g_r
            return g_f, h_f, g_r, h_r, a1f, b2f, a1r, b2r

        z = jnp.zeros((1, GB * 2 * NSTATE), f32)
        _, _, _, _, a1f, b2f, a1r, b2r = lax.fori_loop(0, nc, step, (z,) * 8)
        a1f_ref[...], a2f_ref[...], a1r_ref[...], a2r_ref[...] = a1f, _swap(b2f), a1r, _swap(b2r)
        for g in range(GB):
            dv = jnp.concatenate([sf[:, _group_lanes(g)], sr[:, _group_lanes(g)]], axis=1).astype(bf16)
            du_ref[g] = (lax.dot_general(dy_ref[g], m_ref[g], NT, preferred_element_type=f32)
                         + lax.dot_general(dv, pin_ref[g], NT, preferred_element_type=f32)).astype(bf16)
            dpin_ref[g] = lax.dot_general(u_ref[g], dv, TN, preferred_element_type=f32)

    seq = pl.BlockSpec((GB, nc, n), lambda i: (i, 0, 0))
    mat = pl.BlockSpec((GB, n, n), lambda i: (i, 0, 0))
    cf = pl.BlockSpec((None, 1, GB * 2 * NSTATE), lambda i: (i, 0, 0))
    scr = pltpu.VMEM((nc, GB * 2 * NSTATE), f32)
    mat_s = jax.ShapeDtypeStruct((GROUPS, n, n), f32)
    cf_s = jax.ShapeDtypeStruct((GROUPS // GB, 1, GB * 2 * NSTATE), f32)
    return pl.pallas_call(
        body, grid=(GROUPS // GB,), in_specs=[seq, seq, mat, mat, mat, cf, cf, cf, cf],
        out_specs=[seq, mat, mat, mat, cf, cf, cf, cf],
        out_shape=[jax.ShapeDtypeStruct((GROUPS, nc, n), bf16), mat_s, mat_s, mat_s, cf_s, cf_s, cf_s, cf_s],
        scratch_shapes=[scr] * 6, name="s5_core_bwd",
        compiler_params=_params(("parallel",)))(ug, dyg, msum, pin, pout, *coefs)


def _block_movers():
    a_in, l_in = jnp.divmod(jnp.arange(GB * 128, dtype=jnp.int32), 128)
    a_out, c_out = jnp.divmod(jnp.arange(128, dtype=jnp.int32), GCH)
    j = jnp.arange(GB, dtype=jnp.int32)[:, None, None]
    hit = (a_in[None, :, None] == a_out[None, None, :]) & (l_in[None, :, None] == GCH * j + c_out[None, None, :])
    return hit.astype(bf16)


def _to_groups(x, movers, mask):
    n = x.shape[0]
    nc = n // CHUNK
    half = CHUNK // 2

    def body(x_ref, mv_ref, o_ref):
        keep = lax.broadcasted_iota(jnp.int32, (nc, 1), 0) >= PAD // CHUNK
        steps = [x_ref[pl.ds(t, nc, stride=CHUNK), :] for t in range(CHUNK)]
        if mask:
            steps = [jnp.where(keep, s, 0.0) for s in steps]
        lo = jnp.concatenate(steps[:half], axis=1).astype(bf16)
        hi = jnp.concatenate(steps[half:], axis=1).astype(bf16)
        for g in range(GB):
            o_ref[g] = jnp.concatenate([jnp.dot(lo, mv_ref[g], preferred_element_type=f32),
                                        jnp.dot(hi, mv_ref[g], preferred_element_type=f32)], axis=1).astype(bf16)

    return pl.pallas_call(
        body, grid=(GROUPS // GB,),
        in_specs=[pl.BlockSpec((n, 128), lambda i: (0, i)), pl.BlockSpec(movers.shape, lambda i: (0, 0, 0))],
        out_specs=pl.BlockSpec((GB, nc, CHUNK * GCH), lambda i: (i, 0, 0)),
        out_shape=jax.ShapeDtypeStruct((GROUPS, nc, CHUNK * GCH), bf16), name="s5_to_groups",
        compiler_params=_params(("parallel",)))(x, movers)


def _from_groups(y, movers, base=None):
    nc = y.shape[1]
    n = nc * CHUNK
    half = CHUNK // 2

    def body(*refs):
        y_ref, mv_ref = refs[:2]
        o_ref = refs[-1]
        keep = lax.broadcasted_iota(jnp.int32, (nc, 1), 0) >= PAD // CHUNK
        lo = jnp.concatenate([y_ref[g][:, :128] for g in range(GB)], axis=1)
        hi = jnp.concatenate([y_ref[g][:, 128:] for g in range(GB)], axis=1)
        for t in range(CHUNK):
            rows = pl.ds(t, nc, stride=CHUNK)
            v = jnp.dot(lo if t < half else hi, mv_ref[t % half], preferred_element_type=f32)
            if base is not None:
                v = refs[2][rows, :] + jnp.where(keep, v, 0.0)
            o_ref[rows, :] = v

    tok = pl.BlockSpec((n, 128), lambda i: (0, i))
    args = (y, movers) if base is None else (y, movers, base)
    return pl.pallas_call(
        body, grid=(GROUPS // GB,),
        in_specs=[pl.BlockSpec((GB, nc, CHUNK * GCH), lambda i: (i, 0, 0)),
                  pl.BlockSpec(movers.shape, lambda i: (0, 0, 0))] + ([] if base is None else [tok]),
        out_specs=tok, out_shape=jax.ShapeDtypeStruct((n, D), f32), name="s5_from_groups",
        compiler_params=_params(("parallel",)))(*args)


def _gelu(y):
    return 0.5 * y * (1.0 + lax.erf(y * (2.0 ** -0.5)))


def _gelu_grad(y):
    return 0.5 * (1.0 + lax.erf(y * (2.0 ** -0.5))) + y * jnp.exp(-0.5 * y * y) * (1.0 / math.sqrt(2.0 * math.pi))


def _s5_fwd(h, movers, mats, d_skip, w_glu):
    msum, pin, pout, a_re, a_im = mats
    coefs = _s5_coefs(a_re, a_im)
    ug = _to_groups(h, movers, mask=True)
    ys = _from_groups(_s5_core_fwd(ug, msum.astype(bf16), pin.astype(bf16), pout.astype(bf16), coefs), movers)

    def post(i, ys, h, d):
        y = ys + d * h
        return (y, _gelu(y)), ()

    y, g_bf = _ew("s5_gelu", post, [ys, h], [d_skip], [(D, f32), (D, bf16)])
    gw = _mm("s5_glu_mm", g_bf, w_glu)

    def glu(i, y, gw):
        return (_gelu(y) * _sigmoid(gw),), ()

    z_bf = _ew("s5_glu", glu, [y, gw], [], [(D, bf16)])[0]
    return z_bf, (ug, y, g_bf, gw, z_bf)


def _s5_bwd(dmix_bf, h, movers, saved, mats, vjp_mats, d_skip, w_glu, w_out):
    ug, y, g_bf, gw, z_bf = saved
    msum, pin, pout, a_re, a_im = mats
    coefs = _s5_coefs(a_re, a_im)
    dz = _mm("s5_dz_mm", dmix_bf, w_out, trans_b=True)
    d_w_out = _mm_tn("s5_dwout", z_bf, dmix_bf)

    def dglu(i, dz, y, gw):
        g, s = _gelu(y), _sigmoid(gw)
        return (dz * g * s * (1.0 - s), dz * s), ()

    dgw_bf, dg1 = _ew("s5_dglu", dglu, [dz, y, gw], [], [(D, bf16), (D, f32)])
    d_w_glu = _mm_tn("s5_dwglu", g_bf, dgw_bf)
    dg2 = _mm("s5_dg_mm", dgw_bf, w_glu, trans_b=True)

    def dgelu(i, dg1, dg2, y, h, d):
        dy = (dg1 + dg2) * _gelu_grad(y)
        return (dy, dy * d), (jnp.sum(dy * h, axis=0, keepdims=True),)

    dy, dh_skip, dd = _ew("s5_dgelu", dgelu, [dg1, dg2, y, h], [d_skip], [(D, f32), (D, f32)], [(1, D)])
    dug, dm, dpin, dpout, a1f, a2f, a1r, a2r = _s5_core_bwd(
        ug, _to_groups(dy, movers, mask=False), msum.astype(bf16), pin.astype(bf16), pout.astype(bf16), coefs)
    dh = _from_groups(dug, movers, base=dh_skip)
    a1 = jnp.stack([a1f, a1r]).reshape(2, GROUPS, 2 * NSTATE)
    a2 = jnp.stack([a2f, a2r]).reshape(2, GROUPS, 2 * NSTATE)
    da_re = a1[..., :NSTATE] + a1[..., NSTATE:]
    da_im = a2[..., NSTATE:] - a2[..., :NSTATE]
    d_params = vjp_mats((dm, dpin, dpout, da_re, da_im))
    return dh, d_params, dd[0], d_w_glu, d_w_out


def _rope_tables(n):
    row = jnp.arange(n, dtype=jnp.int32) - OFF
    real = row >= 0
    rid = jnp.where(real, row // GRID_W, 0).astype(f32)
    cid = jnp.where(real, row % GRID_W, 0).astype(f32)
    half = HD // 2
    inv = ROPE_THETA ** (-jnp.arange(0, half, 2, dtype=f32) / half)
    ar, ac = rid[:, None] * inv[None, :], cid[:, None] * inv[None, :]
    cos = jnp.concatenate([jnp.cos(ar), jnp.cos(ar), jnp.cos(ac), jnp.cos(ac)], axis=1)
    sin = jnp.concatenate([-jnp.sin(ar), jnp.sin(ar), -jnp.sin(ac), jnp.sin(ac)], axis=1)
    return jnp.tile(cos, (1, 2)), jnp.tile(sin, (1, 2))


def _head_mats():
    head = jnp.arange(QW, dtype=jnp.int32)[:, None] // HD == jnp.arange(128, dtype=jnp.int32)[None, :]
    return head.astype(f32) * (1.0 / HD), head.astype(f32).T


def _rot(v):
    w = v.shape[1]
    lane = lax.broadcasted_iota(jnp.int32, v.shape, 1)
    return jnp.where(lane % 32 < 16, pltpu.roll(v, w - 16, 1), pltpu.roll(v, 16, 1))


def _head_mean(v, e, et):
    w = v.shape[1]
    m = jnp.dot(v, e[:w], preferred_element_type=f32, precision=HIGH)
    return m, et[:, :w]


def _rms_rope(t, gain, e, et, cos, sin):
    w = t.shape[1]
    ms, spread = _head_mean(t * t, e, et)
    rs = jnp.dot(lax.rsqrt(ms + QK_EPS), spread, preferred_element_type=f32, precision=HIGH)
    n0 = t * rs
    n = n0 * gain
    reps = w // 128
    return n * jnp.tile(cos, (1, reps)) + _rot(n) * jnp.tile(sin, (1, reps))


def _rms_rope_bwd(dout, t, gain, e, et, cos, sin):
    w = t.shape[1]
    reps = w // 128
    ms, spread = _head_mean(t * t, e, et)
    rs = jnp.dot(lax.rsqrt(ms + QK_EPS), spread, preferred_element_type=f32, precision=HIGH)
    n0 = t * rs
    dn = dout * jnp.tile(cos, (1, reps)) + _rot(dout * jnp.tile(sin, (1, reps)))
    dn0 = dn * gain
    mm, _ = _head_mean(dn0 * n0, e, et)
    corr = jnp.dot(mm, spread, preferred_element_type=f32, precision=HIGH)
    return rs * (dn0 - n0 * corr), jnp.sum(dn * n0, axis=0, keepdims=True)


def _qk_fwd(qkv, qg, kg, e, et, cos, sin):
    def fn(i, qkv, cos, sin, qg, kg, e, et):
        q = _rms_rope(qkv[:, :QW], qg, e, et, cos, sin) * Q_SCALE
        k = _rms_rope(qkv[:, QW:QW + KW], kg, e, et, cos, sin)
        return (q, k, qkv[:, QW + KW:]), ()

    return _ew("qk_rope", fn, [qkv, cos, sin], [qg, kg, e, et], [(QW, bf16), (KW, bf16), (KW, bf16)])


def _qk_bwd(qkv, dq, dk, dv, qg, kg, e, et, cos, sin):
    def fn(i, qkv, cos, sin, dq, dk, dv, qg, kg, e, et):
        dtq, dgq = _rms_rope_bwd(dq * (HD ** -0.5), qkv[:, :QW], qg, e, et, cos, sin)
        dtk, dgk = _rms_rope_bwd(dk * math.log(2.0), qkv[:, QW:QW + KW], kg, e, et, cos, sin)
        return (jnp.concatenate([dtq, dtk, dv], axis=1),), (dgq, dgk)

    return _ew("qk_rope_bwd", fn, [qkv, cos, sin, dq, dk, dv], [qg, kg, e, et], [(QKVW, bf16)], [(1, QW), (1, KW)])


def _to_heads(a, nh):
    return a.reshape(a.shape[0], nh, HD).transpose(1, 0, 2)


def _from_heads(a):
    return a.transpose(1, 0, 2).reshape(a.shape[1], a.shape[0] * HD)


def _masked_first(s, c):
    if c:
        return s
    col = lax.broadcasted_iota(jnp.int32, (1, s.shape[1]), 1)
    return jnp.where(col >= PAD, s, NEG)


def _flash_fwd(q, k, v1, tq=ROW_TILE, tc=KEY_CHUNK):
    n = q.shape[1]
    nc = n // tc
    pair = 2

    def body(q_ref, k_ref, v_ref, o_ref, ob_ref, lse_ref):
        def scores(h, c):
            ks = k_ref[0, pl.ds(c * tc, tc), :]
            return _masked_first(lax.dot_general(q_ref[h], ks, NT, preferred_element_type=f32), c)

        m = [jnp.full((tq, 1), NEG, f32) for _ in range(pair)]
        acc = [jnp.zeros((tq, 2 * HD), f32) for _ in range(pair)]
        nxt = [scores(h, 0) for h in range(pair)]
        for c in range(nc):
            for h in range(pair):
                s = nxt[h]
                if c + 1 < nc:
                    nxt[h] = scores(h, c + 1)
                m_new = jnp.maximum(m[h], jnp.max(s, axis=1, keepdims=True))
                p = jnp.exp2(s - m_new)
                acc[h] = jnp.exp2(m[h] - m_new) * acc[h] + jnp.dot(p.astype(bf16), v_ref[0, pl.ds(c * tc, tc), :],
                                                                     preferred_element_type=f32)
                m[h] = m_new
        ls = [a[:, HD:HD + 1] for a in acc]
        o = jnp.concatenate([a[:, :HD] / l for a, l in zip(acc, ls)], axis=1)
        o_ref[...] = o
        ob_ref[...] = o.astype(bf16)
        lse_ref[0] = jnp.concatenate([mh + jnp.log2(l) for mh, l in zip(m, ls)], axis=1)

    kv_of = NQ // NKV // pair
    tok = pl.BlockSpec((tq, pair * HD), lambda hp, i: (i, hp))
    return pl.pallas_call(
        body, grid=(NQ // pair, n // tq),
        in_specs=[pl.BlockSpec((pair, tq, HD), lambda hp, i: (hp, i, 0)),
                  pl.BlockSpec((1, n, HD), lambda hp, i: (hp // kv_of, 0, 0)),
                  pl.BlockSpec((1, n, 2 * HD), lambda hp, i: (hp // kv_of, 0, 0))],
        out_specs=[tok, tok, pl.BlockSpec((1, tq, pair), lambda hp, i: (hp, i, 0))],
        out_shape=[jax.ShapeDtypeStruct((n, NQ * HD), f32), jax.ShapeDtypeStruct((n, NQ * HD), bf16),
                   jax.ShapeDtypeStruct((NQ // pair, n, pair), f32)],
        name="flash_fwd", compiler_params=_params(("parallel", "parallel")))(q, k, v1)


def _flash_bwd(q, k, kt, v, do, lse_row, delta_row, tk=ROW_TILE, tc=KEY_CHUNK):
    n = q.shape[1]
    nc = n // tc
    grp = NQ // NKV
    pair = 2

    def body(q_ref, do_ref, lse_ref, delta_ref, k_ref, kt_ref, v_ref, dqt_ref, dk_ref, dv_ref):
        j, g = pl.program_id(1), pl.program_id(2)
        kb, vb, ktb = k_ref[0], v_ref[0], kt_ref[0]
        valid = lax.broadcasted_iota(jnp.int32, (tk, 1), 0) + j * tk >= PAD

        @pl.when(j == 0)
        def _():
            for h in range(pair):
                dqt_ref[pair * g + h] = jnp.zeros((HD, n), f32)

        def products(h, c):
            rows = pl.ds(c * tc, tc)
            return (lax.dot_general(kb, q_ref[h, rows, :], NT, preferred_element_type=f32),
                    lax.dot_general(vb, do_ref[h, rows, :], NT, preferred_element_type=f32))

        dk = jnp.zeros((tk, HD), f32)
        dv = jnp.zeros((tk, HD), f32)
        nxt = [products(h, 0) for h in range(pair)]
        for c in range(nc):
            rows = pl.ds(c * tc, tc)
            for h in range(pair):
                st, dpt = nxt[h]
                if c + 1 < nc:
                    nxt[h] = products(h, c + 1)
                pt = jnp.exp2(jnp.where(valid, st, NEG) - lse_ref[h, :, rows])
                dv = dv + jnp.dot(pt.astype(bf16), do_ref[h, rows, :], preferred_element_type=f32)
                dst = (pt * (dpt - delta_ref[h, :, rows])).astype(bf16)
                dk = dk + jnp.dot(dst, q_ref[h, rows, :], preferred_element_type=f32)
                dqt_ref[pair * g + h, :, rows] += jnp.dot(ktb, dst, preferred_element_type=f32)

        @pl.when(g == 0)
        def _():
            dk_ref[0] = dk
            dv_ref[0] = dv

        @pl.when(g > 0)
        def _():
            dk_ref[0] += dk
            dv_ref[0] += dv

    steps = grp // pair
    hspec = pl.BlockSpec((pair, n, HD), lambda h, j, g: (h * steps + g, 0, 0))
    rspec = pl.BlockSpec((pair, 1, n), lambda h, j, g: (h * steps + g, 0, 0))
    kspec = pl.BlockSpec((1, tk, HD), lambda h, j, g: (h, j, 0))
    return pl.pallas_call(
        body, grid=(NKV, n // tk, steps),
        in_specs=[hspec, hspec, rspec, rspec, kspec, pl.BlockSpec((1, HD, tk), lambda h, j, g: (h, 0, j)), kspec],
        out_specs=[pl.BlockSpec((grp, HD, n), lambda h, j, g: (h, 0, 0)), kspec, kspec],
        out_shape=[jax.ShapeDtypeStruct((NQ, HD, n), f32)] + [jax.ShapeDtypeStruct((NKV, n, HD), f32)] * 2,
        name="flash_bwd", compiler_params=_params(("parallel", "arbitrary", "arbitrary")))(
            q, do, lse_row, delta_row, k, kt, v)


def _attn_fwd(h_bf, w_qkv_t, qg, kg, tabs):
    e, et, cos, sin = tabs
    qkv = _mm("attn_qkv_mm", h_bf, w_qkv_t, trans_b=True)
    q_bf, k_bf, v_bf = _qk_fwd(qkv, qg, kg, e, et, cos, sin)
    q16, k4, v4 = _to_heads(q_bf, NQ), _to_heads(k_bf, NKV), _to_heads(v_bf, NKV)
    ones = jnp.zeros((NKV, v4.shape[1], HD), bf16).at[:, :, 0].set(1.0)
    o, o_bf, lse = _flash_fwd(q16, k4, jnp.concatenate([v4, ones], axis=2))
    lse_row = lse.transpose(0, 2, 1).reshape(NQ, 1, qkv.shape[0])
    return o_bf, (qkv, q16, k4, v4, o, lse_row, o_bf)


def _attn_bwd(dmix_bf, h_bf, saved, w_qkv_t, qg, kg, w_out, tabs):
    e, et, cos, sin = tabs
    qkv, q16, k4, v4, o, lse_row, o_bf = saved
    n = qkv.shape[0]
    do = _mm("attn_do_mm", dmix_bf, w_out, trans_b=True, out_dtype=bf16)
    d_w_out = _mm_tn("attn_dwout", o_bf, dmix_bf)

    def head_dots(i, do, o, e):
        return (jnp.dot(do.astype(f32) * o, e, preferred_element_type=f32, precision=HIGH) * HD,), ()

    delta = _ew("attn_delta", head_dots, [do, o], [e], [(128, f32)])[0]
    dqt, dk4, dv4 = _flash_bwd(q16, k4, k4.transpose(0, 2, 1), v4, _to_heads(do, NQ), lse_row,
                               delta[:, :NQ].T.reshape(NQ, 1, n))
    dq = dqt.transpose(2, 0, 1).reshape(n, QW)
    dqkv_bf, dgq, dgk = _qk_bwd(qkv, dq, _from_heads(dk4), _from_heads(dv4), qg, kg, e, et, cos, sin)
    d_w_qkv_t = _mm_tn("attn_dwqkv", dqkv_bf, h_bf)
    dh = _mm("attn_dh_mm", dqkv_bf, w_qkv_t)
    return dh, d_w_qkv_t, dgq.reshape(NQ, HD).sum(0), dgk.reshape(NKV, HD).sum(0), d_w_out


def _all_gather(name, shard):
    def body(x_ref, out_ref, send_sems, recv_sems, local_sem):
        x, y, c = lax.axis_index("x"), lax.axis_index("y"), lax.axis_index("c")
        me, sibling = (x, y, c), (x, y, 1 - c)
        chips = [(1 - x, y), (x, 1 - y), (1 - x, 1 - y)]

        def slot(px, py, pc):
            return out_ref.at[4 * px + 2 * py + pc]

        def copy(k, block, to, src=None):
            return pltpu.make_async_remote_copy(
                src_ref=slot(*block) if src is None else src, dst_ref=slot(*block),
                send_sem=send_sems.at[k], recv_sem=recv_sems.at[k], device_id=to, device_id_type=MESH)

        mine = pltpu.make_async_copy(x_ref, slot(*me), local_sem)
        mine.start()
        first = [copy(0, me, sibling, src=x_ref)]
        first += [copy(1 + j, me, (*chip, c), src=x_ref) for j, chip in enumerate(chips)]
        for cp in first:
            cp.start()
        passed = [copy(4 + j, (*chip, c), sibling) for j, chip in enumerate(chips)]
        for j, chip in enumerate(chips):
            copy(1 + j, (*chip, c), me).wait_recv()
            passed[j].start()
        copy(0, sibling, me).wait_recv()
        for j, chip in enumerate(chips):
            copy(4 + j, (*chip, 1 - c), me).wait_recv()
        for cp in first + passed:
            cp.wait_send()
        mine.wait()

    return pl.pallas_call(
        body, out_shape=jax.ShapeDtypeStruct((8,) + shard.shape, shard.dtype), in_specs=[ANY], out_specs=ANY,
        scratch_shapes=[pltpu.SemaphoreType.DMA((7,)), pltpu.SemaphoreType.DMA((7,)), pltpu.SemaphoreType.DMA],
        name=name)(shard)


def _swap_sibling(name, theirs):
    k = len(theirs)

    def body(*refs):
        src, dst, send_sems, recv_sems = refs[:k], refs[k:2 * k], refs[2 * k], refs[2 * k + 1]
        x, y, c = lax.axis_index("x"), lax.axis_index("y"), lax.axis_index("c")
        copies = [pltpu.make_async_remote_copy(src_ref=src[j], dst_ref=dst[j], send_sem=send_sems.at[j],
                                               recv_sem=recv_sems.at[j], device_id=(x, y, 1 - c), device_id_type=MESH)
                  for j in range(k)]
        for cp in copies:
            cp.start()
        for cp in copies:
            cp.wait()

    return pl.pallas_call(
        body, out_shape=[jax.ShapeDtypeStruct(a.shape, a.dtype) for a in theirs], in_specs=[ANY] * k,
        out_specs=[ANY] * k, scratch_shapes=[pltpu.SemaphoreType.DMA((k,)), pltpu.SemaphoreType.DMA((k,))],
        name=name)(*theirs)


def _exchange_chips(name, parts):
    k = len(parts)

    def body(*refs):
        p_refs, t_refs = refs[:k], refs[k:2 * k]
        send_sems, recv_sems, local_sems = refs[2 * k:]
        x, y, c = lax.axis_index("x"), lax.axis_index("y"), lax.axis_index("c")
        q = 2 * x + y
        copies = []
        for j in range(k):
            copies.append(pltpu.make_async_copy(p_refs[j].at[q], t_refs[j].at[q], local_sems.at[j]))
            for hop in (1, 2, 3):
                tx, ty = x ^ (hop >> 1), y ^ (hop & 1)
                copies.append(pltpu.make_async_remote_copy(
                    src_ref=p_refs[j].at[2 * tx + ty], dst_ref=t_refs[j].at[q], send_sem=send_sems.at[3 * j + hop - 1],
                    recv_sem=recv_sems.at[3 * j + hop - 1], device_id=(tx, ty, c), device_id_type=MESH))
        for cp in copies:
            cp.start()
        for cp in copies:
            cp.wait()

    return pl.pallas_call(
        body, out_shape=[jax.ShapeDtypeStruct(a.shape, a.dtype) for a in parts], in_specs=[ANY] * k,
        out_specs=[ANY] * k,
        scratch_shapes=[pltpu.SemaphoreType.DMA((3 * k,)), pltpu.SemaphoreType.DMA((3 * k,)),
                        pltpu.SemaphoreType.DMA((k,))],
        name=name)(*parts)


def _reduce_scatter(mine, theirs):
    got = _swap_sibling("rs_sibling", list(theirs))
    parts = []
    for a, b, dt, nm in zip(mine, got, (bf16, f32), ("rs_add2", "rs_add2_small")):
        rows = 4 * a.shape[1]
        parts.append(_ew(nm, lambda i, a, b: ((a + b,), ()), [a.reshape(rows, D), b.reshape(rows, D)], [],
                         [(D, dt)], tile=RS_TILE)[0].reshape(a.shape))
    ts = _exchange_chips("rs_chips", parts)

    def add4(i, a, b, c, d):
        return ((((a.astype(f32) + b.astype(f32)) + c.astype(f32)) + d.astype(f32),), ())

    return [_ew(nm, add4, [(t, 0), (t, 1), (t, 2), (t, 3)], [], [(D, f32)], tile=RS_TILE)[0]
            for t, nm in zip(ts, ("rs_add4", "rs_add4_small"))]


def _pack_rows(parts, rows):
    flat = jnp.concatenate([p.reshape(-1) for p in parts])
    return jnp.pad(flat, (0, rows * D - flat.shape[0])).reshape(rows, D)


def _unpack(flat, shapes):
    out, off = [], 0
    for s in shapes:
        n = math.prod(s)
        out.append(flat[off:off + n].reshape(s))
        off += n
    return out


def _mat_rows(block, transposed, blk):
    a = jnp.swapaxes(block, 1, 2) if transposed else block
    a = jnp.pad(a, ((0, 0), (0, blk - a.shape[1]), (0, 0)))
    return a.reshape(-1, D)


def _mat_block(rows, transposed, blk, real):
    a = rows.reshape(-1, blk, D)[:, :real]
    return jnp.swapaxes(a, 1, 2) if transposed else a


def _mat_full(gathered, blk):
    layers = gathered.shape[1] // blk
    return gathered.reshape(8, layers, blk, D).transpose(1, 0, 2, 3).reshape(layers, 8 * blk, D)


def _vec_full(gathered):
    return gathered.transpose(1, 0, 2).reshape(gathered.shape[1], D)


def _grad_slots(full, small, cc):
    def halves(a, blk):
        a4 = a.reshape(4, 2, blk, D)
        return [lax.dynamic_index_in_dim(a4, sel, axis=1, keepdims=False) for sel in (cc, 1 - cc)]

    mine, theirs = [], []
    for name, _, blk, _ in MATS:
        for layer in full[name]:
            a, b = halves(layer, blk)
            mine.append(a)
            theirs.append(b)
    a, b = halves(small, REP_PIECE)
    return (jnp.concatenate(mine, axis=1), a), (jnp.concatenate(theirs, axis=1), b)


def _local_step(x0, target0, w, fw):
    seq = x0.shape[0]
    n = OFF + seq
    movers = _block_movers()
    h = jnp.concatenate([jnp.zeros((PAD, D), f32), fw['meta_tokens'], x0], axis=0)
    h_bf = h.astype(bf16)
    tabs = _head_mats() + _rope_tables(n)
    qg = [jnp.tile(w['attn_q_gain'][j], NQ)[None, :] for j in range(2)]
    kg = [jnp.tile(w['attn_k_gain'][j], NKV)[None, :] for j in range(2)]
    s5_names = ['s5_lambda_re', 's5_lambda_im', 's5_log_dt', 's5_b_re', 's5_b_im', 's5_c_re', 's5_c_im']
    s5_mats, s5_vjp = [], []
    for j in range(2):
        mats, vjp = jax.vjp(_s5_mats, *[w[k][j] for k in s5_names])
        s5_mats.append(mats)
        s5_vjp.append(vjp)
    saved = []
    for i in range(DEPTH):
        j = i // 2
        if i % 2 == 0:
            mixed, sv = _s5_fwd(h, movers, s5_mats[j], w['s5_d'][j][None, :], (fw['s5_w_glu'], j))
            w_out = (fw['s5_w_out'], j)
        else:
            mixed, sv = _attn_fwd(h_bf, (fw['attn_w_qkv'], j), qg[j], kg[j], tabs)
            w_out = (fw['attn_w_out'], j)
        r1, h1, h1_bf = _mm_ln("mixer_out_ln", mixed, w_out, h, fw['ln_gain'][i, 0][None, :], fw['ln_bias'][i, 0][None, :])
        gate, up, act = _ffn_up(h1_bf, (fw['ffn_w_gate'], i), (fw['ffn_w_up'], i))
        r2, h2, h2_bf = _mm_ln("ffn_down_ln", act, (fw['ffn_w_down'], i), h1, fw['ln_gain'][i, 1][None, :],
                               fw['ln_bias'][i, 1][None, :])
        saved.append((h, h_bf, sv, r1, h1_bf, gate, up, act, r2))
        h, h_bf = h2, h2_bf

    d_b, sq = _loss_grad(h, target0)
    loss = 0.5 * jnp.sum(sq) * (1.0 / D)

    grads = {k: [None] * (DEPTH if k.startswith('ffn') else 2) for k in WEIGHTS}
    d_ln_gain = [[None, None] for _ in range(DEPTH)]
    d_ln_bias = [[None, None] for _ in range(DEPTH)]
    d_a = None
    for i in reversed(range(DEPTH)):
        j = i // 2
        h_in, h_in_bf, sv, r1, h1_bf, gate, up, act, r2 = saved[i]
        dr2, dr2_bf, dg, db = _ln_bwd(d_a, d_b, r2, fw['ln_gain'][i, 1][None, :])
        d_ln_gain[i][1], d_ln_bias[i][1] = dg[0], db[0]
        dgate, dup = _ffn_dup(dr2_bf, (fw['ffn_w_down'], i), gate, up)
        grads['ffn_w_down'][i] = _mm_tn("ffn_dwdown", act, dr2_bf, tk=DFFP // 2)
        grads['ffn_w_gate'][i] = _mm_tn("ffn_dwgate", dgate, h1_bf, tk=DFFP // 2)
        grads['ffn_w_up'][i] = _mm_tn("ffn_dwup", dup, h1_bf, tk=DFFP // 2)
        dh1 = _mm2("ffn_dh_mm", dgate, (fw['ffn_w_gate'], i), dup, (fw['ffn_w_up'], i))
        dr1, dr1_bf, dg, db = _ln_bwd(dr2, dh1, r1, fw['ln_gain'][i, 0][None, :])
        d_ln_gain[i][0], d_ln_bias[i][0] = dg[0], db[0]
        if i % 2 == 0:
            dh, d_par, dd, d_w_glu, d_w_out = _s5_bwd(dr1_bf, h_in, movers, sv, s5_mats[j], s5_vjp[j],
                                                      w['s5_d'][j][None, :], (fw['s5_w_glu'], j), (fw['s5_w_out'], j))
            for k, g in zip(s5_names, d_par):
                grads[k][j] = g
            grads['s5_d'][j], grads['s5_w_glu'][j], grads['s5_w_out'][j] = dd, d_w_glu, d_w_out
        else:
            dh, d_w_qkv, dgq, dgk, d_w_out = _attn_bwd(dr1_bf, h_in_bf, sv, (fw['attn_w_qkv'], j), qg[j], kg[j],
                                                       (fw['attn_w_out'], j), tabs)
            grads['attn_w_qkv'][j], grads['attn_w_out'][j] = d_w_qkv, d_w_out
            grads['attn_q_gain'][j], grads['attn_k_gain'][j] = dgq, dgk
        d_a, d_b = dr1, dh
    dh0 = _ew("dh0", lambda i, a, b: ((ALPHA * a + b,), ()), [d_a, d_b], [], [(D, f32)])[0]
    mats = {m[0] for m in MATS}
    full = {k: (v if k in mats else jnp.stack(v)) for k, v in grads.items() if v[0] is not None}
    full['meta_tokens'] = dh0[PAD:OFF]
    full['ln_gain'] = jnp.stack([jnp.stack(r) for r in d_ln_gain])
    full['ln_bias'] = jnp.stack([jnp.stack(r) for r in d_ln_bias])

    return loss, dh0[OFF:], full


def kernel(x, meta_tokens, s5_lambda_re, s5_lambda_im, s5_log_dt, s5_b_re, s5_b_im, s5_c_re, s5_c_im, s5_d, s5_w_glu, s5_w_out, attn_w_qkv, attn_q_gain, attn_k_gain, attn_w_out, ffn_w_gate, ffn_w_up, ffn_w_down, ln_gain, ln_bias, loss_target, m_meta_tokens, m_s5_lambda_re, m_s5_lambda_im, m_s5_log_dt, m_s5_b_re, m_s5_b_im, m_s5_c_re, m_s5_c_im, m_s5_d, m_s5_w_glu, m_s5_w_out, m_attn_w_qkv, m_attn_q_gain, m_attn_k_gain, m_attn_w_out, m_ffn_w_gate, m_ffn_w_up, m_ffn_w_down, m_ln_gain, m_ln_bias, v_meta_tokens, v_s5_lambda_re, v_s5_lambda_im, v_s5_log_dt, v_s5_b_re, v_s5_b_im, v_s5_c_re, v_s5_c_im, v_s5_d, v_s5_w_glu, v_s5_w_out, v_attn_w_qkv, v_attn_q_gain, v_attn_k_gain, v_attn_w_out, v_ffn_w_gate, v_ffn_w_up, v_ffn_w_down, v_ln_gain, v_ln_bias):
    w = dict(zip(WEIGHTS, (meta_tokens, s5_lambda_re, s5_lambda_im, s5_log_dt, s5_b_re, s5_b_im, s5_c_re, s5_c_im, s5_d, s5_w_glu, s5_w_out, attn_w_qkv, attn_q_gain, attn_k_gain, attn_w_out, ffn_w_gate, ffn_w_up, ffn_w_down, ln_gain, ln_bias)))
    mom = dict(zip(WEIGHTS, (m_meta_tokens, m_s5_lambda_re, m_s5_lambda_im, m_s5_log_dt, m_s5_b_re, m_s5_b_im, m_s5_c_re, m_s5_c_im, m_s5_d, m_s5_w_glu, m_s5_w_out, m_attn_w_qkv, m_attn_q_gain, m_attn_k_gain, m_attn_w_out, m_ffn_w_gate, m_ffn_w_up, m_ffn_w_down, m_ln_gain, m_ln_bias)))
    vel = dict(zip(WEIGHTS, (v_meta_tokens, v_s5_lambda_re, v_s5_lambda_im, v_s5_log_dt, v_s5_b_re, v_s5_b_im, v_s5_c_re, v_s5_c_im, v_s5_d, v_s5_w_glu, v_s5_w_out, v_attn_w_qkv, v_attn_q_gain, v_attn_k_gain, v_attn_w_out, v_ffn_w_gate, v_ffn_w_up, v_ffn_w_down, v_ln_gain, v_ln_bias)))
    cc = lax.axis_index("c")
    dev = 4 * lax.axis_index("x") + 2 * lax.axis_index("y") + cc

    mat_rows = jnp.concatenate([_mat_rows(w[n], t, blk) for n, t, blk, _ in MATS]).astype(bf16)
    g_mats = _all_gather("ag_weights", mat_rows)
    g_vecs = _all_gather("ag_vectors", jnp.concatenate([w[n].reshape(-1, 128) for n in VECS]))
    fw, off = {}, 0
    for n, _, blk, _ in MATS:
        rows = w[n].shape[0] * blk
        fw[n] = _mat_full(g_mats[:, off:off + rows], blk)
        off += rows
    off = 0
    for n in VECS:
        rows = w[n].size // 128
        fw[n] = _vec_full(g_vecs[:, off:off + rows]).reshape(w[n].shape[:-1] + (D,))
        off += rows

    loss, grad_x, full = _local_step(x[0], loss_target[0], w, fw)
    loss = lax.psum(loss, AXES)
    grad_x = grad_x[None]

    small_names = REPL + VECS
    mine, theirs = _grad_slots(full, _pack_rows([full[k] for k in small_names], REP_ROWS), cc)
    red, red_small = _reduce_scatter(mine, theirs)
    small_all = _all_gather("ag_small_grads", red_small).reshape(REP_ROWS * D)
    g, off = {}, 0
    for n, t, blk, real in MATS:
        rows = w[n].shape[0] * blk
        g[n] = _mat_block(red[off:off + rows], t, blk, real)
        off += rows
    small = dict(zip(small_names, _unpack(small_all, [full[k].shape for k in small_names])))
    for k in REPL:
        g[k] = small[k]
    for k in VECS:
        g[k] = lax.dynamic_slice_in_dim(small[k], dev * 128, 128, axis=small[k].ndim - 1)

    delta, new_m, new_v = {}, {}, {}
    for n in WEIGHTS:
        shp = w[n].shape
        res = _adamw(*[d[n].reshape(-1, shp[-1]) for d in (w, g, mom, vel)])
        delta[n], new_m[n], new_v[n] = [a.reshape(shp) for a in res]
    return (loss, grad_x, *[g[k] for k in WEIGHTS], *[delta[k] for k in WEIGHTS],
            *[new_m[k] for k in WEIGHTS], *[new_v[k] for k in WEIGHTS])
```

```python
import functools
import math

import jax
import jax.numpy as jnp
from jax import lax
from jax.experimental import pallas as pl
from jax.experimental.pallas import tpu as pltpu

f32 = jnp.float32
bf16 = jnp.bfloat16
HIGH = lax.Precision.HIGH
MESH = pl.DeviceIdType.MESH
AXES = ("x", "y", "c")
ANY = pl.BlockSpec(memory_space=pl.ANY)

D = 1024
DEPTH = 4
N_META = 16
PAD = 240
OFF = PAD + N_META
ROW_TILE = 768
KEY_CHUNK = 256
FFN_TILE = 256
ADAM_TILE = 544
GRID_W = 64
HD = 64
NQ = 16
NKV = 4
QW = NQ * HD
KW = NKV * HD
QKVW = QW + 2 * KW
DFF = 2816
GROUPS = 64
GCH = 16
NSTATE = 64
CHUNK = 16
GB = 8
ROPE_THETA = 10000.0
LN_EPS = 1e-5
QK_EPS = 1e-6
ALPHA = (2.0 * DEPTH) ** 0.25
ADAM_LR, ADAM_B1, ADAM_B2, ADAM_EPS, ADAM_WD, ADAM_STEP = 0.001, 0.9, 0.999, 1e-08, 0.01, 10
NEG = -1e30
Q_SCALE = HD ** -0.5 * math.log2(math.e)
VMEM_MB = 56

NT = (((1,), (1,)), ((), ()))
TN = (((0,), (0,)), ((), ()))

WEIGHTS = ['meta_tokens', 's5_lambda_re', 's5_lambda_im', 's5_log_dt', 's5_b_re', 's5_b_im', 's5_c_re', 's5_c_im',
           's5_d', 's5_w_glu', 's5_w_out', 'attn_w_qkv', 'attn_q_gain', 'attn_k_gain', 'attn_w_out', 'ffn_w_gate',
           'ffn_w_up', 'ffn_w_down', 'ln_gain', 'ln_bias']
DFFP = DFF
FF_BLK, FF_BLKP = DFF // 8, DFFP // 8
MATS = [('s5_w_glu', False, 128, 128), ('s5_w_out', False, 128, 128), ('attn_w_qkv', True, 192, 192),
        ('attn_w_out', False, 128, 128), ('ffn_w_gate', True, FF_BLKP, FF_BLK), ('ffn_w_up', True, FF_BLKP, FF_BLK),
        ('ffn_w_down', False, FF_BLKP, FF_BLK)]
VECS = ['meta_tokens', 'ln_gain', 'ln_bias']
REPL = ['s5_lambda_re', 's5_lambda_im', 's5_log_dt', 's5_b_re', 's5_b_im', 's5_c_re', 's5_c_im', 's5_d',
        'attn_q_gain', 'attn_k_gain']
MAT_ROWS = 3 * 2 * 128 + 2 * 192 + 3 * DEPTH * FF_BLKP
REP_PIECE = 160
REP_ROWS = 8 * REP_PIECE
RS_TILE = 768


def _params(sem, mb=VMEM_MB):
    return pltpu.CompilerParams(dimension_semantics=sem, vmem_limit_bytes=mb << 20)


def _ew(name, fn, rows, consts, outs, accs=(), tile=ROW_TILE):
    first = rows[0][0] if isinstance(rows[0], tuple) else rows[0]
    n = first.shape[-2]
    tile = min(tile, n)
    assert n % tile == 0, (name, n, tile)
    n_in, n_o, n_a = len(rows) + len(consts), len(outs), len(accs)

    def body(*refs):
        i = pl.program_id(0)
        res_o, res_a = fn(i, *[r[...] for r in refs[:n_in]])
        for r, val in zip(refs[n_in:n_in + n_o], res_o):
            r[...] = val.astype(r.dtype)
        if n_a:
            a_refs = refs[n_in + n_o:]

            @pl.when(i == 0)
            def _():
                for r in a_refs:
                    r[...] = jnp.zeros(r.shape, r.dtype)

            for r, val in zip(a_refs, res_a):
                r[...] += val

    in_specs, args = [], []
    for a in rows:
        if isinstance(a, tuple):
            arr, k = a
            in_specs.append(pl.BlockSpec((None, tile, arr.shape[2]), functools.partial(lambda i, k: (k, i, 0), k=k)))
            args.append(arr)
        else:
            in_specs.append(pl.BlockSpec((tile, a.shape[1]), lambda i: (i, 0)))
            args.append(a)
    for c in consts:
        in_specs.append(pl.BlockSpec(c.shape, lambda i: (0, 0)))
        args.append(c)
    out_specs = [pl.BlockSpec((tile, c), lambda i: (i, 0)) for c, _ in outs]
    out_specs += [pl.BlockSpec(s, lambda i: (0, 0)) for s in accs]
    out_shape = [jax.ShapeDtypeStruct((n, c), dt) for c, dt in outs]
    out_shape += [jax.ShapeDtypeStruct(s, f32) for s in accs]
    res = pl.pallas_call(body, grid=(n // tile,), in_specs=in_specs, out_specs=out_specs, out_shape=out_shape,
                         name=name, compiler_params=_params(("arbitrary",)))(*args)
    return res


def _mm(name, a, b, trans_b=False, out_dtype=f32, tm=ROW_TILE):
    m, k = a.shape
    spec, b, shape = _whole(b)
    n = shape[0] if trans_b else shape[1]
    tm = min(tm, m)
    assert m % tm == 0
    dims = NT if trans_b else (((1,), (0,)), ((), ()))

    def body(a_ref, b_ref, o_ref):
        o_ref[...] = lax.dot_general(a_ref[...], b_ref[...], dims, preferred_element_type=f32).astype(o_ref.dtype)

    return pl.pallas_call(
        body, grid=(m // tm,), in_specs=[pl.BlockSpec((tm, k), lambda i: (i, 0)), spec],
        out_specs=pl.BlockSpec((tm, n), lambda i: (i, 0)),
        out_shape=jax.ShapeDtypeStruct((m, n), out_dtype), name=name, compiler_params=_params(("parallel",)))(a, b)


def _whole(b):
    if isinstance(b, tuple):
        arr, layer = b
        return pl.BlockSpec((None,) + arr.shape[1:], lambda i: (layer, 0, 0)), arr, arr.shape[1:]
    return pl.BlockSpec(b.shape, lambda i: (0, 0)), b, b.shape


def _mm2(name, a1, b1, a2, b2, out_dtype=f32, tm=ROW_TILE // 2):
    m, k = a1.shape
    spec1, b1, shape = _whole(b1)
    spec2, b2, _ = _whole(b2)
    n = shape[1]
    tm = min(tm, m)
    assert m % tm == 0

    def body(a1_ref, b1_ref, a2_ref, b2_ref, o_ref):
        acc = jnp.dot(a1_ref[...], b1_ref[...], preferred_element_type=f32)
        acc += jnp.dot(a2_ref[...], b2_ref[...], preferred_element_type=f32)
        o_ref[...] = acc.astype(o_ref.dtype)

    row = pl.BlockSpec((tm, k), lambda i: (i, 0))
    return pl.pallas_call(
        body, grid=(m // tm,), in_specs=[row, spec1, row, spec2], out_specs=pl.BlockSpec((tm, n), lambda i: (i, 0)),
        out_shape=jax.ShapeDtypeStruct((m, n), out_dtype), name=name,
        compiler_params=_params(("parallel",)))(a1, b1, a2, b2)


def _mm_tn(name, a, g, tk=512, tl=ROW_TILE):
    rows, k1 = a.shape
    n = g.shape[1]
    tl = min(tl, rows)
    assert rows % tl == 0 and k1 % tk == 0

    def body(a_ref, g_ref, o_ref):
        @pl.when(pl.program_id(1) == 0)
        def _():
            o_ref[...] = jnp.zeros(o_ref.shape, f32)

        o_ref[...] += lax.dot_general(a_ref[...], g_ref[...], TN, preferred_element_type=f32)

    return pl.pallas_call(
        body, grid=(k1 // tk, rows // tl),
        in_specs=[pl.BlockSpec((tl, tk), lambda k, l: (l, k)), pl.BlockSpec((tl, n), lambda k, l: (l, 0))],
        out_specs=pl.BlockSpec((tk, n), lambda k, l: (k, 0)),
        out_shape=jax.ShapeDtypeStruct((k1, n), f32), name=name,
        compiler_params=_params(("parallel", "arbitrary")))(a, g)


def _ln_stats(r):
    mean = jnp.mean(r, axis=-1, keepdims=True)
    c = r - mean
    rstd = lax.rsqrt(jnp.mean(c * c, axis=-1, keepdims=True) + LN_EPS)
    return c * rstd, rstd


def _mm_ln(name, a, b, h, gain, bias, tm=ROW_TILE):
    m, k = a.shape
    spec, b, _ = _whole(b)

    def body(a_ref, b_ref, h_ref, g_ref, bias_ref, r_ref, y_ref, yb_ref):
        r = ALPHA * h_ref[...] + jnp.dot(a_ref[...], b_ref[...], preferred_element_type=f32)
        y = _ln_stats(r)[0] * g_ref[...] + bias_ref[...]
        r_ref[...] = r
        y_ref[...] = y
        yb_ref[...] = y.astype(bf16)

    row = pl.BlockSpec((tm, D), lambda i: (i, 0))
    vec = pl.BlockSpec((1, D), lambda i: (0, 0))
    return pl.pallas_call(
        body, grid=(m // tm,), in_specs=[pl.BlockSpec((tm, k), lambda i: (i, 0)), spec, row, vec, vec],
        out_specs=[row, row, row],
        out_shape=[jax.ShapeDtypeStruct((m, D), f32), jax.ShapeDtypeStruct((m, D), f32), jax.ShapeDtypeStruct((m, D), bf16)],
        name=name, compiler_params=_params(("parallel",)))(a, b, h, gain, bias)


def _ln_bwd(d_a, d_b, r, gain):
    def core(dout, r, g):
        xhat, rstd = _ln_stats(r)
        dxh = dout * g
        dr = rstd * (dxh - jnp.mean(dxh, axis=-1, keepdims=True) - xhat * jnp.mean(dxh * xhat, axis=-1, keepdims=True))
        return (dr, dr), (jnp.sum(dout * xhat, axis=0, keepdims=True), jnp.sum(dout, axis=0, keepdims=True))

    outs, accs = [(D, f32), (D, bf16)], [(1, D), (1, D)]
    if d_a is None:
        return _ew("ln_bwd_top", lambda i, d, r, g: core(d, r, g), [d_b, r], [gain], outs, accs)
    return _ew("ln_bwd", lambda i, da, db, r, g: core(ALPHA * da + db, r, g), [d_a, d_b, r], [gain], outs, accs)


def _sigmoid(x):
    return 1.0 / (1.0 + jnp.exp(-x))


def _ffn_up(h_bf, w_gate_t, w_up_t, tm=FFN_TILE):
    m, k = h_bf.shape
    gspec, w_gate_t, (n, _) = _whole(w_gate_t)
    uspec, w_up_t, _ = _whole(w_up_t)

    def body(h_ref, wg_ref, wu_ref, a_ref, t_ref, b_ref):
        h = h_ref[...]
        g = lax.dot_general(h, wg_ref[...], NT, preferred_element_type=f32).astype(bf16).astype(f32)
        u = lax.dot_general(h, wu_ref[...], NT, preferred_element_type=f32).astype(bf16).astype(f32)
        s = _sigmoid(g)
        t = g * s
        a_ref[...] = (t * u).astype(bf16)
        t_ref[...] = t.astype(bf16)
        b_ref[...] = (u * s * (1.0 + g * (1.0 - s))).astype(bf16)

    row = pl.BlockSpec((tm, n), lambda i: (i, 0))
    return pl.pallas_call(
        body, grid=(m // tm,), in_specs=[pl.BlockSpec((tm, k), lambda i: (i, 0)), gspec, uspec],
        out_specs=[row, row, row], out_shape=[jax.ShapeDtypeStruct((m, n), bf16)] * 3, name="ffn_up",
        compiler_params=_params(("parallel",)))(h_bf, w_gate_t, w_up_t)


def _ffn_dup(df_bf, w_down, silu, up_dsilu, tm=FFN_TILE):
    m, k = df_bf.shape
    wspec, w_down, (n, _) = _whole(w_down)

    def body(d_ref, w_ref, t_ref, b_ref, dg_ref, du_ref):
        da = lax.dot_general(d_ref[...], w_ref[...], NT, preferred_element_type=f32).astype(bf16).astype(f32)
        dg_ref[...] = (da * b_ref[...].astype(f32)).astype(bf16)
        du_ref[...] = (da * t_ref[...].astype(f32)).astype(bf16)

    row = pl.BlockSpec((tm, n), lambda i: (i, 0))
    return pl.pallas_call(
        body, grid=(m // tm,),
        in_specs=[pl.BlockSpec((tm, k), lambda i: (i, 0)), wspec, row, row],
        out_specs=[row, row], out_shape=[jax.ShapeDtypeStruct((m, n), bf16)] * 2, name="ffn_dup",
        compiler_params=_params(("parallel",)))(df_bf, w_down, silu, up_dsilu)


def _loss_grad(h, target):
    n = h.shape[0]

    def body(h_ref, t_ref, d_ref, sq_ref):
        i = pl.program_id(0)

        @pl.when(i == 0)
        def _():
            d_ref[...] = jnp.zeros(d_ref.shape, f32)
            sq_ref[...] = jnp.zeros(sq_ref.shape, f32)

        @pl.when(i > 0)
        def _():
            e = h_ref[...] - t_ref[...]
            d_ref[...] = e * (1.0 / D)
            sq_ref[...] += jnp.sum(e * e, axis=0, keepdims=True)

    return pl.pallas_call(
        body, grid=(n // OFF,),
        in_specs=[pl.BlockSpec((OFF, D), lambda i: (i, 0)), pl.BlockSpec((OFF, D), lambda i: (jnp.maximum(i - 1, 0), 0))],
        out_specs=[pl.BlockSpec((OFF, D), lambda i: (i, 0)), pl.BlockSpec((1, D), lambda i: (0, 0))],
        out_shape=[jax.ShapeDtypeStruct((n, D), f32), jax.ShapeDtypeStruct((1, D), f32)], name="loss",
        compiler_params=_params(("arbitrary",)))(h, target)


def _adamw(w, g, m, v):
    def fn(i, w, g, m, v):
        m = ADAM_B1 * m + (1.0 - ADAM_B1) * g
        v = ADAM_B2 * v + (1.0 - ADAM_B2) * jnp.square(g)
        m_hat = m / (1.0 - ADAM_B1 ** ADAM_STEP)
        v_hat = v / (1.0 - ADAM_B2 ** ADAM_STEP)
        delta = -ADAM_LR * (m_hat / (jnp.sqrt(v_hat) + ADAM_EPS) + ADAM_WD * w)
        return (delta, m, v), ()

    rows, cols = w.shape
    cap = ADAM_TILE if cols > 128 else 4 * ADAM_TILE
    fits = [t for t in range(8, min(rows, cap) + 1, 8) if rows % t == 0]
    return _ew("adamw", fn, [w, g, m, v], [], [(cols, f32)] * 3, tile=max(fits) if fits else rows)


def _s5_mats(lam_re, lam_im, log_dt, b_re, b_im, c_re, c_im):
    steps = jnp.arange(CHUNK + 1, dtype=f32)
    n = CHUNK * GCH
    last = n - GCH

    def one(lr, li, ldt, br, bi, cr, ci, reverse):
        dt = jnp.exp(ldt)[:, None]
        mag = jnp.exp(lr * dt)
        abr, abi = mag * jnp.cos(li * dt), mag * jnp.sin(li * dt)
        nr, ni = abr - 1.0, abi
        den = lr * lr + li * li
        zr, zi = (nr * lr + ni * li) / den, (ni * lr - nr * li) / den
        bbr = zr[..., None] * br - zi[..., None] * bi
        bbi = zr[..., None] * bi + zi[..., None] * br
        pmag = jnp.exp((lr * dt)[..., None] * steps)
        pang = (li * dt)[..., None] * steps
        pr, pi = pmag * jnp.cos(pang), pmag * jnp.sin(pang)
        crt, cit = jnp.swapaxes(cr, 1, 2)[:, :, None, :], jnp.swapaxes(ci, 1, 2)[:, :, None, :]
        car = crt * pr[..., None] - cit * pi[..., None]
        cai = crt * pi[..., None] + cit * pr[..., None]
        if reverse:
            taps = slice(CHUNK - 1, None, -1)
            outs = slice(CHUNK, 0, -1)
            ins = slice(0, CHUNK)
        else:
            taps, outs, ins = slice(0, CHUNK), slice(1, CHUNK + 1), slice(CHUNK - 1, None, -1)
        kern = (jnp.einsum('gpi,gpq->giq', bbr, car[:, :, taps].reshape(GROUPS, NSTATE, n), precision=HIGH)
                - jnp.einsum('gpi,gpq->giq', bbi, cai[:, :, taps].reshape(GROUPS, NSTATE, n), precision=HIGH))
        wide = jnp.pad(kern, ((0, 0), (0, 0), (0, last) if reverse else (last, 0)))
        m = jnp.stack([wide[:, :, last - GCH * t:last - GCH * t + n] for t in range(CHUNK)], axis=1)
        qr = jnp.swapaxes(pr[:, :, ins], 1, 2)[:, :, None, :]
        qi = jnp.swapaxes(pi[:, :, ins], 1, 2)[:, :, None, :]
        bbrt, bbit = jnp.swapaxes(bbr, 1, 2)[:, None], jnp.swapaxes(bbi, 1, 2)[:, None]
        pin = jnp.concatenate([qr * bbrt - qi * bbit, qr * bbit + qi * bbrt], axis=-1)
        pout = jnp.concatenate([car[:, :, outs].reshape(GROUPS, NSTATE, n),
                                -cai[:, :, outs].reshape(GROUPS, NSTATE, n)], axis=1)
        return (m.reshape(GROUPS, n, n), pin.reshape(GROUPS, n, 2 * NSTATE), pout, pr[:, :, CHUNK], pi[:, :, CHUNK])

    mf, pinf, poutf, arf, aif = one(lam_re[0], lam_im[0], log_dt[0], b_re[0], b_im[0], c_re[0], c_im[0], False)
    mr, pinr, poutr, arr, air = one(lam_re[1], lam_im[1], log_dt[1], b_re[1], b_im[1], c_re[1], c_im[1], True)
    return (mf + mr, jnp.concatenate([pinf, pinr], 2), jnp.concatenate([poutf, poutr], 1),
            jnp.stack([arf, arr]), jnp.stack([aif, air]))


def _s5_coefs(a_re, a_im):
    c1 = jnp.concatenate([a_re, a_re], -1)
    c2 = jnp.concatenate([-a_im, a_im], -1)
    return tuple(c.reshape(GROUPS // GB, 1, GB * 2 * NSTATE) for c in (c1[0], c2[0], c1[1], c2[1]))


def _swap(s):
    w = s.shape[1]
    lane = lax.broadcasted_iota(jnp.int32, s.shape, 1)
    return jnp.where(lane % (2 * NSTATE) < NSTATE, pltpu.roll(s, w - NSTATE, 1), pltpu.roll(s, NSTATE, 1))


def _group_lanes(g):
    return slice(g * 2 * NSTATE, (g + 1) * 2 * NSTATE)


def _s5_states(nc, u_ref, pin_ref, coef, vf, vr, wf, wr, sf, sr):
    c1f, c2f, c1r, c2r = coef
    for g in range(GB):
        v = jnp.dot(u_ref[g], pin_ref[g], preferred_element_type=f32)
        vf[:, _group_lanes(g)] = v[:, :2 * NSTATE]
        vr[:, _group_lanes(g)] = v[:, 2 * NSTATE:]
    wf[...] = _swap(vf[...])
    wr[...] = _swap(vr[...])

    def step(i, carry):
        s_f, t_f, s_r, t_r = carry
        kf, kr = pl.ds(i, 1), pl.ds(nc - 1 - i, 1)
        sf[kf, :] = s_f
        sr[kr, :] = s_r
        s_f, t_f = c1f * s_f + c2f * t_f + vf[kf, :], c1f * t_f - c2f * s_f + wf[kf, :]
        s_r, t_r = c1r * s_r + c2r * t_r + vr[kr, :], c1r * t_r - c2r * s_r + wr[kr, :]
        return s_f, t_f, s_r, t_r

    z = jnp.zeros((1, GB * 2 * NSTATE), f32)
    lax.fori_loop(0, nc, step, (z, z, z, z))


def _s5_core_fwd(ug, msum, pin, pout, coefs):
    nc = ug.shape[1]
    n = CHUNK * GCH

    def body(u_ref, m_ref, pin_ref, pout_ref, c1f, c2f, c1r, c2r, y_ref, vf, vr, wf, wr, sf, sr):
        coef = (c1f[...], c2f[...], c1r[...], c2r[...])
        _s5_states(nc, u_ref, pin_ref, coef, vf, vr, wf, wr, sf, sr)
        for g in range(GB):
            s_in = jnp.concatenate([sf[:, _group_lanes(g)], sr[:, _group_lanes(g)]], axis=1).astype(bf16)
            y_ref[g] = (jnp.dot(u_ref[g], m_ref[g], preferred_element_type=f32)
                        + jnp.dot(s_in, pout_ref[g], preferred_element_type=f32)).astype(bf16)

    seq = pl.BlockSpec((GB, nc, n), lambda i: (i, 0, 0))
    mat = pl.BlockSpec((GB, n, n), lambda i: (i, 0, 0))
    cf = pl.BlockSpec((None, 1, GB * 2 * NSTATE), lambda i: (i, 0, 0))
    scr = pltpu.VMEM((nc, GB * 2 * NSTATE), f32)
    return pl.pallas_call(
        body, grid=(GROUPS // GB,), in_specs=[seq, mat, mat, mat, cf, cf, cf, cf], out_specs=seq,
        out_shape=jax.ShapeDtypeStruct((GROUPS, nc, n), bf16), scratch_shapes=[scr] * 6,
        name="s5_core_fwd", compiler_params=_params(("parallel",)))(ug, msum, pin, pout, *coefs)


def _s5_core_bwd(ug, dyg, msum, pin, pout, coefs):
    nc = ug.shape[1]
    n = CHUNK * GCH

    def body(u_ref, dy_ref, m_ref, pin_ref, pout_ref, c1f, c2f, c1r, c2r,
             du_ref, dm_ref, dpin_ref, dpout_ref, a1f_ref, a2f_ref, a1r_ref, a2r_ref, vf, vr, wf, wr, sf, sr):
        coef = (c1f[...], c2f[...], c1r[...], c2r[...])
        _s5_states(nc, u_ref, pin_ref, coef, vf, vr, wf, wr, sf, sr)
        for g in range(GB):
            s_in = jnp.concatenate([sf[:, _group_lanes(g)], sr[:, _group_lanes(g)]], axis=1).astype(bf16)
            dy = dy_ref[g]
            ds = lax.dot_general(dy, pout_ref[g], NT, preferred_element_type=f32)
            vf[:, _group_lanes(g)] = ds[:, :2 * NSTATE]
            vr[:, _group_lanes(g)] = ds[:, 2 * NSTATE:]
            dpout_ref[g] = lax.dot_general(s_in, dy, TN, preferred_element_type=f32)
            dm_ref[g] = lax.dot_general(u_ref[g], dy, TN, preferred_element_type=f32)

        wf[...] = _swap(vf[...])
        wr[...] = _swap(vr[...])
        k1f, k2f, k1r, k2r = coef[0], -coef[1], coef[2], -coef[3]

        def step(i, carry):
            g_f, h_f, g_r, h_r, a1f, b2f, a1r, b2r = carry
            kf, kr = pl.ds(nc - 1 - i, 1), pl.ds(i, 1)
            s_f, s_r = sf[kf, :], sr[kr, :]
            sf[kf, :] = g_f
            sr[kr, :] = g_r
            a1f, b2f = a1f + g_f * s_f, b2f + h_f * s_f
            a1r, b2r = a1r + g_r * s_r, b2r + h_r * s_r
            g_f, h_f = vf[kf, :] + k1f * g_f + k2f * h_f, wf[kf, :] + k1f * h_f - k2f * g_f
            g_r, h_r = vr[kr, :] + k1r * g_r + k2r * h_r, wr[kr, :] + k1r * h_r - k2r * g_r
            return g_f, h_f, g_r, h_r, a1f, b2f, a1r, b2r

        z = jnp.zeros((1, GB * 2 * NSTATE), f32)
        _, _, _, _, a1f, b2f, a1r, b2r = lax.fori_loop(0, nc, step, (z,) * 8)
        a1f_ref[...], a2f_ref[...], a1r_ref[...], a2r_ref[...] = a1f, _swap(b2f), a1r, _swap(b2r)
        for g in range(GB):
            dv = jnp.concatenate([sf[:, _group_lanes(g)], sr[:, _group_lanes(g)]], axis=1).astype(bf16)
            du_ref[g] = (lax.dot_general(dy_ref[g], m_ref[g], NT, preferred_element_type=f32)
                         + lax.dot_general(dv, pin_ref[g], NT, preferred_element_type=f32)).astype(bf16)
            dpin_ref[g] = lax.dot_general(u_ref[g], dv, TN, preferred_element_type=f32)

    seq = pl.BlockSpec((GB, nc, n), lambda i: (i, 0, 0))
    mat = pl.BlockSpec((GB, n, n), lambda i: (i, 0, 0))
    cf = pl.BlockSpec((None, 1, GB * 2 * NSTATE), lambda i: (i, 0, 0))
    scr = pltpu.VMEM((nc, GB * 2 * NSTATE), f32)
    mat_s = jax.ShapeDtypeStruct((GROUPS, n, n), f32)
    cf_s = jax.ShapeDtypeStruct((GROUPS // GB, 1, GB * 2 * NSTATE), f32)
    return pl.pallas_call(
        body, grid=(GROUPS // GB,), in_specs=[seq, seq, mat, mat, mat, cf, cf, cf, cf],
        out_specs=[seq, mat, mat, mat, cf, cf, cf, cf],
        out_shape=[jax.ShapeDtypeStruct((GROUPS, nc, n), bf16), mat_s, mat_s, mat_s, cf_s, cf_s, cf_s, cf_s],
        scratch_shapes=[scr] * 6, name="s5_core_bwd",
        compiler_params=_params(("parallel",)))(ug, dyg, msum, pin, pout, *coefs)


def _block_movers():
    a_in, l_in = jnp.divmod(jnp.arange(GB * 128, dtype=jnp.int32), 128)
    a_out, c_out = jnp.divmod(jnp.arange(128, dtype=jnp.int32), GCH)
    j = jnp.arange(GB, dtype=jnp.int32)[:, None, None]
    hit = (a_in[None, :, None] == a_out[None, None, :]) & (l_in[None, :, None] == GCH * j + c_out[None, None, :])
    return hit.astype(bf16)


def _to_groups(x, movers, mask):
    n = x.shape[0]
    nc = n // CHUNK
    half = CHUNK // 2

    def body(x_ref, mv_ref, o_ref):
        keep = lax.broadcasted_iota(jnp.int32, (nc, 1), 0) >= PAD // CHUNK
        steps = [x_ref[pl.ds(t, nc, stride=CHUNK), :] for t in range(CHUNK)]
        if mask:
            steps = [jnp.where(keep, s, 0.0) for s in steps]
        lo = jnp.concatenate(steps[:half], axis=1).astype(bf16)
        hi = jnp.concatenate(steps[half:], axis=1).astype(bf16)
        for g in range(GB):
            o_ref[g] = jnp.concatenate([jnp.dot(lo, mv_ref[g], preferred_element_type=f32),
                                        jnp.dot(hi, mv_ref[g], preferred_element_type=f32)], axis=1).astype(bf16)

    return pl.pallas_call(
        body, grid=(GROUPS // GB,),
        in_specs=[pl.BlockSpec((n, 128), lambda i: (0, i)), pl.BlockSpec(movers.shape, lambda i: (0, 0, 0))],
        out_specs=pl.BlockSpec((GB, nc, CHUNK * GCH), lambda i: (i, 0, 0)),
        out_shape=jax.ShapeDtypeStruct((GROUPS, nc, CHUNK * GCH), bf16), name="s5_to_groups",
        compiler_params=_params(("parallel",)))(x, movers)


def _from_groups(y, movers, base=None):
    nc = y.shape[1]
    n = nc * CHUNK
    half = CHUNK // 2

    def body(*refs):
        y_ref, mv_ref = refs[:2]
        o_ref = refs[-1]
        keep = lax.broadcasted_iota(jnp.int32, (nc, 1), 0) >= PAD // CHUNK
        lo = jnp.concatenate([y_ref[g][:, :128] for g in range(GB)], axis=1)
        hi = jnp.concatenate([y_ref[g][:, 128:] for g in range(GB)], axis=1)
        for t in range(CHUNK):
            rows = pl.ds(t, nc, stride=CHUNK)
            v = jnp.dot(lo if t < half else hi, mv_ref[t % half], preferred_element_type=f32)
            if base is not None:
                v = refs[2][rows, :] + jnp.where(keep, v, 0.0)
            o_ref[rows, :] = v

    tok = pl.BlockSpec((n, 128), lambda i: (0, i))
    args = (y, movers) if base is None else (y, movers, base)
    return pl.pallas_call(
        body, grid=(GROUPS // GB,),
        in_specs=[pl.BlockSpec((GB, nc, CHUNK * GCH), lambda i: (i, 0, 0)),
                  pl.BlockSpec(movers.shape, lambda i: (0, 0, 0))] + ([] if base is None else [tok]),
        out_specs=tok, out_shape=jax.ShapeDtypeStruct((n, D), f32), name="s5_from_groups",
        compiler_params=_params(("parallel",)))(*args)


def _gelu(y):
    return 0.5 * y * (1.0 + lax.erf(y * (2.0 ** -0.5)))


def _gelu_grad(y):
    return 0.5 * (1.0 + lax.erf(y * (2.0 ** -0.5))) + y * jnp.exp(-0.5 * y * y) * (1.0 / math.sqrt(2.0 * math.pi))


def _s5_fwd(h, movers, mats, d_skip, w_glu):
    msum, pin, pout, a_re, a_im = mats
    coefs = _s5_coefs(a_re, a_im)
    ug = _to_groups(h, movers, mask=True)
    ys = _from_groups(_s5_core_fwd(ug, msum.astype(bf16), pin.astype(bf16), pout.astype(bf16), coefs), movers)

    def post(i, ys, h, d):
        y = ys + d * h
        return (y, _gelu(y)), ()

    y, g_bf = _ew("s5_gelu", post, [ys, h], [d_skip], [(D, f32), (D, bf16)])
    gw = _mm("s5_glu_mm", g_bf, w_glu)

    def glu(i, y, gw):
        return (_gelu(y) * _sigmoid(gw),), ()

    z_bf = _ew("s5_glu", glu, [y, gw], [], [(D, bf16)])[0]
    return z_bf, (ug, y, g_bf, gw, z_bf)


def _s5_bwd(dmix_bf, h, movers, saved, mats, vjp_mats, d_skip, w_glu, w_out):
    ug, y, g_bf, gw, z_bf = saved
    msum, pin, pout, a_re, a_im = mats
    coefs = _s5_coefs(a_re, a_im)
    dz = _mm("s5_dz_mm", dmix_bf, w_out, trans_b=True)
    d_w_out = _mm_tn("s5_dwout", z_bf, dmix_bf)

    def dglu(i, dz, y, gw):
        g, s = _gelu(y), _sigmoid(gw)
        return (dz * g * s * (1.0 - s), dz * s), ()

    dgw_bf, dg1 = _ew("s5_dglu", dglu, [dz, y, gw], [], [(D, bf16), (D, f32)])
    d_w_glu = _mm_tn("s5_dwglu", g_bf, dgw_bf)
    dg2 = _mm("s5_dg_mm", dgw_bf, w_glu, trans_b=True)

    def dgelu(i, dg1, dg2, y, h, d):
        dy = (dg1 + dg2) * _gelu_grad(y)
        return (dy, dy * d), (jnp.sum(dy * h, axis=0, keepdims=True),)

    dy, dh_skip, dd = _ew("s5_dgelu", dgelu, [dg1, dg2, y, h], [d_skip], [(D, f32), (D, f32)], [(1, D)])
    dug, dm, dpin, dpout, a1f, a2f, a1r, a2r = _s5_core_bwd(
        ug, _to_groups(dy, movers, mask=False), msum.astype(bf16), pin.astype(bf16), pout.astype(bf16), coefs)
    dh = _from_groups(dug, movers, base=dh_skip)
    a1 = jnp.stack([a1f, a1r]).reshape(2, GROUPS, 2 * NSTATE)
    a2 = jnp.stack([a2f, a2r]).reshape(2, GROUPS, 2 * NSTATE)
    da_re = a1[..., :NSTATE] + a1[..., NSTATE:]
    da_im = a2[..., NSTATE:] - a2[..., :NSTATE]
    d_params = vjp_mats((dm, dpin, dpout, da_re, da_im))
    return dh, d_params, dd[0], d_w_glu, d_w_out


def _rope_tables(n):
    row = jnp.arange(n, dtype=jnp.int32) - OFF
    real = row >= 0
    rid = jnp.where(real, row // GRID_W, 0).astype(f32)
    cid = jnp.where(real, row % GRID_W, 0).astype(f32)
    half = HD // 2
    inv = ROPE_THETA ** (-jnp.arange(0, half, 2, dtype=f32) / half)
    ar, ac = rid[:, None] * inv[None, :], cid[:, None] * inv[None, :]
    cos = jnp.concatenate([jnp.cos(ar), jnp.cos(ar), jnp.cos(ac), jnp.cos(ac)], axis=1)
    sin = jnp.concatenate([-jnp.sin(ar), jnp.sin(ar), -jnp.sin(ac), jnp.sin(ac)], axis=1)
    return jnp.tile(cos, (1, 2)), jnp.tile(sin, (1, 2))


def _head_mats():
    head = jnp.arange(QW, dtype=jnp.int32)[:, None] // HD == jnp.arange(128, dtype=jnp.int32)[None, :]
    return head.astype(f32) * (1.0 / HD), head.astype(f32).T


def _rot(v):
    w = v.shape[1]
    lane = lax.broadcasted_iota(jnp.int32, v.shape, 1)
    return jnp.where(lane % 32 < 16, pltpu.roll(v, w - 16, 1), pltpu.roll(v, 16, 1))


def _head_mean(v, e, et):
    w = v.shape[1]
    m = jnp.dot(v, e[:w], preferred_element_type=f32, precision=HIGH)
    return m, et[:, :w]


def _rms_rope(t, gain, e, et, cos, sin):
    w = t.shape[1]
    ms, spread = _head_mean(t * t, e, et)
    rs = jnp.dot(lax.rsqrt(ms + QK_EPS), spread, preferred_element_type=f32, precision=HIGH)
    n0 = t * rs
    n = n0 * gain
    reps = w // 128
    return n * jnp.tile(cos, (1, reps)) + _rot(n) * jnp.tile(sin, (1, reps))


def _rms_rope_bwd(dout, t, gain, e, et, cos, sin):
    w = t.shape[1]
    reps = w // 128
    ms, spread = _head_mean(t * t, e, et)
    rs = jnp.dot(lax.rsqrt(ms + QK_EPS), spread, preferred_element_type=f32, precision=HIGH)
    n0 = t * rs
    dn = dout * jnp.tile(cos, (1, reps)) + _rot(dout * jnp.tile(sin, (1, reps)))
    dn0 = dn * gain
    mm, _ = _head_mean(dn0 * n0, e, et)
    corr = jnp.dot(mm, spread, preferred_element_type=f32, precision=HIGH)
    return rs * (dn0 - n0 * corr), jnp.sum(dn * n0, axis=0, keepdims=True)


def _qk_fwd(qkv, qg, kg, e, et, cos, sin):
    def fn(i, qkv, cos, sin, qg, kg, e, et):
        q = _rms_rope(qkv[:, :QW], qg, e, et, cos, sin) * Q_SCALE
        k = _rms_rope(qkv[:, QW:QW + KW], kg, e, et, cos, sin)
        return (q, k, qkv[:, QW + KW:]), ()

    return _ew("qk_rope", fn, [qkv, cos, sin], [qg, kg, e, et], [(QW, bf16), (KW, bf16), (KW, bf16)])


def _qk_bwd(qkv, dq, dk, dv, qg, kg, e, et, cos, sin):
    def fn(i, qkv, cos, sin, dq, dk, dv, qg, kg, e, et):
        dtq, dgq = _rms_rope_bwd(dq * (HD ** -0.5), qkv[:, :QW], qg, e, et, cos, sin)
        dtk, dgk = _rms_rope_bwd(dk * math.log(2.0), qkv[:, QW:QW + KW], kg, e, et, cos, sin)
        return (jnp.concatenate([dtq, dtk, dv], axis=1),), (dgq, dgk)

    return _ew("qk_rope_bwd", fn, [qkv, cos, sin, dq, dk, dv], [qg, kg, e, et], [(QKVW, bf16)], [(1, QW), (1, KW)])


def _to_heads(a, nh):
    return a.reshape(a.shape[0], nh, HD).transpose(1, 0, 2)


def _from_heads(a):
    return a.transpose(1, 0, 2).reshape(a.shape[1], a.shape[0] * HD)


def _masked_first(s, c):
    if c:
        return s
    col = lax.broadcasted_iota(jnp.int32, (1, s.shape[1]), 1)
    return jnp.where(col >= PAD, s, NEG)


def _flash_fwd(q, k, v1, tq=ROW_TILE, tc=KEY_CHUNK):
    n = q.shape[1]
    nc = n // tc
    pair = 2

    def body(q_ref, k_ref, v_ref, o_ref, ob_ref, lse_ref):
        def scores(h, c):
            ks = k_ref[0, pl.ds(c * tc, tc), :]
            return _masked_first(lax.dot_general(q_ref[h], ks, NT, preferred_element_type=f32), c)

        m = [jnp.full((tq, 1), NEG, f32) for _ in range(pair)]
        acc = [jnp.zeros((tq, 2 * HD), f32) for _ in range(pair)]
        nxt = [scores(h, 0) for h in range(pair)]
        for c in range(nc):
            for h in range(pair):
                s = nxt[h]
                if c + 1 < nc:
                    nxt[h] = scores(h, c + 1)
                m_new = jnp.maximum(m[h], jnp.max(s, axis=1, keepdims=True))
                p = jnp.exp2(s - m_new)
                acc[h] = jnp.exp2(m[h] - m_new) * acc[h] + jnp.dot(p.astype(bf16), v_ref[0, pl.ds(c * tc, tc), :],
                                                                     preferred_element_type=f32)
                m[h] = m_new
        ls = [a[:, HD:HD + 1] for a in acc]
        o = jnp.concatenate([a[:, :HD] / l for a, l in zip(acc, ls)], axis=1)
        o_ref[...] = o
        ob_ref[...] = o.astype(bf16)
        lse_ref[0] = jnp.concatenate([mh + jnp.log2(l) for mh, l in zip(m, ls)], axis=1)

    kv_of = NQ // NKV // pair
    tok = pl.BlockSpec((tq, pair * HD), lambda hp, i: (i, hp))
    return pl.pallas_call(
        body, grid=(NQ // pair, n // tq),
        in_specs=[pl.BlockSpec((pair, tq, HD), lambda hp, i: (hp, i, 0)),
                  pl.BlockSpec((1, n, HD), lambda hp, i: (hp // kv_of, 0, 0)),
                  pl.BlockSpec((1, n, 2 * HD), lambda hp, i: (hp // kv_of, 0, 0))],
        out_specs=[tok, tok, pl.BlockSpec((1, tq, pair), lambda hp, i: (hp, i, 0))],
        out_shape=[jax.ShapeDtypeStruct((n, NQ * HD), f32), jax.ShapeDtypeStruct((n, NQ * HD), bf16),
                   jax.ShapeDtypeStruct((NQ // pair, n, pair), f32)],
        name="flash_fwd", compiler_params=_params(("parallel", "parallel")))(q, k, v1)


def _flash_bwd(q, k, kt, v, do, lse_row, delta_row, tk=ROW_TILE, tc=KEY_CHUNK):
    n = q.shape[1]
    nc = n // tc
    grp = NQ // NKV
    pair = 2

    def body(q_ref, do_ref, lse_ref, delta_ref, k_ref, kt_ref, v_ref, dqt_ref, dk_ref, dv_ref):
        j, g = pl.program_id(1), pl.program_id(2)
        kb, vb, ktb = k_ref[0], v_ref[0], kt_ref[0]
        valid = lax.broadcasted_iota(jnp.int32, (tk, 1), 0) + j * tk >= PAD

        @pl.when(j == 0)
        def _():
            for h in range(pair):
                dqt_ref[pair * g + h] = jnp.zeros((HD, n), f32)

        def products(h, c):
            rows = pl.ds(c * tc, tc)
            return (lax.dot_general(kb, q_ref[h, rows, :], NT, preferred_element_type=f32),
                    lax.dot_general(vb, do_ref[h, rows, :], NT, preferred_element_type=f32))

        dk = jnp.zeros((tk, HD), f32)
        dv = jnp.zeros((tk, HD), f32)
        nxt = [products(h, 0) for h in range(pair)]
        for c in range(nc):
            rows = pl.ds(c * tc, tc)
            for h in range(pair):
                st, dpt = nxt[h]
                if c + 1 < nc:
                    nxt[h] = products(h, c + 1)
                pt = jnp.exp2(jnp.where(valid, st, NEG) - lse_ref[h, :, rows])
                dv = dv + jnp.dot(pt.astype(bf16), do_ref[h, rows, :], preferred_element_type=f32)
                dst = (pt * (dpt - delta_ref[h, :, rows])).astype(bf16)
                dk = dk + jnp.dot(dst, q_ref[h, rows, :], preferred_element_type=f32)
                dqt_ref[pair * g + h, :, rows] += jnp.dot(ktb, dst, preferred_element_type=f32)

        @pl.when(g == 0)
        def _():
            dk_ref[0] = dk
            dv_ref[0] = dv

        @pl.when(g > 0)
        def _():
            dk_ref[0] += dk
            dv_ref[0] += dv

    steps = grp // pair
    hspec = pl.BlockSpec((pair, n, HD), lambda h, j, g: (h * steps + g, 0, 0))
    rspec = pl.BlockSpec((pair, 1, n), lambda h, j, g: (h * steps + g, 0, 0))
    kspec = pl.BlockSpec((1, tk, HD), lambda h, j, g: (h, j, 0))
    return pl.pallas_call(
        body, grid=(NKV, n // tk, steps),
        in_specs=[hspec, hspec, rspec, rspec, kspec, pl.BlockSpec((1, HD, tk), lambda h, j, g: (h, 0, j)), kspec],
        out_specs=[pl.BlockSpec((grp, HD, n), lambda h, j, g: (h, 0, 0)), kspec, kspec],
        out_shape=[jax.ShapeDtypeStruct((NQ, HD, n), f32)] + [jax.ShapeDtypeStruct((NKV, n, HD), f32)] * 2,
        name="flash_bwd", compiler_params=_params(("parallel", "arbitrary", "arbitrary")))(
            q, do, lse_row, delta_row, k, kt, v)


def _attn_fwd(h_bf, w_qkv_t, qg, kg, tabs):
    e, et, cos, sin = tabs
    qkv = _mm("attn_qkv_mm", h_bf, w_qkv_t, trans_b=True)
    q_bf, k_bf, v_bf = _qk_fwd(qkv, qg, kg, e, et, cos, sin)
    q16, k4, v4 = _to_heads(q_bf, NQ), _to_heads(k_bf, NKV), _to_heads(v_bf, NKV)
    ones = jnp.zeros((NKV, v4.shape[1], HD), bf16).at[:, :, 0].set(1.0)
    o, o_bf, lse = _flash_fwd(q16, k4, jnp.concatenate([v4, ones], axis=2))
    lse_row = lse.transpose(0, 2, 1).reshape(NQ, 1, qkv.shape[0])
    return o_bf, (qkv, q16, k4, v4, o, lse_row, o_bf)


def _attn_bwd(dmix_bf, h_bf, saved, w_qkv_t, qg, kg, w_out, tabs):
    e, et, cos, sin = tabs
    qkv, q16, k4, v4, o, lse_row, o_bf = saved
    n = qkv.shape[0]
    do = _mm("attn_do_mm", dmix_bf, w_out, trans_b=True, out_dtype=bf16)
    d_w_out = _mm_tn("attn_dwout", o_bf, dmix_bf)

    def head_dots(i, do, o, e):
        return (jnp.dot(do.astype(f32) * o, e, preferred_element_type=f32, precision=HIGH) * HD,), ()

    delta = _ew("attn_delta", head_dots, [do, o], [e], [(128, f32)])[0]
    dqt, dk4, dv4 = _flash_bwd(q16, k4, k4.transpose(0, 2, 1), v4, _to_heads(do, NQ), lse_row,
                               delta[:, :NQ].T.reshape(NQ, 1, n))
    dq = dqt.transpose(2, 0, 1).reshape(n, QW)
    dqkv_bf, dgq, dgk = _qk_bwd(qkv, dq, _from_heads(dk4), _from_heads(dv4), qg, kg, e, et, cos, sin)
    d_w_qkv_t = _mm_tn("attn_dwqkv", dqkv_bf, h_bf)
    dh = _mm("attn_dh_mm", dqkv_bf, w_qkv_t)
    return dh, d_w_qkv_t, dgq.reshape(NQ, HD).sum(0), dgk.reshape(NKV, HD).sum(0), d_w_out


def _all_gather(name, shard):
    def body(x_ref, out_ref, send_sems, recv_sems, local_sem):
        x, y, c = lax.axis_index("x"), lax.axis_index("y"), lax.axis_index("c")
        me, sibling = (x, y, c), (x, y, 1 - c)
        chips = [(1 - x, y), (x, 1 - y), (1 - x, 1 - y)]

        def slot(px, py, pc):
            return out_ref.at[4 * px + 2 * py + pc]

        def copy(k, block, to, src=None):
            return pltpu.make_async_remote_copy(
                src_ref=slot(*block) if src is None else src, dst_ref=slot(*block),
                send_sem=send_sems.at[k], recv_sem=recv_sems.at[k], device_id=to, device_id_type=MESH)

        mine = pltpu.make_async_copy(x_ref, slot(*me), local_sem)
        mine.start()
        first = [copy(0, me, sibling, src=x_ref)]
        first += [copy(1 + j, me, (*chip, c), src=x_ref) for j, chip in enumerate(chips)]
        for cp in first:
            cp.start()
        passed = [copy(4 + j, (*chip, c), sibling) for j, chip in enumerate(chips)]
        for j, chip in enumerate(chips):
            copy(1 + j, (*chip, c), me).wait_recv()
            passed[j].start()
        copy(0, sibling, me).wait_recv()
        for j, chip in enumerate(chips):
            copy(4 + j, (*chip, 1 - c), me).wait_recv()
        for cp in first + passed:
            cp.wait_send()
        mine.wait()

    return pl.pallas_call(
        body, out_shape=jax.ShapeDtypeStruct((8,) + shard.shape, shard.dtype), in_specs=[ANY], out_specs=ANY,
        scratch_shapes=[pltpu.SemaphoreType.DMA((7,)), pltpu.SemaphoreType.DMA((7,)), pltpu.SemaphoreType.DMA],
        name=name)(shard)


def _swap_sibling(name, theirs):
    k = len(theirs)

    def body(*refs):
        src, dst, send_sems, recv_sems = refs[:k], refs[k:2 * k], refs[2 * k], refs[2 * k + 1]
        x, y, c = lax.axis_index("x"), lax.axis_index("y"), lax.axis_index("c")
        copies = [pltpu.make_async_remote_copy(src_ref=src[j], dst_ref=dst[j], send_sem=send_sems.at[j],
                                               recv_sem=recv_sems.at[j], device_id=(x, y, 1 - c), device_id_type=MESH)
                  for j in range(k)]
        for cp in copies:
            cp.start()
        for cp in copies:
            cp.wait()

    return pl.pallas_call(
        body, out_shape=[jax.ShapeDtypeStruct(a.shape, a.dtype) for a in theirs], in_specs=[ANY] * k,
        out_specs=[ANY] * k, scratch_shapes=[pltpu.SemaphoreType.DMA((k,)), pltpu.SemaphoreType.DMA((k,))],
        name=name)(*theirs)


def _exchange_chips(name, parts):
    k = len(parts)

    def body(*refs):
        p_refs, t_refs = refs[:k], refs[k:2 * k]
        send_sems, recv_sems, local_sems = refs[2 * k:]
        x, y, c = lax.axis_index("x"), lax.axis_index("y"), lax.axis_index("c")
        q = 2 * x + y
        copies = []
        for j in range(k):
            copies.append(pltpu.make_async_copy(p_refs[j].at[q], t_refs[j].at[q], local_sems.at[j]))
            for hop in (1, 2, 3):
                tx, ty = x ^ (hop >> 1), y ^ (hop & 1)
                copies.append(pltpu.make_async_remote_copy(
                    src_ref=p_refs[j].at[2 * tx + ty], dst_ref=t_refs[j].at[q], send_sem=send_sems.at[3 * j + hop - 1],
                    recv_sem=recv_sems.at[3 * j + hop - 1], device_id=(tx, ty, c), device_id_type=MESH))
        for cp in copies:
            cp.start()
        for cp in copies:
            cp.wait()

    return pl.pallas_call(
        body, out_shape=[jax.ShapeDtypeStruct(a.shape, a.dtype) for a in parts], in_specs=[ANY] * k,
        out_specs=[ANY] * k,
        scratch_shapes=[pltpu.SemaphoreType.DMA((3 * k,)), pltpu.SemaphoreType.DMA((3 * k,)),
                        pltpu.SemaphoreType.DMA((k,))],
        name=name)(*parts)


def _reduce_scatter(mine, theirs):
    got = _swap_sibling("rs_sibling", list(theirs))
    parts = []
    for a, b, dt, nm in zip(mine, got, (bf16, f32), ("rs_add2", "rs_add2_small")):
        rows = 4 * a.shape[1]
        parts.append(_ew(nm, lambda i, a, b: ((a + b,), ()), [a.reshape(rows, D), b.reshape(rows, D)], [],
                         [(D, dt)], tile=RS_TILE)[0].reshape(a.shape))
    ts = _exchange_chips("rs_chips", parts)

    def add4(i, a, b, c, d):
        return ((((a.astype(f32) + b.astype(f32)) + c.astype(f32)) + d.astype(f32),), ())

    return [_ew(nm, add4, [(t, 0), (t, 1), (t, 2), (t, 3)], [], [(D, f32)], tile=RS_TILE)[0]
            for t, nm in zip(ts, ("rs_add4", "rs_add4_small"))]


def _pack_rows(parts, rows):
    flat = jnp.concatenate([p.reshape(-1) for p in parts])
    return jnp.pad(flat, (0, rows * D - flat.shape[0])).reshape(rows, D)


def _unpack(flat, shapes):
    out, off = [], 0
    for s in shapes:
        n = math.prod(s)
        out.append(flat[off:off + n].reshape(s))
        off += n
    return out


def _mat_rows(block, transposed, blk):
    a = jnp.swapaxes(block, 1, 2) if transposed else block
    a = jnp.pad(a, ((0, 0), (0, blk - a.shape[1]), (0, 0)))
    return a.reshape(-1, D)


def _mat_block(rows, transposed, blk, real):
    a = rows.reshape(-1, blk, D)[:, :real]
    return jnp.swapaxes(a, 1, 2) if transposed else a


def _mat_full(gathered, blk):
    layers = gathered.shape[1] // blk
    return gathered.reshape(8, layers, blk, D).transpose(1, 0, 2, 3).reshape(layers, 8 * blk, D)


def _vec_full(gathered):
    return gathered.transpose(1, 0, 2).reshape(gathered.shape[1], D)


def _grad_slots(full, small, cc):
    def halves(a, blk):
        a4 = a.reshape(4, 2, blk, D)
        return [lax.dynamic_index_in_dim(a4, sel, axis=1, keepdims=False) for sel in (cc, 1 - cc)]

    mine, theirs = [], []
    for name, _, blk, _ in MATS:
        for layer in full[name]:
            a, b = halves(layer, blk)
            mine.append(a)
            theirs.append(b)
    a, b = halves(small, REP_PIECE)
    return (jnp.concatenate(mine, axis=1), a), (jnp.concatenate(theirs, axis=1), b)


def _local_step(x0, target0, w, fw):
    seq = x0.shape[0]
    n = OFF + seq
    movers = _block_movers()
    h = jnp.concatenate([jnp.zeros((PAD, D), f32), fw['meta_tokens'], x0], axis=0)
    h_bf = h.astype(bf16)
    tabs = _head_mats() + _rope_tables(n)
    qg = [jnp.tile(w['attn_q_gain'][j], NQ)[None, :] for j in range(2)]
    kg = [jnp.tile(w['attn_k_gain'][j], NKV)[None, :] for j in range(2)]
    s5_names = ['s5_lambda_re', 's5_lambda_im', 's5_log_dt', 's5_b_re', 's5_b_im', 's5_c_re', 's5_c_im']
    s5_mats, s5_vjp = [], []
    for j in range(2):
        mats, vjp = jax.vjp(_s5_mats, *[w[k][j] for k in s5_names])
        s5_mats.append(mats)
        s5_vjp.append(vjp)
    saved = []
    for i in range(DEPTH):
        j = i // 2
        if i % 2 == 0:
            mixed, sv = _s5_fwd(h, movers, s5_mats[j], w['s5_d'][j][None, :], (fw['s5_w_glu'], j))
            w_out = (fw['s5_w_out'], j)
        else:
            mixed, sv = _attn_fwd(h_bf, (fw['attn_w_qkv'], j), qg[j], kg[j], tabs)
            w_out = (fw['attn_w_out'], j)
        r1, h1, h1_bf = _mm_ln("mixer_out_ln", mixed, w_out, h, fw['ln_gain'][i, 0][None, :], fw['ln_bias'][i, 0][None, :])
        act, silu_g, up_ds = _ffn_up(h1_bf, (fw['ffn_w_gate'], i), (fw['ffn_w_up'], i))
        r2, h2, h2_bf = _mm_ln("ffn_down_ln", act, (fw['ffn_w_down'], i), h1, fw['ln_gain'][i, 1][None, :],
                               fw['ln_bias'][i, 1][None, :])
        saved.append((h, h_bf, sv, r1, h1_bf, silu_g, up_ds, act, r2))
        h, h_bf = h2, h2_bf

    d_b, sq = _loss_grad(h, target0)
    loss = 0.5 * jnp.sum(sq) * (1.0 / D)

    grads = {k: [None] * (DEPTH if k.startswith('ffn') else 2) for k in WEIGHTS}
    d_ln_gain = [[None, None] for _ in range(DEPTH)]
    d_ln_bias = [[None, None] for _ in range(DEPTH)]
    d_a = None
    for i in reversed(range(DEPTH)):
        j = i // 2
        h_in, h_in_bf, sv, r1, h1_bf, silu_g, up_ds, act, r2 = saved[i]
        dr2, dr2_bf, dg, db = _ln_bwd(d_a, d_b, r2, fw['ln_gain'][i, 1][None, :])
        d_ln_gain[i][1], d_ln_bias[i][1] = dg[0], db[0]
        dgate, dup = _ffn_dup(dr2_bf, (fw['ffn_w_down'], i), silu_g, up_ds)
        grads['ffn_w_down'][i] = _mm_tn("ffn_dwdown", act, dr2_bf, tk=DFFP // 2)
        grads['ffn_w_gate'][i] = _mm_tn("ffn_dwgate", dgate, h1_bf, tk=DFFP // 2)
        grads['ffn_w_up'][i] = _mm_tn("ffn_dwup", dup, h1_bf, tk=DFFP // 2)
        dh1 = _mm2("ffn_dh_mm", dgate, (fw['ffn_w_gate'], i), dup, (fw['ffn_w_up'], i))
        dr1, dr1_bf, dg, db = _ln_bwd(dr2, dh1, r1, fw['ln_gain'][i, 0][None, :])
        d_ln_gain[i][0], d_ln_bias[i][0] = dg[0], db[0]
        if i % 2 == 0:
            dh, d_par, dd, d_w_glu, d_w_out = _s5_bwd(dr1_bf, h_in, movers, sv, s5_mats[j], s5_vjp[j],
                                                      w['s5_d'][j][None, :], (fw['s5_w_glu'], j), (fw['s5_w_out'], j))
            for k, g in zip(s5_names, d_par):
                grads[k][j] = g
            grads['s5_d'][j], grads['s5_w_glu'][j], grads['s5_w_out'][j] = dd, d_w_glu, d_w_out
        else:
            dh, d_w_qkv, dgq, dgk, d_w_out = _attn_bwd(dr1_bf, h_in_bf, sv, (fw['attn_w_qkv'], j), qg[j], kg[j],
                                                       (fw['attn_w_out'], j), tabs)
            grads['attn_w_qkv'][j], grads['attn_w_out'][j] = d_w_qkv, d_w_out
            grads['attn_q_gain'][j], grads['attn_k_gain'][j] = dgq, dgk
        d_a, d_b = dr1, dh
    dh0 = _ew("dh0", lambda i, a, b: ((ALPHA * a + b,), ()), [d_a, d_b], [], [(D, f32)])[0]
    mats = {m[0] for m in MATS}
    full = {k: (v if k in mats else jnp.stack(v)) for k, v in grads.items() if v[0] is not None}
    full['meta_tokens'] = dh0[PAD:OFF]
    full['ln_gain'] = jnp.stack([jnp.stack(r) for r in d_ln_gain])
    full['ln_bias'] = jnp.stack([jnp.stack(r) for r in d_ln_bias])

    return loss, dh0[OFF:], full


def kernel(x, meta_tokens, s5_lambda_re, s5_lambda_im, s5_log_dt, s5_b_re, s5_b_im, s5_c_re, s5_c_im, s5_d, s5_w_glu, s5_w_out, attn_w_qkv, attn_q_gain, attn_k_gain, attn_w_out, ffn_w_gate, ffn_w_up, ffn_w_down, ln_gain, ln_bias, loss_target, m_meta_tokens, m_s5_lambda_re, m_s5_lambda_im, m_s5_log_dt, m_s5_b_re, m_s5_b_im, m_s5_c_re, m_s5_c_im, m_s5_d, m_s5_w_glu, m_s5_w_out, m_attn_w_qkv, m_attn_q_gain, m_attn_k_gain, m_attn_w_out, m_ffn_w_gate, m_ffn_w_up, m_ffn_w_down, m_ln_gain, m_ln_bias, v_meta_tokens, v_s5_lambda_re, v_s5_lambda_im, v_s5_log_dt, v_s5_b_re, v_s5_b_im, v_s5_c_re, v_s5_c_im, v_s5_d, v_s5_w_glu, v_s5_w_out, v_attn_w_qkv, v_attn_q_gain, v_attn_k_gain, v_attn_w_out, v_ffn_w_gate, v_ffn_w_up, v_ffn_w_down, v_ln_gain, v_ln_bias):
    w = dict(zip(WEIGHTS, (meta_tokens, s5_lambda_re, s5_lambda_im, s5_log_dt, s5_b_re, s5_b_im, s5_c_re, s5_c_im, s5_d, s5_w_glu, s5_w_out, attn_w_qkv, attn_q_gain, attn_k_gain, attn_w_out, ffn_w_gate, ffn_w_up, ffn_w_down, ln_gain, ln_bias)))
    mom = dict(zip(WEIGHTS, (m_meta_tokens, m_s5_lambda_re, m_s5_lambda_im, m_s5_log_dt, m_s5_b_re, m_s5_b_im, m_s5_c_re, m_s5_c_im, m_s5_d, m_s5_w_glu, m_s5_w_out, m_attn_w_qkv, m_attn_q_gain, m_attn_k_gain, m_attn_w_out, m_ffn_w_gate, m_ffn_w_up, m_ffn_w_down, m_ln_gain, m_ln_bias)))
    vel = dict(zip(WEIGHTS, (v_meta_tokens, v_s5_lambda_re, v_s5_lambda_im, v_s5_log_dt, v_s5_b_re, v_s5_b_im, v_s5_c_re, v_s5_c_im, v_s5_d, v_s5_w_glu, v_s5_w_out, v_attn_w_qkv, v_attn_q_gain, v_attn_k_gain, v_attn_w_out, v_ffn_w_gate, v_ffn_w_up, v_ffn_w_down, v_ln_gain, v_ln_bias)))
    cc = lax.axis_index("c")
    dev = 4 * lax.axis_index("x") + 2 * lax.axis_index("y") + cc

    mat_rows = jnp.concatenate([_mat_rows(w[n], t, blk) for n, t, blk, _ in MATS]).astype(bf16)
    g_mats = _all_gather("ag_weights", mat_rows)
    g_vecs = _all_gather("ag_vectors", jnp.concatenate([w[n].reshape(-1, 128) for n in VECS]))
    fw, off = {}, 0
    for n, _, blk, _ in MATS:
        rows = w[n].shape[0] * blk
        fw[n] = _mat_full(g_mats[:, off:off + rows], blk)
        off += rows
    off = 0
    for n in VECS:
        rows = w[n].size // 128
        fw[n] = _vec_full(g_vecs[:, off:off + rows]).reshape(w[n].shape[:-1] + (D,))
        off += rows

    loss, grad_x, full = _local_step(x[0], loss_target[0], w, fw)
    loss = lax.psum(loss, AXES)
    grad_x = grad_x[None]

    small_names = REPL + VECS
    mine, theirs = _grad_slots(full, _pack_rows([full[k] for k in small_names], REP_ROWS), cc)
    red, red_small = _reduce_scatter(mine, theirs)
    small_all = _all_gather("ag_small_grads", red_small).reshape(REP_ROWS * D)
    g, off = {}, 0
    for n, t, blk, real in MATS:
        rows = w[n].shape[0] * blk
        g[n] = _mat_block(red[off:off + rows], t, blk, real)
        off += rows
    small = dict(zip(small_names, _unpack(small_all, [full[k].shape for k in small_names])))
    for k in REPL:
        g[k] = small[k]
    for k in VECS:
        g[k] = lax.dynamic_slice_in_dim(small[k], dev * 128, 128, axis=small[k].ndim - 1)

    delta, new_m, new_v = {}, {}, {}
    for n in WEIGHTS:
        shp = w[n].shape
        res = _adamw(*[d[n].reshape(-1, shp[-1]) for d in (w, g, mom, vel)])
        delta[n], new_m[n], new_v[n] = [a.reshape(shp) for a in res]
    return (loss, grad_x, *[g[k] for k in WEIGHTS], *[delta[k] for k in WEIGHTS],
            *[new_m[k] for k in WEIGHTS], *[new_v[k] for k in WEIGHTS])
```
